```python
import jax, jax.numpy as jnp
from jax import lax
import numpy as np

D_MODEL = 1024
BATCH = 8
SEQ = 2048
DEPTH = 1

MIX_WIDTH = D_MODEL
MLSTM_HEADS = 4
MLSTM_WIDTH = MIX_WIDTH // 2
MLSTM_HEAD_DIM = MLSTM_WIDTH // MLSTM_HEADS
MLSTM_CHUNK = 64
CONV_WIDTH = 4
GLA_HEADS = 4
GLA_WIDTH = MIX_WIDTH - MLSTM_WIDTH
GLA_DV = GLA_WIDTH // GLA_HEADS
GLA_DK = GLA_DV // 2
GLA_GATE_RANK = 16
GLA_TAU = 16.0
GLA_CHUNK = 64
IN_SPLITS = (
    MLSTM_WIDTH, MLSTM_WIDTH, MLSTM_WIDTH, MLSTM_WIDTH,
    MLSTM_HEADS, MLSTM_HEADS,
    GLA_HEADS * GLA_DK, GLA_HEADS * GLA_DK,
    GLA_WIDTH, GLA_WIDTH,
    GLA_GATE_RANK,
)
IN_WIDTH = sum(IN_SPLITS)
N_GROUPS = 4
EXPERTS_PER_GROUP = 8
N_EXPERTS = N_GROUPS * EXPERTS_PER_GROUP
TOP_K_IN_GROUP = 2
D_EXPERT = D_MODEL // 2
DEEPNORM_ALPHA = (2 * DEPTH) ** 0.25
DEEPNORM_BETA = (8 * DEPTH) ** -0.25
LN_EPS = 1e-5

kernel_name = 'hybrid_mlstm_gla_hmoe_deepnorm_adaln'


def layer_norm(x, g, b):
    xf = x.astype(jnp.float32)
    mu = jnp.mean(xf, -1, keepdims=True)
    var = jnp.mean(jnp.square(xf - mu), -1, keepdims=True)
    return ((xf - mu) * lax.rsqrt(var + LN_EPS)).astype(x.dtype) * g + b


def head_rms_norm(h, g):
    B, S, H, d = h.shape
    hn = h * lax.rsqrt(jnp.mean(jnp.square(h), -1, keepdims=True) + LN_EPS)
    return hn.reshape(B, S, H * d).astype(g.dtype) * g


def causal_depthwise_conv(u, w, b):
    ch = u.shape[-1]
    out = lax.conv_general_dilated(
        u, w[:, None, :], window_strides=(1,), padding=[(w.shape[0] - 1, 0)],
        dimension_numbers=('NWC', 'WIO', 'NWC'), feature_group_count=ch)
    return out + b


def to_chunks(t, L):
    B, S = t.shape[:2]
    t = t.reshape((B, S // L, L) + t.shape[2:])
    perm = (1, 0, 3, 2) + tuple(range(4, t.ndim))
    return t.transpose(perm)


def from_chunks(h):
    NC, B, H, L, d = h.shape
    return h.transpose(1, 0, 3, 2, 4).reshape(B, NC * L, H, d)


def mlstm_chunkwise(q, k, v, ig, lf):
    B, S, H, d = q.shape
    L = MLSTM_CHUNK
    xs = tuple(to_chunks(t, L) for t in (q, k, v, ig, lf))
    mask = jnp.tril(jnp.ones((L, L), dtype=bool))
    init = (jnp.zeros((B, H, d, d), jnp.float32),
            jnp.zeros((B, H, d), jnp.float32),
            jnp.zeros((B, H), jnp.float32))

    def step(carry, inp):
        C, n, m = carry
        qc, kc, vc, ic, fc = inp
        b = jnp.cumsum(fc, axis=-1)
        dmat = jnp.where(mask, b[..., :, None] - b[..., None, :] + ic[..., None, :], -jnp.inf)
        m_inter = b + m[..., None]
        m_t = jnp.maximum(jnp.max(dmat, -1), m_inter)
        scores = jnp.einsum('bhtd,bhsd->bhts', qc, kc) * jnp.exp(dmat - m_t[..., None])
        inter = jnp.exp(m_inter - m_t)
        num = (jnp.einsum('bhts,bhsv->bhtv', scores, vc)
               + inter[..., None] * jnp.einsum('bhtk,bhkv->bhtv', qc, C))
        den = jnp.sum(scores, -1) + inter * jnp.einsum('bhtk,bhk->bht', qc, n)
        h = num / jnp.maximum(jnp.abs(den), jnp.exp(-m_t))[..., None]
        bL = b[..., -1]
        w_log = bL[..., None] - b + ic
        m_new = jnp.maximum(bL + m, jnp.max(w_log, -1))
        decay = jnp.exp(bL + m - m_new)
        ws = jnp.exp(w_log - m_new[..., None])
        C_new = decay[..., None, None] * C + jnp.einsum('bhs,bhsk,bhsv->bhkv', ws, kc, vc)
        n_new = decay[..., None] * n + jnp.einsum('bhs,bhsk->bhk', ws, kc)
        return (C_new, n_new, m_new), h

    _, h = lax.scan(step, init, xs)
    return from_chunks(h)


def gla_chunked(q, k, v, log_a):
    B, S, H, dk = q.shape
    dv = v.shape[-1]
    L = GLA_CHUNK
    xs = tuple(to_chunks(t, L) for t in (q, k, v, log_a))
    mask = jnp.tril(jnp.ones((L, L), dtype=bool))
    init = jnp.zeros((B, H, dk, dv), jnp.float32)

    def step(state, inp):
        qc, kc, vc, ac = inp
        cum = jnp.cumsum(ac, axis=2)
        rel = cum[:, :, :, None, :] - cum[:, :, None, :, :]
        decay = jnp.exp(jnp.where(mask[:, :, None], rel, -jnp.inf))
        attn = jnp.einsum('bhtc,bhsc,bhtsc->bhts', qc, kc, decay)
        o = (jnp.einsum('bhts,bhsv->bhtv', attn, vc)
             + jnp.einsum('bhtc,bhcv->bhtv', qc * jnp.exp(cum), state))
        cL = cum[:, :, -1, :]
        state_new = (jnp.exp(cL)[..., None] * state
                     + jnp.einsum('bhsc,bhsv->bhcv', kc * jnp.exp(cL[:, :, None, :] - cum), vc))
        return state_new, o

    _, o = lax.scan(step, init, xs)
    return from_chunks(o)


def hybrid_mixer(u, w_in, w_conv, b_conv, b_igate, b_fgate, mlstm_norm_g,
                 w_gla_a, b_gla_a, gla_norm_g, w_out):
    B, S, _ = u.shape
    f32 = jnp.float32
    proj = u @ w_in
    points = [int(p) for p in np.cumsum(IN_SPLITS)[:-1]]
    mq, mk, mv, mo, mi, mf, gq, gk, gv, gg, ga = jnp.split(proj, points, axis=-1)

    def heads(t, h):
        return t.reshape(B, S, h, -1).astype(f32)

    qk = jax.nn.silu(causal_depthwise_conv(jnp.concatenate([mq, mk], -1), w_conv, b_conv))
    mq, mk = jnp.split(qk, 2, axis=-1)
    h_m = mlstm_chunkwise(
        heads(mq, MLSTM_HEADS),
        heads(mk, MLSTM_HEADS) * (MLSTM_HEAD_DIM ** -0.5),
        heads(mv, MLSTM_HEADS),
        (mi + b_igate).astype(f32),
        jax.nn.log_sigmoid((mf + b_fgate).astype(f32)))
    h_m = jax.nn.sigmoid(heads(mo, MLSTM_HEADS)) * h_m
    h_m = head_rms_norm(h_m, mlstm_norm_g)

    log_a = jax.nn.log_sigmoid((ga @ w_gla_a + b_gla_a).astype(f32)) / GLA_TAU
    h_g = gla_chunked(
        heads(gq, GLA_HEADS) * (GLA_DK ** -0.5),
        heads(gk, GLA_HEADS),
        heads(gv, GLA_HEADS),
        log_a.reshape(B, S, GLA_HEADS, GLA_DK))
    h_g = head_rms_norm(h_g, gla_norm_g) * jax.nn.silu(gg)

    return jnp.concatenate([h_m, h_g], axis=-1) @ w_out


def hierarchical_moe(u, w_rg, b_rg, w_re, b_re, w_gate, w_up, w_down):
    B, S, D = u.shape
    N = B * S
    hf = u.reshape(N, D)
    g_logits = (hf @ w_rg + b_rg).astype(jnp.float32)
    g_sel = jnp.argmax(g_logits, -1)
    p_g = jnp.take_along_axis(jax.nn.softmax(g_logits, -1), g_sel[:, None], 1)[:, 0]
    e_logits = (hf @ w_re + b_re).astype(jnp.float32).reshape(N, N_GROUPS, EXPERTS_PER_GROUP)
    e_in = jnp.take_along_axis(e_logits, g_sel[:, None, None], 1)[:, 0]
    top_v, top_i = lax.top_k(e_in, TOP_K_IN_GROUP)
    p_e = jax.nn.softmax(top_v, -1)
    expert_id = (g_sel[:, None] * EXPERTS_PER_GROUP + top_i).reshape(-1)
    weight = (p_g[:, None] * p_e).reshape(-1)
    order = jnp.argsort(expert_id)
    tok = order // TOP_K_IN_GROUP
    sizes = jnp.bincount(expert_id, length=N_EXPERTS).astype(jnp.int32)
    xs = hf[tok]
    hid = (jax.nn.silu(lax.ragged_dot(xs, w_gate, sizes))
           * lax.ragged_dot(xs, w_up, sizes))
    ys = lax.ragged_dot(hid, w_down, sizes) * weight[order][:, None].astype(hf.dtype)
    out = jnp.zeros((N, D), hf.dtype).at[tok].add(ys)
    return out.reshape(B, S, D)


def setup_inputs(seed: int = 0) -> dict:
    key = jax.random.key(seed)
    ks = jax.random.split(key, 26)
    f32 = jnp.float32
    D = D_MODEL
    Lr = DEPTH

    def nrm(k, shape, s):
        return jax.random.normal(k, shape, f32) * s

    col_scale = np.concatenate([
        np.ones(2 * MLSTM_WIDTH), np.full(MLSTM_WIDTH, DEEPNORM_BETA),
        np.ones(MLSTM_WIDTH + 2 * MLSTM_HEADS + 2 * GLA_HEADS * GLA_DK),
        np.full(GLA_WIDTH, DEEPNORM_BETA), np.ones(GLA_WIDTH + GLA_GATE_RANK)]).astype(np.float32)
    return {
        'x': nrm(ks[0], (BATCH, SEQ, D), 1.0),
        'c': nrm(ks[1], (BATCH, D), 1.0),
        'w_ada': nrm(ks[2], (Lr, D, 6 * D), 0.1 * D ** -0.5),
        'b_ada': nrm(ks[3], (Lr, 6 * D), 0.02),
        'w_in': nrm(ks[4], (Lr, D, IN_WIDTH), D ** -0.5) * jnp.asarray(col_scale),
        'w_conv': nrm(ks[5], (Lr, CONV_WIDTH, 2 * MLSTM_WIDTH), CONV_WIDTH ** -0.5),
        'b_conv': nrm(ks[6], (Lr, 2 * MLSTM_WIDTH), 0.02),
        'b_igate': nrm(ks[7], (Lr, MLSTM_HEADS), 0.1),
        'b_fgate': jnp.linspace(3.0, 6.0, MLSTM_HEADS, dtype=f32)[None] + nrm(ks[8], (Lr, MLSTM_HEADS), 0.1),
        'mlstm_norm_g': 1.0 + nrm(ks[9], (Lr, MLSTM_WIDTH), 0.02),
        'w_gla_a': nrm(ks[10], (Lr, GLA_GATE_RANK, GLA_HEADS * GLA_DK), GLA_GATE_RANK ** -0.5),
        'b_gla_a': nrm(ks[11], (Lr, GLA_HEADS * GLA_DK), 0.1),
        'gla_norm_g': 1.0 + nrm(ks[12], (Lr, GLA_WIDTH), 0.02),
        'w_out': nrm(ks[13], (Lr, MIX_WIDTH, D), MIX_WIDTH ** -0.5 * DEEPNORM_BETA),
        'ln1_g': 1.0 + nrm(ks[14], (Lr, D), 0.02),
        'ln1_b': nrm(ks[15], (Lr, D), 0.02),
        'w_route_group': nrm(ks[16], (Lr, D, N_GROUPS), D ** -0.5),
        'b_route_group': nrm(ks[17], (Lr, N_GROUPS), 0.01),
        'w_route_expert': nrm(ks[18], (Lr, D, N_EXPERTS), D ** -0.5),
        'b_route_expert': nrm(ks[19], (Lr, N_EXPERTS), 0.01),
        'w_gate': nrm(ks[20], (Lr, N_EXPERTS, D, D_EXPERT), D ** -0.5),
        'w_up': nrm(ks[21], (Lr, N_EXPERTS, D, D_EXPERT), D ** -0.5),
        'w_down': nrm(ks[22], (Lr, N_EXPERTS, D_EXPERT, D), D_EXPERT ** -0.5 * DEEPNORM_BETA),
        'ln2_g': 1.0 + nrm(ks[23], (Lr, D), 0.02),
        'ln2_b': nrm(ks[24], (Lr, D), 0.02),
    }


def reference(x, c, w_ada, b_ada, w_in, w_conv, b_conv, b_igate, b_fgate, mlstm_norm_g,
              w_gla_a, b_gla_a, gla_norm_g, w_out, ln1_g, ln1_b,
              w_route_group, b_route_group, w_route_expert, b_route_expert,
              w_gate, w_up, w_down, ln2_g, ln2_b):
    c_act = jax.nn.silu(c)
    for l in range(DEPTH):
        mod = c_act @ w_ada[l] + b_ada[l]
        sh1, sc1, g1, sh2, sc2, g2 = [m[:, None, :] for m in jnp.split(mod, 6, axis=-1)]
        u = x * (1.0 + sc1) + sh1
        y = hybrid_mixer(u, w_in[l], w_conv[l], b_conv[l], b_igate[l], b_fgate[l],
                         mlstm_norm_g[l], w_gla_a[l], b_gla_a[l], gla_norm_g[l], w_out[l])
        x = layer_norm(DEEPNORM_ALPHA * x + (1.0 + g1) * y, ln1_g[l], ln1_b[l])
        u = x * (1.0 + sc2) + sh2
        y = hierarchical_moe(u, w_route_group[l], b_route_group[l], w_route_expert[l],
                             b_route_expert[l], w_gate[l], w_up[l], w_down[l])
        x = layer_norm(DEEPNORM_ALPHA * x + (1.0 + g2) * y, ln2_g[l], ln2_b[l])
    return x
```

```python
import functools

import numpy as np
import jax
import jax.numpy as jnp
from jax import lax
from jax.experimental import pallas as pl
from jax.experimental.pallas import tpu as pltpu

F32 = jnp.float32
BF16 = jnp.bfloat16
HIGHEST = lax.Precision.HIGHEST

DEPTH = 1
M_HEADS = 4
M_HD = 128
M_W = M_HEADS * M_HD
CONV_W = 4
G_HEADS = 4
G_DK = 64
G_DV = 128
G_W = G_HEADS * G_DV
G_KW = G_HEADS * G_DK
G_RANK = 16
G_TAU = 16.0
G_CHUNK = 64
N_GROUPS = 4
E_PER_G = 8
N_EXP = N_GROUPS * E_PER_G
D_EXP = 512
ALPHA = (2 * DEPTH) ** 0.25
LN_EPS = 1e-5

LANES = 128
SUBLANES = 8
VMEM_LIMIT = 48 * 1024 * 1024

C_QK = 0
C_VO = 1024
C_GQK = 2048
C_GV = 2560
C_GG = 3072
C_SMALL = 3584
C_TOT = 3712
SM_I, SM_F, SM_A = 0, 8, 16

FFN_TM = 256


def _cparams(n_axes=1):
    return pltpu.CompilerParams(dimension_semantics=("arbitrary",) * n_axes,
                                vmem_limit_bytes=VMEM_LIMIT)


def _sigmoid(x):
    return 1.0 / (1.0 + jnp.exp(-x))


def _log_sigmoid(x):
    return jnp.minimum(x, 0.0) - jnp.log(1.0 + jnp.exp(-jnp.abs(x)))


def _ada_kernel(c_ref, w_ref, b_ref, o_ref):
    c = c_ref[...]
    ca = c * _sigmoid(c)
    o_ref[...] = jnp.dot(ca, w_ref[...], preferred_element_type=F32, precision=HIGHEST) + b_ref[...]


def _ada(c, w, b):
    B, D = c.shape
    n_out = w.shape[1]
    tn = 1024
    return pl.pallas_call(
        _ada_kernel,
        grid=(n_out // tn,),
        in_specs=[pl.BlockSpec((B, D), lambda j: (0, 0)),
                  pl.BlockSpec((D, tn), lambda j: (0, j)),
                  pl.BlockSpec((1, tn), lambda j: (0, j))],
        out_specs=pl.BlockSpec((B, tn), lambda j: (0, j)),
        out_shape=jax.ShapeDtypeStruct((B, n_out), F32),
        compiler_params=_cparams(),
        name="ada",
    )(c, w, b.reshape(1, n_out))


def _inproj_kernel(x_ref, mod_ref, w_ref, wc_ref, bc_ref, wa_ref, ba_ref, bg_ref,
                   oa_ref, la_ref, g_ref, halo_ref, *, tm, tpb, lm):
    i = pl.program_id(0)

    @pl.when(i % tpb == 0)
    def _():
        halo_ref[...] = jnp.zeros_like(halo_ref)

    mod = mod_ref[0]
    u = (x_ref[...] * (1.0 + mod[1:2, :]) + mod[0:1, :]).astype(BF16)

    p = jnp.dot(u, w_ref[:, C_QK:C_QK + 2 * M_W], preferred_element_type=F32)
    ext = jnp.concatenate([halo_ref[...], p], axis=0)
    acc = bc_ref[...] + wc_ref[CONV_W - 1:CONV_W, :] * p
    for j in range(CONV_W - 1):
        sh = pltpu.roll(ext, CONV_W - 1 - j, 0)[SUBLANES:, :]
        acc = acc + wc_ref[j:j + 1, :] * sh
    halo_ref[...] = p[tm - SUBLANES:, :]
    qk = acc * _sigmoid(acc)
    oa_ref[:, C_QK:C_QK + M_W] = qk[:, :M_W].astype(BF16)
    oa_ref[:, C_QK + M_W:C_QK + 2 * M_W] = (qk[:, M_W:] * (M_HD ** -0.5)).astype(BF16)

    p = jnp.dot(u, w_ref[:, C_VO:C_VO + 2 * M_W], preferred_element_type=F32)
    oa_ref[:, C_VO:C_VO + 2 * M_W] = p.astype(BF16)

    p = jnp.dot(u, w_ref[:, C_GQK:C_GQK + G_KW], preferred_element_type=F32)
    oa_ref[:, C_GQK:C_GQK + G_KW] = (p * (G_DK ** -0.5)).astype(BF16)
    p = jnp.dot(u, w_ref[:, C_GQK + G_KW:C_SMALL], preferred_element_type=F32)
    oa_ref[:, C_GQK + G_KW:C_SMALL] = p.astype(BF16)

    ps = jnp.dot(u, w_ref[:, C_SMALL:C_TOT], preferred_element_type=F32)
    la = jnp.dot(ps.astype(BF16), wa_ref[...], preferred_element_type=F32) + ba_ref[...]
    la_ref[...] = _log_sigmoid(la) * (1.0 / G_TAU)
    pt = ps.T
    gi = pt[SM_I:SM_I + SUBLANES, :] + bg_ref[0:SUBLANES, :]
    gf = _log_sigmoid(pt[SM_F:SM_F + SUBLANES, :] + bg_ref[SUBLANES:2 * SUBLANES, :])
    for j in range(tm // lm):
        g_ref[j, 0:SUBLANES, :] = gi[:, j * lm:(j + 1) * lm]
        g_ref[j, SUBLANES:2 * SUBLANES, :] = gf[:, j * lm:(j + 1) * lm]


def _inproj(x2, mod3, w_p, w_conv, b_conv, wa_pad, b_gla, bg, *, S, tm, lm):
    N, D = x2.shape
    tpb = S // tm
    kern = functools.partial(_inproj_kernel, tm=tm, tpb=tpb, lm=lm)
    return pl.pallas_call(
        kern,
        grid=(N // tm,),
        in_specs=[pl.BlockSpec((tm, D), lambda i: (i, 0)),
                  pl.BlockSpec((1, 6, D), lambda i: (i // tpb, 0, 0)),
                  pl.BlockSpec((D, C_TOT), lambda i: (0, 0)),
                  pl.BlockSpec((CONV_W, 2 * M_W), lambda i: (0, 0)),
                  pl.BlockSpec((1, 2 * M_W), lambda i: (0, 0)),
                  pl.BlockSpec((LANES, G_KW), lambda i: (0, 0)),
                  pl.BlockSpec((1, G_KW), lambda i: (0, 0)),
                  pl.BlockSpec((2 * SUBLANES, 1), lambda i: (0, 0))],
        out_specs=[pl.BlockSpec((tm, C_SMALL), lambda i: (i, 0)),
                   pl.BlockSpec((tm, G_KW), lambda i: (i, 0)),
                   pl.BlockSpec((tm // lm, 2 * SUBLANES, lm), lambda i: (i, 0, 0))],
        out_shape=[jax.ShapeDtypeStruct((N, C_SMALL), BF16),
                   jax.ShapeDtypeStruct((N, G_KW), F32),
                   jax.ShapeDtypeStruct((N // lm, 2 * SUBLANES, lm), F32)],
        scratch_shapes=[pltpu.VMEM((SUBLANES, 2 * M_W), F32)],
        compiler_params=_cparams(),
        name="inproj",
    )(x2, mod3, w_p, w_conv, b_conv, wa_pad, b_gla, bg)


def _mlstm_kernel(qk_ref, vo_ref, g_ref, u_ref, gain_ref, out_ref, c_ref, *, L, NC):
    c_ref[...] = jnp.zeros_like(c_ref)
    lane = lax.broadcasted_iota(jnp.int32, (SUBLANES, L), 1)
    tril = (lax.broadcasted_iota(jnp.int32, (L, L), 0) >= lax.broadcasted_iota(jnp.int32, (L, L), 1))
    ones_v = jnp.ones((L, M_HD), BF16)
    zpad = jnp.zeros((LANES - 5 * SUBLANES, L), F32)

    def chunk(c, m_prev):
        r0 = pl.multiple_of(c * L, L)
        gi = g_ref[c, 0:SUBLANES, :]
        gf = g_ref[c, SUBLANES:2 * SUBLANES, :]
        b = jnp.dot(gf, u_ref[...], preferred_element_type=F32, precision=HIGHEST)
        a = gi - b
        G = a
        s = 1
        while s < L:
            sh = pltpu.roll(G, s, 1)
            G = jnp.maximum(G, jnp.where(lane >= s, sh, -jnp.inf))
            s *= 2
        M = jnp.maximum(G, m_prev)
        mt = b + M
        ML = M[:, L - 1:L]
        m_new = b[:, L - 1:L] + ML
        Z = jnp.concatenate([M, mt, a, jnp.broadcast_to(m_prev, (SUBLANES, L)),
                             jnp.broadcast_to(ML, (SUBLANES, L)), zpad], axis=0)
        Zt = Z.T
        dec = jnp.exp(m_prev - ML)
        for h in range(M_HEADS):
            M_col = Zt[:, h:h + 1]
            mt_col = Zt[:, SUBLANES + h:SUBLANES + h + 1]
            a_col = Zt[:, 2 * SUBLANES + h:2 * SUBLANES + h + 1]
            mp_col = Zt[:, 3 * SUBLANES + h:3 * SUBLANES + h + 1]
            ML_col = Zt[:, 4 * SUBLANES + h:4 * SUBLANES + h + 1]
            hs = slice(h * M_HD, (h + 1) * M_HD)
            hs2 = slice(M_W + h * M_HD, M_W + (h + 1) * M_HD)
            q = qk_ref[pl.ds(r0, L), hs]
            k = qk_ref[pl.ds(r0, L), hs2]
            v = vo_ref[pl.ds(r0, L), hs]
            og = vo_ref[pl.ds(r0, L), hs2]
            dm = jnp.exp(jnp.where(tril, a[h:h + 1, :] - M_col, -jnp.inf))
            sc = lax.dot_general(q, k, (((1,), (1,)), ((), ())), preferred_element_type=F32)
            pm = (sc * dm).astype(BF16)
            vext = jnp.concatenate([v, ones_v], axis=1)
            cst = c_ref[h]
            nd = (jnp.dot(pm, vext, preferred_element_type=F32)
                  + jnp.exp(mp_col - M_col) * jnp.dot(q, cst.astype(BF16), preferred_element_type=F32))
            hh = nd[:, :M_HD] / jnp.maximum(jnp.abs(nd[:, M_HD:]), jnp.exp(-mt_col))
            hh = _sigmoid(og.astype(F32)) * hh
            hn = hh * lax.rsqrt(jnp.mean(hh * hh, axis=-1, keepdims=True) + LN_EPS)
            out_ref[pl.ds(r0, L), hs] = (hn * gain_ref[:, hs]).astype(BF16)
            kw = (jnp.exp(a_col - ML_col) * k.astype(F32)).astype(BF16)
            upd = lax.dot_general(kw, vext, (((0,), (0,)), ((), ())), preferred_element_type=F32)
            c_ref[h] = dec[h:h + 1, :] * cst + upd
        return m_new

    lax.fori_loop(0, NC, chunk, jnp.zeros((SUBLANES, 1), F32))


def _mlstm(oa, g3, u_tri, gain, *, B, S, L):
    N = oa.shape[0]
    NC = S // L
    kern = functools.partial(_mlstm_kernel, L=L, NC=NC)
    return pl.pallas_call(
        kern,
        grid=(B,),
        in_specs=[pl.BlockSpec((S, 2 * M_W), lambda b: (b, C_QK // (2 * M_W))),
                  pl.BlockSpec((S, 2 * M_W), lambda b: (b, C_VO // (2 * M_W))),
                  pl.BlockSpec((NC, 2 * SUBLANES, L), lambda b: (b, 0, 0)),
                  pl.BlockSpec((L, L), lambda b: (0, 0)),
                  pl.BlockSpec((1, M_W), lambda b: (0, 0))],
        out_specs=pl.BlockSpec((S, M_W), lambda b: (b, 0)),
        out_shape=jax.ShapeDtypeStruct((N, M_W), BF16),
        scratch_shapes=[pltpu.VMEM((M_HEADS, M_HD, 2 * M_HD), F32)],
        compiler_params=_cparams(),
        name="mlstm",
    )(oa, oa, g3, u_tri, gain)


_G_LEVELS = 6
_G_XROW = 2 * G_CHUNK + SUBLANES


def _gla_consts():
    L = G_CHUNK
    t = np.arange(L)
    blocks = [(t[None, :] <= t[:, None]).astype(np.float32),
              (t[None, :] > t[:, None]).astype(np.float32),
              np.ones((SUBLANES, L), np.float32)]
    masks = [np.eye(L, dtype=np.float32)]
    m = 1
    while m < L:
        wl = np.zeros((L, L), np.float32)
        for r in range(L):
            r0 = (r // (2 * m)) * 2 * m + m
            if r % (2 * m) >= m:
                wl[r, r0:r + 1] = 1.0
            else:
                wl[r, r + 1:r0] = 1.0
        blocks.append(wl)
        tt, ss = t[:, None], t[None, :]
        masks.append(((tt // (2 * m) == ss // (2 * m)) & (tt % (2 * m) >= m)
                      & (ss % (2 * m) < m)).astype(np.float32))
        m *= 2
    w = np.concatenate(blocks, axis=0)
    w3 = np.concatenate([w, w, w], axis=1)
    mk = np.stack([np.concatenate([x, x], axis=0) for x in masks])
    return w3, mk


def _gla_kernel(qk_ref, v_ref, gg_ref, la_ref, w3_ref, mk_ref, gain_ref, out_ref, st_ref, *, NC):
    L = G_CHUNK
    st_ref[...] = jnp.zeros_like(st_ref)
    lane_lo = lax.broadcasted_iota(jnp.int32, (L, LANES), 1) < G_DK
    br = lax.broadcasted_iota(jnp.int32, (2 * G_DV, LANES), 0) < G_DV
    bl = lax.broadcasted_iota(jnp.int32, (2 * G_DV, LANES), 1) < G_DK
    bmask = br == bl

    def chunk(c, carry):
        r0 = pl.multiple_of(c * L, L)
        la = la_ref[pl.ds(r0, L), :]
        hi = la.astype(BF16)
        r1 = la - hi.astype(F32)
        mid = r1.astype(BF16)
        lo = (r1 - mid.astype(F32)).astype(BF16)
        stk = jnp.concatenate([hi, mid, lo], axis=0)
        X = jnp.exp(jnp.dot(w3_ref[...], stk, preferred_element_type=F32))
        q = qk_ref[pl.ds(r0, L), 0:G_KW].astype(F32)
        k = qk_ref[pl.ds(r0, L), G_KW:2 * G_KW].astype(F32)
        outs = []
        for p in range(2):
            ls = slice(LANES * p, LANES * (p + 1))
            qp, kp = q[:, ls], k[:, ls]
            A = jnp.zeros((2 * L, L), F32)
            for lev in range(_G_LEVELS + 1):
                if lev == 0:
                    qt, kt = qp, kp
                else:
                    xl = X[_G_XROW + L * (lev - 1):_G_XROW + L * lev, ls]
                    qt, kt = qp * xl, kp * xl
                qs = jnp.concatenate([jnp.where(lane_lo, qt, 0.0), jnp.where(lane_lo, 0.0, qt)],
                                     axis=0).astype(BF16)
                sc = lax.dot_general(qs, kt.astype(BF16), (((1,), (1,)), ((), ())),
                                     preferred_element_type=F32)
                A = A + sc * mk_ref[lev]
            Ab = A.astype(BF16)
            vp = v_ref[pl.ds(r0, L), 2 * G_DV * p:2 * G_DV * (p + 1)]
            oi0 = jnp.dot(Ab[0:L], vp[:, 0:G_DV], preferred_element_type=F32)
            oi1 = jnp.dot(Ab[L:2 * L], vp[:, G_DV:2 * G_DV], preferred_element_type=F32)
            st = st_ref[p]
            qc = (qp * X[0:L, ls]).astype(BF16)
            o_inter = lax.dot_general(qc, st.astype(BF16), (((1,), (1,)), ((), ())),
                                      preferred_element_type=F32)
            outs.append(o_inter + jnp.concatenate([oi0, oi1], axis=1))
            kc = (kp * X[L:2 * L, ls]).astype(BF16)
            upd = lax.dot_general(vp, kc, (((0,), (0,)), ((), ())), preferred_element_type=F32)
            dec = X[2 * L:2 * L + 1, ls]
            st_ref[p] = jnp.where(bmask, dec * st + upd, 0.0)
        gg = gg_ref[pl.ds(r0, L), :].astype(F32)
        gate = gg * _sigmoid(gg)
        for p in range(2):
            for hh in range(2):
                h = 2 * p + hh
                o = outs[p][:, G_DV * hh:G_DV * (hh + 1)]
                hn = o * lax.rsqrt(jnp.mean(o * o, axis=-1, keepdims=True) + LN_EPS)
                hs = slice(G_DV * h, G_DV * (h + 1))
                out_ref[pl.ds(r0, L), hs] = (hn * gain_ref[:, hs] * gate[:, hs]).astype(BF16)
        return carry

    lax.fori_loop(0, NC, chunk, 0)


def _gla(oa, la, w3, mk, gain, *, B, S):
    N = oa.shape[0]
    NC = S // G_CHUNK
    kern = functools.partial(_gla_kernel, NC=NC)
    return pl.pallas_call(
        kern,
        grid=(B,),
        in_specs=[pl.BlockSpec((S, 2 * G_KW), lambda b: (b, C_GQK // (2 * G_KW))),
                  pl.BlockSpec((S, G_W), lambda b: (b, C_GV // G_W)),
                  pl.BlockSpec((S, G_W), lambda b: (b, C_GG // G_W)),
                  pl.BlockSpec((S, G_KW), lambda b: (b, 0)),
                  pl.BlockSpec(w3.shape, lambda b: (0, 0)),
                  pl.BlockSpec(mk.shape, lambda b: (0, 0, 0)),
                  pl.BlockSpec((1, G_W), lambda b: (0, 0))],
        out_specs=pl.BlockSpec((S, G_W), lambda b: (b, 0)),
        out_shape=jax.ShapeDtypeStruct((N, G_W), BF16),
        scratch_shapes=[pltpu.VMEM((2, 2 * G_DV, LANES), F32)],
        compiler_params=_cparams(),
        name="gla",
    )(oa, oa, oa, la, w3, mk, gain)


def _layer_norm(z, g, b):
    mu = jnp.mean(z, axis=-1, keepdims=True)
    zc = z - mu
    var = jnp.mean(zc * zc, axis=-1, keepdims=True)
    return zc * lax.rsqrt(var + LN_EPS) * g + b


def _outproj_kernel(hm_ref, hg_ref, w_ref, x_ref, mod_ref, g_ref, b_ref, wr_ref, br_ref,
                    x1_ref, u2_ref, rrow_ref, rcol_ref, *, tm):
    mod = mod_ref[0]
    y = (jnp.dot(hm_ref[...], w_ref[0:M_W, :], preferred_element_type=F32)
         + jnp.dot(hg_ref[...], w_ref[M_W:M_W + G_W, :], preferred_element_type=F32))
    z = ALPHA * x_ref[...] + (1.0 + mod[2:3, :]) * y
    x1 = _layer_norm(z, g_ref[...], b_ref[...])
    x1_ref[...] = x1
    u2 = x1 * (1.0 + mod[4:5, :]) + mod[3:4, :]
    u2_ref[...] = u2

    logits = jnp.dot(u2, wr_ref[...], preferred_element_type=F32, precision=HIGHEST) + br_ref[...]
    lt = logits.T
    row = lax.broadcasted_iota(jnp.int32, (SUBLANES, tm), 0)
    gl = jnp.where(row < N_GROUPS, lt[0:SUBLANES, :], -jnp.inf)
    gmax = jnp.max(gl, axis=0, keepdims=True)
    gsel = jnp.min(jnp.where(gl == gmax, row, SUBLANES), axis=0, keepdims=True)
    pg = 1.0 / jnp.sum(jnp.exp(gl - gmax), axis=0, keepdims=True)
    ein = jnp.zeros((SUBLANES, tm), F32)
    for g in range(N_GROUPS):
        ein = jnp.where(gsel == g, lt[SUBLANES * (g + 1):SUBLANES * (g + 2), :], ein)
    v1 = jnp.max(ein, axis=0, keepdims=True)
    i1 = jnp.min(jnp.where(ein == v1, row, SUBLANES), axis=0, keepdims=True)
    rest = jnp.where(row == i1, -jnp.inf, ein)
    v2 = jnp.max(rest, axis=0, keepdims=True)
    i2 = jnp.min(jnp.where(rest == v2, row, SUBLANES), axis=0, keepdims=True)
    t2 = jnp.exp(v2 - v1)
    p1 = 1.0 / (1.0 + t2)
    e0 = (gsel * E_PER_G + i1).astype(F32)
    e1 = (gsel * E_PER_G + i2).astype(F32)
    r = jnp.concatenate([e0, e1, pg * p1, pg * (t2 * p1), jnp.zeros((SUBLANES - 4, tm), F32)], axis=0)
    rrow_ref[0] = r
    rcol_ref[...] = jnp.concatenate([r, jnp.zeros((LANES - SUBLANES, tm), F32)], axis=0).T


def _outproj(hm, hg, w_out, x2, mod3, g, b, wr, br, *, S, tm):
    N, D = x2.shape
    tpb = S // tm
    kern = functools.partial(_outproj_kernel, tm=tm)
    return pl.pallas_call(
        kern,
        grid=(N // tm,),
        in_specs=[pl.BlockSpec((tm, M_W), lambda i: (i, 0)),
                  pl.BlockSpec((tm, G_W), lambda i: (i, 0)),
                  pl.BlockSpec((M_W + G_W, D), lambda i: (0, 0)),
                  pl.BlockSpec((tm, D), lambda i: (i, 0)),
                  pl.BlockSpec((1, 6, D), lambda i: (i // tpb, 0, 0)),
                  pl.BlockSpec((1, D), lambda i: (0, 0)),
                  pl.BlockSpec((1, D), lambda i: (0, 0)),
                  pl.BlockSpec((D, LANES), lambda i: (0, 0)),
                  pl.BlockSpec((1, LANES), lambda i: (0, 0))],
        out_specs=[pl.BlockSpec((tm, D), lambda i: (i, 0)),
                   pl.BlockSpec((tm, D), lambda i: (i, 0)),
                   pl.BlockSpec((1, SUBLANES, tm), lambda i: (i, 0, 0)),
                   pl.BlockSpec((tm, LANES), lambda i: (i, 0))],
        out_shape=[jax.ShapeDtypeStruct((N, D), F32),
                   jax.ShapeDtypeStruct((N, D), F32),
                   jax.ShapeDtypeStruct((N // tm, SUBLANES, tm), F32),
                   jax.ShapeDtypeStruct((N, LANES), F32)],
        compiler_params=_cparams(),
        name="outproj",
    )(hm, hg, w_out, x2, mod3, g, b, wr, br)


def _route_kernel(rr_ref, u_ref, lt_ref, pos_ref, meta_ref, rank_ref, *, NT, tb, TM):
    iota_e = lax.broadcasted_iota(jnp.int32, (N_EXP, tb), 0).astype(F32)

    def p1(j, carry):
        r = rr_ref[j]
        ranks = []
        for k in range(2):
            oh = jnp.where(iota_e == r[k:k + 1, :], 1.0, 0.0)
            cum = jnp.dot(oh.astype(BF16), u_ref[...], preferred_element_type=F32)
            ranks.append(jnp.sum(oh * (carry + cum - 1.0), axis=0, keepdims=True))
            carry = carry + jnp.sum(oh, axis=1, keepdims=True)
        rank_ref[j] = jnp.concatenate(ranks + [jnp.zeros((SUBLANES - 2, tb), F32)], axis=0)
        return carry

    counts = lax.fori_loop(0, NT, p1, jnp.zeros((N_EXP, 1), F32))
    padded = jnp.floor((counts + (TM - 1.0)) * (1.0 / TM)) * TM
    offs = jnp.dot(lt_ref[...], jnp.broadcast_to(padded, (N_EXP, LANES)),
                   preferred_element_type=F32, precision=HIGHEST)[:, 0:1]
    total = jnp.sum(padded, axis=0, keepdims=True)

    def p2(j, carry):
        r = rr_ref[j]
        rk = rank_ref[j]
        rows = []
        for k in range(2):
            oh = jnp.where(iota_e == r[k:k + 1, :], 1.0, 0.0)
            rows.append(jnp.sum(oh * offs, axis=0, keepdims=True) + rk[k:k + 1, :])
        pos_ref[j] = jnp.concatenate(rows + [jnp.zeros((SUBLANES - 2, tb), F32)], axis=0).astype(jnp.int32)
        return carry

    lax.fori_loop(0, NT, p2, 0)
    tstart = lax.broadcasted_iota(jnp.int32, (N_EXP, tb), 1).astype(F32) * TM
    te = jnp.sum(jnp.where(offs <= tstart, 1.0, 0.0), axis=0, keepdims=True) - 1.0
    nv = jnp.broadcast_to(total * (1.0 / TM), (1, tb))
    meta_ref[...] = jnp.concatenate([te, nv, jnp.zeros((SUBLANES - 2, tb), F32)], axis=0).astype(jnp.int32)


def _route(rrow, u256, ltri, *, TM):
    NT, _, tb = rrow.shape
    kern = functools.partial(_route_kernel, NT=NT, tb=tb, TM=TM)
    return pl.pallas_call(
        kern,
        grid=(1,),
        in_specs=[pl.BlockSpec((NT, SUBLANES, tb), lambda i: (0, 0, 0)),
                  pl.BlockSpec((tb, tb), lambda i: (0, 0)),
                  pl.BlockSpec((N_EXP, N_EXP), lambda i: (0, 0))],
        out_specs=[pl.BlockSpec((NT, SUBLANES, tb), lambda i: (0, 0, 0)),
                   pl.BlockSpec((SUBLANES, tb), lambda i: (0, 0))],
        out_shape=[jax.ShapeDtypeStruct((NT, SUBLANES, tb), jnp.int32),
                   jax.ShapeDtypeStruct((SUBLANES, tb), jnp.int32)],
        scratch_shapes=[pltpu.VMEM((NT, SUBLANES, tb), F32)],
        compiler_params=_cparams(),
        name="route",
    )(rrow, u256, ltri)


def _dispatch_kernel(pos_ref, u_ref, xs_in_ref, xs_ref, sem, *, tm):
    del xs_in_ref

    def row_copy(t, p):
        return pltpu.make_async_copy(u_ref.at[pl.ds(t, 1), :], xs_ref.at[pl.ds(p, 1), :], sem)

    def issue(t, carry):
        for k in range(2):
            row_copy(t, pos_ref[0, k, t]).start()
        return carry

    lax.fori_loop(0, tm, issue, 0)

    def drain(t, carry):
        for k in range(2):
            row_copy(0, 0).wait()
        return carry

    lax.fori_loop(0, tm, drain, 0)


def _dispatch(pos, u2, xs0, *, tm):
    N, D = u2.shape
    kern = functools.partial(_dispatch_kernel, tm=tm)
    return pl.pallas_call(
        kern,
        grid=(N // tm,),
        in_specs=[pl.BlockSpec((1, SUBLANES, tm), lambda i: (i, 0, 0), memory_space=pltpu.SMEM),
                  pl.BlockSpec((tm, D), lambda i: (i, 0)),
                  pl.BlockSpec(memory_space=pl.ANY)],
        out_specs=pl.BlockSpec(memory_space=pl.ANY),
        out_shape=jax.ShapeDtypeStruct(xs0.shape, xs0.dtype),
        scratch_shapes=[pltpu.SemaphoreType.DMA(())],
        input_output_aliases={2: 0},
        compiler_params=_cparams(),
        name="dispatch",
    )(pos, u2, xs0)


def _ffn_kernel(te_ref, nv_ref, xs_ref, wg_ref, wu_ref, wd_ref, o_ref, wgb, wub, wdb):
    i = pl.program_id(0)
    prev = te_ref[jnp.maximum(i - 1, 0)]

    @pl.when((i == 0) | (te_ref[i] != prev))
    def _():
        wgb[...] = wg_ref[0].astype(BF16)
        wub[...] = wu_ref[0].astype(BF16)
        wdb[...] = wd_ref[0].astype(BF16)

    @pl.when(i < nv_ref[0])
    def _():
        x = xs_ref[...].astype(BF16)
        g = jnp.dot(x, wgb[...], preferred_element_type=F32)
        u = jnp.dot(x, wub[...], preferred_element_type=F32)
        h = (g * _sigmoid(g) * u).astype(BF16)
        o_ref[...] = jnp.dot(h, wdb[...], preferred_element_type=F32)

    @pl.when(i >= nv_ref[0])
    def _():
        o_ref[...] = jnp.zeros_like(o_ref)


def _ffn(te, nv, xs, wg, wu, wd, *, TM):
    P, D = xs.shape
    n_tiles = P // TM
    grid_spec = pltpu.PrefetchScalarGridSpec(
        num_scalar_prefetch=2,
        grid=(n_tiles,),
        in_specs=[pl.BlockSpec((TM, D), lambda i, te, nv: (i, 0)),
                  pl.BlockSpec((1, D, D_EXP), lambda i, te, nv: (te[i], 0, 0)),
                  pl.BlockSpec((1, D, D_EXP), lambda i, te, nv: (te[i], 0, 0)),
                  pl.BlockSpec((1, D_EXP, D), lambda i, te, nv: (te[i], 0, 0))],
        out_specs=pl.BlockSpec((TM, D), lambda i, te, nv: (i, 0)),
        scratch_shapes=[pltpu.VMEM((D, D_EXP), BF16), pltpu.VMEM((D, D_EXP), BF16),
                        pltpu.VMEM((D_EXP, D), BF16)],
    )
    return pl.pallas_call(
        _ffn_kernel,
        grid_spec=grid_spec,
        out_shape=jax.ShapeDtypeStruct((P, D), F32),
        compiler_params=_cparams(),
        name="ffn",
    )(te, nv, xs, wg, wu, wd)


def _combine_kernel(pos_ref, ys_ref, rcol_ref, x1_ref, mod_ref, g_ref, b_ref, o_ref, buf, sem, *, tm):
    def row_copy(k, t, p):
        return pltpu.make_async_copy(ys_ref.at[pl.ds(p, 1), :], buf.at[k, pl.ds(t, 1), :], sem)

    def issue(t, carry):
        for k in range(2):
            row_copy(k, t, pos_ref[0, k, t]).start()
        return carry

    lax.fori_loop(0, tm, issue, 0)

    def drain(t, carry):
        for k in range(2):
            row_copy(0, 0, 0).wait()
        return carry

    lax.fori_loop(0, tm, drain, 0)

    mod = mod_ref[0]
    rc = rcol_ref[...]
    y = rc[:, 2:3] * buf[0] + rc[:, 3:4] * buf[1]
    z = ALPHA * x1_ref[...] + (1.0 + mod[5:6, :]) * y
    o_ref[...] = _layer_norm(z, g_ref[...], b_ref[...])


def _combine(pos, ys, rcol, x1, mod3, g, b, *, S, tm):
    N, D = x1.shape
    tpb = S // tm
    kern = functools.partial(_combine_kernel, tm=tm)
    return pl.pallas_call(
        kern,
        grid=(N // tm,),
        in_specs=[pl.BlockSpec((1, SUBLANES, tm), lambda i: (i, 0, 0), memory_space=pltpu.SMEM),
                  pl.BlockSpec(memory_space=pl.ANY),
                  pl.BlockSpec((tm, LANES), lambda i: (i, 0)),
                  pl.BlockSpec((tm, D), lambda i: (i, 0)),
                  pl.BlockSpec((1, 6, D), lambda i: (i // tpb, 0, 0)),
                  pl.BlockSpec((1, D), lambda i: (0, 0)),
                  pl.BlockSpec((1, D), lambda i: (0, 0))],
        out_specs=pl.BlockSpec((tm, D), lambda i: (i, 0)),
        out_shape=jax.ShapeDtypeStruct((N, D), F32),
        scratch_shapes=[pltpu.VMEM((2, tm, D), F32), pltpu.SemaphoreType.DMA(())],
        compiler_params=_cparams(),
        name="combine",
    )(pos, ys, rcol, x1, mod3, g, b)


def _permute_w_in(w_in):
    D = w_in.shape[0]
    o = 0
    parts = {}
    for name, width in (("mq", M_W), ("mk", M_W), ("mv", M_W), ("mo", M_W), ("mi", M_HEADS), ("mf", M_HEADS),
                        ("gq", G_KW), ("gk", G_KW), ("gv", G_W), ("gg", G_W), ("ga", G_RANK)):
        parts[name] = w_in[:, o:o + width]
        o += width
    z = lambda n: jnp.zeros((D, n), w_in.dtype)
    small = jnp.concatenate([parts["mi"], z(SM_F - M_HEADS), parts["mf"], z(SM_A - SM_F - M_HEADS),
                             parts["ga"], z(LANES - SM_A - G_RANK)], axis=1)
    return jnp.concatenate([parts["mq"], parts["mk"], parts["mv"], parts["mo"], parts["gq"], parts["gk"],
                            parts["gv"], parts["gg"], small], axis=1)


def _layer(x, c, l, w_ada, b_ada, w_in, w_conv, b_conv, b_igate, b_fgate, mlstm_norm_g, w_gla_a, b_gla_a,
           gla_norm_g, w_out, ln1_g, ln1_b, w_route_group, b_route_group, w_route_expert, b_route_expert,
           w_gate, w_up, w_down, ln2_g, ln2_b):
    B, S, D = x.shape
    N = B * S
    x2 = x.reshape(N, D)
    tm_in = min(512, S)
    tm = min(256, S)
    lm = min(256, S)

    mod3 = _ada(c, w_ada[l], b_ada[l]).reshape(B, 6, D)

    w_p = _permute_w_in(w_in[l]).astype(BF16)
    wa_pad = jnp.zeros((LANES, G_KW), F32).at[SM_A:SM_A + G_RANK].set(w_gla_a[l]).astype(BF16)
    bg = (jnp.zeros((2 * SUBLANES, 1), F32).at[0:M_HEADS, 0].set(b_igate[l])
          .at[SUBLANES:SUBLANES + M_HEADS, 0].set(b_fgate[l]))
    oa, la, g3 = _inproj(x2, mod3, w_p, w_conv[l], b_conv[l].reshape(1, -1), wa_pad,
                         b_gla_a[l].reshape(1, -1), bg, S=S, tm=tm_in, lm=lm)

    u_tri = jnp.asarray(np.triu(np.ones((lm, lm), np.float32)))
    hm = _mlstm(oa, g3, u_tri, mlstm_norm_g[l].reshape(1, -1), B=B, S=S, L=lm)
    w3_np, mk_np = _gla_consts()
    hg = _gla(oa, la, jnp.asarray(w3_np, BF16), jnp.asarray(mk_np), gla_norm_g[l].reshape(1, -1), B=B, S=S)

    wr = (jnp.zeros((D, LANES), F32).at[:, 0:N_GROUPS].set(w_route_group[l])
          .at[:, SUBLANES:SUBLANES + N_EXP].set(w_route_expert[l]))
    br = (jnp.zeros((1, LANES), F32).at[0, 0:N_GROUPS].set(b_route_group[l])
          .at[0, SUBLANES:SUBLANES + N_EXP].set(b_route_expert[l]))
    x1, u2, rrow, rcol = _outproj(hm, hg, w_out[l].astype(BF16), x2, mod3, ln1_g[l].reshape(1, -1),
                                  ln1_b[l].reshape(1, -1), wr, br, S=S, tm=tm)

    u_cnt = jnp.asarray(np.triu(np.ones((tm, tm), np.float32)), BF16)
    ltri = jnp.asarray(np.tril(np.ones((N_EXP, N_EXP), np.float32), -1))
    pos, meta = _route(rrow, u_cnt, ltri, TM=FFN_TM)
    n_tiles = (2 * N) // FFN_TM + N_EXP
    te = meta[0, :n_tiles]
    nv = meta[1, 0:1]

    xs = _dispatch(pos, u2, jnp.zeros((n_tiles * FFN_TM, D), F32), tm=tm)
    ys = _ffn(te, nv, xs, w_gate[l], w_up[l], w_down[l], TM=FFN_TM)
    out = _combine(pos, ys, rcol, x1, mod3, ln2_g[l].reshape(1, -1), ln2_b[l].reshape(1, -1), S=S, tm=tm)
    return out.reshape(B, S, D)


def kernel(x, c, w_ada, b_ada, w_in, w_conv, b_conv, b_igate, b_fgate, mlstm_norm_g, w_gla_a, b_gla_a,
           gla_norm_g, w_out, ln1_g, ln1_b, w_route_group, b_route_group, w_route_expert, b_route_expert,
           w_gate, w_up, w_down, ln2_g, ln2_b):
    for l in range(DEPTH):
        x = _layer(x, c, l, w_ada, b_ada, w_in, w_conv, b_conv, b_igate, b_fgate, mlstm_norm_g, w_gla_a,
                   b_gla_a, gla_norm_g, w_out, ln1_g, ln1_b, w_route_group, b_route_group, w_route_expert,
                   b_route_expert, w_gate, w_up, w_down, ln2_g, ln2_b)
    return x
```

```python
import functools

import numpy as np
import jax
import jax.numpy as jnp
from jax import lax
from jax.experimental import pallas as pl
from jax.experimental.pallas import tpu as pltpu

F32 = jnp.float32
BF16 = jnp.bfloat16
HIGHEST = lax.Precision.HIGHEST

DEPTH = 1
M_HEADS = 4
M_HD = 128
M_W = M_HEADS * M_HD
CONV_W = 4
G_HEADS = 4
G_DK = 64
G_DV = 128
G_W = G_HEADS * G_DV
G_KW = G_HEADS * G_DK
G_RANK = 16
G_TAU = 16.0
G_CHUNK = 64
N_GROUPS = 4
E_PER_G = 8
N_EXP = N_GROUPS * E_PER_G
D_EXP = 512
ALPHA = (2 * DEPTH) ** 0.25
LN_EPS = 1e-5

LANES = 128
SUBLANES = 8
VMEM_LIMIT = 48 * 1024 * 1024

C_QK = 0
C_VO = 1024
C_GQK = 2048
C_GV = 2560
C_GG = 3072
C_SMALL = 3584
C_TOT = 3712
SM_I, SM_F, SM_A = 0, 8, 16

FFN_TM = 256
GRAN = SUBLANES
G_LAST = LANES - 1


def _cparams(n_axes=1):
    return pltpu.CompilerParams(dimension_semantics=("arbitrary",) * n_axes,
                                vmem_limit_bytes=VMEM_LIMIT)


def _sigmoid(x):
    return 1.0 / (1.0 + jnp.exp(-x))


def _log_sigmoid(x):
    return jnp.minimum(x, 0.0) - jnp.log(1.0 + jnp.exp(-jnp.abs(x)))


def _ada_kernel(c_ref, w_ref, b_ref, o_ref):
    c = c_ref[...]
    ca = c * _sigmoid(c)
    o_ref[...] = jnp.dot(ca, w_ref[...], preferred_element_type=F32, precision=HIGHEST) + b_ref[...]


def _ada(c, w, b):
    B, D = c.shape
    n_out = w.shape[1]
    tn = 1024
    return pl.pallas_call(
        _ada_kernel,
        grid=(n_out // tn,),
        in_specs=[pl.BlockSpec((B, D), lambda j: (0, 0)),
                  pl.BlockSpec((D, tn), lambda j: (0, j)),
                  pl.BlockSpec((1, tn), lambda j: (0, j))],
        out_specs=pl.BlockSpec((B, tn), lambda j: (0, j)),
        out_shape=jax.ShapeDtypeStruct((B, n_out), F32),
        compiler_params=_cparams(),
        name="ada",
    )(c, w, b.reshape(1, n_out))


def _inproj_kernel(x_ref, mod_ref, w_ref, wc_ref, bc_ref, wa_ref, ba_ref, bg_ref,
                   oa_ref, la_ref, g_ref, halo_ref, *, tm, tpb, lm):
    i = pl.program_id(0)

    @pl.when(i % tpb == 0)
    def _():
        halo_ref[...] = jnp.zeros_like(halo_ref)

    mod = mod_ref[0]
    u = (x_ref[...] * (1.0 + mod[1:2, :]) + mod[0:1, :]).astype(BF16)

    p = jnp.dot(u, w_ref[:, C_QK:C_QK + 2 * M_W], preferred_element_type=F32)
    ext = jnp.concatenate([halo_ref[...], p], axis=0)
    acc = bc_ref[...] + wc_ref[CONV_W - 1:CONV_W, :] * p
    for j in range(CONV_W - 1):
        sh = pltpu.roll(ext, CONV_W - 1 - j, 0)[SUBLANES:, :]
        acc = acc + wc_ref[j:j + 1, :] * sh
    halo_ref[...] = p[tm - SUBLANES:, :]
    qk = acc * _sigmoid(acc)
    oa_ref[:, C_QK:C_QK + M_W] = qk[:, :M_W].astype(BF16)
    oa_ref[:, C_QK + M_W:C_QK + 2 * M_W] = (qk[:, M_W:] * (M_HD ** -0.5)).astype(BF16)

    p = jnp.dot(u, w_ref[:, C_VO:C_VO + 2 * M_W], preferred_element_type=F32)
    oa_ref[:, C_VO:C_VO + 2 * M_W] = p.astype(BF16)

    p = jnp.dot(u, w_ref[:, C_GQK:C_GQK + G_KW], preferred_element_type=F32)
    oa_ref[:, C_GQK:C_GQK + G_KW] = (p * (G_DK ** -0.5)).astype(BF16)
    p = jnp.dot(u, w_ref[:, C_GQK + G_KW:C_SMALL], preferred_element_type=F32)
    oa_ref[:, C_GQK + G_KW:C_SMALL] = p.astype(BF16)

    ps = jnp.dot(u, w_ref[:, C_SMALL:C_TOT], preferred_element_type=F32)
    la = jnp.dot(ps.astype(BF16), wa_ref[...], preferred_element_type=F32) + ba_ref[...]
    la_ref[...] = _log_sigmoid(la) * (1.0 / G_TAU)
    pt = ps.T
    gi = pt[SM_I:SM_I + SUBLANES, :] + bg_ref[0:SUBLANES, :]
    gf = _log_sigmoid(pt[SM_F:SM_F + SUBLANES, :] + bg_ref[SUBLANES:2 * SUBLANES, :])
    for j in range(tm // lm):
        g_ref[j, 0:SUBLANES, :] = gi[:, j * lm:(j + 1) * lm]
        g_ref[j, SUBLANES:2 * SUBLANES, :] = gf[:, j * lm:(j + 1) * lm]


def _inproj(x2, mod3, w_p, w_conv, b_conv, wa_pad, b_gla, bg, *, S, tm, lm):
    N, D = x2.shape
    tpb = S // tm
    kern = functools.partial(_inproj_kernel, tm=tm, tpb=tpb, lm=lm)
    return pl.pallas_call(
        kern,
        grid=(N // tm,),
        in_specs=[pl.BlockSpec((tm, D), lambda i: (i, 0)),
                  pl.BlockSpec((1, 6, D), lambda i: (i // tpb, 0, 0)),
                  pl.BlockSpec((D, C_TOT), lambda i: (0, 0)),
                  pl.BlockSpec((CONV_W, 2 * M_W), lambda i: (0, 0)),
                  pl.BlockSpec((1, 2 * M_W), lambda i: (0, 0)),
                  pl.BlockSpec((LANES, G_KW), lambda i: (0, 0)),
                  pl.BlockSpec((1, G_KW), lambda i: (0, 0)),
                  pl.BlockSpec((2 * SUBLANES, 1), lambda i: (0, 0))],
        out_specs=[pl.BlockSpec((tm, C_SMALL), lambda i: (i, 0)),
                   pl.BlockSpec((tm, G_KW), lambda i: (i, 0)),
                   pl.BlockSpec((tm // lm, 2 * SUBLANES, lm), lambda i: (i, 0, 0))],
        out_shape=[jax.ShapeDtypeStruct((N, C_SMALL), BF16),
                   jax.ShapeDtypeStruct((N, G_KW), F32),
                   jax.ShapeDtypeStruct((N // lm, 2 * SUBLANES, lm), F32)],
        scratch_shapes=[pltpu.VMEM((SUBLANES, 2 * M_W), F32)],
        compiler_params=_cparams(),
        name="inproj",
    )(x2, mod3, w_p, w_conv, b_conv, wa_pad, b_gla, bg)


def _mlstm_kernel(qk_ref, vo_ref, g_ref, u_ref, gain_ref, out_ref, c_ref, *, L, NC):
    c_ref[...] = jnp.zeros_like(c_ref)
    lane = lax.broadcasted_iota(jnp.int32, (SUBLANES, L), 1)
    tril = (lax.broadcasted_iota(jnp.int32, (L, L), 0) >= lax.broadcasted_iota(jnp.int32, (L, L), 1))
    ones_v = jnp.ones((L, M_HD), BF16)
    zpad = jnp.zeros((LANES - 5 * SUBLANES, L), F32)

    def chunk(c, m_prev):
        r0 = pl.multiple_of(c * L, L)
        gi = g_ref[c, 0:SUBLANES, :]
        gf = g_ref[c, SUBLANES:2 * SUBLANES, :]
        b = jnp.dot(gf, u_ref[...], preferred_element_type=F32, precision=HIGHEST)
        a = gi - b
        G = a
        s = 1
        while s < L:
            sh = pltpu.roll(G, s, 1)
            G = jnp.maximum(G, jnp.where(lane >= s, sh, -jnp.inf))
            s *= 2
        M = jnp.maximum(G, m_prev)
        mt = b + M
        ML = M[:, L - 1:L]
        m_new = b[:, L - 1:L] + ML
        Z = jnp.concatenate([M, mt, a, jnp.broadcast_to(m_prev, (SUBLANES, L)),
                             jnp.broadcast_to(ML, (SUBLANES, L)), zpad], axis=0)
        Zt = Z.T
        dec = jnp.exp(m_prev - ML)
        for h in range(M_HEADS):
            M_col = Zt[:, h:h + 1]
            mt_col = Zt[:, SUBLANES + h:SUBLANES + h + 1]
            a_col = Zt[:, 2 * SUBLANES + h:2 * SUBLANES + h + 1]
            mp_col = Zt[:, 3 * SUBLANES + h:3 * SUBLANES + h + 1]
            ML_col = Zt[:, 4 * SUBLANES + h:4 * SUBLANES + h + 1]
            hs = slice(h * M_HD, (h + 1) * M_HD)
            hs2 = slice(M_W + h * M_HD, M_W + (h + 1) * M_HD)
            q = qk_ref[pl.ds(r0, L), hs]
            k = qk_ref[pl.ds(r0, L), hs2]
            v = vo_ref[pl.ds(r0, L), hs]
            og = vo_ref[pl.ds(r0, L), hs2]
            dm = jnp.exp(jnp.where(tril, a[h:h + 1, :] - M_col, -jnp.inf))
            sc = lax.dot_general(q, k, (((1,), (1,)), ((), ())), preferred_element_type=F32)
            pm = (sc * dm).astype(BF16)
            vext = jnp.concatenate([v, ones_v], axis=1)
            cst = c_ref[h]
            nd = (jnp.dot(pm, vext, preferred_element_type=F32)
                  + jnp.exp(mp_col - M_col) * jnp.dot(q, cst.astype(BF16), preferred_element_type=F32))
            hh = nd[:, :M_HD] / jnp.maximum(jnp.abs(nd[:, M_HD:]), jnp.exp(-mt_col))
            hh = _sigmoid(og.astype(F32)) * hh
            hn = hh * lax.rsqrt(jnp.mean(hh * hh, axis=-1, keepdims=True) + LN_EPS)
            out_ref[pl.ds(r0, L), hs] = (hn * gain_ref[:, hs]).astype(BF16)
            kw = (jnp.exp(a_col - ML_col) * k.astype(F32)).astype(BF16)
            upd = lax.dot_general(kw, vext, (((0,), (0,)), ((), ())), preferred_element_type=F32)
            c_ref[h] = dec[h:h + 1, :] * cst + upd
        return m_new

    lax.fori_loop(0, NC, chunk, jnp.zeros((SUBLANES, 1), F32))


def _mlstm(oa, g3, u_tri, gain, *, B, S, L):
    N = oa.shape[0]
    NC = S // L
    kern = functools.partial(_mlstm_kernel, L=L, NC=NC)
    return pl.pallas_call(
        kern,
        grid=(B,),
        in_specs=[pl.BlockSpec((S, 2 * M_W), lambda b: (b, C_QK // (2 * M_W))),
                  pl.BlockSpec((S, 2 * M_W), lambda b: (b, C_VO // (2 * M_W))),
                  pl.BlockSpec((NC, 2 * SUBLANES, L), lambda b: (b, 0, 0)),
                  pl.BlockSpec((L, L), lambda b: (0, 0)),
                  pl.BlockSpec((1, M_W), lambda b: (0, 0))],
        out_specs=pl.BlockSpec((S, M_W), lambda b: (b, 0)),
        out_shape=jax.ShapeDtypeStruct((N, M_W), BF16),
        scratch_shapes=[pltpu.VMEM((M_HEADS, M_HD, 2 * M_HD), F32)],
        compiler_params=_cparams(),
        name="mlstm",
    )(oa, oa, g3, u_tri, gain)


_G_LEVELS = 6
_G_XROW = 2 * G_CHUNK + SUBLANES


def _gla_consts():
    L = G_CHUNK
    t = np.arange(L)
    blocks = [(t[None, :] <= t[:, None]).astype(np.float32),
              (t[None, :] > t[:, None]).astype(np.float32),
              np.ones((SUBLANES, L), np.float32)]
    masks = [np.eye(L, dtype=np.float32)]
    m = 1
    while m < L:
        wl = np.zeros((L, L), np.float32)
        for r in range(L):
            r0 = (r // (2 * m)) * 2 * m + m
            if r % (2 * m) >= m:
                wl[r, r0:r + 1] = 1.0
            else:
                wl[r, r + 1:r0] = 1.0
        blocks.append(wl)
        tt, ss = t[:, None], t[None, :]
        masks.append(((tt // (2 * m) == ss // (2 * m)) & (tt % (2 * m) >= m)
                      & (ss % (2 * m) < m)).astype(np.float32))
        m *= 2
    w = np.concatenate(blocks, axis=0)
    w3 = np.concatenate([w, w, w], axis=1)
    mk = np.stack([np.concatenate([x, x], axis=0) for x in masks])
    return w3, mk


def _gla_kernel(qk_ref, v_ref, gg_ref, la_ref, w3_ref, mk_ref, gain_ref, out_ref, st_ref, *, NC):
    L = G_CHUNK
    st_ref[...] = jnp.zeros_like(st_ref)
    lane_lo = lax.broadcasted_iota(jnp.int32, (L, LANES), 1) < G_DK
    br = lax.broadcasted_iota(jnp.int32, (2 * G_DV, LANES), 0) < G_DV
    bl = lax.broadcasted_iota(jnp.int32, (2 * G_DV, LANES), 1) < G_DK
    bmask = br == bl

    def chunk(c, carry):
        r0 = pl.multiple_of(c * L, L)
        la = la_ref[pl.ds(r0, L), :]
        hi = la.astype(BF16)
        r1 = la - hi.astype(F32)
        mid = r1.astype(BF16)
        lo = (r1 - mid.astype(F32)).astype(BF16)
        stk = jnp.concatenate([hi, mid, lo], axis=0)
        X = jnp.exp(jnp.dot(w3_ref[...], stk, preferred_element_type=F32))
        q = qk_ref[pl.ds(r0, L), 0:G_KW].astype(F32)
        k = qk_ref[pl.ds(r0, L), G_KW:2 * G_KW].astype(F32)
        outs = []
        for p in range(2):
            ls = slice(LANES * p, LANES * (p + 1))
            qp, kp = q[:, ls], k[:, ls]
            A = jnp.zeros((2 * L, L), F32)
            for lev in range(_G_LEVELS + 1):
                if lev == 0:
                    qt, kt = qp, kp
                else:
                    xl = X[_G_XROW + L * (lev - 1):_G_XROW + L * lev, ls]
                    qt, kt = qp * xl, kp * xl
                qs = jnp.concatenate([jnp.where(lane_lo, qt, 0.0), jnp.where(lane_lo, 0.0, qt)],
                                     axis=0).astype(BF16)
                sc = lax.dot_general(qs, kt.astype(BF16), (((1,), (1,)), ((), ())),
                                     preferred_element_type=F32)
                A = A + sc * mk_ref[lev]
            Ab = A.astype(BF16)
            vp = v_ref[pl.ds(r0, L), 2 * G_DV * p:2 * G_DV * (p + 1)]
            oi0 = jnp.dot(Ab[0:L], vp[:, 0:G_DV], preferred_element_type=F32)
            oi1 = jnp.dot(Ab[L:2 * L], vp[:, G_DV:2 * G_DV], preferred_element_type=F32)
            st = st_ref[p]
            qc = (qp * X[0:L, ls]).astype(BF16)
            o_inter = lax.dot_general(qc, st.astype(BF16), (((1,), (1,)), ((), ())),
                                      preferred_element_type=F32)
            outs.append(o_inter + jnp.concatenate([oi0, oi1], axis=1))
            kc = (kp * X[L:2 * L, ls]).astype(BF16)
            upd = lax.dot_general(vp, kc, (((0,), (0,)), ((), ())), preferred_element_type=F32)
            dec = X[2 * L:2 * L + 1, ls]
            st_ref[p] = jnp.where(bmask, dec * st + upd, 0.0)
        gg = gg_ref[pl.ds(r0, L), :].astype(F32)
        gate = gg * _sigmoid(gg)
        for p in range(2):
            for hh in range(2):
                h = 2 * p + hh
                o = outs[p][:, G_DV * hh:G_DV * (hh + 1)]
                hn = o * lax.rsqrt(jnp.mean(o * o, axis=-1, keepdims=True) + LN_EPS)
                hs = slice(G_DV * h, G_DV * (h + 1))
                out_ref[pl.ds(r0, L), hs] = (hn * gain_ref[:, hs] * gate[:, hs]).astype(BF16)
        return carry

    lax.fori_loop(0, NC, chunk, 0)


def _gla(oa, la, w3, mk, gain, *, B, S):
    N = oa.shape[0]
    NC = S // G_CHUNK
    kern = functools.partial(_gla_kernel, NC=NC)
    return pl.pallas_call(
        kern,
        grid=(B,),
        in_specs=[pl.BlockSpec((S, 2 * G_KW), lambda b: (b, C_GQK // (2 * G_KW))),
                  pl.BlockSpec((S, G_W), lambda b: (b, C_GV // G_W)),
                  pl.BlockSpec((S, G_W), lambda b: (b, C_GG // G_W)),
                  pl.BlockSpec((S, G_KW), lambda b: (b, 0)),
                  pl.BlockSpec(w3.shape, lambda b: (0, 0)),
                  pl.BlockSpec(mk.shape, lambda b: (0, 0, 0)),
                  pl.BlockSpec((1, G_W), lambda b: (0, 0))],
        out_specs=pl.BlockSpec((S, G_W), lambda b: (b, 0)),
        out_shape=jax.ShapeDtypeStruct((N, G_W), BF16),
        scratch_shapes=[pltpu.VMEM((2, 2 * G_DV, LANES), F32)],
        compiler_params=_cparams(),
        name="gla",
    )(oa, oa, oa, la, w3, mk, gain)


def _layer_norm(z, g, b):
    mu = jnp.mean(z, axis=-1, keepdims=True)
    zc = z - mu
    var = jnp.mean(zc * zc, axis=-1, keepdims=True)
    return zc * lax.rsqrt(var + LN_EPS) * g + b


def _outproj_kernel(hm_ref, hg_ref, w_ref, x_ref, mod_ref, g_ref, b_ref, wr_ref, br_ref,
                    x1_ref, u2_ref, rrow_ref, *, tm):
    mod = mod_ref[0]
    y = (jnp.dot(hm_ref[...], w_ref[0:M_W, :], preferred_element_type=F32)
         + jnp.dot(hg_ref[...], w_ref[M_W:M_W + G_W, :], preferred_element_type=F32))
    z = ALPHA * x_ref[...] + (1.0 + mod[2:3, :]) * y
    x1 = _layer_norm(z, g_ref[...], b_ref[...])
    x1_ref[...] = x1
    u2 = x1 * (1.0 + mod[4:5, :]) + mod[3:4, :]
    u2_ref[...] = u2.astype(BF16)

    logits = jnp.dot(u2, wr_ref[...], preferred_element_type=F32, precision=HIGHEST) + br_ref[...]
    lt = logits.T
    row = lax.broadcasted_iota(jnp.int32, (SUBLANES, tm), 0)
    gl = jnp.where(row < N_GROUPS, lt[0:SUBLANES, :], -jnp.inf)
    gmax = jnp.max(gl, axis=0, keepdims=True)
    gsel = jnp.min(jnp.where(gl == gmax, row, SUBLANES), axis=0, keepdims=True)
    pg = 1.0 / jnp.sum(jnp.exp(gl - gmax), axis=0, keepdims=True)
    ein = jnp.zeros((SUBLANES, tm), F32)
    for g in range(N_GROUPS):
        ein = jnp.where(gsel == g, lt[SUBLANES * (g + 1):SUBLANES * (g + 2), :], ein)
    v1 = jnp.max(ein, axis=0, keepdims=True)
    i1 = jnp.min(jnp.where(ein == v1, row, SUBLANES), axis=0, keepdims=True)
    rest = jnp.where(row == i1, -jnp.inf, ein)
    v2 = jnp.max(rest, axis=0, keepdims=True)
    i2 = jnp.min(jnp.where(rest == v2, row, SUBLANES), axis=0, keepdims=True)
    t2 = jnp.exp(v2 - v1)
    p1 = 1.0 / (1.0 + t2)
    e0 = (gsel * E_PER_G + i1).astype(F32)
    e1 = (gsel * E_PER_G + i2).astype(F32)
    rrow_ref[0] = jnp.concatenate([e0, e1, pg * p1, pg * (t2 * p1), jnp.zeros((SUBLANES - 4, tm), F32)],
                                  axis=0)


def _outproj(hm, hg, w_out, x2, mod3, g, b, wr, br, *, S, tm):
    N, D = x2.shape
    tpb = S // tm
    kern = functools.partial(_outproj_kernel, tm=tm)
    return pl.pallas_call(
        kern,
        grid=(N // tm,),
        in_specs=[pl.BlockSpec((tm, M_W), lambda i: (i, 0)),
                  pl.BlockSpec((tm, G_W), lambda i: (i, 0)),
                  pl.BlockSpec((M_W + G_W, D), lambda i: (0, 0)),
                  pl.BlockSpec((tm, D), lambda i: (i, 0)),
                  pl.BlockSpec((1, 6, D), lambda i: (i // tpb, 0, 0)),
                  pl.BlockSpec((1, D), lambda i: (0, 0)),
                  pl.BlockSpec((1, D), lambda i: (0, 0)),
                  pl.BlockSpec((D, LANES), lambda i: (0, 0)),
                  pl.BlockSpec((1, LANES), lambda i: (0, 0))],
        out_specs=[pl.BlockSpec((tm, D), lambda i: (i, 0)),
                   pl.BlockSpec((tm, D), lambda i: (i, 0)),
                   pl.BlockSpec((1, SUBLANES, tm), lambda i: (i, 0, 0))],
        out_shape=[jax.ShapeDtypeStruct((N, D), F32),
                   jax.ShapeDtypeStruct((N, D), BF16),
                   jax.ShapeDtypeStruct((N // tm, SUBLANES, tm), F32)],
        compiler_params=_cparams(),
        name="outproj",
    )(hm, hg, w_out, x2, mod3, g, b, wr, br)


def _slots_per_tile(tb):
    worst = 2 * tb + N_EXP * (GRAN - 1)
    return -(-worst // LANES) * LANES


def _ffn_tiles(n_tok, tb):
    worst_rows = 2 * n_tok + (n_tok // tb) * N_EXP * (GRAN - 1)
    return -(-worst_rows // FFN_TM) + N_EXP


def _route_kernel(rr_ref, u_ref, lt_ref, srow_ref, col_ref, gd_ref, meta_ref, mg_ref, part_ref,
                  *, NT, tb, TM):
    iota_e = lax.broadcasted_iota(jnp.int32, (N_EXP, tb), 0).astype(F32)
    glane = lax.broadcasted_iota(jnp.int32, (N_EXP, LANES), 1).astype(F32)
    ltri = lt_ref[...]

    def prefix_e(col):
        return jnp.dot(ltri, jnp.broadcast_to(col, (N_EXP, LANES)),
                       preferred_element_type=F32, precision=HIGHEST)[:, 0:1]

    def p1(j, run8):
        r = rr_ref[j]
        oh0 = jnp.where(iota_e == r[0:1, :], 1.0, 0.0)
        oh1 = jnp.where(iota_e == r[1:2, :], 1.0, 0.0)
        cum0 = jnp.dot(oh0.astype(BF16), u_ref[...], preferred_element_type=F32)
        cum1 = jnp.dot(oh1.astype(BF16), u_ref[...], preferred_element_type=F32)
        c0 = jnp.sum(oh0, axis=1, keepdims=True)
        n8 = jnp.floor((c0 + jnp.sum(oh1, axis=1, keepdims=True) + (GRAN - 1.0)) * (1.0 / GRAN))
        lo8 = prefix_e(n8)
        s0 = jnp.sum(oh0 * (GRAN * lo8 + cum0 - 1.0), axis=0, keepdims=True)
        s1 = jnp.sum(oh1 * (GRAN * lo8 + c0 + cum1 - 1.0), axis=0, keepdims=True)
        info = jnp.concatenate([s0, s1, r[2:4, :], jnp.zeros((SUBLANES - 4, tb), F32)], axis=0)
        srow_ref[j] = info.astype(jnp.int32)
        col_ref[pl.ds(pl.multiple_of(j * tb, tb), tb), :] = jnp.concatenate(
            [info, jnp.zeros((LANES - SUBLANES, tb), F32)], axis=0).T
        mg = jnp.where((lo8 <= glane) & (glane < lo8 + n8), 1.0, 0.0)
        mg_ref[j] = mg
        part = jnp.sum(mg * (run8 + glane - lo8), axis=0, keepdims=True)
        gcnt = jnp.broadcast_to(jnp.sum(n8, axis=0, keepdims=True), (1, LANES))
        part_ref[j] = jnp.concatenate([part, gcnt, jnp.zeros((SUBLANES - 2, LANES), F32)], axis=0)
        return run8 + n8

    tot8 = lax.fori_loop(0, NT, p1, jnp.zeros((N_EXP, 1), F32))
    seg_t = jnp.floor((tot8 * GRAN + (TM - 1.0)) * (1.0 / TM))
    base_t = prefix_e(seg_t)
    base8 = base_t * (TM // GRAN)
    lane1 = lax.broadcasted_iota(jnp.int32, (1, LANES), 1)

    def p2(j, carry):
        pr = part_ref[j]
        dst = (pr[0:1, :] + jnp.sum(mg_ref[j] * base8, axis=0, keepdims=True)) * GRAN
        gd_ref[j] = jnp.where(lane1 == G_LAST, pr[1:2, :], dst).astype(jnp.int32)
        return carry

    lax.fori_loop(0, NT, p2, 0)
    eye = jnp.where(glane == lax.broadcasted_iota(jnp.int32, (N_EXP, LANES), 0).astype(F32), 1.0, 0.0)
    tail_row = jnp.sum(eye * ((base8 + tot8) * GRAN), axis=0, keepdims=True)
    tail_n8 = jnp.sum(eye * (seg_t * (TM // GRAN) - tot8), axis=0, keepdims=True)
    nv_l = jnp.broadcast_to(jnp.sum(seg_t, axis=0, keepdims=True), (1, LANES))
    gd_ref[NT] = jnp.where(lane1 == G_LAST, nv_l, tail_row).astype(jnp.int32)
    gd_ref[NT + 1] = tail_n8.astype(jnp.int32)
    ti = lax.broadcasted_iota(jnp.int32, (N_EXP, tb), 1).astype(F32)
    te = jnp.sum(jnp.where(base_t <= ti, 1.0, 0.0), axis=0, keepdims=True) - 1.0
    nv = jnp.broadcast_to(jnp.sum(seg_t, axis=0, keepdims=True), (1, tb))
    meta_ref[...] = jnp.concatenate([te, nv, jnp.zeros((SUBLANES - 2, tb), F32)],
                                    axis=0).astype(jnp.int32)


def _route(rrow, u_cnt, ltri, *, TM):
    NT, _, tb = rrow.shape
    kern = functools.partial(_route_kernel, NT=NT, tb=tb, TM=TM)
    full3 = lambda i: (0, 0, 0)
    return pl.pallas_call(
        kern,
        grid=(1,),
        in_specs=[pl.BlockSpec((NT, SUBLANES, tb), full3),
                  pl.BlockSpec((tb, tb), lambda i: (0, 0)),
                  pl.BlockSpec((N_EXP, N_EXP), lambda i: (0, 0))],
        out_specs=[pl.BlockSpec((NT, SUBLANES, tb), full3),
                   pl.BlockSpec((NT * tb, LANES), lambda i: (0, 0)),
                   pl.BlockSpec((NT + 2, 1, LANES), full3),
                   pl.BlockSpec((SUBLANES, tb), lambda i: (0, 0))],
        out_shape=[jax.ShapeDtypeStruct((NT, SUBLANES, tb), jnp.int32),
                   jax.ShapeDtypeStruct((NT * tb, LANES), F32),
                   jax.ShapeDtypeStruct((NT + 2, 1, LANES), jnp.int32),
                   jax.ShapeDtypeStruct((SUBLANES, tb), jnp.int32)],
        scratch_shapes=[pltpu.VMEM((NT, N_EXP, LANES), F32), pltpu.VMEM((NT, SUBLANES, LANES), F32)],
        compiler_params=_cparams(),
        name="route",
    )(rrow, u_cnt, ltri)


def _granule_copy(src_ref, src_row, dst_ref, dst_row, sem):
    return pltpu.make_async_copy(src_ref.at[pl.ds(src_row, GRAN), :], dst_ref.at[pl.ds(dst_row, GRAN), :], sem)


def _dispatch_kernel(gd_ref, srow_ref, u_ref, xs_ref, buf, zbuf, sems, *, NT, SL, TM, n_tiles):
    j = pl.program_id(0)
    slot = j % 2
    zsem = sems.at[2]

    def drain(tile, sl):
        def w(g, carry):
            _granule_copy(buf.at[sl], 0, xs_ref, 0, sems.at[sl]).wait()
            return carry
        lax.fori_loop(0, gd_ref[tile, G_LAST], w, 0)

    def tile_fill(t):
        return pltpu.make_async_copy(zbuf, xs_ref.at[pl.ds(pl.multiple_of(t * TM, TM), TM), :], zsem)

    def zero_fill(wait):
        for e in range(N_EXP):
            def zg(g, carry, e=e):
                cp = _granule_copy(zbuf, 0, xs_ref, pl.multiple_of(gd_ref[NT, e] + g * GRAN, GRAN), zsem)
                cp.wait() if wait else cp.start()
                return carry
            lax.fori_loop(0, gd_ref[NT + 1, e], zg, 0)

        def zt(t, carry):
            tile_fill(t).wait() if wait else tile_fill(t).start()
            return carry
        lax.fori_loop(gd_ref[NT, G_LAST], n_tiles, zt, 0)

    @pl.when(j == 0)
    def _():
        zbuf[...] = jnp.zeros_like(zbuf)
        zero_fill(False)

    @pl.when(j >= 2)
    def _():
        drain(j - 2, slot)

    s = srow_ref[0]
    rows = lax.broadcasted_iota(jnp.int32, (SL, s.shape[1]), 0)
    oh = jnp.where((rows == s[0:1, :]) | (rows == s[1:2, :]), 1.0, 0.0).astype(BF16)
    buf[slot] = jnp.dot(oh, u_ref[...], preferred_element_type=F32)

    def issue(g, carry):
        _granule_copy(buf.at[slot], pl.multiple_of(g * GRAN, GRAN), xs_ref,
                      pl.multiple_of(gd_ref[j, g], GRAN), sems.at[slot]).start()
        return carry

    lax.fori_loop(0, gd_ref[j, G_LAST], issue, 0)

    @pl.when(j == NT - 1)
    def _():
        drain(j, slot)
        if NT > 1:
            drain(j - 1, 1 - slot)
        zero_fill(True)


def _dispatch(gd, srow, u2, *, n_tiles, TM):
    N, D = u2.shape
    NT, _, tb = srow.shape
    SL = _slots_per_tile(tb)
    n_rows = n_tiles * TM
    kern = functools.partial(_dispatch_kernel, NT=NT, SL=SL, TM=TM, n_tiles=n_tiles)
    grid_spec = pltpu.PrefetchScalarGridSpec(
        num_scalar_prefetch=1,
        grid=(NT,),
        in_specs=[pl.BlockSpec((1, SUBLANES, tb), lambda j, gd: (j, 0, 0)),
                  pl.BlockSpec((tb, D), lambda j, gd: (j, 0))],
        out_specs=pl.BlockSpec(memory_space=pl.ANY),
        scratch_shapes=[pltpu.VMEM((2, SL, D), F32), pltpu.VMEM((TM, D), F32),
                        pltpu.SemaphoreType.DMA((3,))],
    )
    return pl.pallas_call(
        kern,
        grid_spec=grid_spec,
        out_shape=jax.ShapeDtypeStruct((n_rows, D), F32),
        compiler_params=_cparams(),
        name="dispatch",
    )(gd, srow, u2)


def _ffn_kernel(te_ref, nv_ref, xs_ref, wg_ref, wu_ref, wd_ref, o_ref, wgb, wub, wdb):
    i = pl.program_id(0)
    prev = te_ref[jnp.maximum(i - 1, 0)]

    @pl.when((i == 0) | (te_ref[i] != prev))
    def _():
        wgb[...] = wg_ref[0].astype(BF16)
        wub[...] = wu_ref[0].astype(BF16)
        wdb[...] = wd_ref[0].astype(BF16)

    @pl.when(i < nv_ref[0])
    def _():
        x = xs_ref[...].astype(BF16)
        g = jnp.dot(x, wgb[...], preferred_element_type=F32)
        u = jnp.dot(x, wub[...], preferred_element_type=F32)
        h = (g * _sigmoid(g) * u).astype(BF16)
        o_ref[...] = jnp.dot(h, wdb[...], preferred_element_type=F32)

    @pl.when(i >= nv_ref[0])
    def _():
        o_ref[...] = jnp.zeros_like(o_ref)


def _ffn(te, nv, xs, wg, wu, wd, *, TM):
    P, D = xs.shape
    n_tiles = P // TM
    grid_spec = pltpu.PrefetchScalarGridSpec(
        num_scalar_prefetch=2,
        grid=(n_tiles,),
        in_specs=[pl.BlockSpec((TM, D), lambda i, te, nv: (jnp.minimum(i, nv[0] - 1), 0)),
                  pl.BlockSpec((1, D, D_EXP), lambda i, te, nv: (te[i], 0, 0)),
                  pl.BlockSpec((1, D, D_EXP), lambda i, te, nv: (te[i], 0, 0)),
                  pl.BlockSpec((1, D_EXP, D), lambda i, te, nv: (te[i], 0, 0))],
        out_specs=pl.BlockSpec((TM, D), lambda i, te, nv: (i, 0)),
        scratch_shapes=[pltpu.VMEM((D, D_EXP), BF16), pltpu.VMEM((D, D_EXP), BF16),
                        pltpu.VMEM((D_EXP, D), BF16)],
    )
    return pl.pallas_call(
        _ffn_kernel,
        grid_spec=grid_spec,
        out_shape=jax.ShapeDtypeStruct((P, D), F32),
        compiler_params=_cparams(),
        name="ffn",
    )(te, nv, xs, wg, wu, wd)


def _combine_kernel(gd_ref, ys_ref, col_ref, x1_ref, mod_ref, g_ref, b_ref, o_ref, buf, sems, *, NT, SL):
    j = pl.program_id(0)
    slot = j % 2

    def fetch(tile, sl):
        def f(g, carry):
            _granule_copy(ys_ref, pl.multiple_of(gd_ref[tile, g], GRAN), buf.at[sl],
                          pl.multiple_of(g * GRAN, GRAN), sems.at[sl]).start()
            return carry
        lax.fori_loop(0, gd_ref[tile, G_LAST], f, 0)

    @pl.when(j == 0)
    def _():
        fetch(0, 0)

    @pl.when(j + 1 < NT)
    def _():
        fetch(j + 1, 1 - slot)

    ng = gd_ref[j, G_LAST]

    def w(g, carry):
        _granule_copy(ys_ref, 0, buf.at[slot], 0, sems.at[slot]).wait()
        return carry

    lax.fori_loop(0, ng, w, 0)

    rows = lax.broadcasted_iota(jnp.int32, (SL, 1), 0)
    yb = jnp.where(rows < ng * GRAN, buf[slot], 0.0).astype(BF16)
    col = col_ref[...]
    tb = col.shape[0]
    lanes = lax.broadcasted_iota(jnp.int32, (tb, SL), 1).astype(F32)
    wsel = (jnp.where(lanes == col[:, 0:1], col[:, 2:3], 0.0)
            + jnp.where(lanes == col[:, 1:2], col[:, 3:4], 0.0))
    whi = wsel.astype(BF16)
    wlo = (wsel - whi.astype(F32)).astype(BF16)
    y = jnp.dot(whi, yb, preferred_element_type=F32) + jnp.dot(wlo, yb, preferred_element_type=F32)
    mod = mod_ref[0]
    z = ALPHA * x1_ref[...] + (1.0 + mod[5:6, :]) * y
    o_ref[...] = _layer_norm(z, g_ref[...], b_ref[...])


def _combine(gd, ys, col, x1, mod3, g, b, *, S, tb):
    N, D = x1.shape
    NT = N // tb
    tpb = S // tb
    SL = _slots_per_tile(tb)
    kern = functools.partial(_combine_kernel, NT=NT, SL=SL)
    grid_spec = pltpu.PrefetchScalarGridSpec(
        num_scalar_prefetch=1,
        grid=(NT,),
        in_specs=[pl.BlockSpec(memory_space=pl.ANY),
                  pl.BlockSpec((tb, LANES), lambda j, gd: (j, 0)),
                  pl.BlockSpec((tb, D), lambda j, gd: (j, 0)),
                  pl.BlockSpec((1, 6, D), lambda j, gd: (j // tpb, 0, 0)),
                  pl.BlockSpec((1, D), lambda j, gd: (0, 0)),
                  pl.BlockSpec((1, D), lambda j, gd: (0, 0))],
        out_specs=pl.BlockSpec((tb, D), lambda j, gd: (j, 0)),
        scratch_shapes=[pltpu.VMEM((2, SL, D), F32), pltpu.SemaphoreType.DMA((2,))],
    )
    return pl.pallas_call(
        kern,
        grid_spec=grid_spec,
        out_shape=jax.ShapeDtypeStruct((N, D), F32),
        compiler_params=_cparams(),
        name="combine",
    )(gd, ys, col, x1, mod3, g, b)


def _permute_w_in(w_in):
    D = w_in.shape[0]
    o = 0
    parts = {}
    for name, width in (("mq", M_W), ("mk", M_W), ("mv", M_W), ("mo", M_W), ("mi", M_HEADS), ("mf", M_HEADS),
                        ("gq", G_KW), ("gk", G_KW), ("gv", G_W), ("gg", G_W), ("ga", G_RANK)):
        parts[name] = w_in[:, o:o + width]
        o += width
    z = lambda n: jnp.zeros((D, n), w_in.dtype)
    small = jnp.concatenate([parts["mi"], z(SM_F - M_HEADS), parts["mf"], z(SM_A - SM_F - M_HEADS),
                             parts["ga"], z(LANES - SM_A - G_RANK)], axis=1)
    return jnp.concatenate([parts["mq"], parts["mk"], parts["mv"], parts["mo"], parts["gq"], parts["gk"],
                            parts["gv"], parts["gg"], small], axis=1)


def _layer(x, c, l, w_ada, b_ada, w_in, w_conv, b_conv, b_igate, b_fgate, mlstm_norm_g, w_gla_a, b_gla_a,
           gla_norm_g, w_out, ln1_g, ln1_b, w_route_group, b_route_group, w_route_expert, b_route_expert,
           w_gate, w_up, w_down, ln2_g, ln2_b):
    B, S, D = x.shape
    N = B * S
    x2 = x.reshape(N, D)
    tm_in = min(512, S)
    tm = min(256, S)
    lm = min(256, S)

    mod3 = _ada(c, w_ada[l], b_ada[l]).reshape(B, 6, D)

    w_p = _permute_w_in(w_in[l]).astype(BF16)
    wa_pad = jnp.zeros((LANES, G_KW), F32).at[SM_A:SM_A + G_RANK].set(w_gla_a[l]).astype(BF16)
    bg = (jnp.zeros((2 * SUBLANES, 1), F32).at[0:M_HEADS, 0].set(b_igate[l])
          .at[SUBLANES:SUBLANES + M_HEADS, 0].set(b_fgate[l]))
    oa, la, g3 = _inproj(x2, mod3, w_p, w_conv[l], b_conv[l].reshape(1, -1), wa_pad,
                         b_gla_a[l].reshape(1, -1), bg, S=S, tm=tm_in, lm=lm)

    u_tri = jnp.asarray(np.triu(np.ones((lm, lm), np.float32)))
    hm = _mlstm(oa, g3, u_tri, mlstm_norm_g[l].reshape(1, -1), B=B, S=S, L=lm)
    w3_np, mk_np = _gla_consts()
    hg = _gla(oa, la, jnp.asarray(w3_np, BF16), jnp.asarray(mk_np), gla_norm_g[l].reshape(1, -1), B=B, S=S)

    wr = (jnp.zeros((D, LANES), F32).at[:, 0:N_GROUPS].set(w_route_group[l])
          .at[:, SUBLANES:SUBLANES + N_EXP].set(w_route_expert[l]))
    br = (jnp.zeros((1, LANES), F32).at[0, 0:N_GROUPS].set(b_route_group[l])
          .at[0, SUBLANES:SUBLANES + N_EXP].set(b_route_expert[l]))
    x1, u2, rrow = _outproj(hm, hg, w_out[l].astype(BF16), x2, mod3, ln1_g[l].reshape(1, -1),
                            ln1_b[l].reshape(1, -1), wr, br, S=S, tm=tm)

    u_cnt = jnp.asarray(np.triu(np.ones((tm, tm), np.float32)), BF16)
    ltri = jnp.asarray(np.tril(np.ones((N_EXP, N_EXP), np.float32), -1))
    srow, col, gd3, meta = _route(rrow, u_cnt, ltri, TM=FFN_TM)
    gd = gd3.reshape(N // tm + 2, LANES)
    n_tiles = _ffn_tiles(N, tm)
    te, nv = meta[0, :n_tiles], meta[1, 0:1]

    xs = _dispatch(gd, srow, u2, n_tiles=n_tiles, TM=FFN_TM)
    ys = _ffn(te, nv, xs, w_gate[l], w_up[l], w_down[l], TM=FFN_TM)
    out = _combine(gd, ys, col, x1, mod3, ln2_g[l].reshape(1, -1), ln2_b[l].reshape(1, -1), S=S, tb=tm)
    return out.reshape(B, S, D)


def kernel(x, c, w_ada, b_ada, w_in, w_conv, b_conv, b_igate, b_fgate, mlstm_norm_g, w_gla_a, b_gla_a,
           gla_norm_g, w_out, ln1_g, ln1_b, w_route_group, b_route_group, w_route_expert, b_route_expert,
           w_gate, w_up, w_down, ln2_g, ln2_b):
    for l in range(DEPTH):
        x = _layer(x, c, l, w_ada, b_ada, w_in, w_conv, b_conv, b_igate, b_fgate, mlstm_norm_g, w_gla_a,
                   b_gla_a, gla_norm_g, w_out, ln1_g, ln1_b, w_route_group, b_route_group, w_route_expert,
                   b_route_expert, w_gate, w_up, w_down, ln2_g, ln2_b)
    return x
```

```python
import functools

import numpy as np
import jax
import jax.numpy as jnp
from jax import lax
from jax.experimental import pallas as pl
from jax.experimental.pallas import tpu as pltpu

F32 = jnp.float32
BF16 = jnp.bfloat16
HIGHEST = lax.Precision.HIGHEST

DEPTH = 1
M_HEADS = 4
M_HD = 128
M_W = M_HEADS * M_HD
CONV_W = 4
G_HEADS = 4
G_DK = 64
G_DV = 128
G_W = G_HEADS * G_DV
G_KW = G_HEADS * G_DK
G_RANK = 16
G_TAU = 16.0
G_CHUNK = 64
N_GROUPS = 4
E_PER_G = 8
N_EXP = N_GROUPS * E_PER_G
D_EXP = 512
ALPHA = (2 * DEPTH) ** 0.25
LN_EPS = 1e-5

LANES = 128
SUBLANES = 8
VMEM_LIMIT = 48 * 1024 * 1024

C_QK = 0
C_VO = 1024
C_GQK = 2048
C_GV = 2560
C_GG = 3072
C_SMALL = 3584
C_TOT = 3712
SM_I, SM_F, SM_A = 0, 8, 16

FFN_TM = 256
GRAN = SUBLANES
G_LAST = LANES - 1


def _cparams(n_axes=1):
    return pltpu.CompilerParams(dimension_semantics=("arbitrary",) * n_axes,
                                vmem_limit_bytes=VMEM_LIMIT)


def _sigmoid(x):
    return 1.0 / (1.0 + jnp.exp(-x))


def _log_sigmoid(x):
    return jnp.minimum(x, 0.0) - jnp.log(1.0 + jnp.exp(-jnp.abs(x)))


def _ada_kernel(c_ref, w_ref, b_ref, o_ref):
    c = c_ref[...]
    ca = c * _sigmoid(c)
    o_ref[...] = jnp.dot(ca, w_ref[...], preferred_element_type=F32, precision=HIGHEST) + b_ref[...]


def _ada(c, w, b):
    B, D = c.shape
    n_out = w.shape[1]
    tn = 1024
    return pl.pallas_call(
        _ada_kernel,
        grid=(n_out // tn,),
        in_specs=[pl.BlockSpec((B, D), lambda j: (0, 0)),
                  pl.BlockSpec((D, tn), lambda j: (0, j)),
                  pl.BlockSpec((1, tn), lambda j: (0, j))],
        out_specs=pl.BlockSpec((B, tn), lambda j: (0, j)),
        out_shape=jax.ShapeDtypeStruct((B, n_out), F32),
        compiler_params=_cparams(),
        name="ada",
    )(c, w, b.reshape(1, n_out))


def _inproj_kernel(x_ref, mod_ref, w_ref, wc_ref, bc_ref, wa_ref, ba_ref, bg_ref,
                   oa_ref, la_ref, g_ref, halo_ref, *, tm, tpb, lm):
    i = pl.program_id(0)

    @pl.when(i % tpb == 0)
    def _():
        halo_ref[...] = jnp.zeros_like(halo_ref)

    mod = mod_ref[0]
    u = (x_ref[...] * (1.0 + mod[1:2, :]) + mod[0:1, :]).astype(BF16)

    p = jnp.dot(u, w_ref[:, C_QK:C_QK + 2 * M_W], preferred_element_type=F32)
    ext = jnp.concatenate([halo_ref[...], p], axis=0)
    acc = bc_ref[...] + wc_ref[CONV_W - 1:CONV_W, :] * p
    for j in range(CONV_W - 1):
        sh = pltpu.roll(ext, CONV_W - 1 - j, 0)[SUBLANES:, :]
        acc = acc + wc_ref[j:j + 1, :] * sh
    halo_ref[...] = p[tm - SUBLANES:, :]
    qk = acc * _sigmoid(acc)
    oa_ref[:, C_QK:C_QK + M_W] = qk[:, :M_W].astype(BF16)
    oa_ref[:, C_QK + M_W:C_QK + 2 * M_W] = (qk[:, M_W:] * (M_HD ** -0.5)).astype(BF16)

    p = jnp.dot(u, w_ref[:, C_VO:C_VO + 2 * M_W], preferred_element_type=F32)
    oa_ref[:, C_VO:C_VO + 2 * M_W] = p.astype(BF16)

    p = jnp.dot(u, w_ref[:, C_GQK:C_GQK + G_KW], preferred_element_type=F32)
    oa_ref[:, C_GQK:C_GQK + G_KW] = (p * (G_DK ** -0.5)).astype(BF16)
    p = jnp.dot(u, w_ref[:, C_GQK + G_KW:C_SMALL], preferred_element_type=F32)
    oa_ref[:, C_GQK + G_KW:C_SMALL] = p.astype(BF16)

    ps = jnp.dot(u, w_ref[:, C_SMALL:C_TOT], preferred_element_type=F32)
    la = jnp.dot(ps.astype(BF16), wa_ref[...], preferred_element_type=F32) + ba_ref[...]
    la_ref[...] = _log_sigmoid(la) * (1.0 / G_TAU)
    pt = ps.T
    gi = pt[SM_I:SM_I + SUBLANES, :] + bg_ref[0:SUBLANES, :]
    gf = _log_sigmoid(pt[SM_F:SM_F + SUBLANES, :] + bg_ref[SUBLANES:2 * SUBLANES, :])
    for j in range(tm // lm):
        g_ref[j, 0:SUBLANES, :] = gi[:, j * lm:(j + 1) * lm]
        g_ref[j, SUBLANES:2 * SUBLANES, :] = gf[:, j * lm:(j + 1) * lm]


def _inproj(x2, mod3, w_p, w_conv, b_conv, wa_pad, b_gla, bg, *, S, tm, lm):
    N, D = x2.shape
    tpb = S // tm
    kern = functools.partial(_inproj_kernel, tm=tm, tpb=tpb, lm=lm)
    return pl.pallas_call(
        kern,
        grid=(N // tm,),
        in_specs=[pl.BlockSpec((tm, D), lambda i: (i, 0)),
                  pl.BlockSpec((1, 6, D), lambda i: (i // tpb, 0, 0)),
                  pl.BlockSpec((D, C_TOT), lambda i: (0, 0)),
                  pl.BlockSpec((CONV_W, 2 * M_W), lambda i: (0, 0)),
                  pl.BlockSpec((1, 2 * M_W), lambda i: (0, 0)),
                  pl.BlockSpec((LANES, G_KW), lambda i: (0, 0)),
                  pl.BlockSpec((1, G_KW), lambda i: (0, 0)),
                  pl.BlockSpec((2 * SUBLANES, 1), lambda i: (0, 0))],
        out_specs=[pl.BlockSpec((tm, C_SMALL), lambda i: (i, 0)),
                   pl.BlockSpec((tm, G_KW), lambda i: (i, 0)),
                   pl.BlockSpec((tm // lm, 2 * SUBLANES, lm), lambda i: (i, 0, 0))],
        out_shape=[jax.ShapeDtypeStruct((N, C_SMALL), BF16),
                   jax.ShapeDtypeStruct((N, G_KW), F32),
                   jax.ShapeDtypeStruct((N // lm, 2 * SUBLANES, lm), F32)],
        scratch_shapes=[pltpu.VMEM((SUBLANES, 2 * M_W), F32)],
        compiler_params=_cparams(),
        name="inproj",
    )(x2, mod3, w_p, w_conv, b_conv, wa_pad, b_gla, bg)


def _mlstm_kernel(qk_ref, vo_ref, g_ref, u_ref, gain_ref, out_ref, c_ref, *, L, NC):
    c_ref[...] = jnp.zeros_like(c_ref)
    lane = lax.broadcasted_iota(jnp.int32, (SUBLANES, L), 1)
    tril = (lax.broadcasted_iota(jnp.int32, (L, L), 0) >= lax.broadcasted_iota(jnp.int32, (L, L), 1))
    ones_v = jnp.ones((L, M_HD), BF16)
    zpad = jnp.zeros((LANES - 5 * SUBLANES, L), F32)

    def chunk(c, m_prev):
        r0 = pl.multiple_of(c * L, L)
        gi = g_ref[c, 0:SUBLANES, :]
        gf = g_ref[c, SUBLANES:2 * SUBLANES, :]
        b = jnp.dot(gf, u_ref[...], preferred_element_type=F32, precision=HIGHEST)
        a = gi - b
        G = a
        s = 1
        while s < L:
            sh = pltpu.roll(G, s, 1)
            G = jnp.maximum(G, jnp.where(lane >= s, sh, -jnp.inf))
            s *= 2
        M = jnp.maximum(G, m_prev)
        mt = b + M
        ML = M[:, L - 1:L]
        m_new = b[:, L - 1:L] + ML
        Z = jnp.concatenate([M, mt, a, jnp.broadcast_to(m_prev, (SUBLANES, L)),
                             jnp.broadcast_to(ML, (SUBLANES, L)), zpad], axis=0)
        Zt = Z.T
        dec = jnp.exp(m_prev - ML)
        for h in range(M_HEADS):
            M_col = Zt[:, h:h + 1]
            mt_col = Zt[:, SUBLANES + h:SUBLANES + h + 1]
            a_col = Zt[:, 2 * SUBLANES + h:2 * SUBLANES + h + 1]
            mp_col = Zt[:, 3 * SUBLANES + h:3 * SUBLANES + h + 1]
            ML_col = Zt[:, 4 * SUBLANES + h:4 * SUBLANES + h + 1]
            hs = slice(h * M_HD, (h + 1) * M_HD)
            hs2 = slice(M_W + h * M_HD, M_W + (h + 1) * M_HD)
            q = qk_ref[pl.ds(r0, L), hs]
            k = qk_ref[pl.ds(r0, L), hs2]
            v = vo_ref[pl.ds(r0, L), hs]
            og = vo_ref[pl.ds(r0, L), hs2]
            dm = jnp.exp(jnp.where(tril, a[h:h + 1, :] - M_col, -jnp.inf))
            sc = lax.dot_general(q, k, (((1,), (1,)), ((), ())), preferred_element_type=F32)
            pm = (sc * dm).astype(BF16)
            vext = jnp.concatenate([v, ones_v], axis=1)
            cst = c_ref[h]
            nd = (jnp.dot(pm, vext, preferred_element_type=F32)
                  + jnp.exp(mp_col - M_col) * jnp.dot(q, cst.astype(BF16), preferred_element_type=F32))
            hh = nd[:, :M_HD] / jnp.maximum(jnp.abs(nd[:, M_HD:]), jnp.exp(-mt_col))
            hh = _sigmoid(og.astype(F32)) * hh
            hn = hh * lax.rsqrt(jnp.mean(hh * hh, axis=-1, keepdims=True) + LN_EPS)
            out_ref[pl.ds(r0, L), hs] = (hn * gain_ref[:, hs]).astype(BF16)
            kw = (jnp.exp(a_col - ML_col) * k.astype(F32)).astype(BF16)
            upd = lax.dot_general(kw, vext, (((0,), (0,)), ((), ())), preferred_element_type=F32)
            c_ref[h] = dec[h:h + 1, :] * cst + upd
        return m_new

    lax.fori_loop(0, NC, chunk, jnp.zeros((SUBLANES, 1), F32), unroll=2)


def _mlstm(oa, g3, u_tri, gain, *, B, S, L):
    N = oa.shape[0]
    NC = S // L
    kern = functools.partial(_mlstm_kernel, L=L, NC=NC)
    return pl.pallas_call(
        kern,
        grid=(B,),
        in_specs=[pl.BlockSpec((S, 2 * M_W), lambda b: (b, C_QK // (2 * M_W))),
                  pl.BlockSpec((S, 2 * M_W), lambda b: (b, C_VO // (2 * M_W))),
                  pl.BlockSpec((NC, 2 * SUBLANES, L), lambda b: (b, 0, 0)),
                  pl.BlockSpec((L, L), lambda b: (0, 0)),
                  pl.BlockSpec((1, M_W), lambda b: (0, 0))],
        out_specs=pl.BlockSpec((S, M_W), lambda b: (b, 0)),
        out_shape=jax.ShapeDtypeStruct((N, M_W), BF16),
        scratch_shapes=[pltpu.VMEM((M_HEADS, M_HD, 2 * M_HD), F32)],
        compiler_params=_cparams(),
        name="mlstm",
    )(oa, oa, g3, u_tri, gain)


_G_LEVELS = 6
_G_XROW = 2 * G_CHUNK + SUBLANES


def _gla_consts():
    L = G_CHUNK
    t = np.arange(L)
    blocks = [(t[None, :] <= t[:, None]).astype(np.float32),
              (t[None, :] > t[:, None]).astype(np.float32),
              np.ones((SUBLANES, L), np.float32)]
    masks = [np.eye(L, dtype=np.float32)]
    m = 1
    while m < L:
        wl = np.zeros((L, L), np.float32)
        for r in range(L):
            r0 = (r // (2 * m)) * 2 * m + m
            if r % (2 * m) >= m:
                wl[r, r0:r + 1] = 1.0
            else:
                wl[r, r + 1:r0] = 1.0
        blocks.append(wl)
        tt, ss = t[:, None], t[None, :]
        masks.append(((tt // (2 * m) == ss // (2 * m)) & (tt % (2 * m) >= m)
                      & (ss % (2 * m) < m)).astype(np.float32))
        m *= 2
    w = np.concatenate(blocks, axis=0)
    w3 = np.concatenate([w, w, w], axis=1)
    mk = np.stack([np.concatenate([x, x], axis=0) for x in masks])
    return w3, mk


def _gla_kernel(qk_ref, v_ref, gg_ref, la_ref, w3_ref, mk_ref, gain_ref, out_ref, st_ref, *, NC):
    L = G_CHUNK
    st_ref[...] = jnp.zeros_like(st_ref)
    lane_lo = lax.broadcasted_iota(jnp.int32, (L, LANES), 1) < G_DK
    br = lax.broadcasted_iota(jnp.int32, (2 * G_DV, LANES), 0) < G_DV
    bl = lax.broadcasted_iota(jnp.int32, (2 * G_DV, LANES), 1) < G_DK
    bmask = br == bl

    def chunk(c, carry):
        r0 = pl.multiple_of(c * L, L)
        la = la_ref[pl.ds(r0, L), :]
        hi = la.astype(BF16)
        r1 = la - hi.astype(F32)
        mid = r1.astype(BF16)
        lo = (r1 - mid.astype(F32)).astype(BF16)
        stk = jnp.concatenate([hi, mid, lo], axis=0)
        X = jnp.exp(jnp.dot(w3_ref[...], stk, preferred_element_type=F32))
        q = qk_ref[pl.ds(r0, L), 0:G_KW].astype(F32)
        k = qk_ref[pl.ds(r0, L), G_KW:2 * G_KW].astype(F32)
        outs = []
        for p in range(2):
            ls = slice(LANES * p, LANES * (p + 1))
            qp, kp = q[:, ls], k[:, ls]
            A = jnp.zeros((2 * L, L), F32)
            for lev in range(_G_LEVELS + 1):
                if lev == 0:
                    qt, kt = qp, kp
                else:
                    xl = X[_G_XROW + L * (lev - 1):_G_XROW + L * lev, ls]
                    qt, kt = qp * xl, kp * xl
                qs = jnp.concatenate([jnp.where(lane_lo, qt, 0.0), jnp.where(lane_lo, 0.0, qt)],
                                     axis=0).astype(BF16)
                sc = lax.dot_general(qs, kt.astype(BF16), (((1,), (1,)), ((), ())),
                                     preferred_element_type=F32)
                A = A + sc * mk_ref[lev]
            Ab = A.astype(BF16)
            vp = v_ref[pl.ds(r0, L), 2 * G_DV * p:2 * G_DV * (p + 1)]
            oi0 = jnp.dot(Ab[0:L], vp[:, 0:G_DV], preferred_element_type=F32)
            oi1 = jnp.dot(Ab[L:2 * L], vp[:, G_DV:2 * G_DV], preferred_element_type=F32)
            st = st_ref[p]
            qc = (qp * X[0:L, ls]).astype(BF16)
            o_inter = lax.dot_general(qc, st.astype(BF16), (((1,), (1,)), ((), ())),
                                      preferred_element_type=F32)
            outs.append(o_inter + jnp.concatenate([oi0, oi1], axis=1))
            kc = (kp * X[L:2 * L, ls]).astype(BF16)
            upd = lax.dot_general(vp, kc, (((0,), (0,)), ((), ())), preferred_element_type=F32)
            dec = X[2 * L:2 * L + 1, ls]
            st_ref[p] = jnp.where(bmask, dec * st + upd, 0.0)
        gg = gg_ref[pl.ds(r0, L), :].astype(F32)
        gate = gg * _sigmoid(gg)
        for p in range(2):
            for hh in range(2):
                h = 2 * p + hh
                o = outs[p][:, G_DV * hh:G_DV * (hh + 1)]
                hn = o * lax.rsqrt(jnp.mean(o * o, axis=-1, keepdims=True) + LN_EPS)
                hs = slice(G_DV * h, G_DV * (h + 1))
                out_ref[pl.ds(r0, L), hs] = (hn * gain_ref[:, hs] * gate[:, hs]).astype(BF16)
        return carry

    lax.fori_loop(0, NC, chunk, 0, unroll=2)


def _gla(oa, la, w3, mk, gain, *, B, S):
    N = oa.shape[0]
    NC = S // G_CHUNK
    kern = functools.partial(_gla_kernel, NC=NC)
    return pl.pallas_call(
        kern,
        grid=(B,),
        in_specs=[pl.BlockSpec((S, 2 * G_KW), lambda b: (b, C_GQK // (2 * G_KW))),
                  pl.BlockSpec((S, G_W), lambda b: (b, C_GV // G_W)),
                  pl.BlockSpec((S, G_W), lambda b: (b, C_GG // G_W)),
                  pl.BlockSpec((S, G_KW), lambda b: (b, 0)),
                  pl.BlockSpec(w3.shape, lambda b: (0, 0)),
                  pl.BlockSpec(mk.shape, lambda b: (0, 0, 0)),
                  pl.BlockSpec((1, G_W), lambda b: (0, 0))],
        out_specs=pl.BlockSpec((S, G_W), lambda b: (b, 0)),
        out_shape=jax.ShapeDtypeStruct((N, G_W), BF16),
        scratch_shapes=[pltpu.VMEM((2, 2 * G_DV, LANES), F32)],
        compiler_params=_cparams(),
        name="gla",
    )(oa, oa, oa, la, w3, mk, gain)


def _layer_norm(z, g, b):
    mu = jnp.mean(z, axis=-1, keepdims=True)
    zc = z - mu
    var = jnp.mean(zc * zc, axis=-1, keepdims=True)
    return zc * lax.rsqrt(var + LN_EPS) * g + b


def _outproj_kernel(hm_ref, hg_ref, w_ref, x_ref, mod_ref, g_ref, b_ref, wr_ref, br_ref,
                    x1_ref, u2_ref, rrow_ref, *, tm):
    mod = mod_ref[0]
    y = (jnp.dot(hm_ref[...], w_ref[0:M_W, :], preferred_element_type=F32)
         + jnp.dot(hg_ref[...], w_ref[M_W:M_W + G_W, :], preferred_element_type=F32))
    z = ALPHA * x_ref[...] + (1.0 + mod[2:3, :]) * y
    x1 = _layer_norm(z, g_ref[...], b_ref[...])
    x1_ref[...] = x1
    u2 = x1 * (1.0 + mod[4:5, :]) + mod[3:4, :]
    u2_ref[...] = u2.astype(BF16)

    u2h = u2.astype(BF16)
    u2l = (u2 - u2h.astype(F32)).astype(BF16)
    lh = jnp.dot(u2h, wr_ref[...], preferred_element_type=F32)
    ll = jnp.dot(u2l, wr_ref[:, 0:LANES], preferred_element_type=F32)
    logits = lh[:, 0:LANES] + lh[:, LANES:2 * LANES] + ll + br_ref[...]
    lt = logits.T
    row = lax.broadcasted_iota(jnp.int32, (SUBLANES, tm), 0)
    gl = jnp.where(row < N_GROUPS, lt[0:SUBLANES, :], -jnp.inf)
    gmax = jnp.max(gl, axis=0, keepdims=True)
    gsel = jnp.min(jnp.where(gl == gmax, row, SUBLANES), axis=0, keepdims=True)
    pg = 1.0 / jnp.sum(jnp.exp(gl - gmax), axis=0, keepdims=True)
    ein = jnp.zeros((SUBLANES, tm), F32)
    for g in range(N_GROUPS):
        ein = jnp.where(gsel == g, lt[SUBLANES * (g + 1):SUBLANES * (g + 2), :], ein)
    v1 = jnp.max(ein, axis=0, keepdims=True)
    i1 = jnp.min(jnp.where(ein == v1, row, SUBLANES), axis=0, keepdims=True)
    rest = jnp.where(row == i1, -jnp.inf, ein)
    v2 = jnp.max(rest, axis=0, keepdims=True)
    i2 = jnp.min(jnp.where(rest == v2, row, SUBLANES), axis=0, keepdims=True)
    t2 = jnp.exp(v2 - v1)
    p1 = 1.0 / (1.0 + t2)
    e0 = (gsel * E_PER_G + i1).astype(F32)
    e1 = (gsel * E_PER_G + i2).astype(F32)
    rrow_ref[0] = jnp.concatenate([e0, e1, pg * p1, pg * (t2 * p1), jnp.zeros((SUBLANES - 4, tm), F32)],
                                  axis=0)


def _outproj(hm, hg, w_out, x2, mod3, g, b, wr, br, *, S, tm):
    N, D = x2.shape
    tpb = S // tm
    kern = functools.partial(_outproj_kernel, tm=tm)
    return pl.pallas_call(
        kern,
        grid=(N // tm,),
        in_specs=[pl.BlockSpec((tm, M_W), lambda i: (i, 0)),
                  pl.BlockSpec((tm, G_W), lambda i: (i, 0)),
                  pl.BlockSpec((M_W + G_W, D), lambda i: (0, 0)),
                  pl.BlockSpec((tm, D), lambda i: (i, 0)),
                  pl.BlockSpec((1, 6, D), lambda i: (i // tpb, 0, 0)),
                  pl.BlockSpec((1, D), lambda i: (0, 0)),
                  pl.BlockSpec((1, D), lambda i: (0, 0)),
                  pl.BlockSpec((D, 2 * LANES), lambda i: (0, 0)),
                  pl.BlockSpec((1, LANES), lambda i: (0, 0))],
        out_specs=[pl.BlockSpec((tm, D), lambda i: (i, 0)),
                   pl.BlockSpec((tm, D), lambda i: (i, 0)),
                   pl.BlockSpec((1, SUBLANES, tm), lambda i: (i, 0, 0))],
        out_shape=[jax.ShapeDtypeStruct((N, D), F32),
                   jax.ShapeDtypeStruct((N, D), BF16),
                   jax.ShapeDtypeStruct((N // tm, SUBLANES, tm), F32)],
        compiler_params=_cparams(),
        name="outproj",
    )(hm, hg, w_out, x2, mod3, g, b, wr, br)


def _slots_per_tile(tb):
    worst = 2 * tb + N_EXP * (GRAN - 1)
    return -(-worst // LANES) * LANES


def _ffn_tiles(n_tok, tb):
    worst_rows = 2 * n_tok + (n_tok // tb) * N_EXP * (GRAN - 1)
    return -(-worst_rows // FFN_TM) + N_EXP


def _route_kernel(rr_ref, u_ref, lt_ref, srow_ref, col_ref, gd_ref, meta_ref, mg_ref, part_ref,
                  *, NT, tb, TM):
    iota_e = lax.broadcasted_iota(jnp.int32, (N_EXP, tb), 0).astype(F32)
    glane = lax.broadcasted_iota(jnp.int32, (N_EXP, LANES), 1).astype(F32)
    ltri = lt_ref[...]

    def prefix_e(col):
        return jnp.dot(ltri, jnp.broadcast_to(col, (N_EXP, LANES)),
                       preferred_element_type=F32, precision=HIGHEST)[:, 0:1]

    def p1(j, run8):
        r = rr_ref[j]
        oh0 = jnp.where(iota_e == r[0:1, :], 1.0, 0.0)
        oh1 = jnp.where(iota_e == r[1:2, :], 1.0, 0.0)
        cum0 = jnp.dot(oh0.astype(BF16), u_ref[...], preferred_element_type=F32)
        cum1 = jnp.dot(oh1.astype(BF16), u_ref[...], preferred_element_type=F32)
        c0 = jnp.sum(oh0, axis=1, keepdims=True)
        n8 = jnp.floor((c0 + jnp.sum(oh1, axis=1, keepdims=True) + (GRAN - 1.0)) * (1.0 / GRAN))
        lo8 = prefix_e(n8)
        s0 = jnp.sum(oh0 * (GRAN * lo8 + cum0 - 1.0), axis=0, keepdims=True)
        s1 = jnp.sum(oh1 * (GRAN * lo8 + c0 + cum1 - 1.0), axis=0, keepdims=True)
        info = jnp.concatenate([s0, s1, r[2:4, :], jnp.zeros((SUBLANES - 4, tb), F32)], axis=0)
        srow_ref[j] = info.astype(jnp.int32)
        col_ref[pl.ds(pl.multiple_of(j * tb, tb), tb), :] = jnp.concatenate(
            [info, jnp.zeros((LANES - SUBLANES, tb), F32)], axis=0).T
        mg = jnp.where((lo8 <= glane) & (glane < lo8 + n8), 1.0, 0.0)
        mg_ref[j] = mg
        part = jnp.sum(mg * (run8 + glane - lo8), axis=0, keepdims=True)
        gcnt = jnp.broadcast_to(jnp.sum(n8, axis=0, keepdims=True), (1, LANES))
        part_ref[j] = jnp.concatenate([part, gcnt, jnp.zeros((SUBLANES - 2, LANES), F32)], axis=0)
        return run8 + n8

    tot8 = lax.fori_loop(0, NT, p1, jnp.zeros((N_EXP, 1), F32))
    seg_t = jnp.floor((tot8 * GRAN + (TM - 1.0)) * (1.0 / TM))
    base_t = prefix_e(seg_t)
    base8 = base_t * (TM // GRAN)
    lane1 = lax.broadcasted_iota(jnp.int32, (1, LANES), 1)

    def p2(j, carry):
        pr = part_ref[j]
        dst = (pr[0:1, :] + jnp.sum(mg_ref[j] * base8, axis=0, keepdims=True)) * GRAN
        gd_ref[j] = jnp.where(lane1 == G_LAST, pr[1:2, :], dst).astype(jnp.int32)
        return carry

    lax.fori_loop(0, NT, p2, 0)
    eye = jnp.where(glane == lax.broadcasted_iota(jnp.int32, (N_EXP, LANES), 0).astype(F32), 1.0, 0.0)
    tail_row = jnp.sum(eye * ((base8 + tot8) * GRAN), axis=0, keepdims=True)
    tail_n8 = jnp.sum(eye * (seg_t * (TM // GRAN) - tot8), axis=0, keepdims=True)
    nv_l = jnp.broadcast_to(jnp.sum(seg_t, axis=0, keepdims=True), (1, LANES))
    gd_ref[NT] = jnp.where(lane1 == G_LAST, nv_l, tail_row).astype(jnp.int32)
    gd_ref[NT + 1] = tail_n8.astype(jnp.int32)
    ti = lax.broadcasted_iota(jnp.int32, (N_EXP, tb), 1).astype(F32)
    te = jnp.sum(jnp.where(base_t <= ti, 1.0, 0.0), axis=0, keepdims=True) - 1.0
    nv = jnp.broadcast_to(jnp.sum(seg_t, axis=0, keepdims=True), (1, tb))
    meta_ref[...] = jnp.concatenate([te, nv, jnp.zeros((SUBLANES - 2, tb), F32)],
                                    axis=0).astype(jnp.int32)


def _route(rrow, u_cnt, ltri, *, TM):
    NT, _, tb = rrow.shape
    kern = functools.partial(_route_kernel, NT=NT, tb=tb, TM=TM)
    full3 = lambda i: (0, 0, 0)
    return pl.pallas_call(
        kern,
        grid=(1,),
        in_specs=[pl.BlockSpec((NT, SUBLANES, tb), full3),
                  pl.BlockSpec((tb, tb), lambda i: (0, 0)),
                  pl.BlockSpec((N_EXP, N_EXP), lambda i: (0, 0))],
        out_specs=[pl.BlockSpec((NT, SUBLANES, tb), full3),
                   pl.BlockSpec((NT * tb, LANES), lambda i: (0, 0)),
                   pl.BlockSpec((NT + 2, 1, LANES), full3),
                   pl.BlockSpec((SUBLANES, tb), lambda i: (0, 0))],
        out_shape=[jax.ShapeDtypeStruct((NT, SUBLANES, tb), jnp.int32),
                   jax.ShapeDtypeStruct((NT * tb, LANES), F32),
                   jax.ShapeDtypeStruct((NT + 2, 1, LANES), jnp.int32),
                   jax.ShapeDtypeStruct((SUBLANES, tb), jnp.int32)],
        scratch_shapes=[pltpu.VMEM((NT, N_EXP, LANES), F32), pltpu.VMEM((NT, SUBLANES, LANES), F32)],
        compiler_params=_cparams(),
        name="route",
    )(rrow, u_cnt, ltri)


U32 = jnp.uint32
_HI_MASK = 0xFFFF0000


def _pack_halves(x):
    c = x.shape[1] // 2
    lo = lax.bitcast_convert_type(x[:, :c], U32)
    hi = lax.bitcast_convert_type(x[:, c:], U32)
    return (lo >> 16) | (hi & U32(_HI_MASK))


def _unpack_halves(w):
    lo = lax.bitcast_convert_type(w << 16, F32)
    hi = lax.bitcast_convert_type(w & U32(_HI_MASK), F32)
    return jnp.concatenate([lo, hi], axis=1).astype(BF16)


def _granule_copy(src_ref, src_row, dst_ref, dst_row, sem):
    return pltpu.make_async_copy(src_ref.at[pl.ds(src_row, GRAN), :], dst_ref.at[pl.ds(dst_row, GRAN), :], sem)


def _dispatch_kernel(gd_ref, srow_ref, u_ref, xs_ref, buf, zbuf, sems, *, NT, SL, TM, n_tiles):
    j = pl.program_id(0)
    slot = j % 2
    zsem = sems.at[2]

    def drain(tile, sl):
        def w(g, carry):
            _granule_copy(buf.at[sl], 0, xs_ref, 0, sems.at[sl]).wait()
            return carry
        lax.fori_loop(0, gd_ref[tile, G_LAST], w, 0)

    def tile_fill(t):
        return pltpu.make_async_copy(zbuf, xs_ref.at[pl.ds(pl.multiple_of(t * TM, TM), TM), :], zsem)

    def zero_fill(wait):
        for e in range(N_EXP):
            def zg(g, carry, e=e):
                cp = _granule_copy(zbuf, 0, xs_ref, pl.multiple_of(gd_ref[NT, e] + g * GRAN, GRAN), zsem)
                cp.wait() if wait else cp.start()
                return carry
            lax.fori_loop(0, gd_ref[NT + 1, e], zg, 0)

        def zt(t, carry):
            tile_fill(t).wait() if wait else tile_fill(t).start()
            return carry
        lax.fori_loop(gd_ref[NT, G_LAST], n_tiles, zt, 0)

    @pl.when(j == 0)
    def _():
        zbuf[...] = jnp.zeros_like(zbuf)
        zero_fill(False)

    @pl.when(j >= 2)
    def _():
        drain(j - 2, slot)

    s = srow_ref[0]
    rows = lax.broadcasted_iota(jnp.int32, (SL, s.shape[1]), 0)
    oh = jnp.where((rows == s[0:1, :]) | (rows == s[1:2, :]), 1.0, 0.0).astype(BF16)
    buf[slot] = _pack_halves(jnp.dot(oh, u_ref[...], preferred_element_type=F32))

    def issue(g, carry):
        _granule_copy(buf.at[slot], pl.multiple_of(g * GRAN, GRAN), xs_ref,
                      pl.multiple_of(gd_ref[j, g], GRAN), sems.at[slot]).start()
        return carry

    lax.fori_loop(0, gd_ref[j, G_LAST], issue, 0)

    @pl.when(j == NT - 1)
    def _():
        drain(j, slot)
        if NT > 1:
            drain(j - 1, 1 - slot)
        zero_fill(True)


def _dispatch(gd, srow, u2, *, n_tiles, TM):
    N, D = u2.shape
    NT, _, tb = srow.shape
    SL = _slots_per_tile(tb)
    n_rows = n_tiles * TM
    kern = functools.partial(_dispatch_kernel, NT=NT, SL=SL, TM=TM, n_tiles=n_tiles)
    grid_spec = pltpu.PrefetchScalarGridSpec(
        num_scalar_prefetch=1,
        grid=(NT,),
        in_specs=[pl.BlockSpec((1, SUBLANES, tb), lambda j, gd: (j, 0, 0)),
                  pl.BlockSpec((tb, D), lambda j, gd: (j, 0))],
        out_specs=pl.BlockSpec(memory_space=pl.ANY),
        scratch_shapes=[pltpu.VMEM((2, SL, D // 2), U32), pltpu.VMEM((TM, D // 2), U32),
                        pltpu.SemaphoreType.DMA((3,))],
    )
    return pl.pallas_call(
        kern,
        grid_spec=grid_spec,
        out_shape=jax.ShapeDtypeStruct((n_rows, D // 2), U32),
        compiler_params=_cparams(),
        name="dispatch",
    )(gd, srow, u2)


def _ffn_kernel(te_ref, nv_ref, xs_ref, wg_ref, wu_ref, wd_ref, o_ref, wgb, wub, wdb):
    i = pl.program_id(0)
    prev = te_ref[jnp.maximum(i - 1, 0)]

    @pl.when((i == 0) | (te_ref[i] != prev))
    def _():
        wgb[...] = wg_ref[0].astype(BF16)
        wub[...] = wu_ref[0].astype(BF16)
        wdb[...] = wd_ref[0].astype(BF16)

    @pl.when(i < nv_ref[0])
    def _():
        x = _unpack_halves(xs_ref[...])
        g = jnp.dot(x, wgb[...], preferred_element_type=F32)
        u = jnp.dot(x, wub[...], preferred_element_type=F32)
        h = (g * _sigmoid(g) * u).astype(BF16)
        y = jnp.dot(h, wdb[...], preferred_element_type=F32)
        o_ref[...] = _pack_halves(y.astype(BF16).astype(F32))

    @pl.when(i >= nv_ref[0])
    def _():
        o_ref[...] = jnp.zeros_like(o_ref)


def _ffn(te, nv, xs, wg, wu, wd, *, TM):
    P, DW = xs.shape
    D = 2 * DW
    n_tiles = P // TM
    grid_spec = pltpu.PrefetchScalarGridSpec(
        num_scalar_prefetch=2,
        grid=(n_tiles,),
        in_specs=[pl.BlockSpec((TM, DW), lambda i, te, nv: (jnp.maximum(jnp.minimum(i, nv[0] - 1), 0), 0)),
                  pl.BlockSpec((1, D, D_EXP), lambda i, te, nv: (te[i], 0, 0)),
                  pl.BlockSpec((1, D, D_EXP), lambda i, te, nv: (te[i], 0, 0)),
                  pl.BlockSpec((1, D_EXP, D), lambda i, te, nv: (te[i], 0, 0))],
        out_specs=pl.BlockSpec((TM, DW), lambda i, te, nv: (i, 0)),
        scratch_shapes=[pltpu.VMEM((D, D_EXP), BF16), pltpu.VMEM((D, D_EXP), BF16),
                        pltpu.VMEM((D_EXP, D), BF16)],
    )
    return pl.pallas_call(
        _ffn_kernel,
        grid_spec=grid_spec,
        out_shape=jax.ShapeDtypeStruct((P, DW), U32),
        compiler_params=_cparams(),
        name="ffn",
    )(te, nv, xs, wg, wu, wd)


def _combine_kernel(gd_ref, ys_ref, col_ref, x1_ref, mod_ref, g_ref, b_ref, o_ref, buf, sems, *, NT, SL):
    j = pl.program_id(0)
    slot = j % 2

    def fetch(tile, sl):
        def f(g, carry):
            _granule_copy(ys_ref, pl.multiple_of(gd_ref[tile, g], GRAN), buf.at[sl],
                          pl.multiple_of(g * GRAN, GRAN), sems.at[sl]).start()
            return carry
        lax.fori_loop(0, gd_ref[tile, G_LAST], f, 0)

    @pl.when(j == 0)
    def _():
        fetch(0, 0)

    @pl.when(j + 1 < NT)
    def _():
        fetch(j + 1, 1 - slot)

    ng = gd_ref[j, G_LAST]

    def w(g, carry):
        _granule_copy(ys_ref, 0, buf.at[slot], 0, sems.at[slot]).wait()
        return carry

    lax.fori_loop(0, ng, w, 0)

    rows = lax.broadcasted_iota(jnp.int32, (SL, 1), 0)
    yb = _unpack_halves(jnp.where(rows < ng * GRAN, buf[slot], U32(0)))
    col = col_ref[...]
    tb = col.shape[0]
    lanes = lax.broadcasted_iota(jnp.int32, (tb, SL), 1).astype(F32)
    wsel = (jnp.where(lanes == col[:, 0:1], col[:, 2:3], 0.0)
            + jnp.where(lanes == col[:, 1:2], col[:, 3:4], 0.0))
    whi = wsel.astype(BF16)
    wlo = (wsel - whi.astype(F32)).astype(BF16)
    y = jnp.dot(whi, yb, preferred_element_type=F32) + jnp.dot(wlo, yb, preferred_element_type=F32)
    mod = mod_ref[0]
    z = ALPHA * x1_ref[...] + (1.0 + mod[5:6, :]) * y
    o_ref[...] = _layer_norm(z, g_ref[...], b_ref[...])


def _combine(gd, ys, col, x1, mod3, g, b, *, S, tb):
    N, D = x1.shape
    NT = N // tb
    tpb = S // tb
    SL = _slots_per_tile(tb)
    kern = functools.partial(_combine_kernel, NT=NT, SL=SL)
    grid_spec = pltpu.PrefetchScalarGridSpec(
        num_scalar_prefetch=1,
        grid=(NT,),
        in_specs=[pl.BlockSpec(memory_space=pl.ANY),
                  pl.BlockSpec((tb, LANES), lambda j, gd: (j, 0)),
                  pl.BlockSpec((tb, D), lambda j, gd: (j, 0)),
                  pl.BlockSpec((1, 6, D), lambda j, gd: (j // tpb, 0, 0)),
                  pl.BlockSpec((1, D), lambda j, gd: (0, 0)),
                  pl.BlockSpec((1, D), lambda j, gd: (0, 0))],
        out_specs=pl.BlockSpec((tb, D), lambda j, gd: (j, 0)),
        scratch_shapes=[pltpu.VMEM((2, SL, D // 2), U32), pltpu.SemaphoreType.DMA((2,))],
    )
    return pl.pallas_call(
        kern,
        grid_spec=grid_spec,
        out_shape=jax.ShapeDtypeStruct((N, D), F32),
        compiler_params=_cparams(),
        name="combine",
    )(gd, ys, col, x1, mod3, g, b)


def _permute_w_in(w_in):
    D = w_in.shape[0]
    o = 0
    parts = {}
    for name, width in (("mq", M_W), ("mk", M_W), ("mv", M_W), ("mo", M_W), ("mi", M_HEADS), ("mf", M_HEADS),
                        ("gq", G_KW), ("gk", G_KW), ("gv", G_W), ("gg", G_W), ("ga", G_RANK)):
        parts[name] = w_in[:, o:o + width]
        o += width
    z = lambda n: jnp.zeros((D, n), w_in.dtype)
    small = jnp.concatenate([parts["mi"], z(SM_F - M_HEADS), parts["mf"], z(SM_A - SM_F - M_HEADS),
                             parts["ga"], z(LANES - SM_A - G_RANK)], axis=1)
    return jnp.concatenate([parts["mq"], parts["mk"], parts["mv"], parts["mo"], parts["gq"], parts["gk"],
                            parts["gv"], parts["gg"], small], axis=1)


def _layer(x, c, l, w_ada, b_ada, w_in, w_conv, b_conv, b_igate, b_fgate, mlstm_norm_g, w_gla_a, b_gla_a,
           gla_norm_g, w_out, ln1_g, ln1_b, w_route_group, b_route_group, w_route_expert, b_route_expert,
           w_gate, w_up, w_down, ln2_g, ln2_b):
    B, S, D = x.shape
    N = B * S
    x2 = x.reshape(N, D)
    tm_in = min(512, S)
    tm = min(256, S)
    lm = min(256, S)

    mod3 = _ada(c, w_ada[l], b_ada[l]).reshape(B, 6, D)

    w_p = _permute_w_in(w_in[l]).astype(BF16)
    wa_pad = jnp.zeros((LANES, G_KW), F32).at[SM_A:SM_A + G_RANK].set(w_gla_a[l]).astype(BF16)
    bg = (jnp.zeros((2 * SUBLANES, 1), F32).at[0:M_HEADS, 0].set(b_igate[l])
          .at[SUBLANES:SUBLANES + M_HEADS, 0].set(b_fgate[l]))
    oa, la, g3 = _inproj(x2, mod3, w_p, w_conv[l], b_conv[l].reshape(1, -1), wa_pad,
                         b_gla_a[l].reshape(1, -1), bg, S=S, tm=tm_in, lm=lm)

    u_tri = jnp.asarray(np.triu(np.ones((lm, lm), np.float32)))
    hm = _mlstm(oa, g3, u_tri, mlstm_norm_g[l].reshape(1, -1), B=B, S=S, L=lm)
    w3_np, mk_np = _gla_consts()
    hg = _gla(oa, la, jnp.asarray(w3_np, BF16), jnp.asarray(mk_np), gla_norm_g[l].reshape(1, -1), B=B, S=S)

    wr = (jnp.zeros((D, LANES), F32).at[:, 0:N_GROUPS].set(w_route_group[l])
          .at[:, SUBLANES:SUBLANES + N_EXP].set(w_route_expert[l]))
    br = (jnp.zeros((1, LANES), F32).at[0, 0:N_GROUPS].set(b_route_group[l])
          .at[0, SUBLANES:SUBLANES + N_EXP].set(b_route_expert[l]))
    wr_hi = wr.astype(BF16)
    wr2 = jnp.concatenate([wr_hi, (wr - wr_hi.astype(F32)).astype(BF16)], axis=1)
    x1, u2, rrow = _outproj(hm, hg, w_out[l].astype(BF16), x2, mod3, ln1_g[l].reshape(1, -1),
                            ln1_b[l].reshape(1, -1), wr2, br, S=S, tm=tm)

    u_cnt = jnp.asarray(np.triu(np.ones((tm, tm), np.float32)), BF16)
    ltri = jnp.asarray(np.tril(np.ones((N_EXP, N_EXP), np.float32), -1))
    srow, col, gd3, meta = _route(rrow, u_cnt, ltri, TM=FFN_TM)
    gd = gd3.reshape(N // tm + 2, LANES)
    n_tiles = _ffn_tiles(N, tm)
    te, nv = meta[0, :n_tiles], meta[1, 0:1]

    xs = _dispatch(gd, srow, u2, n_tiles=n_tiles, TM=FFN_TM)
    ys = _ffn(te, nv, xs, w_gate[l], w_up[l], w_down[l], TM=FFN_TM)
    out = _combine(gd, ys, col, x1, mod3, ln2_g[l].reshape(1, -1), ln2_b[l].reshape(1, -1), S=S, tb=tm)
    return out.reshape(B, S, D)


def kernel(x, c, w_ada, b_ada, w_in, w_conv, b_conv, b_igate, b_fgate, mlstm_norm_g, w_gla_a, b_gla_a,
           gla_norm_g, w_out, ln1_g, ln1_b, w_route_group, b_route_group, w_route_expert, b_route_expert,
           w_gate, w_up, w_down, ln2_g, ln2_b):
    for l in range(DEPTH):
        x = _layer(x, c, l, w_ada, b_ada, w_in, w_conv, b_conv, b_igate, b_fgate, mlstm_norm_g, w_gla_a,
                   b_gla_a, gla_norm_g, w_out, ln1_g, ln1_b, w_route_group, b_route_group, w_route_expert,
                   b_route_expert, w_gate, w_up, w_down, ln2_g, ln2_b)
    return x
```

```python
import functools

import numpy as np
import jax
import jax.numpy as jnp
from jax import lax
from jax.experimental import pallas as pl
from jax.experimental.pallas import tpu as pltpu

F32 = jnp.float32
BF16 = jnp.bfloat16
HIGHEST = lax.Precision.HIGHEST

DEPTH = 1
M_HEADS = 4
M_HD = 128
M_W = M_HEADS * M_HD
CONV_W = 4
G_HEADS = 4
G_DK = 64
G_DV = 128
G_W = G_HEADS * G_DV
G_KW = G_HEADS * G_DK
G_RANK = 16
G_TAU = 16.0
G_CHUNK = 64
N_GROUPS = 4
E_PER_G = 8
N_EXP = N_GROUPS * E_PER_G
D_EXP = 512
ALPHA = (2 * DEPTH) ** 0.25
LN_EPS = 1e-5

LANES = 128
SUBLANES = 8
VMEM_LIMIT = 48 * 1024 * 1024

C_QK = 0
C_VO = 1024
C_GQK = 2048
C_GV = 2560
C_GG = 3072
C_SMALL = 3584
C_TOT = 3712
SM_I, SM_F, SM_A = 0, 8, 16

FFN_TM = 256
GRAN = SUBLANES
G_LAST = LANES - 1


def _cparams(n_axes=1):
    return pltpu.CompilerParams(dimension_semantics=("arbitrary",) * n_axes,
                                vmem_limit_bytes=VMEM_LIMIT)


def _sigmoid(x):
    return 1.0 / (1.0 + jnp.exp(-x))


def _log_sigmoid(x):
    return jnp.minimum(x, 0.0) - jnp.log(1.0 + jnp.exp(-jnp.abs(x)))


def _ada_kernel(c_ref, w_ref, b_ref, o_ref):
    c = c_ref[...]
    ca = c * _sigmoid(c)
    o_ref[...] = jnp.dot(ca, w_ref[...], preferred_element_type=F32, precision=HIGHEST) + b_ref[...]


def _ada(c, w, b):
    B, D = c.shape
    n_out = w.shape[1]
    tn = 1024
    return pl.pallas_call(
        _ada_kernel,
        grid=(n_out // tn,),
        in_specs=[pl.BlockSpec((B, D), lambda j: (0, 0)),
                  pl.BlockSpec((D, tn), lambda j: (0, j)),
                  pl.BlockSpec((1, tn), lambda j: (0, j))],
        out_specs=pl.BlockSpec((B, tn), lambda j: (0, j)),
        out_shape=jax.ShapeDtypeStruct((B, n_out), F32),
        compiler_params=_cparams(),
        name="ada",
    )(c, w, b.reshape(1, n_out))


def _inproj_kernel(x_ref, mod_ref, w_ref, wc_ref, bc_ref, wa_ref, ba_ref, bg_ref,
                   oa_ref, la_ref, g_ref, halo_ref, *, tm, tpb, lm):
    i = pl.program_id(0)

    @pl.when(i % tpb == 0)
    def _():
        halo_ref[...] = jnp.zeros_like(halo_ref)

    mod = mod_ref[0]
    u = (x_ref[...] * (1.0 + mod[1:2, :]) + mod[0:1, :]).astype(BF16)

    p = jnp.dot(u, w_ref[:, C_QK:C_QK + 2 * M_W], preferred_element_type=F32)
    ext = jnp.concatenate([halo_ref[...], p], axis=0)
    acc = bc_ref[...] + wc_ref[CONV_W - 1:CONV_W, :] * p
    for j in range(CONV_W - 1):
        sh = pltpu.roll(ext, CONV_W - 1 - j, 0)[SUBLANES:, :]
        acc = acc + wc_ref[j:j + 1, :] * sh
    halo_ref[...] = p[tm - SUBLANES:, :]
    qk = acc * _sigmoid(acc)
    oa_ref[:, C_QK:C_QK + M_W] = qk[:, :M_W].astype(BF16)
    oa_ref[:, C_QK + M_W:C_QK + 2 * M_W] = (qk[:, M_W:] * (M_HD ** -0.5)).astype(BF16)

    p = jnp.dot(u, w_ref[:, C_VO:C_VO + 2 * M_W], preferred_element_type=F32)
    oa_ref[:, C_VO:C_VO + 2 * M_W] = p.astype(BF16)

    p = jnp.dot(u, w_ref[:, C_GQK:C_GQK + G_KW], preferred_element_type=F32)
    oa_ref[:, C_GQK:C_GQK + G_KW] = (p * (G_DK ** -0.5)).astype(BF16)
    p = jnp.dot(u, w_ref[:, C_GQK + G_KW:C_SMALL], preferred_element_type=F32)
    oa_ref[:, C_GQK + G_KW:C_SMALL] = p.astype(BF16)

    ps = jnp.dot(u, w_ref[:, C_SMALL:C_TOT], preferred_element_type=F32)
    la = jnp.dot(ps.astype(BF16), wa_ref[...], preferred_element_type=F32) + ba_ref[...]
    la_ref[...] = _log_sigmoid(la) * (1.0 / G_TAU)
    pt = ps.T
    gi = pt[SM_I:SM_I + SUBLANES, :] + bg_ref[0:SUBLANES, :]
    gf = _log_sigmoid(pt[SM_F:SM_F + SUBLANES, :] + bg_ref[SUBLANES:2 * SUBLANES, :])
    for j in range(tm // lm):
        g_ref[j, 0:SUBLANES, :] = gi[:, j * lm:(j + 1) * lm]
        g_ref[j, SUBLANES:2 * SUBLANES, :] = gf[:, j * lm:(j + 1) * lm]


def _inproj(x2, mod3, w_p, w_conv, b_conv, wa_pad, b_gla, bg, *, S, tm, lm):
    N, D = x2.shape
    tpb = S // tm
    kern = functools.partial(_inproj_kernel, tm=tm, tpb=tpb, lm=lm)
    return pl.pallas_call(
        kern,
        grid=(N // tm,),
        in_specs=[pl.BlockSpec((tm, D), lambda i: (i, 0)),
                  pl.BlockSpec((1, 6, D), lambda i: (i // tpb, 0, 0)),
                  pl.BlockSpec((D, C_TOT), lambda i: (0, 0)),
                  pl.BlockSpec((CONV_W, 2 * M_W), lambda i: (0, 0)),
                  pl.BlockSpec((1, 2 * M_W), lambda i: (0, 0)),
                  pl.BlockSpec((LANES, G_KW), lambda i: (0, 0)),
                  pl.BlockSpec((1, G_KW), lambda i: (0, 0)),
                  pl.BlockSpec((2 * SUBLANES, 1), lambda i: (0, 0))],
        out_specs=[pl.BlockSpec((tm, C_SMALL), lambda i: (i, 0)),
                   pl.BlockSpec((tm, G_KW), lambda i: (i, 0)),
                   pl.BlockSpec((tm // lm, 2 * SUBLANES, lm), lambda i: (i, 0, 0))],
        out_shape=[jax.ShapeDtypeStruct((N, C_SMALL), BF16),
                   jax.ShapeDtypeStruct((N, G_KW), F32),
                   jax.ShapeDtypeStruct((N // lm, 2 * SUBLANES, lm), F32)],
        scratch_shapes=[pltpu.VMEM((SUBLANES, 2 * M_W), F32)],
        compiler_params=_cparams(),
        name="inproj",
    )(x2, mod3, w_p, w_conv, b_conv, wa_pad, b_gla, bg)


def _mlstm_kernel(qk_ref, vo_ref, g_ref, u_ref, gain_ref, out_ref, c_ref, zt_ref, a_ref, dec_ref, *, L, NC):
    c_ref[...] = jnp.zeros_like(c_ref)
    lane = lax.broadcasted_iota(jnp.int32, (SUBLANES, L), 1)
    tril = (lax.broadcasted_iota(jnp.int32, (L, L), 0) >= lax.broadcasted_iota(jnp.int32, (L, L), 1))
    ones_v = jnp.ones((L, M_HD), BF16)
    zpad = jnp.zeros((LANES - 4 * SUBLANES, L), F32)

    bs, gs = [], []
    for c in range(NC):
        b = jnp.dot(g_ref[c, SUBLANES:2 * SUBLANES, :], u_ref[...], preferred_element_type=F32,
                    precision=HIGHEST)
        a = g_ref[c, 0:SUBLANES, :] - b
        a_ref[c] = a
        G = a
        s = 1
        while s < L:
            G = jnp.maximum(G, jnp.where(lane >= s, pltpu.roll(G, s, 1), -jnp.inf))
            s *= 2
        bs.append(b)
        gs.append(G)
    m_prev = jnp.zeros((SUBLANES, 1), F32)
    for c in range(NC):
        M = jnp.maximum(gs[c], m_prev)
        ML = M[:, L - 1:L]
        Z = jnp.concatenate([M, jnp.exp(m_prev - M), jnp.exp(-(bs[c] + M)), jnp.exp(a_ref[c] - ML), zpad],
                            axis=0)
        zt_ref[c] = Z.T
        dec_ref[c] = jnp.broadcast_to(jnp.exp(m_prev - ML), (SUBLANES, 2 * M_HD))
        m_prev = bs[c][:, L - 1:L] + ML

    def chunk(c, carry):
        r0 = pl.multiple_of(c * L, L)
        Zt = zt_ref[c]
        a = a_ref[c]
        dec = dec_ref[c]
        rows = pl.ds(r0, L)
        heads = range(M_HEADS)
        hs = [slice(h * M_HD, (h + 1) * M_HD) for h in heads]
        hs2 = [slice(M_W + h * M_HD, M_W + (h + 1) * M_HD) for h in heads]
        q = [qk_ref[rows, hs[h]] for h in heads]
        k = [qk_ref[rows, hs2[h]] for h in heads]
        vext = [jnp.concatenate([vo_ref[rows, hs[h]], ones_v], axis=1) for h in heads]
        cst = [c_ref[h] for h in heads]
        sc = [lax.dot_general(q[h], k[h], (((1,), (1,)), ((), ())), preferred_element_type=F32) for h in heads]
        qc = [jnp.dot(q[h], cst[h].astype(BF16), preferred_element_type=F32) for h in heads]
        pm = [(sc[h] * jnp.exp(jnp.where(tril, a[h:h + 1, :] - Zt[:, h:h + 1], -jnp.inf))).astype(BF16)
              for h in heads]
        pv = [jnp.dot(pm[h], vext[h], preferred_element_type=F32) for h in heads]
        kw = [(Zt[:, 3 * SUBLANES + h:3 * SUBLANES + h + 1] * k[h].astype(F32)).astype(BF16) for h in heads]
        upd = [lax.dot_general(kw[h], vext[h], (((0,), (0,)), ((), ())), preferred_element_type=F32)
               for h in heads]
        for h in heads:
            c_ref[h] = dec[h:h + 1, :] * cst[h] + upd[h]
            nd = pv[h] + Zt[:, SUBLANES + h:SUBLANES + h + 1] * qc[h]
            hh = nd[:, :M_HD] / jnp.maximum(jnp.abs(nd[:, M_HD:]), Zt[:, 2 * SUBLANES + h:2 * SUBLANES + h + 1])
            hh = _sigmoid(vo_ref[rows, hs2[h]].astype(F32)) * hh
            hn = hh * lax.rsqrt(jnp.mean(hh * hh, axis=-1, keepdims=True) + LN_EPS)
            out_ref[rows, hs[h]] = (hn * gain_ref[:, hs[h]]).astype(BF16)
        return carry

    lax.fori_loop(0, NC, chunk, 0)


def _mlstm(oa, g3, u_tri, gain, *, B, S, L):
    N = oa.shape[0]
    NC = S // L
    kern = functools.partial(_mlstm_kernel, L=L, NC=NC)
    return pl.pallas_call(
        kern,
        grid=(B,),
        in_specs=[pl.BlockSpec((S, 2 * M_W), lambda b: (b, C_QK // (2 * M_W))),
                  pl.BlockSpec((S, 2 * M_W), lambda b: (b, C_VO // (2 * M_W))),
                  pl.BlockSpec((NC, 2 * SUBLANES, L), lambda b: (b, 0, 0)),
                  pl.BlockSpec((L, L), lambda b: (0, 0)),
                  pl.BlockSpec((1, M_W), lambda b: (0, 0))],
        out_specs=pl.BlockSpec((S, M_W), lambda b: (b, 0)),
        out_shape=jax.ShapeDtypeStruct((N, M_W), BF16),
        scratch_shapes=[pltpu.VMEM((M_HEADS, M_HD, 2 * M_HD), F32), pltpu.VMEM((NC, L, LANES), F32),
                        pltpu.VMEM((NC, SUBLANES, L), F32), pltpu.VMEM((NC, SUBLANES, 2 * M_HD), F32)],
        compiler_params=_cparams(),
        name="mlstm",
    )(oa, oa, g3, u_tri, gain)


_G_LEVELS = 6
_G_XROW = 2 * G_CHUNK + SUBLANES


def _gla_consts():
    L = G_CHUNK
    t = np.arange(L)
    blocks = [(t[None, :] <= t[:, None]).astype(np.float32),
              (t[None, :] > t[:, None]).astype(np.float32),
              np.ones((SUBLANES, L), np.float32)]
    masks = [np.eye(L, dtype=np.float32)]
    m = 1
    while m < L:
        wl = np.zeros((L, L), np.float32)
        for r in range(L):
            r0 = (r // (2 * m)) * 2 * m + m
            if r % (2 * m) >= m:
                wl[r, r0:r + 1] = 1.0
            else:
                wl[r, r + 1:r0] = 1.0
        blocks.append(wl)
        tt, ss = t[:, None], t[None, :]
        masks.append(((tt // (2 * m) == ss // (2 * m)) & (tt % (2 * m) >= m)
                      & (ss % (2 * m) < m)).astype(np.float32))
        m *= 2
    w = np.concatenate(blocks, axis=0)
    w3 = np.concatenate([w, w, w], axis=1)
    mk = np.stack([np.concatenate([x] * G_HEADS, axis=0) for x in masks])
    return w3, mk


def _gla_kernel(qk_ref, v_ref, gg_ref, la_ref, w3_ref, mk_ref, gain_ref, out_ref, st_ref, *, NC, S, nb):
    L = G_CHUNK
    st_ref[...] = jnp.zeros_like(st_ref)
    lane_head = lax.broadcasted_iota(jnp.int32, (L, G_KW), 1) // G_DK
    br = lax.broadcasted_iota(jnp.int32, (2 * G_DV, LANES), 0) < G_DV
    bl = lax.broadcasted_iota(jnp.int32, (2 * G_DV, LANES), 1) < G_DK
    bmask = br == bl
    nt = (((1,), (1,)), ((), ()))
    tn = (((0,), (0,)), ((), ()))

    def chunk(c, carry):
        rows = [pl.ds(pl.multiple_of(bi * S + c * L, L), L) for bi in range(nb)]
        X, q, k = [], [], []
        for bi in range(nb):
            la = la_ref[rows[bi], :]
            hi = la.astype(BF16)
            r1 = la - hi.astype(F32)
            mid = r1.astype(BF16)
            lo = (r1 - mid.astype(F32)).astype(BF16)
            stk = jnp.concatenate([hi, mid, lo], axis=0)
            X.append(jnp.exp(jnp.dot(w3_ref[...], stk, preferred_element_type=F32)))
            q.append(qk_ref[rows[bi], 0:G_KW].astype(F32))
            k.append(qk_ref[rows[bi], G_KW:2 * G_KW].astype(F32))

        sc = [[None] * (_G_LEVELS + 1) for _ in range(nb)]
        for lev in range(_G_LEVELS + 1):
            for bi in range(nb):
                if lev == 0:
                    qt, kt = q[bi], k[bi]
                else:
                    xl = X[bi][_G_XROW + L * (lev - 1):_G_XROW + L * lev, :]
                    qt, kt = q[bi] * xl, k[bi] * xl
                q4 = jnp.concatenate([jnp.where(lane_head == h, qt, 0.0) for h in range(G_HEADS)],
                                     axis=0).astype(BF16)
                sc[bi][lev] = lax.dot_general(q4, kt.astype(BF16), nt, preferred_element_type=F32)
        Ab = []
        for bi in range(nb):
            A = sc[bi][0] * mk_ref[0]
            for lev in range(1, _G_LEVELS + 1):
                A = A + sc[bi][lev] * mk_ref[lev]
            Ab.append(A.astype(BF16))

        for bi in range(nb):
            gg = gg_ref[rows[bi], :].astype(F32)
            gate = gg * _sigmoid(gg)
            for p in range(2):
                ls = slice(LANES * p, LANES * (p + 1))
                vp = v_ref[rows[bi], 2 * G_DV * p:2 * G_DV * (p + 1)]
                oi = [jnp.dot(Ab[bi][L * (2 * p + hh):L * (2 * p + hh + 1)],
                              vp[:, G_DV * hh:G_DV * (hh + 1)], preferred_element_type=F32)
                      for hh in range(2)]
                st = st_ref[bi, p]
                qc = (q[bi][:, ls] * X[bi][0:L, ls]).astype(BF16)
                o_inter = lax.dot_general(qc, st.astype(BF16), nt, preferred_element_type=F32)
                kc = (k[bi][:, ls] * X[bi][L:2 * L, ls]).astype(BF16)
                upd = lax.dot_general(vp, kc, tn, preferred_element_type=F32)
                dec = X[bi][2 * L:2 * L + 1, ls]
                st_ref[bi, p] = jnp.where(bmask, dec * st + upd, 0.0)
                for hh in range(2):
                    o = o_inter[:, G_DV * hh:G_DV * (hh + 1)] + oi[hh]
                    hn = o * lax.rsqrt(jnp.mean(o * o, axis=-1, keepdims=True) + LN_EPS)
                    hs = slice(G_DV * (2 * p + hh), G_DV * (2 * p + hh + 1))
                    out_ref[rows[bi], hs] = (hn * gain_ref[:, hs] * gate[:, hs]).astype(BF16)
        return carry

    lax.fori_loop(0, NC, chunk, 0)


def _gla(oa, la, w3, mk, gain, *, B, S, nb):
    N = oa.shape[0]
    NC = S // G_CHUNK
    kern = functools.partial(_gla_kernel, NC=NC, S=S, nb=nb)
    R = nb * S
    return pl.pallas_call(
        kern,
        grid=(B // nb,),
        in_specs=[pl.BlockSpec((R, 2 * G_KW), lambda b: (b, C_GQK // (2 * G_KW))),
                  pl.BlockSpec((R, G_W), lambda b: (b, C_GV // G_W)),
                  pl.BlockSpec((R, G_W), lambda b: (b, C_GG // G_W)),
                  pl.BlockSpec((R, G_KW), lambda b: (b, 0)),
                  pl.BlockSpec(w3.shape, lambda b: (0, 0)),
                  pl.BlockSpec(mk.shape, lambda b: (0, 0, 0)),
                  pl.BlockSpec((1, G_W), lambda b: (0, 0))],
        out_specs=pl.BlockSpec((R, G_W), lambda b: (b, 0)),
        out_shape=jax.ShapeDtypeStruct((N, G_W), BF16),
        scratch_shapes=[pltpu.VMEM((nb, 2, 2 * G_DV, LANES), F32)],
        compiler_params=_cparams(),
        name="gla",
    )(oa, oa, oa, la, w3, mk, gain)


def _layer_norm(z, g, b):
    mu = jnp.mean(z, axis=-1, keepdims=True)
    zc = z - mu
    var = jnp.mean(zc * zc, axis=-1, keepdims=True)
    return zc * lax.rsqrt(var + LN_EPS) * g + b


def _outproj_kernel(hm_ref, hg_ref, w_ref, x_ref, mod_ref, g_ref, b_ref, wr_ref, br_ref,
                    x1_ref, u2_ref, rrow_ref, *, tm):
    mod = mod_ref[0]
    y = (jnp.dot(hm_ref[...], w_ref[0:M_W, :], preferred_element_type=F32)
         + jnp.dot(hg_ref[...], w_ref[M_W:M_W + G_W, :], preferred_element_type=F32))
    z = ALPHA * x_ref[...] + (1.0 + mod[2:3, :]) * y
    x1 = _layer_norm(z, g_ref[...], b_ref[...])
    x1_ref[...] = x1
    u2 = x1 * (1.0 + mod[4:5, :]) + mod[3:4, :]
    u2_ref[...] = u2.astype(BF16)

    u2h = u2.astype(BF16)
    u2l = (u2 - u2h.astype(F32)).astype(BF16)
    lh = jnp.dot(u2h, wr_ref[...], preferred_element_type=F32)
    ll = jnp.dot(u2l, wr_ref[:, 0:LANES], preferred_element_type=F32)
    logits = lh[:, 0:LANES] + lh[:, LANES:2 * LANES] + ll + br_ref[...]
    lt = logits.T
    row = lax.broadcasted_iota(jnp.int32, (SUBLANES, tm), 0)
    gl = jnp.where(row < N_GROUPS, lt[0:SUBLANES, :], -jnp.inf)
    gmax = jnp.max(gl, axis=0, keepdims=True)
    gsel = jnp.min(jnp.where(gl == gmax, row, SUBLANES), axis=0, keepdims=True)
    pg = 1.0 / jnp.sum(jnp.exp(gl - gmax), axis=0, keepdims=True)
    ein = jnp.zeros((SUBLANES, tm), F32)
    for g in range(N_GROUPS):
        ein = jnp.where(gsel == g, lt[SUBLANES * (g + 1):SUBLANES * (g + 2), :], ein)
    v1 = jnp.max(ein, axis=0, keepdims=True)
    i1 = jnp.min(jnp.where(ein == v1, row, SUBLANES), axis=0, keepdims=True)
    rest = jnp.where(row == i1, -jnp.inf, ein)
    v2 = jnp.max(rest, axis=0, keepdims=True)
    i2 = jnp.min(jnp.where(rest == v2, row, SUBLANES), axis=0, keepdims=True)
    t2 = jnp.exp(v2 - v1)
    p1 = 1.0 / (1.0 + t2)
    e0 = (gsel * E_PER_G + i1).astype(F32)
    e1 = (gsel * E_PER_G + i2).astype(F32)
    rrow_ref[0] = jnp.concatenate([e0, e1, pg * p1, pg * (t2 * p1), jnp.zeros((SUBLANES - 4, tm), F32)],
                                  axis=0)


def _outproj(hm, hg, w_out, x2, mod3, g, b, wr, br, *, S, tm):
    N, D = x2.shape
    tpb = S // tm
    kern = functools.partial(_outproj_kernel, tm=tm)
    return pl.pallas_call(
        kern,
        grid=(N // tm,),
        in_specs=[pl.BlockSpec((tm, M_W), lambda i: (i, 0)),
                  pl.BlockSpec((tm, G_W), lambda i: (i, 0)),
                  pl.BlockSpec((M_W + G_W, D), lambda i: (0, 0)),
                  pl.BlockSpec((tm, D), lambda i: (i, 0)),
                  pl.BlockSpec((1, 6, D), lambda i: (i // tpb, 0, 0)),
                  pl.BlockSpec((1, D), lambda i: (0, 0)),
                  pl.BlockSpec((1, D), lambda i: (0, 0)),
                  pl.BlockSpec((D, 2 * LANES), lambda i: (0, 0)),
                  pl.BlockSpec((1, LANES), lambda i: (0, 0))],
        out_specs=[pl.BlockSpec((tm, D), lambda i: (i, 0)),
                   pl.BlockSpec((tm, D), lambda i: (i, 0)),
                   pl.BlockSpec((1, SUBLANES, tm), lambda i: (i, 0, 0))],
        out_shape=[jax.ShapeDtypeStruct((N, D), F32),
                   jax.ShapeDtypeStruct((N, D), BF16),
                   jax.ShapeDtypeStruct((N // tm, SUBLANES, tm), F32)],
        compiler_params=_cparams(),
        name="outproj",
    )(hm, hg, w_out, x2, mod3, g, b, wr, br)


def _slots_per_tile(tb):
    worst = 2 * tb + N_EXP * (GRAN - 1)
    return -(-worst // LANES) * LANES


def _ffn_tiles(n_tok, tb):
    worst_rows = 2 * n_tok + (n_tok // tb) * N_EXP * (GRAN - 1)
    return -(-worst_rows // FFN_TM) + N_EXP


def _route_kernel(rr_ref, u_ref, lt_ref, srow_ref, col_ref, gd_ref, meta_ref, mg_ref, part_ref,
                  *, NT, tb, TM):
    iota_e = lax.broadcasted_iota(jnp.int32, (N_EXP, tb), 0).astype(F32)
    glane = lax.broadcasted_iota(jnp.int32, (N_EXP, LANES), 1).astype(F32)
    ltri = lt_ref[...]

    def prefix_e(col):
        return jnp.dot(ltri, jnp.broadcast_to(col, (N_EXP, LANES)),
                       preferred_element_type=F32, precision=HIGHEST)[:, 0:1]

    def p1(j, run8):
        r = rr_ref[j]
        oh0 = jnp.where(iota_e == r[0:1, :], 1.0, 0.0)
        oh1 = jnp.where(iota_e == r[1:2, :], 1.0, 0.0)
        cum0 = jnp.dot(oh0.astype(BF16), u_ref[...], preferred_element_type=F32)
        cum1 = jnp.dot(oh1.astype(BF16), u_ref[...], preferred_element_type=F32)
        c0 = jnp.sum(oh0, axis=1, keepdims=True)
        n8 = jnp.floor((c0 + jnp.sum(oh1, axis=1, keepdims=True) + (GRAN - 1.0)) * (1.0 / GRAN))
        lo8 = prefix_e(n8)
        s0 = jnp.sum(oh0 * (GRAN * lo8 + cum0 - 1.0), axis=0, keepdims=True)
        s1 = jnp.sum(oh1 * (GRAN * lo8 + c0 + cum1 - 1.0), axis=0, keepdims=True)
        info = jnp.concatenate([s0, s1, r[2:4, :], jnp.zeros((SUBLANES - 4, tb), F32)], axis=0)
        srow_ref[j] = info.astype(jnp.int32)
        col_ref[pl.ds(pl.multiple_of(j * tb, tb), tb), :] = jnp.concatenate(
            [info, jnp.zeros((LANES - SUBLANES, tb), F32)], axis=0).T
        mg = jnp.where((lo8 <= glane) & (glane < lo8 + n8), 1.0, 0.0)
        mg_ref[j] = mg
        part = jnp.sum(mg * (run8 + glane - lo8), axis=0, keepdims=True)
        gcnt = jnp.broadcast_to(jnp.sum(n8, axis=0, keepdims=True), (1, LANES))
        part_ref[j] = jnp.concatenate([part, gcnt, jnp.zeros((SUBLANES - 2, LANES), F32)], axis=0)
        return run8 + n8

    tot8 = lax.fori_loop(0, NT, p1, jnp.zeros((N_EXP, 1), F32))
    seg_t = jnp.floor((tot8 * GRAN + (TM - 1.0)) * (1.0 / TM))
    base_t = prefix_e(seg_t)
    base8 = base_t * (TM // GRAN)
    lane1 = lax.broadcasted_iota(jnp.int32, (1, LANES), 1)

    def p2(j, carry):
        pr = part_ref[j]
        dst = (pr[0:1, :] + jnp.sum(mg_ref[j] * base8, axis=0, keepdims=True)) * GRAN
        gd_ref[j] = jnp.where(lane1 == G_LAST, pr[1:2, :], dst).astype(jnp.int32)
        return carry

    lax.fori_loop(0, NT, p2, 0)
    eye = jnp.where(glane == lax.broadcasted_iota(jnp.int32, (N_EXP, LANES), 0).astype(F32), 1.0, 0.0)
    tail_row = jnp.sum(eye * ((base8 + tot8) * GRAN), axis=0, keepdims=True)
    tail_n8 = jnp.sum(eye * (seg_t * (TM // GRAN) - tot8), axis=0, keepdims=True)
    nv_l = jnp.broadcast_to(jnp.sum(seg_t, axis=0, keepdims=True), (1, LANES))
    gd_ref[NT] = jnp.where(lane1 == G_LAST, nv_l, tail_row).astype(jnp.int32)
    gd_ref[NT + 1] = tail_n8.astype(jnp.int32)
    ti = lax.broadcasted_iota(jnp.int32, (N_EXP, tb), 1).astype(F32)
    te = jnp.sum(jnp.where(base_t <= ti, 1.0, 0.0), axis=0, keepdims=True) - 1.0
    nv = jnp.broadcast_to(jnp.sum(seg_t, axis=0, keepdims=True), (1, tb))
    meta_ref[...] = jnp.concatenate([te, nv, jnp.zeros((SUBLANES - 2, tb), F32)],
                                    axis=0).astype(jnp.int32)


def _route(rrow, u_cnt, ltri, *, TM):
    NT, _, tb = rrow.shape
    kern = functools.partial(_route_kernel, NT=NT, tb=tb, TM=TM)
    full3 = lambda i: (0, 0, 0)
    return pl.pallas_call(
        kern,
        grid=(1,),
        in_specs=[pl.BlockSpec((NT, SUBLANES, tb), full3),
                  pl.BlockSpec((tb, tb), lambda i: (0, 0)),
                  pl.BlockSpec((N_EXP, N_EXP), lambda i: (0, 0))],
        out_specs=[pl.BlockSpec((NT, SUBLANES, tb), full3),
                   pl.BlockSpec((NT * tb, LANES), lambda i: (0, 0)),
                   pl.BlockSpec((NT + 2, 1, LANES), full3),
                   pl.BlockSpec((SUBLANES, tb), lambda i: (0, 0))],
        out_shape=[jax.ShapeDtypeStruct((NT, SUBLANES, tb), jnp.int32),
                   jax.ShapeDtypeStruct((NT * tb, LANES), F32),
                   jax.ShapeDtypeStruct((NT + 2, 1, LANES), jnp.int32),
                   jax.ShapeDtypeStruct((SUBLANES, tb), jnp.int32)],
        scratch_shapes=[pltpu.VMEM((NT, N_EXP, LANES), F32), pltpu.VMEM((NT, SUBLANES, LANES), F32)],
        compiler_params=_cparams(),
        name="route",
    )(rrow, u_cnt, ltri)


U32 = jnp.uint32
_HI_MASK = 0xFFFF0000


def _pack_halves(x):
    c = x.shape[1] // 2
    lo = lax.bitcast_convert_type(x[:, :c], U32)
    hi = lax.bitcast_convert_type(x[:, c:], U32)
    return (lo >> 16) | (hi & U32(_HI_MASK))


def _unpack_halves(w):
    lo = lax.bitcast_convert_type(w << 16, F32)
    hi = lax.bitcast_convert_type(w & U32(_HI_MASK), F32)
    return jnp.concatenate([lo, hi], axis=1).astype(BF16)


def _granule_copy(src_ref, src_row, dst_ref, dst_row, sem):
    return pltpu.make_async_copy(src_ref.at[pl.ds(src_row, GRAN), :], dst_ref.at[pl.ds(dst_row, GRAN), :], sem)


def _wait_granules(n, src_ref, dst_ref, sem, n_max):
    b = 1
    while b <= n_max:
        @pl.when((n & b) != 0)
        def _(b=b):
            pltpu.make_async_copy(src_ref.at[pl.ds(0, b * GRAN), :], dst_ref.at[pl.ds(0, b * GRAN), :],
                                  sem).wait()
        b *= 2


def _dispatch_kernel(gd_ref, srow_ref, u_ref, xs_ref, buf, zbuf, sems, *, NT, SL, TM, n_tiles):
    j = pl.program_id(0)
    slot = j % 2
    zsem = sems.at[2]

    def drain(tile, sl):
        _wait_granules(gd_ref[tile, G_LAST], buf.at[sl], xs_ref, sems.at[sl], SL // GRAN)

    def tile_fill(t):
        return pltpu.make_async_copy(zbuf, xs_ref.at[pl.ds(pl.multiple_of(t * TM, TM), TM), :], zsem)

    def zero_fill(wait):
        for e in range(N_EXP):
            def zg(g, carry, e=e):
                cp = _granule_copy(zbuf, 0, xs_ref, pl.multiple_of(gd_ref[NT, e] + g * GRAN, GRAN), zsem)
                cp.wait() if wait else cp.start()
                return carry
            lax.fori_loop(0, gd_ref[NT + 1, e], zg, 0)

        def zt(t, carry):
            tile_fill(t).wait() if wait else tile_fill(t).start()
            return carry
        lax.fori_loop(gd_ref[NT, G_LAST], n_tiles, zt, 0)

    @pl.when(j == 0)
    def _():
        zbuf[...] = jnp.zeros_like(zbuf)
        zero_fill(False)

    @pl.when(j >= 2)
    def _():
        drain(j - 2, slot)

    s = srow_ref[0]
    rows = lax.broadcasted_iota(jnp.int32, (SL, s.shape[1]), 0)
    oh = jnp.where((rows == s[0:1, :]) | (rows == s[1:2, :]), 1.0, 0.0).astype(BF16)
    buf[slot] = _pack_halves(jnp.dot(oh, u_ref[...], preferred_element_type=F32))

    def issue(g, carry):
        _granule_copy(buf.at[slot], pl.multiple_of(g * GRAN, GRAN), xs_ref,
                      pl.multiple_of(gd_ref[j, g], GRAN), sems.at[slot]).start()
        return carry

    lax.fori_loop(0, gd_ref[j, G_LAST], issue, 0)

    @pl.when(j == NT - 1)
    def _():
        drain(j, slot)
        if NT > 1:
            drain(j - 1, 1 - slot)
        zero_fill(True)


def _dispatch(gd, srow, u2, *, n_tiles, TM):
    N, D = u2.shape
    NT, _, tb = srow.shape
    SL = _slots_per_tile(tb)
    n_rows = n_tiles * TM
    kern = functools.partial(_dispatch_kernel, NT=NT, SL=SL, TM=TM, n_tiles=n_tiles)
    grid_spec = pltpu.PrefetchScalarGridSpec(
        num_scalar_prefetch=1,
        grid=(NT,),
        in_specs=[pl.BlockSpec((1, SUBLANES, tb), lambda j, gd: (j, 0, 0)),
                  pl.BlockSpec((tb, D), lambda j, gd: (j, 0))],
        out_specs=pl.BlockSpec(memory_space=pl.ANY),
        scratch_shapes=[pltpu.VMEM((2, SL, D // 2), U32), pltpu.VMEM((TM, D // 2), U32),
                        pltpu.SemaphoreType.DMA((3,))],
    )
    return pl.pallas_call(
        kern,
        grid_spec=grid_spec,
        out_shape=jax.ShapeDtypeStruct((n_rows, D // 2), U32),
        compiler_params=_cparams(),
        name="dispatch",
    )(gd, srow, u2)


def _ffn_kernel(te_ref, nv_ref, xs_ref, wg_ref, wu_ref, wd_ref, o_ref, wgb, wub, wdb):
    i = pl.program_id(0)
    prev = te_ref[jnp.maximum(i - 1, 0)]

    @pl.when((i == 0) | (te_ref[i] != prev))
    def _():
        wgb[...] = wg_ref[0].astype(BF16)
        wub[...] = wu_ref[0].astype(BF16)
        wdb[...] = wd_ref[0].astype(BF16)

    @pl.when(i < nv_ref[0])
    def _():
        x = _unpack_halves(xs_ref[...])
        g = jnp.dot(x, wgb[...], preferred_element_type=F32)
        u = jnp.dot(x, wub[...], preferred_element_type=F32)
        h = (g * _sigmoid(g) * u).astype(BF16)
        y = jnp.dot(h, wdb[...], preferred_element_type=F32)
        o_ref[...] = _pack_halves(y.astype(BF16).astype(F32))

    @pl.when(i >= nv_ref[0])
    def _():
        o_ref[...] = jnp.zeros_like(o_ref)


def _ffn(te, nv, xs, wg, wu, wd, *, TM):
    P, DW = xs.shape
    D = 2 * DW
    n_tiles = P // TM
    grid_spec = pltpu.PrefetchScalarGridSpec(
        num_scalar_prefetch=2,
        grid=(n_tiles,),
        in_specs=[pl.BlockSpec((TM, DW), lambda i, te, nv: (jnp.maximum(jnp.minimum(i, nv[0] - 1), 0), 0)),
                  pl.BlockSpec((1, D, D_EXP), lambda i, te, nv: (te[i], 0, 0)),
                  pl.BlockSpec((1, D, D_EXP), lambda i, te, nv: (te[i], 0, 0)),
                  pl.BlockSpec((1, D_EXP, D), lambda i, te, nv: (te[i], 0, 0))],
        out_specs=pl.BlockSpec((TM, DW), lambda i, te, nv: (i, 0)),
        scratch_shapes=[pltpu.VMEM((D, D_EXP), BF16), pltpu.VMEM((D, D_EXP), BF16),
                        pltpu.VMEM((D_EXP, D), BF16)],
    )
    return pl.pallas_call(
        _ffn_kernel,
        grid_spec=grid_spec,
        out_shape=jax.ShapeDtypeStruct((P, DW), U32),
        compiler_params=_cparams(),
        name="ffn",
    )(te, nv, xs, wg, wu, wd)


def _combine_kernel(gd_ref, ys_ref, col_ref, x1_ref, mod_ref, g_ref, b_ref, o_ref, buf, sems, *, NT, SL):
    j = pl.program_id(0)
    slot = j % 2

    def fetch(tile, sl):
        def f(g, carry):
            _granule_copy(ys_ref, pl.multiple_of(gd_ref[tile, g], GRAN), buf.at[sl],
                          pl.multiple_of(g * GRAN, GRAN), sems.at[sl]).start()
            return carry
        lax.fori_loop(0, gd_ref[tile, G_LAST], f, 0)

    @pl.when(j == 0)
    def _():
        fetch(0, 0)

    @pl.when(j + 1 < NT)
    def _():
        fetch(j + 1, 1 - slot)

    ng = gd_ref[j, G_LAST]

    _wait_granules(ng, ys_ref, buf.at[slot], sems.at[slot], SL // GRAN)

    rows = lax.broadcasted_iota(jnp.int32, (SL, 1), 0)
    yb = _unpack_halves(jnp.where(rows < ng * GRAN, buf[slot], U32(0)))
    col = col_ref[...]
    tb = col.shape[0]
    lanes = lax.broadcasted_iota(jnp.int32, (tb, SL), 1).astype(F32)
    wsel = (jnp.where(lanes == col[:, 0:1], col[:, 2:3], 0.0)
            + jnp.where(lanes == col[:, 1:2], col[:, 3:4], 0.0))
    whi = wsel.astype(BF16)
    wlo = (wsel - whi.astype(F32)).astype(BF16)
    y = jnp.dot(whi, yb, preferred_element_type=F32) + jnp.dot(wlo, yb, preferred_element_type=F32)
    mod = mod_ref[0]
    z = ALPHA * x1_ref[...] + (1.0 + mod[5:6, :]) * y
    o_ref[...] = _layer_norm(z, g_ref[...], b_ref[...])


def _combine(gd, ys, col, x1, mod3, g, b, *, S, tb):
    N, D = x1.shape
    NT = N // tb
    tpb = S // tb
    SL = _slots_per_tile(tb)
    kern = functools.partial(_combine_kernel, NT=NT, SL=SL)
    grid_spec = pltpu.PrefetchScalarGridSpec(
        num_scalar_prefetch=1,
        grid=(NT,),
        in_specs=[pl.BlockSpec(memory_space=pl.ANY),
                  pl.BlockSpec((tb, LANES), lambda j, gd: (j, 0)),
                  pl.BlockSpec((tb, D), lambda j, gd: (j, 0)),
                  pl.BlockSpec((1, 6, D), lambda j, gd: (j // tpb, 0, 0)),
                  pl.BlockSpec((1, D), lambda j, gd: (0, 0)),
                  pl.BlockSpec((1, D), lambda j, gd: (0, 0))],
        out_specs=pl.BlockSpec((tb, D), lambda j, gd: (j, 0)),
        scratch_shapes=[pltpu.VMEM((2, SL, D // 2), U32), pltpu.SemaphoreType.DMA((2,))],
    )
    return pl.pallas_call(
        kern,
        grid_spec=grid_spec,
        out_shape=jax.ShapeDtypeStruct((N, D), F32),
        compiler_params=_cparams(),
        name="combine",
    )(gd, ys, col, x1, mod3, g, b)


def _permute_w_in(w_in):
    D = w_in.shape[0]
    o = 0
    parts = {}
    for name, width in (("mq", M_W), ("mk", M_W), ("mv", M_W), ("mo", M_W), ("mi", M_HEADS), ("mf", M_HEADS),
                        ("gq", G_KW), ("gk", G_KW), ("gv", G_W), ("gg", G_W), ("ga", G_RANK)):
        parts[name] = w_in[:, o:o + width]
        o += width
    z = lambda n: jnp.zeros((D, n), w_in.dtype)
    small = jnp.concatenate([parts["mi"], z(SM_F - M_HEADS), parts["mf"], z(SM_A - SM_F - M_HEADS),
                             parts["ga"], z(LANES - SM_A - G_RANK)], axis=1)
    return jnp.concatenate([parts["mq"], parts["mk"], parts["mv"], parts["mo"], parts["gq"], parts["gk"],
                            parts["gv"], parts["gg"], small], axis=1)


def _layer(x, c, l, w_ada, b_ada, w_in, w_conv, b_conv, b_igate, b_fgate, mlstm_norm_g, w_gla_a, b_gla_a,
           gla_norm_g, w_out, ln1_g, ln1_b, w_route_group, b_route_group, w_route_expert, b_route_expert,
           w_gate, w_up, w_down, ln2_g, ln2_b):
    B, S, D = x.shape
    N = B * S
    x2 = x.reshape(N, D)
    tm_in = min(512, S)
    tm = min(256, S)
    lm = min(256, S)

    mod3 = _ada(c, w_ada[l], b_ada[l]).reshape(B, 6, D)

    w_p = _permute_w_in(w_in[l]).astype(BF16)
    wa_pad = jnp.zeros((LANES, G_KW), F32).at[SM_A:SM_A + G_RANK].set(w_gla_a[l]).astype(BF16)
    bg = (jnp.zeros((2 * SUBLANES, 1), F32).at[0:M_HEADS, 0].set(b_igate[l])
          .at[SUBLANES:SUBLANES + M_HEADS, 0].set(b_fgate[l]))
    oa, la, g3 = _inproj(x2, mod3, w_p, w_conv[l], b_conv[l].reshape(1, -1), wa_pad,
                         b_gla_a[l].reshape(1, -1), bg, S=S, tm=tm_in, lm=lm)

    u_tri = jnp.asarray(np.triu(np.ones((lm, lm), np.float32)))
    hm = _mlstm(oa, g3, u_tri, mlstm_norm_g[l].reshape(1, -1), B=B, S=S, L=lm)
    w3_np, mk_np = _gla_consts()
    hg = _gla(oa, la, jnp.asarray(w3_np, BF16), jnp.asarray(mk_np), gla_norm_g[l].reshape(1, -1), B=B, S=S,
              nb=2 if B % 2 == 0 else 1)

    wr = (jnp.zeros((D, LANES), F32).at[:, 0:N_GROUPS].set(w_route_group[l])
          .at[:, SUBLANES:SUBLANES + N_EXP].set(w_route_expert[l]))
    br = (jnp.zeros((1, LANES), F32).at[0, 0:N_GROUPS].set(b_route_group[l])
          .at[0, SUBLANES:SUBLANES + N_EXP].set(b_route_expert[l]))
    wr_hi = wr.astype(BF16)
    wr2 = jnp.concatenate([wr_hi, (wr - wr_hi.astype(F32)).astype(BF16)], axis=1)
    x1, u2, rrow = _outproj(hm, hg, w_out[l].astype(BF16), x2, mod3, ln1_g[l].reshape(1, -1),
                            ln1_b[l].reshape(1, -1), wr2, br, S=S, tm=tm)

    u_cnt = jnp.asarray(np.triu(np.ones((tm, tm), np.float32)), BF16)
    ltri = jnp.asarray(np.tril(np.ones((N_EXP, N_EXP), np.float32), -1))
    srow, col, gd3, meta = _route(rrow, u_cnt, ltri, TM=FFN_TM)
    gd = gd3.reshape(N // tm + 2, LANES)
    n_tiles = _ffn_tiles(N, tm)
    te, nv = meta[0, :n_tiles], meta[1, 0:1]

    xs = _dispatch(gd, srow, u2, n_tiles=n_tiles, TM=FFN_TM)
    ys = _ffn(te, nv, xs, w_gate[l], w_up[l], w_down[l], TM=FFN_TM)
    out = _combine(gd, ys, col, x1, mod3, ln2_g[l].reshape(1, -1), ln2_b[l].reshape(1, -1), S=S, tb=tm)
    return out.reshape(B, S, D)


def kernel(x, c, w_ada, b_ada, w_in, w_conv, b_conv, b_igate, b_fgate, mlstm_norm_g, w_gla_a, b_gla_a,
           gla_norm_g, w_out, ln1_g, ln1_b, w_route_group, b_route_group, w_route_expert, b_route_expert,
           w_gate, w_up, w_down, ln2_g, ln2_b):
    for l in range(DEPTH):
        x = _layer(x, c, l, w_ada, b_ada, w_in, w_conv, b_conv, b_igate, b_fgate, mlstm_norm_g, w_gla_a,
                   b_gla_a, gla_norm_g, w_out, ln1_g, ln1_b, w_route_group, b_route_group, w_route_expert,
                   b_route_expert, w_gate, w_up, w_down, ln2_g, ln2_b)
    return x
```

```python
import functools

import numpy as np
import jax
import jax.numpy as jnp
from jax import lax
from jax.experimental import pallas as pl
from jax.experimental.pallas import tpu as pltpu

F32 = jnp.float32
BF16 = jnp.bfloat16
HIGHEST = lax.Precision.HIGHEST

DEPTH = 1
M_HEADS = 4
M_HD = 128
M_W = M_HEADS * M_HD
CONV_W = 4
G_HEADS = 4
G_DK = 64
G_DV = 128
G_W = G_HEADS * G_DV
G_KW = G_HEADS * G_DK
G_RANK = 16
G_TAU = 16.0
G_CHUNK = 64
N_GROUPS = 4
E_PER_G = 8
N_EXP = N_GROUPS * E_PER_G
D_EXP = 512
ALPHA = (2 * DEPTH) ** 0.25
LN_EPS = 1e-5

LANES = 128
SUBLANES = 8
VMEM_LIMIT = 48 * 1024 * 1024

C_QK = 0
C_VO = 1024
C_GQK = 2048
C_GV = 2560
C_GG = 3072
C_SMALL = 3584
C_TOT = 3712
SM_I, SM_F, SM_A = 0, 8, 16
IN_GATES = 4 * M_W
IN_G = IN_GATES + 2 * M_HEADS
IN_GA = IN_G + 2 * G_KW + 2 * G_W
IN_TOT = IN_GA + G_RANK

FFN_TM = 256
FFN_CB = 256
GRAN = SUBLANES
G_LAST = LANES - 1


def _cparams(n_axes=1):
    return pltpu.CompilerParams(dimension_semantics=("arbitrary",) * n_axes,
                                vmem_limit_bytes=VMEM_LIMIT)


def _sigmoid(x):
    return 1.0 / (1.0 + jnp.exp(-x))


def _log_sigmoid(x):
    return jnp.minimum(x, 0.0) - jnp.log(1.0 + jnp.exp(-jnp.abs(x)))


def _ada_kernel(c_ref, w_ref, b_ref, o_ref):
    c = c_ref[...]
    ca = c * _sigmoid(c)
    o_ref[...] = jnp.dot(ca, w_ref[...], preferred_element_type=F32, precision=HIGHEST) + b_ref[...]


def _ada(c, w, b):
    B, D = c.shape
    n_out = w.shape[1]
    tn = 1024
    return pl.pallas_call(
        _ada_kernel,
        grid=(n_out // tn,),
        in_specs=[pl.BlockSpec((B, D), lambda j: (0, 0)),
                  pl.BlockSpec((D, tn), lambda j: (0, j)),
                  pl.BlockSpec((1, tn), lambda j: (0, j))],
        out_specs=pl.BlockSpec((B, tn), lambda j: (0, j)),
        out_shape=jax.ShapeDtypeStruct((B, n_out), F32),
        compiler_params=_cparams(),
        name="ada",
    )(c, w, b.reshape(1, n_out))


def _inproj_kernel(x_ref, mod_ref, win_ref, wc_ref, bc_ref, wa_ref, ba_ref, bg_ref,
                   oa_ref, la_ref, g_ref, halo_ref, w_ref, *, tm, tpb, lm):
    i = pl.program_id(0)

    @pl.when(i == 0)
    def _():
        rc = LANES
        for r in range(0, win_ref.shape[0], rc):
            rs = slice(r, r + rc)
            w_ref[rs, 0:IN_GATES] = win_ref[rs, 0:IN_GATES].astype(BF16)
            t = win_ref[rs, IN_GATES:IN_TOT]
            w_ref[rs, C_GQK:C_SMALL] = t[:, IN_G - IN_GATES:IN_GA - IN_GATES].astype(BF16)
            z = lambda n: jnp.zeros((rc, n), F32)
            small = jnp.concatenate([t[:, 0:M_HEADS], z(SM_F - M_HEADS), t[:, M_HEADS:2 * M_HEADS],
                                     z(SM_A - SM_F - M_HEADS), t[:, IN_GA - IN_GATES:IN_TOT - IN_GATES],
                                     z(LANES - SM_A - G_RANK)], axis=1)
            w_ref[rs, C_SMALL:C_TOT] = small.astype(BF16)

    @pl.when(i % tpb == 0)
    def _():
        halo_ref[...] = jnp.zeros_like(halo_ref)

    mod = mod_ref[0]
    u = (x_ref[...] * (1.0 + mod[1:2, :]) + mod[0:1, :]).astype(BF16)

    p = jnp.dot(u, w_ref[:, C_QK:C_QK + 2 * M_W], preferred_element_type=F32)
    ext = jnp.concatenate([halo_ref[...], p], axis=0)
    acc = bc_ref[...] + wc_ref[CONV_W - 1:CONV_W, :] * p
    for j in range(CONV_W - 1):
        sh = pltpu.roll(ext, CONV_W - 1 - j, 0)[SUBLANES:, :]
        acc = acc + wc_ref[j:j + 1, :] * sh
    halo_ref[...] = p[tm - SUBLANES:, :]
    qk = acc * _sigmoid(acc)
    oa_ref[:, C_QK:C_QK + M_W] = qk[:, :M_W].astype(BF16)
    oa_ref[:, C_QK + M_W:C_QK + 2 * M_W] = (qk[:, M_W:] * (M_HD ** -0.5)).astype(BF16)

    p = jnp.dot(u, w_ref[:, C_VO:C_VO + 2 * M_W], preferred_element_type=F32)
    oa_ref[:, C_VO:C_VO + 2 * M_W] = p.astype(BF16)

    p = jnp.dot(u, w_ref[:, C_GQK:C_GQK + G_KW], preferred_element_type=F32)
    oa_ref[:, C_GQK:C_GQK + G_KW] = (p * (G_DK ** -0.5)).astype(BF16)
    p = jnp.dot(u, w_ref[:, C_GQK + G_KW:C_SMALL], preferred_element_type=F32)
    oa_ref[:, C_GQK + G_KW:C_SMALL] = p.astype(BF16)

    ps = jnp.dot(u, w_ref[:, C_SMALL:C_TOT], preferred_element_type=F32)
    la = jnp.dot(ps.astype(BF16), wa_ref[...], preferred_element_type=F32) + ba_ref[...]
    la_ref[...] = _log_sigmoid(la) * (1.0 / G_TAU)
    pt = ps.T
    gi = pt[SM_I:SM_I + SUBLANES, :] + bg_ref[0:SUBLANES, :]
    gf = _log_sigmoid(pt[SM_F:SM_F + SUBLANES, :] + bg_ref[SUBLANES:2 * SUBLANES, :])
    for j in range(tm // lm):
        g_ref[j, 0:SUBLANES, :] = gi[:, j * lm:(j + 1) * lm]
        g_ref[j, SUBLANES:2 * SUBLANES, :] = gf[:, j * lm:(j + 1) * lm]


def _inproj(x2, mod3, w_in, w_conv, b_conv, wa_pad, b_gla, bg, *, S, tm, lm):
    N, D = x2.shape
    tpb = S // tm
    kern = functools.partial(_inproj_kernel, tm=tm, tpb=tpb, lm=lm)
    return pl.pallas_call(
        kern,
        grid=(N // tm,),
        in_specs=[pl.BlockSpec((tm, D), lambda i: (i, 0)),
                  pl.BlockSpec((1, 6, D), lambda i: (i // tpb, 0, 0)),
                  pl.BlockSpec((D, IN_TOT), lambda i: (0, 0), pipeline_mode=pl.Buffered(1)),
                  pl.BlockSpec((CONV_W, 2 * M_W), lambda i: (0, 0)),
                  pl.BlockSpec((1, 2 * M_W), lambda i: (0, 0)),
                  pl.BlockSpec((LANES, G_KW), lambda i: (0, 0)),
                  pl.BlockSpec((1, G_KW), lambda i: (0, 0)),
                  pl.BlockSpec((2 * SUBLANES, 1), lambda i: (0, 0))],
        out_specs=[pl.BlockSpec((tm, C_SMALL), lambda i: (i, 0)),
                   pl.BlockSpec((tm, G_KW), lambda i: (i, 0)),
                   pl.BlockSpec((tm // lm, 2 * SUBLANES, lm), lambda i: (i, 0, 0))],
        out_shape=[jax.ShapeDtypeStruct((N, C_SMALL), BF16),
                   jax.ShapeDtypeStruct((N, G_KW), F32),
                   jax.ShapeDtypeStruct((N // lm, 2 * SUBLANES, lm), F32)],
        scratch_shapes=[pltpu.VMEM((SUBLANES, 2 * M_W), F32), pltpu.VMEM((D, C_TOT), BF16)],
        compiler_params=_cparams(),
        name="inproj",
    )(x2, mod3, w_in, w_conv, b_conv, wa_pad, b_gla, bg)


def _mlstm_kernel(qk_ref, vo_ref, g_ref, u_ref, gain_ref, out_ref, c_ref, zt_ref, a_ref, dec_ref, *, L, NC):
    c_ref[...] = jnp.zeros_like(c_ref)
    lane = lax.broadcasted_iota(jnp.int32, (SUBLANES, L), 1)
    tril = (lax.broadcasted_iota(jnp.int32, (L, L), 0) >= lax.broadcasted_iota(jnp.int32, (L, L), 1))
    ones_v = jnp.ones((L, M_HD), BF16)
    zpad = jnp.zeros((LANES - 4 * SUBLANES, L), F32)

    bs, gs = [], []
    for c in range(NC):
        b = jnp.dot(g_ref[c, SUBLANES:2 * SUBLANES, :], u_ref[...], preferred_element_type=F32,
                    precision=HIGHEST)
        a = g_ref[c, 0:SUBLANES, :] - b
        a_ref[c] = a
        G = a
        s = 1
        while s < L:
            G = jnp.maximum(G, jnp.where(lane >= s, pltpu.roll(G, s, 1), -jnp.inf))
            s *= 2
        bs.append(b)
        gs.append(G)
    m_prev = jnp.zeros((SUBLANES, 1), F32)
    for c in range(NC):
        M = jnp.maximum(gs[c], m_prev)
        ML = M[:, L - 1:L]
        Z = jnp.concatenate([M, jnp.exp(m_prev - M), jnp.exp(-(bs[c] + M)), jnp.exp(a_ref[c] - ML), zpad],
                            axis=0)
        zt_ref[c] = Z.T
        dec_ref[c] = jnp.broadcast_to(jnp.exp(m_prev - ML), (SUBLANES, 2 * M_HD))
        m_prev = bs[c][:, L - 1:L] + ML

    def chunk(c, carry):
        r0 = pl.multiple_of(c * L, L)
        Zt = zt_ref[c]
        a = a_ref[c]
        dec = dec_ref[c]
        rows = pl.ds(r0, L)
        heads = range(M_HEADS)
        hs = [slice(h * M_HD, (h + 1) * M_HD) for h in heads]
        hs2 = [slice(M_W + h * M_HD, M_W + (h + 1) * M_HD) for h in heads]
        q = [qk_ref[rows, hs[h]] for h in heads]
        k = [qk_ref[rows, hs2[h]] for h in heads]
        vext = [jnp.concatenate([vo_ref[rows, hs[h]], ones_v], axis=1) for h in heads]
        cst = [c_ref[h] for h in heads]
        sc = [lax.dot_general(q[h], k[h], (((1,), (1,)), ((), ())), preferred_element_type=F32) for h in heads]
        qc = [jnp.dot(q[h], cst[h].astype(BF16), preferred_element_type=F32) for h in heads]
        pm = [(sc[h] * jnp.exp(jnp.where(tril, a[h:h + 1, :] - Zt[:, h:h + 1], -jnp.inf))).astype(BF16)
              for h in heads]
        pv = [jnp.dot(pm[h], vext[h], preferred_element_type=F32) for h in heads]
        kw = [(Zt[:, 3 * SUBLANES + h:3 * SUBLANES + h + 1] * k[h].astype(F32)).astype(BF16) for h in heads]
        upd = [lax.dot_general(kw[h], vext[h], (((0,), (0,)), ((), ())), preferred_element_type=F32)
               for h in heads]
        for h in heads:
            c_ref[h] = dec[h:h + 1, :] * cst[h] + upd[h]
            nd = pv[h] + Zt[:, SUBLANES + h:SUBLANES + h + 1] * qc[h]
            hh = nd[:, :M_HD] / jnp.maximum(jnp.abs(nd[:, M_HD:]), Zt[:, 2 * SUBLANES + h:2 * SUBLANES + h + 1])
            hh = _sigmoid(vo_ref[rows, hs2[h]].astype(F32)) * hh
            hn = hh * lax.rsqrt(jnp.mean(hh * hh, axis=-1, keepdims=True) + LN_EPS)
            out_ref[rows, hs[h]] = (hn * gain_ref[:, hs[h]]).astype(BF16)
        return carry

    lax.fori_loop(0, NC, chunk, 0)


def _mlstm(oa, g3, u_tri, gain, *, B, S, L):
    N = oa.shape[0]
    NC = S // L
    kern = functools.partial(_mlstm_kernel, L=L, NC=NC)
    return pl.pallas_call(
        kern,
        grid=(B,),
        in_specs=[pl.BlockSpec((S, 2 * M_W), lambda b: (b, C_QK // (2 * M_W))),
                  pl.BlockSpec((S, 2 * M_W), lambda b: (b, C_VO // (2 * M_W))),
                  pl.BlockSpec((NC, 2 * SUBLANES, L), lambda b: (b, 0, 0)),
                  pl.BlockSpec((L, L), lambda b: (0, 0)),
                  pl.BlockSpec((1, M_W), lambda b: (0, 0))],
        out_specs=pl.BlockSpec((S, M_W), lambda b: (b, 0)),
        out_shape=jax.ShapeDtypeStruct((N, M_W), BF16),
        scratch_shapes=[pltpu.VMEM((M_HEADS, M_HD, 2 * M_HD), F32), pltpu.VMEM((NC, L, LANES), F32),
                        pltpu.VMEM((NC, SUBLANES, L), F32), pltpu.VMEM((NC, SUBLANES, 2 * M_HD), F32)],
        compiler_params=_cparams(),
        name="mlstm",
    )(oa, oa, g3, u_tri, gain)


_G_LEVELS = 6
_G_XROW = 2 * G_CHUNK + SUBLANES


def _gla_consts():
    L = G_CHUNK
    t = np.arange(L)
    blocks = [(t[None, :] <= t[:, None]).astype(np.float32),
              (t[None, :] > t[:, None]).astype(np.float32),
              np.ones((SUBLANES, L), np.float32)]
    masks = [np.eye(L, dtype=np.float32)]
    m = 1
    while m < L:
        wl = np.zeros((L, L), np.float32)
        for r in range(L):
            r0 = (r // (2 * m)) * 2 * m + m
            if r % (2 * m) >= m:
                wl[r, r0:r + 1] = 1.0
            else:
                wl[r, r + 1:r0] = 1.0
        blocks.append(wl)
        tt, ss = t[:, None], t[None, :]
        masks.append(((tt // (2 * m) == ss // (2 * m)) & (tt % (2 * m) >= m)
                      & (ss % (2 * m) < m)).astype(np.float32))
        m *= 2
    w = np.concatenate(blocks, axis=0)
    w3 = np.concatenate([w, w, w], axis=1)
    mk = np.stack([np.concatenate([x] * G_HEADS, axis=0) for x in masks])
    return w3, mk


def _gla_kernel(qk_ref, v_ref, gg_ref, la_ref, w3_ref, mk_ref, gain_ref, out_ref, st_ref, *, NC, S, nb):
    L = G_CHUNK
    st_ref[...] = jnp.zeros_like(st_ref)
    lane_head = lax.broadcasted_iota(jnp.int32, (L, G_KW), 1) // G_DK
    br = lax.broadcasted_iota(jnp.int32, (2 * G_DV, LANES), 0) < G_DV
    bl = lax.broadcasted_iota(jnp.int32, (2 * G_DV, LANES), 1) < G_DK
    bmask = br == bl
    nt = (((1,), (1,)), ((), ()))
    tn = (((0,), (0,)), ((), ()))

    def chunk(c, carry):
        rows = [pl.ds(pl.multiple_of(bi * S + c * L, L), L) for bi in range(nb)]
        X, q, k = [], [], []
        for bi in range(nb):
            la = la_ref[rows[bi], :]
            hi = la.astype(BF16)
            r1 = la - hi.astype(F32)
            mid = r1.astype(BF16)
            lo = (r1 - mid.astype(F32)).astype(BF16)
            stk = jnp.concatenate([hi, mid, lo], axis=0)
            X.append(jnp.exp(jnp.dot(w3_ref[...], stk, preferred_element_type=F32)))
            q.append(qk_ref[rows[bi], 0:G_KW].astype(F32))
            k.append(qk_ref[rows[bi], G_KW:2 * G_KW].astype(F32))

        sc = [[None] * (_G_LEVELS + 1) for _ in range(nb)]
        for lev in range(_G_LEVELS + 1):
            for bi in range(nb):
                if lev == 0:
                    qt, kt = q[bi], k[bi]
                else:
                    xl = X[bi][_G_XROW + L * (lev - 1):_G_XROW + L * lev, :]
                    qt, kt = q[bi] * xl, k[bi] * xl
                q4 = jnp.concatenate([jnp.where(lane_head == h, qt, 0.0) for h in range(G_HEADS)],
                                     axis=0).astype(BF16)
                sc[bi][lev] = lax.dot_general(q4, kt.astype(BF16), nt, preferred_element_type=F32)
        Ab = []
        for bi in range(nb):
            A = sc[bi][0] * mk_ref[0]
            for lev in range(1, _G_LEVELS + 1):
                A = A + sc[bi][lev] * mk_ref[lev]
            Ab.append(A.astype(BF16))

        for bi in range(nb):
            gg = gg_ref[rows[bi], :].astype(F32)
            gate = gg * _sigmoid(gg)
            for p in range(2):
                ls = slice(LANES * p, LANES * (p + 1))
                vp = v_ref[rows[bi], 2 * G_DV * p:2 * G_DV * (p + 1)]
                oi = [jnp.dot(Ab[bi][L * (2 * p + hh):L * (2 * p + hh + 1)],
                              vp[:, G_DV * hh:G_DV * (hh + 1)], preferred_element_type=F32)
                      for hh in range(2)]
                st = st_ref[bi, p]
                qc = (q[bi][:, ls] * X[bi][0:L, ls]).astype(BF16)
                o_inter = lax.dot_general(qc, st.astype(BF16), nt, preferred_element_type=F32)
                kc = (k[bi][:, ls] * X[bi][L:2 * L, ls]).astype(BF16)
                upd = lax.dot_general(vp, kc, tn, preferred_element_type=F32)
                dec = X[bi][2 * L:2 * L + 1, ls]
                st_ref[bi, p] = jnp.where(bmask, dec * st + upd, 0.0)
                for hh in range(2):
                    o = o_inter[:, G_DV * hh:G_DV * (hh + 1)] + oi[hh]
                    hn = o * lax.rsqrt(jnp.mean(o * o, axis=-1, keepdims=True) + LN_EPS)
                    hs = slice(G_DV * (2 * p + hh), G_DV * (2 * p + hh + 1))
                    out_ref[rows[bi], hs] = (hn * gain_ref[:, hs] * gate[:, hs]).astype(BF16)
        return carry

    lax.fori_loop(0, NC, chunk, 0)


def _gla(oa, la, w3, mk, gain, *, B, S, nb):
    N = oa.shape[0]
    NC = S // G_CHUNK
    kern = functools.partial(_gla_kernel, NC=NC, S=S, nb=nb)
    R = nb * S
    return pl.pallas_call(
        kern,
        grid=(B // nb,),
        in_specs=[pl.BlockSpec((R, 2 * G_KW), lambda b: (b, C_GQK // (2 * G_KW))),
                  pl.BlockSpec((R, G_W), lambda b: (b, C_GV // G_W)),
                  pl.BlockSpec((R, G_W), lambda b: (b, C_GG // G_W)),
                  pl.BlockSpec((R, G_KW), lambda b: (b, 0)),
                  pl.BlockSpec(w3.shape, lambda b: (0, 0)),
                  pl.BlockSpec(mk.shape, lambda b: (0, 0, 0)),
                  pl.BlockSpec((1, G_W), lambda b: (0, 0))],
        out_specs=pl.BlockSpec((R, G_W), lambda b: (b, 0)),
        out_shape=jax.ShapeDtypeStruct((N, G_W), BF16),
        scratch_shapes=[pltpu.VMEM((nb, 2, 2 * G_DV, LANES), F32)],
        compiler_params=_cparams(),
        name="gla",
    )(oa, oa, oa, la, w3, mk, gain)


def _layer_norm(z, g, b):
    mu = jnp.mean(z, axis=-1, keepdims=True)
    zc = z - mu
    var = jnp.mean(zc * zc, axis=-1, keepdims=True)
    return zc * lax.rsqrt(var + LN_EPS) * g + b


def _outproj_kernel(hm_ref, hg_ref, wf_ref, x_ref, mod_ref, g_ref, b_ref, wr_ref, br_ref,
                    x1_ref, u2_ref, rrow_ref, w_ref, *, tb, nh):
    @pl.when(pl.program_id(0) == 0)
    def _():
        w_ref[...] = wf_ref[...].astype(BF16)

    mod = mod_ref[0]
    blocks = [slice(tb * j, tb * (j + 1)) for j in range(nh)]
    y = [jnp.dot(hm_ref[r, :], w_ref[0:M_W, :], preferred_element_type=F32)
         + jnp.dot(hg_ref[r, :], w_ref[M_W:M_W + G_W, :], preferred_element_type=F32) for r in blocks]
    u2 = []
    for j, r in enumerate(blocks):
        z = ALPHA * x_ref[r, :] + (1.0 + mod[2:3, :]) * y[j]
        x1 = _layer_norm(z, g_ref[...], b_ref[...])
        x1_ref[r, :] = x1
        u2.append(x1 * (1.0 + mod[4:5, :]) + mod[3:4, :])
        u2_ref[r, :] = u2[j].astype(BF16)

    u2h = [u.astype(BF16) for u in u2]
    u2l = [(u2[j] - u2h[j].astype(F32)).astype(BF16) for j in range(nh)]
    lh = [jnp.dot(u, wr_ref[...], preferred_element_type=F32) for u in u2h]
    ll = [jnp.dot(u, wr_ref[:, 0:LANES], preferred_element_type=F32) for u in u2l]
    for j in range(nh):
        logits = lh[j][:, 0:LANES] + lh[j][:, LANES:2 * LANES] + ll[j] + br_ref[...]
        rrow_ref[j] = _route_select(logits.T, tb)


def _route_select(lt, tm):
    row = lax.broadcasted_iota(jnp.int32, (SUBLANES, tm), 0)
    gl = jnp.where(row < N_GROUPS, lt[0:SUBLANES, :], -jnp.inf)
    gmax = jnp.max(gl, axis=0, keepdims=True)
    gsel = jnp.min(jnp.where(gl == gmax, row, SUBLANES), axis=0, keepdims=True)
    pg = 1.0 / jnp.sum(jnp.exp(gl - gmax), axis=0, keepdims=True)
    ein = jnp.zeros((SUBLANES, tm), F32)
    for g in range(N_GROUPS):
        ein = jnp.where(gsel == g, lt[SUBLANES * (g + 1):SUBLANES * (g + 2), :], ein)
    v1 = jnp.max(ein, axis=0, keepdims=True)
    i1 = jnp.min(jnp.where(ein == v1, row, SUBLANES), axis=0, keepdims=True)
    rest = jnp.where(row == i1, -jnp.inf, ein)
    v2 = jnp.max(rest, axis=0, keepdims=True)
    i2 = jnp.min(jnp.where(rest == v2, row, SUBLANES), axis=0, keepdims=True)
    t2 = jnp.exp(v2 - v1)
    p1 = 1.0 / (1.0 + t2)
    e0 = (gsel * E_PER_G + i1).astype(F32)
    e1 = (gsel * E_PER_G + i2).astype(F32)
    return jnp.concatenate([e0, e1, pg * p1, pg * (t2 * p1), jnp.zeros((SUBLANES - 4, tm), F32)], axis=0)


def _outproj(hm, hg, w_out, x2, mod3, g, b, wr, br, *, S, tb, nh):
    N, D = x2.shape
    tm = tb * nh
    tpb = S // tm
    kern = functools.partial(_outproj_kernel, tb=tb, nh=nh)
    return pl.pallas_call(
        kern,
        grid=(N // tm,),
        in_specs=[pl.BlockSpec((tm, M_W), lambda i: (i, 0)),
                  pl.BlockSpec((tm, G_W), lambda i: (i, 0)),
                  pl.BlockSpec((M_W + G_W, D), lambda i: (0, 0), pipeline_mode=pl.Buffered(1)),
                  pl.BlockSpec((tm, D), lambda i: (i, 0)),
                  pl.BlockSpec((1, 6, D), lambda i: (i // tpb, 0, 0)),
                  pl.BlockSpec((1, D), lambda i: (0, 0)),
                  pl.BlockSpec((1, D), lambda i: (0, 0)),
                  pl.BlockSpec((D, 2 * LANES), lambda i: (0, 0)),
                  pl.BlockSpec((1, LANES), lambda i: (0, 0))],
        out_specs=[pl.BlockSpec((tm, D), lambda i: (i, 0)),
                   pl.BlockSpec((tm, D), lambda i: (i, 0)),
                   pl.BlockSpec((nh, SUBLANES, tb), lambda i: (i, 0, 0))],
        out_shape=[jax.ShapeDtypeStruct((N, D), F32),
                   jax.ShapeDtypeStruct((N, D), BF16),
                   jax.ShapeDtypeStruct((N // tb, SUBLANES, tb), F32)],
        scratch_shapes=[pltpu.VMEM((M_W + G_W, D), BF16)],
        compiler_params=_cparams(),
        name="outproj",
    )(hm, hg, w_out, x2, mod3, g, b, wr, br)


def _slots_per_tile(tb):
    worst = 2 * tb + N_EXP * (GRAN - 1)
    return -(-worst // LANES) * LANES


def _ffn_tiles(n_tok, tb):
    worst_rows = 2 * n_tok + (n_tok // tb) * N_EXP * (GRAN - 1)
    return -(-worst_rows // FFN_TM) + N_EXP


def _route_kernel(rr_ref, u_ref, lt_ref, srow_ref, col_ref, gd_ref, meta_ref, mg_ref, part_ref,
                  *, NT, tb, TM):
    iota_e = lax.broadcasted_iota(jnp.int32, (N_EXP, tb), 0).astype(F32)
    glane = lax.broadcasted_iota(jnp.int32, (N_EXP, LANES), 1).astype(F32)
    ltri = lt_ref[...]

    def prefix_e(col):
        return jnp.dot(ltri, jnp.broadcast_to(col, (N_EXP, LANES)),
                       preferred_element_type=F32, precision=HIGHEST)[:, 0:1]

    def p1(j, run8):
        r = rr_ref[j]
        oh0 = jnp.where(iota_e == r[0:1, :], 1.0, 0.0)
        oh1 = jnp.where(iota_e == r[1:2, :], 1.0, 0.0)
        cum0 = jnp.dot(oh0.astype(BF16), u_ref[...], preferred_element_type=F32)
        cum1 = jnp.dot(oh1.astype(BF16), u_ref[...], preferred_element_type=F32)
        c0 = jnp.sum(oh0, axis=1, keepdims=True)
        n8 = jnp.floor((c0 + jnp.sum(oh1, axis=1, keepdims=True) + (GRAN - 1.0)) * (1.0 / GRAN))
        lo8 = prefix_e(n8)
        s0 = jnp.sum(oh0 * (GRAN * lo8 + cum0 - 1.0), axis=0, keepdims=True)
        s1 = jnp.sum(oh1 * (GRAN * lo8 + c0 + cum1 - 1.0), axis=0, keepdims=True)
        info = jnp.concatenate([s0, s1, r[2:4, :], jnp.zeros((SUBLANES - 4, tb), F32)], axis=0)
        srow_ref[j] = info.astype(jnp.int32)
        col_ref[pl.ds(pl.multiple_of(j * tb, tb), tb), :] = jnp.concatenate(
            [info, jnp.zeros((LANES - SUBLANES, tb), F32)], axis=0).T
        mg = jnp.where((lo8 <= glane) & (glane < lo8 + n8), 1.0, 0.0)
        mg_ref[j] = mg
        part = jnp.sum(mg * (run8 + glane - lo8), axis=0, keepdims=True)
        gcnt = jnp.broadcast_to(jnp.sum(n8, axis=0, keepdims=True), (1, LANES))
        part_ref[j] = jnp.concatenate([part, gcnt, jnp.zeros((SUBLANES - 2, LANES), F32)], axis=0)
        return run8 + n8

    tot8 = lax.fori_loop(0, NT, p1, jnp.zeros((N_EXP, 1), F32))
    seg_t = jnp.floor((tot8 * GRAN + (TM - 1.0)) * (1.0 / TM))
    base_t = prefix_e(seg_t)
    base8 = base_t * (TM // GRAN)
    lane1 = lax.broadcasted_iota(jnp.int32, (1, LANES), 1)

    def p2(j, carry):
        pr = part_ref[j]
        dst = (pr[0:1, :] + jnp.sum(mg_ref[j] * base8, axis=0, keepdims=True)) * GRAN
        gd_ref[j] = jnp.where(lane1 == G_LAST, pr[1:2, :], dst).astype(jnp.int32)
        return carry

    lax.fori_loop(0, NT, p2, 0)
    eye = jnp.where(glane == lax.broadcasted_iota(jnp.int32, (N_EXP, LANES), 0).astype(F32), 1.0, 0.0)
    tail_row = jnp.sum(eye * ((base8 + tot8) * GRAN), axis=0, keepdims=True)
    tail_n8 = jnp.sum(eye * (seg_t * (TM // GRAN) - tot8), axis=0, keepdims=True)
    nv_l = jnp.broadcast_to(jnp.sum(seg_t, axis=0, keepdims=True), (1, LANES))
    gd_ref[NT] = jnp.where(lane1 == G_LAST, nv_l, tail_row).astype(jnp.int32)
    gd_ref[NT + 1] = tail_n8.astype(jnp.int32)
    ti = lax.broadcasted_iota(jnp.int32, (N_EXP, tb), 1).astype(F32)
    te = jnp.sum(jnp.where(base_t <= ti, 1.0, 0.0), axis=0, keepdims=True) - 1.0
    nv = jnp.broadcast_to(jnp.sum(seg_t, axis=0, keepdims=True), (1, tb))
    meta_ref[...] = jnp.concatenate([te, nv, jnp.zeros((SUBLANES - 2, tb), F32)],
                                    axis=0).astype(jnp.int32)


def _route(rrow, u_cnt, ltri, *, TM):
    NT, _, tb = rrow.shape
    kern = functools.partial(_route_kernel, NT=NT, tb=tb, TM=TM)
    full3 = lambda i: (0, 0, 0)
    return pl.pallas_call(
        kern,
        grid=(1,),
        in_specs=[pl.BlockSpec((NT, SUBLANES, tb), full3),
                  pl.BlockSpec((tb, tb), lambda i: (0, 0)),
                  pl.BlockSpec((N_EXP, N_EXP), lambda i: (0, 0))],
        out_specs=[pl.BlockSpec((NT, SUBLANES, tb), full3),
                   pl.BlockSpec((NT * tb, LANES), lambda i: (0, 0)),
                   pl.BlockSpec((NT + 2, 1, LANES), full3),
                   pl.BlockSpec((SUBLANES, tb), lambda i: (0, 0))],
        out_shape=[jax.ShapeDtypeStruct((NT, SUBLANES, tb), jnp.int32),
                   jax.ShapeDtypeStruct((NT * tb, LANES), F32),
                   jax.ShapeDtypeStruct((NT + 2, 1, LANES), jnp.int32),
                   jax.ShapeDtypeStruct((SUBLANES, tb), jnp.int32)],
        scratch_shapes=[pltpu.VMEM((NT, N_EXP, LANES), F32), pltpu.VMEM((NT, SUBLANES, LANES), F32)],
        compiler_params=_cparams(),
        name="route",
    )(rrow, u_cnt, ltri)


U32 = jnp.uint32
_HI_MASK = 0xFFFF0000


def _pack_halves(x):
    c = x.shape[1] // 2
    lo = lax.bitcast_convert_type(x[:, :c], U32)
    hi = lax.bitcast_convert_type(x[:, c:], U32)
    return (lo >> 16) | (hi & U32(_HI_MASK))


def _unpack_halves(w):
    lo = lax.bitcast_convert_type(w << 16, F32)
    hi = lax.bitcast_convert_type(w & U32(_HI_MASK), F32)
    return jnp.concatenate([lo, hi], axis=1).astype(BF16)


def _granule_copy(src_ref, src_row, dst_ref, dst_row, sem):
    return pltpu.make_async_copy(src_ref.at[pl.ds(src_row, GRAN), :], dst_ref.at[pl.ds(dst_row, GRAN), :], sem)


def _wait_granules(n, src_ref, dst_ref, sem, n_max):
    b = 1
    while b <= n_max:
        @pl.when((n & b) != 0)
        def _(b=b):
            pltpu.make_async_copy(src_ref.at[pl.ds(0, b * GRAN), :], dst_ref.at[pl.ds(0, b * GRAN), :],
                                  sem).wait()
        b *= 2


def _dispatch_kernel(gd_ref, srow_ref, u_ref, xs_ref, buf, zbuf, sems, *, NT, SL, TM, n_tiles):
    j = pl.program_id(0)
    slot = j % 2
    zsem = sems.at[2]

    def drain(tile, sl):
        _wait_granules(gd_ref[tile, G_LAST], buf.at[sl], xs_ref, sems.at[sl], SL // GRAN)

    def tile_fill(t):
        return pltpu.make_async_copy(zbuf, xs_ref.at[pl.ds(pl.multiple_of(t * TM, TM), TM), :], zsem)

    def zero_fill(wait):
        for e in range(N_EXP):
            def zg(g, carry, e=e):
                cp = _granule_copy(zbuf, 0, xs_ref, pl.multiple_of(gd_ref[NT, e] + g * GRAN, GRAN), zsem)
                cp.wait() if wait else cp.start()
                return carry
            lax.fori_loop(0, gd_ref[NT + 1, e], zg, 0)

        def zt(t, carry):
            tile_fill(t).wait() if wait else tile_fill(t).start()
            return carry
        lax.fori_loop(gd_ref[NT, G_LAST], n_tiles, zt, 0)

    @pl.when(j == 0)
    def _():
        zbuf[...] = jnp.zeros_like(zbuf)
        zero_fill(False)

    @pl.when(j >= 2)
    def _():
        drain(j - 2, slot)

    s = srow_ref[0]
    rows = lax.broadcasted_iota(jnp.int32, (SL, s.shape[1]), 0)
    oh = jnp.where((rows == s[0:1, :]) | (rows == s[1:2, :]), 1.0, 0.0).astype(BF16)
    buf[slot] = _pack_halves(jnp.dot(oh, u_ref[...], preferred_element_type=F32))

    def issue(g, carry):
        _granule_copy(buf.at[slot], pl.multiple_of(g * GRAN, GRAN), xs_ref,
                      pl.multiple_of(gd_ref[j, g], GRAN), sems.at[slot]).start()
        return carry

    lax.fori_loop(0, gd_ref[j, G_LAST], issue, 0)

    @pl.when(j == NT - 1)
    def _():
        drain(j, slot)
        if NT > 1:
            drain(j - 1, 1 - slot)
        zero_fill(True)


def _dispatch(gd, srow, u2, *, n_tiles, TM):
    N, D = u2.shape
    NT, _, tb = srow.shape
    SL = _slots_per_tile(tb)
    n_rows = n_tiles * TM
    kern = functools.partial(_dispatch_kernel, NT=NT, SL=SL, TM=TM, n_tiles=n_tiles)
    grid_spec = pltpu.PrefetchScalarGridSpec(
        num_scalar_prefetch=1,
        grid=(NT,),
        in_specs=[pl.BlockSpec((1, SUBLANES, tb), lambda j, gd: (j, 0, 0)),
                  pl.BlockSpec((tb, D), lambda j, gd: (j, 0))],
        out_specs=pl.BlockSpec(memory_space=pl.ANY),
        scratch_shapes=[pltpu.VMEM((2, SL, D // 2), U32), pltpu.VMEM((TM, D // 2), U32),
                        pltpu.SemaphoreType.DMA((3,))],
    )
    return pl.pallas_call(
        kern,
        grid_spec=grid_spec,
        out_shape=jax.ShapeDtypeStruct((n_rows, D // 2), U32),
        compiler_params=_cparams(),
        name="dispatch",
    )(gd, srow, u2)


def _ffn_kernel(te_ref, nv_ref, xs_ref, wg_ref, wu_ref, wd_ref, o_ref, wgb, wub, wdb, sg, su, sd, slot_ref,
                sems):
    i = pl.program_id(0)
    nv = nv_ref[0]
    e = te_ref[i]

    def weight_copies(ex, sl):
        return (pltpu.make_async_copy(wg_ref.at[ex], sg.at[sl], sems.at[sl]),
                pltpu.make_async_copy(wu_ref.at[ex], su.at[sl], sems.at[sl]),
                pltpu.make_async_copy(wd_ref.at[ex], sd.at[sl], sems.at[sl]))

    @pl.when(i == 0)
    def _():
        slot_ref[0] = 0
        for cp in weight_copies(e, 0):
            cp.start()

    @pl.when((i < nv) & ((i == 0) | (e != te_ref[jnp.maximum(i - 1, 0)])))
    def _():
        sl = slot_ref[0]
        for cp in weight_copies(e, sl):
            cp.wait()
        wgb[...] = sg[sl].astype(BF16)
        wub[...] = su[sl].astype(BF16)
        wdb[...] = sd[sl].astype(BF16)
        nxt = lax.while_loop(lambda t: (t < nv) & (te_ref[jnp.minimum(t, nv - 1)] == e), lambda t: t + 1, i + 1)

        @pl.when(nxt < nv)
        def _():
            for cp in weight_copies(te_ref[nxt], 1 - sl):
                cp.start()
        slot_ref[0] = 1 - sl

    @pl.when(i < nv)
    def _():
        hm = xs_ref.shape[0] // 2
        halves = (slice(0, hm), slice(hm, 2 * hm))
        x = [_unpack_halves(xs_ref[r, :]) for r in halves]
        g = [jnp.dot(x[j], wgb[...], preferred_element_type=F32) for j in range(2)]
        u = [jnp.dot(x[j], wub[...], preferred_element_type=F32) for j in range(2)]
        h = [(g[j] * _sigmoid(g[j]) * u[j]).astype(BF16) for j in range(2)]
        y = [jnp.dot(h[j], wdb[...], preferred_element_type=F32) for j in range(2)]
        for j in range(2):
            o_ref[halves[j], :] = _pack_halves(y[j].astype(BF16).astype(F32))

    @pl.when(i >= nv_ref[0])
    def _():
        o_ref[...] = jnp.zeros_like(o_ref)


def _ffn(te, nv, xs, wg, wu, wd, *, TM):
    P, DW = xs.shape
    D = 2 * DW
    n_tiles = P // TM
    grid_spec = pltpu.PrefetchScalarGridSpec(
        num_scalar_prefetch=2,
        grid=(n_tiles,),
        in_specs=[pl.BlockSpec((TM, DW), lambda i, te, nv: (jnp.maximum(jnp.minimum(i, nv[0] - 1), 0), 0)),
                  pl.BlockSpec(memory_space=pl.ANY),
                  pl.BlockSpec(memory_space=pl.ANY),
                  pl.BlockSpec(memory_space=pl.ANY)],
        out_specs=pl.BlockSpec((TM, DW), lambda i, te, nv: (i, 0)),
        scratch_shapes=[pltpu.VMEM((D, D_EXP), BF16), pltpu.VMEM((D, D_EXP), BF16),
                        pltpu.VMEM((D_EXP, D), BF16),
                        pltpu.VMEM((2, D, D_EXP), F32), pltpu.VMEM((2, D, D_EXP), F32),
                        pltpu.VMEM((2, D_EXP, D), F32), pltpu.SMEM((1,), jnp.int32),
                        pltpu.SemaphoreType.DMA((2,))],
    )
    return pl.pallas_call(
        _ffn_kernel,
        grid_spec=grid_spec,
        out_shape=jax.ShapeDtypeStruct((P, DW), U32),
        compiler_params=_cparams(),
        name="ffn",
    )(te, nv, xs, wg, wu, wd)


def _combine_kernel(gd_ref, ys_ref, col_ref, x1_ref, mod_ref, g_ref, b_ref, o_ref, buf, sems, *, NT, SL):
    j = pl.program_id(0)
    slot = j % 2

    def fetch(tile, sl):
        def f(g, carry):
            _granule_copy(ys_ref, pl.multiple_of(gd_ref[tile, g], GRAN), buf.at[sl],
                          pl.multiple_of(g * GRAN, GRAN), sems.at[sl]).start()
            return carry
        lax.fori_loop(0, gd_ref[tile, G_LAST], f, 0)

    @pl.when(j == 0)
    def _():
        fetch(0, 0)

    @pl.when(j + 1 < NT)
    def _():
        fetch(j + 1, 1 - slot)

    ng = gd_ref[j, G_LAST]

    _wait_granules(ng, ys_ref, buf.at[slot], sems.at[slot], SL // GRAN)

    rows = lax.broadcasted_iota(jnp.int32, (SL, 1), 0)
    yb = _unpack_halves(jnp.where(rows < ng * GRAN, buf[slot], U32(0)))
    col = col_ref[...]
    tb = col.shape[0]
    lanes = lax.broadcasted_iota(jnp.int32, (tb, SL), 1).astype(F32)
    wsel = (jnp.where(lanes == col[:, 0:1], col[:, 2:3], 0.0)
            + jnp.where(lanes == col[:, 1:2], col[:, 3:4], 0.0))
    whi = wsel.astype(BF16)
    wlo = (wsel - whi.astype(F32)).astype(BF16)
    y = jnp.dot(whi, yb, preferred_element_type=F32) + jnp.dot(wlo, yb, preferred_element_type=F32)
    mod = mod_ref[0]
    z = ALPHA * x1_ref[...] + (1.0 + mod[5:6, :]) * y
    o_ref[...] = _layer_norm(z, g_ref[...], b_ref[...])


def _combine(gd, ys, col, x1, mod3, g, b, *, S, tb):
    N, D = x1.shape
    NT = N // tb
    tpb = S // tb
    SL = _slots_per_tile(tb)
    kern = functools.partial(_combine_kernel, NT=NT, SL=SL)
    grid_spec = pltpu.PrefetchScalarGridSpec(
        num_scalar_prefetch=1,
        grid=(NT,),
        in_specs=[pl.BlockSpec(memory_space=pl.ANY),
                  pl.BlockSpec((tb, LANES), lambda j, gd: (j, 0)),
                  pl.BlockSpec((tb, D), lambda j, gd: (j, 0)),
                  pl.BlockSpec((1, 6, D), lambda j, gd: (j // tpb, 0, 0)),
                  pl.BlockSpec((1, D), lambda j, gd: (0, 0)),
                  pl.BlockSpec((1, D), lambda j, gd: (0, 0))],
        out_specs=pl.BlockSpec((tb, D), lambda j, gd: (j, 0)),
        scratch_shapes=[pltpu.VMEM((2, SL, D // 2), U32), pltpu.SemaphoreType.DMA((2,))],
    )
    return pl.pallas_call(
        kern,
        grid_spec=grid_spec,
        out_shape=jax.ShapeDtypeStruct((N, D), F32),
        compiler_params=_cparams(),
        name="combine",
    )(gd, ys, col, x1, mod3, g, b)


def _layer(x, c, l, w_ada, b_ada, w_in, w_conv, b_conv, b_igate, b_fgate, mlstm_norm_g, w_gla_a, b_gla_a,
           gla_norm_g, w_out, ln1_g, ln1_b, w_route_group, b_route_group, w_route_expert, b_route_expert,
           w_gate, w_up, w_down, ln2_g, ln2_b):
    B, S, D = x.shape
    N = B * S
    x2 = x.reshape(N, D)
    tm_in = min(512, S)
    tm = min(256, S)
    lm = min(256, S)

    mod3 = _ada(c, w_ada[l], b_ada[l]).reshape(B, 6, D)

    wa_pad = jnp.zeros((LANES, G_KW), F32).at[SM_A:SM_A + G_RANK].set(w_gla_a[l]).astype(BF16)
    bg = (jnp.zeros((2 * SUBLANES, 1), F32).at[0:M_HEADS, 0].set(b_igate[l])
          .at[SUBLANES:SUBLANES + M_HEADS, 0].set(b_fgate[l]))
    oa, la, g3 = _inproj(x2, mod3, w_in[l], w_conv[l], b_conv[l].reshape(1, -1), wa_pad,
                         b_gla_a[l].reshape(1, -1), bg, S=S, tm=tm_in, lm=lm)

    u_tri = jnp.asarray(np.triu(np.ones((lm, lm), np.float32)))
    hm = _mlstm(oa, g3, u_tri, mlstm_norm_g[l].reshape(1, -1), B=B, S=S, L=lm)
    w3_np, mk_np = _gla_consts()
    hg = _gla(oa, la, jnp.asarray(w3_np, BF16), jnp.asarray(mk_np), gla_norm_g[l].reshape(1, -1), B=B, S=S,
              nb=2 if B % 2 == 0 else 1)

    wr = (jnp.zeros((D, LANES), F32).at[:, 0:N_GROUPS].set(w_route_group[l])
          .at[:, SUBLANES:SUBLANES + N_EXP].set(w_route_expert[l]))
    br = (jnp.zeros((1, LANES), F32).at[0, 0:N_GROUPS].set(b_route_group[l])
          .at[0, SUBLANES:SUBLANES + N_EXP].set(b_route_expert[l]))
    wr_hi = wr.astype(BF16)
    wr2 = jnp.concatenate([wr_hi, (wr - wr_hi.astype(F32)).astype(BF16)], axis=1)
    x1, u2, rrow = _outproj(hm, hg, w_out[l], x2, mod3, ln1_g[l].reshape(1, -1),
                            ln1_b[l].reshape(1, -1), wr2, br, S=S, tb=tm, nh=2 if S % (2 * tm) == 0 else 1)

    u_cnt = jnp.asarray(np.triu(np.ones((tm, tm), np.float32)), BF16)
    ltri = jnp.asarray(np.tril(np.ones((N_EXP, N_EXP), np.float32), -1))
    srow, col, gd3, meta = _route(rrow, u_cnt, ltri, TM=FFN_TM)
    gd = gd3.reshape(N // tm + 2, LANES)
    n_tiles = _ffn_tiles(N, tm)
    te, nv = meta[0, :n_tiles], meta[1, 0:1]

    xs = _dispatch(gd, srow, u2, n_tiles=n_tiles, TM=FFN_TM)
    ys = _ffn(te, nv, xs, w_gate[l], w_up[l], w_down[l], TM=FFN_TM)
    out = _combine(gd, ys, col, x1, mod3, ln2_g[l].reshape(1, -1), ln2_b[l].reshape(1, -1), S=S, tb=tm)
    return out.reshape(B, S, D)


def kernel(x, c, w_ada, b_ada, w_in, w_conv, b_conv, b_igate, b_fgate, mlstm_norm_g, w_gla_a, b_gla_a,
           gla_norm_g, w_out, ln1_g, ln1_b, w_route_group, b_route_group, w_route_expert, b_route_expert,
           w_gate, w_up, w_down, ln2_g, ln2_b):
    for l in range(DEPTH):
        x = _layer(x, c, l, w_ada, b_ada, w_in, w_conv, b_conv, b_igate, b_fgate, mlstm_norm_g, w_gla_a,
                   b_gla_a, gla_norm_g, w_out, ln1_g, ln1_b, w_route_group, b_route_group, w_route_expert,
                   b_route_expert, w_gate, w_up, w_down, ln2_g, ln2_b)
    return x
```

```python
import functools

import numpy as np
import jax
import jax.numpy as jnp
from jax import lax
from jax.experimental import pallas as pl
from jax.experimental.pallas import tpu as pltpu

F32 = jnp.float32
BF16 = jnp.bfloat16
HIGHEST = lax.Precision.HIGHEST

DEPTH = 1
M_HEADS = 4
M_HD = 128
M_W = M_HEADS * M_HD
CONV_W = 4
G_HEADS = 4
G_DK = 64
G_DV = 128
G_W = G_HEADS * G_DV
G_KW = G_HEADS * G_DK
G_RANK = 16
G_TAU = 16.0
G_CHUNK = 64
N_GROUPS = 4
E_PER_G = 8
N_EXP = N_GROUPS * E_PER_G
D_EXP = 512
ALPHA = (2 * DEPTH) ** 0.25
LN_EPS = 1e-5

LANES = 128
SUBLANES = 8
VMEM_LIMIT = 48 * 1024 * 1024

C_QK = 0
C_VO = 1024
C_GQK = 2048
C_GV = 2560
C_GG = 3072
C_SMALL = 3584
C_TOT = 3712
SM_I, SM_F, SM_A = 0, 8, 16
IN_GATES = 4 * M_W
IN_G = IN_GATES + 2 * M_HEADS
IN_GA = IN_G + 2 * G_KW + 2 * G_W
IN_TOT = IN_GA + G_RANK

FFN_TM = 256
FFN_CB = 256
GRAN = SUBLANES
G_LAST = LANES - 1


def _cparams(n_axes=1):
    return pltpu.CompilerParams(dimension_semantics=("arbitrary",) * n_axes,
                                vmem_limit_bytes=VMEM_LIMIT)


def _sigmoid(x):
    return 1.0 / (1.0 + jnp.exp(-x))


def _log_sigmoid(x):
    return jnp.minimum(x, 0.0) - jnp.log(1.0 + jnp.exp(-jnp.abs(x)))


def _ada_kernel(c_ref, w_ref, b_ref, o_ref):
    c = c_ref[...]
    ca = c * _sigmoid(c)
    o_ref[...] = jnp.dot(ca, w_ref[...], preferred_element_type=F32, precision=HIGHEST) + b_ref[...]


def _ada(c, w, b):
    B, D = c.shape
    n_out = w.shape[1]
    tn = 1024
    return pl.pallas_call(
        _ada_kernel,
        grid=(n_out // tn,),
        in_specs=[pl.BlockSpec((B, D), lambda j: (0, 0)),
                  pl.BlockSpec((D, tn), lambda j: (0, j)),
                  pl.BlockSpec((1, tn), lambda j: (0, j))],
        out_specs=pl.BlockSpec((B, tn), lambda j: (0, j)),
        out_shape=jax.ShapeDtypeStruct((B, n_out), F32),
        compiler_params=_cparams(),
        name="ada",
    )(c, w, b.reshape(1, n_out))


def _inproj_kernel(x_ref, mod_ref, win_ref, wc_ref, bc_ref, wa_ref, ba_ref, bg_ref,
                   oa_ref, la_ref, g_ref, halo_ref, w_ref, *, tm, tpb, lm):
    i = pl.program_id(0)

    @pl.when(i == 0)
    def _():
        rc = LANES
        for r in range(0, win_ref.shape[0], rc):
            rs = slice(r, r + rc)
            w_ref[rs, 0:IN_GATES] = win_ref[rs, 0:IN_GATES].astype(BF16)
            t = win_ref[rs, IN_GATES:IN_TOT]
            w_ref[rs, C_GQK:C_SMALL] = t[:, IN_G - IN_GATES:IN_GA - IN_GATES].astype(BF16)
            z = lambda n: jnp.zeros((rc, n), F32)
            small = jnp.concatenate([t[:, 0:M_HEADS], z(SM_F - M_HEADS), t[:, M_HEADS:2 * M_HEADS],
                                     z(SM_A - SM_F - M_HEADS), t[:, IN_GA - IN_GATES:IN_TOT - IN_GATES],
                                     z(LANES - SM_A - G_RANK)], axis=1)
            w_ref[rs, C_SMALL:C_TOT] = small.astype(BF16)

    @pl.when(i % tpb == 0)
    def _():
        halo_ref[...] = jnp.zeros_like(halo_ref)

    mod = mod_ref[0]
    u = (x_ref[...] * (1.0 + mod[1:2, :]) + mod[0:1, :]).astype(BF16)

    p = jnp.dot(u, w_ref[:, C_QK:C_QK + 2 * M_W], preferred_element_type=F32)
    ext = jnp.concatenate([halo_ref[...], p], axis=0)
    acc = bc_ref[...] + wc_ref[CONV_W - 1:CONV_W, :] * p
    for j in range(CONV_W - 1):
        sh = pltpu.roll(ext, CONV_W - 1 - j, 0)[SUBLANES:, :]
        acc = acc + wc_ref[j:j + 1, :] * sh
    halo_ref[...] = p[tm - SUBLANES:, :]
    qk = acc * _sigmoid(acc)
    oa_ref[:, C_QK:C_QK + M_W] = qk[:, :M_W].astype(BF16)
    oa_ref[:, C_QK + M_W:C_QK + 2 * M_W] = (qk[:, M_W:] * (M_HD ** -0.5)).astype(BF16)

    p = jnp.dot(u, w_ref[:, C_VO:C_VO + 2 * M_W], preferred_element_type=F32)
    oa_ref[:, C_VO:C_VO + 2 * M_W] = p.astype(BF16)

    p = jnp.dot(u, w_ref[:, C_GQK:C_GQK + G_KW], preferred_element_type=F32)
    oa_ref[:, C_GQK:C_GQK + G_KW] = (p * (G_DK ** -0.5)).astype(BF16)
    p = jnp.dot(u, w_ref[:, C_GQK + G_KW:C_SMALL], preferred_element_type=F32)
    oa_ref[:, C_GQK + G_KW:C_SMALL] = p.astype(BF16)

    ps = jnp.dot(u, w_ref[:, C_SMALL:C_TOT], preferred_element_type=F32)
    la = jnp.dot(ps.astype(BF16), wa_ref[...], preferred_element_type=F32) + ba_ref[...]
    la_ref[...] = _log_sigmoid(la) * (1.0 / G_TAU)
    pt = ps.T
    gi = pt[SM_I:SM_I + SUBLANES, :] + bg_ref[0:SUBLANES, :]
    gf = _log_sigmoid(pt[SM_F:SM_F + SUBLANES, :] + bg_ref[SUBLANES:2 * SUBLANES, :])
    for j in range(tm // lm):
        g_ref[j, 0:SUBLANES, :] = gi[:, j * lm:(j + 1) * lm]
        g_ref[j, SUBLANES:2 * SUBLANES, :] = gf[:, j * lm:(j + 1) * lm]


def _inproj(x2, mod3, w_in, w_conv, b_conv, wa_pad, b_gla, bg, *, S, tm, lm):
    N, D = x2.shape
    tpb = S // tm
    kern = functools.partial(_inproj_kernel, tm=tm, tpb=tpb, lm=lm)
    return pl.pallas_call(
        kern,
        grid=(N // tm,),
        in_specs=[pl.BlockSpec((tm, D), lambda i: (i, 0)),
                  pl.BlockSpec((1, 6, D), lambda i: (i // tpb, 0, 0)),
                  pl.BlockSpec((D, IN_TOT), lambda i: (0, 0), pipeline_mode=pl.Buffered(1)),
                  pl.BlockSpec((CONV_W, 2 * M_W), lambda i: (0, 0)),
                  pl.BlockSpec((1, 2 * M_W), lambda i: (0, 0)),
                  pl.BlockSpec((LANES, G_KW), lambda i: (0, 0)),
                  pl.BlockSpec((1, G_KW), lambda i: (0, 0)),
                  pl.BlockSpec((2 * SUBLANES, 1), lambda i: (0, 0))],
        out_specs=[pl.BlockSpec((tm, C_SMALL), lambda i: (i, 0)),
                   pl.BlockSpec((tm, G_KW), lambda i: (i, 0)),
                   pl.BlockSpec((tm // lm, 2 * SUBLANES, lm), lambda i: (i, 0, 0))],
        out_shape=[jax.ShapeDtypeStruct((N, C_SMALL), BF16),
                   jax.ShapeDtypeStruct((N, G_KW), F32),
                   jax.ShapeDtypeStruct((N // lm, 2 * SUBLANES, lm), F32)],
        scratch_shapes=[pltpu.VMEM((SUBLANES, 2 * M_W), F32), pltpu.VMEM((D, C_TOT), BF16)],
        compiler_params=_cparams(),
        name="inproj",
    )(x2, mod3, w_in, w_conv, b_conv, wa_pad, b_gla, bg)


def _mlstm_kernel(qk_ref, vo_ref, g_ref, u_ref, gain_ref, out_ref, c_ref, zt_ref, a_ref, dec_ref, *, L, NC):
    c_ref[...] = jnp.zeros_like(c_ref)
    lane = lax.broadcasted_iota(jnp.int32, (SUBLANES, L), 1)
    tril = (lax.broadcasted_iota(jnp.int32, (L, L), 0) >= lax.broadcasted_iota(jnp.int32, (L, L), 1))
    ones_v = jnp.ones((L, M_HD), BF16)
    zpad = jnp.zeros((LANES - 4 * SUBLANES, L), F32)

    bs, gs = [], []
    for c in range(NC):
        b = jnp.dot(g_ref[c, SUBLANES:2 * SUBLANES, :], u_ref[...], preferred_element_type=F32,
                    precision=HIGHEST)
        a = g_ref[c, 0:SUBLANES, :] - b
        a_ref[c] = a
        G = a
        s = 1
        while s < L:
            G = jnp.maximum(G, jnp.where(lane >= s, pltpu.roll(G, s, 1), -jnp.inf))
            s *= 2
        bs.append(b)
        gs.append(G)
    m_prev = jnp.zeros((SUBLANES, 1), F32)
    for c in range(NC):
        M = jnp.maximum(gs[c], m_prev)
        ML = M[:, L - 1:L]
        Z = jnp.concatenate([M, jnp.exp(m_prev - M), jnp.exp(-(bs[c] + M)), jnp.exp(a_ref[c] - ML), zpad],
                            axis=0)
        zt_ref[c] = Z.T
        dec_ref[c] = jnp.broadcast_to(jnp.exp(m_prev - ML), (SUBLANES, 2 * M_HD))
        m_prev = bs[c][:, L - 1:L] + ML

    def chunk(c, carry):
        r0 = pl.multiple_of(c * L, L)
        Zt = zt_ref[c]
        a = a_ref[c]
        dec = dec_ref[c]
        rows = pl.ds(r0, L)
        heads = range(M_HEADS)
        hs = [slice(h * M_HD, (h + 1) * M_HD) for h in heads]
        hs2 = [slice(M_W + h * M_HD, M_W + (h + 1) * M_HD) for h in heads]
        q = [qk_ref[rows, hs[h]] for h in heads]
        k = [qk_ref[rows, hs2[h]] for h in heads]
        vext = [jnp.concatenate([vo_ref[rows, hs[h]], ones_v], axis=1) for h in heads]
        cst = [c_ref[h] for h in heads]
        sc = [lax.dot_general(q[h], k[h], (((1,), (1,)), ((), ())), preferred_element_type=F32) for h in heads]
        qc = [jnp.dot(q[h], cst[h].astype(BF16), preferred_element_type=F32) for h in heads]
        pm = [(sc[h] * jnp.exp(jnp.where(tril, a[h:h + 1, :] - Zt[:, h:h + 1], -jnp.inf))).astype(BF16)
              for h in heads]
        pv = [jnp.dot(pm[h], vext[h], preferred_element_type=F32) for h in heads]
        kw = [(Zt[:, 3 * SUBLANES + h:3 * SUBLANES + h + 1] * k[h].astype(F32)).astype(BF16) for h in heads]
        upd = [lax.dot_general(kw[h], vext[h], (((0,), (0,)), ((), ())), preferred_element_type=F32)
               for h in heads]
        for h in heads:
            c_ref[h] = dec[h:h + 1, :] * cst[h] + upd[h]
            nd = pv[h] + Zt[:, SUBLANES + h:SUBLANES + h + 1] * qc[h]
            hh = nd[:, :M_HD] / jnp.maximum(jnp.abs(nd[:, M_HD:]), Zt[:, 2 * SUBLANES + h:2 * SUBLANES + h + 1])
            hh = _sigmoid(vo_ref[rows, hs2[h]].astype(F32)) * hh
            hn = hh * lax.rsqrt(jnp.mean(hh * hh, axis=-1, keepdims=True) + LN_EPS)
            out_ref[rows, hs[h]] = (hn * gain_ref[:, hs[h]]).astype(BF16)
        return carry

    lax.fori_loop(0, NC, chunk, 0)


def _mlstm(oa, g3, u_tri, gain, *, B, S, L):
    N = oa.shape[0]
    NC = S // L
    kern = functools.partial(_mlstm_kernel, L=L, NC=NC)
    return pl.pallas_call(
        kern,
        grid=(B,),
        in_specs=[pl.BlockSpec((S, 2 * M_W), lambda b: (b, C_QK // (2 * M_W))),
                  pl.BlockSpec((S, 2 * M_W), lambda b: (b, C_VO // (2 * M_W))),
                  pl.BlockSpec((NC, 2 * SUBLANES, L), lambda b: (b, 0, 0)),
                  pl.BlockSpec((L, L), lambda b: (0, 0)),
                  pl.BlockSpec((1, M_W), lambda b: (0, 0))],
        out_specs=pl.BlockSpec((S, M_W), lambda b: (b, 0)),
        out_shape=jax.ShapeDtypeStruct((N, M_W), BF16),
        scratch_shapes=[pltpu.VMEM((M_HEADS, M_HD, 2 * M_HD), F32), pltpu.VMEM((NC, L, LANES), F32),
                        pltpu.VMEM((NC, SUBLANES, L), F32), pltpu.VMEM((NC, SUBLANES, 2 * M_HD), F32)],
        compiler_params=_cparams(),
        name="mlstm",
    )(oa, oa, g3, u_tri, gain)


_G_LEVELS = 6
_G_XROW = 2 * G_CHUNK + SUBLANES


def _gla_consts():
    L = G_CHUNK
    t = np.arange(L)
    blocks = [(t[None, :] <= t[:, None]).astype(np.float32),
              (t[None, :] > t[:, None]).astype(np.float32),
              np.ones((SUBLANES, L), np.float32)]
    masks = [np.eye(L, dtype=np.float32)]
    m = 1
    while m < L:
        wl = np.zeros((L, L), np.float32)
        for r in range(L):
            r0 = (r // (2 * m)) * 2 * m + m
            if r % (2 * m) >= m:
                wl[r, r0:r + 1] = 1.0
            else:
                wl[r, r + 1:r0] = 1.0
        blocks.append(wl)
        tt, ss = t[:, None], t[None, :]
        masks.append(((tt // (2 * m) == ss // (2 * m)) & (tt % (2 * m) >= m)
                      & (ss % (2 * m) < m)).astype(np.float32))
        m *= 2
    w = np.concatenate(blocks, axis=0)
    w3 = np.concatenate([w, w, w], axis=1)
    mk = np.stack([np.concatenate([x] * G_HEADS, axis=0) for x in masks])
    return w3, mk


def _gla_kernel(qk_ref, v_ref, gg_ref, la_ref, w3_ref, mk_ref, gain_ref, out_ref, st_ref, *, NC, S, nb):
    L = G_CHUNK
    st_ref[...] = jnp.zeros_like(st_ref)
    lane_head = lax.broadcasted_iota(jnp.int32, (L, G_KW), 1) // G_DK
    br = lax.broadcasted_iota(jnp.int32, (2 * G_DV, LANES), 0) < G_DV
    bl = lax.broadcasted_iota(jnp.int32, (2 * G_DV, LANES), 1) < G_DK
    bmask = br == bl
    nt = (((1,), (1,)), ((), ()))
    tn = (((0,), (0,)), ((), ()))

    def chunk(c, carry):
        rows = [pl.ds(pl.multiple_of(bi * S + c * L, L), L) for bi in range(nb)]
        X, q, k = [], [], []
        for bi in range(nb):
            la = la_ref[rows[bi], :]
            hi = la.astype(BF16)
            r1 = la - hi.astype(F32)
            mid = r1.astype(BF16)
            lo = (r1 - mid.astype(F32)).astype(BF16)
            stk = jnp.concatenate([hi, mid, lo], axis=0)
            X.append(jnp.exp(jnp.dot(w3_ref[...], stk, preferred_element_type=F32)))
            q.append(qk_ref[rows[bi], 0:G_KW].astype(F32))
            k.append(qk_ref[rows[bi], G_KW:2 * G_KW].astype(F32))

        sc = [[None] * (_G_LEVELS + 1) for _ in range(nb)]
        for lev in range(_G_LEVELS + 1):
            for bi in range(nb):
                if lev == 0:
                    qt, kt = q[bi], k[bi]
                else:
                    xl = X[bi][_G_XROW + L * (lev - 1):_G_XROW + L * lev, :]
                    qt, kt = q[bi] * xl, k[bi] * xl
                q4 = jnp.concatenate([jnp.where(lane_head == h, qt, 0.0) for h in range(G_HEADS)],
                                     axis=0).astype(BF16)
                sc[bi][lev] = lax.dot_general(q4, kt.astype(BF16), nt, preferred_element_type=F32)
        Ab = []
        for bi in range(nb):
            A = sc[bi][0] * mk_ref[0]
            for lev in range(1, _G_LEVELS + 1):
                A = A + sc[bi][lev] * mk_ref[lev]
            Ab.append(A.astype(BF16))

        for bi in range(nb):
            gg = gg_ref[rows[bi], :].astype(F32)
            gate = gg * _sigmoid(gg)
            for p in range(2):
                ls = slice(LANES * p, LANES * (p + 1))
                vp = v_ref[rows[bi], 2 * G_DV * p:2 * G_DV * (p + 1)]
                oi = [jnp.dot(Ab[bi][L * (2 * p + hh):L * (2 * p + hh + 1)],
                              vp[:, G_DV * hh:G_DV * (hh + 1)], preferred_element_type=F32)
                      for hh in range(2)]
                st = st_ref[bi, p]
                qc = (q[bi][:, ls] * X[bi][0:L, ls]).astype(BF16)
                o_inter = lax.dot_general(qc, st.astype(BF16), nt, preferred_element_type=F32)
                kc = (k[bi][:, ls] * X[bi][L:2 * L, ls]).astype(BF16)
                upd = lax.dot_general(vp, kc, tn, preferred_element_type=F32)
                dec = X[bi][2 * L:2 * L + 1, ls]
                st_ref[bi, p] = jnp.where(bmask, dec * st + upd, 0.0)
                for hh in range(2):
                    o = o_inter[:, G_DV * hh:G_DV * (hh + 1)] + oi[hh]
                    hn = o * lax.rsqrt(jnp.mean(o * o, axis=-1, keepdims=True) + LN_EPS)
                    hs = slice(G_DV * (2 * p + hh), G_DV * (2 * p + hh + 1))
                    out_ref[rows[bi], hs] = (hn * gain_ref[:, hs] * gate[:, hs]).astype(BF16)
        return carry

    lax.fori_loop(0, NC, chunk, 0)


def _gla(oa, la, w3, mk, gain, *, B, S, nb):
    N = oa.shape[0]
    NC = S // G_CHUNK
    kern = functools.partial(_gla_kernel, NC=NC, S=S, nb=nb)
    R = nb * S
    return pl.pallas_call(
        kern,
        grid=(B // nb,),
        in_specs=[pl.BlockSpec((R, 2 * G_KW), lambda b: (b, C_GQK // (2 * G_KW))),
                  pl.BlockSpec((R, G_W), lambda b: (b, C_GV // G_W)),
                  pl.BlockSpec((R, G_W), lambda b: (b, C_GG // G_W)),
                  pl.BlockSpec((R, G_KW), lambda b: (b, 0)),
                  pl.BlockSpec(w3.shape, lambda b: (0, 0)),
                  pl.BlockSpec(mk.shape, lambda b: (0, 0, 0)),
                  pl.BlockSpec((1, G_W), lambda b: (0, 0))],
        out_specs=pl.BlockSpec((R, G_W), lambda b: (b, 0)),
        out_shape=jax.ShapeDtypeStruct((N, G_W), BF16),
        scratch_shapes=[pltpu.VMEM((nb, 2, 2 * G_DV, LANES), F32)],
        compiler_params=_cparams(),
        name="gla",
    )(oa, oa, oa, la, w3, mk, gain)


def _layer_norm(z, g, b):
    mu = jnp.mean(z, axis=-1, keepdims=True)
    zc = z - mu
    var = jnp.mean(zc * zc, axis=-1, keepdims=True)
    return zc * lax.rsqrt(var + LN_EPS) * g + b


def _outproj_kernel(hm_ref, hg_ref, wf_ref, x_ref, mod_ref, g_ref, b_ref, wr_ref, br_ref,
                    x1_ref, u2_ref, rrow_ref, w_ref, *, tb, nh):
    @pl.when(pl.program_id(0) == 0)
    def _():
        w_ref[...] = wf_ref[...].astype(BF16)

    mod = mod_ref[0]
    blocks = [slice(tb * j, tb * (j + 1)) for j in range(nh)]
    y = [jnp.dot(hm_ref[r, :], w_ref[0:M_W, :], preferred_element_type=F32)
         + jnp.dot(hg_ref[r, :], w_ref[M_W:M_W + G_W, :], preferred_element_type=F32) for r in blocks]
    u2 = []
    for j, r in enumerate(blocks):
        z = ALPHA * x_ref[r, :] + (1.0 + mod[2:3, :]) * y[j]
        x1 = _layer_norm(z, g_ref[...], b_ref[...])
        x1_ref[r, :] = x1
        u2.append(x1 * (1.0 + mod[4:5, :]) + mod[3:4, :])
        u2_ref[r, :] = u2[j].astype(BF16)

    u2h = [u.astype(BF16) for u in u2]
    u2l = [(u2[j] - u2h[j].astype(F32)).astype(BF16) for j in range(nh)]
    lh = [jnp.dot(u, wr_ref[...], preferred_element_type=F32) for u in u2h]
    ll = [jnp.dot(u, wr_ref[:, 0:LANES], preferred_element_type=F32) for u in u2l]
    for j in range(nh):
        logits = lh[j][:, 0:LANES] + lh[j][:, LANES:2 * LANES] + ll[j] + br_ref[...]
        rrow_ref[j] = _route_select(logits.T, tb)


def _route_select(lt, tm):
    row = lax.broadcasted_iota(jnp.int32, (SUBLANES, tm), 0)
    gl = jnp.where(row < N_GROUPS, lt[0:SUBLANES, :], -jnp.inf)
    gmax = jnp.max(gl, axis=0, keepdims=True)
    gsel = jnp.min(jnp.where(gl == gmax, row, SUBLANES), axis=0, keepdims=True)
    pg = 1.0 / jnp.sum(jnp.exp(gl - gmax), axis=0, keepdims=True)
    ein = jnp.zeros((SUBLANES, tm), F32)
    for g in range(N_GROUPS):
        ein = jnp.where(gsel == g, lt[SUBLANES * (g + 1):SUBLANES * (g + 2), :], ein)
    v1 = jnp.max(ein, axis=0, keepdims=True)
    i1 = jnp.min(jnp.where(ein == v1, row, SUBLANES), axis=0, keepdims=True)
    rest = jnp.where(row == i1, -jnp.inf, ein)
    v2 = jnp.max(rest, axis=0, keepdims=True)
    i2 = jnp.min(jnp.where(rest == v2, row, SUBLANES), axis=0, keepdims=True)
    t2 = jnp.exp(v2 - v1)
    p1 = 1.0 / (1.0 + t2)
    e0 = (gsel * E_PER_G + i1).astype(F32)
    e1 = (gsel * E_PER_G + i2).astype(F32)
    return jnp.concatenate([e0, e1, pg * p1, pg * (t2 * p1), jnp.zeros((SUBLANES - 4, tm), F32)], axis=0)


def _outproj(hm, hg, w_out, x2, mod3, g, b, wr, br, *, S, tb, nh):
    N, D = x2.shape
    tm = tb * nh
    tpb = S // tm
    kern = functools.partial(_outproj_kernel, tb=tb, nh=nh)
    return pl.pallas_call(
        kern,
        grid=(N // tm,),
        in_specs=[pl.BlockSpec((tm, M_W), lambda i: (i, 0)),
                  pl.BlockSpec((tm, G_W), lambda i: (i, 0)),
                  pl.BlockSpec((M_W + G_W, D), lambda i: (0, 0), pipeline_mode=pl.Buffered(1)),
                  pl.BlockSpec((tm, D), lambda i: (i, 0)),
                  pl.BlockSpec((1, 6, D), lambda i: (i // tpb, 0, 0)),
                  pl.BlockSpec((1, D), lambda i: (0, 0)),
                  pl.BlockSpec((1, D), lambda i: (0, 0)),
                  pl.BlockSpec((D, 2 * LANES), lambda i: (0, 0)),
                  pl.BlockSpec((1, LANES), lambda i: (0, 0))],
        out_specs=[pl.BlockSpec((tm, D), lambda i: (i, 0)),
                   pl.BlockSpec((tm, D), lambda i: (i, 0)),
                   pl.BlockSpec((nh, SUBLANES, tb), lambda i: (i, 0, 0))],
        out_shape=[jax.ShapeDtypeStruct((N, D), F32),
                   jax.ShapeDtypeStruct((N, D), BF16),
                   jax.ShapeDtypeStruct((N // tb, SUBLANES, tb), F32)],
        scratch_shapes=[pltpu.VMEM((M_W + G_W, D), BF16)],
        compiler_params=_cparams(),
        name="outproj",
    )(hm, hg, w_out, x2, mod3, g, b, wr, br)


def _slots_per_tile(tb):
    worst = 2 * tb + N_EXP * (GRAN - 1)
    return -(-worst // LANES) * LANES


def _ffn_tiles(n_tok, tb):
    worst_rows = 2 * n_tok + (n_tok // tb) * N_EXP * (GRAN - 1)
    return -(-worst_rows // FFN_TM) + N_EXP


def _route_kernel(rr_ref, u_ref, lt_ref, srow_ref, col_ref, gd_ref, meta_ref, mg_ref, part_ref,
                  *, NT, tb, TM):
    iota_e = lax.broadcasted_iota(jnp.int32, (N_EXP, tb), 0).astype(F32)
    glane = lax.broadcasted_iota(jnp.int32, (N_EXP, LANES), 1).astype(F32)
    ltri = lt_ref[...]

    def prefix_e(col):
        return jnp.dot(ltri, jnp.broadcast_to(col, (N_EXP, LANES)),
                       preferred_element_type=F32, precision=HIGHEST)[:, 0:1]

    def p1(j, run8):
        r = rr_ref[j]
        oh0 = jnp.where(iota_e == r[0:1, :], 1.0, 0.0)
        oh1 = jnp.where(iota_e == r[1:2, :], 1.0, 0.0)
        cum0 = jnp.dot(oh0.astype(BF16), u_ref[...], preferred_element_type=F32)
        cum1 = jnp.dot(oh1.astype(BF16), u_ref[...], preferred_element_type=F32)
        c0 = jnp.sum(oh0, axis=1, keepdims=True)
        n8 = jnp.floor((c0 + jnp.sum(oh1, axis=1, keepdims=True) + (GRAN - 1.0)) * (1.0 / GRAN))
        lo8 = prefix_e(n8)
        s0 = jnp.sum(oh0 * (GRAN * lo8 + cum0 - 1.0), axis=0, keepdims=True)
        s1 = jnp.sum(oh1 * (GRAN * lo8 + c0 + cum1 - 1.0), axis=0, keepdims=True)
        info = jnp.concatenate([s0, s1, r[2:4, :], jnp.zeros((SUBLANES - 4, tb), F32)], axis=0)
        srow_ref[j] = info
        col_ref[pl.ds(pl.multiple_of(j * tb, tb), tb), :] = jnp.concatenate(
            [info, jnp.zeros((LANES - SUBLANES, tb), F32)], axis=0).T
        mg = jnp.where((lo8 <= glane) & (glane < lo8 + n8), 1.0, 0.0)
        mg_ref[j] = mg
        part = jnp.sum(mg * (run8 + glane - lo8), axis=0, keepdims=True)
        gcnt = jnp.broadcast_to(jnp.sum(n8, axis=0, keepdims=True), (1, LANES))
        part_ref[j] = jnp.concatenate([part, gcnt, jnp.zeros((SUBLANES - 2, LANES), F32)], axis=0)
        return run8 + n8

    tot8 = lax.fori_loop(0, NT, p1, jnp.zeros((N_EXP, 1), F32), unroll=4 if NT % 4 == 0 else 1)
    seg_t = jnp.floor((tot8 * GRAN + (TM - 1.0)) * (1.0 / TM))
    base_t = prefix_e(seg_t)
    base8 = base_t * (TM // GRAN)
    lane1 = lax.broadcasted_iota(jnp.int32, (1, LANES), 1)

    def p2(j, carry):
        pr = part_ref[j]
        dst = (pr[0:1, :] + jnp.sum(mg_ref[j] * base8, axis=0, keepdims=True)) * GRAN
        gd_ref[j] = jnp.where(lane1 == G_LAST, pr[1:2, :], dst).astype(jnp.int32)
        return carry

    lax.fori_loop(0, NT, p2, 0, unroll=4 if NT % 4 == 0 else 1)
    eye = jnp.where(glane == lax.broadcasted_iota(jnp.int32, (N_EXP, LANES), 0).astype(F32), 1.0, 0.0)
    tail_row = jnp.sum(eye * ((base8 + tot8) * GRAN), axis=0, keepdims=True)
    tail_n8 = jnp.sum(eye * (seg_t * (TM // GRAN) - tot8), axis=0, keepdims=True)
    nv_l = jnp.broadcast_to(jnp.sum(seg_t, axis=0, keepdims=True), (1, LANES))
    gd_ref[NT] = jnp.where(lane1 == G_LAST, nv_l, tail_row).astype(jnp.int32)
    gd_ref[NT + 1] = tail_n8.astype(jnp.int32)
    ti = lax.broadcasted_iota(jnp.int32, (N_EXP, tb), 1).astype(F32)
    te = jnp.sum(jnp.where(base_t <= ti, 1.0, 0.0), axis=0, keepdims=True) - 1.0
    nv = jnp.broadcast_to(jnp.sum(seg_t, axis=0, keepdims=True), (1, tb))
    meta_ref[...] = jnp.concatenate([te, nv, jnp.zeros((SUBLANES - 2, tb), F32)],
                                    axis=0).astype(jnp.int32)


def _route(rrow, u_cnt, ltri, *, TM):
    NT, _, tb = rrow.shape
    kern = functools.partial(_route_kernel, NT=NT, tb=tb, TM=TM)
    full3 = lambda i: (0, 0, 0)
    return pl.pallas_call(
        kern,
        grid=(1,),
        in_specs=[pl.BlockSpec((NT, SUBLANES, tb), full3),
                  pl.BlockSpec((tb, tb), lambda i: (0, 0)),
                  pl.BlockSpec((N_EXP, N_EXP), lambda i: (0, 0))],
        out_specs=[pl.BlockSpec((NT, SUBLANES, tb), full3),
                   pl.BlockSpec((NT * tb, LANES), lambda i: (0, 0)),
                   pl.BlockSpec((NT + 2, 1, LANES), full3),
                   pl.BlockSpec((SUBLANES, tb), lambda i: (0, 0))],
        out_shape=[jax.ShapeDtypeStruct((NT, SUBLANES, tb), F32),
                   jax.ShapeDtypeStruct((NT * tb, LANES), F32),
                   jax.ShapeDtypeStruct((NT + 2, 1, LANES), jnp.int32),
                   jax.ShapeDtypeStruct((SUBLANES, tb), jnp.int32)],
        scratch_shapes=[pltpu.VMEM((NT, N_EXP, LANES), F32), pltpu.VMEM((NT, SUBLANES, LANES), F32)],
        compiler_params=_cparams(),
        name="route",
    )(rrow, u_cnt, ltri)


U32 = jnp.uint32
_HI_MASK = 0xFFFF0000


def _pack_halves(x):
    c = x.shape[1] // 2
    lo = lax.bitcast_convert_type(x[:, :c], U32)
    hi = lax.bitcast_convert_type(x[:, c:], U32)
    return (lo >> 16) | (hi & U32(_HI_MASK))


def _unpack_halves(w):
    lo = lax.bitcast_convert_type(w << 16, F32)
    hi = lax.bitcast_convert_type(w & U32(_HI_MASK), F32)
    return jnp.concatenate([lo, hi], axis=1).astype(BF16)


def _granule_copy(src_ref, src_row, dst_ref, dst_row, sem):
    return pltpu.make_async_copy(src_ref.at[pl.ds(src_row, GRAN), :], dst_ref.at[pl.ds(dst_row, GRAN), :], sem)


def _wait_granules(n, src_ref, dst_ref, sem, n_max):
    b = 1
    while b <= n_max:
        @pl.when((n & b) != 0)
        def _(b=b):
            pltpu.make_async_copy(src_ref.at[pl.ds(0, b * GRAN), :], dst_ref.at[pl.ds(0, b * GRAN), :],
                                  sem).wait()
        b *= 2


def _dispatch_kernel(gd_ref, srow_ref, u_ref, xs_ref, buf, zbuf, sems, *, NT, SL, TM, n_tiles):
    j = pl.program_id(0)
    slot = j % 2
    zsem = sems.at[2]

    def drain(tile, sl):
        _wait_granules(gd_ref[tile, G_LAST], buf.at[sl], xs_ref, sems.at[sl], SL // GRAN)

    def tile_fill(t):
        return pltpu.make_async_copy(zbuf, xs_ref.at[pl.ds(pl.multiple_of(t * TM, TM), TM), :], zsem)

    def zero_fill(wait):
        for e in range(N_EXP):
            def zg(g, carry, e=e):
                cp = _granule_copy(zbuf, 0, xs_ref, pl.multiple_of(gd_ref[NT, e] + g * GRAN, GRAN), zsem)
                cp.wait() if wait else cp.start()
                return carry
            lax.fori_loop(0, gd_ref[NT + 1, e], zg, 0)

        def zt(t, carry):
            tile_fill(t).wait() if wait else tile_fill(t).start()
            return carry
        lax.fori_loop(gd_ref[NT, G_LAST], n_tiles, zt, 0)

    @pl.when(j == 0)
    def _():
        zbuf[...] = jnp.zeros_like(zbuf)
        zero_fill(False)

    @pl.when(j >= 2)
    def _():
        drain(j - 2, slot)

    s = srow_ref[0]
    rows = lax.broadcasted_iota(jnp.int32, (SL, s.shape[1]), 0).astype(F32)
    m0 = rows == s[0:1, :]
    m1 = rows == s[1:2, :]
    oh = jnp.where(m0 | m1, 1.0, 0.0).astype(BF16)
    dw = u_ref.shape[1] // 2
    buf[slot, :, 0:dw] = _pack_halves(jnp.dot(oh, u_ref[...], preferred_element_type=F32))
    wrow = jnp.sum(jnp.where(m0, s[2:3, :], 0.0) + jnp.where(m1, s[3:4, :], 0.0), axis=1, keepdims=True)
    buf[slot, :, dw:dw + LANES] = lax.bitcast_convert_type(jnp.broadcast_to(wrow, (SL, LANES)), U32)

    def issue(g, carry):
        _granule_copy(buf.at[slot], pl.multiple_of(g * GRAN, GRAN), xs_ref,
                      pl.multiple_of(gd_ref[j, g], GRAN), sems.at[slot]).start()
        return carry

    lax.fori_loop(0, gd_ref[j, G_LAST], issue, 0)

    @pl.when(j == NT - 1)
    def _():
        drain(j, slot)
        if NT > 1:
            drain(j - 1, 1 - slot)
        zero_fill(True)


def _dispatch(gd, srow, u2, *, n_tiles, TM):
    N, D = u2.shape
    NT, _, tb = srow.shape
    SL = _slots_per_tile(tb)
    n_rows = n_tiles * TM
    kern = functools.partial(_dispatch_kernel, NT=NT, SL=SL, TM=TM, n_tiles=n_tiles)
    grid_spec = pltpu.PrefetchScalarGridSpec(
        num_scalar_prefetch=1,
        grid=(NT,),
        in_specs=[pl.BlockSpec((1, SUBLANES, tb), lambda j, gd: (j, 0, 0)),
                  pl.BlockSpec((tb, D), lambda j, gd: (j, 0))],
        out_specs=pl.BlockSpec(memory_space=pl.ANY),
        scratch_shapes=[pltpu.VMEM((2, SL, D // 2 + LANES), U32), pltpu.VMEM((TM, D // 2 + LANES), U32),
                        pltpu.SemaphoreType.DMA((3,))],
    )
    return pl.pallas_call(
        kern,
        grid_spec=grid_spec,
        out_shape=jax.ShapeDtypeStruct((n_rows, D // 2 + LANES), U32),
        compiler_params=_cparams(),
        name="dispatch",
    )(gd, srow, u2)


def _ffn_kernel(te_ref, nv_ref, xs_ref, wg_ref, wu_ref, wd_ref, o_ref, wgb, wub, wdb, sg, su, sd, slot_ref,
                sems):
    i = pl.program_id(0)
    nv = nv_ref[0]
    e = te_ref[i]

    def weight_copies(ex, sl):
        return (pltpu.make_async_copy(wg_ref.at[ex], sg.at[sl], sems.at[sl]),
                pltpu.make_async_copy(wu_ref.at[ex], su.at[sl], sems.at[sl]),
                pltpu.make_async_copy(wd_ref.at[ex], sd.at[sl], sems.at[sl]))

    @pl.when(i == 0)
    def _():
        slot_ref[0] = 0
        for cp in weight_copies(e, 0):
            cp.start()

    @pl.when((i < nv) & ((i == 0) | (e != te_ref[jnp.maximum(i - 1, 0)])))
    def _():
        sl = slot_ref[0]
        for cp in weight_copies(e, sl):
            cp.wait()
        wgb[...] = sg[sl].astype(BF16)
        wub[...] = su[sl].astype(BF16)
        wdb[...] = sd[sl].astype(BF16)
        nxt = lax.while_loop(lambda t: (t < nv) & (te_ref[jnp.minimum(t, nv - 1)] == e), lambda t: t + 1, i + 1)

        @pl.when(nxt < nv)
        def _():
            for cp in weight_copies(te_ref[nxt], 1 - sl):
                cp.start()
        slot_ref[0] = 1 - sl

    @pl.when(i < nv)
    def _():
        hm = xs_ref.shape[0] // 2
        dw = o_ref.shape[1]
        halves = (slice(0, hm), slice(hm, 2 * hm))
        x = [_unpack_halves(xs_ref[r, 0:dw]) for r in halves]
        g = [jnp.dot(x[j], wgb[...], preferred_element_type=F32) for j in range(2)]
        u = [jnp.dot(x[j], wub[...], preferred_element_type=F32) for j in range(2)]
        h = [(g[j] * _sigmoid(g[j]) * u[j]).astype(BF16) for j in range(2)]
        y = [jnp.dot(h[j], wdb[...], preferred_element_type=F32) for j in range(2)]
        for j in range(2):
            wt = lax.bitcast_convert_type(xs_ref[halves[j], dw:dw + LANES], F32)
            yw = y[j] * jnp.concatenate([wt] * (2 * dw // LANES), axis=1)
            o_ref[halves[j], :] = _pack_halves(yw.astype(BF16).astype(F32))

    @pl.when(i >= nv_ref[0])
    def _():
        o_ref[...] = jnp.zeros_like(o_ref)


def _ffn(te, nv, xs, wg, wu, wd, *, TM):
    P, XW = xs.shape
    DW = XW - LANES
    D = 2 * DW
    n_tiles = P // TM
    grid_spec = pltpu.PrefetchScalarGridSpec(
        num_scalar_prefetch=2,
        grid=(n_tiles,),
        in_specs=[pl.BlockSpec((TM, XW), lambda i, te, nv: (jnp.maximum(jnp.minimum(i, nv[0] - 1), 0), 0)),
                  pl.BlockSpec(memory_space=pl.ANY),
                  pl.BlockSpec(memory_space=pl.ANY),
                  pl.BlockSpec(memory_space=pl.ANY)],
        out_specs=pl.BlockSpec((TM, DW), lambda i, te, nv: (i, 0)),
        scratch_shapes=[pltpu.VMEM((D, D_EXP), BF16), pltpu.VMEM((D, D_EXP), BF16),
                        pltpu.VMEM((D_EXP, D), BF16),
                        pltpu.VMEM((2, D, D_EXP), F32), pltpu.VMEM((2, D, D_EXP), F32),
                        pltpu.VMEM((2, D_EXP, D), F32), pltpu.SMEM((1,), jnp.int32),
                        pltpu.SemaphoreType.DMA((2,))],
    )
    return pl.pallas_call(
        _ffn_kernel,
        grid_spec=grid_spec,
        out_shape=jax.ShapeDtypeStruct((P, DW), U32),
        compiler_params=_cparams(),
        name="ffn",
    )(te, nv, xs, wg, wu, wd)


def _combine_kernel(gd_ref, ys_ref, col_ref, x1_ref, mod_ref, g_ref, b_ref, o_ref, buf, sems, *, NT, SL):
    j = pl.program_id(0)
    slot = j % 2

    def fetch(tile, sl):
        def f(g, carry):
            _granule_copy(ys_ref, pl.multiple_of(gd_ref[tile, g], GRAN), buf.at[sl],
                          pl.multiple_of(g * GRAN, GRAN), sems.at[sl]).start()
            return carry
        lax.fori_loop(0, gd_ref[tile, G_LAST], f, 0)

    @pl.when(j == 0)
    def _():
        fetch(0, 0)

    @pl.when(j + 1 < NT)
    def _():
        fetch(j + 1, 1 - slot)

    ng = gd_ref[j, G_LAST]

    _wait_granules(ng, ys_ref, buf.at[slot], sems.at[slot], SL // GRAN)

    rows = lax.broadcasted_iota(jnp.int32, (SL, 1), 0)
    yb = _unpack_halves(jnp.where(rows < ng * GRAN, buf[slot], U32(0)))
    col = col_ref[...]
    tb = col.shape[0]
    lanes = lax.broadcasted_iota(jnp.int32, (tb, SL), 1).astype(F32)
    sel = jnp.where((lanes == col[:, 0:1]) | (lanes == col[:, 1:2]), 1.0, 0.0).astype(BF16)
    y = jnp.dot(sel, yb, preferred_element_type=F32)
    mod = mod_ref[0]
    z = ALPHA * x1_ref[...] + (1.0 + mod[5:6, :]) * y
    o_ref[...] = _layer_norm(z, g_ref[...], b_ref[...])


def _combine(gd, ys, col, x1, mod3, g, b, *, S, tb):
    N, D = x1.shape
    NT = N // tb
    tpb = S // tb
    SL = _slots_per_tile(tb)
    kern = functools.partial(_combine_kernel, NT=NT, SL=SL)
    grid_spec = pltpu.PrefetchScalarGridSpec(
        num_scalar_prefetch=1,
        grid=(NT,),
        in_specs=[pl.BlockSpec(memory_space=pl.ANY),
                  pl.BlockSpec((tb, LANES), lambda j, gd: (j, 0)),
                  pl.BlockSpec((tb, D), lambda j, gd: (j, 0)),
                  pl.BlockSpec((1, 6, D), lambda j, gd: (j // tpb, 0, 0)),
                  pl.BlockSpec((1, D), lambda j, gd: (0, 0)),
                  pl.BlockSpec((1, D), lambda j, gd: (0, 0))],
        out_specs=pl.BlockSpec((tb, D), lambda j, gd: (j, 0)),
        scratch_shapes=[pltpu.VMEM((2, SL, D // 2), U32), pltpu.SemaphoreType.DMA((2,))],
    )
    return pl.pallas_call(
        kern,
        grid_spec=grid_spec,
        out_shape=jax.ShapeDtypeStruct((N, D), F32),
        compiler_params=_cparams(),
        name="combine",
    )(gd, ys, col, x1, mod3, g, b)


def _layer(x, c, l, w_ada, b_ada, w_in, w_conv, b_conv, b_igate, b_fgate, mlstm_norm_g, w_gla_a, b_gla_a,
           gla_norm_g, w_out, ln1_g, ln1_b, w_route_group, b_route_group, w_route_expert, b_route_expert,
           w_gate, w_up, w_down, ln2_g, ln2_b):
    B, S, D = x.shape
    N = B * S
    x2 = x.reshape(N, D)
    tm_in = min(512, S)
    tm = min(256, S)
    lm = min(256, S)

    mod3 = _ada(c, w_ada[l], b_ada[l]).reshape(B, 6, D)

    wa_pad = jnp.zeros((LANES, G_KW), F32).at[SM_A:SM_A + G_RANK].set(w_gla_a[l]).astype(BF16)
    bg = (jnp.zeros((2 * SUBLANES, 1), F32).at[0:M_HEADS, 0].set(b_igate[l])
          .at[SUBLANES:SUBLANES + M_HEADS, 0].set(b_fgate[l]))
    oa, la, g3 = _inproj(x2, mod3, w_in[l], w_conv[l], b_conv[l].reshape(1, -1), wa_pad,
                         b_gla_a[l].reshape(1, -1), bg, S=S, tm=tm_in, lm=lm)

    u_tri = jnp.asarray(np.triu(np.ones((lm, lm), np.float32)))
    hm = _mlstm(oa, g3, u_tri, mlstm_norm_g[l].reshape(1, -1), B=B, S=S, L=lm)
    w3_np, mk_np = _gla_consts()
    hg = _gla(oa, la, jnp.asarray(w3_np, BF16), jnp.asarray(mk_np), gla_norm_g[l].reshape(1, -1), B=B, S=S,
              nb=2 if B % 2 == 0 else 1)

    wr = (jnp.zeros((D, LANES), F32).at[:, 0:N_GROUPS].set(w_route_group[l])
          .at[:, SUBLANES:SUBLANES + N_EXP].set(w_route_expert[l]))
    br = (jnp.zeros((1, LANES), F32).at[0, 0:N_GROUPS].set(b_route_group[l])
          .at[0, SUBLANES:SUBLANES + N_EXP].set(b_route_expert[l]))
    wr_hi = wr.astype(BF16)
    wr2 = jnp.concatenate([wr_hi, (wr - wr_hi.astype(F32)).astype(BF16)], axis=1)
    x1, u2, rrow = _outproj(hm, hg, w_out[l], x2, mod3, ln1_g[l].reshape(1, -1),
                            ln1_b[l].reshape(1, -1), wr2, br, S=S, tb=tm, nh=2 if S % (2 * tm) == 0 else 1)

    u_cnt = jnp.asarray(np.triu(np.ones((tm, tm), np.float32)), BF16)
    ltri = jnp.asarray(np.tril(np.ones((N_EXP, N_EXP), np.float32), -1))
    srow, col, gd3, meta = _route(rrow, u_cnt, ltri, TM=FFN_TM)
    gd = gd3.reshape(N // tm + 2, LANES)
    n_tiles = _ffn_tiles(N, tm)
    te, nv = meta[0, :n_tiles], meta[1, 0:1]

    xs = _dispatch(gd, srow, u2, n_tiles=n_tiles, TM=FFN_TM)
    ys = _ffn(te, nv, xs, w_gate[l], w_up[l], w_down[l], TM=FFN_TM)
    out = _combine(gd, ys, col, x1, mod3, ln2_g[l].reshape(1, -1), ln2_b[l].reshape(1, -1), S=S, tb=tm)
    return out.reshape(B, S, D)


def kernel(x, c, w_ada, b_ada, w_in, w_conv, b_conv, b_igate, b_fgate, mlstm_norm_g, w_gla_a, b_gla_a,
           gla_norm_g, w_out, ln1_g, ln1_b, w_route_group, b_route_group, w_route_expert, b_route_expert,
           w_gate, w_up, w_down, ln2_g, ln2_b):
    for l in range(DEPTH):
        x = _layer(x, c, l, w_ada, b_ada, w_in, w_conv, b_conv, b_igate, b_fgate, mlstm_norm_g, w_gla_a,
                   b_gla_a, gla_norm_g, w_out, ln1_g, ln1_b, w_route_group, b_route_group, w_route_expert,
                   b_route_expert, w_gate, w_up, w_down, ln2_g, ln2_b)
    return x
```

```python
import functools

import numpy as np
import jax
import jax.numpy as jnp
from jax import lax
from jax.experimental import pallas as pl
from jax.experimental.pallas import tpu as pltpu

F32 = jnp.float32
BF16 = jnp.bfloat16
HIGHEST = lax.Precision.HIGHEST

DEPTH = 1
M_HEADS = 4
M_HD = 128
M_W = M_HEADS * M_HD
CONV_W = 4
G_HEADS = 4
G_DK = 64
G_DV = 128
G_W = G_HEADS * G_DV
G_KW = G_HEADS * G_DK
G_RANK = 16
G_TAU = 16.0
G_CHUNK = 64
N_GROUPS = 4
E_PER_G = 8
N_EXP = N_GROUPS * E_PER_G
D_EXP = 512
ALPHA = (2 * DEPTH) ** 0.25
LN_EPS = 1e-5

LANES = 128
SUBLANES = 8
VMEM_LIMIT = 48 * 1024 * 1024

C_QK = 0
C_VO = 1024
C_GQK = 2048
C_GV = 2560
C_GG = 3072
C_SMALL = 3584
C_TOT = 3712
SM_I, SM_F, SM_A = 0, 8, 16
IN_GATES = 4 * M_W
IN_G = IN_GATES + 2 * M_HEADS
IN_GA = IN_G + 2 * G_KW + 2 * G_W
IN_TOT = IN_GA + G_RANK

FFN_TM = 256
FFN_SUB = 1
GRAN = SUBLANES
G_LAST = LANES - 1


def _cparams(n_axes=1):
    return pltpu.CompilerParams(dimension_semantics=("arbitrary",) * n_axes,
                                vmem_limit_bytes=VMEM_LIMIT)


def _sigmoid(x):
    return 1.0 / (1.0 + jnp.exp(-x))


def _log_sigmoid(x):
    return jnp.minimum(x, 0.0) - jnp.log(1.0 + jnp.exp(-jnp.abs(x)))


def _ada_kernel(c_ref, w_ref, b_ref, o_ref):
    c = c_ref[...]
    ca = c * _sigmoid(c)
    o_ref[...] = jnp.dot(ca, w_ref[...], preferred_element_type=F32, precision=HIGHEST) + b_ref[...]


def _ada(c, w, b):
    B, D = c.shape
    n_out = w.shape[1]
    tn = 1024
    return pl.pallas_call(
        _ada_kernel,
        grid=(n_out // tn,),
        in_specs=[pl.BlockSpec((B, D), lambda j: (0, 0)),
                  pl.BlockSpec((D, tn), lambda j: (0, j)),
                  pl.BlockSpec((1, tn), lambda j: (0, j))],
        out_specs=pl.BlockSpec((B, tn), lambda j: (0, j)),
        out_shape=jax.ShapeDtypeStruct((B, n_out), F32),
        compiler_params=_cparams(),
        name="ada",
    )(c, w, b.reshape(1, n_out))


def _inproj_kernel(x_ref, mod_ref, win_ref, wc_ref, bc_ref, wa_ref, ba_ref, bg_ref,
                   oa_ref, la_ref, g_ref, halo_ref, w_ref, *, tm, tpb, lm):
    i = pl.program_id(0)

    @pl.when(i == 0)
    def _():
        rc = LANES
        for r in range(0, win_ref.shape[0], rc):
            rs = slice(r, r + rc)
            w_ref[rs, 0:IN_GATES] = win_ref[rs, 0:IN_GATES].astype(BF16)
            t = win_ref[rs, IN_GATES:IN_TOT]
            w_ref[rs, C_GQK:C_SMALL] = t[:, IN_G - IN_GATES:IN_GA - IN_GATES].astype(BF16)
            z = lambda n: jnp.zeros((rc, n), F32)
            small = jnp.concatenate([t[:, 0:M_HEADS], z(SM_F - M_HEADS), t[:, M_HEADS:2 * M_HEADS],
                                     z(SM_A - SM_F - M_HEADS), t[:, IN_GA - IN_GATES:IN_TOT - IN_GATES],
                                     z(LANES - SM_A - G_RANK)], axis=1)
            w_ref[rs, C_SMALL:C_TOT] = small.astype(BF16)

    @pl.when(i % tpb == 0)
    def _():
        halo_ref[...] = jnp.zeros_like(halo_ref)

    mod = mod_ref[0]
    u = (x_ref[...] * (1.0 + mod[1:2, :]) + mod[0:1, :]).astype(BF16)

    p = jnp.dot(u, w_ref[:, C_QK:C_QK + 2 * M_W], preferred_element_type=F32)
    ext = jnp.concatenate([halo_ref[...], p], axis=0)
    acc = bc_ref[...] + wc_ref[CONV_W - 1:CONV_W, :] * p
    for j in range(CONV_W - 1):
        sh = pltpu.roll(ext, CONV_W - 1 - j, 0)[SUBLANES:, :]
        acc = acc + wc_ref[j:j + 1, :] * sh
    halo_ref[...] = p[tm - SUBLANES:, :]
    qk = acc * _sigmoid(acc)
    oa_ref[:, C_QK:C_QK + M_W] = qk[:, :M_W].astype(BF16)
    oa_ref[:, C_QK + M_W:C_QK + 2 * M_W] = (qk[:, M_W:] * (M_HD ** -0.5)).astype(BF16)

    p = jnp.dot(u, w_ref[:, C_VO:C_VO + 2 * M_W], preferred_element_type=F32)
    oa_ref[:, C_VO:C_VO + 2 * M_W] = p.astype(BF16)

    p = jnp.dot(u, w_ref[:, C_GQK:C_GQK + G_KW], preferred_element_type=F32)
    oa_ref[:, C_GQK:C_GQK + G_KW] = (p * (G_DK ** -0.5)).astype(BF16)
    p = jnp.dot(u, w_ref[:, C_GQK + G_KW:C_SMALL], preferred_element_type=F32)
    oa_ref[:, C_GQK + G_KW:C_SMALL] = p.astype(BF16)

    ps = jnp.dot(u, w_ref[:, C_SMALL:C_TOT], preferred_element_type=F32)
    la = jnp.dot(ps.astype(BF16), wa_ref[...], preferred_element_type=F32) + ba_ref[...]
    la_ref[...] = _log_sigmoid(la) * (1.0 / G_TAU)
    pt = ps.T
    gi = pt[SM_I:SM_I + SUBLANES, :] + bg_ref[0:SUBLANES, :]
    gf = _log_sigmoid(pt[SM_F:SM_F + SUBLANES, :] + bg_ref[SUBLANES:2 * SUBLANES, :])
    for j in range(tm // lm):
        g_ref[j, 0:SUBLANES, :] = gi[:, j * lm:(j + 1) * lm]
        g_ref[j, SUBLANES:2 * SUBLANES, :] = gf[:, j * lm:(j + 1) * lm]


def _inproj(x2, mod3, w_in, w_conv, b_conv, wa_pad, b_gla, bg, *, S, tm, lm):
    N, D = x2.shape
    tpb = S // tm
    kern = functools.partial(_inproj_kernel, tm=tm, tpb=tpb, lm=lm)
    return pl.pallas_call(
        kern,
        grid=(N // tm,),
        in_specs=[pl.BlockSpec((tm, D), lambda i: (i, 0)),
                  pl.BlockSpec((1, 6, D), lambda i: (i // tpb, 0, 0)),
                  pl.BlockSpec((D, IN_TOT), lambda i: (0, 0), pipeline_mode=pl.Buffered(1)),
                  pl.BlockSpec((CONV_W, 2 * M_W), lambda i: (0, 0)),
                  pl.BlockSpec((1, 2 * M_W), lambda i: (0, 0)),
                  pl.BlockSpec((LANES, G_KW), lambda i: (0, 0)),
                  pl.BlockSpec((1, G_KW), lambda i: (0, 0)),
                  pl.BlockSpec((2 * SUBLANES, 1), lambda i: (0, 0))],
        out_specs=[pl.BlockSpec((tm, C_SMALL), lambda i: (i, 0)),
                   pl.BlockSpec((tm, G_KW), lambda i: (i, 0)),
                   pl.BlockSpec((tm // lm, 2 * SUBLANES, lm), lambda i: (i, 0, 0))],
        out_shape=[jax.ShapeDtypeStruct((N, C_SMALL), BF16),
                   jax.ShapeDtypeStruct((N, G_KW), F32),
                   jax.ShapeDtypeStruct((N // lm, 2 * SUBLANES, lm), F32)],
        scratch_shapes=[pltpu.VMEM((SUBLANES, 2 * M_W), F32), pltpu.VMEM((D, C_TOT), BF16)],
        compiler_params=_cparams(),
        name="inproj",
    )(x2, mod3, w_in, w_conv, b_conv, wa_pad, b_gla, bg)


def _mlstm_kernel(qk_ref, vo_ref, g_ref, u_ref, gain_ref, out_ref, c_ref, zt_ref, a_ref, dec_ref, *, L, NC):
    c_ref[...] = jnp.zeros_like(c_ref)
    lane = lax.broadcasted_iota(jnp.int32, (SUBLANES, L), 1)
    tril = (lax.broadcasted_iota(jnp.int32, (L, L), 0) >= lax.broadcasted_iota(jnp.int32, (L, L), 1))
    ones_v = jnp.ones((L, M_HD), BF16)
    zpad = jnp.zeros((LANES - 4 * SUBLANES, L), F32)

    bs, gs = [], []
    for c in range(NC):
        b = jnp.dot(g_ref[c, SUBLANES:2 * SUBLANES, :], u_ref[...], preferred_element_type=F32,
                    precision=HIGHEST)
        a = g_ref[c, 0:SUBLANES, :] - b
        a_ref[c] = a
        G = a
        s = 1
        while s < L:
            G = jnp.maximum(G, jnp.where(lane >= s, pltpu.roll(G, s, 1), -jnp.inf))
            s *= 2
        bs.append(b)
        gs.append(G)
    m_prev = jnp.zeros((SUBLANES, 1), F32)
    for c in range(NC):
        M = jnp.maximum(gs[c], m_prev)
        ML = M[:, L - 1:L]
        Z = jnp.concatenate([M, jnp.exp(m_prev - M), jnp.exp(-(bs[c] + M)), jnp.exp(a_ref[c] - ML), zpad],
                            axis=0)
        zt_ref[c] = Z.T
        dec_ref[c] = jnp.broadcast_to(jnp.exp(m_prev - ML), (SUBLANES, 2 * M_HD))
        m_prev = bs[c][:, L - 1:L] + ML

    def chunk(c, carry):
        r0 = pl.multiple_of(c * L, L)
        Zt = zt_ref[c]
        a = a_ref[c]
        dec = dec_ref[c]
        rows = pl.ds(r0, L)
        heads = range(M_HEADS)
        hs = [slice(h * M_HD, (h + 1) * M_HD) for h in heads]
        hs2 = [slice(M_W + h * M_HD, M_W + (h + 1) * M_HD) for h in heads]
        q = [qk_ref[rows, hs[h]] for h in heads]
        k = [qk_ref[rows, hs2[h]] for h in heads]
        vext = [jnp.concatenate([vo_ref[rows, hs[h]], ones_v], axis=1) for h in heads]
        cst = [c_ref[h] for h in heads]
        sc = [lax.dot_general(q[h], k[h], (((1,), (1,)), ((), ())), preferred_element_type=F32) for h in heads]
        qc = [jnp.dot(q[h], cst[h].astype(BF16), preferred_element_type=F32) for h in heads]
        pm = [(sc[h] * jnp.exp(jnp.where(tril, a[h:h + 1, :] - Zt[:, h:h + 1], -jnp.inf))).astype(BF16)
              for h in heads]
        pv = [jnp.dot(pm[h], vext[h], preferred_element_type=F32) for h in heads]
        kw = [(Zt[:, 3 * SUBLANES + h:3 * SUBLANES + h + 1] * k[h].astype(F32)).astype(BF16) for h in heads]
        upd = [lax.dot_general(kw[h], vext[h], (((0,), (0,)), ((), ())), preferred_element_type=F32)
               for h in heads]
        for h in heads:
            c_ref[h] = dec[h:h + 1, :] * cst[h] + upd[h]
            nd = pv[h] + Zt[:, SUBLANES + h:SUBLANES + h + 1] * qc[h]
            hh = nd[:, :M_HD] / jnp.maximum(jnp.abs(nd[:, M_HD:]), Zt[:, 2 * SUBLANES + h:2 * SUBLANES + h + 1])
            hh = _sigmoid(vo_ref[rows, hs2[h]].astype(F32)) * hh
            hn = hh * lax.rsqrt(jnp.mean(hh * hh, axis=-1, keepdims=True) + LN_EPS)
            out_ref[rows, hs[h]] = (hn * gain_ref[:, hs[h]]).astype(BF16)
        return carry

    lax.fori_loop(0, NC, chunk, 0)


def _mlstm(oa, g3, u_tri, gain, *, B, S, L):
    N = oa.shape[0]
    NC = S // L
    kern = functools.partial(_mlstm_kernel, L=L, NC=NC)
    return pl.pallas_call(
        kern,
        grid=(B,),
        in_specs=[pl.BlockSpec((S, 2 * M_W), lambda b: (b, C_QK // (2 * M_W))),
                  pl.BlockSpec((S, 2 * M_W), lambda b: (b, C_VO // (2 * M_W))),
                  pl.BlockSpec((NC, 2 * SUBLANES, L), lambda b: (b, 0, 0)),
                  pl.BlockSpec((L, L), lambda b: (0, 0)),
                  pl.BlockSpec((1, M_W), lambda b: (0, 0))],
        out_specs=pl.BlockSpec((S, M_W), lambda b: (b, 0)),
        out_shape=jax.ShapeDtypeStruct((N, M_W), BF16),
        scratch_shapes=[pltpu.VMEM((M_HEADS, M_HD, 2 * M_HD), F32), pltpu.VMEM((NC, L, LANES), F32),
                        pltpu.VMEM((NC, SUBLANES, L), F32), pltpu.VMEM((NC, SUBLANES, 2 * M_HD), F32)],
        compiler_params=_cparams(),
        name="mlstm",
    )(oa, oa, g3, u_tri, gain)


_G_LEVELS = 6
_G_XROW = 2 * G_CHUNK + SUBLANES


def _gla_consts():
    L = G_CHUNK
    t = np.arange(L)
    blocks = [(t[None, :] <= t[:, None]).astype(np.float32),
              (t[None, :] > t[:, None]).astype(np.float32),
              np.ones((SUBLANES, L), np.float32)]
    masks = [np.eye(L, dtype=np.float32)]
    m = 1
    while m < L:
        wl = np.zeros((L, L), np.float32)
        for r in range(L):
            r0 = (r // (2 * m)) * 2 * m + m
            if r % (2 * m) >= m:
                wl[r, r0:r + 1] = 1.0
            else:
                wl[r, r + 1:r0] = 1.0
        blocks.append(wl)
        tt, ss = t[:, None], t[None, :]
        masks.append(((tt // (2 * m) == ss // (2 * m)) & (tt % (2 * m) >= m)
                      & (ss % (2 * m) < m)).astype(np.float32))
        m *= 2
    w = np.concatenate(blocks, axis=0)
    w3 = np.concatenate([w, w, w], axis=1)
    mk = np.stack([np.concatenate([x] * G_HEADS, axis=0) for x in masks])
    return w3, mk


def _gla_kernel(qk_ref, v_ref, gg_ref, la_ref, w3_ref, mk_ref, gain_ref, out_ref, st_ref, *, NC, S, nb):
    L = G_CHUNK
    st_ref[...] = jnp.zeros_like(st_ref)
    lane_head = lax.broadcasted_iota(jnp.int32, (L, G_KW), 1) // G_DK
    br = lax.broadcasted_iota(jnp.int32, (2 * G_DV, LANES), 0) < G_DV
    bl = lax.broadcasted_iota(jnp.int32, (2 * G_DV, LANES), 1) < G_DK
    bmask = br == bl
    nt = (((1,), (1,)), ((), ()))
    tn = (((0,), (0,)), ((), ()))

    def chunk(c, carry):
        rows = [pl.ds(pl.multiple_of(bi * S + c * L, L), L) for bi in range(nb)]
        X, q, k = [], [], []
        for bi in range(nb):
            la = la_ref[rows[bi], :]
            hi = la.astype(BF16)
            r1 = la - hi.astype(F32)
            mid = r1.astype(BF16)
            lo = (r1 - mid.astype(F32)).astype(BF16)
            stk = jnp.concatenate([hi, mid, lo], axis=0)
            X.append(jnp.exp(jnp.dot(w3_ref[...], stk, preferred_element_type=F32)))
            q.append(qk_ref[rows[bi], 0:G_KW].astype(F32))
            k.append(qk_ref[rows[bi], G_KW:2 * G_KW].astype(F32))

        sc = [[None] * (_G_LEVELS + 1) for _ in range(nb)]
        for lev in range(_G_LEVELS + 1):
            for bi in range(nb):
                if lev == 0:
                    qt, kt = q[bi], k[bi]
                else:
                    xl = X[bi][_G_XROW + L * (lev - 1):_G_XROW + L * lev, :]
                    qt, kt = q[bi] * xl, k[bi] * xl
                q4 = jnp.concatenate([jnp.where(lane_head == h, qt, 0.0) for h in range(G_HEADS)],
                                     axis=0).astype(BF16)
                sc[bi][lev] = lax.dot_general(q4, kt.astype(BF16), nt, preferred_element_type=F32)
        Ab = []
        for bi in range(nb):
            A = sc[bi][0] * mk_ref[0]
            for lev in range(1, _G_LEVELS + 1):
                A = A + sc[bi][lev] * mk_ref[lev]
            Ab.append(A.astype(BF16))

        for bi in range(nb):
            gg = gg_ref[rows[bi], :].astype(F32)
            gate = gg * _sigmoid(gg)
            for p in range(2):
                ls = slice(LANES * p, LANES * (p + 1))
                vp = v_ref[rows[bi], 2 * G_DV * p:2 * G_DV * (p + 1)]
                oi = [jnp.dot(Ab[bi][L * (2 * p + hh):L * (2 * p + hh + 1)],
                              vp[:, G_DV * hh:G_DV * (hh + 1)], preferred_element_type=F32)
                      for hh in range(2)]
                st = st_ref[bi, p]
                qc = (q[bi][:, ls] * X[bi][0:L, ls]).astype(BF16)
                o_inter = lax.dot_general(qc, st.astype(BF16), nt, preferred_element_type=F32)
                kc = (k[bi][:, ls] * X[bi][L:2 * L, ls]).astype(BF16)
                upd = lax.dot_general(vp, kc, tn, preferred_element_type=F32)
                dec = X[bi][2 * L:2 * L + 1, ls]
                st_ref[bi, p] = jnp.where(bmask, dec * st + upd, 0.0)
                for hh in range(2):
                    o = o_inter[:, G_DV * hh:G_DV * (hh + 1)] + oi[hh]
                    hn = o * lax.rsqrt(jnp.mean(o * o, axis=-1, keepdims=True) + LN_EPS)
                    hs = slice(G_DV * (2 * p + hh), G_DV * (2 * p + hh + 1))
                    out_ref[rows[bi], hs] = (hn * gain_ref[:, hs] * gate[:, hs]).astype(BF16)
        return carry

    lax.fori_loop(0, NC, chunk, 0)


def _gla(oa, la, w3, mk, gain, *, B, S, nb):
    N = oa.shape[0]
    NC = S // G_CHUNK
    kern = functools.partial(_gla_kernel, NC=NC, S=S, nb=nb)
    R = nb * S
    return pl.pallas_call(
        kern,
        grid=(B // nb,),
        in_specs=[pl.BlockSpec((R, 2 * G_KW), lambda b: (b, C_GQK // (2 * G_KW))),
                  pl.BlockSpec((R, G_W), lambda b: (b, C_GV // G_W)),
                  pl.BlockSpec((R, G_W), lambda b: (b, C_GG // G_W)),
                  pl.BlockSpec((R, G_KW), lambda b: (b, 0)),
                  pl.BlockSpec(w3.shape, lambda b: (0, 0)),
                  pl.BlockSpec(mk.shape, lambda b: (0, 0, 0)),
                  pl.BlockSpec((1, G_W), lambda b: (0, 0))],
        out_specs=pl.BlockSpec((R, G_W), lambda b: (b, 0)),
        out_shape=jax.ShapeDtypeStruct((N, G_W), BF16),
        scratch_shapes=[pltpu.VMEM((nb, 2, 2 * G_DV, LANES), F32)],
        compiler_params=_cparams(),
        name="gla",
    )(oa, oa, oa, la, w3, mk, gain)


def _layer_norm(z, g, b):
    mu = jnp.mean(z, axis=-1, keepdims=True)
    zc = z - mu
    var = jnp.mean(zc * zc, axis=-1, keepdims=True)
    return zc * lax.rsqrt(var + LN_EPS) * g + b


def _outproj_kernel(hm_ref, hg_ref, wf_ref, x_ref, mod_ref, g_ref, b_ref, wr_ref, br_ref,
                    x1_ref, u2_ref, rrow_ref, w_ref, *, tb, nh):
    @pl.when(pl.program_id(0) == 0)
    def _():
        w_ref[...] = wf_ref[...].astype(BF16)

    mod = mod_ref[0]
    blocks = [slice(tb * j, tb * (j + 1)) for j in range(nh)]
    y = [jnp.dot(hm_ref[r, :], w_ref[0:M_W, :], preferred_element_type=F32)
         + jnp.dot(hg_ref[r, :], w_ref[M_W:M_W + G_W, :], preferred_element_type=F32) for r in blocks]
    u2 = []
    for j, r in enumerate(blocks):
        z = ALPHA * x_ref[r, :] + (1.0 + mod[2:3, :]) * y[j]
        x1 = _layer_norm(z, g_ref[...], b_ref[...])
        x1_ref[r, :] = x1
        u2.append(x1 * (1.0 + mod[4:5, :]) + mod[3:4, :])
        u2_ref[r, :] = u2[j].astype(BF16)

    u2h = [u.astype(BF16) for u in u2]
    u2l = [(u2[j] - u2h[j].astype(F32)).astype(BF16) for j in range(nh)]
    lh = [jnp.dot(u, wr_ref[...], preferred_element_type=F32) for u in u2h]
    ll = [jnp.dot(u, wr_ref[:, 0:LANES], preferred_element_type=F32) for u in u2l]
    for j in range(nh):
        logits = lh[j][:, 0:LANES] + lh[j][:, LANES:2 * LANES] + ll[j] + br_ref[...]
        rrow_ref[j] = _route_select(logits.T, tb)


def _route_select(lt, tm):
    row = lax.broadcasted_iota(jnp.int32, (SUBLANES, tm), 0)
    gl = jnp.where(row < N_GROUPS, lt[0:SUBLANES, :], -jnp.inf)
    gmax = jnp.max(gl, axis=0, keepdims=True)
    gsel = jnp.min(jnp.where(gl == gmax, row, SUBLANES), axis=0, keepdims=True)
    pg = 1.0 / jnp.sum(jnp.exp(gl - gmax), axis=0, keepdims=True)
    ein = jnp.zeros((SUBLANES, tm), F32)
    for g in range(N_GROUPS):
        ein = jnp.where(gsel == g, lt[SUBLANES * (g + 1):SUBLANES * (g + 2), :], ein)
    v1 = jnp.max(ein, axis=0, keepdims=True)
    i1 = jnp.min(jnp.where(ein == v1, row, SUBLANES), axis=0, keepdims=True)
    rest = jnp.where(row == i1, -jnp.inf, ein)
    v2 = jnp.max(rest, axis=0, keepdims=True)
    i2 = jnp.min(jnp.where(rest == v2, row, SUBLANES), axis=0, keepdims=True)
    t2 = jnp.exp(v2 - v1)
    p1 = 1.0 / (1.0 + t2)
    e0 = (gsel * E_PER_G + i1).astype(F32)
    e1 = (gsel * E_PER_G + i2).astype(F32)
    return jnp.concatenate([e0, e1, pg * p1, pg * (t2 * p1), jnp.zeros((SUBLANES - 4, tm), F32)], axis=0)


def _outproj(hm, hg, w_out, x2, mod3, g, b, wr, br, *, S, tb, nh):
    N, D = x2.shape
    tm = tb * nh
    tpb = S // tm
    kern = functools.partial(_outproj_kernel, tb=tb, nh=nh)
    return pl.pallas_call(
        kern,
        grid=(N // tm,),
        in_specs=[pl.BlockSpec((tm, M_W), lambda i: (i, 0)),
                  pl.BlockSpec((tm, G_W), lambda i: (i, 0)),
                  pl.BlockSpec((M_W + G_W, D), lambda i: (0, 0), pipeline_mode=pl.Buffered(1)),
                  pl.BlockSpec((tm, D), lambda i: (i, 0)),
                  pl.BlockSpec((1, 6, D), lambda i: (i // tpb, 0, 0)),
                  pl.BlockSpec((1, D), lambda i: (0, 0)),
                  pl.BlockSpec((1, D), lambda i: (0, 0)),
                  pl.BlockSpec((D, 2 * LANES), lambda i: (0, 0)),
                  pl.BlockSpec((1, LANES), lambda i: (0, 0))],
        out_specs=[pl.BlockSpec((tm, D), lambda i: (i, 0)),
                   pl.BlockSpec((tm, D), lambda i: (i, 0)),
                   pl.BlockSpec((nh, SUBLANES, tb), lambda i: (i, 0, 0))],
        out_shape=[jax.ShapeDtypeStruct((N, D), F32),
                   jax.ShapeDtypeStruct((N, D), BF16),
                   jax.ShapeDtypeStruct((N // tb, SUBLANES, tb), F32)],
        scratch_shapes=[pltpu.VMEM((M_W + G_W, D), BF16)],
        compiler_params=_cparams(),
        name="outproj",
    )(hm, hg, w_out, x2, mod3, g, b, wr, br)


def _slots_per_tile(tb):
    worst = 2 * tb + N_EXP * (GRAN - 1)
    return -(-worst // LANES) * LANES


def _ffn_tiles(n_tok, tb):
    worst_rows = 2 * n_tok + (n_tok // tb) * N_EXP * (GRAN - 1)
    return -(-worst_rows // FFN_TM) + N_EXP


def _route_kernel(rr_ref, u_ref, lt_ref, srow_ref, col_ref, gd_ref, meta_ref, mg_ref, part_ref,
                  *, NT, tb, TM):
    iota_e = lax.broadcasted_iota(jnp.int32, (N_EXP, tb), 0).astype(F32)
    glane = lax.broadcasted_iota(jnp.int32, (N_EXP, LANES), 1).astype(F32)
    ltri = lt_ref[...]

    def prefix_e(col):
        return jnp.dot(ltri, jnp.broadcast_to(col, (N_EXP, LANES)),
                       preferred_element_type=F32, precision=HIGHEST)[:, 0:1]

    def p1(j, run8):
        r = rr_ref[j]
        oh0 = jnp.where(iota_e == r[0:1, :], 1.0, 0.0)
        oh1 = jnp.where(iota_e == r[1:2, :], 1.0, 0.0)
        cum0 = jnp.dot(oh0.astype(BF16), u_ref[...], preferred_element_type=F32)
        cum1 = jnp.dot(oh1.astype(BF16), u_ref[...], preferred_element_type=F32)
        c0 = jnp.sum(oh0, axis=1, keepdims=True)
        n8 = jnp.floor((c0 + jnp.sum(oh1, axis=1, keepdims=True) + (GRAN - 1.0)) * (1.0 / GRAN))
        lo8 = prefix_e(n8)
        s0 = jnp.sum(oh0 * (GRAN * lo8 + cum0 - 1.0), axis=0, keepdims=True)
        s1 = jnp.sum(oh1 * (GRAN * lo8 + c0 + cum1 - 1.0), axis=0, keepdims=True)
        info = jnp.concatenate([s0, s1, r[2:4, :], jnp.zeros((SUBLANES - 4, tb), F32)], axis=0)
        srow_ref[j] = info
        col_ref[pl.ds(pl.multiple_of(j * tb, tb), tb), :] = jnp.concatenate(
            [info, jnp.zeros((LANES - SUBLANES, tb), F32)], axis=0).T
        mg = jnp.where((lo8 <= glane) & (glane < lo8 + n8), 1.0, 0.0)
        mg_ref[j] = mg
        part = jnp.sum(mg * (run8 + glane - lo8), axis=0, keepdims=True)
        gcnt = jnp.broadcast_to(jnp.sum(n8, axis=0, keepdims=True), (1, LANES))
        part_ref[j] = jnp.concatenate([part, gcnt, jnp.zeros((SUBLANES - 2, LANES), F32)], axis=0)
        return run8 + n8

    tot8 = lax.fori_loop(0, NT, p1, jnp.zeros((N_EXP, 1), F32), unroll=4 if NT % 4 == 0 else 1)
    seg_t = jnp.floor((tot8 * GRAN + (TM - 1.0)) * (1.0 / TM))
    base_t = prefix_e(seg_t)
    base8 = base_t * (TM // GRAN)
    lane1 = lax.broadcasted_iota(jnp.int32, (1, LANES), 1)

    def p2(j, carry):
        pr = part_ref[j]
        dst = (pr[0:1, :] + jnp.sum(mg_ref[j] * base8, axis=0, keepdims=True)) * GRAN
        gd_ref[j] = jnp.where(lane1 == G_LAST, pr[1:2, :], dst).astype(jnp.int32)
        return carry

    lax.fori_loop(0, NT, p2, 0, unroll=4 if NT % 4 == 0 else 1)
    eye = jnp.where(glane == lax.broadcasted_iota(jnp.int32, (N_EXP, LANES), 0).astype(F32), 1.0, 0.0)
    tail_row = jnp.sum(eye * ((base8 + tot8) * GRAN), axis=0, keepdims=True)
    tail_n8 = jnp.sum(eye * (seg_t * (TM // GRAN) - tot8), axis=0, keepdims=True)
    nv_l = jnp.broadcast_to(jnp.sum(seg_t, axis=0, keepdims=True), (1, LANES))
    gd_ref[NT] = jnp.where(lane1 == G_LAST, nv_l, tail_row).astype(jnp.int32)
    gd_ref[NT + 1] = tail_n8.astype(jnp.int32)
    ti = lax.broadcasted_iota(jnp.int32, (N_EXP, tb), 1).astype(F32)
    te = jnp.sum(jnp.where(base_t <= ti, 1.0, 0.0), axis=0, keepdims=True) - 1.0
    nv = jnp.broadcast_to(jnp.sum(seg_t, axis=0, keepdims=True), (1, tb))
    meta_ref[...] = jnp.concatenate([te, nv, jnp.zeros((SUBLANES - 2, tb), F32)],
                                    axis=0).astype(jnp.int32)


def _route(rrow, u_cnt, ltri, *, TM):
    NT, _, tb = rrow.shape
    kern = functools.partial(_route_kernel, NT=NT, tb=tb, TM=TM)
    full3 = lambda i: (0, 0, 0)
    return pl.pallas_call(
        kern,
        grid=(1,),
        in_specs=[pl.BlockSpec((NT, SUBLANES, tb), full3),
                  pl.BlockSpec((tb, tb), lambda i: (0, 0)),
                  pl.BlockSpec((N_EXP, N_EXP), lambda i: (0, 0))],
        out_specs=[pl.BlockSpec((NT, SUBLANES, tb), full3),
                   pl.BlockSpec((NT * tb, LANES), lambda i: (0, 0)),
                   pl.BlockSpec((NT + 2, 1, LANES), full3),
                   pl.BlockSpec((SUBLANES, tb), lambda i: (0, 0))],
        out_shape=[jax.ShapeDtypeStruct((NT, SUBLANES, tb), F32),
                   jax.ShapeDtypeStruct((NT * tb, LANES), F32),
                   jax.ShapeDtypeStruct((NT + 2, 1, LANES), jnp.int32),
                   jax.ShapeDtypeStruct((SUBLANES, tb), jnp.int32)],
        scratch_shapes=[pltpu.VMEM((NT, N_EXP, LANES), F32), pltpu.VMEM((NT, SUBLANES, LANES), F32)],
        compiler_params=_cparams(),
        name="route",
    )(rrow, u_cnt, ltri)


U32 = jnp.uint32
_HI_MASK = 0xFFFF0000


def _pack_halves(x):
    c = x.shape[1] // 2
    lo = lax.bitcast_convert_type(x[:, :c], U32)
    hi = lax.bitcast_convert_type(x[:, c:], U32)
    return (lo >> 16) | (hi & U32(_HI_MASK))


def _unpack_halves(w):
    lo = lax.bitcast_convert_type(w << 16, F32)
    hi = lax.bitcast_convert_type(w & U32(_HI_MASK), F32)
    return jnp.concatenate([lo, hi], axis=1).astype(BF16)


def _granule_copy(src_ref, src_row, dst_ref, dst_row, sem):
    return pltpu.make_async_copy(src_ref.at[pl.ds(src_row, GRAN), :], dst_ref.at[pl.ds(dst_row, GRAN), :], sem)


def _wait_granules(n, src_ref, dst_ref, sem, n_max):
    b = 1
    while b <= n_max:
        @pl.when((n & b) != 0)
        def _(b=b):
            pltpu.make_async_copy(src_ref.at[pl.ds(0, b * GRAN), :], dst_ref.at[pl.ds(0, b * GRAN), :],
                                  sem).wait()
        b *= 2


def _dispatch_kernel(gd_ref, srow_ref, u_ref, xs_ref, buf, zbuf, sems, *, NT, SL, TM, n_tiles):
    j = pl.program_id(0)
    slot = j % 2
    zsem = sems.at[2]

    def drain(tile, sl):
        _wait_granules(gd_ref[tile, G_LAST], buf.at[sl], xs_ref, sems.at[sl], SL // GRAN)

    def tile_fill(t):
        return pltpu.make_async_copy(zbuf, xs_ref.at[pl.ds(pl.multiple_of(t * TM, TM), TM), :], zsem)

    def zero_fill(wait):
        for e in range(N_EXP):
            def zg(g, carry, e=e):
                cp = _granule_copy(zbuf, 0, xs_ref, pl.multiple_of(gd_ref[NT, e] + g * GRAN, GRAN), zsem)
                cp.wait() if wait else cp.start()
                return carry
            lax.fori_loop(0, gd_ref[NT + 1, e], zg, 0)

        def zt(t, carry):
            tile_fill(t).wait() if wait else tile_fill(t).start()
            return carry
        lax.fori_loop(gd_ref[NT, G_LAST], n_tiles, zt, 0)

    @pl.when(j == 0)
    def _():
        zbuf[...] = jnp.zeros_like(zbuf)
        zero_fill(False)

    @pl.when(j >= 2)
    def _():
        drain(j - 2, slot)

    s = srow_ref[0]
    rows = lax.broadcasted_iota(jnp.int32, (SL, s.shape[1]), 0).astype(F32)
    m0 = rows == s[0:1, :]
    m1 = rows == s[1:2, :]
    oh = jnp.where(m0 | m1, 1.0, 0.0).astype(BF16)
    dw = u_ref.shape[1] // 2
    buf[slot, :, 0:dw] = _pack_halves(jnp.dot(oh, u_ref[...], preferred_element_type=F32))
    wrow = jnp.sum(jnp.where(m0, s[2:3, :], 0.0) + jnp.where(m1, s[3:4, :], 0.0), axis=1, keepdims=True)
    buf[slot, :, dw:dw + LANES] = lax.bitcast_convert_type(jnp.broadcast_to(wrow, (SL, LANES)), U32)

    def issue(g, carry):
        _granule_copy(buf.at[slot], pl.multiple_of(g * GRAN, GRAN), xs_ref,
                      pl.multiple_of(gd_ref[j, g], GRAN), sems.at[slot]).start()
        return carry

    lax.fori_loop(0, gd_ref[j, G_LAST], issue, 0)

    @pl.when(j == NT - 1)
    def _():
        drain(j, slot)
        if NT > 1:
            drain(j - 1, 1 - slot)
        zero_fill(True)


def _dispatch(gd, srow, u2, *, n_tiles, TM):
    N, D = u2.shape
    NT, _, tb = srow.shape
    SL = _slots_per_tile(tb)
    n_rows = n_tiles * TM
    kern = functools.partial(_dispatch_kernel, NT=NT, SL=SL, TM=TM, n_tiles=n_tiles)
    grid_spec = pltpu.PrefetchScalarGridSpec(
        num_scalar_prefetch=1,
        grid=(NT,),
        in_specs=[pl.BlockSpec((1, SUBLANES, tb), lambda j, gd: (j, 0, 0)),
                  pl.BlockSpec((tb, D), lambda j, gd: (j, 0))],
        out_specs=pl.BlockSpec(memory_space=pl.ANY),
        scratch_shapes=[pltpu.VMEM((2, SL, D // 2 + LANES), U32), pltpu.VMEM((TM, D // 2 + LANES), U32),
                        pltpu.SemaphoreType.DMA((3,))],
    )
    return pl.pallas_call(
        kern,
        grid_spec=grid_spec,
        out_shape=jax.ShapeDtypeStruct((n_rows, D // 2 + LANES), U32),
        compiler_params=_cparams(),
        name="dispatch",
    )(gd, srow, u2)


def _ffn_kernel(te_ref, nv_ref, xs_ref, wg_ref, wu_ref, wd_ref, o_ref, wgb, wub, wdb, sg, su, sd, slot_ref,
                sems):
    i = pl.program_id(0)
    nv = nv_ref[0]
    e = te_ref[i]

    def weight_copies(ex, sl):
        return (pltpu.make_async_copy(wg_ref.at[ex], sg.at[sl], sems.at[sl]),
                pltpu.make_async_copy(wu_ref.at[ex], su.at[sl], sems.at[sl]),
                pltpu.make_async_copy(wd_ref.at[ex], sd.at[sl], sems.at[sl]))

    @pl.when(i == 0)
    def _():
        slot_ref[0] = 0
        for cp in weight_copies(e, 0):
            cp.start()

    @pl.when((i < nv) & ((i == 0) | (e != te_ref[jnp.maximum(i - 1, 0)])))
    def _():
        sl = slot_ref[0]
        for cp in weight_copies(e, sl):
            cp.wait()
        wgb[...] = sg[sl].astype(BF16)
        wub[...] = su[sl].astype(BF16)
        wdb[...] = sd[sl].astype(BF16)
        nxt = lax.while_loop(lambda t: (t < nv) & (te_ref[jnp.minimum(t, nv - 1)] == e), lambda t: t + 1, i + 1)

        @pl.when(nxt < nv)
        def _():
            for cp in weight_copies(te_ref[nxt], 1 - sl):
                cp.start()
        slot_ref[0] = 1 - sl

    @pl.when(i < nv)
    def _():
        nsub = FFN_SUB
        hm = xs_ref.shape[0] // nsub
        dw = o_ref.shape[1]
        halves = tuple(slice(hm * j, hm * (j + 1)) for j in range(nsub))
        x = [_unpack_halves(xs_ref[r, 0:dw]) for r in halves]
        g = [jnp.dot(x[j], wgb[...], preferred_element_type=F32) for j in range(nsub)]
        u = [jnp.dot(x[j], wub[...], preferred_element_type=F32) for j in range(nsub)]
        h = [(g[j] * _sigmoid(g[j]) * u[j]).astype(BF16) for j in range(nsub)]
        y = [jnp.dot(h[j], wdb[...], preferred_element_type=F32) for j in range(nsub)]
        for j in range(nsub):
            wt = lax.bitcast_convert_type(xs_ref[halves[j], dw:dw + LANES], F32)
            yw = y[j] * jnp.concatenate([wt] * (2 * dw // LANES), axis=1)
            o_ref[halves[j], :] = _pack_halves(yw.astype(BF16).astype(F32))

    @pl.when(i >= nv_ref[0])
    def _():
        o_ref[...] = jnp.zeros_like(o_ref)


def _ffn(te, nv, xs, wg, wu, wd, *, TM):
    P, XW = xs.shape
    DW = XW - LANES
    D = 2 * DW
    n_tiles = P // TM
    grid_spec = pltpu.PrefetchScalarGridSpec(
        num_scalar_prefetch=2,
        grid=(n_tiles,),
        in_specs=[pl.BlockSpec((TM, XW), lambda i, te, nv: (jnp.maximum(jnp.minimum(i, nv[0] - 1), 0), 0)),
                  pl.BlockSpec(memory_space=pl.ANY),
                  pl.BlockSpec(memory_space=pl.ANY),
                  pl.BlockSpec(memory_space=pl.ANY)],
        out_specs=pl.BlockSpec((TM, DW), lambda i, te, nv: (i, 0)),
        scratch_shapes=[pltpu.VMEM((D, D_EXP), BF16), pltpu.VMEM((D, D_EXP), BF16),
                        pltpu.VMEM((D_EXP, D), BF16),
                        pltpu.VMEM((2, D, D_EXP), F32), pltpu.VMEM((2, D, D_EXP), F32),
                        pltpu.VMEM((2, D_EXP, D), F32), pltpu.SMEM((1,), jnp.int32),
                        pltpu.SemaphoreType.DMA((2,))],
    )
    return pl.pallas_call(
        _ffn_kernel,
        grid_spec=grid_spec,
        out_shape=jax.ShapeDtypeStruct((P, DW), U32),
        compiler_params=_cparams(),
        name="ffn",
    )(te, nv, xs, wg, wu, wd)


def _combine_kernel(gd_ref, ys_ref, col_ref, x1_ref, mod_ref, g_ref, b_ref, o_ref, buf, sems, *, NT, SL):
    j = pl.program_id(0)
    slot = j % 2

    def fetch(tile, sl):
        def f(g, carry):
            _granule_copy(ys_ref, pl.multiple_of(gd_ref[tile, g], GRAN), buf.at[sl],
                          pl.multiple_of(g * GRAN, GRAN), sems.at[sl]).start()
            return carry
        lax.fori_loop(0, gd_ref[tile, G_LAST], f, 0)

    @pl.when(j == 0)
    def _():
        fetch(0, 0)

    @pl.when(j + 1 < NT)
    def _():
        fetch(j + 1, 1 - slot)

    ng = gd_ref[j, G_LAST]

    _wait_granules(ng, ys_ref, buf.at[slot], sems.at[slot], SL // GRAN)

    rows = lax.broadcasted_iota(jnp.int32, (SL, 1), 0)
    yb = _unpack_halves(jnp.where(rows < ng * GRAN, buf[slot], U32(0)))
    col = col_ref[...]
    tb = col.shape[0]
    lanes = lax.broadcasted_iota(jnp.int32, (tb, SL), 1).astype(F32)
    sel = jnp.where((lanes == col[:, 0:1]) | (lanes == col[:, 1:2]), 1.0, 0.0).astype(BF16)
    y = jnp.dot(sel, yb, preferred_element_type=F32)
    mod = mod_ref[0]
    z = ALPHA * x1_ref[...] + (1.0 + mod[5:6, :]) * y
    o_ref[...] = _layer_norm(z, g_ref[...], b_ref[...])


def _combine(gd, ys, col, x1, mod3, g, b, *, S, tb):
    N, D = x1.shape
    NT = N // tb
    tpb = S // tb
    SL = _slots_per_tile(tb)
    kern = functools.partial(_combine_kernel, NT=NT, SL=SL)
    grid_spec = pltpu.PrefetchScalarGridSpec(
        num_scalar_prefetch=1,
        grid=(NT,),
        in_specs=[pl.BlockSpec(memory_space=pl.ANY),
                  pl.BlockSpec((tb, LANES), lambda j, gd: (j, 0)),
                  pl.BlockSpec((tb, D), lambda j, gd: (j, 0)),
                  pl.BlockSpec((1, 6, D), lambda j, gd: (j // tpb, 0, 0)),
                  pl.BlockSpec((1, D), lambda j, gd: (0, 0)),
                  pl.BlockSpec((1, D), lambda j, gd: (0, 0))],
        out_specs=pl.BlockSpec((tb, D), lambda j, gd: (j, 0)),
        scratch_shapes=[pltpu.VMEM((2, SL, D // 2), U32), pltpu.SemaphoreType.DMA((2,))],
    )
    return pl.pallas_call(
        kern,
        grid_spec=grid_spec,
        out_shape=jax.ShapeDtypeStruct((N, D), F32),
        compiler_params=_cparams(),
        name="combine",
    )(gd, ys, col, x1, mod3, g, b)


def _layer(x, c, l, w_ada, b_ada, w_in, w_conv, b_conv, b_igate, b_fgate, mlstm_norm_g, w_gla_a, b_gla_a,
           gla_norm_g, w_out, ln1_g, ln1_b, w_route_group, b_route_group, w_route_expert, b_route_expert,
           w_gate, w_up, w_down, ln2_g, ln2_b):
    B, S, D = x.shape
    N = B * S
    x2 = x.reshape(N, D)
    tm_in = min(512, S)
    tm = min(256, S)
    lm = min(256, S)

    mod3 = _ada(c, w_ada[l], b_ada[l]).reshape(B, 6, D)

    wa_pad = jnp.zeros((LANES, G_KW), F32).at[SM_A:SM_A + G_RANK].set(w_gla_a[l]).astype(BF16)
    bg = (jnp.zeros((2 * SUBLANES, 1), F32).at[0:M_HEADS, 0].set(b_igate[l])
          .at[SUBLANES:SUBLANES + M_HEADS, 0].set(b_fgate[l]))
    oa, la, g3 = _inproj(x2, mod3, w_in[l], w_conv[l], b_conv[l].reshape(1, -1), wa_pad,
                         b_gla_a[l].reshape(1, -1), bg, S=S, tm=tm_in, lm=lm)

    u_tri = jnp.asarray(np.triu(np.ones((lm, lm), np.float32)))
    hm = _mlstm(oa, g3, u_tri, mlstm_norm_g[l].reshape(1, -1), B=B, S=S, L=lm)
    w3_np, mk_np = _gla_consts()
    hg = _gla(oa, la, jnp.asarray(w3_np, BF16), jnp.asarray(mk_np), gla_norm_g[l].reshape(1, -1), B=B, S=S,
              nb=2 if B % 2 == 0 else 1)

    wr = (jnp.zeros((D, LANES), F32).at[:, 0:N_GROUPS].set(w_route_group[l])
          .at[:, SUBLANES:SUBLANES + N_EXP].set(w_route_expert[l]))
    br = (jnp.zeros((1, LANES), F32).at[0, 0:N_GROUPS].set(b_route_group[l])
          .at[0, SUBLANES:SUBLANES + N_EXP].set(b_route_expert[l]))
    wr_hi = wr.astype(BF16)
    wr2 = jnp.concatenate([wr_hi, (wr - wr_hi.astype(F32)).astype(BF16)], axis=1)
    x1, u2, rrow = _outproj(hm, hg, w_out[l], x2, mod3, ln1_g[l].reshape(1, -1),
                            ln1_b[l].reshape(1, -1), wr2, br, S=S, tb=tm, nh=2 if S % (2 * tm) == 0 else 1)

    u_cnt = jnp.asarray(np.triu(np.ones((tm, tm), np.float32)), BF16)
    ltri = jnp.asarray(np.tril(np.ones((N_EXP, N_EXP), np.float32), -1))
    srow, col, gd3, meta = _route(rrow, u_cnt, ltri, TM=FFN_TM)
    gd = gd3.reshape(N // tm + 2, LANES)
    n_tiles = _ffn_tiles(N, tm)
    te, nv = meta[0, :n_tiles], meta[1, 0:1]

    xs = _dispatch(gd, srow, u2, n_tiles=n_tiles, TM=FFN_TM)
    ys = _ffn(te, nv, xs, w_gate[l], w_up[l], w_down[l], TM=FFN_TM)
    out = _combine(gd, ys, col, x1, mod3, ln2_g[l].reshape(1, -1), ln2_b[l].reshape(1, -1), S=S, tb=tm)
    return out.reshape(B, S, D)


def kernel(x, c, w_ada, b_ada, w_in, w_conv, b_conv, b_igate, b_fgate, mlstm_norm_g, w_gla_a, b_gla_a,
           gla_norm_g, w_out, ln1_g, ln1_b, w_route_group, b_route_group, w_route_expert, b_route_expert,
           w_gate, w_up, w_down, ln2_g, ln2_b):
    for l in range(DEPTH):
        x = _layer(x, c, l, w_ada, b_ada, w_in, w_conv, b_conv, b_igate, b_fgate, mlstm_norm_g, w_gla_a,
                   b_gla_a, gla_norm_g, w_out, ln1_g, ln1_b, w_route_group, b_route_group, w_route_expert,
                   b_route_expert, w_gate, w_up, w_down, ln2_g, ln2_b)
    return x
```

```python
import functools

import numpy as np
import jax
import jax.numpy as jnp
from jax import lax
from jax.experimental import pallas as pl
from jax.experimental.pallas import tpu as pltpu

F32 = jnp.float32
BF16 = jnp.bfloat16
HIGHEST = lax.Precision.HIGHEST

DEPTH = 1
M_HEADS = 4
M_HD = 128
M_W = M_HEADS * M_HD
CONV_W = 4
G_HEADS = 4
G_DK = 64
G_DV = 128
G_W = G_HEADS * G_DV
G_KW = G_HEADS * G_DK
G_RANK = 16
G_TAU = 16.0
G_CHUNK = 64
N_GROUPS = 4
E_PER_G = 8
N_EXP = N_GROUPS * E_PER_G
D_EXP = 512
ALPHA = (2 * DEPTH) ** 0.25
LN_EPS = 1e-5

LANES = 128
SUBLANES = 8
VMEM_LIMIT = 48 * 1024 * 1024

C_QK = 0
C_VO = 1024
C_GQK = 2048
C_GV = 2560
C_GG = 3072
C_SMALL = 3584
C_TOT = 3712
SM_I, SM_F, SM_A = 0, 8, 16
IN_GATES = 4 * M_W
IN_G = IN_GATES + 2 * M_HEADS
IN_GA = IN_G + 2 * G_KW + 2 * G_W
IN_TOT = IN_GA + G_RANK

FFN_TM = 512
FFN_SUB = 2
GRAN = SUBLANES
G_LAST = LANES - 1


def _cparams(n_axes=1):
    return pltpu.CompilerParams(dimension_semantics=("arbitrary",) * n_axes,
                                vmem_limit_bytes=VMEM_LIMIT)


def _sigmoid(x):
    return 1.0 / (1.0 + jnp.exp(-x))


def _log_sigmoid(x):
    return jnp.minimum(x, 0.0) - jnp.log(1.0 + jnp.exp(-jnp.abs(x)))


def _ada_kernel(c_ref, w_ref, b_ref, o_ref):
    c = c_ref[...]
    ca = c * _sigmoid(c)
    o_ref[...] = jnp.dot(ca, w_ref[...], preferred_element_type=F32, precision=HIGHEST) + b_ref[...]


def _ada(c, w, b):
    B, D = c.shape
    n_out = w.shape[1]
    tn = 1024
    return pl.pallas_call(
        _ada_kernel,
        grid=(n_out // tn,),
        in_specs=[pl.BlockSpec((B, D), lambda j: (0, 0)),
                  pl.BlockSpec((D, tn), lambda j: (0, j)),
                  pl.BlockSpec((1, tn), lambda j: (0, j))],
        out_specs=pl.BlockSpec((B, tn), lambda j: (0, j)),
        out_shape=jax.ShapeDtypeStruct((B, n_out), F32),
        compiler_params=_cparams(),
        name="ada",
    )(c, w, b.reshape(1, n_out))


def _inproj_kernel(x_ref, mod_ref, win_ref, wc_ref, bc_ref, wa_ref, ba_ref, bg_ref,
                   oa_ref, la_ref, g_ref, halo_ref, w_ref, *, tm, tpb, lm):
    i = pl.program_id(0)

    @pl.when(i == 0)
    def _():
        rc = LANES
        for r in range(0, win_ref.shape[0], rc):
            rs = slice(r, r + rc)
            w_ref[rs, 0:IN_GATES] = win_ref[rs, 0:IN_GATES].astype(BF16)
            t = win_ref[rs, IN_GATES:IN_TOT]
            w_ref[rs, C_GQK:C_SMALL] = t[:, IN_G - IN_GATES:IN_GA - IN_GATES].astype(BF16)
            z = lambda n: jnp.zeros((rc, n), F32)
            small = jnp.concatenate([t[:, 0:M_HEADS], z(SM_F - M_HEADS), t[:, M_HEADS:2 * M_HEADS],
                                     z(SM_A - SM_F - M_HEADS), t[:, IN_GA - IN_GATES:IN_TOT - IN_GATES],
                                     z(LANES - SM_A - G_RANK)], axis=1)
            w_ref[rs, C_SMALL:C_TOT] = small.astype(BF16)

    @pl.when(i % tpb == 0)
    def _():
        halo_ref[...] = jnp.zeros_like(halo_ref)

    mod = mod_ref[0]
    u = (x_ref[...] * (1.0 + mod[1:2, :]) + mod[0:1, :]).astype(BF16)

    p = jnp.dot(u, w_ref[:, C_QK:C_QK + 2 * M_W], preferred_element_type=F32)
    ext = jnp.concatenate([halo_ref[...], p], axis=0)
    acc = bc_ref[...] + wc_ref[CONV_W - 1:CONV_W, :] * p
    for j in range(CONV_W - 1):
        sh = pltpu.roll(ext, CONV_W - 1 - j, 0)[SUBLANES:, :]
        acc = acc + wc_ref[j:j + 1, :] * sh
    halo_ref[...] = p[tm - SUBLANES:, :]
    qk = acc * _sigmoid(acc)
    oa_ref[:, C_QK:C_QK + M_W] = qk[:, :M_W].astype(BF16)
    oa_ref[:, C_QK + M_W:C_QK + 2 * M_W] = (qk[:, M_W:] * (M_HD ** -0.5)).astype(BF16)

    p = jnp.dot(u, w_ref[:, C_VO:C_VO + 2 * M_W], preferred_element_type=F32)
    oa_ref[:, C_VO:C_VO + 2 * M_W] = p.astype(BF16)

    p = jnp.dot(u, w_ref[:, C_GQK:C_GQK + G_KW], preferred_element_type=F32)
    oa_ref[:, C_GQK:C_GQK + G_KW] = (p * (G_DK ** -0.5)).astype(BF16)
    p = jnp.dot(u, w_ref[:, C_GQK + G_KW:C_SMALL], preferred_element_type=F32)
    oa_ref[:, C_GQK + G_KW:C_SMALL] = p.astype(BF16)

    ps = jnp.dot(u, w_ref[:, C_SMALL:C_TOT], preferred_element_type=F32)
    la = jnp.dot(ps.astype(BF16), wa_ref[...], preferred_element_type=F32) + ba_ref[...]
    la_ref[...] = _log_sigmoid(la) * (1.0 / G_TAU)
    pt = ps.T
    gi = pt[SM_I:SM_I + SUBLANES, :] + bg_ref[0:SUBLANES, :]
    gf = _log_sigmoid(pt[SM_F:SM_F + SUBLANES, :] + bg_ref[SUBLANES:2 * SUBLANES, :])
    for j in range(tm // lm):
        g_ref[j, 0:SUBLANES, :] = gi[:, j * lm:(j + 1) * lm]
        g_ref[j, SUBLANES:2 * SUBLANES, :] = gf[:, j * lm:(j + 1) * lm]


def _inproj(x2, mod3, w_in, w_conv, b_conv, wa_pad, b_gla, bg, *, S, tm, lm):
    N, D = x2.shape
    tpb = S // tm
    kern = functools.partial(_inproj_kernel, tm=tm, tpb=tpb, lm=lm)
    return pl.pallas_call(
        kern,
        grid=(N // tm,),
        in_specs=[pl.BlockSpec((tm, D), lambda i: (i, 0)),
                  pl.BlockSpec((1, 6, D), lambda i: (i // tpb, 0, 0)),
                  pl.BlockSpec((D, IN_TOT), lambda i: (0, 0), pipeline_mode=pl.Buffered(1)),
                  pl.BlockSpec((CONV_W, 2 * M_W), lambda i: (0, 0)),
                  pl.BlockSpec((1, 2 * M_W), lambda i: (0, 0)),
                  pl.BlockSpec((LANES, G_KW), lambda i: (0, 0)),
                  pl.BlockSpec((1, G_KW), lambda i: (0, 0)),
                  pl.BlockSpec((2 * SUBLANES, 1), lambda i: (0, 0))],
        out_specs=[pl.BlockSpec((tm, C_SMALL), lambda i: (i, 0)),
                   pl.BlockSpec((tm, G_KW), lambda i: (i, 0)),
                   pl.BlockSpec((tm // lm, 2 * SUBLANES, lm), lambda i: (i, 0, 0))],
        out_shape=[jax.ShapeDtypeStruct((N, C_SMALL), BF16),
                   jax.ShapeDtypeStruct((N, G_KW), F32),
                   jax.ShapeDtypeStruct((N // lm, 2 * SUBLANES, lm), F32)],
        scratch_shapes=[pltpu.VMEM((SUBLANES, 2 * M_W), F32), pltpu.VMEM((D, C_TOT), BF16)],
        compiler_params=_cparams(),
        name="inproj",
    )(x2, mod3, w_in, w_conv, b_conv, wa_pad, b_gla, bg)


def _mlstm_kernel(qk_ref, vo_ref, g_ref, u_ref, gain_ref, out_ref, c_ref, zt_ref, a_ref, dec_ref, *, L, NC):
    c_ref[...] = jnp.zeros_like(c_ref)
    lane = lax.broadcasted_iota(jnp.int32, (SUBLANES, L), 1)
    tril = (lax.broadcasted_iota(jnp.int32, (L, L), 0) >= lax.broadcasted_iota(jnp.int32, (L, L), 1))
    ones_v = jnp.ones((L, M_HD), BF16)
    zpad = jnp.zeros((LANES - 4 * SUBLANES, L), F32)

    bs, gs = [], []
    for c in range(NC):
        b = jnp.dot(g_ref[c, SUBLANES:2 * SUBLANES, :], u_ref[...], preferred_element_type=F32,
                    precision=HIGHEST)
        a = g_ref[c, 0:SUBLANES, :] - b
        a_ref[c] = a
        G = a
        s = 1
        while s < L:
            G = jnp.maximum(G, jnp.where(lane >= s, pltpu.roll(G, s, 1), -jnp.inf))
            s *= 2
        bs.append(b)
        gs.append(G)
    m_prev = jnp.zeros((SUBLANES, 1), F32)
    for c in range(NC):
        M = jnp.maximum(gs[c], m_prev)
        ML = M[:, L - 1:L]
        Z = jnp.concatenate([M, jnp.exp(m_prev - M), jnp.exp(-(bs[c] + M)), jnp.exp(a_ref[c] - ML), zpad],
                            axis=0)
        zt_ref[c] = Z.T
        dec_ref[c] = jnp.broadcast_to(jnp.exp(m_prev - ML), (SUBLANES, 2 * M_HD))
        m_prev = bs[c][:, L - 1:L] + ML

    def chunk(c, carry):
        r0 = pl.multiple_of(c * L, L)
        Zt = zt_ref[c]
        a = a_ref[c]
        dec = dec_ref[c]
        rows = pl.ds(r0, L)
        heads = range(M_HEADS)
        hs = [slice(h * M_HD, (h + 1) * M_HD) for h in heads]
        hs2 = [slice(M_W + h * M_HD, M_W + (h + 1) * M_HD) for h in heads]
        q = [qk_ref[rows, hs[h]] for h in heads]
        k = [qk_ref[rows, hs2[h]] for h in heads]
        vext = [jnp.concatenate([vo_ref[rows, hs[h]], ones_v], axis=1) for h in heads]
        cst = [c_ref[h] for h in heads]
        sc = [lax.dot_general(q[h], k[h], (((1,), (1,)), ((), ())), preferred_element_type=F32) for h in heads]
        qc = [jnp.dot(q[h], cst[h].astype(BF16), preferred_element_type=F32) for h in heads]
        pm = [(sc[h] * jnp.exp(jnp.where(tril, a[h:h + 1, :] - Zt[:, h:h + 1], -jnp.inf))).astype(BF16)
              for h in heads]
        pv = [jnp.dot(pm[h], vext[h], preferred_element_type=F32) for h in heads]
        kw = [(Zt[:, 3 * SUBLANES + h:3 * SUBLANES + h + 1] * k[h].astype(F32)).astype(BF16) for h in heads]
        upd = [lax.dot_general(kw[h], vext[h], (((0,), (0,)), ((), ())), preferred_element_type=F32)
               for h in heads]
        for h in heads:
            c_ref[h] = dec[h:h + 1, :] * cst[h] + upd[h]
            nd = pv[h] + Zt[:, SUBLANES + h:SUBLANES + h + 1] * qc[h]
            hh = nd[:, :M_HD] / jnp.maximum(jnp.abs(nd[:, M_HD:]), Zt[:, 2 * SUBLANES + h:2 * SUBLANES + h + 1])
            hh = _sigmoid(vo_ref[rows, hs2[h]].astype(F32)) * hh
            hn = hh * lax.rsqrt(jnp.mean(hh * hh, axis=-1, keepdims=True) + LN_EPS)
            out_ref[rows, hs[h]] = (hn * gain_ref[:, hs[h]]).astype(BF16)
        return carry

    lax.fori_loop(0, NC, chunk, 0)


def _mlstm(oa, g3, u_tri, gain, *, B, S, L):
    N = oa.shape[0]
    NC = S // L
    kern = functools.partial(_mlstm_kernel, L=L, NC=NC)
    return pl.pallas_call(
        kern,
        grid=(B,),
        in_specs=[pl.BlockSpec((S, 2 * M_W), lambda b: (b, C_QK // (2 * M_W))),
                  pl.BlockSpec((S, 2 * M_W), lambda b: (b, C_VO // (2 * M_W))),
                  pl.BlockSpec((NC, 2 * SUBLANES, L), lambda b: (b, 0, 0)),
                  pl.BlockSpec((L, L), lambda b: (0, 0)),
                  pl.BlockSpec((1, M_W), lambda b: (0, 0))],
        out_specs=pl.BlockSpec((S, M_W), lambda b: (b, 0)),
        out_shape=jax.ShapeDtypeStruct((N, M_W), BF16),
        scratch_shapes=[pltpu.VMEM((M_HEADS, M_HD, 2 * M_HD), F32), pltpu.VMEM((NC, L, LANES), F32),
                        pltpu.VMEM((NC, SUBLANES, L), F32), pltpu.VMEM((NC, SUBLANES, 2 * M_HD), F32)],
        compiler_params=_cparams(),
        name="mlstm",
    )(oa, oa, g3, u_tri, gain)


_G_LEVELS = 6
_G_XROW = 2 * G_CHUNK + SUBLANES


def _gla_consts():
    L = G_CHUNK
    t = np.arange(L)
    blocks = [(t[None, :] <= t[:, None]).astype(np.float32),
              (t[None, :] > t[:, None]).astype(np.float32),
              np.ones((SUBLANES, L), np.float32)]
    masks = [np.eye(L, dtype=np.float32)]
    m = 1
    while m < L:
        wl = np.zeros((L, L), np.float32)
        for r in range(L):
            r0 = (r // (2 * m)) * 2 * m + m
            if r % (2 * m) >= m:
                wl[r, r0:r + 1] = 1.0
            else:
                wl[r, r + 1:r0] = 1.0
        blocks.append(wl)
        tt, ss = t[:, None], t[None, :]
        masks.append(((tt // (2 * m) == ss // (2 * m)) & (tt % (2 * m) >= m)
                      & (ss % (2 * m) < m)).astype(np.float32))
        m *= 2
    w = np.concatenate(blocks, axis=0)
    w3 = np.concatenate([w, w, w], axis=1)
    mk = np.stack([np.concatenate([x] * G_HEADS, axis=0) for x in masks])
    return w3, mk


def _gla_kernel(qk_ref, v_ref, gg_ref, la_ref, w3_ref, mk_ref, gain_ref, out_ref, st_ref, *, NC, S, nb):
    L = G_CHUNK
    st_ref[...] = jnp.zeros_like(st_ref)
    lane_head = lax.broadcasted_iota(jnp.int32, (L, G_KW), 1) // G_DK
    br = lax.broadcasted_iota(jnp.int32, (2 * G_DV, LANES), 0) < G_DV
    bl = lax.broadcasted_iota(jnp.int32, (2 * G_DV, LANES), 1) < G_DK
    bmask = br == bl
    nt = (((1,), (1,)), ((), ()))
    tn = (((0,), (0,)), ((), ()))

    def chunk(c, carry):
        rows = [pl.ds(pl.multiple_of(bi * S + c * L, L), L) for bi in range(nb)]
        X, q, k = [], [], []
        for bi in range(nb):
            la = la_ref[rows[bi], :]
            hi = la.astype(BF16)
            r1 = la - hi.astype(F32)
            mid = r1.astype(BF16)
            lo = (r1 - mid.astype(F32)).astype(BF16)
            stk = jnp.concatenate([hi, mid, lo], axis=0)
            X.append(jnp.exp(jnp.dot(w3_ref[...], stk, preferred_element_type=F32)))
            q.append(qk_ref[rows[bi], 0:G_KW].astype(F32))
            k.append(qk_ref[rows[bi], G_KW:2 * G_KW].astype(F32))

        sc = [[None] * (_G_LEVELS + 1) for _ in range(nb)]
        for lev in range(_G_LEVELS + 1):
            for bi in range(nb):
                if lev == 0:
                    qt, kt = q[bi], k[bi]
                else:
                    xl = X[bi][_G_XROW + L * (lev - 1):_G_XROW + L * lev, :]
                    qt, kt = q[bi] * xl, k[bi] * xl
                q4 = jnp.concatenate([jnp.where(lane_head == h, qt, 0.0) for h in range(G_HEADS)],
                                     axis=0).astype(BF16)
                sc[bi][lev] = lax.dot_general(q4, kt.astype(BF16), nt, preferred_element_type=F32)
        Ab = []
        for bi in range(nb):
            A = sc[bi][0] * mk_ref[0]
            for lev in range(1, _G_LEVELS + 1):
                A = A + sc[bi][lev] * mk_ref[lev]
            Ab.append(A.astype(BF16))

        for bi in range(nb):
            gg = gg_ref[rows[bi], :].astype(F32)
            gate = gg * _sigmoid(gg)
            for p in range(2):
                ls = slice(LANES * p, LANES * (p + 1))
                vp = v_ref[rows[bi], 2 * G_DV * p:2 * G_DV * (p + 1)]
                oi = [jnp.dot(Ab[bi][L * (2 * p + hh):L * (2 * p + hh + 1)],
                              vp[:, G_DV * hh:G_DV * (hh + 1)], preferred_element_type=F32)
                      for hh in range(2)]
                st = st_ref[bi, p]
                qc = (q[bi][:, ls] * X[bi][0:L, ls]).astype(BF16)
                o_inter = lax.dot_general(qc, st.astype(BF16), nt, preferred_element_type=F32)
                kc = (k[bi][:, ls] * X[bi][L:2 * L, ls]).astype(BF16)
                upd = lax.dot_general(vp, kc, tn, preferred_element_type=F32)
                dec = X[bi][2 * L:2 * L + 1, ls]
                st_ref[bi, p] = jnp.where(bmask, dec * st + upd, 0.0)
                for hh in range(2):
                    o = o_inter[:, G_DV * hh:G_DV * (hh + 1)] + oi[hh]
                    hn = o * lax.rsqrt(jnp.mean(o * o, axis=-1, keepdims=True) + LN_EPS)
                    hs = slice(G_DV * (2 * p + hh), G_DV * (2 * p + hh + 1))
                    out_ref[rows[bi], hs] = (hn * gain_ref[:, hs] * gate[:, hs]).astype(BF16)
        return carry

    lax.fori_loop(0, NC, chunk, 0)


def _gla(oa, la, w3, mk, gain, *, B, S, nb):
    N = oa.shape[0]
    NC = S // G_CHUNK
    kern = functools.partial(_gla_kernel, NC=NC, S=S, nb=nb)
    R = nb * S
    return pl.pallas_call(
        kern,
        grid=(B // nb,),
        in_specs=[pl.BlockSpec((R, 2 * G_KW), lambda b: (b, C_GQK // (2 * G_KW))),
                  pl.BlockSpec((R, G_W), lambda b: (b, C_GV // G_W)),
                  pl.BlockSpec((R, G_W), lambda b: (b, C_GG // G_W)),
                  pl.BlockSpec((R, G_KW), lambda b: (b, 0)),
                  pl.BlockSpec(w3.shape, lambda b: (0, 0)),
                  pl.BlockSpec(mk.shape, lambda b: (0, 0, 0)),
                  pl.BlockSpec((1, G_W), lambda b: (0, 0))],
        out_specs=pl.BlockSpec((R, G_W), lambda b: (b, 0)),
        out_shape=jax.ShapeDtypeStruct((N, G_W), BF16),
        scratch_shapes=[pltpu.VMEM((nb, 2, 2 * G_DV, LANES), F32)],
        compiler_params=_cparams(),
        name="gla",
    )(oa, oa, oa, la, w3, mk, gain)


def _layer_norm(z, g, b):
    mu = jnp.mean(z, axis=-1, keepdims=True)
    zc = z - mu
    var = jnp.mean(zc * zc, axis=-1, keepdims=True)
    return zc * lax.rsqrt(var + LN_EPS) * g + b


def _outproj_kernel(hm_ref, hg_ref, wf_ref, x_ref, mod_ref, g_ref, b_ref, wr_ref, br_ref,
                    x1_ref, u2_ref, rrow_ref, w_ref, *, tb, nh):
    @pl.when(pl.program_id(0) == 0)
    def _():
        w_ref[...] = wf_ref[...].astype(BF16)

    mod = mod_ref[0]
    blocks = [slice(tb * j, tb * (j + 1)) for j in range(nh)]
    y = [jnp.dot(hm_ref[r, :], w_ref[0:M_W, :], preferred_element_type=F32)
         + jnp.dot(hg_ref[r, :], w_ref[M_W:M_W + G_W, :], preferred_element_type=F32) for r in blocks]
    u2 = []
    for j, r in enumerate(blocks):
        z = ALPHA * x_ref[r, :] + (1.0 + mod[2:3, :]) * y[j]
        x1 = _layer_norm(z, g_ref[...], b_ref[...])
        x1_ref[r, :] = x1
        u2.append(x1 * (1.0 + mod[4:5, :]) + mod[3:4, :])
        u2_ref[r, :] = u2[j].astype(BF16)

    u2h = [u.astype(BF16) for u in u2]
    u2l = [(u2[j] - u2h[j].astype(F32)).astype(BF16) for j in range(nh)]
    lh = [jnp.dot(u, wr_ref[...], preferred_element_type=F32) for u in u2h]
    ll = [jnp.dot(u, wr_ref[:, 0:LANES], preferred_element_type=F32) for u in u2l]
    for j in range(nh):
        logits = lh[j][:, 0:LANES] + lh[j][:, LANES:2 * LANES] + ll[j] + br_ref[...]
        rrow_ref[j] = _route_select(logits.T, tb)


def _route_select(lt, tm):
    row = lax.broadcasted_iota(jnp.int32, (SUBLANES, tm), 0)
    gl = jnp.where(row < N_GROUPS, lt[0:SUBLANES, :], -jnp.inf)
    gmax = jnp.max(gl, axis=0, keepdims=True)
    gsel = jnp.min(jnp.where(gl == gmax, row, SUBLANES), axis=0, keepdims=True)
    pg = 1.0 / jnp.sum(jnp.exp(gl - gmax), axis=0, keepdims=True)
    ein = jnp.zeros((SUBLANES, tm), F32)
    for g in range(N_GROUPS):
        ein = jnp.where(gsel == g, lt[SUBLANES * (g + 1):SUBLANES * (g + 2), :], ein)
    v1 = jnp.max(ein, axis=0, keepdims=True)
    i1 = jnp.min(jnp.where(ein == v1, row, SUBLANES), axis=0, keepdims=True)
    rest = jnp.where(row == i1, -jnp.inf, ein)
    v2 = jnp.max(rest, axis=0, keepdims=True)
    i2 = jnp.min(jnp.where(rest == v2, row, SUBLANES), axis=0, keepdims=True)
    t2 = jnp.exp(v2 - v1)
    p1 = 1.0 / (1.0 + t2)
    e0 = (gsel * E_PER_G + i1).astype(F32)
    e1 = (gsel * E_PER_G + i2).astype(F32)
    return jnp.concatenate([e0, e1, pg * p1, pg * (t2 * p1), jnp.zeros((SUBLANES - 4, tm), F32)], axis=0)


def _outproj(hm, hg, w_out, x2, mod3, g, b, wr, br, *, S, tb, nh):
    N, D = x2.shape
    tm = tb * nh
    tpb = S // tm
    kern = functools.partial(_outproj_kernel, tb=tb, nh=nh)
    return pl.pallas_call(
        kern,
        grid=(N // tm,),
        in_specs=[pl.BlockSpec((tm, M_W), lambda i: (i, 0)),
                  pl.BlockSpec((tm, G_W), lambda i: (i, 0)),
                  pl.BlockSpec((M_W + G_W, D), lambda i: (0, 0), pipeline_mode=pl.Buffered(1)),
                  pl.BlockSpec((tm, D), lambda i: (i, 0)),
                  pl.BlockSpec((1, 6, D), lambda i: (i // tpb, 0, 0)),
                  pl.BlockSpec((1, D), lambda i: (0, 0)),
                  pl.BlockSpec((1, D), lambda i: (0, 0)),
                  pl.BlockSpec((D, 2 * LANES), lambda i: (0, 0)),
                  pl.BlockSpec((1, LANES), lambda i: (0, 0))],
        out_specs=[pl.BlockSpec((tm, D), lambda i: (i, 0)),
                   pl.BlockSpec((tm, D), lambda i: (i, 0)),
                   pl.BlockSpec((nh, SUBLANES, tb), lambda i: (i, 0, 0))],
        out_shape=[jax.ShapeDtypeStruct((N, D), F32),
                   jax.ShapeDtypeStruct((N, D), BF16),
                   jax.ShapeDtypeStruct((N // tb, SUBLANES, tb), F32)],
        scratch_shapes=[pltpu.VMEM((M_W + G_W, D), BF16)],
        compiler_params=_cparams(),
        name="outproj",
    )(hm, hg, w_out, x2, mod3, g, b, wr, br)


def _slots_per_tile(tb):
    worst = 2 * tb + N_EXP * (GRAN - 1)
    return -(-worst // LANES) * LANES


def _ffn_tiles(n_tok, tb):
    worst_rows = 2 * n_tok + (n_tok // tb) * N_EXP * (GRAN - 1)
    return -(-worst_rows // FFN_TM) + N_EXP


def _route_kernel(rr_ref, u_ref, lt_ref, srow_ref, col_ref, gd_ref, meta_ref, mg_ref, part_ref,
                  *, NT, tb, TM):
    iota_e = lax.broadcasted_iota(jnp.int32, (N_EXP, tb), 0).astype(F32)
    glane = lax.broadcasted_iota(jnp.int32, (N_EXP, LANES), 1).astype(F32)
    ltri = lt_ref[...]

    def prefix_e(col):
        return jnp.dot(ltri, jnp.broadcast_to(col, (N_EXP, LANES)),
                       preferred_element_type=F32, precision=HIGHEST)[:, 0:1]

    def p1(j, run8):
        r = rr_ref[j]
        oh0 = jnp.where(iota_e == r[0:1, :], 1.0, 0.0)
        oh1 = jnp.where(iota_e == r[1:2, :], 1.0, 0.0)
        cum0 = jnp.dot(oh0.astype(BF16), u_ref[...], preferred_element_type=F32)
        cum1 = jnp.dot(oh1.astype(BF16), u_ref[...], preferred_element_type=F32)
        c0 = jnp.sum(oh0, axis=1, keepdims=True)
        n8 = jnp.floor((c0 + jnp.sum(oh1, axis=1, keepdims=True) + (GRAN - 1.0)) * (1.0 / GRAN))
        lo8 = prefix_e(n8)
        s0 = jnp.sum(oh0 * (GRAN * lo8 + cum0 - 1.0), axis=0, keepdims=True)
        s1 = jnp.sum(oh1 * (GRAN * lo8 + c0 + cum1 - 1.0), axis=0, keepdims=True)
        info = jnp.concatenate([s0, s1, r[2:4, :], jnp.zeros((SUBLANES - 4, tb), F32)], axis=0)
        srow_ref[j] = info
        col_ref[pl.ds(pl.multiple_of(j * tb, tb), tb), :] = jnp.concatenate(
            [info, jnp.zeros((LANES - SUBLANES, tb), F32)], axis=0).T
        mg = jnp.where((lo8 <= glane) & (glane < lo8 + n8), 1.0, 0.0)
        mg_ref[j] = mg
        part = jnp.sum(mg * (run8 + glane - lo8), axis=0, keepdims=True)
        gcnt = jnp.broadcast_to(jnp.sum(n8, axis=0, keepdims=True), (1, LANES))
        part_ref[j] = jnp.concatenate([part, gcnt, jnp.zeros((SUBLANES - 2, LANES), F32)], axis=0)
        return run8 + n8

    tot8 = lax.fori_loop(0, NT, p1, jnp.zeros((N_EXP, 1), F32), unroll=4 if NT % 4 == 0 else 1)
    seg_t = jnp.floor((tot8 * GRAN + (TM - 1.0)) * (1.0 / TM))
    base_t = prefix_e(seg_t)
    base8 = base_t * (TM // GRAN)
    lane1 = lax.broadcasted_iota(jnp.int32, (1, LANES), 1)

    def p2(j, carry):
        pr = part_ref[j]
        dst = (pr[0:1, :] + jnp.sum(mg_ref[j] * base8, axis=0, keepdims=True)) * GRAN
        gd_ref[j] = jnp.where(lane1 == G_LAST, pr[1:2, :], dst).astype(jnp.int32)
        return carry

    lax.fori_loop(0, NT, p2, 0, unroll=4 if NT % 4 == 0 else 1)
    eye = jnp.where(glane == lax.broadcasted_iota(jnp.int32, (N_EXP, LANES), 0).astype(F32), 1.0, 0.0)
    tail_row = jnp.sum(eye * ((base8 + tot8) * GRAN), axis=0, keepdims=True)
    tail_n8 = jnp.sum(eye * (seg_t * (TM // GRAN) - tot8), axis=0, keepdims=True)
    nv_l = jnp.broadcast_to(jnp.sum(seg_t, axis=0, keepdims=True), (1, LANES))
    gd_ref[NT] = jnp.where(lane1 == G_LAST, nv_l, tail_row).astype(jnp.int32)
    gd_ref[NT + 1] = tail_n8.astype(jnp.int32)
    ti = lax.broadcasted_iota(jnp.int32, (N_EXP, tb), 1).astype(F32)
    te = jnp.sum(jnp.where(base_t <= ti, 1.0, 0.0), axis=0, keepdims=True) - 1.0
    nv = jnp.broadcast_to(jnp.sum(seg_t, axis=0, keepdims=True), (1, tb))
    meta_ref[...] = jnp.concatenate([te, nv, jnp.zeros((SUBLANES - 2, tb), F32)],
                                    axis=0).astype(jnp.int32)


def _route(rrow, u_cnt, ltri, *, TM):
    NT, _, tb = rrow.shape
    kern = functools.partial(_route_kernel, NT=NT, tb=tb, TM=TM)
    full3 = lambda i: (0, 0, 0)
    return pl.pallas_call(
        kern,
        grid=(1,),
        in_specs=[pl.BlockSpec((NT, SUBLANES, tb), full3),
                  pl.BlockSpec((tb, tb), lambda i: (0, 0)),
                  pl.BlockSpec((N_EXP, N_EXP), lambda i: (0, 0))],
        out_specs=[pl.BlockSpec((NT, SUBLANES, tb), full3),
                   pl.BlockSpec((NT * tb, LANES), lambda i: (0, 0)),
                   pl.BlockSpec((NT + 2, 1, LANES), full3),
                   pl.BlockSpec((SUBLANES, tb), lambda i: (0, 0))],
        out_shape=[jax.ShapeDtypeStruct((NT, SUBLANES, tb), F32),
                   jax.ShapeDtypeStruct((NT * tb, LANES), F32),
                   jax.ShapeDtypeStruct((NT + 2, 1, LANES), jnp.int32),
                   jax.ShapeDtypeStruct((SUBLANES, tb), jnp.int32)],
        scratch_shapes=[pltpu.VMEM((NT, N_EXP, LANES), F32), pltpu.VMEM((NT, SUBLANES, LANES), F32)],
        compiler_params=_cparams(),
        name="route",
    )(rrow, u_cnt, ltri)


U32 = jnp.uint32
_HI_MASK = 0xFFFF0000


def _pack_halves(x):
    c = x.shape[1] // 2
    lo = lax.bitcast_convert_type(x[:, :c], U32)
    hi = lax.bitcast_convert_type(x[:, c:], U32)
    return (lo >> 16) | (hi & U32(_HI_MASK))


def _unpack_halves(w):
    lo = lax.bitcast_convert_type(w << 16, F32)
    hi = lax.bitcast_convert_type(w & U32(_HI_MASK), F32)
    return jnp.concatenate([lo, hi], axis=1).astype(BF16)


def _granule_copy(src_ref, src_row, dst_ref, dst_row, sem):
    return pltpu.make_async_copy(src_ref.at[pl.ds(src_row, GRAN), :], dst_ref.at[pl.ds(dst_row, GRAN), :], sem)


def _wait_granules(n, src_ref, dst_ref, sem, n_max):
    b = 1
    while b <= n_max:
        @pl.when((n & b) != 0)
        def _(b=b):
            pltpu.make_async_copy(src_ref.at[pl.ds(0, b * GRAN), :], dst_ref.at[pl.ds(0, b * GRAN), :],
                                  sem).wait()
        b *= 2


def _dispatch_kernel(gd_ref, srow_ref, u_ref, xs_ref, buf, zbuf, sems, *, NT, SL, TM, n_tiles):
    j = pl.program_id(0)
    slot = j % 2
    zsem = sems.at[2]

    def drain(tile, sl):
        _wait_granules(gd_ref[tile, G_LAST], buf.at[sl], xs_ref, sems.at[sl], SL // GRAN)

    def tile_fill(t):
        return pltpu.make_async_copy(zbuf, xs_ref.at[pl.ds(pl.multiple_of(t * TM, TM), TM), :], zsem)

    def zero_fill(wait):
        for e in range(N_EXP):
            def zg(g, carry, e=e):
                cp = _granule_copy(zbuf, 0, xs_ref, pl.multiple_of(gd_ref[NT, e] + g * GRAN, GRAN), zsem)
                cp.wait() if wait else cp.start()
                return carry
            lax.fori_loop(0, gd_ref[NT + 1, e], zg, 0)

        def zt(t, carry):
            tile_fill(t).wait() if wait else tile_fill(t).start()
            return carry
        lax.fori_loop(gd_ref[NT, G_LAST], n_tiles, zt, 0)

    @pl.when(j == 0)
    def _():
        zbuf[...] = jnp.zeros_like(zbuf)
        zero_fill(False)

    @pl.when(j >= 2)
    def _():
        drain(j - 2, slot)

    s = srow_ref[0]
    rows = lax.broadcasted_iota(jnp.int32, (SL, s.shape[1]), 0).astype(F32)
    m0 = rows == s[0:1, :]
    m1 = rows == s[1:2, :]
    oh = jnp.where(m0 | m1, 1.0, 0.0).astype(BF16)
    dw = u_ref.shape[1] // 2
    buf[slot, :, 0:dw] = _pack_halves(jnp.dot(oh, u_ref[...], preferred_element_type=F32))
    wrow = jnp.sum(jnp.where(m0, s[2:3, :], 0.0) + jnp.where(m1, s[3:4, :], 0.0), axis=1, keepdims=True)
    buf[slot, :, dw:dw + LANES] = lax.bitcast_convert_type(jnp.broadcast_to(wrow, (SL, LANES)), U32)

    def issue(g, carry):
        _granule_copy(buf.at[slot], pl.multiple_of(g * GRAN, GRAN), xs_ref,
                      pl.multiple_of(gd_ref[j, g], GRAN), sems.at[slot]).start()
        return carry

    lax.fori_loop(0, gd_ref[j, G_LAST], issue, 0)

    @pl.when(j == NT - 1)
    def _():
        drain(j, slot)
        if NT > 1:
            drain(j - 1, 1 - slot)
        zero_fill(True)


def _dispatch(gd, srow, u2, *, n_tiles, TM):
    N, D = u2.shape
    NT, _, tb = srow.shape
    SL = _slots_per_tile(tb)
    n_rows = n_tiles * TM
    kern = functools.partial(_dispatch_kernel, NT=NT, SL=SL, TM=TM, n_tiles=n_tiles)
    grid_spec = pltpu.PrefetchScalarGridSpec(
        num_scalar_prefetch=1,
        grid=(NT,),
        in_specs=[pl.BlockSpec((1, SUBLANES, tb), lambda j, gd: (j, 0, 0)),
                  pl.BlockSpec((tb, D), lambda j, gd: (j, 0))],
        out_specs=pl.BlockSpec(memory_space=pl.ANY),
        scratch_shapes=[pltpu.VMEM((2, SL, D // 2 + LANES), U32), pltpu.VMEM((TM, D // 2 + LANES), U32),
                        pltpu.SemaphoreType.DMA((3,))],
    )
    return pl.pallas_call(
        kern,
        grid_spec=grid_spec,
        out_shape=jax.ShapeDtypeStruct((n_rows, D // 2 + LANES), U32),
        compiler_params=_cparams(),
        name="dispatch",
    )(gd, srow, u2)


def _ffn_kernel(te_ref, nv_ref, xs_ref, wg_ref, wu_ref, wd_ref, o_ref, wgb, wub, wdb, sg, su, sd, slot_ref,
                sems):
    i = pl.program_id(0)
    nv = nv_ref[0]
    e = te_ref[i]

    def weight_copies(ex, sl):
        return (pltpu.make_async_copy(wg_ref.at[ex], sg.at[sl], sems.at[sl]),
                pltpu.make_async_copy(wu_ref.at[ex], su.at[sl], sems.at[sl]),
                pltpu.make_async_copy(wd_ref.at[ex], sd.at[sl], sems.at[sl]))

    @pl.when(i == 0)
    def _():
        slot_ref[0] = 0
        for cp in weight_copies(e, 0):
            cp.start()

    @pl.when((i < nv) & ((i == 0) | (e != te_ref[jnp.maximum(i - 1, 0)])))
    def _():
        sl = slot_ref[0]
        for cp in weight_copies(e, sl):
            cp.wait()
        wgb[...] = sg[sl].astype(BF16)
        wub[...] = su[sl].astype(BF16)
        wdb[...] = sd[sl].astype(BF16)
        nxt = lax.while_loop(lambda t: (t < nv) & (te_ref[jnp.minimum(t, nv - 1)] == e), lambda t: t + 1, i + 1)

        @pl.when(nxt < nv)
        def _():
            for cp in weight_copies(te_ref[nxt], 1 - sl):
                cp.start()
        slot_ref[0] = 1 - sl

    @pl.when(i < nv)
    def _():
        nsub = FFN_SUB
        hm = xs_ref.shape[0] // nsub
        dw = o_ref.shape[1]
        halves = tuple(slice(hm * j, hm * (j + 1)) for j in range(nsub))
        x = [_unpack_halves(xs_ref[r, 0:dw]) for r in halves]
        g = [jnp.dot(x[j], wgb[...], preferred_element_type=F32) for j in range(nsub)]
        u = [jnp.dot(x[j], wub[...], preferred_element_type=F32) for j in range(nsub)]
        h = [(g[j] * _sigmoid(g[j]) * u[j]).astype(BF16) for j in range(nsub)]
        y = [jnp.dot(h[j], wdb[...], preferred_element_type=F32) for j in range(nsub)]
        for j in range(nsub):
            wt = lax.bitcast_convert_type(xs_ref[halves[j], dw:dw + LANES], F32)
            yw = y[j] * jnp.concatenate([wt] * (2 * dw // LANES), axis=1)
            o_ref[halves[j], :] = _pack_halves(yw.astype(BF16).astype(F32))

    @pl.when(i >= nv_ref[0])
    def _():
        o_ref[...] = jnp.zeros_like(o_ref)


def _ffn(te, nv, xs, wg, wu, wd, *, TM):
    P, XW = xs.shape
    DW = XW - LANES
    D = 2 * DW
    n_tiles = P // TM
    grid_spec = pltpu.PrefetchScalarGridSpec(
        num_scalar_prefetch=2,
        grid=(n_tiles,),
        in_specs=[pl.BlockSpec((TM, XW), lambda i, te, nv: (jnp.maximum(jnp.minimum(i, nv[0] - 1), 0), 0)),
                  pl.BlockSpec(memory_space=pl.ANY),
                  pl.BlockSpec(memory_space=pl.ANY),
                  pl.BlockSpec(memory_space=pl.ANY)],
        out_specs=pl.BlockSpec((TM, DW), lambda i, te, nv: (i, 0)),
        scratch_shapes=[pltpu.VMEM((D, D_EXP), BF16), pltpu.VMEM((D, D_EXP), BF16),
                        pltpu.VMEM((D_EXP, D), BF16),
                        pltpu.VMEM((2, D, D_EXP), F32), pltpu.VMEM((2, D, D_EXP), F32),
                        pltpu.VMEM((2, D_EXP, D), F32), pltpu.SMEM((1,), jnp.int32),
                        pltpu.SemaphoreType.DMA((2,))],
    )
    return pl.pallas_call(
        _ffn_kernel,
        grid_spec=grid_spec,
        out_shape=jax.ShapeDtypeStruct((P, DW), U32),
        compiler_params=_cparams(),
        name="ffn",
    )(te, nv, xs, wg, wu, wd)


def _combine_kernel(gd_ref, ys_ref, col_ref, x1_ref, mod_ref, g_ref, b_ref, o_ref, buf, sems, *, NT, SL):
    j = pl.program_id(0)
    slot = j % 2

    def fetch(tile, sl):
        def f(g, carry):
            _granule_copy(ys_ref, pl.multiple_of(gd_ref[tile, g], GRAN), buf.at[sl],
                          pl.multiple_of(g * GRAN, GRAN), sems.at[sl]).start()
            return carry
        lax.fori_loop(0, gd_ref[tile, G_LAST], f, 0)

    @pl.when(j == 0)
    def _():
        fetch(0, 0)

    @pl.when(j + 1 < NT)
    def _():
        fetch(j + 1, 1 - slot)

    ng = gd_ref[j, G_LAST]

    _wait_granules(ng, ys_ref, buf.at[slot], sems.at[slot], SL // GRAN)

    rows = lax.broadcasted_iota(jnp.int32, (SL, 1), 0)
    yb = _unpack_halves(jnp.where(rows < ng * GRAN, buf[slot], U32(0)))
    col = col_ref[...]
    tb = col.shape[0]
    lanes = lax.broadcasted_iota(jnp.int32, (tb, SL), 1).astype(F32)
    sel = jnp.where((lanes == col[:, 0:1]) | (lanes == col[:, 1:2]), 1.0, 0.0).astype(BF16)
    y = jnp.dot(sel, yb, preferred_element_type=F32)
    mod = mod_ref[0]
    z = ALPHA * x1_ref[...] + (1.0 + mod[5:6, :]) * y
    o_ref[...] = _layer_norm(z, g_ref[...], b_ref[...])


def _combine(gd, ys, col, x1, mod3, g, b, *, S, tb):
    N, D = x1.shape
    NT = N // tb
    tpb = S // tb
    SL = _slots_per_tile(tb)
    kern = functools.partial(_combine_kernel, NT=NT, SL=SL)
    grid_spec = pltpu.PrefetchScalarGridSpec(
        num_scalar_prefetch=1,
        grid=(NT,),
        in_specs=[pl.BlockSpec(memory_space=pl.ANY),
                  pl.BlockSpec((tb, LANES), lambda j, gd: (j, 0)),
                  pl.BlockSpec((tb, D), lambda j, gd: (j, 0)),
                  pl.BlockSpec((1, 6, D), lambda j, gd: (j // tpb, 0, 0)),
                  pl.BlockSpec((1, D), lambda j, gd: (0, 0)),
                  pl.BlockSpec((1, D), lambda j, gd: (0, 0))],
        out_specs=pl.BlockSpec((tb, D), lambda j, gd: (j, 0)),
        scratch_shapes=[pltpu.VMEM((2, SL, D // 2), U32), pltpu.SemaphoreType.DMA((2,))],
    )
    return pl.pallas_call(
        kern,
        grid_spec=grid_spec,
        out_shape=jax.ShapeDtypeStruct((N, D), F32),
        compiler_params=_cparams(),
        name="combine",
    )(gd, ys, col, x1, mod3, g, b)


def _layer(x, c, l, w_ada, b_ada, w_in, w_conv, b_conv, b_igate, b_fgate, mlstm_norm_g, w_gla_a, b_gla_a,
           gla_norm_g, w_out, ln1_g, ln1_b, w_route_group, b_route_group, w_route_expert, b_route_expert,
           w_gate, w_up, w_down, ln2_g, ln2_b):
    B, S, D = x.shape
    N = B * S
    x2 = x.reshape(N, D)
    tm_in = min(512, S)
    tm = min(256, S)
    lm = min(256, S)

    mod3 = _ada(c, w_ada[l], b_ada[l]).reshape(B, 6, D)

    wa_pad = jnp.zeros((LANES, G_KW), F32).at[SM_A:SM_A + G_RANK].set(w_gla_a[l]).astype(BF16)
    bg = (jnp.zeros((2 * SUBLANES, 1), F32).at[0:M_HEADS, 0].set(b_igate[l])
          .at[SUBLANES:SUBLANES + M_HEADS, 0].set(b_fgate[l]))
    oa, la, g3 = _inproj(x2, mod3, w_in[l], w_conv[l], b_conv[l].reshape(1, -1), wa_pad,
                         b_gla_a[l].reshape(1, -1), bg, S=S, tm=tm_in, lm=lm)

    u_tri = jnp.asarray(np.triu(np.ones((lm, lm), np.float32)))
    hm = _mlstm(oa, g3, u_tri, mlstm_norm_g[l].reshape(1, -1), B=B, S=S, L=lm)
    w3_np, mk_np = _gla_consts()
    hg = _gla(oa, la, jnp.asarray(w3_np, BF16), jnp.asarray(mk_np), gla_norm_g[l].reshape(1, -1), B=B, S=S,
              nb=2 if B % 2 == 0 else 1)

    wr = (jnp.zeros((D, LANES), F32).at[:, 0:N_GROUPS].set(w_route_group[l])
          .at[:, SUBLANES:SUBLANES + N_EXP].set(w_route_expert[l]))
    br = (jnp.zeros((1, LANES), F32).at[0, 0:N_GROUPS].set(b_route_group[l])
          .at[0, SUBLANES:SUBLANES + N_EXP].set(b_route_expert[l]))
    wr_hi = wr.astype(BF16)
    wr2 = jnp.concatenate([wr_hi, (wr - wr_hi.astype(F32)).astype(BF16)], axis=1)
    x1, u2, rrow = _outproj(hm, hg, w_out[l], x2, mod3, ln1_g[l].reshape(1, -1),
                            ln1_b[l].reshape(1, -1), wr2, br, S=S, tb=tm, nh=2 if S % (2 * tm) == 0 else 1)

    u_cnt = jnp.asarray(np.triu(np.ones((tm, tm), np.float32)), BF16)
    ltri = jnp.asarray(np.tril(np.ones((N_EXP, N_EXP), np.float32), -1))
    srow, col, gd3, meta = _route(rrow, u_cnt, ltri, TM=FFN_TM)
    gd = gd3.reshape(N // tm + 2, LANES)
    n_tiles = _ffn_tiles(N, tm)
    te, nv = meta[0, :n_tiles], meta[1, 0:1]

    xs = _dispatch(gd, srow, u2, n_tiles=n_tiles, TM=FFN_TM)
    ys = _ffn(te, nv, xs, w_gate[l], w_up[l], w_down[l], TM=FFN_TM)
    out = _combine(gd, ys, col, x1, mod3, ln2_g[l].reshape(1, -1), ln2_b[l].reshape(1, -1), S=S, tb=tm)
    return out.reshape(B, S, D)


def kernel(x, c, w_ada, b_ada, w_in, w_conv, b_conv, b_igate, b_fgate, mlstm_norm_g, w_gla_a, b_gla_a,
           gla_norm_g, w_out, ln1_g, ln1_b, w_route_group, b_route_group, w_route_expert, b_route_expert,
           w_gate, w_up, w_down, ln2_g, ln2_b):
    for l in range(DEPTH):
        x = _layer(x, c, l, w_ada, b_ada, w_in, w_conv, b_conv, b_igate, b_fgate, mlstm_norm_g, w_gla_a,
                   b_gla_a, gla_norm_g, w_out, ln1_g, ln1_b, w_route_group, b_route_group, w_route_expert,
                   b_route_expert, w_gate, w_up, w_down, ln2_g, ln2_b)
    return x
```

```python
import functools

import numpy as np
import jax
import jax.numpy as jnp
from jax import lax
from jax.experimental import pallas as pl
from jax.experimental.pallas import tpu as pltpu

F32 = jnp.float32
BF16 = jnp.bfloat16
HIGHEST = lax.Precision.HIGHEST

DEPTH = 1
M_HEADS = 4
M_HD = 128
M_W = M_HEADS * M_HD
CONV_W = 4
G_HEADS = 4
G_DK = 64
G_DV = 128
G_W = G_HEADS * G_DV
G_KW = G_HEADS * G_DK
G_RANK = 16
G_TAU = 16.0
G_CHUNK = 64
N_GROUPS = 4
E_PER_G = 8
N_EXP = N_GROUPS * E_PER_G
D_EXP = 512
ALPHA = (2 * DEPTH) ** 0.25
LN_EPS = 1e-5

LANES = 128
SUBLANES = 8
VMEM_LIMIT = 48 * 1024 * 1024
VMEM_LIMIT_BIG = 56 * 1024 * 1024

C_QK = 0
C_VO = 1024
C_GQK = 2048
C_GV = 2560
C_GG = 3072
C_SMALL = 3584
C_TOT = 3712
SM_I, SM_F, SM_A = 0, 8, 16
IN_GATES = 4 * M_W
IN_G = IN_GATES + 2 * M_HEADS
IN_GA = IN_G + 2 * G_KW + 2 * G_W
IN_TOT = IN_GA + G_RANK

FFN_TM = 512
FFN_SUB = 2
GRAN = SUBLANES
G_LAST = LANES - 1


def _cparams(n_axes=1, vmem_limit=VMEM_LIMIT):
    return pltpu.CompilerParams(dimension_semantics=("arbitrary",) * n_axes,
                                vmem_limit_bytes=vmem_limit)


def _sigmoid(x):
    return 1.0 / (1.0 + jnp.exp(-x))


def _log_sigmoid(x):
    return jnp.minimum(x, 0.0) - jnp.log(1.0 + jnp.exp(-jnp.abs(x)))


def _ada_kernel(c_ref, w_ref, b_ref, o_ref):
    c = c_ref[...]
    ca = c * _sigmoid(c)
    o_ref[...] = jnp.dot(ca, w_ref[...], preferred_element_type=F32, precision=HIGHEST) + b_ref[...]


def _ada(c, w, b):
    B, D = c.shape
    n_out = w.shape[1]
    tn = 1024
    return pl.pallas_call(
        _ada_kernel,
        grid=(n_out // tn,),
        in_specs=[pl.BlockSpec((B, D), lambda j: (0, 0)),
                  pl.BlockSpec((D, tn), lambda j: (0, j)),
                  pl.BlockSpec((1, tn), lambda j: (0, j))],
        out_specs=pl.BlockSpec((B, tn), lambda j: (0, j)),
        out_shape=jax.ShapeDtypeStruct((B, n_out), F32),
        compiler_params=_cparams(),
        name="ada",
    )(c, w, b.reshape(1, n_out))


def _inproj_kernel(x_ref, mod_ref, win_ref, wc_ref, bc_ref, wa_ref, ba_ref, bg_ref,
                   oa_ref, la_ref, g_ref, halo_ref, w_ref, *, tm, tpb, lm):
    i = pl.program_id(0)

    @pl.when(i == 0)
    def _():
        rc = LANES
        for r in range(0, win_ref.shape[0], rc):
            rs = slice(r, r + rc)
            w_ref[rs, 0:IN_GATES] = win_ref[rs, 0:IN_GATES].astype(BF16)
            t = win_ref[rs, IN_GATES:IN_TOT]
            w_ref[rs, C_GQK:C_SMALL] = t[:, IN_G - IN_GATES:IN_GA - IN_GATES].astype(BF16)
            z = lambda n: jnp.zeros((rc, n), F32)
            small = jnp.concatenate([t[:, 0:M_HEADS], z(SM_F - M_HEADS), t[:, M_HEADS:2 * M_HEADS],
                                     z(SM_A - SM_F - M_HEADS), t[:, IN_GA - IN_GATES:IN_TOT - IN_GATES],
                                     z(LANES - SM_A - G_RANK)], axis=1)
            w_ref[rs, C_SMALL:C_TOT] = small.astype(BF16)

    @pl.when(i % tpb == 0)
    def _():
        halo_ref[...] = jnp.zeros_like(halo_ref)

    mod = mod_ref[0]
    u = (x_ref[...] * (1.0 + mod[1:2, :]) + mod[0:1, :]).astype(BF16)

    p = jnp.dot(u, w_ref[:, C_QK:C_QK + 2 * M_W], preferred_element_type=F32)
    ext = jnp.concatenate([halo_ref[...], p], axis=0)
    acc = bc_ref[...] + wc_ref[CONV_W - 1:CONV_W, :] * p
    for j in range(CONV_W - 1):
        sh = pltpu.roll(ext, CONV_W - 1 - j, 0)[SUBLANES:, :]
        acc = acc + wc_ref[j:j + 1, :] * sh
    halo_ref[...] = p[tm - SUBLANES:, :]
    qk = acc * _sigmoid(acc)
    oa_ref[:, C_QK:C_QK + M_W] = qk[:, :M_W].astype(BF16)
    oa_ref[:, C_QK + M_W:C_QK + 2 * M_W] = (qk[:, M_W:] * (M_HD ** -0.5)).astype(BF16)

    p = jnp.dot(u, w_ref[:, C_VO:C_VO + 2 * M_W], preferred_element_type=F32)
    oa_ref[:, C_VO:C_VO + 2 * M_W] = p.astype(BF16)

    p = jnp.dot(u, w_ref[:, C_GQK:C_GQK + G_KW], preferred_element_type=F32)
    oa_ref[:, C_GQK:C_GQK + G_KW] = (p * (G_DK ** -0.5)).astype(BF16)
    p = jnp.dot(u, w_ref[:, C_GQK + G_KW:C_SMALL], preferred_element_type=F32)
    oa_ref[:, C_GQK + G_KW:C_SMALL] = p.astype(BF16)

    ps = jnp.dot(u, w_ref[:, C_SMALL:C_TOT], preferred_element_type=F32)
    la = jnp.dot(ps.astype(BF16), wa_ref[...], preferred_element_type=F32) + ba_ref[...]
    la_ref[...] = _log_sigmoid(la) * (1.0 / G_TAU)
    pt = ps.T
    gi = pt[SM_I:SM_I + SUBLANES, :] + bg_ref[0:SUBLANES, :]
    gf = _log_sigmoid(pt[SM_F:SM_F + SUBLANES, :] + bg_ref[SUBLANES:2 * SUBLANES, :])
    for j in range(tm // lm):
        g_ref[j, 0:SUBLANES, :] = gi[:, j * lm:(j + 1) * lm]
        g_ref[j, SUBLANES:2 * SUBLANES, :] = gf[:, j * lm:(j + 1) * lm]


def _inproj(x2, mod3, w_in, w_conv, b_conv, wa_pad, b_gla, bg, *, S, tm, lm):
    N, D = x2.shape
    tpb = S // tm
    kern = functools.partial(_inproj_kernel, tm=tm, tpb=tpb, lm=lm)
    return pl.pallas_call(
        kern,
        grid=(N // tm,),
        in_specs=[pl.BlockSpec((tm, D), lambda i: (i, 0)),
                  pl.BlockSpec((1, 6, D), lambda i: (i // tpb, 0, 0)),
                  pl.BlockSpec((D, IN_TOT), lambda i: (0, 0), pipeline_mode=pl.Buffered(1)),
                  pl.BlockSpec((CONV_W, 2 * M_W), lambda i: (0, 0)),
                  pl.BlockSpec((1, 2 * M_W), lambda i: (0, 0)),
                  pl.BlockSpec((LANES, G_KW), lambda i: (0, 0)),
                  pl.BlockSpec((1, G_KW), lambda i: (0, 0)),
                  pl.BlockSpec((2 * SUBLANES, 1), lambda i: (0, 0))],
        out_specs=[pl.BlockSpec((tm, C_SMALL), lambda i: (i, 0)),
                   pl.BlockSpec((tm, G_KW), lambda i: (i, 0)),
                   pl.BlockSpec((tm // lm, 2 * SUBLANES, lm), lambda i: (i, 0, 0))],
        out_shape=[jax.ShapeDtypeStruct((N, C_SMALL), BF16),
                   jax.ShapeDtypeStruct((N, G_KW), F32),
                   jax.ShapeDtypeStruct((N // lm, 2 * SUBLANES, lm), F32)],
        scratch_shapes=[pltpu.VMEM((SUBLANES, 2 * M_W), F32), pltpu.VMEM((D, C_TOT), BF16)],
        compiler_params=_cparams(),
        name="inproj",
    )(x2, mod3, w_in, w_conv, b_conv, wa_pad, b_gla, bg)


def _mlstm_kernel(qk_ref, vo_ref, g_ref, u_ref, gain_ref, out_ref, c_ref, zt_ref, a_ref, dec_ref,
                  *, L, NC, S, nb):
    c_ref[...] = jnp.zeros_like(c_ref)
    lane = lax.broadcasted_iota(jnp.int32, (SUBLANES, L), 1)
    tril = (lax.broadcasted_iota(jnp.int32, (L, L), 0) >= lax.broadcasted_iota(jnp.int32, (L, L), 1))
    ones_v = jnp.ones((L, M_HD), BF16)
    zpad = jnp.zeros((LANES - 4 * SUBLANES, L), F32)

    for bi in range(nb):
        bs, gs = [], []
        for c in range(NC):
            ci = bi * NC + c
            b = jnp.dot(g_ref[ci, SUBLANES:2 * SUBLANES, :], u_ref[...], preferred_element_type=F32,
                        precision=HIGHEST)
            a = g_ref[ci, 0:SUBLANES, :] - b
            a_ref[ci] = a
            G = a
            s = 1
            while s < L:
                G = jnp.maximum(G, jnp.where(lane >= s, pltpu.roll(G, s, 1), -jnp.inf))
                s *= 2
            bs.append(b)
            gs.append(G)
        m_prev = jnp.zeros((SUBLANES, 1), F32)
        for c in range(NC):
            ci = bi * NC + c
            M = jnp.maximum(gs[c], m_prev)
            ML = M[:, L - 1:L]
            Z = jnp.concatenate([M, jnp.exp(m_prev - M), jnp.exp(-(bs[c] + M)), jnp.exp(a_ref[ci] - ML), zpad],
                                axis=0)
            zt_ref[ci] = Z.T
            dec_ref[ci] = jnp.broadcast_to(jnp.exp(m_prev - ML), (SUBLANES, 2 * M_HD))
            m_prev = bs[c][:, L - 1:L] + ML

    chains = [(bi, h) for bi in range(nb) for h in range(M_HEADS)]
    nt = (((1,), (1,)), ((), ()))
    tn = (((0,), (0,)), ((), ()))

    def chunk(c, carry):
        rows = [pl.ds(pl.multiple_of(bi * S + c * L, L), L) for bi in range(nb)]
        Zt = [zt_ref[bi * NC + c] for bi in range(nb)]
        a = [a_ref[bi * NC + c] for bi in range(nb)]
        dec = [dec_ref[bi * NC + c] for bi in range(nb)]
        hs = [slice(h * M_HD, (h + 1) * M_HD) for h in range(M_HEADS)]
        hs2 = [slice(M_W + h * M_HD, M_W + (h + 1) * M_HD) for h in range(M_HEADS)]
        q = [qk_ref[rows[bi], hs[h]] for bi, h in chains]
        k = [qk_ref[rows[bi], hs2[h]] for bi, h in chains]
        vext = [jnp.concatenate([vo_ref[rows[bi], hs[h]], ones_v], axis=1) for bi, h in chains]
        cst = [c_ref[bi * M_HEADS + h] for bi, h in chains]
        n = range(len(chains))
        sc = [lax.dot_general(q[i], k[i], nt, preferred_element_type=F32) for i in n]
        qc = [jnp.dot(q[i], cst[i].astype(BF16), preferred_element_type=F32) for i in n]
        pm = [(sc[i] * jnp.exp(jnp.where(tril, a[bi][h:h + 1, :] - Zt[bi][:, h:h + 1], -jnp.inf))).astype(BF16)
              for i, (bi, h) in enumerate(chains)]
        pv = [jnp.dot(pm[i], vext[i], preferred_element_type=F32) for i in n]
        kw = [(Zt[bi][:, 3 * SUBLANES + h:3 * SUBLANES + h + 1] * k[i].astype(F32)).astype(BF16)
              for i, (bi, h) in enumerate(chains)]
        upd = [lax.dot_general(kw[i], vext[i], tn, preferred_element_type=F32) for i in n]
        for i, (bi, h) in enumerate(chains):
            c_ref[bi * M_HEADS + h] = dec[bi][h:h + 1, :] * cst[i] + upd[i]
            nd = pv[i] + Zt[bi][:, SUBLANES + h:SUBLANES + h + 1] * qc[i]
            hh = nd[:, :M_HD] / jnp.maximum(jnp.abs(nd[:, M_HD:]),
                                            Zt[bi][:, 2 * SUBLANES + h:2 * SUBLANES + h + 1])
            hh = _sigmoid(vo_ref[rows[bi], hs2[h]].astype(F32)) * hh
            hn = hh * lax.rsqrt(jnp.mean(hh * hh, axis=-1, keepdims=True) + LN_EPS)
            out_ref[rows[bi], hs[h]] = (hn * gain_ref[:, hs[h]]).astype(BF16)
        return carry

    lax.fori_loop(0, NC, chunk, 0)


def _mlstm(oa, g3, u_tri, gain, *, B, S, L, nb):
    N = oa.shape[0]
    NC = S // L
    R = nb * S
    kern = functools.partial(_mlstm_kernel, L=L, NC=NC, S=S, nb=nb)
    return pl.pallas_call(
        kern,
        grid=(B // nb,),
        in_specs=[pl.BlockSpec((R, 2 * M_W), lambda b: (b, C_QK // (2 * M_W))),
                  pl.BlockSpec((R, 2 * M_W), lambda b: (b, C_VO // (2 * M_W))),
                  pl.BlockSpec((nb * NC, 2 * SUBLANES, L), lambda b: (b, 0, 0)),
                  pl.BlockSpec((L, L), lambda b: (0, 0)),
                  pl.BlockSpec((1, M_W), lambda b: (0, 0))],
        out_specs=pl.BlockSpec((R, M_W), lambda b: (b, 0)),
        out_shape=jax.ShapeDtypeStruct((N, M_W), BF16),
        scratch_shapes=[pltpu.VMEM((nb * M_HEADS, M_HD, 2 * M_HD), F32),
                        pltpu.VMEM((nb * NC, L, LANES), F32),
                        pltpu.VMEM((nb * NC, SUBLANES, L), F32),
                        pltpu.VMEM((nb * NC, SUBLANES, 2 * M_HD), F32)],
        compiler_params=_cparams(vmem_limit=VMEM_LIMIT_BIG),
        name="mlstm",
    )(oa, oa, g3, u_tri, gain)


_G_LEVELS = 6
_G_XROW = 2 * G_CHUNK + SUBLANES


def _gla_consts():
    L = G_CHUNK
    t = np.arange(L)
    blocks = [(t[None, :] <= t[:, None]).astype(np.float32),
              (t[None, :] > t[:, None]).astype(np.float32),
              np.ones((SUBLANES, L), np.float32)]
    masks = [np.eye(L, dtype=np.float32)]
    m = 1
    while m < L:
        wl = np.zeros((L, L), np.float32)
        for r in range(L):
            r0 = (r // (2 * m)) * 2 * m + m
            if r % (2 * m) >= m:
                wl[r, r0:r + 1] = 1.0
            else:
                wl[r, r + 1:r0] = 1.0
        blocks.append(wl)
        tt, ss = t[:, None], t[None, :]
        masks.append(((tt // (2 * m) == ss // (2 * m)) & (tt % (2 * m) >= m)
                      & (ss % (2 * m) < m)).astype(np.float32))
        m *= 2
    w = np.concatenate(blocks, axis=0)
    w3 = np.concatenate([w, w, w], axis=1)
    mk = np.stack([np.concatenate([x] * G_HEADS, axis=0) for x in masks])
    return w3, mk


def _gla_kernel(qk_ref, v_ref, gg_ref, la_ref, w3_ref, mk_ref, gain_ref, out_ref, st_ref, *, NC, S, nb):
    L = G_CHUNK
    st_ref[...] = jnp.zeros_like(st_ref)
    lane_head = lax.broadcasted_iota(jnp.int32, (L, G_KW), 1) // G_DK
    br = lax.broadcasted_iota(jnp.int32, (2 * G_DV, LANES), 0) < G_DV
    bl = lax.broadcasted_iota(jnp.int32, (2 * G_DV, LANES), 1) < G_DK
    bmask = br == bl
    nt = (((1,), (1,)), ((), ()))
    tn = (((0,), (0,)), ((), ()))

    def chunk(c, carry):
        rows = [pl.ds(pl.multiple_of(bi * S + c * L, L), L) for bi in range(nb)]
        X, q, k = [], [], []
        for bi in range(nb):
            la = la_ref[rows[bi], :]
            hi = la.astype(BF16)
            r1 = la - hi.astype(F32)
            mid = r1.astype(BF16)
            lo = (r1 - mid.astype(F32)).astype(BF16)
            stk = jnp.concatenate([hi, mid, lo], axis=0)
            X.append(jnp.exp(jnp.dot(w3_ref[...], stk, preferred_element_type=F32)))
            q.append(qk_ref[rows[bi], 0:G_KW].astype(F32))
            k.append(qk_ref[rows[bi], G_KW:2 * G_KW].astype(F32))

        sc = [[None] * (_G_LEVELS + 1) for _ in range(nb)]
        for lev in range(_G_LEVELS + 1):
            for bi in range(nb):
                if lev == 0:
                    qt, kt = q[bi], k[bi]
                else:
                    xl = X[bi][_G_XROW + L * (lev - 1):_G_XROW + L * lev, :]
                    qt, kt = q[bi] * xl, k[bi] * xl
                q4 = jnp.concatenate([jnp.where(lane_head == h, qt, 0.0) for h in range(G_HEADS)],
                                     axis=0).astype(BF16)
                sc[bi][lev] = lax.dot_general(q4, kt.astype(BF16), nt, preferred_element_type=F32)
        Ab = []
        for bi in range(nb):
            A = sc[bi][0] * mk_ref[0]
            for lev in range(1, _G_LEVELS + 1):
                A = A + sc[bi][lev] * mk_ref[lev]
            Ab.append(A.astype(BF16))

        for bi in range(nb):
            gg = gg_ref[rows[bi], :].astype(F32)
            gate = gg * _sigmoid(gg)
            for p in range(2):
                ls = slice(LANES * p, LANES * (p + 1))
                vp = v_ref[rows[bi], 2 * G_DV * p:2 * G_DV * (p + 1)]
                oi = [jnp.dot(Ab[bi][L * (2 * p + hh):L * (2 * p + hh + 1)],
                              vp[:, G_DV * hh:G_DV * (hh + 1)], preferred_element_type=F32)
                      for hh in range(2)]
                st = st_ref[bi, p]
                qc = (q[bi][:, ls] * X[bi][0:L, ls]).astype(BF16)
                o_inter = lax.dot_general(qc, st.astype(BF16), nt, preferred_element_type=F32)
                kc = (k[bi][:, ls] * X[bi][L:2 * L, ls]).astype(BF16)
                upd = lax.dot_general(vp, kc, tn, preferred_element_type=F32)
                dec = X[bi][2 * L:2 * L + 1, ls]
                st_ref[bi, p] = jnp.where(bmask, dec * st + upd, 0.0)
                for hh in range(2):
                    o = o_inter[:, G_DV * hh:G_DV * (hh + 1)] + oi[hh]
                    hn = o * lax.rsqrt(jnp.mean(o * o, axis=-1, keepdims=True) + LN_EPS)
                    hs = slice(G_DV * (2 * p + hh), G_DV * (2 * p + hh + 1))
                    out_ref[rows[bi], hs] = (hn * gain_ref[:, hs] * gate[:, hs]).astype(BF16)
        return carry

    lax.fori_loop(0, NC, chunk, 0)


def _gla(oa, la, w3, mk, gain, *, B, S, nb):
    N = oa.shape[0]
    NC = S // G_CHUNK
    kern = functools.partial(_gla_kernel, NC=NC, S=S, nb=nb)
    R = nb * S
    return pl.pallas_call(
        kern,
        grid=(B // nb,),
        in_specs=[pl.BlockSpec((R, 2 * G_KW), lambda b: (b, C_GQK // (2 * G_KW))),
                  pl.BlockSpec((R, G_W), lambda b: (b, C_GV // G_W)),
                  pl.BlockSpec((R, G_W), lambda b: (b, C_GG // G_W)),
                  pl.BlockSpec((R, G_KW), lambda b: (b, 0)),
                  pl.BlockSpec(w3.shape, lambda b: (0, 0)),
                  pl.BlockSpec(mk.shape, lambda b: (0, 0, 0)),
                  pl.BlockSpec((1, G_W), lambda b: (0, 0))],
        out_specs=pl.BlockSpec((R, G_W), lambda b: (b, 0)),
        out_shape=jax.ShapeDtypeStruct((N, G_W), BF16),
        scratch_shapes=[pltpu.VMEM((nb, 2, 2 * G_DV, LANES), F32)],
        compiler_params=_cparams(),
        name="gla",
    )(oa, oa, oa, la, w3, mk, gain)


def _layer_norm(z, g, b):
    mu = jnp.mean(z, axis=-1, keepdims=True)
    zc = z - mu
    var = jnp.mean(zc * zc, axis=-1, keepdims=True)
    return zc * lax.rsqrt(var + LN_EPS) * g + b


def _outproj_kernel(hm_ref, hg_ref, wf_ref, x_ref, mod_ref, g_ref, b_ref, wr_ref, br_ref,
                    x1_ref, u2_ref, rrow_ref, w_ref, *, tb, nh):
    @pl.when(pl.program_id(0) == 0)
    def _():
        w_ref[...] = wf_ref[...].astype(BF16)

    mod = mod_ref[0]
    blocks = [slice(tb * j, tb * (j + 1)) for j in range(nh)]
    y = [jnp.dot(hm_ref[r, :], w_ref[0:M_W, :], preferred_element_type=F32)
         + jnp.dot(hg_ref[r, :], w_ref[M_W:M_W + G_W, :], preferred_element_type=F32) for r in blocks]
    u2 = []
    for j, r in enumerate(blocks):
        z = ALPHA * x_ref[r, :] + (1.0 + mod[2:3, :]) * y[j]
        x1 = _layer_norm(z, g_ref[...], b_ref[...])
        x1_ref[r, :] = x1
        u2.append(x1 * (1.0 + mod[4:5, :]) + mod[3:4, :])
        u2_ref[r, :] = u2[j].astype(BF16)

    u2h = [u.astype(BF16) for u in u2]
    u2l = [(u2[j] - u2h[j].astype(F32)).astype(BF16) for j in range(nh)]
    lh = [jnp.dot(u, wr_ref[...], preferred_element_type=F32) for u in u2h]
    ll = [jnp.dot(u, wr_ref[:, 0:LANES], preferred_element_type=F32) for u in u2l]
    for j in range(nh):
        logits = lh[j][:, 0:LANES] + lh[j][:, LANES:2 * LANES] + ll[j] + br_ref[...]
        rrow_ref[j] = _route_select(logits.T, tb)


def _route_select(lt, tm):
    row = lax.broadcasted_iota(jnp.int32, (SUBLANES, tm), 0)
    gl = jnp.where(row < N_GROUPS, lt[0:SUBLANES, :], -jnp.inf)
    gmax = jnp.max(gl, axis=0, keepdims=True)
    gsel = jnp.min(jnp.where(gl == gmax, row, SUBLANES), axis=0, keepdims=True)
    pg = 1.0 / jnp.sum(jnp.exp(gl - gmax), axis=0, keepdims=True)
    ein = jnp.zeros((SUBLANES, tm), F32)
    for g in range(N_GROUPS):
        ein = jnp.where(gsel == g, lt[SUBLANES * (g + 1):SUBLANES * (g + 2), :], ein)
    v1 = jnp.max(ein, axis=0, keepdims=True)
    i1 = jnp.min(jnp.where(ein == v1, row, SUBLANES), axis=0, keepdims=True)
    rest = jnp.where(row == i1, -jnp.inf, ein)
    v2 = jnp.max(rest, axis=0, keepdims=True)
    i2 = jnp.min(jnp.where(rest == v2, row, SUBLANES), axis=0, keepdims=True)
    t2 = jnp.exp(v2 - v1)
    p1 = 1.0 / (1.0 + t2)
    e0 = (gsel * E_PER_G + i1).astype(F32)
    e1 = (gsel * E_PER_G + i2).astype(F32)
    return jnp.concatenate([e0, e1, pg * p1, pg * (t2 * p1), jnp.zeros((SUBLANES - 4, tm), F32)], axis=0)


def _outproj(hm, hg, w_out, x2, mod3, g, b, wr, br, *, S, tb, nh):
    N, D = x2.shape
    tm = tb * nh
    tpb = S // tm
    kern = functools.partial(_outproj_kernel, tb=tb, nh=nh)
    return pl.pallas_call(
        kern,
        grid=(N // tm,),
        in_specs=[pl.BlockSpec((tm, M_W), lambda i: (i, 0)),
                  pl.BlockSpec((tm, G_W), lambda i: (i, 0)),
                  pl.BlockSpec((M_W + G_W, D), lambda i: (0, 0), pipeline_mode=pl.Buffered(1)),
                  pl.BlockSpec((tm, D), lambda i: (i, 0)),
                  pl.BlockSpec((1, 6, D), lambda i: (i // tpb, 0, 0)),
                  pl.BlockSpec((1, D), lambda i: (0, 0)),
                  pl.BlockSpec((1, D), lambda i: (0, 0)),
                  pl.BlockSpec((D, 2 * LANES), lambda i: (0, 0)),
                  pl.BlockSpec((1, LANES), lambda i: (0, 0))],
        out_specs=[pl.BlockSpec((tm, D), lambda i: (i, 0)),
                   pl.BlockSpec((tm, D), lambda i: (i, 0)),
                   pl.BlockSpec((nh, SUBLANES, tb), lambda i: (i, 0, 0))],
        out_shape=[jax.ShapeDtypeStruct((N, D), F32),
                   jax.ShapeDtypeStruct((N, D), BF16),
                   jax.ShapeDtypeStruct((N // tb, SUBLANES, tb), F32)],
        scratch_shapes=[pltpu.VMEM((M_W + G_W, D), BF16)],
        compiler_params=_cparams(),
        name="outproj",
    )(hm, hg, w_out, x2, mod3, g, b, wr, br)


def _slots_per_tile(tb):
    worst = 2 * tb + N_EXP * (GRAN - 1)
    return -(-worst // LANES) * LANES


def _ffn_tiles(n_tok, tb):
    worst_rows = 2 * n_tok + (n_tok // tb) * N_EXP * (GRAN - 1)
    return -(-worst_rows // FFN_TM) + N_EXP


def _route_kernel(rr_ref, u_ref, lt_ref, srow_ref, col_ref, gd_ref, meta_ref, mg_ref, part_ref,
                  *, NT, tb, TM):
    iota_e = lax.broadcasted_iota(jnp.int32, (N_EXP, tb), 0).astype(F32)
    glane = lax.broadcasted_iota(jnp.int32, (N_EXP, LANES), 1).astype(F32)
    ltri = lt_ref[...]

    def prefix_e(col):
        return jnp.dot(ltri, jnp.broadcast_to(col, (N_EXP, LANES)),
                       preferred_element_type=F32, precision=HIGHEST)[:, 0:1]

    def p1(j, run8):
        r = rr_ref[j]
        oh0 = jnp.where(iota_e == r[0:1, :], 1.0, 0.0)
        oh1 = jnp.where(iota_e == r[1:2, :], 1.0, 0.0)
        cum0 = jnp.dot(oh0.astype(BF16), u_ref[...], preferred_element_type=F32)
        cum1 = jnp.dot(oh1.astype(BF16), u_ref[...], preferred_element_type=F32)
        c0 = jnp.sum(oh0, axis=1, keepdims=True)
        n8 = jnp.floor((c0 + jnp.sum(oh1, axis=1, keepdims=True) + (GRAN - 1.0)) * (1.0 / GRAN))
        lo8 = prefix_e(n8)
        s0 = jnp.sum(oh0 * (GRAN * lo8 + cum0 - 1.0), axis=0, keepdims=True)
        s1 = jnp.sum(oh1 * (GRAN * lo8 + c0 + cum1 - 1.0), axis=0, keepdims=True)
        info = jnp.concatenate([s0, s1, r[2:4, :], jnp.zeros((SUBLANES - 4, tb), F32)], axis=0)
        srow_ref[j] = info
        col_ref[pl.ds(pl.multiple_of(j * tb, tb), tb), :] = jnp.concatenate(
            [info, jnp.zeros((LANES - SUBLANES, tb), F32)], axis=0).T
        mg = jnp.where((lo8 <= glane) & (glane < lo8 + n8), 1.0, 0.0)
        mg_ref[j] = mg
        part = jnp.sum(mg * (run8 + glane - lo8), axis=0, keepdims=True)
        gcnt = jnp.broadcast_to(jnp.sum(n8, axis=0, keepdims=True), (1, LANES))
        part_ref[j] = jnp.concatenate([part, gcnt, jnp.zeros((SUBLANES - 2, LANES), F32)], axis=0)
        return run8 + n8

    tot8 = lax.fori_loop(0, NT, p1, jnp.zeros((N_EXP, 1), F32), unroll=4 if NT % 4 == 0 else 1)
    seg_t = jnp.floor((tot8 * GRAN + (TM - 1.0)) * (1.0 / TM))
    base_t = prefix_e(seg_t)
    base8 = base_t * (TM // GRAN)
    lane1 = lax.broadcasted_iota(jnp.int32, (1, LANES), 1)

    def p2(j, carry):
        pr = part_ref[j]
        dst = (pr[0:1, :] + jnp.sum(mg_ref[j] * base8, axis=0, keepdims=True)) * GRAN
        gd_ref[j] = jnp.where(lane1 == G_LAST, pr[1:2, :], dst).astype(jnp.int32)
        return carry

    lax.fori_loop(0, NT, p2, 0, unroll=4 if NT % 4 == 0 else 1)
    eye = jnp.where(glane == lax.broadcasted_iota(jnp.int32, (N_EXP, LANES), 0).astype(F32), 1.0, 0.0)
    tail_row = jnp.sum(eye * ((base8 + tot8) * GRAN), axis=0, keepdims=True)
    tail_n8 = jnp.sum(eye * (seg_t * (TM // GRAN) - tot8), axis=0, keepdims=True)
    nv_l = jnp.broadcast_to(jnp.sum(seg_t, axis=0, keepdims=True), (1, LANES))
    gd_ref[NT] = jnp.where(lane1 == G_LAST, nv_l, tail_row).astype(jnp.int32)
    gd_ref[NT + 1] = tail_n8.astype(jnp.int32)
    ti = lax.broadcasted_iota(jnp.int32, (N_EXP, tb), 1).astype(F32)
    te = jnp.sum(jnp.where(base_t <= ti, 1.0, 0.0), axis=0, keepdims=True) - 1.0
    nv = jnp.broadcast_to(jnp.sum(seg_t, axis=0, keepdims=True), (1, tb))
    meta_ref[...] = jnp.concatenate([te, nv, jnp.zeros((SUBLANES - 2, tb), F32)],
                                    axis=0).astype(jnp.int32)


def _route(rrow, u_cnt, ltri, *, TM):
    NT, _, tb = rrow.shape
    kern = functools.partial(_route_kernel, NT=NT, tb=tb, TM=TM)
    full3 = lambda i: (0, 0, 0)
    return pl.pallas_call(
        kern,
        grid=(1,),
        in_specs=[pl.BlockSpec((NT, SUBLANES, tb), full3),
                  pl.BlockSpec((tb, tb), lambda i: (0, 0)),
                  pl.BlockSpec((N_EXP, N_EXP), lambda i: (0, 0))],
        out_specs=[pl.BlockSpec((NT, SUBLANES, tb), full3),
                   pl.BlockSpec((NT * tb, LANES), lambda i: (0, 0)),
                   pl.BlockSpec((NT + 2, 1, LANES), full3),
                   pl.BlockSpec((SUBLANES, tb), lambda i: (0, 0))],
        out_shape=[jax.ShapeDtypeStruct((NT, SUBLANES, tb), F32),
                   jax.ShapeDtypeStruct((NT * tb, LANES), F32),
                   jax.ShapeDtypeStruct((NT + 2, 1, LANES), jnp.int32),
                   jax.ShapeDtypeStruct((SUBLANES, tb), jnp.int32)],
        scratch_shapes=[pltpu.VMEM((NT, N_EXP, LANES), F32), pltpu.VMEM((NT, SUBLANES, LANES), F32)],
        compiler_params=_cparams(),
        name="route",
    )(rrow, u_cnt, ltri)


U32 = jnp.uint32
_HI_MASK = 0xFFFF0000


def _pack_halves(x):
    c = x.shape[1] // 2
    lo = lax.bitcast_convert_type(x[:, :c], U32)
    hi = lax.bitcast_convert_type(x[:, c:], U32)
    return (lo >> 16) | (hi & U32(_HI_MASK))


def _unpack_halves(w):
    lo = lax.bitcast_convert_type(w << 16, F32)
    hi = lax.bitcast_convert_type(w & U32(_HI_MASK), F32)
    return jnp.concatenate([lo, hi], axis=1).astype(BF16)


def _granule_copy(src_ref, src_row, dst_ref, dst_row, sem):
    return pltpu.make_async_copy(src_ref.at[pl.ds(src_row, GRAN), :], dst_ref.at[pl.ds(dst_row, GRAN), :], sem)


def _wait_granules(n, src_ref, dst_ref, sem, n_max):
    b = 1
    while b <= n_max:
        @pl.when((n & b) != 0)
        def _(b=b):
            pltpu.make_async_copy(src_ref.at[pl.ds(0, b * GRAN), :], dst_ref.at[pl.ds(0, b * GRAN), :],
                                  sem).wait()
        b *= 2


def _dispatch_kernel(gd_ref, srow_ref, u_ref, xs_ref, buf, zbuf, sems, *, NT, SL, TM, n_tiles):
    j = pl.program_id(0)
    slot = j % 2
    zsem = sems.at[2]

    def drain(tile, sl):
        _wait_granules(gd_ref[tile, G_LAST], buf.at[sl], xs_ref, sems.at[sl], SL // GRAN)

    def tile_fill(t):
        return pltpu.make_async_copy(zbuf, xs_ref.at[pl.ds(pl.multiple_of(t * TM, TM), TM), :], zsem)

    def zero_fill(wait):
        for e in range(N_EXP):
            def zg(g, carry, e=e):
                cp = _granule_copy(zbuf, 0, xs_ref, pl.multiple_of(gd_ref[NT, e] + g * GRAN, GRAN), zsem)
                cp.wait() if wait else cp.start()
                return carry
            lax.fori_loop(0, gd_ref[NT + 1, e], zg, 0)

        def zt(t, carry):
            tile_fill(t).wait() if wait else tile_fill(t).start()
            return carry
        lax.fori_loop(gd_ref[NT, G_LAST], n_tiles, zt, 0)

    @pl.when(j == 0)
    def _():
        zbuf[...] = jnp.zeros_like(zbuf)
        zero_fill(False)

    @pl.when(j >= 2)
    def _():
        drain(j - 2, slot)

    s = srow_ref[0]
    rows = lax.broadcasted_iota(jnp.int32, (SL, s.shape[1]), 0).astype(F32)
    m0 = rows == s[0:1, :]
    m1 = rows == s[1:2, :]
    oh = jnp.where(m0 | m1, 1.0, 0.0).astype(BF16)
    dw = u_ref.shape[1] // 2
    buf[slot, :, 0:dw] = _pack_halves(jnp.dot(oh, u_ref[...], preferred_element_type=F32))
    wrow = jnp.sum(jnp.where(m0, s[2:3, :], 0.0) + jnp.where(m1, s[3:4, :], 0.0), axis=1, keepdims=True)
    buf[slot, :, dw:dw + LANES] = lax.bitcast_convert_type(jnp.broadcast_to(wrow, (SL, LANES)), U32)

    def issue(g, carry):
        _granule_copy(buf.at[slot], pl.multiple_of(g * GRAN, GRAN), xs_ref,
                      pl.multiple_of(gd_ref[j, g], GRAN), sems.at[slot]).start()
        return carry

    lax.fori_loop(0, gd_ref[j, G_LAST], issue, 0)

    @pl.when(j == NT - 1)
    def _():
        drain(j, slot)
        if NT > 1:
            drain(j - 1, 1 - slot)
        zero_fill(True)


def _dispatch(gd, srow, u2, *, n_tiles, TM):
    N, D = u2.shape
    NT, _, tb = srow.shape
    SL = _slots_per_tile(tb)
    n_rows = n_tiles * TM
    kern = functools.partial(_dispatch_kernel, NT=NT, SL=SL, TM=TM, n_tiles=n_tiles)
    grid_spec = pltpu.PrefetchScalarGridSpec(
        num_scalar_prefetch=1,
        grid=(NT,),
        in_specs=[pl.BlockSpec((1, SUBLANES, tb), lambda j, gd: (j, 0, 0)),
                  pl.BlockSpec((tb, D), lambda j, gd: (j, 0))],
        out_specs=pl.BlockSpec(memory_space=pl.ANY),
        scratch_shapes=[pltpu.VMEM((2, SL, D // 2 + LANES), U32), pltpu.VMEM((TM, D // 2 + LANES), U32),
                        pltpu.SemaphoreType.DMA((3,))],
    )
    return pl.pallas_call(
        kern,
        grid_spec=grid_spec,
        out_shape=jax.ShapeDtypeStruct((n_rows, D // 2 + LANES), U32),
        compiler_params=_cparams(),
        name="dispatch",
    )(gd, srow, u2)


def _ffn_kernel(te_ref, nv_ref, xs_ref, wg_ref, wu_ref, wd_ref, o_ref, wgb, wub, wdb, sg, su, sd, slot_ref,
                sems):
    i = pl.program_id(0)
    nv = nv_ref[0]
    e = te_ref[i]

    def weight_copies(ex, sl):
        return (pltpu.make_async_copy(wg_ref.at[ex], sg.at[sl], sems.at[sl]),
                pltpu.make_async_copy(wu_ref.at[ex], su.at[sl], sems.at[sl]),
                pltpu.make_async_copy(wd_ref.at[ex], sd.at[sl], sems.at[sl]))

    @pl.when(i == 0)
    def _():
        slot_ref[0] = 0
        for cp in weight_copies(e, 0):
            cp.start()

    @pl.when((i < nv) & ((i == 0) | (e != te_ref[jnp.maximum(i - 1, 0)])))
    def _():
        sl = slot_ref[0]
        for cp in weight_copies(e, sl):
            cp.wait()
        wgb[...] = sg[sl].astype(BF16)
        wub[...] = su[sl].astype(BF16)
        wdb[...] = sd[sl].astype(BF16)
        nxt = lax.while_loop(lambda t: (t < nv) & (te_ref[jnp.minimum(t, nv - 1)] == e), lambda t: t + 1, i + 1)

        @pl.when(nxt < nv)
        def _():
            for cp in weight_copies(te_ref[nxt], 1 - sl):
                cp.start()
        slot_ref[0] = 1 - sl

    @pl.when(i < nv)
    def _():
        nsub = FFN_SUB
        hm = xs_ref.shape[0] // nsub
        dw = o_ref.shape[1]
        halves = tuple(slice(hm * j, hm * (j + 1)) for j in range(nsub))
        x = [_unpack_halves(xs_ref[r, 0:dw]) for r in halves]
        g = [jnp.dot(x[j], wgb[...], preferred_element_type=F32) for j in range(nsub)]
        u = [jnp.dot(x[j], wub[...], preferred_element_type=F32) for j in range(nsub)]
        h = [(g[j] * _sigmoid(g[j]) * u[j]).astype(BF16) for j in range(nsub)]
        y = [jnp.dot(h[j], wdb[...], preferred_element_type=F32) for j in range(nsub)]
        for j in range(nsub):
            wt = lax.bitcast_convert_type(xs_ref[halves[j], dw:dw + LANES], F32)
            yw = y[j] * jnp.concatenate([wt] * (2 * dw // LANES), axis=1)
            o_ref[halves[j], :] = _pack_halves(yw.astype(BF16).astype(F32))

    @pl.when(i >= nv_ref[0])
    def _():
        o_ref[...] = jnp.zeros_like(o_ref)


def _ffn(te, nv, xs, wg, wu, wd, *, TM):
    P, XW = xs.shape
    DW = XW - LANES
    D = 2 * DW
    n_tiles = P // TM
    grid_spec = pltpu.PrefetchScalarGridSpec(
        num_scalar_prefetch=2,
        grid=(n_tiles,),
        in_specs=[pl.BlockSpec((TM, XW), lambda i, te, nv: (jnp.maximum(jnp.minimum(i, nv[0] - 1), 0), 0)),
                  pl.BlockSpec(memory_space=pl.ANY),
                  pl.BlockSpec(memory_space=pl.ANY),
                  pl.BlockSpec(memory_space=pl.ANY)],
        out_specs=pl.BlockSpec((TM, DW), lambda i, te, nv: (i, 0)),
        scratch_shapes=[pltpu.VMEM((D, D_EXP), BF16), pltpu.VMEM((D, D_EXP), BF16),
                        pltpu.VMEM((D_EXP, D), BF16),
                        pltpu.VMEM((2, D, D_EXP), F32), pltpu.VMEM((2, D, D_EXP), F32),
                        pltpu.VMEM((2, D_EXP, D), F32), pltpu.SMEM((1,), jnp.int32),
                        pltpu.SemaphoreType.DMA((2,))],
    )
    return pl.pallas_call(
        _ffn_kernel,
        grid_spec=grid_spec,
        out_shape=jax.ShapeDtypeStruct((P, DW), U32),
        compiler_params=_cparams(),
        name="ffn",
    )(te, nv, xs, wg, wu, wd)


def _combine_kernel(gd_ref, ys_ref, col_ref, x1_ref, mod_ref, g_ref, b_ref, o_ref, buf, sems, *, NT, SL):
    j = pl.program_id(0)
    slot = j % 2

    def fetch(tile, sl):
        def f(g, carry):
            _granule_copy(ys_ref, pl.multiple_of(gd_ref[tile, g], GRAN), buf.at[sl],
                          pl.multiple_of(g * GRAN, GRAN), sems.at[sl]).start()
            return carry
        lax.fori_loop(0, gd_ref[tile, G_LAST], f, 0)

    @pl.when(j == 0)
    def _():
        fetch(0, 0)

    @pl.when(j + 1 < NT)
    def _():
        fetch(j + 1, 1 - slot)

    ng = gd_ref[j, G_LAST]

    _wait_granules(ng, ys_ref, buf.at[slot], sems.at[slot], SL // GRAN)

    rows = lax.broadcasted_iota(jnp.int32, (SL, 1), 0)
    yb = _unpack_halves(jnp.where(rows < ng * GRAN, buf[slot], U32(0)))
    col = col_ref[...]
    tb = col.shape[0]
    lanes = lax.broadcasted_iota(jnp.int32, (tb, SL), 1).astype(F32)
    sel = jnp.where((lanes == col[:, 0:1]) | (lanes == col[:, 1:2]), 1.0, 0.0).astype(BF16)
    y = jnp.dot(sel, yb, preferred_element_type=F32)
    mod = mod_ref[0]
    z = ALPHA * x1_ref[...] + (1.0 + mod[5:6, :]) * y
    o_ref[...] = _layer_norm(z, g_ref[...], b_ref[...])


def _combine(gd, ys, col, x1, mod3, g, b, *, S, tb):
    N, D = x1.shape
    NT = N // tb
    tpb = S // tb
    SL = _slots_per_tile(tb)
    kern = functools.partial(_combine_kernel, NT=NT, SL=SL)
    grid_spec = pltpu.PrefetchScalarGridSpec(
        num_scalar_prefetch=1,
        grid=(NT,),
        in_specs=[pl.BlockSpec(memory_space=pl.ANY),
                  pl.BlockSpec((tb, LANES), lambda j, gd: (j, 0)),
                  pl.BlockSpec((tb, D), lambda j, gd: (j, 0)),
                  pl.BlockSpec((1, 6, D), lambda j, gd: (j // tpb, 0, 0)),
                  pl.BlockSpec((1, D), lambda j, gd: (0, 0)),
                  pl.BlockSpec((1, D), lambda j, gd: (0, 0))],
        out_specs=pl.BlockSpec((tb, D), lambda j, gd: (j, 0)),
        scratch_shapes=[pltpu.VMEM((2, SL, D // 2), U32), pltpu.SemaphoreType.DMA((2,))],
    )
    return pl.pallas_call(
        kern,
        grid_spec=grid_spec,
        out_shape=jax.ShapeDtypeStruct((N, D), F32),
        compiler_params=_cparams(),
        name="combine",
    )(gd, ys, col, x1, mod3, g, b)


def _layer(x, c, l, w_ada, b_ada, w_in, w_conv, b_conv, b_igate, b_fgate, mlstm_norm_g, w_gla_a, b_gla_a,
           gla_norm_g, w_out, ln1_g, ln1_b, w_route_group, b_route_group, w_route_expert, b_route_expert,
           w_gate, w_up, w_down, ln2_g, ln2_b):
    B, S, D = x.shape
    N = B * S
    x2 = x.reshape(N, D)
    tm_in = min(512, S)
    tm = min(256, S)
    lm = min(256, S)

    mod3 = _ada(c, w_ada[l], b_ada[l]).reshape(B, 6, D)

    wa_pad = jnp.zeros((LANES, G_KW), F32).at[SM_A:SM_A + G_RANK].set(w_gla_a[l]).astype(BF16)
    bg = (jnp.zeros((2 * SUBLANES, 1), F32).at[0:M_HEADS, 0].set(b_igate[l])
          .at[SUBLANES:SUBLANES + M_HEADS, 0].set(b_fgate[l]))
    oa, la, g3 = _inproj(x2, mod3, w_in[l], w_conv[l], b_conv[l].reshape(1, -1), wa_pad,
                         b_gla_a[l].reshape(1, -1), bg, S=S, tm=tm_in, lm=lm)

    u_tri = jnp.asarray(np.triu(np.ones((lm, lm), np.float32)))
    nb = 2 if B % 2 == 0 else 1
    hm = _mlstm(oa, g3, u_tri, mlstm_norm_g[l].reshape(1, -1), B=B, S=S, L=lm, nb=nb)
    w3_np, mk_np = _gla_consts()
    hg = _gla(oa, la, jnp.asarray(w3_np, BF16), jnp.asarray(mk_np), gla_norm_g[l].reshape(1, -1), B=B, S=S,
              nb=nb)

    wr = (jnp.zeros((D, LANES), F32).at[:, 0:N_GROUPS].set(w_route_group[l])
          .at[:, SUBLANES:SUBLANES + N_EXP].set(w_route_expert[l]))
    br = (jnp.zeros((1, LANES), F32).at[0, 0:N_GROUPS].set(b_route_group[l])
          .at[0, SUBLANES:SUBLANES + N_EXP].set(b_route_expert[l]))
    wr_hi = wr.astype(BF16)
    wr2 = jnp.concatenate([wr_hi, (wr - wr_hi.astype(F32)).astype(BF16)], axis=1)
    x1, u2, rrow = _outproj(hm, hg, w_out[l], x2, mod3, ln1_g[l].reshape(1, -1),
                            ln1_b[l].reshape(1, -1), wr2, br, S=S, tb=tm, nh=2 if S % (2 * tm) == 0 else 1)

    u_cnt = jnp.asarray(np.triu(np.ones((tm, tm), np.float32)), BF16)
    ltri = jnp.asarray(np.tril(np.ones((N_EXP, N_EXP), np.float32), -1))
    srow, col, gd3, meta = _route(rrow, u_cnt, ltri, TM=FFN_TM)
    gd = gd3.reshape(N // tm + 2, LANES)
    n_tiles = _ffn_tiles(N, tm)
    te, nv = meta[0, :n_tiles], meta[1, 0:1]

    xs = _dispatch(gd, srow, u2, n_tiles=n_tiles, TM=FFN_TM)
    ys = _ffn(te, nv, xs, w_gate[l], w_up[l], w_down[l], TM=FFN_TM)
    out = _combine(gd, ys, col, x1, mod3, ln2_g[l].reshape(1, -1), ln2_b[l].reshape(1, -1), S=S, tb=tm)
    return out.reshape(B, S, D)


def kernel(x, c, w_ada, b_ada, w_in, w_conv, b_conv, b_igate, b_fgate, mlstm_norm_g, w_gla_a, b_gla_a,
           gla_norm_g, w_out, ln1_g, ln1_b, w_route_group, b_route_group, w_route_expert, b_route_expert,
           w_gate, w_up, w_down, ln2_g, ln2_b):
    for l in range(DEPTH):
        x = _layer(x, c, l, w_ada, b_ada, w_in, w_conv, b_conv, b_igate, b_fgate, mlstm_norm_g, w_gla_a,
                   b_gla_a, gla_norm_g, w_out, ln1_g, ln1_b, w_route_group, b_route_group, w_route_expert,
                   b_route_expert, w_gate, w_up, w_down, ln2_g, ln2_b)
    return x
```

```python
import functools

import numpy as np
import jax
import jax.numpy as jnp
from jax import lax
from jax.experimental import pallas as pl
from jax.experimental.pallas import tpu as pltpu

F32 = jnp.float32
BF16 = jnp.bfloat16
HIGHEST = lax.Precision.HIGHEST

DEPTH = 1
M_HEADS = 4
M_HD = 128
M_W = M_HEADS * M_HD
CONV_W = 4
G_HEADS = 4
G_DK = 64
G_DV = 128
G_W = G_HEADS * G_DV
G_KW = G_HEADS * G_DK
G_RANK = 16
G_TAU = 16.0
G_CHUNK = 64
N_GROUPS = 4
E_PER_G = 8
N_EXP = N_GROUPS * E_PER_G
D_EXP = 512
ALPHA = (2 * DEPTH) ** 0.25
LN_EPS = 1e-5

LANES = 128
SUBLANES = 8
VMEM_LIMIT = 48 * 1024 * 1024
VMEM_LIMIT_BIG = 56 * 1024 * 1024

C_QK = 0
C_VO = 1024
C_GQK = 2048
C_GV = 2560
C_GG = 3072
C_SMALL = 3584
C_TOT = 3712
SM_I, SM_F, SM_A = 0, 8, 16
IN_GATES = 4 * M_W
IN_G = IN_GATES + 2 * M_HEADS
IN_GA = IN_G + 2 * G_KW + 2 * G_W
IN_TOT = IN_GA + G_RANK

FFN_TM = 512
FFN_SUB = 2
GRAN = SUBLANES
G_LAST = LANES - 1


def _cparams(n_axes=1, vmem_limit=VMEM_LIMIT):
    return pltpu.CompilerParams(dimension_semantics=("arbitrary",) * n_axes,
                                vmem_limit_bytes=vmem_limit)


def _sigmoid(x):
    return 1.0 / (1.0 + jnp.exp(-x))


def _log_sigmoid(x):
    return jnp.minimum(x, 0.0) - jnp.log(1.0 + jnp.exp(-jnp.abs(x)))


def _ada_kernel(c_ref, w_ref, b_ref, o_ref):
    c = c_ref[...]
    ca = c * _sigmoid(c)
    o_ref[...] = jnp.dot(ca, w_ref[...], preferred_element_type=F32, precision=HIGHEST) + b_ref[...]


def _ada(c, w, b):
    B, D = c.shape
    n_out = w.shape[1]
    tn = 1024
    return pl.pallas_call(
        _ada_kernel,
        grid=(n_out // tn,),
        in_specs=[pl.BlockSpec((B, D), lambda j: (0, 0)),
                  pl.BlockSpec((D, tn), lambda j: (0, j)),
                  pl.BlockSpec((1, tn), lambda j: (0, j))],
        out_specs=pl.BlockSpec((B, tn), lambda j: (0, j)),
        out_shape=jax.ShapeDtypeStruct((B, n_out), F32),
        compiler_params=_cparams(),
        name="ada",
    )(c, w, b.reshape(1, n_out))


def _inproj_kernel(x_ref, mod_ref, win_ref, wc_ref, bc_ref, wa_ref, ba_ref, bg_ref,
                   oa_ref, la_ref, g_ref, halo_ref, w_ref, *, tm, tpb, lm):
    i = pl.program_id(0)

    @pl.when(i == 0)
    def _():
        rc = LANES
        for r in range(0, win_ref.shape[0], rc):
            rs = slice(r, r + rc)
            w_ref[rs, 0:IN_GATES] = win_ref[rs, 0:IN_GATES].astype(BF16)
            t = win_ref[rs, IN_GATES:IN_TOT]
            w_ref[rs, C_GQK:C_SMALL] = t[:, IN_G - IN_GATES:IN_GA - IN_GATES].astype(BF16)
            z = lambda n: jnp.zeros((rc, n), F32)
            small = jnp.concatenate([t[:, 0:M_HEADS], z(SM_F - M_HEADS), t[:, M_HEADS:2 * M_HEADS],
                                     z(SM_A - SM_F - M_HEADS), t[:, IN_GA - IN_GATES:IN_TOT - IN_GATES],
                                     z(LANES - SM_A - G_RANK)], axis=1)
            w_ref[rs, C_SMALL:C_TOT] = small.astype(BF16)

    @pl.when(i % tpb == 0)
    def _():
        halo_ref[...] = jnp.zeros_like(halo_ref)

    mod = mod_ref[0]
    u = (x_ref[...] * (1.0 + mod[1:2, :]) + mod[0:1, :]).astype(BF16)

    p = jnp.dot(u, w_ref[:, C_QK:C_QK + 2 * M_W], preferred_element_type=F32)
    ext = jnp.concatenate([halo_ref[...], p], axis=0)
    acc = bc_ref[...] + wc_ref[CONV_W - 1:CONV_W, :] * p
    for j in range(CONV_W - 1):
        sh = pltpu.roll(ext, CONV_W - 1 - j, 0)[SUBLANES:, :]
        acc = acc + wc_ref[j:j + 1, :] * sh
    halo_ref[...] = p[tm - SUBLANES:, :]
    qk = acc * _sigmoid(acc)
    oa_ref[:, C_QK:C_QK + M_W] = qk[:, :M_W].astype(BF16)
    oa_ref[:, C_QK + M_W:C_QK + 2 * M_W] = (qk[:, M_W:] * (M_HD ** -0.5)).astype(BF16)

    p = jnp.dot(u, w_ref[:, C_VO:C_VO + 2 * M_W], preferred_element_type=F32)
    oa_ref[:, C_VO:C_VO + 2 * M_W] = p.astype(BF16)

    p = jnp.dot(u, w_ref[:, C_GQK:C_GQK + G_KW], preferred_element_type=F32)
    oa_ref[:, C_GQK:C_GQK + G_KW] = (p * (G_DK ** -0.5)).astype(BF16)
    p = jnp.dot(u, w_ref[:, C_GQK + G_KW:C_SMALL], preferred_element_type=F32)
    oa_ref[:, C_GQK + G_KW:C_SMALL] = p.astype(BF16)

    ps = jnp.dot(u, w_ref[:, C_SMALL:C_TOT], preferred_element_type=F32)
    la = jnp.dot(ps.astype(BF16), wa_ref[...], preferred_element_type=F32) + ba_ref[...]
    la_ref[...] = _log_sigmoid(la) * (1.0 / G_TAU)
    pt = ps.T
    gi = pt[SM_I:SM_I + SUBLANES, :] + bg_ref[0:SUBLANES, :]
    gf = _log_sigmoid(pt[SM_F:SM_F + SUBLANES, :] + bg_ref[SUBLANES:2 * SUBLANES, :])
    for j in range(tm // lm):
        g_ref[j, 0:SUBLANES, :] = gi[:, j * lm:(j + 1) * lm]
        g_ref[j, SUBLANES:2 * SUBLANES, :] = gf[:, j * lm:(j + 1) * lm]


def _inproj(x2, mod3, w_in, w_conv, b_conv, wa_pad, b_gla, bg, *, S, tm, lm):
    N, D = x2.shape
    tpb = S // tm
    kern = functools.partial(_inproj_kernel, tm=tm, tpb=tpb, lm=lm)
    return pl.pallas_call(
        kern,
        grid=(N // tm,),
        in_specs=[pl.BlockSpec((tm, D), lambda i: (i, 0)),
                  pl.BlockSpec((1, 6, D), lambda i: (i // tpb, 0, 0)),
                  pl.BlockSpec((D, IN_TOT), lambda i: (0, 0), pipeline_mode=pl.Buffered(1)),
                  pl.BlockSpec((CONV_W, 2 * M_W), lambda i: (0, 0)),
                  pl.BlockSpec((1, 2 * M_W), lambda i: (0, 0)),
                  pl.BlockSpec((LANES, G_KW), lambda i: (0, 0)),
                  pl.BlockSpec((1, G_KW), lambda i: (0, 0)),
                  pl.BlockSpec((2 * SUBLANES, 1), lambda i: (0, 0))],
        out_specs=[pl.BlockSpec((tm, C_SMALL), lambda i: (i, 0)),
                   pl.BlockSpec((tm, G_KW), lambda i: (i, 0)),
                   pl.BlockSpec((tm // lm, 2 * SUBLANES, lm), lambda i: (i, 0, 0))],
        out_shape=[jax.ShapeDtypeStruct((N, C_SMALL), BF16),
                   jax.ShapeDtypeStruct((N, G_KW), F32),
                   jax.ShapeDtypeStruct((N // lm, 2 * SUBLANES, lm), F32)],
        scratch_shapes=[pltpu.VMEM((SUBLANES, 2 * M_W), F32), pltpu.VMEM((D, C_TOT), BF16)],
        compiler_params=_cparams(),
        name="inproj",
    )(x2, mod3, w_in, w_conv, b_conv, wa_pad, b_gla, bg)


def _mlstm_kernel(qk_ref, vo_ref, g_ref, u_ref, gain_ref, out_ref, c_ref, zt_ref, a_ref, dec_ref,
                  *, L, NC, S, nb):
    c_ref[...] = jnp.zeros_like(c_ref)
    lane = lax.broadcasted_iota(jnp.int32, (SUBLANES, L), 1)
    tril = (lax.broadcasted_iota(jnp.int32, (L, L), 0) >= lax.broadcasted_iota(jnp.int32, (L, L), 1))
    ones_v = jnp.ones((L, M_HD), BF16)
    zpad = jnp.zeros((LANES - 4 * SUBLANES, L), F32)

    for bi in range(nb):
        bs, gs = [], []
        for c in range(NC):
            ci = bi * NC + c
            b = jnp.dot(g_ref[ci, SUBLANES:2 * SUBLANES, :], u_ref[...], preferred_element_type=F32,
                        precision=HIGHEST)
            a = g_ref[ci, 0:SUBLANES, :] - b
            a_ref[ci] = a
            G = a
            s = 1
            while s < L:
                G = jnp.maximum(G, jnp.where(lane >= s, pltpu.roll(G, s, 1), -jnp.inf))
                s *= 2
            bs.append(b)
            gs.append(G)
        m_prev = jnp.zeros((SUBLANES, 1), F32)
        for c in range(NC):
            ci = bi * NC + c
            M = jnp.maximum(gs[c], m_prev)
            ML = M[:, L - 1:L]
            Z = jnp.concatenate([M, jnp.exp(m_prev - M), jnp.exp(-(bs[c] + M)), jnp.exp(a_ref[ci] - ML), zpad],
                                axis=0)
            zt_ref[ci] = Z.T
            dec_ref[ci] = jnp.broadcast_to(jnp.exp(m_prev - ML), (SUBLANES, 2 * M_HD))
            m_prev = bs[c][:, L - 1:L] + ML

    chains = [(bi, h) for bi in range(nb) for h in range(M_HEADS)]
    nt = (((1,), (1,)), ((), ()))
    tn = (((0,), (0,)), ((), ()))

    def chunk(c, carry):
        rows = [pl.ds(pl.multiple_of(bi * S + c * L, L), L) for bi in range(nb)]
        Zt = [zt_ref[bi * NC + c] for bi in range(nb)]
        a = [a_ref[bi * NC + c] for bi in range(nb)]
        dec = [dec_ref[bi * NC + c] for bi in range(nb)]
        hs = [slice(h * M_HD, (h + 1) * M_HD) for h in range(M_HEADS)]
        hs2 = [slice(M_W + h * M_HD, M_W + (h + 1) * M_HD) for h in range(M_HEADS)]
        q = [qk_ref[rows[bi], hs[h]] for bi, h in chains]
        k = [qk_ref[rows[bi], hs2[h]] for bi, h in chains]
        vext = [jnp.concatenate([vo_ref[rows[bi], hs[h]], ones_v], axis=1) for bi, h in chains]
        cst = [c_ref[bi * M_HEADS + h] for bi, h in chains]
        n = range(len(chains))
        sc = [lax.dot_general(q[i], k[i], nt, preferred_element_type=F32) for i in n]
        qc = [jnp.dot(q[i], cst[i].astype(BF16), preferred_element_type=F32) for i in n]
        pm = [(sc[i] * jnp.exp(jnp.where(tril, a[bi][h:h + 1, :] - Zt[bi][:, h:h + 1], -jnp.inf))).astype(BF16)
              for i, (bi, h) in enumerate(chains)]
        pv = [jnp.dot(pm[i], vext[i], preferred_element_type=F32) for i in n]
        kw = [(Zt[bi][:, 3 * SUBLANES + h:3 * SUBLANES + h + 1] * k[i].astype(F32)).astype(BF16)
              for i, (bi, h) in enumerate(chains)]
        upd = [lax.dot_general(kw[i], vext[i], tn, preferred_element_type=F32) for i in n]
        for i, (bi, h) in enumerate(chains):
            c_ref[bi * M_HEADS + h] = dec[bi][h:h + 1, :] * cst[i] + upd[i]
            nd = pv[i] + Zt[bi][:, SUBLANES + h:SUBLANES + h + 1] * qc[i]
            hh = nd[:, :M_HD] / jnp.maximum(jnp.abs(nd[:, M_HD:]),
                                            Zt[bi][:, 2 * SUBLANES + h:2 * SUBLANES + h + 1])
            hh = _sigmoid(vo_ref[rows[bi], hs2[h]].astype(F32)) * hh
            hn = hh * lax.rsqrt(jnp.mean(hh * hh, axis=-1, keepdims=True) + LN_EPS)
            out_ref[rows[bi], hs[h]] = (hn * gain_ref[:, hs[h]]).astype(BF16)
        return carry

    lax.fori_loop(0, NC, chunk, 0)


def _mlstm(oa, g3, u_tri, gain, *, B, S, L, nb):
    N = oa.shape[0]
    NC = S // L
    R = nb * S
    kern = functools.partial(_mlstm_kernel, L=L, NC=NC, S=S, nb=nb)
    return pl.pallas_call(
        kern,
        grid=(B // nb,),
        in_specs=[pl.BlockSpec((R, 2 * M_W), lambda b: (b, C_QK // (2 * M_W))),
                  pl.BlockSpec((R, 2 * M_W), lambda b: (b, C_VO // (2 * M_W))),
                  pl.BlockSpec((nb * NC, 2 * SUBLANES, L), lambda b: (b, 0, 0)),
                  pl.BlockSpec((L, L), lambda b: (0, 0)),
                  pl.BlockSpec((1, M_W), lambda b: (0, 0))],
        out_specs=pl.BlockSpec((R, M_W), lambda b: (b, 0)),
        out_shape=jax.ShapeDtypeStruct((N, M_W), BF16),
        scratch_shapes=[pltpu.VMEM((nb * M_HEADS, M_HD, 2 * M_HD), F32),
                        pltpu.VMEM((nb * NC, L, LANES), F32),
                        pltpu.VMEM((nb * NC, SUBLANES, L), F32),
                        pltpu.VMEM((nb * NC, SUBLANES, 2 * M_HD), F32)],
        compiler_params=_cparams(vmem_limit=VMEM_LIMIT_BIG),
        name="mlstm",
    )(oa, oa, g3, u_tri, gain)


_G_LEVELS = 6
_G_XROW = 2 * G_CHUNK + SUBLANES


def _gla_consts():
    L = G_CHUNK
    t = np.arange(L)
    blocks = [(t[None, :] <= t[:, None]).astype(np.float32),
              (t[None, :] > t[:, None]).astype(np.float32),
              np.ones((SUBLANES, L), np.float32)]
    masks = [np.eye(L, dtype=np.float32)]
    m = 1
    while m < L:
        wl = np.zeros((L, L), np.float32)
        for r in range(L):
            r0 = (r // (2 * m)) * 2 * m + m
            if r % (2 * m) >= m:
                wl[r, r0:r + 1] = 1.0
            else:
                wl[r, r + 1:r0] = 1.0
        blocks.append(wl)
        tt, ss = t[:, None], t[None, :]
        masks.append(((tt // (2 * m) == ss // (2 * m)) & (tt % (2 * m) >= m)
                      & (ss % (2 * m) < m)).astype(np.float32))
        m *= 2
    w = np.concatenate(blocks, axis=0)
    w3 = np.concatenate([w, w, w], axis=1)
    mk = np.stack([np.concatenate([x] * G_HEADS, axis=0) for x in masks])
    return w3, mk


def _gla_kernel(qk_ref, v_ref, gg_ref, la_ref, w3_ref, mk_ref, gain_ref, out_ref, st_ref, *, NC, S, nb):
    L = G_CHUNK
    st_ref[...] = jnp.zeros_like(st_ref)
    lane_head = lax.broadcasted_iota(jnp.int32, (L, G_KW), 1) // G_DK
    br = lax.broadcasted_iota(jnp.int32, (2 * G_DV, LANES), 0) < G_DV
    bl = lax.broadcasted_iota(jnp.int32, (2 * G_DV, LANES), 1) < G_DK
    bmask = br == bl
    nt = (((1,), (1,)), ((), ()))
    tn = (((0,), (0,)), ((), ()))

    def chunk(c, carry):
        rows = [pl.ds(pl.multiple_of(bi * S + c * L, L), L) for bi in range(nb)]
        X, q, k = [], [], []
        for bi in range(nb):
            la = la_ref[rows[bi], :]
            hi = la.astype(BF16)
            r1 = la - hi.astype(F32)
            mid = r1.astype(BF16)
            lo = (r1 - mid.astype(F32)).astype(BF16)
            stk = jnp.concatenate([hi, mid, lo], axis=0)
            X.append(jnp.exp(jnp.dot(w3_ref[...], stk, preferred_element_type=F32)))
            q.append(qk_ref[rows[bi], 0:G_KW].astype(F32))
            k.append(qk_ref[rows[bi], G_KW:2 * G_KW].astype(F32))

        sc = [[None] * (_G_LEVELS + 1) for _ in range(nb)]
        for lev in range(_G_LEVELS + 1):
            for bi in range(nb):
                if lev == 0:
                    qt, kt = q[bi], k[bi]
                else:
                    xl = X[bi][_G_XROW + L * (lev - 1):_G_XROW + L * lev, :]
                    qt, kt = q[bi] * xl, k[bi] * xl
                q4 = jnp.concatenate([jnp.where(lane_head == h, qt, 0.0) for h in range(G_HEADS)],
                                     axis=0).astype(BF16)
                sc[bi][lev] = lax.dot_general(q4, kt.astype(BF16), nt, preferred_element_type=F32)
        Ab = []
        for bi in range(nb):
            A = sc[bi][0] * mk_ref[0]
            for lev in range(1, _G_LEVELS + 1):
                A = A + sc[bi][lev] * mk_ref[lev]
            Ab.append(A.astype(BF16))

        for bi in range(nb):
            gg = gg_ref[rows[bi], :].astype(F32)
            gate = gg * _sigmoid(gg)
            for p in range(2):
                ls = slice(LANES * p, LANES * (p + 1))
                vp = v_ref[rows[bi], 2 * G_DV * p:2 * G_DV * (p + 1)]
                oi = [jnp.dot(Ab[bi][L * (2 * p + hh):L * (2 * p + hh + 1)],
                              vp[:, G_DV * hh:G_DV * (hh + 1)], preferred_element_type=F32)
                      for hh in range(2)]
                st = st_ref[bi, p]
                qc = (q[bi][:, ls] * X[bi][0:L, ls]).astype(BF16)
                o_inter = lax.dot_general(qc, st.astype(BF16), nt, preferred_element_type=F32)
                kc = (k[bi][:, ls] * X[bi][L:2 * L, ls]).astype(BF16)
                upd = lax.dot_general(vp, kc, tn, preferred_element_type=F32)
                dec = X[bi][2 * L:2 * L + 1, ls]
                st_ref[bi, p] = jnp.where(bmask, dec * st + upd, 0.0)
                for hh in range(2):
                    o = o_inter[:, G_DV * hh:G_DV * (hh + 1)] + oi[hh]
                    hn = o * lax.rsqrt(jnp.mean(o * o, axis=-1, keepdims=True) + LN_EPS)
                    hs = slice(G_DV * (2 * p + hh), G_DV * (2 * p + hh + 1))
                    out_ref[rows[bi], hs] = (hn * gain_ref[:, hs] * gate[:, hs]).astype(BF16)
        return carry

    lax.fori_loop(0, NC, chunk, 0)


def _gla(oa, la, w3, mk, gain, *, B, S, nb):
    N = oa.shape[0]
    NC = S // G_CHUNK
    kern = functools.partial(_gla_kernel, NC=NC, S=S, nb=nb)
    R = nb * S
    return pl.pallas_call(
        kern,
        grid=(B // nb,),
        in_specs=[pl.BlockSpec((R, 2 * G_KW), lambda b: (b, C_GQK // (2 * G_KW))),
                  pl.BlockSpec((R, G_W), lambda b: (b, C_GV // G_W)),
                  pl.BlockSpec((R, G_W), lambda b: (b, C_GG // G_W)),
                  pl.BlockSpec((R, G_KW), lambda b: (b, 0)),
                  pl.BlockSpec(w3.shape, lambda b: (0, 0)),
                  pl.BlockSpec(mk.shape, lambda b: (0, 0, 0)),
                  pl.BlockSpec((1, G_W), lambda b: (0, 0))],
        out_specs=pl.BlockSpec((R, G_W), lambda b: (b, 0)),
        out_shape=jax.ShapeDtypeStruct((N, G_W), BF16),
        scratch_shapes=[pltpu.VMEM((nb, 2, 2 * G_DV, LANES), F32)],
        compiler_params=_cparams(),
        name="gla",
    )(oa, oa, oa, la, w3, mk, gain)


def _layer_norm(z, g, b):
    mu = jnp.mean(z, axis=-1, keepdims=True)
    zc = z - mu
    var = jnp.mean(zc * zc, axis=-1, keepdims=True)
    return zc * lax.rsqrt(var + LN_EPS) * g + b


def _outproj_kernel(hm_ref, hg_ref, wf_ref, x_ref, mod_ref, g_ref, b_ref, wr_ref, br_ref,
                    x1_ref, u2_ref, rrow_ref, w_ref, *, tb, nh):
    @pl.when(pl.program_id(0) == 0)
    def _():
        w_ref[...] = wf_ref[...].astype(BF16)

    mod = mod_ref[0]
    blocks = [slice(tb * j, tb * (j + 1)) for j in range(nh)]
    y = [jnp.dot(hm_ref[r, :], w_ref[0:M_W, :], preferred_element_type=F32)
         + jnp.dot(hg_ref[r, :], w_ref[M_W:M_W + G_W, :], preferred_element_type=F32) for r in blocks]
    u2 = []
    for j, r in enumerate(blocks):
        z = ALPHA * x_ref[r, :] + (1.0 + mod[2:3, :]) * y[j]
        x1 = _layer_norm(z, g_ref[...], b_ref[...])
        x1_ref[r, :] = x1
        u2.append(x1 * (1.0 + mod[4:5, :]) + mod[3:4, :])
        u2_ref[r, :] = u2[j].astype(BF16)

    u2h = [u.astype(BF16) for u in u2]
    u2l = [(u2[j] - u2h[j].astype(F32)).astype(BF16) for j in range(nh)]
    lh = [jnp.dot(u, wr_ref[...], preferred_element_type=F32) for u in u2h]
    ll = [jnp.dot(u, wr_ref[:, 0:LANES], preferred_element_type=F32) for u in u2l]
    for j in range(nh):
        logits = lh[j][:, 0:LANES] + lh[j][:, LANES:2 * LANES] + ll[j] + br_ref[...]
        rrow_ref[j] = _route_select(logits.T, tb)


def _route_select(lt, tm):
    row = lax.broadcasted_iota(jnp.int32, (SUBLANES, tm), 0)
    gl = jnp.where(row < N_GROUPS, lt[0:SUBLANES, :], -jnp.inf)
    gmax = jnp.max(gl, axis=0, keepdims=True)
    gsel = jnp.min(jnp.where(gl == gmax, row, SUBLANES), axis=0, keepdims=True)
    pg = 1.0 / jnp.sum(jnp.exp(gl - gmax), axis=0, keepdims=True)
    ein = jnp.zeros((SUBLANES, tm), F32)
    for g in range(N_GROUPS):
        ein = jnp.where(gsel == g, lt[SUBLANES * (g + 1):SUBLANES * (g + 2), :], ein)
    v1 = jnp.max(ein, axis=0, keepdims=True)
    i1 = jnp.min(jnp.where(ein == v1, row, SUBLANES), axis=0, keepdims=True)
    rest = jnp.where(row == i1, -jnp.inf, ein)
    v2 = jnp.max(rest, axis=0, keepdims=True)
    i2 = jnp.min(jnp.where(rest == v2, row, SUBLANES), axis=0, keepdims=True)
    t2 = jnp.exp(v2 - v1)
    p1 = 1.0 / (1.0 + t2)
    e0 = (gsel * E_PER_G + i1).astype(F32)
    e1 = (gsel * E_PER_G + i2).astype(F32)
    return jnp.concatenate([e0, e1, pg * p1, pg * (t2 * p1), jnp.zeros((SUBLANES - 4, tm), F32)], axis=0)


def _outproj(hm, hg, w_out, x2, mod3, g, b, wr, br, *, S, tb, nh):
    N, D = x2.shape
    tm = tb * nh
    tpb = S // tm
    kern = functools.partial(_outproj_kernel, tb=tb, nh=nh)
    return pl.pallas_call(
        kern,
        grid=(N // tm,),
        in_specs=[pl.BlockSpec((tm, M_W), lambda i: (i, 0)),
                  pl.BlockSpec((tm, G_W), lambda i: (i, 0)),
                  pl.BlockSpec((M_W + G_W, D), lambda i: (0, 0), pipeline_mode=pl.Buffered(1)),
                  pl.BlockSpec((tm, D), lambda i: (i, 0)),
                  pl.BlockSpec((1, 6, D), lambda i: (i // tpb, 0, 0)),
                  pl.BlockSpec((1, D), lambda i: (0, 0)),
                  pl.BlockSpec((1, D), lambda i: (0, 0)),
                  pl.BlockSpec((D, 2 * LANES), lambda i: (0, 0)),
                  pl.BlockSpec((1, LANES), lambda i: (0, 0))],
        out_specs=[pl.BlockSpec((tm, D), lambda i: (i, 0)),
                   pl.BlockSpec((tm, D), lambda i: (i, 0)),
                   pl.BlockSpec((nh, SUBLANES, tb), lambda i: (i, 0, 0))],
        out_shape=[jax.ShapeDtypeStruct((N, D), F32),
                   jax.ShapeDtypeStruct((N, D), BF16),
                   jax.ShapeDtypeStruct((N // tb, SUBLANES, tb), F32)],
        scratch_shapes=[pltpu.VMEM((M_W + G_W, D), BF16)],
        compiler_params=_cparams(),
        name="outproj",
    )(hm, hg, w_out, x2, mod3, g, b, wr, br)


def _slots_per_tile(tb):
    worst = 2 * tb + N_EXP * (GRAN - 1)
    return -(-worst // LANES) * LANES


def _ffn_tiles(n_tok, tb):
    worst_rows = 2 * n_tok + (n_tok // tb) * N_EXP * (GRAN - 1)
    return -(-worst_rows // FFN_TM) + N_EXP


def _route_kernel(rr_ref, u_ref, lt_ref, srow_ref, col_ref, gd_ref, meta_ref, mg_ref, part_ref,
                  *, NT, tb, TM):
    iota_e = lax.broadcasted_iota(jnp.int32, (N_EXP, tb), 0).astype(F32)
    glane = lax.broadcasted_iota(jnp.int32, (N_EXP, LANES), 1).astype(F32)
    ltri = lt_ref[...]

    def prefix_e(col):
        return jnp.dot(ltri, jnp.broadcast_to(col, (N_EXP, LANES)),
                       preferred_element_type=F32, precision=HIGHEST)[:, 0:1]

    def p1(j, run8):
        r = rr_ref[j]
        oh0 = jnp.where(iota_e == r[0:1, :], 1.0, 0.0)
        oh1 = jnp.where(iota_e == r[1:2, :], 1.0, 0.0)
        cum0 = jnp.dot(oh0.astype(BF16), u_ref[...], preferred_element_type=F32)
        cum1 = jnp.dot(oh1.astype(BF16), u_ref[...], preferred_element_type=F32)
        c0 = jnp.sum(oh0, axis=1, keepdims=True)
        n8 = jnp.floor((c0 + jnp.sum(oh1, axis=1, keepdims=True) + (GRAN - 1.0)) * (1.0 / GRAN))
        lo8 = prefix_e(n8)
        s0 = jnp.sum(oh0 * (GRAN * lo8 + cum0 - 1.0), axis=0, keepdims=True)
        s1 = jnp.sum(oh1 * (GRAN * lo8 + c0 + cum1 - 1.0), axis=0, keepdims=True)
        info = jnp.concatenate([s0, s1, r[2:4, :], jnp.zeros((SUBLANES - 4, tb), F32)], axis=0)
        srow_ref[j] = info
        col_ref[pl.ds(pl.multiple_of(j * tb, tb), tb), :] = jnp.concatenate(
            [info, jnp.zeros((LANES - SUBLANES, tb), F32)], axis=0).T
        mg = jnp.where((lo8 <= glane) & (glane < lo8 + n8), 1.0, 0.0)
        mg_ref[j] = mg
        part = jnp.sum(mg * (run8 + glane - lo8), axis=0, keepdims=True)
        gcnt = jnp.broadcast_to(jnp.sum(n8, axis=0, keepdims=True), (1, LANES))
        part_ref[j] = jnp.concatenate([part, gcnt, jnp.zeros((SUBLANES - 2, LANES), F32)], axis=0)
        return run8 + n8

    tot8 = lax.fori_loop(0, NT, p1, jnp.zeros((N_EXP, 1), F32), unroll=4 if NT % 4 == 0 else 1)
    seg_t = jnp.floor((tot8 * GRAN + (TM - 1.0)) * (1.0 / TM))
    base_t = prefix_e(seg_t)
    base8 = base_t * (TM // GRAN)
    lane1 = lax.broadcasted_iota(jnp.int32, (1, LANES), 1)

    def p2(j, carry):
        pr = part_ref[j]
        dst = (pr[0:1, :] + jnp.sum(mg_ref[j] * base8, axis=0, keepdims=True)) * GRAN
        gd_ref[j] = jnp.where(lane1 == G_LAST, pr[1:2, :], dst).astype(jnp.int32)
        return carry

    lax.fori_loop(0, NT, p2, 0, unroll=4 if NT % 4 == 0 else 1)
    eye = jnp.where(glane == lax.broadcasted_iota(jnp.int32, (N_EXP, LANES), 0).astype(F32), 1.0, 0.0)
    tail_row = jnp.sum(eye * ((base8 + tot8) * GRAN), axis=0, keepdims=True)
    tail_n8 = jnp.sum(eye * (seg_t * (TM // GRAN) - tot8), axis=0, keepdims=True)
    nv_l = jnp.broadcast_to(jnp.sum(seg_t, axis=0, keepdims=True), (1, LANES))
    gd_ref[NT] = jnp.where(lane1 == G_LAST, nv_l, tail_row).astype(jnp.int32)
    gd_ref[NT + 1] = tail_n8.astype(jnp.int32)
    ti = lax.broadcasted_iota(jnp.int32, (N_EXP, tb), 1).astype(F32)
    te = jnp.sum(jnp.where(base_t <= ti, 1.0, 0.0), axis=0, keepdims=True) - 1.0
    nv = jnp.broadcast_to(jnp.sum(seg_t, axis=0, keepdims=True), (1, tb))
    meta_ref[...] = jnp.concatenate([te, nv, jnp.zeros((SUBLANES - 2, tb), F32)],
                                    axis=0).astype(jnp.int32)


def _route(rrow, u_cnt, ltri, *, TM):
    NT, _, tb = rrow.shape
    kern = functools.partial(_route_kernel, NT=NT, tb=tb, TM=TM)
    full3 = lambda i: (0, 0, 0)
    return pl.pallas_call(
        kern,
        grid=(1,),
        in_specs=[pl.BlockSpec((NT, SUBLANES, tb), full3),
                  pl.BlockSpec((tb, tb), lambda i: (0, 0)),
                  pl.BlockSpec((N_EXP, N_EXP), lambda i: (0, 0))],
        out_specs=[pl.BlockSpec((NT, SUBLANES, tb), full3),
                   pl.BlockSpec((NT * tb, LANES), lambda i: (0, 0)),
                   pl.BlockSpec((NT + 2, 1, LANES), full3),
                   pl.BlockSpec((SUBLANES, tb), lambda i: (0, 0))],
        out_shape=[jax.ShapeDtypeStruct((NT, SUBLANES, tb), F32),
                   jax.ShapeDtypeStruct((NT * tb, LANES), F32),
                   jax.ShapeDtypeStruct((NT + 2, 1, LANES), jnp.int32),
                   jax.ShapeDtypeStruct((SUBLANES, tb), jnp.int32)],
        scratch_shapes=[pltpu.VMEM((NT, N_EXP, LANES), F32), pltpu.VMEM((NT, SUBLANES, LANES), F32)],
        compiler_params=_cparams(),
        name="route",
    )(rrow, u_cnt, ltri)


U32 = jnp.uint32
_HI_MASK = 0xFFFF0000


def _pack_halves(x):
    c = x.shape[1] // 2
    lo = lax.bitcast_convert_type(x[:, :c], U32)
    hi = lax.bitcast_convert_type(x[:, c:], U32)
    return (lo >> 16) | (hi & U32(_HI_MASK))


def _unpack_halves(w):
    lo = lax.bitcast_convert_type(w << 16, F32)
    hi = lax.bitcast_convert_type(w & U32(_HI_MASK), F32)
    return jnp.concatenate([lo, hi], axis=1).astype(BF16)


def _granule_copy(src_ref, src_row, dst_ref, dst_row, sem):
    return pltpu.make_async_copy(src_ref.at[pl.ds(src_row, GRAN), :], dst_ref.at[pl.ds(dst_row, GRAN), :], sem)


def _for_granules(n, body, unroll=4):
    def blk(i, carry):
        for t in range(unroll):
            body(i * unroll + t)
        return carry

    def one(g, carry):
        body(g)
        return carry

    nblk = n // unroll
    lax.fori_loop(0, nblk, blk, 0)
    lax.fori_loop(nblk * unroll, n, one, 0)


def _wait_granules(n, src_ref, dst_ref, sem, n_max):
    b = 1
    while b <= n_max:
        @pl.when((n & b) != 0)
        def _(b=b):
            pltpu.make_async_copy(src_ref.at[pl.ds(0, b * GRAN), :], dst_ref.at[pl.ds(0, b * GRAN), :],
                                  sem).wait()
        b *= 2


def _dispatch_kernel(gd_ref, srow_ref, u_ref, xs_ref, buf, zbuf, sems, *, NT, SL, TM, n_tiles):
    j = pl.program_id(0)
    slot = j % 2
    zsem = sems.at[2]

    def drain(tile, sl):
        _wait_granules(gd_ref[tile, G_LAST], buf.at[sl], xs_ref, sems.at[sl], SL // GRAN)

    def tile_fill(t):
        return pltpu.make_async_copy(zbuf, xs_ref.at[pl.ds(pl.multiple_of(t * TM, TM), TM), :], zsem)

    def zero_fill(wait):
        for e in range(N_EXP):
            n, row0 = gd_ref[NT + 1, e], gd_ref[NT, e]
            b = TM // GRAN // 2
            while b >= 1:
                @pl.when((n & b) != 0)
                def _(b=b, n=n, row0=row0):
                    start = pl.multiple_of(row0 + ((n >> b.bit_length()) << b.bit_length()) * GRAN, GRAN)
                    cp = pltpu.make_async_copy(zbuf.at[pl.ds(0, b * GRAN), :],
                                               xs_ref.at[pl.ds(start, b * GRAN), :], zsem)
                    cp.wait() if wait else cp.start()
                b //= 2

        def zt(t, carry):
            tile_fill(t).wait() if wait else tile_fill(t).start()
            return carry
        lax.fori_loop(gd_ref[NT, G_LAST], n_tiles, zt, 0)

    @pl.when(j == 0)
    def _():
        zbuf[...] = jnp.zeros_like(zbuf)
        zero_fill(False)

    @pl.when(j >= 2)
    def _():
        drain(j - 2, slot)

    s = srow_ref[0]
    rows = lax.broadcasted_iota(jnp.int32, (SL, s.shape[1]), 0).astype(F32)
    m0 = rows == s[0:1, :]
    m1 = rows == s[1:2, :]
    oh = jnp.where(m0 | m1, 1.0, 0.0).astype(BF16)
    dw = u_ref.shape[1] // 2
    buf[slot, :, 0:dw] = _pack_halves(jnp.dot(oh, u_ref[...], preferred_element_type=F32))
    wrow = jnp.sum(jnp.where(m0, s[2:3, :], 0.0) + jnp.where(m1, s[3:4, :], 0.0), axis=1, keepdims=True)
    buf[slot, :, dw:dw + LANES] = lax.bitcast_convert_type(jnp.broadcast_to(wrow, (SL, LANES)), U32)

    def issue(g):
        _granule_copy(buf.at[slot], pl.multiple_of(g * GRAN, GRAN), xs_ref,
                      pl.multiple_of(gd_ref[j, g], GRAN), sems.at[slot]).start()

    _for_granules(gd_ref[j, G_LAST], issue)

    @pl.when(j == NT - 1)
    def _():
        drain(j, slot)
        if NT > 1:
            drain(j - 1, 1 - slot)
        zero_fill(True)


def _dispatch(gd, srow, u2, *, n_tiles, TM):
    N, D = u2.shape
    NT, _, tb = srow.shape
    SL = _slots_per_tile(tb)
    n_rows = n_tiles * TM
    kern = functools.partial(_dispatch_kernel, NT=NT, SL=SL, TM=TM, n_tiles=n_tiles)
    grid_spec = pltpu.PrefetchScalarGridSpec(
        num_scalar_prefetch=1,
        grid=(NT,),
        in_specs=[pl.BlockSpec((1, SUBLANES, tb), lambda j, gd: (j, 0, 0)),
                  pl.BlockSpec((tb, D), lambda j, gd: (j, 0))],
        out_specs=pl.BlockSpec(memory_space=pl.ANY),
        scratch_shapes=[pltpu.VMEM((2, SL, D // 2 + LANES), U32), pltpu.VMEM((TM, D // 2 + LANES), U32),
                        pltpu.SemaphoreType.DMA((3,))],
    )
    return pl.pallas_call(
        kern,
        grid_spec=grid_spec,
        out_shape=jax.ShapeDtypeStruct((n_rows, D // 2 + LANES), U32),
        compiler_params=_cparams(),
        name="dispatch",
    )(gd, srow, u2)


def _ffn_kernel(te_ref, nv_ref, xs_ref, wg_ref, wu_ref, wd_ref, o_ref, wgb, wub, wdb, sg, su, sd, slot_ref,
                sems):
    i = pl.program_id(0)
    nv = nv_ref[0]
    e = te_ref[i]

    def weight_copies(ex, sl):
        return (pltpu.make_async_copy(wg_ref.at[ex], sg.at[sl], sems.at[sl]),
                pltpu.make_async_copy(wu_ref.at[ex], su.at[sl], sems.at[sl]),
                pltpu.make_async_copy(wd_ref.at[ex], sd.at[sl], sems.at[sl]))

    @pl.when(i == 0)
    def _():
        slot_ref[0] = 0
        for cp in weight_copies(e, 0):
            cp.start()

    @pl.when((i < nv) & ((i == 0) | (e != te_ref[jnp.maximum(i - 1, 0)])))
    def _():
        sl = slot_ref[0]
        for cp in weight_copies(e, sl):
            cp.wait()
        wgb[...] = sg[sl].astype(BF16)
        wub[...] = su[sl].astype(BF16)
        wdb[...] = sd[sl].astype(BF16)
        nxt = lax.while_loop(lambda t: (t < nv) & (te_ref[jnp.minimum(t, nv - 1)] == e), lambda t: t + 1, i + 1)

        @pl.when(nxt < nv)
        def _():
            for cp in weight_copies(te_ref[nxt], 1 - sl):
                cp.start()
        slot_ref[0] = 1 - sl

    @pl.when(i < nv)
    def _():
        nsub = FFN_SUB
        hm = xs_ref.shape[0] // nsub
        dw = o_ref.shape[1]
        halves = tuple(slice(hm * j, hm * (j + 1)) for j in range(nsub))
        x = [_unpack_halves(xs_ref[r, 0:dw]) for r in halves]
        g = [jnp.dot(x[j], wgb[...], preferred_element_type=F32) for j in range(nsub)]
        u = [jnp.dot(x[j], wub[...], preferred_element_type=F32) for j in range(nsub)]
        h = [(g[j] * _sigmoid(g[j]) * u[j]).astype(BF16) for j in range(nsub)]
        y = [jnp.dot(h[j], wdb[...], preferred_element_type=F32) for j in range(nsub)]
        for j in range(nsub):
            wt = lax.bitcast_convert_type(xs_ref[halves[j], dw:dw + LANES], F32)
            yw = y[j] * jnp.concatenate([wt] * (2 * dw // LANES), axis=1)
            o_ref[halves[j], :] = _pack_halves(yw.astype(BF16).astype(F32))

    @pl.when(i >= nv_ref[0])
    def _():
        o_ref[...] = jnp.zeros_like(o_ref)


def _ffn(te, nv, xs, wg, wu, wd, *, TM):
    P, XW = xs.shape
    DW = XW - LANES
    D = 2 * DW
    n_tiles = P // TM
    grid_spec = pltpu.PrefetchScalarGridSpec(
        num_scalar_prefetch=2,
        grid=(n_tiles,),
        in_specs=[pl.BlockSpec((TM, XW), lambda i, te, nv: (jnp.maximum(jnp.minimum(i, nv[0] - 1), 0), 0)),
                  pl.BlockSpec(memory_space=pl.ANY),
                  pl.BlockSpec(memory_space=pl.ANY),
                  pl.BlockSpec(memory_space=pl.ANY)],
        out_specs=pl.BlockSpec((TM, DW), lambda i, te, nv: (i, 0)),
        scratch_shapes=[pltpu.VMEM((D, D_EXP), BF16), pltpu.VMEM((D, D_EXP), BF16),
                        pltpu.VMEM((D_EXP, D), BF16),
                        pltpu.VMEM((2, D, D_EXP), F32), pltpu.VMEM((2, D, D_EXP), F32),
                        pltpu.VMEM((2, D_EXP, D), F32), pltpu.SMEM((1,), jnp.int32),
                        pltpu.SemaphoreType.DMA((2,))],
    )
    return pl.pallas_call(
        _ffn_kernel,
        grid_spec=grid_spec,
        out_shape=jax.ShapeDtypeStruct((P, DW), U32),
        compiler_params=_cparams(),
        name="ffn",
    )(te, nv, xs, wg, wu, wd)


def _combine_kernel(gd_ref, ys_ref, col_ref, x1_ref, mod_ref, g_ref, b_ref, o_ref, buf, sems, *, NT, SL):
    j = pl.program_id(0)
    slot = j % 2

    def fetch(tile, sl):
        def f(g):
            _granule_copy(ys_ref, pl.multiple_of(gd_ref[tile, g], GRAN), buf.at[sl],
                          pl.multiple_of(g * GRAN, GRAN), sems.at[sl]).start()
        _for_granules(gd_ref[tile, G_LAST], f)

    @pl.when(j == 0)
    def _():
        fetch(0, 0)

    @pl.when(j + 1 < NT)
    def _():
        fetch(j + 1, 1 - slot)

    ng = gd_ref[j, G_LAST]

    _wait_granules(ng, ys_ref, buf.at[slot], sems.at[slot], SL // GRAN)

    rows = lax.broadcasted_iota(jnp.int32, (SL, 1), 0)
    yb = _unpack_halves(jnp.where(rows < ng * GRAN, buf[slot], U32(0)))
    col = col_ref[...]
    tb = col.shape[0]
    lanes = lax.broadcasted_iota(jnp.int32, (tb, SL), 1).astype(F32)
    sel = jnp.where((lanes == col[:, 0:1]) | (lanes == col[:, 1:2]), 1.0, 0.0).astype(BF16)
    y = jnp.dot(sel, yb, preferred_element_type=F32)
    mod = mod_ref[0]
    z = ALPHA * x1_ref[...] + (1.0 + mod[5:6, :]) * y
    o_ref[...] = _layer_norm(z, g_ref[...], b_ref[...])


def _combine(gd, ys, col, x1, mod3, g, b, *, S, tb):
    N, D = x1.shape
    NT = N // tb
    tpb = S // tb
    SL = _slots_per_tile(tb)
    kern = functools.partial(_combine_kernel, NT=NT, SL=SL)
    grid_spec = pltpu.PrefetchScalarGridSpec(
        num_scalar_prefetch=1,
        grid=(NT,),
        in_specs=[pl.BlockSpec(memory_space=pl.ANY),
                  pl.BlockSpec((tb, LANES), lambda j, gd: (j, 0)),
                  pl.BlockSpec((tb, D), lambda j, gd: (j, 0)),
                  pl.BlockSpec((1, 6, D), lambda j, gd: (j // tpb, 0, 0)),
                  pl.BlockSpec((1, D), lambda j, gd: (0, 0)),
                  pl.BlockSpec((1, D), lambda j, gd: (0, 0))],
        out_specs=pl.BlockSpec((tb, D), lambda j, gd: (j, 0)),
        scratch_shapes=[pltpu.VMEM((2, SL, D // 2), U32), pltpu.SemaphoreType.DMA((2,))],
    )
    return pl.pallas_call(
        kern,
        grid_spec=grid_spec,
        out_shape=jax.ShapeDtypeStruct((N, D), F32),
        compiler_params=_cparams(),
        name="combine",
    )(gd, ys, col, x1, mod3, g, b)


def _layer(x, c, l, w_ada, b_ada, w_in, w_conv, b_conv, b_igate, b_fgate, mlstm_norm_g, w_gla_a, b_gla_a,
           gla_norm_g, w_out, ln1_g, ln1_b, w_route_group, b_route_group, w_route_expert, b_route_expert,
           w_gate, w_up, w_down, ln2_g, ln2_b):
    B, S, D = x.shape
    N = B * S
    x2 = x.reshape(N, D)
    tm_in = min(512, S)
    tm = min(256, S)
    lm = min(256, S)

    mod3 = _ada(c, w_ada[l], b_ada[l]).reshape(B, 6, D)

    wa_pad = jnp.zeros((LANES, G_KW), F32).at[SM_A:SM_A + G_RANK].set(w_gla_a[l]).astype(BF16)
    bg = (jnp.zeros((2 * SUBLANES, 1), F32).at[0:M_HEADS, 0].set(b_igate[l])
          .at[SUBLANES:SUBLANES + M_HEADS, 0].set(b_fgate[l]))
    oa, la, g3 = _inproj(x2, mod3, w_in[l], w_conv[l], b_conv[l].reshape(1, -1), wa_pad,
                         b_gla_a[l].reshape(1, -1), bg, S=S, tm=tm_in, lm=lm)

    u_tri = jnp.asarray(np.triu(np.ones((lm, lm), np.float32)))
    nb = 2 if B % 2 == 0 else 1
    hm = _mlstm(oa, g3, u_tri, mlstm_norm_g[l].reshape(1, -1), B=B, S=S, L=lm, nb=nb)
    w3_np, mk_np = _gla_consts()
    hg = _gla(oa, la, jnp.asarray(w3_np, BF16), jnp.asarray(mk_np), gla_norm_g[l].reshape(1, -1), B=B, S=S,
              nb=nb)

    wr = (jnp.zeros((D, LANES), F32).at[:, 0:N_GROUPS].set(w_route_group[l])
          .at[:, SUBLANES:SUBLANES + N_EXP].set(w_route_expert[l]))
    br = (jnp.zeros((1, LANES), F32).at[0, 0:N_GROUPS].set(b_route_group[l])
          .at[0, SUBLANES:SUBLANES + N_EXP].set(b_route_expert[l]))
    wr_hi = wr.astype(BF16)
    wr2 = jnp.concatenate([wr_hi, (wr - wr_hi.astype(F32)).astype(BF16)], axis=1)
    x1, u2, rrow = _outproj(hm, hg, w_out[l], x2, mod3, ln1_g[l].reshape(1, -1),
                            ln1_b[l].reshape(1, -1), wr2, br, S=S, tb=tm, nh=2 if S % (2 * tm) == 0 else 1)

    u_cnt = jnp.asarray(np.triu(np.ones((tm, tm), np.float32)), BF16)
    ltri = jnp.asarray(np.tril(np.ones((N_EXP, N_EXP), np.float32), -1))
    srow, col, gd3, meta = _route(rrow, u_cnt, ltri, TM=FFN_TM)
    gd = gd3.reshape(N // tm + 2, LANES)
    n_tiles = _ffn_tiles(N, tm)
    te, nv = meta[0, :n_tiles], meta[1, 0:1]

    xs = _dispatch(gd, srow, u2, n_tiles=n_tiles, TM=FFN_TM)
    ys = _ffn(te, nv, xs, w_gate[l], w_up[l], w_down[l], TM=FFN_TM)
    out = _combine(gd, ys, col, x1, mod3, ln2_g[l].reshape(1, -1), ln2_b[l].reshape(1, -1), S=S, tb=tm)
    return out.reshape(B, S, D)


def kernel(x, c, w_ada, b_ada, w_in, w_conv, b_conv, b_igate, b_fgate, mlstm_norm_g, w_gla_a, b_gla_a,
           gla_norm_g, w_out, ln1_g, ln1_b, w_route_group, b_route_group, w_route_expert, b_route_expert,
           w_gate, w_up, w_down, ln2_g, ln2_b):
    for l in range(DEPTH):
        x = _layer(x, c, l, w_ada, b_ada, w_in, w_conv, b_conv, b_igate, b_fgate, mlstm_norm_g, w_gla_a,
                   b_gla_a, gla_norm_g, w_out, ln1_g, ln1_b, w_route_group, b_route_group, w_route_expert,
                   b_route_expert, w_gate, w_up, w_down, ln2_g, ln2_b)
    return x
```

```python
import functools

import numpy as np
import jax
import jax.numpy as jnp
from jax import lax
from jax.experimental import pallas as pl
from jax.experimental.pallas import tpu as pltpu

F32 = jnp.float32
BF16 = jnp.bfloat16
HIGHEST = lax.Precision.HIGHEST

DEPTH = 1
M_HEADS = 4
M_HD = 128
M_W = M_HEADS * M_HD
CONV_W = 4
G_HEADS = 4
G_DK = 64
G_DV = 128
G_W = G_HEADS * G_DV
G_KW = G_HEADS * G_DK
G_RANK = 16
G_TAU = 16.0
G_CHUNK = 64
N_GROUPS = 4
E_PER_G = 8
N_EXP = N_GROUPS * E_PER_G
D_EXP = 512
ALPHA = (2 * DEPTH) ** 0.25
LN_EPS = 1e-5

LANES = 128
SUBLANES = 8
VMEM_LIMIT = 48 * 1024 * 1024

C_QK = 0
C_VO = 1024
C_GQK = 2048
C_GV = 2560
C_GG = 3072
C_SMALL = 3584
C_TOT = 3712
SM_I, SM_F, SM_A = 0, 8, 16
IN_GATES = 4 * M_W
IN_G = IN_GATES + 2 * M_HEADS
IN_GA = IN_G + 2 * G_KW + 2 * G_W
IN_TOT = IN_GA + G_RANK

FFN_TM = 512
FFN_SUB = 2
GRAN = SUBLANES
G_LAST = LANES - 1


def _cparams(n_axes=1, vmem_limit=VMEM_LIMIT):
    return pltpu.CompilerParams(dimension_semantics=("arbitrary",) * n_axes,
                                vmem_limit_bytes=vmem_limit)


def _sigmoid(x):
    return 1.0 / (1.0 + jnp.exp(-x))


def _log_sigmoid(x):
    return jnp.minimum(x, 0.0) - jnp.log(1.0 + jnp.exp(-jnp.abs(x)))


def _ada_kernel(c_ref, w_ref, b_ref, o_ref):
    c = c_ref[...]
    ca = c * _sigmoid(c)
    o_ref[...] = jnp.dot(ca, w_ref[...], preferred_element_type=F32, precision=HIGHEST) + b_ref[...]


def _ada(c, w, b):
    B, D = c.shape
    n_out = w.shape[1]
    tn = 1024
    return pl.pallas_call(
        _ada_kernel,
        grid=(n_out // tn,),
        in_specs=[pl.BlockSpec((B, D), lambda j: (0, 0)),
                  pl.BlockSpec((D, tn), lambda j: (0, j)),
                  pl.BlockSpec((1, tn), lambda j: (0, j))],
        out_specs=pl.BlockSpec((B, tn), lambda j: (0, j)),
        out_shape=jax.ShapeDtypeStruct((B, n_out), F32),
        compiler_params=_cparams(),
        name="ada",
    )(c, w, b.reshape(1, n_out))


def _inproj_kernel(x_ref, mod_ref, win_ref, wc_ref, bc_ref, wa_ref, ba_ref, bg_ref,
                   oa_ref, la_ref, g_ref, halo_ref, w_ref, *, tm, tpb, lm):
    i = pl.program_id(0)

    @pl.when(i == 0)
    def _():
        rc = LANES
        for r in range(0, win_ref.shape[0], rc):
            rs = slice(r, r + rc)
            w_ref[rs, 0:IN_GATES] = win_ref[rs, 0:IN_GATES].astype(BF16)
            t = win_ref[rs, IN_GATES:IN_TOT]
            w_ref[rs, C_GQK:C_SMALL] = t[:, IN_G - IN_GATES:IN_GA - IN_GATES].astype(BF16)
            z = lambda n: jnp.zeros((rc, n), F32)
            small = jnp.concatenate([t[:, 0:M_HEADS], z(SM_F - M_HEADS), t[:, M_HEADS:2 * M_HEADS],
                                     z(SM_A - SM_F - M_HEADS), t[:, IN_GA - IN_GATES:IN_TOT - IN_GATES],
                                     z(LANES - SM_A - G_RANK)], axis=1)
            w_ref[rs, C_SMALL:C_TOT] = small.astype(BF16)

    @pl.when(i % tpb == 0)
    def _():
        halo_ref[0:SUBLANES, :] = jnp.zeros((SUBLANES, halo_ref.shape[1]), F32)

    mod = mod_ref[0]
    u = (x_ref[...] * (1.0 + mod[1:2, :]) + mod[0:1, :]).astype(BF16)

    p = jnp.dot(u, w_ref[:, C_QK:C_QK + 2 * M_W], preferred_element_type=F32)
    halo_ref[SUBLANES:SUBLANES + tm, :] = p
    acc = bc_ref[...] + wc_ref[CONV_W - 1:CONV_W, :] * p
    for j in range(CONV_W - 1):
        acc = acc + wc_ref[j:j + 1, :] * halo_ref[pl.ds(SUBLANES - (CONV_W - 1) + j, tm), :]
    halo_ref[0:SUBLANES, :] = p[tm - SUBLANES:, :]
    qk = acc * _sigmoid(acc)
    oa_ref[:, C_QK:C_QK + M_W] = qk[:, :M_W].astype(BF16)
    oa_ref[:, C_QK + M_W:C_QK + 2 * M_W] = (qk[:, M_W:] * (M_HD ** -0.5)).astype(BF16)

    p = jnp.dot(u, w_ref[:, C_VO:C_VO + 2 * M_W], preferred_element_type=F32)
    oa_ref[:, C_VO:C_VO + 2 * M_W] = p.astype(BF16)

    p = jnp.dot(u, w_ref[:, C_GQK:C_GQK + G_KW], preferred_element_type=F32)
    oa_ref[:, C_GQK:C_GQK + G_KW] = (p * (G_DK ** -0.5)).astype(BF16)
    p = jnp.dot(u, w_ref[:, C_GQK + G_KW:C_SMALL], preferred_element_type=F32)
    oa_ref[:, C_GQK + G_KW:C_SMALL] = p.astype(BF16)

    ps = jnp.dot(u, w_ref[:, C_SMALL:C_TOT], preferred_element_type=F32)
    la = jnp.dot(ps.astype(BF16), wa_ref[...], preferred_element_type=F32) + ba_ref[...]
    la_ref[...] = _log_sigmoid(la) * (1.0 / G_TAU)
    pt = ps.T
    gi = pt[SM_I:SM_I + SUBLANES, :] + bg_ref[0:SUBLANES, :]
    gf = _log_sigmoid(pt[SM_F:SM_F + SUBLANES, :] + bg_ref[SUBLANES:2 * SUBLANES, :])
    for j in range(tm // lm):
        g_ref[j, 0:SUBLANES, :] = gi[:, j * lm:(j + 1) * lm]
        g_ref[j, SUBLANES:2 * SUBLANES, :] = gf[:, j * lm:(j + 1) * lm]


def _inproj(x2, mod3, w_in, w_conv, b_conv, wa_pad, b_gla, bg, *, S, tm, lm):
    N, D = x2.shape
    tpb = S // tm
    kern = functools.partial(_inproj_kernel, tm=tm, tpb=tpb, lm=lm)
    return pl.pallas_call(
        kern,
        grid=(N // tm,),
        in_specs=[pl.BlockSpec((tm, D), lambda i: (i, 0)),
                  pl.BlockSpec((1, 6, D), lambda i: (i // tpb, 0, 0)),
                  pl.BlockSpec((D, IN_TOT), lambda i: (0, 0), pipeline_mode=pl.Buffered(1)),
                  pl.BlockSpec((CONV_W, 2 * M_W), lambda i: (0, 0)),
                  pl.BlockSpec((1, 2 * M_W), lambda i: (0, 0)),
                  pl.BlockSpec((LANES, G_KW), lambda i: (0, 0)),
                  pl.BlockSpec((1, G_KW), lambda i: (0, 0)),
                  pl.BlockSpec((2 * SUBLANES, 1), lambda i: (0, 0))],
        out_specs=[pl.BlockSpec((tm, C_SMALL), lambda i: (i, 0)),
                   pl.BlockSpec((tm, G_KW), lambda i: (i, 0)),
                   pl.BlockSpec((tm // lm, 2 * SUBLANES, lm), lambda i: (i, 0, 0))],
        out_shape=[jax.ShapeDtypeStruct((N, C_SMALL), BF16),
                   jax.ShapeDtypeStruct((N, G_KW), F32),
                   jax.ShapeDtypeStruct((N // lm, 2 * SUBLANES, lm), F32)],
        scratch_shapes=[pltpu.VMEM((SUBLANES + tm, 2 * M_W), F32), pltpu.VMEM((D, C_TOT), BF16)],
        compiler_params=_cparams(),
        name="inproj",
    )(x2, mod3, w_in, w_conv, b_conv, wa_pad, b_gla, bg)


def _mlstm_kernel(qk_ref, vo_ref, g_ref, u_ref, gain_ref, out_ref, c_ref, zt_ref, a_ref, dec_ref, m_ref,
                  *, L, NC, nb):
    @pl.when(pl.program_id(1) == 0)
    def _():
        c_ref[...] = jnp.zeros_like(c_ref)
        m_ref[...] = jnp.zeros_like(m_ref)

    lane = lax.broadcasted_iota(jnp.int32, (SUBLANES, L), 1)
    tril = (lax.broadcasted_iota(jnp.int32, (L, L), 0) >= lax.broadcasted_iota(jnp.int32, (L, L), 1))
    ones_v = jnp.ones((L, M_HD), BF16)
    zpad = jnp.zeros((LANES - 4 * SUBLANES, L), F32)

    for bi in range(nb):
        bs, gs = [], []
        for c in range(NC):
            ci = bi * NC + c
            b = jnp.dot(g_ref[bi, c, SUBLANES:2 * SUBLANES, :], u_ref[...], preferred_element_type=F32,
                        precision=HIGHEST)
            a = g_ref[bi, c, 0:SUBLANES, :] - b
            a_ref[ci] = a
            G = a
            s = 1
            while s < L:
                G = jnp.maximum(G, jnp.where(lane >= s, pltpu.roll(G, s, 1), -jnp.inf))
                s *= 2
            bs.append(b)
            gs.append(G)
        m_prev = m_ref[bi][:, 0:1]
        for c in range(NC):
            ci = bi * NC + c
            M = jnp.maximum(gs[c], m_prev)
            ML = M[:, L - 1:L]
            Z = jnp.concatenate([M, jnp.exp(m_prev - M), jnp.exp(-(bs[c] + M)), jnp.exp(a_ref[ci] - ML), zpad],
                                axis=0)
            zt_ref[ci] = Z.T
            dec_ref[ci] = jnp.broadcast_to(jnp.exp(m_prev - ML), (SUBLANES, 2 * M_HD))
            m_prev = bs[c][:, L - 1:L] + ML
        m_ref[bi] = jnp.broadcast_to(m_prev, (SUBLANES, LANES))

    chains =[(bi, h) for bi in range(nb) for h in range(M_HEADS)]
    nt = (((1,), (1,)), ((), ()))
    tn = (((0,), (0,)), ((), ()))

    def chunk(c, carry):
        rows = pl.ds(pl.multiple_of(c * L, L), L)
        Zt = [zt_ref[bi * NC + c] for bi in range(nb)]
        a = [a_ref[bi * NC + c] for bi in range(nb)]
        dec = [dec_ref[bi * NC + c] for bi in range(nb)]
        hs = [slice(h * M_HD, (h + 1) * M_HD) for h in range(M_HEADS)]
        hs2 = [slice(M_W + h * M_HD, M_W + (h + 1) * M_HD) for h in range(M_HEADS)]
        q = [qk_ref[bi, rows, hs[h]] for bi, h in chains]
        k = [qk_ref[bi, rows, hs2[h]] for bi, h in chains]
        vext = [jnp.concatenate([vo_ref[bi, rows, hs[h]], ones_v], axis=1) for bi, h in chains]
        cst = [c_ref[bi * M_HEADS + h] for bi, h in chains]
        n = range(len(chains))
        sc = [lax.dot_general(q[i], k[i], nt, preferred_element_type=F32) for i in n]
        qc = [jnp.dot(q[i], cst[i].astype(BF16), preferred_element_type=F32) for i in n]
        pm = [(sc[i] * jnp.exp(jnp.where(tril, a[bi][h:h + 1, :] - Zt[bi][:, h:h + 1], -jnp.inf))).astype(BF16)
              for i, (bi, h) in enumerate(chains)]
        pv = [jnp.dot(pm[i], vext[i], preferred_element_type=F32) for i in n]
        kw = [(Zt[bi][:, 3 * SUBLANES + h:3 * SUBLANES + h + 1] * k[i].astype(F32)).astype(BF16)
              for i, (bi, h) in enumerate(chains)]
        upd = [lax.dot_general(kw[i], vext[i], tn, preferred_element_type=F32) for i in n]
        for i, (bi, h) in enumerate(chains):
            c_ref[bi * M_HEADS + h] = dec[bi][h:h + 1, :] * cst[i] + upd[i]
            nd = pv[i] + Zt[bi][:, SUBLANES + h:SUBLANES + h + 1] * qc[i]
            hh = nd[:, :M_HD] / jnp.maximum(jnp.abs(nd[:, M_HD:]),
                                            Zt[bi][:, 2 * SUBLANES + h:2 * SUBLANES + h + 1])
            hh = _sigmoid(vo_ref[bi, rows, hs2[h]].astype(F32)) * hh
            hn = hh * lax.rsqrt(jnp.mean(hh * hh, axis=-1, keepdims=True) + LN_EPS)
            out_ref[bi, rows, hs[h]] = (hn * gain_ref[:, hs[h]]).astype(BF16)
        return carry

    lax.fori_loop(0, NC, chunk, 0)


def _mlstm(oa, g3, u_tri, gain, *, B, S, L, nb, ts):
    N = oa.shape[0]
    NC = ts // L
    oa3 = oa.reshape(B, S, oa.shape[1])
    g4 = g3.reshape(B, S // L, 2 * SUBLANES, L)
    kern = functools.partial(_mlstm_kernel, L=L, NC=NC, nb=nb)
    out = pl.pallas_call(
        kern,
        grid=(B // nb, S // ts),
        in_specs=[pl.BlockSpec((nb, ts, 2 * M_W), lambda b, t: (b, t, C_QK // (2 * M_W))),
                  pl.BlockSpec((nb, ts, 2 * M_W), lambda b, t: (b, t, C_VO // (2 * M_W))),
                  pl.BlockSpec((nb, NC, 2 * SUBLANES, L), lambda b, t: (b, t, 0, 0)),
                  pl.BlockSpec((L, L), lambda b, t: (0, 0)),
                  pl.BlockSpec((1, M_W), lambda b, t: (0, 0))],
        out_specs=pl.BlockSpec((nb, ts, M_W), lambda b, t: (b, t, 0)),
        out_shape=jax.ShapeDtypeStruct((B, S, M_W), BF16),
        scratch_shapes=[pltpu.VMEM((nb * M_HEADS, M_HD, 2 * M_HD), F32),
                        pltpu.VMEM((nb * NC, L, LANES), F32),
                        pltpu.VMEM((nb * NC, SUBLANES, L), F32),
                        pltpu.VMEM((nb * NC, SUBLANES, 2 * M_HD), F32),
                        pltpu.VMEM((nb, SUBLANES, LANES), F32)],
        compiler_params=_cparams(2),
        name="mlstm",
    )(oa3, oa3, g4, u_tri, gain)
    return out.reshape(N, M_W)


_G_LEVELS = 6
_G_XROW = 2 * G_CHUNK + SUBLANES


def _gla_consts():
    L = G_CHUNK
    t = np.arange(L)
    blocks = [(t[None, :] <= t[:, None]).astype(np.float32),
              (t[None, :] > t[:, None]).astype(np.float32),
              np.ones((SUBLANES, L), np.float32)]
    masks = [np.eye(L, dtype=np.float32)]
    m = 1
    while m < L:
        wl = np.zeros((L, L), np.float32)
        for r in range(L):
            r0 = (r // (2 * m)) * 2 * m + m
            if r % (2 * m) >= m:
                wl[r, r0:r + 1] = 1.0
            else:
                wl[r, r + 1:r0] = 1.0
        blocks.append(wl)
        tt, ss = t[:, None], t[None, :]
        masks.append(((tt // (2 * m) == ss // (2 * m)) & (tt % (2 * m) >= m)
                      & (ss % (2 * m) < m)).astype(np.float32))
        m *= 2
    w = np.concatenate(blocks, axis=0)
    w3 = np.concatenate([w, w, w], axis=1)
    mk = np.stack([np.concatenate([x] * G_HEADS, axis=0) for x in masks])
    return w3, mk


def _gla_kernel(qk_ref, v_ref, gg_ref, la_ref, w3_ref, mk_ref, gain_ref, out_ref, st_ref, *, NC, nb):
    L = G_CHUNK

    @pl.when(pl.program_id(1) == 0)
    def _():
        st_ref[...] = jnp.zeros_like(st_ref)

    lane_head = lax.broadcasted_iota(jnp.int32, (L, G_KW), 1) // G_DK
    br = lax.broadcasted_iota(jnp.int32, (2 * G_DV, LANES), 0) < G_DV
    bl = lax.broadcasted_iota(jnp.int32, (2 * G_DV, LANES), 1) < G_DK
    bmask = br == bl
    nt = (((1,), (1,)), ((), ()))
    tn = (((0,), (0,)), ((), ()))

    def chunk(c, carry):
        rows = pl.ds(pl.multiple_of(c * L, L), L)
        X, q, k = [], [], []
        for bi in range(nb):
            la = la_ref[bi, rows, :]
            hi = la.astype(BF16)
            r1 = la - hi.astype(F32)
            mid = r1.astype(BF16)
            lo = (r1 - mid.astype(F32)).astype(BF16)
            stk = jnp.concatenate([hi, mid, lo], axis=0)
            X.append(jnp.exp(jnp.dot(w3_ref[...], stk, preferred_element_type=F32)))
            q.append(qk_ref[bi, rows, 0:G_KW].astype(F32))
            k.append(qk_ref[bi, rows, G_KW:2 * G_KW].astype(F32))

        sc = [[None] * (_G_LEVELS + 1) for _ in range(nb)]
        for lev in range(_G_LEVELS + 1):
            for bi in range(nb):
                if lev == 0:
                    qt, kt = q[bi], k[bi]
                else:
                    xl = X[bi][_G_XROW + L * (lev - 1):_G_XROW + L * lev, :]
                    qt, kt = q[bi] * xl, k[bi] * xl
                q4 = jnp.concatenate([jnp.where(lane_head == h, qt, 0.0) for h in range(G_HEADS)],
                                     axis=0).astype(BF16)
                sc[bi][lev] = lax.dot_general(q4, kt.astype(BF16), nt, preferred_element_type=F32)
        Ab = []
        for bi in range(nb):
            A = sc[bi][0] * mk_ref[0]
            for lev in range(1, _G_LEVELS + 1):
                A = A + sc[bi][lev] * mk_ref[lev]
            Ab.append(A.astype(BF16))

        for bi in range(nb):
            gg = gg_ref[bi, rows, :].astype(F32)
            gate = gg * _sigmoid(gg)
            for p in range(2):
                ls = slice(LANES * p, LANES * (p + 1))
                vp = v_ref[bi, rows, 2 * G_DV * p:2 * G_DV * (p + 1)]
                oi = [jnp.dot(Ab[bi][L * (2 * p + hh):L * (2 * p + hh + 1)],
                              vp[:, G_DV * hh:G_DV * (hh + 1)], preferred_element_type=F32)
                      for hh in range(2)]
                st = st_ref[bi, p]
                qc = (q[bi][:, ls] * X[bi][0:L, ls]).astype(BF16)
                o_inter = lax.dot_general(qc, st.astype(BF16), nt, preferred_element_type=F32)
                kc = (k[bi][:, ls] * X[bi][L:2 * L, ls]).astype(BF16)
                upd = lax.dot_general(vp, kc, tn, preferred_element_type=F32)
                dec = X[bi][2 * L:2 * L + 1, ls]
                st_ref[bi, p] = jnp.where(bmask, dec * st + upd, 0.0)
                for hh in range(2):
                    o = o_inter[:, G_DV * hh:G_DV * (hh + 1)] + oi[hh]
                    hn = o * lax.rsqrt(jnp.mean(o * o, axis=-1, keepdims=True) + LN_EPS)
                    hs = slice(G_DV * (2 * p + hh), G_DV * (2 * p + hh + 1))
                    out_ref[bi, rows, hs] = (hn * gain_ref[:, hs] * gate[:, hs]).astype(BF16)
        return carry

    lax.fori_loop(0, NC, chunk, 0)


def _gla(oa, la, w3, mk, gain, *, B, S, nb, ts):
    N = oa.shape[0]
    oa3 = oa.reshape(B, S, oa.shape[1])
    la3 = la.reshape(B, S, G_KW)
    kern = functools.partial(_gla_kernel, NC=ts // G_CHUNK, nb=nb)
    out = pl.pallas_call(
        kern,
        grid=(B // nb, S // ts),
        in_specs=[pl.BlockSpec((nb, ts, 2 * G_KW), lambda b, t: (b, t, C_GQK // (2 * G_KW))),
                  pl.BlockSpec((nb, ts, G_W), lambda b, t: (b, t, C_GV // G_W)),
                  pl.BlockSpec((nb, ts, G_W), lambda b, t: (b, t, C_GG // G_W)),
                  pl.BlockSpec((nb, ts, G_KW), lambda b, t: (b, t, 0)),
                  pl.BlockSpec(w3.shape, lambda b, t: (0, 0)),
                  pl.BlockSpec(mk.shape, lambda b, t: (0, 0, 0)),
                  pl.BlockSpec((1, G_W), lambda b, t: (0, 0))],
        out_specs=pl.BlockSpec((nb, ts, G_W), lambda b, t: (b, t, 0)),
        out_shape=jax.ShapeDtypeStruct((B, S, G_W), BF16),
        scratch_shapes=[pltpu.VMEM((nb, 2, 2 * G_DV, LANES), F32)],
        compiler_params=_cparams(2),
        name="gla",
    )(oa3, oa3, oa3, la3, w3, mk, gain)
    return out.reshape(N, G_W)


def _layer_norm(z, g, b):
    mu = jnp.mean(z, axis=-1, keepdims=True)
    zc = z - mu
    var = jnp.mean(zc * zc, axis=-1, keepdims=True)
    return zc * lax.rsqrt(var + LN_EPS) * g + b


def _outproj_kernel(hm_ref, hg_ref, wf_ref, x_ref, mod_ref, g_ref, b_ref, wr_ref, br_ref,
                    x1_ref, u2_ref, rrow_ref, w_ref, *, tb, nh):
    @pl.when(pl.program_id(0) == 0)
    def _():
        w_ref[...] = wf_ref[...].astype(BF16)

    mod = mod_ref[0]
    blocks = [slice(tb * j, tb * (j + 1)) for j in range(nh)]
    y = [jnp.dot(hm_ref[r, :], w_ref[0:M_W, :], preferred_element_type=F32)
         + jnp.dot(hg_ref[r, :], w_ref[M_W:M_W + G_W, :], preferred_element_type=F32) for r in blocks]
    u2 = []
    for j, r in enumerate(blocks):
        z = ALPHA * x_ref[r, :] + (1.0 + mod[2:3, :]) * y[j]
        x1 = _layer_norm(z, g_ref[...], b_ref[...])
        x1_ref[r, :] = x1
        u2.append(x1 * (1.0 + mod[4:5, :]) + mod[3:4, :])
        u2_ref[r, :] = u2[j].astype(BF16)

    u2h = [u.astype(BF16) for u in u2]
    u2l = [(u2[j] - u2h[j].astype(F32)).astype(BF16) for j in range(nh)]
    lh = [jnp.dot(u, wr_ref[...], preferred_element_type=F32) for u in u2h]
    ll = [jnp.dot(u, wr_ref[:, 0:LANES], preferred_element_type=F32) for u in u2l]
    for j in range(nh):
        logits = lh[j][:, 0:LANES] + lh[j][:, LANES:2 * LANES] + ll[j] + br_ref[...]
        rrow_ref[j] = _route_select(logits.T, tb)


def _route_select(lt, tm):
    row = lax.broadcasted_iota(jnp.int32, (SUBLANES, tm), 0)
    gl = jnp.where(row < N_GROUPS, lt[0:SUBLANES, :], -jnp.inf)
    gmax = jnp.max(gl, axis=0, keepdims=True)
    gsel = jnp.min(jnp.where(gl == gmax, row, SUBLANES), axis=0, keepdims=True)
    pg = 1.0 / jnp.sum(jnp.exp(gl - gmax), axis=0, keepdims=True)
    ein = jnp.zeros((SUBLANES, tm), F32)
    for g in range(N_GROUPS):
        ein = jnp.where(gsel == g, lt[SUBLANES * (g + 1):SUBLANES * (g + 2), :], ein)
    v1 = jnp.max(ein, axis=0, keepdims=True)
    i1 = jnp.min(jnp.where(ein == v1, row, SUBLANES), axis=0, keepdims=True)
    rest = jnp.where(row == i1, -jnp.inf, ein)
    v2 = jnp.max(rest, axis=0, keepdims=True)
    i2 = jnp.min(jnp.where(rest == v2, row, SUBLANES), axis=0, keepdims=True)
    t2 = jnp.exp(v2 - v1)
    p1 = 1.0 / (1.0 + t2)
    e0 = (gsel * E_PER_G + i1).astype(F32)
    e1 = (gsel * E_PER_G + i2).astype(F32)
    return jnp.concatenate([e0, e1, pg * p1, pg * (t2 * p1), jnp.zeros((SUBLANES - 4, tm), F32)], axis=0)


def _outproj(hm, hg, w_out, x2, mod3, g, b, wr, br, *, S, tb, nh):
    N, D = x2.shape
    tm = tb * nh
    tpb = S // tm
    kern = functools.partial(_outproj_kernel, tb=tb, nh=nh)
    return pl.pallas_call(
        kern,
        grid=(N // tm,),
        in_specs=[pl.BlockSpec((tm, M_W), lambda i: (i, 0)),
                  pl.BlockSpec((tm, G_W), lambda i: (i, 0)),
                  pl.BlockSpec((M_W + G_W, D), lambda i: (0, 0), pipeline_mode=pl.Buffered(1)),
                  pl.BlockSpec((tm, D), lambda i: (i, 0)),
                  pl.BlockSpec((1, 6, D), lambda i: (i // tpb, 0, 0)),
                  pl.BlockSpec((1, D), lambda i: (0, 0)),
                  pl.BlockSpec((1, D), lambda i: (0, 0)),
                  pl.BlockSpec((D, 2 * LANES), lambda i: (0, 0)),
                  pl.BlockSpec((1, LANES), lambda i: (0, 0))],
        out_specs=[pl.BlockSpec((tm, D), lambda i: (i, 0)),
                   pl.BlockSpec((tm, D), lambda i: (i, 0)),
                   pl.BlockSpec((nh, SUBLANES, tb), lambda i: (i, 0, 0))],
        out_shape=[jax.ShapeDtypeStruct((N, D), F32),
                   jax.ShapeDtypeStruct((N, D), BF16),
                   jax.ShapeDtypeStruct((N // tb, SUBLANES, tb), F32)],
        scratch_shapes=[pltpu.VMEM((M_W + G_W, D), BF16)],
        compiler_params=_cparams(),
        name="outproj",
    )(hm, hg, w_out, x2, mod3, g, b, wr, br)


def _slots_per_tile(tb):
    worst = 2 * tb + N_EXP * (GRAN - 1)
    return -(-worst // LANES) * LANES


def _ffn_tiles(n_tok, tb):
    worst_rows = 2 * n_tok + (n_tok // tb) * N_EXP * (GRAN - 1)
    return -(-worst_rows // FFN_TM) + N_EXP


def _route_kernel(rr_ref, u_ref, lt_ref, srow_ref, col_ref, gd_ref, meta_ref, mg_ref, part_ref,
                  *, NT, tb, TM):
    iota_e = lax.broadcasted_iota(jnp.int32, (N_EXP, tb), 0).astype(F32)
    glane = lax.broadcasted_iota(jnp.int32, (N_EXP, LANES), 1).astype(F32)
    ltri = lt_ref[...]

    def prefix_e(col):
        return jnp.dot(ltri, jnp.broadcast_to(col, (N_EXP, LANES)),
                       preferred_element_type=F32, precision=HIGHEST)[:, 0:1]

    def p1(j, run8):
        r = rr_ref[j]
        oh0 = jnp.where(iota_e == r[0:1, :], 1.0, 0.0)
        oh1 = jnp.where(iota_e == r[1:2, :], 1.0, 0.0)
        cum0 = jnp.dot(oh0.astype(BF16), u_ref[...], preferred_element_type=F32)
        cum1 = jnp.dot(oh1.astype(BF16), u_ref[...], preferred_element_type=F32)
        c0 = jnp.sum(oh0, axis=1, keepdims=True)
        n8 = jnp.floor((c0 + jnp.sum(oh1, axis=1, keepdims=True) + (GRAN - 1.0)) * (1.0 / GRAN))
        lo8 = prefix_e(n8)
        s0 = jnp.sum(oh0 * (GRAN * lo8 + cum0 - 1.0), axis=0, keepdims=True)
        s1 = jnp.sum(oh1 * (GRAN * lo8 + c0 + cum1 - 1.0), axis=0, keepdims=True)
        info = jnp.concatenate([s0, s1, r[2:4, :], jnp.zeros((SUBLANES - 4, tb), F32)], axis=0)
        srow_ref[j] = info
        col_ref[pl.ds(pl.multiple_of(j * tb, tb), tb), :] = jnp.concatenate(
            [info, jnp.zeros((LANES - SUBLANES, tb), F32)], axis=0).T
        mg = jnp.where((lo8 <= glane) & (glane < lo8 + n8), 1.0, 0.0)
        mg_ref[j] = mg
        part = jnp.sum(mg * (run8 + glane - lo8), axis=0, keepdims=True)
        gcnt = jnp.broadcast_to(jnp.sum(n8, axis=0, keepdims=True), (1, LANES))
        part_ref[j] = jnp.concatenate([part, gcnt, jnp.zeros((SUBLANES - 2, LANES), F32)], axis=0)
        return run8 + n8

    tot8 = lax.fori_loop(0, NT, p1, jnp.zeros((N_EXP, 1), F32), unroll=4 if NT % 4 == 0 else 1)
    seg_t = jnp.floor((tot8 * GRAN + (TM - 1.0)) * (1.0 / TM))
    base_t = prefix_e(seg_t)
    base8 = base_t * (TM // GRAN)
    lane1 = lax.broadcasted_iota(jnp.int32, (1, LANES), 1)

    def p2(j, carry):
        pr = part_ref[j]
        dst = (pr[0:1, :] + jnp.sum(mg_ref[j] * base8, axis=0, keepdims=True)) * GRAN
        gd_ref[j] = jnp.where(lane1 == G_LAST, pr[1:2, :], dst).astype(jnp.int32)
        return carry

    lax.fori_loop(0, NT, p2, 0, unroll=4 if NT % 4 == 0 else 1)
    eye = jnp.where(glane == lax.broadcasted_iota(jnp.int32, (N_EXP, LANES), 0).astype(F32), 1.0, 0.0)
    tail_row = jnp.sum(eye * ((base8 + tot8) * GRAN), axis=0, keepdims=True)
    tail_n8 = jnp.sum(eye * (seg_t * (TM // GRAN) - tot8), axis=0, keepdims=True)
    nv_l = jnp.broadcast_to(jnp.sum(seg_t, axis=0, keepdims=True), (1, LANES))
    gd_ref[NT] = jnp.where(lane1 == G_LAST, nv_l, tail_row).astype(jnp.int32)
    gd_ref[NT + 1] = tail_n8.astype(jnp.int32)
    ti = lax.broadcasted_iota(jnp.int32, (N_EXP, tb), 1).astype(F32)
    te = jnp.sum(jnp.where(base_t <= ti, 1.0, 0.0), axis=0, keepdims=True) - 1.0
    nv = jnp.broadcast_to(jnp.sum(seg_t, axis=0, keepdims=True), (1, tb))
    meta_ref[...] = jnp.concatenate([te, nv, jnp.zeros((SUBLANES - 2, tb), F32)],
                                    axis=0).astype(jnp.int32)


def _route(rrow, u_cnt, ltri, *, TM):
    NT, _, tb = rrow.shape
    kern = functools.partial(_route_kernel, NT=NT, tb=tb, TM=TM)
    full3 = lambda i: (0, 0, 0)
    return pl.pallas_call(
        kern,
        grid=(1,),
        in_specs=[pl.BlockSpec((NT, SUBLANES, tb), full3),
                  pl.BlockSpec((tb, tb), lambda i: (0, 0)),
                  pl.BlockSpec((N_EXP, N_EXP), lambda i: (0, 0))],
        out_specs=[pl.BlockSpec((NT, SUBLANES, tb), full3),
                   pl.BlockSpec((NT * tb, LANES), lambda i: (0, 0)),
                   pl.BlockSpec((NT + 2, 1, LANES), full3),
                   pl.BlockSpec((SUBLANES, tb), lambda i: (0, 0))],
        out_shape=[jax.ShapeDtypeStruct((NT, SUBLANES, tb), F32),
                   jax.ShapeDtypeStruct((NT * tb, LANES), F32),
                   jax.ShapeDtypeStruct((NT + 2, 1, LANES), jnp.int32),
                   jax.ShapeDtypeStruct((SUBLANES, tb), jnp.int32)],
        scratch_shapes=[pltpu.VMEM((NT, N_EXP, LANES), F32), pltpu.VMEM((NT, SUBLANES, LANES), F32)],
        compiler_params=_cparams(),
        name="route",
    )(rrow, u_cnt, ltri)


U32 = jnp.uint32
_HI_MASK = 0xFFFF0000


def _pack_halves(x):
    c = x.shape[1] // 2
    lo = lax.bitcast_convert_type(x[:, :c], U32)
    hi = lax.bitcast_convert_type(x[:, c:], U32)
    return (lo >> 16) | (hi & U32(_HI_MASK))


def _unpack_halves(w):
    lo = lax.bitcast_convert_type(w << 16, F32)
    hi = lax.bitcast_convert_type(w & U32(_HI_MASK), F32)
    return jnp.concatenate([lo, hi], axis=1).astype(BF16)


def _granule_copy(src_ref, src_row, dst_ref, dst_row, sem):
    return pltpu.make_async_copy(src_ref.at[pl.ds(src_row, GRAN), :], dst_ref.at[pl.ds(dst_row, GRAN), :], sem)


def _for_granules(n, body, unroll=4):
    def blk(i, carry):
        for t in range(unroll):
            body(i * unroll + t)
        return carry

    def one(g, carry):
        body(g)
        return carry

    nblk = n // unroll
    lax.fori_loop(0, nblk, blk, 0)
    lax.fori_loop(nblk * unroll, n, one, 0)


def _wait_granules(n, src_ref, dst_ref, sem, n_max):
    b = 1
    while b <= n_max:
        @pl.when((n & b) != 0)
        def _(b=b):
            pltpu.make_async_copy(src_ref.at[pl.ds(0, b * GRAN), :], dst_ref.at[pl.ds(0, b * GRAN), :],
                                  sem).wait()
        b *= 2


def _dispatch_kernel(gd_ref, srow_ref, u_ref, xs_ref, buf, zbuf, sems, *, NT, SL, TM, n_tiles):
    j = pl.program_id(0)
    slot = j % 2
    zsem = sems.at[2]

    def drain(tile, sl):
        _wait_granules(gd_ref[tile, G_LAST], buf.at[sl], xs_ref, sems.at[sl], SL // GRAN)

    def tile_fill(t):
        return pltpu.make_async_copy(zbuf, xs_ref.at[pl.ds(pl.multiple_of(t * TM, TM), TM), :], zsem)

    def zero_fill(wait):
        for e in range(N_EXP):
            n, row0 = gd_ref[NT + 1, e], gd_ref[NT, e]
            b = TM // GRAN // 2
            while b >= 1:
                @pl.when((n & b) != 0)
                def _(b=b, n=n, row0=row0):
                    start = pl.multiple_of(row0 + ((n >> b.bit_length()) << b.bit_length()) * GRAN, GRAN)
                    cp = pltpu.make_async_copy(zbuf.at[pl.ds(0, b * GRAN), :],
                                               xs_ref.at[pl.ds(start, b * GRAN), :], zsem)
                    cp.wait() if wait else cp.start()
                b //= 2

        def zt(t, carry):
            tile_fill(t).wait() if wait else tile_fill(t).start()
            return carry
        lax.fori_loop(gd_ref[NT, G_LAST], n_tiles, zt, 0)

    @pl.when(j == 0)
    def _():
        zbuf[...] = jnp.zeros_like(zbuf)
        zero_fill(False)

    @pl.when(j >= 2)
    def _():
        drain(j - 2, slot)

    s = srow_ref[0]
    rows = lax.broadcasted_iota(jnp.int32, (SL, s.shape[1]), 0).astype(F32)
    m0 = rows == s[0:1, :]
    m1 = rows == s[1:2, :]
    oh = jnp.where(m0 | m1, 1.0, 0.0).astype(BF16)
    dw = u_ref.shape[1] // 2
    buf[slot, :, 0:dw] = _pack_halves(jnp.dot(oh, u_ref[...], preferred_element_type=F32))
    wrow = jnp.sum(jnp.where(m0, s[2:3, :], 0.0) + jnp.where(m1, s[3:4, :], 0.0), axis=1, keepdims=True)
    buf[slot, :, dw:dw + LANES] = lax.bitcast_convert_type(jnp.broadcast_to(wrow, (SL, LANES)), U32)

    def issue(g):
        _granule_copy(buf.at[slot], pl.multiple_of(g * GRAN, GRAN), xs_ref,
                      pl.multiple_of(gd_ref[j, g], GRAN), sems.at[slot]).start()

    _for_granules(gd_ref[j, G_LAST], issue)

    @pl.when(j == NT - 1)
    def _():
        drain(j, slot)
        if NT > 1:
            drain(j - 1, 1 - slot)
        zero_fill(True)


def _dispatch(gd, srow, u2, *, n_tiles, TM):
    N, D = u2.shape
    NT, _, tb = srow.shape
    SL = _slots_per_tile(tb)
    n_rows = n_tiles * TM
    kern = functools.partial(_dispatch_kernel, NT=NT, SL=SL, TM=TM, n_tiles=n_tiles)
    grid_spec = pltpu.PrefetchScalarGridSpec(
        num_scalar_prefetch=1,
        grid=(NT,),
        in_specs=[pl.BlockSpec((1, SUBLANES, tb), lambda j, gd: (j, 0, 0)),
                  pl.BlockSpec((tb, D), lambda j, gd: (j, 0))],
        out_specs=pl.BlockSpec(memory_space=pl.ANY),
        scratch_shapes=[pltpu.VMEM((2, SL, D // 2 + LANES), U32), pltpu.VMEM((TM, D // 2 + LANES), U32),
                        pltpu.SemaphoreType.DMA((3,))],
    )
    return pl.pallas_call(
        kern,
        grid_spec=grid_spec,
        out_shape=jax.ShapeDtypeStruct((n_rows, D // 2 + LANES), U32),
        compiler_params=_cparams(),
        name="dispatch",
    )(gd, srow, u2)


def _ffn_kernel(te_ref, nv_ref, xs_ref, wg_ref, wu_ref, wd_ref, o_ref, wgb, wub, wdb, sg, su, sd, slot_ref,
                sems):
    i = pl.program_id(0)
    nv = nv_ref[0]
    e = te_ref[i]

    def weight_copies(ex, sl):
        return (pltpu.make_async_copy(wg_ref.at[ex], sg.at[sl], sems.at[sl]),
                pltpu.make_async_copy(wu_ref.at[ex], su.at[sl], sems.at[sl]),
                pltpu.make_async_copy(wd_ref.at[ex], sd.at[sl], sems.at[sl]))

    @pl.when(i == 0)
    def _():
        slot_ref[0] = 0
        for cp in weight_copies(e, 0):
            cp.start()

    @pl.when((i < nv) & ((i == 0) | (e != te_ref[jnp.maximum(i - 1, 0)])))
    def _():
        sl = slot_ref[0]
        for cp in weight_copies(e, sl):
            cp.wait()
        wgb[...] = sg[sl].astype(BF16)
        wub[...] = su[sl].astype(BF16)
        wdb[...] = sd[sl].astype(BF16)
        nxt = lax.while_loop(lambda t: (t < nv) & (te_ref[jnp.minimum(t, nv - 1)] == e), lambda t: t + 1, i + 1)

        @pl.when(nxt < nv)
        def _():
            for cp in weight_copies(te_ref[nxt], 1 - sl):
                cp.start()
        slot_ref[0] = 1 - sl

    @pl.when(i < nv)
    def _():
        nsub = FFN_SUB
        hm = xs_ref.shape[0] // nsub
        dw = o_ref.shape[1]
        halves = tuple(slice(hm * j, hm * (j + 1)) for j in range(nsub))
        x = [_unpack_halves(xs_ref[r, 0:dw]) for r in halves]
        g = [jnp.dot(x[j], wgb[...], preferred_element_type=F32) for j in range(nsub)]
        u = [jnp.dot(x[j], wub[...], preferred_element_type=F32) for j in range(nsub)]
        h = [(g[j] * _sigmoid(g[j]) * u[j]).astype(BF16) for j in range(nsub)]
        y = [jnp.dot(h[j], wdb[...], preferred_element_type=F32) for j in range(nsub)]
        for j in range(nsub):
            wt = lax.bitcast_convert_type(xs_ref[halves[j], dw:dw + LANES], F32)
            yw = y[j] * jnp.concatenate([wt] * (2 * dw // LANES), axis=1)
            o_ref[halves[j], :] = _pack_halves(yw.astype(BF16).astype(F32))

    @pl.when(i >= nv_ref[0])
    def _():
        o_ref[...] = jnp.zeros_like(o_ref)


def _ffn(te, nv, xs, wg, wu, wd, *, TM):
    P, XW = xs.shape
    DW = XW - LANES
    D = 2 * DW
    n_tiles = P // TM
    grid_spec = pltpu.PrefetchScalarGridSpec(
        num_scalar_prefetch=2,
        grid=(n_tiles,),
        in_specs=[pl.BlockSpec((TM, XW), lambda i, te, nv: (jnp.maximum(jnp.minimum(i, nv[0] - 1), 0), 0)),
                  pl.BlockSpec(memory_space=pl.ANY),
                  pl.BlockSpec(memory_space=pl.ANY),
                  pl.BlockSpec(memory_space=pl.ANY)],
        out_specs=pl.BlockSpec((TM, DW), lambda i, te, nv: (i, 0)),
        scratch_shapes=[pltpu.VMEM((D, D_EXP), BF16), pltpu.VMEM((D, D_EXP), BF16),
                        pltpu.VMEM((D_EXP, D), BF16),
                        pltpu.VMEM((2, D, D_EXP), F32), pltpu.VMEM((2, D, D_EXP), F32),
                        pltpu.VMEM((2, D_EXP, D), F32), pltpu.SMEM((1,), jnp.int32),
                        pltpu.SemaphoreType.DMA((2,))],
    )
    return pl.pallas_call(
        _ffn_kernel,
        grid_spec=grid_spec,
        out_shape=jax.ShapeDtypeStruct((P, DW), U32),
        compiler_params=_cparams(),
        name="ffn",
    )(te, nv, xs, wg, wu, wd)


def _combine_kernel(gd_ref, ys_ref, col_ref, x1_ref, mod_ref, g_ref, b_ref, o_ref, buf, sems, *, NT, SL):
    j = pl.program_id(0)
    slot = j % 2

    def fetch(tile, sl):
        def f(g):
            _granule_copy(ys_ref, pl.multiple_of(gd_ref[tile, g], GRAN), buf.at[sl],
                          pl.multiple_of(g * GRAN, GRAN), sems.at[sl]).start()
        _for_granules(gd_ref[tile, G_LAST], f)

    @pl.when(j == 0)
    def _():
        fetch(0, 0)

    @pl.when(j + 1 < NT)
    def _():
        fetch(j + 1, 1 - slot)

    ng = gd_ref[j, G_LAST]

    _wait_granules(ng, ys_ref, buf.at[slot], sems.at[slot], SL // GRAN)

    rows = lax.broadcasted_iota(jnp.int32, (SL, 1), 0)
    yb = _unpack_halves(jnp.where(rows < ng * GRAN, buf[slot], U32(0)))
    col = col_ref[...]
    tb = col.shape[0]
    lanes = lax.broadcasted_iota(jnp.int32, (tb, SL), 1).astype(F32)
    sel = jnp.where((lanes == col[:, 0:1]) | (lanes == col[:, 1:2]), 1.0, 0.0).astype(BF16)
    y = jnp.dot(sel, yb, preferred_element_type=F32)
    mod = mod_ref[0]
    z = ALPHA * x1_ref[...] + (1.0 + mod[5:6, :]) * y
    o_ref[...] = _layer_norm(z, g_ref[...], b_ref[...])


def _combine(gd, ys, col, x1, mod3, g, b, *, S, tb):
    N, D = x1.shape
    NT = N // tb
    tpb = S // tb
    SL = _slots_per_tile(tb)
    kern = functools.partial(_combine_kernel, NT=NT, SL=SL)
    grid_spec = pltpu.PrefetchScalarGridSpec(
        num_scalar_prefetch=1,
        grid=(NT,),
        in_specs=[pl.BlockSpec(memory_space=pl.ANY),
                  pl.BlockSpec((tb, LANES), lambda j, gd: (j, 0)),
                  pl.BlockSpec((tb, D), lambda j, gd: (j, 0)),
                  pl.BlockSpec((1, 6, D), lambda j, gd: (j // tpb, 0, 0)),
                  pl.BlockSpec((1, D), lambda j, gd: (0, 0)),
                  pl.BlockSpec((1, D), lambda j, gd: (0, 0))],
        out_specs=pl.BlockSpec((tb, D), lambda j, gd: (j, 0)),
        scratch_shapes=[pltpu.VMEM((2, SL, D // 2), U32), pltpu.SemaphoreType.DMA((2,))],
    )
    return pl.pallas_call(
        kern,
        grid_spec=grid_spec,
        out_shape=jax.ShapeDtypeStruct((N, D), F32),
        compiler_params=_cparams(),
        name="combine",
    )(gd, ys, col, x1, mod3, g, b)


def _layer(x, c, l, w_ada, b_ada, w_in, w_conv, b_conv, b_igate, b_fgate, mlstm_norm_g, w_gla_a, b_gla_a,
           gla_norm_g, w_out, ln1_g, ln1_b, w_route_group, b_route_group, w_route_expert, b_route_expert,
           w_gate, w_up, w_down, ln2_g, ln2_b):
    B, S, D = x.shape
    N = B * S
    x2 = x.reshape(N, D)
    tm_in = min(512, S)
    tm = min(256, S)
    lm = min(256, S)

    mod3 = _ada(c, w_ada[l], b_ada[l]).reshape(B, 6, D)

    wa_pad = jnp.zeros((LANES, G_KW), F32).at[SM_A:SM_A + G_RANK].set(w_gla_a[l]).astype(BF16)
    bg = (jnp.zeros((2 * SUBLANES, 1), F32).at[0:M_HEADS, 0].set(b_igate[l])
          .at[SUBLANES:SUBLANES + M_HEADS, 0].set(b_fgate[l]))
    oa, la, g3 = _inproj(x2, mod3, w_in[l], w_conv[l], b_conv[l].reshape(1, -1), wa_pad,
                         b_gla_a[l].reshape(1, -1), bg, S=S, tm=tm_in, lm=lm)

    u_tri = jnp.asarray(np.triu(np.ones((lm, lm), np.float32)))
    nb = 4 if B % 4 == 0 else (2 if B % 2 == 0 else 1)
    ts = min(512, S)
    hm = _mlstm(oa, g3, u_tri, mlstm_norm_g[l].reshape(1, -1), B=B, S=S, L=lm, nb=nb, ts=ts)
    w3_np, mk_np = _gla_consts()
    hg = _gla(oa, la, jnp.asarray(w3_np, BF16), jnp.asarray(mk_np), gla_norm_g[l].reshape(1, -1), B=B, S=S,
              nb=nb, ts=ts)

    wr = (jnp.zeros((D, LANES), F32).at[:, 0:N_GROUPS].set(w_route_group[l])
          .at[:, SUBLANES:SUBLANES + N_EXP].set(w_route_expert[l]))
    br = (jnp.zeros((1, LANES), F32).at[0, 0:N_GROUPS].set(b_route_group[l])
          .at[0, SUBLANES:SUBLANES + N_EXP].set(b_route_expert[l]))
    wr_hi = wr.astype(BF16)
    wr2 = jnp.concatenate([wr_hi, (wr - wr_hi.astype(F32)).astype(BF16)], axis=1)
    x1, u2, rrow = _outproj(hm, hg, w_out[l], x2, mod3, ln1_g[l].reshape(1, -1),
                            ln1_b[l].reshape(1, -1), wr2, br, S=S, tb=tm, nh=4 if S % (4 * tm) == 0 else 1)

    u_cnt = jnp.asarray(np.triu(np.ones((tm, tm), np.float32)), BF16)
    ltri = jnp.asarray(np.tril(np.ones((N_EXP, N_EXP), np.float32), -1))
    srow, col, gd3, meta = _route(rrow, u_cnt, ltri, TM=FFN_TM)
    gd = gd3.reshape(N // tm + 2, LANES)
    n_tiles = _ffn_tiles(N, tm)
    te, nv = meta[0, :n_tiles], meta[1, 0:1]

    xs = _dispatch(gd, srow, u2, n_tiles=n_tiles, TM=FFN_TM)
    ys = _ffn(te, nv, xs, w_gate[l], w_up[l], w_down[l], TM=FFN_TM)
    out = _combine(gd, ys, col, x1, mod3, ln2_g[l].reshape(1, -1), ln2_b[l].reshape(1, -1), S=S, tb=tm)
    return out.reshape(B, S, D)


def kernel(x, c, w_ada, b_ada, w_in, w_conv, b_conv, b_igate, b_fgate, mlstm_norm_g, w_gla_a, b_gla_a,
           gla_norm_g, w_out, ln1_g, ln1_b, w_route_group, b_route_group, w_route_expert, b_route_expert,
           w_gate, w_up, w_down, ln2_g, ln2_b):
    for l in range(DEPTH):
        x = _layer(x, c, l, w_ada, b_ada, w_in, w_conv, b_conv, b_igate, b_fgate, mlstm_norm_g, w_gla_a,
                   b_gla_a, gla_norm_g, w_out, ln1_g, ln1_b, w_route_group, b_route_group, w_route_expert,
                   b_route_expert, w_gate, w_up, w_down, ln2_g, ln2_b)
    return x
```

```python
import functools

import numpy as np
import jax
import jax.numpy as jnp
from jax import lax
from jax.experimental import pallas as pl
from jax.experimental.pallas import tpu as pltpu

F32 = jnp.float32
BF16 = jnp.bfloat16
HIGHEST = lax.Precision.HIGHEST

DEPTH = 1
M_HEADS = 4
M_HD = 128
M_W = M_HEADS * M_HD
CONV_W = 4
G_HEADS = 4
G_DK = 64
G_DV = 128
G_W = G_HEADS * G_DV
G_KW = G_HEADS * G_DK
G_RANK = 16
G_TAU = 16.0
G_CHUNK = 64
N_GROUPS = 4
E_PER_G = 8
N_EXP = N_GROUPS * E_PER_G
D_EXP = 512
ALPHA = (2 * DEPTH) ** 0.25
LN_EPS = 1e-5

LANES = 128
SUBLANES = 8
VMEM_LIMIT = 48 * 1024 * 1024

C_QK = 0
C_VO = 1024
C_GQK = 2048
C_GV = 2560
C_GG = 3072
C_SMALL = 3584
C_TOT = 3712
SM_I, SM_F, SM_A = 0, 8, 16
IN_GATES = 4 * M_W
IN_G = IN_GATES + 2 * M_HEADS
IN_GA = IN_G + 2 * G_KW + 2 * G_W
IN_TOT = IN_GA + G_RANK

FFN_TM = 512
FFN_SUB = 2
GRAN = SUBLANES
G_LAST = LANES - 1


def _cparams(n_axes=1, vmem_limit=VMEM_LIMIT):
    return pltpu.CompilerParams(dimension_semantics=("arbitrary",) * n_axes,
                                vmem_limit_bytes=vmem_limit)


def _sigmoid(x):
    return 1.0 / (1.0 + jnp.exp(-x))


def _log_sigmoid(x):
    return jnp.minimum(x, 0.0) - jnp.log(1.0 + jnp.exp(-jnp.abs(x)))


def _ada_kernel(c_ref, w_ref, b_ref, o_ref):
    c = c_ref[...]
    ca = c * _sigmoid(c)
    o_ref[...] = jnp.dot(ca, w_ref[...], preferred_element_type=F32, precision=HIGHEST) + b_ref[...]


def _ada(c, w, b):
    B, D = c.shape
    n_out = w.shape[1]
    tn = 1024
    return pl.pallas_call(
        _ada_kernel,
        grid=(n_out // tn,),
        in_specs=[pl.BlockSpec((B, D), lambda j: (0, 0)),
                  pl.BlockSpec((D, tn), lambda j: (0, j)),
                  pl.BlockSpec((1, tn), lambda j: (0, j))],
        out_specs=pl.BlockSpec((B, tn), lambda j: (0, j)),
        out_shape=jax.ShapeDtypeStruct((B, n_out), F32),
        compiler_params=_cparams(),
        name="ada",
    )(c, w, b.reshape(1, n_out))


def _inproj_kernel(x_ref, mod_ref, win_ref, wc_ref, bc_ref, wa_ref, ba_ref, bg_ref,
                   oa_ref, la_ref, g_ref, halo_ref, w_ref, *, tm, tpb, lm):
    i = pl.program_id(0)

    @pl.when(i == 0)
    def _():
        rc = LANES
        for r in range(0, win_ref.shape[0], rc):
            rs = slice(r, r + rc)
            w_ref[rs, 0:IN_GATES] = win_ref[rs, 0:IN_GATES].astype(BF16)
            t = win_ref[rs, IN_GATES:IN_TOT]
            w_ref[rs, C_GQK:C_SMALL] = t[:, IN_G - IN_GATES:IN_GA - IN_GATES].astype(BF16)
            z = lambda n: jnp.zeros((rc, n), F32)
            small = jnp.concatenate([t[:, 0:M_HEADS], z(SM_F - M_HEADS), t[:, M_HEADS:2 * M_HEADS],
                                     z(SM_A - SM_F - M_HEADS), t[:, IN_GA - IN_GATES:IN_TOT - IN_GATES],
                                     z(LANES - SM_A - G_RANK)], axis=1)
            w_ref[rs, C_SMALL:C_TOT] = small.astype(BF16)

    @pl.when(i % tpb == 0)
    def _():
        halo_ref[0:SUBLANES, :] = jnp.zeros((SUBLANES, halo_ref.shape[1]), F32)

    mod = mod_ref[0]
    u = (x_ref[...] * (1.0 + mod[1:2, :]) + mod[0:1, :]).astype(BF16)

    p = jnp.dot(u, w_ref[:, C_QK:C_QK + 2 * M_W], preferred_element_type=F32)
    halo_ref[SUBLANES:SUBLANES + tm, :] = p
    acc = bc_ref[...] + wc_ref[CONV_W - 1:CONV_W, :] * p
    for j in range(CONV_W - 1):
        acc = acc + wc_ref[j:j + 1, :] * halo_ref[pl.ds(SUBLANES - (CONV_W - 1) + j, tm), :]
    halo_ref[0:SUBLANES, :] = p[tm - SUBLANES:, :]
    qk = acc * _sigmoid(acc)
    oa_ref[:, C_QK:C_QK + M_W] = qk[:, :M_W].astype(BF16)
    oa_ref[:, C_QK + M_W:C_QK + 2 * M_W] = (qk[:, M_W:] * (M_HD ** -0.5)).astype(BF16)

    p = jnp.dot(u, w_ref[:, C_VO:C_VO + 2 * M_W], preferred_element_type=F32)
    oa_ref[:, C_VO:C_VO + 2 * M_W] = p.astype(BF16)

    p = jnp.dot(u, w_ref[:, C_GQK:C_GQK + G_KW], preferred_element_type=F32)
    oa_ref[:, C_GQK:C_GQK + G_KW] = (p * (G_DK ** -0.5)).astype(BF16)
    p = jnp.dot(u, w_ref[:, C_GQK + G_KW:C_SMALL], preferred_element_type=F32)
    oa_ref[:, C_GQK + G_KW:C_SMALL] = p.astype(BF16)

    ps = jnp.dot(u, w_ref[:, C_SMALL:C_TOT], preferred_element_type=F32)
    la = jnp.dot(ps.astype(BF16), wa_ref[...], preferred_element_type=F32) + ba_ref[...]
    la_ref[...] = _log_sigmoid(la) * (1.0 / G_TAU)
    pt = ps.T
    gi = pt[SM_I:SM_I + SUBLANES, :] + bg_ref[0:SUBLANES, :]
    gf = _log_sigmoid(pt[SM_F:SM_F + SUBLANES, :] + bg_ref[SUBLANES:2 * SUBLANES, :])
    for j in range(tm // lm):
        g_ref[j, 0:SUBLANES, :] = gi[:, j * lm:(j + 1) * lm]
        g_ref[j, SUBLANES:2 * SUBLANES, :] = gf[:, j * lm:(j + 1) * lm]


def _inproj(x2, mod3, w_in, w_conv, b_conv, wa_pad, b_gla, bg, *, S, tm, lm):
    N, D = x2.shape
    tpb = S // tm
    kern = functools.partial(_inproj_kernel, tm=tm, tpb=tpb, lm=lm)
    return pl.pallas_call(
        kern,
        grid=(N // tm,),
        in_specs=[pl.BlockSpec((tm, D), lambda i: (i, 0)),
                  pl.BlockSpec((1, 6, D), lambda i: (i // tpb, 0, 0)),
                  pl.BlockSpec((D, IN_TOT), lambda i: (0, 0), pipeline_mode=pl.Buffered(1)),
                  pl.BlockSpec((CONV_W, 2 * M_W), lambda i: (0, 0)),
                  pl.BlockSpec((1, 2 * M_W), lambda i: (0, 0)),
                  pl.BlockSpec((LANES, G_KW), lambda i: (0, 0)),
                  pl.BlockSpec((1, G_KW), lambda i: (0, 0)),
                  pl.BlockSpec((2 * SUBLANES, 1), lambda i: (0, 0))],
        out_specs=[pl.BlockSpec((tm, C_SMALL), lambda i: (i, 0)),
                   pl.BlockSpec((tm, G_KW), lambda i: (i, 0)),
                   pl.BlockSpec((tm // lm, 2 * SUBLANES, lm), lambda i: (i, 0, 0))],
        out_shape=[jax.ShapeDtypeStruct((N, C_SMALL), BF16),
                   jax.ShapeDtypeStruct((N, G_KW), F32),
                   jax.ShapeDtypeStruct((N // lm, 2 * SUBLANES, lm), F32)],
        scratch_shapes=[pltpu.VMEM((SUBLANES + tm, 2 * M_W), F32), pltpu.VMEM((D, C_TOT), BF16)],
        compiler_params=_cparams(),
        name="inproj",
    )(x2, mod3, w_in, w_conv, b_conv, wa_pad, b_gla, bg)


def _mlstm_sel():
    sel = np.zeros((2 * LANES, 2 * M_HEADS * M_HD), np.float32)
    for j in range(2 * M_HEADS):
        src = (SUBLANES if j < M_HEADS else 3 * SUBLANES) + j % M_HEADS
        sel[src, M_HD * j:M_HD * (j + 1)] = 1.0
        sel[LANES + src, M_HD * j:M_HD * (j + 1)] = 1.0
    return sel


def _mlstm_kernel(qk_ref, vo_ref, g_ref, u_ref, gain_ref, sel_ref, out_ref, c_ref, zt_ref, a_ref, dec_ref, m_ref,
                  *, L, NC, nb):
    @pl.when(pl.program_id(1) == 0)
    def _():
        c_ref[...] = jnp.zeros_like(c_ref)
        m_ref[...] = jnp.zeros_like(m_ref)

    tril =(lax.broadcasted_iota(jnp.int32, (L, L), 0) >= lax.broadcasted_iota(jnp.int32, (L, L), 1))
    ones_v = jnp.ones((L, M_HD), BF16)
    zpad = jnp.zeros((LANES - 4 * SUBLANES, L), F32)

    order = [(bi, c) for bi in range(nb) for c in range(NC)]
    f_all = jnp.concatenate([g_ref[bi, c, SUBLANES:2 * SUBLANES, :] for bi, c in order], axis=0)
    i_all = jnp.concatenate([g_ref[bi, c, 0:SUBLANES, :] for bi, c in order], axis=0)
    b_all = jnp.dot(f_all, u_ref[...], preferred_element_type=F32, precision=HIGHEST)
    a_all = i_all - b_all
    lane_all = lax.broadcasted_iota(jnp.int32, a_all.shape, 1)
    g_all = a_all
    s = 1
    while s < L:
        g_all = jnp.maximum(g_all, jnp.where(lane_all >= s, pltpu.roll(g_all, s, 1), -jnp.inf))
        s *= 2
    for bi in range(nb):
        m_prev = m_ref[bi][:, 0:1]
        for c in range(NC):
            ci = bi * NC + c
            r8 = slice(SUBLANES * ci, SUBLANES * (ci + 1))
            a, b = a_all[r8], b_all[r8]
            a_ref[ci] = a
            M = jnp.maximum(g_all[r8], m_prev)
            ML = M[:, L - 1:L]
            Z = jnp.concatenate([M, jnp.exp(m_prev - M), jnp.exp(-(b + M)), jnp.exp(a - ML), zpad],
                                axis=0)
            zt_ref[ci] = Z.T
            dec_ref[ci] = jnp.broadcast_to(jnp.exp(m_prev - ML), (SUBLANES, 2 * M_HD))
            m_prev = b[:, L - 1:L] + ML
        m_ref[bi] = jnp.broadcast_to(m_prev, (SUBLANES, LANES))

    chains =[(bi, h) for bi in range(nb) for h in range(M_HEADS)]
    nt = (((1,), (1,)), ((), ()))
    tn = (((0,), (0,)), ((), ()))

    def chunk(c, carry):
        rows = pl.ds(pl.multiple_of(c * L, L), L)
        Zt = [zt_ref[bi * NC + c] for bi in range(nb)]
        a = [a_ref[bi * NC + c] for bi in range(nb)]
        dec = [dec_ref[bi * NC + c] for bi in range(nb)]
        hs = [slice(h * M_HD, (h + 1) * M_HD) for h in range(M_HEADS)]
        hs2 = [slice(M_W + h * M_HD, M_W + (h + 1) * M_HD) for h in range(M_HEADS)]
        q = [qk_ref[bi, rows, hs[h]] for bi, h in chains]
        k = [qk_ref[bi, rows, hs2[h]] for bi, h in chains]
        vext = [jnp.concatenate([vo_ref[bi, rows, hs[h]], ones_v], axis=1) for bi, h in chains]
        cst = [c_ref[bi * M_HEADS + h] for bi, h in chains]
        n = range(len(chains))
        sc = [lax.dot_general(q[i], k[i], nt, preferred_element_type=F32) for i in n]
        qc = [jnp.dot(q[i], cst[i].astype(BF16), preferred_element_type=F32) for i in n]
        pm = [(sc[i] * jnp.exp(jnp.where(tril, a[bi][h:h + 1, :] - Zt[bi][:, h:h + 1], -jnp.inf))).astype(BF16)
              for i, (bi, h) in enumerate(chains)]
        pv = [jnp.dot(pm[i], vext[i], preferred_element_type=F32) for i in n]
        rep = []
        for bi in range(nb):
            zh = Zt[bi].astype(BF16)
            zl = (Zt[bi] - zh.astype(F32)).astype(BF16)
            rep.append(jnp.dot(jnp.concatenate([zh, zl], axis=1), sel_ref[...], preferred_element_type=F32))
        e_inter = [rep[bi][:, M_HD * h:M_HD * (h + 1)] for bi, h in chains]
        w_state = [rep[bi][:, M_HD * (M_HEADS + h):M_HD * (M_HEADS + h + 1)] for bi, h in chains]
        kw = [(w_state[i] * k[i].astype(F32)).astype(BF16) for i in n]
        upd = [lax.dot_general(kw[i], vext[i], tn, preferred_element_type=F32) for i in n]
        for i, (bi, h) in enumerate(chains):
            c_ref[bi * M_HEADS + h] = dec[bi][h:h + 1, :] * cst[i] + upd[i]
            nd = pv[i] + jnp.concatenate([e_inter[i], e_inter[i]], axis=1) * qc[i]
            hh = nd[:, :M_HD] / jnp.maximum(jnp.abs(nd[:, M_HD:]),
                                            Zt[bi][:, 2 * SUBLANES + h:2 * SUBLANES + h + 1])
            hh = _sigmoid(vo_ref[bi, rows, hs2[h]].astype(F32)) * hh
            hn = hh * lax.rsqrt(jnp.mean(hh * hh, axis=-1, keepdims=True) + LN_EPS)
            out_ref[bi, rows, hs[h]] = (hn * gain_ref[:, hs[h]]).astype(BF16)
        return carry

    lax.fori_loop(0, NC, chunk, 0)


def _mlstm(oa, g3, u_tri, gain, *, B, S, L, nb, ts):
    N = oa.shape[0]
    NC = ts // L
    oa3 = oa.reshape(B, S, oa.shape[1])
    g4 = g3.reshape(B, S // L, 2 * SUBLANES, L)
    sel = jnp.asarray(_mlstm_sel(), BF16)
    kern = functools.partial(_mlstm_kernel, L=L, NC=NC, nb=nb)
    out = pl.pallas_call(
        kern,
        grid=(B // nb, S // ts),
        in_specs=[pl.BlockSpec((nb, ts, 2 * M_W), lambda b, t: (b, t, C_QK // (2 * M_W))),
                  pl.BlockSpec((nb, ts, 2 * M_W), lambda b, t: (b, t, C_VO // (2 * M_W))),
                  pl.BlockSpec((nb, NC, 2 * SUBLANES, L), lambda b, t: (b, t, 0, 0)),
                  pl.BlockSpec((L, L), lambda b, t: (0, 0)),
                  pl.BlockSpec((1, M_W), lambda b, t: (0, 0)),
                  pl.BlockSpec(sel.shape, lambda b, t: (0, 0))],
        out_specs=pl.BlockSpec((nb, ts, M_W), lambda b, t: (b, t, 0)),
        out_shape=jax.ShapeDtypeStruct((B, S, M_W), BF16),
        scratch_shapes=[pltpu.VMEM((nb * M_HEADS, M_HD, 2 * M_HD), F32),
                        pltpu.VMEM((nb * NC, L, LANES), F32),
                        pltpu.VMEM((nb * NC, SUBLANES, L), F32),
                        pltpu.VMEM((nb * NC, SUBLANES, 2 * M_HD), F32),
                        pltpu.VMEM((nb, SUBLANES, LANES), F32)],
        compiler_params=_cparams(2),
        name="mlstm",
    )(oa3, oa3, g4, u_tri, gain, sel)
    return out.reshape(N, M_W)


_G_LEVELS = 6
_G_XROW = 2 * G_CHUNK + SUBLANES


def _gla_consts():
    L = G_CHUNK
    t = np.arange(L)
    blocks = [(t[None, :] <= t[:, None]).astype(np.float32),
              (t[None, :] > t[:, None]).astype(np.float32),
              np.ones((SUBLANES, L), np.float32)]
    masks = [np.eye(L, dtype=np.float32)]
    m = 1
    while m < L:
        wl = np.zeros((L, L), np.float32)
        for r in range(L):
            r0 = (r // (2 * m)) * 2 * m + m
            if r % (2 * m) >= m:
                wl[r, r0:r + 1] = 1.0
            else:
                wl[r, r + 1:r0] = 1.0
        blocks.append(wl)
        tt, ss = t[:, None], t[None, :]
        masks.append(((tt // (2 * m) == ss // (2 * m)) & (tt % (2 * m) >= m)
                      & (ss % (2 * m) < m)).astype(np.float32))
        m *= 2
    w = np.concatenate(blocks, axis=0)
    w3 = np.concatenate([w, w, w], axis=1)
    mk = np.stack([np.concatenate([x] * G_HEADS, axis=0) for x in masks])
    return w3, mk


def _gla_kernel(qk_ref, v_ref, gg_ref, la_ref, w3_ref, mk_ref, gain_ref, out_ref, st_ref, *, NC, nb):
    L = G_CHUNK

    @pl.when(pl.program_id(1) == 0)
    def _():
        st_ref[...] = jnp.zeros_like(st_ref)

    lane_head = lax.broadcasted_iota(jnp.int32, (L, G_KW), 1) // G_DK
    br = lax.broadcasted_iota(jnp.int32, (2 * G_DV, LANES), 0) < G_DV
    bl = lax.broadcasted_iota(jnp.int32, (2 * G_DV, LANES), 1) < G_DK
    bmask = br == bl
    nt = (((1,), (1,)), ((), ()))
    tn = (((0,), (0,)), ((), ()))

    def chunk(c, carry):
        rows = pl.ds(pl.multiple_of(c * L, L), L)
        X, q, k = [], [], []
        for bi in range(nb):
            la = la_ref[bi, rows, :]
            hi = la.astype(BF16)
            r1 = la - hi.astype(F32)
            mid = r1.astype(BF16)
            lo = (r1 - mid.astype(F32)).astype(BF16)
            stk = jnp.concatenate([hi, mid, lo], axis=0)
            X.append(jnp.exp(jnp.dot(w3_ref[...], stk, preferred_element_type=F32)))
            q.append(qk_ref[bi, rows, 0:G_KW].astype(F32))
            k.append(qk_ref[bi, rows, G_KW:2 * G_KW].astype(F32))

        sc = [[None] * (_G_LEVELS + 1) for _ in range(nb)]
        for lev in range(_G_LEVELS + 1):
            for bi in range(nb):
                if lev == 0:
                    qt, kt = q[bi], k[bi]
                else:
                    xl = X[bi][_G_XROW + L * (lev - 1):_G_XROW + L * lev, :]
                    qt, kt = q[bi] * xl, k[bi] * xl
                q4 = jnp.concatenate([jnp.where(lane_head == h, qt, 0.0) for h in range(G_HEADS)],
                                     axis=0).astype(BF16)
                sc[bi][lev] = lax.dot_general(q4, kt.astype(BF16), nt, preferred_element_type=F32)
        Ab = []
        for bi in range(nb):
            A = sc[bi][0] * mk_ref[0]
            for lev in range(1, _G_LEVELS + 1):
                A = A + sc[bi][lev] * mk_ref[lev]
            Ab.append(A.astype(BF16))

        for bi in range(nb):
            gg = gg_ref[bi, rows, :].astype(F32)
            gate = gg * _sigmoid(gg)
            for p in range(2):
                ls = slice(LANES * p, LANES * (p + 1))
                vp = v_ref[bi, rows, 2 * G_DV * p:2 * G_DV * (p + 1)]
                oi = [jnp.dot(Ab[bi][L * (2 * p + hh):L * (2 * p + hh + 1)],
                              vp[:, G_DV * hh:G_DV * (hh + 1)], preferred_element_type=F32)
                      for hh in range(2)]
                st = st_ref[bi, p]
                qc = (q[bi][:, ls] * X[bi][0:L, ls]).astype(BF16)
                o_inter = lax.dot_general(qc, st.astype(BF16), nt, preferred_element_type=F32)
                kc = (k[bi][:, ls] * X[bi][L:2 * L, ls]).astype(BF16)
                upd = lax.dot_general(vp, kc, tn, preferred_element_type=F32)
                dec = X[bi][2 * L:2 * L + 1, ls]
                st_ref[bi, p] = jnp.where(bmask, dec * st + upd, 0.0)
                for hh in range(2):
                    o = o_inter[:, G_DV * hh:G_DV * (hh + 1)] + oi[hh]
                    hn = o * lax.rsqrt(jnp.mean(o * o, axis=-1, keepdims=True) + LN_EPS)
                    hs = slice(G_DV * (2 * p + hh), G_DV * (2 * p + hh + 1))
                    out_ref[bi, rows, hs] = (hn * gain_ref[:, hs] * gate[:, hs]).astype(BF16)
        return carry

    lax.fori_loop(0, NC, chunk, 0)


def _gla(oa, la, w3, mk, gain, *, B, S, nb, ts):
    N = oa.shape[0]
    oa3 = oa.reshape(B, S, oa.shape[1])
    la3 = la.reshape(B, S, G_KW)
    kern = functools.partial(_gla_kernel, NC=ts // G_CHUNK, nb=nb)
    out = pl.pallas_call(
        kern,
        grid=(B // nb, S // ts),
        in_specs=[pl.BlockSpec((nb, ts, 2 * G_KW), lambda b, t: (b, t, C_GQK // (2 * G_KW))),
                  pl.BlockSpec((nb, ts, G_W), lambda b, t: (b, t, C_GV // G_W)),
                  pl.BlockSpec((nb, ts, G_W), lambda b, t: (b, t, C_GG // G_W)),
                  pl.BlockSpec((nb, ts, G_KW), lambda b, t: (b, t, 0)),
                  pl.BlockSpec(w3.shape, lambda b, t: (0, 0)),
                  pl.BlockSpec(mk.shape, lambda b, t: (0, 0, 0)),
                  pl.BlockSpec((1, G_W), lambda b, t: (0, 0))],
        out_specs=pl.BlockSpec((nb, ts, G_W), lambda b, t: (b, t, 0)),
        out_shape=jax.ShapeDtypeStruct((B, S, G_W), BF16),
        scratch_shapes=[pltpu.VMEM((nb, 2, 2 * G_DV, LANES), F32)],
        compiler_params=_cparams(2),
        name="gla",
    )(oa3, oa3, oa3, la3, w3, mk, gain)
    return out.reshape(N, G_W)


def _layer_norm(z, g, b):
    mu = jnp.mean(z, axis=-1, keepdims=True)
    zc = z - mu
    var = jnp.mean(zc * zc, axis=-1, keepdims=True)
    return zc * lax.rsqrt(var + LN_EPS) * g + b


def _outproj_kernel(hm_ref, hg_ref, wf_ref, x_ref, mod_ref, g_ref, b_ref, wr_ref, br_ref,
                    x1_ref, u2_ref, rrow_ref, w_ref, *, tb, nh):
    @pl.when(pl.program_id(0) == 0)
    def _():
        w_ref[...] = wf_ref[...].astype(BF16)

    mod = mod_ref[0]
    blocks = [slice(tb * j, tb * (j + 1)) for j in range(nh)]
    y = [jnp.dot(hm_ref[r, :], w_ref[0:M_W, :], preferred_element_type=F32)
         + jnp.dot(hg_ref[r, :], w_ref[M_W:M_W + G_W, :], preferred_element_type=F32) for r in blocks]
    u2 = []
    for j, r in enumerate(blocks):
        z = ALPHA * x_ref[r, :] + (1.0 + mod[2:3, :]) * y[j]
        x1 = _layer_norm(z, g_ref[...], b_ref[...])
        x1_ref[r, :] = x1
        u2.append(x1 * (1.0 + mod[4:5, :]) + mod[3:4, :])
        u2_ref[r, :] = u2[j].astype(BF16)

    u2h = [u.astype(BF16) for u in u2]
    u2l = [(u2[j] - u2h[j].astype(F32)).astype(BF16) for j in range(nh)]
    lh = [jnp.dot(u, wr_ref[...], preferred_element_type=F32) for u in u2h]
    ll = [jnp.dot(u, wr_ref[:, 0:LANES], preferred_element_type=F32) for u in u2l]
    for j in range(nh):
        logits = lh[j][:, 0:LANES] + lh[j][:, LANES:2 * LANES] + ll[j] + br_ref[...]
        rrow_ref[j] = _route_select(logits.T, tb)


def _route_select(lt, tm):
    row = lax.broadcasted_iota(jnp.int32, (SUBLANES, tm), 0)
    gl = jnp.where(row < N_GROUPS, lt[0:SUBLANES, :], -jnp.inf)
    gmax = jnp.max(gl, axis=0, keepdims=True)
    gsel = jnp.min(jnp.where(gl == gmax, row, SUBLANES), axis=0, keepdims=True)
    pg = 1.0 / jnp.sum(jnp.exp(gl - gmax), axis=0, keepdims=True)
    ein = jnp.zeros((SUBLANES, tm), F32)
    for g in range(N_GROUPS):
        ein = jnp.where(gsel == g, lt[SUBLANES * (g + 1):SUBLANES * (g + 2), :], ein)
    v1 = jnp.max(ein, axis=0, keepdims=True)
    i1 = jnp.min(jnp.where(ein == v1, row, SUBLANES), axis=0, keepdims=True)
    rest = jnp.where(row == i1, -jnp.inf, ein)
    v2 = jnp.max(rest, axis=0, keepdims=True)
    i2 = jnp.min(jnp.where(rest == v2, row, SUBLANES), axis=0, keepdims=True)
    t2 = jnp.exp(v2 - v1)
    p1 = 1.0 / (1.0 + t2)
    e0 = (gsel * E_PER_G + i1).astype(F32)
    e1 = (gsel * E_PER_G + i2).astype(F32)
    return jnp.concatenate([e0, e1, pg * p1, pg * (t2 * p1), jnp.zeros((SUBLANES - 4, tm), F32)], axis=0)


def _outproj(hm, hg, w_out, x2, mod3, g, b, wr, br, *, S, tb, nh):
    N, D = x2.shape
    tm = tb * nh
    tpb = S // tm
    kern = functools.partial(_outproj_kernel, tb=tb, nh=nh)
    return pl.pallas_call(
        kern,
        grid=(N // tm,),
        in_specs=[pl.BlockSpec((tm, M_W), lambda i: (i, 0)),
                  pl.BlockSpec((tm, G_W), lambda i: (i, 0)),
                  pl.BlockSpec((M_W + G_W, D), lambda i: (0, 0), pipeline_mode=pl.Buffered(1)),
                  pl.BlockSpec((tm, D), lambda i: (i, 0)),
                  pl.BlockSpec((1, 6, D), lambda i: (i // tpb, 0, 0)),
                  pl.BlockSpec((1, D), lambda i: (0, 0)),
                  pl.BlockSpec((1, D), lambda i: (0, 0)),
                  pl.BlockSpec((D, 2 * LANES), lambda i: (0, 0)),
                  pl.BlockSpec((1, LANES), lambda i: (0, 0))],
        out_specs=[pl.BlockSpec((tm, D), lambda i: (i, 0)),
                   pl.BlockSpec((tm, D), lambda i: (i, 0)),
                   pl.BlockSpec((nh, SUBLANES, tb), lambda i: (i, 0, 0))],
        out_shape=[jax.ShapeDtypeStruct((N, D), F32),
                   jax.ShapeDtypeStruct((N, D), BF16),
                   jax.ShapeDtypeStruct((N // tb, SUBLANES, tb), F32)],
        scratch_shapes=[pltpu.VMEM((M_W + G_W, D), BF16)],
        compiler_params=_cparams(),
        name="outproj",
    )(hm, hg, w_out, x2, mod3, g, b, wr, br)


def _slots_per_tile(tb):
    worst = 2 * tb + N_EXP * (GRAN - 1)
    return -(-worst // LANES) * LANES


def _ffn_tiles(n_tok, tb):
    worst_rows = 2 * n_tok + (n_tok // tb) * N_EXP * (GRAN - 1)
    return -(-worst_rows // FFN_TM) + N_EXP


def _route_kernel(rr_ref, u_ref, lt_ref, srow_ref, col_ref, gd_ref, meta_ref, src_ref, mg_ref, part_ref,
                  tce_ref, tlo_ref, *, NT, tb, TM, n_tiles, SL):
    iota_e = lax.broadcasted_iota(jnp.int32, (N_EXP, tb), 0).astype(F32)
    glane = lax.broadcasted_iota(jnp.int32, (N_EXP, LANES), 1).astype(F32)
    eye = jnp.where(glane == lax.broadcasted_iota(jnp.int32, (N_EXP, LANES), 0).astype(F32), 1.0, 0.0)
    ltri = lt_ref[...]
    big = float(2 ** 24)
    tce_ref[...] = jnp.full(tce_ref.shape, big, F32)
    tlo_ref[...] = jnp.zeros_like(tlo_ref)

    def to_row(col):
        return jnp.sum(eye * col, axis=0, keepdims=True)

    def prefix_e(col):
        return jnp.dot(ltri, jnp.broadcast_to(col, (N_EXP, LANES)),
                       preferred_element_type=F32, precision=HIGHEST)[:, 0:1]

    def p1(j, run8):
        r = rr_ref[j]
        oh0 = jnp.where(iota_e == r[0:1, :], 1.0, 0.0)
        oh1 = jnp.where(iota_e == r[1:2, :], 1.0, 0.0)
        cum0 = jnp.dot(oh0.astype(BF16), u_ref[...], preferred_element_type=F32)
        cum1 = jnp.dot(oh1.astype(BF16), u_ref[...], preferred_element_type=F32)
        c0 = jnp.sum(oh0, axis=1, keepdims=True)
        n8 = jnp.floor((c0 + jnp.sum(oh1, axis=1, keepdims=True) + (GRAN - 1.0)) * (1.0 / GRAN))
        lo8 = prefix_e(n8)
        s0 = jnp.sum(oh0 * (GRAN * lo8 + cum0 - 1.0), axis=0, keepdims=True)
        s1 = jnp.sum(oh1 * (GRAN * lo8 + c0 + cum1 - 1.0), axis=0, keepdims=True)
        info = jnp.concatenate([s0, s1, r[2:4, :], jnp.zeros((SUBLANES - 4, tb), F32)], axis=0)
        srow_ref[j] = info
        col_ref[pl.ds(pl.multiple_of(j * tb, tb), tb), :] = jnp.concatenate(
            [info, jnp.zeros((LANES - SUBLANES, tb), F32)], axis=0).T
        mg = jnp.where((lo8 <= glane) & (glane < lo8 + n8), 1.0, 0.0)
        mg_ref[j] = mg
        part = jnp.sum(mg * (run8 + glane - lo8), axis=0, keepdims=True)
        gcnt = jnp.broadcast_to(jnp.sum(n8, axis=0, keepdims=True), (1, LANES))
        part_ref[j] = jnp.concatenate([part, gcnt, jnp.zeros((SUBLANES - 2, LANES), F32)], axis=0)
        tce_ref[pl.ds(j, 1), :] = to_row(run8 + n8)
        tlo_ref[pl.ds(j, 1), :] = to_row(lo8 - run8)
        return run8 + n8

    tot8 = lax.fori_loop(0, NT, p1, jnp.zeros((N_EXP, 1), F32), unroll=4 if NT % 4 == 0 else 1)
    seg_t = jnp.floor((tot8 * GRAN + (TM - 1.0)) * (1.0 / TM))
    base_t = prefix_e(seg_t)
    base8 = base_t * (TM // GRAN)
    lane1 = lax.broadcasted_iota(jnp.int32, (1, LANES), 1)

    def p2(j, carry):
        pr = part_ref[j]
        dst = (pr[0:1, :] + jnp.sum(mg_ref[j] * base8, axis=0, keepdims=True)) * GRAN
        gd_ref[j] = jnp.where(lane1 == G_LAST, pr[1:2, :], dst).astype(jnp.int32)
        return carry

    lax.fori_loop(0, NT, p2, 0, unroll=4 if NT % 4 == 0 else 1)
    base_row, seg_row, tot_row = to_row(base_t), to_row(seg_t), to_row(tot8)
    gpt = TM // GRAN
    q_row = lax.broadcasted_iota(jnp.int32, (1, LANES), 1).astype(F32)
    sub_j = lax.broadcasted_iota(jnp.int32, tce_ref.shape, 0).astype(F32)
    zero_gran = float(SL // GRAN - 1)

    def p3(i, carry):
        fi = jnp.asarray(i, jnp.int32).astype(F32)
        own = jnp.where((base_row <= fi) & (fi < base_row + seg_row), 1.0, 0.0)
        k_i = jnp.sum(own * (fi - base_row), axis=1, keepdims=True)
        tot_i = jnp.sum(own * tot_row, axis=1, keepdims=True)
        ce = jnp.sum(tce_ref[...] * own, axis=1, keepdims=True)
        lo = jnp.sum(tlo_ref[...] * own, axis=1, keepdims=True)
        r = k_i * gpt + q_row
        jr = jnp.sum(jnp.where(ce <= r, 1.0, 0.0), axis=0, keepdims=True)
        g = jnp.sum(jnp.where(sub_j == jr, lo, 0.0), axis=0, keepdims=True) + r
        src = jnp.where(r < tot_i, jr * (SL // GRAN) + g, zero_gran)
        src_ref[i] = (src * GRAN).astype(jnp.int32)
        return carry

    lax.fori_loop(0, n_tiles + 1, p3, 0, unroll=5 if (n_tiles + 1) % 5 == 0 else 1)
    ti = lax.broadcasted_iota(jnp.int32, (N_EXP, tb), 1).astype(F32)
    te = jnp.sum(jnp.where(base_t <= ti, 1.0, 0.0), axis=0, keepdims=True) - 1.0
    nv = jnp.broadcast_to(jnp.sum(seg_t, axis=0, keepdims=True), (1, tb))
    meta_ref[...] = jnp.concatenate([te, nv, jnp.zeros((SUBLANES - 2, tb), F32)],
                                    axis=0).astype(jnp.int32)


def _route(rrow, u_cnt, ltri, *, TM, n_tiles):
    NT, _, tb = rrow.shape
    SL = _slots_per_tile(tb)
    assert n_tiles + 1 <= tb and SL // GRAN <= G_LAST and TM // GRAN <= LANES
    kern = functools.partial(_route_kernel, NT=NT, tb=tb, TM=TM, n_tiles=n_tiles, SL=SL)
    ntp = -(-NT // SUBLANES) * SUBLANES
    full3 = lambda i: (0, 0, 0)
    return pl.pallas_call(
        kern,
        grid=(1,),
        in_specs=[pl.BlockSpec((NT, SUBLANES, tb), full3),
                  pl.BlockSpec((tb, tb), lambda i: (0, 0)),
                  pl.BlockSpec((N_EXP, N_EXP), lambda i: (0, 0))],
        out_specs=[pl.BlockSpec((NT, SUBLANES, tb), full3),
                   pl.BlockSpec((NT * tb, LANES), lambda i: (0, 0)),
                   pl.BlockSpec((NT, 1, LANES), full3),
                   pl.BlockSpec((SUBLANES, tb), lambda i: (0, 0)),
                   pl.BlockSpec((n_tiles + 1, 1, LANES), full3)],
        out_shape=[jax.ShapeDtypeStruct((NT, SUBLANES, tb), F32),
                   jax.ShapeDtypeStruct((NT * tb, LANES), F32),
                   jax.ShapeDtypeStruct((NT, 1, LANES), jnp.int32),
                   jax.ShapeDtypeStruct((SUBLANES, tb), jnp.int32),
                   jax.ShapeDtypeStruct((n_tiles + 1, 1, LANES), jnp.int32)],
        scratch_shapes=[pltpu.VMEM((NT, N_EXP, LANES), F32), pltpu.VMEM((NT, SUBLANES, LANES), F32),
                        pltpu.VMEM((ntp, LANES), F32), pltpu.VMEM((ntp, LANES), F32)],
        compiler_params=_cparams(),
        name="route",
    )(rrow, u_cnt, ltri)


U32 = jnp.uint32
_HI_MASK = 0xFFFF0000


def _pack_halves(x):
    c = x.shape[1] // 2
    lo = lax.bitcast_convert_type(x[:, :c], U32)
    hi = lax.bitcast_convert_type(x[:, c:], U32)
    return (lo >> 16) | (hi & U32(_HI_MASK))


def _unpack_halves(w):
    lo = lax.bitcast_convert_type(w << 16, F32)
    hi = lax.bitcast_convert_type(w & U32(_HI_MASK), F32)
    return jnp.concatenate([lo, hi], axis=1).astype(BF16)


def _granule_copy(src_ref, src_row, dst_ref, dst_row, sem):
    return pltpu.make_async_copy(src_ref.at[pl.ds(src_row, GRAN), :], dst_ref.at[pl.ds(dst_row, GRAN), :], sem)


def _for_granules(n, body, unroll=4):
    def blk(i, carry):
        for t in range(unroll):
            body(i * unroll + t)
        return carry

    def one(g, carry):
        body(g)
        return carry

    nblk = n // unroll
    lax.fori_loop(0, nblk, blk, 0)
    lax.fori_loop(nblk * unroll, n, one, 0)


def _wait_granules(n, src_ref, dst_ref, sem, n_max):
    b = 1
    while b <= n_max:
        @pl.when((n & b) != 0)
        def _(b=b):
            pltpu.make_async_copy(src_ref.at[pl.ds(0, b * GRAN), :], dst_ref.at[pl.ds(0, b * GRAN), :],
                                  sem).wait()
        b *= 2


def _dispatch_kernel(srow_ref, u_ref, o_ref, *, SL):
    s = srow_ref[0]
    rows = lax.broadcasted_iota(jnp.int32, (SL, s.shape[1]), 0).astype(F32)
    m0 = rows == s[0:1, :]
    m1 = rows == s[1:2, :]
    oh = jnp.where(m0 | m1, 1.0, 0.0).astype(BF16)
    dw = u_ref.shape[1] // 2
    o_ref[:, 0:dw] = _pack_halves(jnp.dot(oh, u_ref[...], preferred_element_type=F32))
    wrow = jnp.sum(jnp.where(m0, s[2:3, :], 0.0) + jnp.where(m1, s[3:4, :], 0.0), axis=1, keepdims=True)
    o_ref[:, dw:dw + LANES] = lax.bitcast_convert_type(jnp.broadcast_to(wrow, (SL, LANES)), U32)


def _dispatch(srow, u2):
    N, D = u2.shape
    NT, _, tb = srow.shape
    SL = _slots_per_tile(tb)
    xw = D // 2 + LANES
    return pl.pallas_call(
        functools.partial(_dispatch_kernel, SL=SL),
        grid=(NT,),
        in_specs=[pl.BlockSpec((1, SUBLANES, tb), lambda j: (j, 0, 0)),
                  pl.BlockSpec((tb, D), lambda j: (j, 0))],
        out_specs=pl.BlockSpec((SL, xw), lambda j: (j, 0)),
        out_shape=jax.ShapeDtypeStruct((NT * SL, xw), U32),
        compiler_params=_cparams(),
        name="dispatch",
    )(srow, u2)


def _ffn_kernel(te_ref, nv_ref, src_ref, x2_ref, wg_ref, wu_ref, wd_ref, o_ref, wgb, wub, wdb, sg, su, sd,
                slot_ref, sems, xbuf, xsems):
    i = pl.program_id(0)
    nv = nv_ref[0]
    e = te_ref[i]
    tm = xbuf.shape[1]
    xs_slot = i % 2

    def fetch_rows(tile, sl):
        for q in range(tm // GRAN):
            _granule_copy(x2_ref, pl.multiple_of(src_ref[tile, q], GRAN), xbuf.at[sl], q * GRAN,
                          xsems.at[sl]).start()

    def wait_rows(sl):
        pltpu.make_async_copy(x2_ref.at[pl.ds(0, tm), :], xbuf.at[sl], xsems.at[sl]).wait()

    @pl.when(i == 0)
    def _():
        fetch_rows(0, 0)

    def weight_copies(ex, sl):
        return (pltpu.make_async_copy(wg_ref.at[ex], sg.at[sl], sems.at[sl]),
                pltpu.make_async_copy(wu_ref.at[ex], su.at[sl], sems.at[sl]),
                pltpu.make_async_copy(wd_ref.at[ex], sd.at[sl], sems.at[sl]))

    @pl.when(i == 0)
    def _():
        slot_ref[0] = 0
        for cp in weight_copies(e, 0):
            cp.start()

    @pl.when((i < nv) & ((i == 0) | (e != te_ref[jnp.maximum(i - 1, 0)])))
    def _():
        sl = slot_ref[0]
        for cp in weight_copies(e, sl):
            cp.wait()
        wgb[...] = sg[sl].astype(BF16)
        wub[...] = su[sl].astype(BF16)
        wdb[...] = sd[sl].astype(BF16)
        nxt = lax.while_loop(lambda t: (t < nv) & (te_ref[jnp.minimum(t, nv - 1)] == e), lambda t: t + 1, i + 1)

        @pl.when(nxt < nv)
        def _():
            for cp in weight_copies(te_ref[nxt], 1 - sl):
                cp.start()
        slot_ref[0] = 1 - sl

    @pl.when(i == nv)
    def _():
        wait_rows(xs_slot)

    @pl.when(i < nv)
    def _():
        wait_rows(xs_slot)
        fetch_rows(i + 1, 1 - xs_slot)
        xs_ref = xbuf.at[xs_slot]
        nsub = FFN_SUB
        hm = tm // nsub
        dw = o_ref.shape[1]
        halves = tuple(slice(hm * j, hm * (j + 1)) for j in range(nsub))
        x = [_unpack_halves(xs_ref[r, 0:dw]) for r in halves]
        g = [jnp.dot(x[j], wgb[...], preferred_element_type=F32) for j in range(nsub)]
        u = [jnp.dot(x[j], wub[...], preferred_element_type=F32) for j in range(nsub)]
        h = [(g[j] * _sigmoid(g[j]) * u[j]).astype(BF16) for j in range(nsub)]
        y = [jnp.dot(h[j], wdb[...], preferred_element_type=F32) for j in range(nsub)]
        for j in range(nsub):
            wt = lax.bitcast_convert_type(xs_ref[halves[j], dw:dw + LANES], F32)
            yw = y[j] * jnp.concatenate([wt] * (2 * dw // LANES), axis=1)
            o_ref[halves[j], :] = _pack_halves(yw.astype(BF16).astype(F32))

    @pl.when((i >= nv) & (i < pl.num_programs(0) - 1))
    def _():
        o_ref[...] = jnp.zeros_like(o_ref)


def _ffn(te, nv, src, x2, wg, wu, wd, *, TM, n_tiles):
    XW = x2.shape[1]
    DW = XW - LANES
    D = 2 * DW
    grid_spec = pltpu.PrefetchScalarGridSpec(
        num_scalar_prefetch=3,
        grid=(n_tiles + 1,),
        in_specs=[pl.BlockSpec(memory_space=pl.ANY),
                  pl.BlockSpec(memory_space=pl.ANY),
                  pl.BlockSpec(memory_space=pl.ANY),
                  pl.BlockSpec(memory_space=pl.ANY)],
        out_specs=pl.BlockSpec((TM, DW), lambda i, te, nv, src: (jnp.minimum(i, n_tiles - 1), 0)),
        scratch_shapes=[pltpu.VMEM((D, D_EXP), BF16), pltpu.VMEM((D, D_EXP), BF16),
                        pltpu.VMEM((D_EXP, D), BF16),
                        pltpu.VMEM((2, D, D_EXP), F32), pltpu.VMEM((2, D, D_EXP), F32),
                        pltpu.VMEM((2, D_EXP, D), F32), pltpu.SMEM((1,), jnp.int32),
                        pltpu.SemaphoreType.DMA((2,)),
                        pltpu.VMEM((2, TM, XW), U32), pltpu.SemaphoreType.DMA((2,))],
    )
    return pl.pallas_call(
        _ffn_kernel,
        grid_spec=grid_spec,
        out_shape=jax.ShapeDtypeStruct((n_tiles * TM, DW), U32),
        compiler_params=_cparams(),
        name="ffn",
    )(te, nv, src, x2, wg, wu, wd)


def _combine_kernel(gd_ref, ys_ref, col_ref, x1_ref, mod_ref, g_ref, b_ref, o_ref, buf, sems, *, NT, SL):
    j = pl.program_id(0)
    slot = j % 2

    def fetch(tile, sl):
        def f(g):
            _granule_copy(ys_ref, pl.multiple_of(gd_ref[tile, g], GRAN), buf.at[sl],
                          pl.multiple_of(g * GRAN, GRAN), sems.at[sl]).start()
        _for_granules(gd_ref[tile, G_LAST], f)

    @pl.when(j == 0)
    def _():
        fetch(0, 0)

    @pl.when(j + 1 < NT)
    def _():
        fetch(j + 1, 1 - slot)

    ng = gd_ref[j, G_LAST]

    _wait_granules(ng, ys_ref, buf.at[slot], sems.at[slot], SL // GRAN)

    rows = lax.broadcasted_iota(jnp.int32, (SL, 1), 0)
    yb = _unpack_halves(jnp.where(rows < ng * GRAN, buf[slot], U32(0)))
    col = col_ref[...]
    tb = col.shape[0]
    lanes = lax.broadcasted_iota(jnp.int32, (tb, SL), 1).astype(F32)
    sel = jnp.where((lanes == col[:, 0:1]) | (lanes == col[:, 1:2]), 1.0, 0.0).astype(BF16)
    y = jnp.dot(sel, yb, preferred_element_type=F32)
    mod = mod_ref[0]
    z = ALPHA * x1_ref[...] + (1.0 + mod[5:6, :]) * y
    o_ref[...] = _layer_norm(z, g_ref[...], b_ref[...])


def _combine(gd, ys, col, x1, mod3, g, b, *, S, tb):
    N, D = x1.shape
    NT = N // tb
    tpb = S // tb
    SL = _slots_per_tile(tb)
    kern = functools.partial(_combine_kernel, NT=NT, SL=SL)
    grid_spec = pltpu.PrefetchScalarGridSpec(
        num_scalar_prefetch=1,
        grid=(NT,),
        in_specs=[pl.BlockSpec(memory_space=pl.ANY),
                  pl.BlockSpec((tb, LANES), lambda j, gd: (j, 0)),
                  pl.BlockSpec((tb, D), lambda j, gd: (j, 0)),
                  pl.BlockSpec((1, 6, D), lambda j, gd: (j // tpb, 0, 0)),
                  pl.BlockSpec((1, D), lambda j, gd: (0, 0)),
                  pl.BlockSpec((1, D), lambda j, gd: (0, 0))],
        out_specs=pl.BlockSpec((tb, D), lambda j, gd: (j, 0)),
        scratch_shapes=[pltpu.VMEM((2, SL, D // 2), U32), pltpu.SemaphoreType.DMA((2,))],
    )
    return pl.pallas_call(
        kern,
        grid_spec=grid_spec,
        out_shape=jax.ShapeDtypeStruct((N, D), F32),
        compiler_params=_cparams(),
        name="combine",
    )(gd, ys, col, x1, mod3, g, b)


def _layer(x, c, l, w_ada, b_ada, w_in, w_conv, b_conv, b_igate, b_fgate, mlstm_norm_g, w_gla_a, b_gla_a,
           gla_norm_g, w_out, ln1_g, ln1_b, w_route_group, b_route_group, w_route_expert, b_route_expert,
           w_gate, w_up, w_down, ln2_g, ln2_b):
    B, S, D = x.shape
    N = B * S
    x2 = x.reshape(N, D)
    tm_in = min(512, S)
    tm = min(256, S)
    lm = min(256, S)

    mod3 = _ada(c, w_ada[l], b_ada[l]).reshape(B, 6, D)

    wa_pad = jnp.zeros((LANES, G_KW), F32).at[SM_A:SM_A + G_RANK].set(w_gla_a[l]).astype(BF16)
    bg = (jnp.zeros((2 * SUBLANES, 1), F32).at[0:M_HEADS, 0].set(b_igate[l])
          .at[SUBLANES:SUBLANES + M_HEADS, 0].set(b_fgate[l]))
    oa, la, g3 = _inproj(x2, mod3, w_in[l], w_conv[l], b_conv[l].reshape(1, -1), wa_pad,
                         b_gla_a[l].reshape(1, -1), bg, S=S, tm=tm_in, lm=lm)

    u_tri = jnp.asarray(np.triu(np.ones((lm, lm), np.float32)))
    nb = 4 if B % 4 == 0 else (2 if B % 2 == 0 else 1)
    ts = min(512, S)
    hm = _mlstm(oa, g3, u_tri, mlstm_norm_g[l].reshape(1, -1), B=B, S=S, L=lm, nb=nb, ts=ts)
    w3_np, mk_np = _gla_consts()
    hg = _gla(oa, la, jnp.asarray(w3_np, BF16), jnp.asarray(mk_np), gla_norm_g[l].reshape(1, -1), B=B, S=S,
              nb=nb, ts=ts)

    wr = (jnp.zeros((D, LANES), F32).at[:, 0:N_GROUPS].set(w_route_group[l])
          .at[:, SUBLANES:SUBLANES + N_EXP].set(w_route_expert[l]))
    br = (jnp.zeros((1, LANES), F32).at[0, 0:N_GROUPS].set(b_route_group[l])
          .at[0, SUBLANES:SUBLANES + N_EXP].set(b_route_expert[l]))
    wr_hi = wr.astype(BF16)
    wr2 = jnp.concatenate([wr_hi, (wr - wr_hi.astype(F32)).astype(BF16)], axis=1)
    x1, u2, rrow = _outproj(hm, hg, w_out[l], x2, mod3, ln1_g[l].reshape(1, -1),
                            ln1_b[l].reshape(1, -1), wr2, br, S=S, tb=tm, nh=4 if S % (4 * tm) == 0 else 1)

    u_cnt = jnp.asarray(np.triu(np.ones((tm, tm), np.float32)), BF16)
    ltri = jnp.asarray(np.tril(np.ones((N_EXP, N_EXP), np.float32), -1))
    n_tiles = _ffn_tiles(N, tm)
    srow, col, gd3, meta, src3 = _route(rrow, u_cnt, ltri, TM=FFN_TM, n_tiles=n_tiles)
    gd = gd3.reshape(N // tm, LANES)
    src = src3.reshape(n_tiles + 1, LANES)
    te, nv = meta[0, :n_tiles + 1], meta[1, 0:1]

    x2s = _dispatch(srow, u2)
    ys = _ffn(te, nv, src, x2s, w_gate[l], w_up[l], w_down[l], TM=FFN_TM, n_tiles=n_tiles)
    out = _combine(gd, ys, col, x1, mod3, ln2_g[l].reshape(1, -1), ln2_b[l].reshape(1, -1), S=S, tb=tm)
    return out.reshape(B, S, D)


def kernel(x, c, w_ada, b_ada, w_in, w_conv, b_conv, b_igate, b_fgate, mlstm_norm_g, w_gla_a, b_gla_a,
           gla_norm_g, w_out, ln1_g, ln1_b, w_route_group, b_route_group, w_route_expert, b_route_expert,
           w_gate, w_up, w_down, ln2_g, ln2_b):
    for l in range(DEPTH):
        x = _layer(x, c, l, w_ada, b_ada, w_in, w_conv, b_conv, b_igate, b_fgate, mlstm_norm_g, w_gla_a,
                   b_gla_a, gla_norm_g, w_out, ln1_g, ln1_b, w_route_group, b_route_group, w_route_expert,
                   b_route_expert, w_gate, w_up, w_down, ln2_g, ln2_b)
    return x
```

```python
import functools

import numpy as np
import jax
import jax.numpy as jnp
from jax import lax
from jax.experimental import pallas as pl
from jax.experimental.pallas import tpu as pltpu

F32 = jnp.float32
BF16 = jnp.bfloat16
U32 = jnp.uint32
HIGHEST = lax.Precision.HIGHEST

DEPTH = 1
M_HEADS = 4
M_HD = 128
M_W = M_HEADS * M_HD
CONV_W = 4
G_HEADS = 4
G_DK = 64
G_DV = 128
G_W = G_HEADS * G_DV
G_KW = G_HEADS * G_DK
G_RANK = 16
G_TAU = 16.0
G_CHUNK = 64
N_GROUPS = 4
E_PER_G = 8
N_EXP = N_GROUPS * E_PER_G
D_EXP = 512
ALPHA = (2 * DEPTH) ** 0.25
LN_EPS = 1e-5

LANES = 128
SUBLANES = 8
VMEM_LIMIT = 48 * 1024 * 1024

C_QK = 0
C_VO = 1024
C_GQK = 2048
C_GV = 2560
C_GG = 3072
C_SMALL = 3584
C_TOT = 3712
SM_I, SM_F, SM_A = 0, 8, 16
IN_GATES = 4 * M_W
IN_G = IN_GATES + 2 * M_HEADS
IN_GA = IN_G + 2 * G_KW + 2 * G_W
IN_TOT = IN_GA + G_RANK

FFN_TM = 512
FFN_SUB = 2
GRAN = SUBLANES
G_LAST = LANES - 1


def _cparams(n_axes=1):
    return pltpu.CompilerParams(dimension_semantics=("arbitrary",) * n_axes,
                                vmem_limit_bytes=VMEM_LIMIT)


def _sigmoid(x):
    return 1.0 / (1.0 + jnp.exp(-x))


def _log_sigmoid(x):
    return jnp.minimum(x, 0.0) - jnp.log(1.0 + jnp.exp(-jnp.abs(x)))


def _ada_kernel(c_ref, w_ref, b_ref, o_ref):
    c = c_ref[...]
    ca = c * _sigmoid(c)
    o_ref[...] = jnp.dot(ca, w_ref[...], preferred_element_type=F32, precision=HIGHEST) + b_ref[...]


def _ada(c, w, b):
    B, D = c.shape
    n_out = w.shape[1]
    tn = 1024
    return pl.pallas_call(
        _ada_kernel,
        grid=(n_out // tn,),
        in_specs=[pl.BlockSpec((B, D), lambda j: (0, 0)),
                  pl.BlockSpec((D, tn), lambda j: (0, j)),
                  pl.BlockSpec((1, tn), lambda j: (0, j))],
        out_specs=pl.BlockSpec((B, tn), lambda j: (0, j)),
        out_shape=jax.ShapeDtypeStruct((B, n_out), F32),
        compiler_params=_cparams(),
        name="ada",
    )(c, w, b.reshape(1, n_out))


def _inproj_kernel(x_ref, mod_ref, win_ref, wc_ref, bc_ref, wa_ref, ba_ref, bg_ref,
                   oa_ref, la_ref, g_ref, halo_ref, w_ref, *, tm, tpb, lm):
    i = pl.program_id(0)

    @pl.when(i == 0)
    def _():
        rc = 2 * LANES
        for r in range(0, IN_GATES, rc):
            w_ref[r:r + rc, :] = win_ref[0, r:r + rc, :].astype(BF16)
        for r in range(0, C_SMALL - C_GQK, rc):
            w_ref[C_GQK + r:C_GQK + r + rc, :] = win_ref[0, IN_G + r:IN_G + r + rc, :].astype(BF16)
        gates = win_ref[0, IN_GATES:IN_G, :]
        z = lambda n: jnp.zeros((n, gates.shape[1]), F32)
        small = jnp.concatenate([gates[0:M_HEADS], z(SM_F - M_HEADS), gates[M_HEADS:2 * M_HEADS],
                                 z(SM_A - SM_F - M_HEADS), win_ref[0, IN_GA:IN_TOT, :],
                                 z(LANES - SM_A - G_RANK)], axis=0)
        w_ref[C_SMALL:C_TOT, :] = small.astype(BF16)

    @pl.when(i % tpb == 0)
    def _():
        halo_ref[0:SUBLANES, :] = jnp.zeros((SUBLANES, halo_ref.shape[1]), F32)

    mod = mod_ref[0]
    u = (x_ref[...] * (1.0 + mod[1:2, :]) + mod[0:1, :]).astype(BF16)

    def proj(c0, c1):
        return lax.dot_general(u, w_ref[c0:c1, :], (((1,), (1,)), ((), ())), preferred_element_type=F32)

    p = proj(C_QK, C_QK + 2 * M_W)
    halo_ref[SUBLANES:SUBLANES + tm, :] = p
    acc = bc_ref[...] + wc_ref[CONV_W - 1:CONV_W, :] * p
    for j in range(CONV_W - 1):
        acc = acc + wc_ref[j:j + 1, :] * halo_ref[pl.ds(SUBLANES - (CONV_W - 1) + j, tm), :]
    halo_ref[0:SUBLANES, :] = p[tm - SUBLANES:, :]
    qk = acc * _sigmoid(acc)
    oa_ref[:, C_QK:C_QK + M_W] = qk[:, :M_W].astype(BF16)
    oa_ref[:, C_QK + M_W:C_QK + 2 * M_W] = (qk[:, M_W:] * (M_HD ** -0.5)).astype(BF16)

    p = proj(C_VO, C_VO + 2 * M_W)
    oa_ref[:, C_VO:C_VO + 2 * M_W] = p.astype(BF16)

    p = proj(C_GQK, C_GQK + G_KW)
    oa_ref[:, C_GQK:C_GQK + G_KW] = (p * (G_DK ** -0.5)).astype(BF16)
    p = proj(C_GQK + G_KW, C_SMALL)
    oa_ref[:, C_GQK + G_KW:C_SMALL] = p.astype(BF16)

    ps = proj(C_SMALL, C_TOT)
    la = jnp.dot(ps.astype(BF16), wa_ref[...], preferred_element_type=F32) + ba_ref[...]
    la_ref[...] = _log_sigmoid(la) * (1.0 / G_TAU)
    pt = ps.T
    gi = pt[SM_I:SM_I + SUBLANES, :] + bg_ref[0:SUBLANES, :]
    gf = _log_sigmoid(pt[SM_F:SM_F + SUBLANES, :] + bg_ref[SUBLANES:2 * SUBLANES, :])
    for j in range(tm // lm):
        g_ref[j, 0:SUBLANES, :] = gi[:, j * lm:(j + 1) * lm]
        g_ref[j, SUBLANES:2 * SUBLANES, :] = gf[:, j * lm:(j + 1) * lm]


def _inproj(x2, mod3, w_in, w_conv, b_conv, wa_pad, b_gla, bg, *, S, tm, lm, layer):
    N, D = x2.shape
    tpb = S // tm
    kern = functools.partial(_inproj_kernel, tm=tm, tpb=tpb, lm=lm)
    return pl.pallas_call(
        kern,
        grid=(N // tm,),
        in_specs=[pl.BlockSpec((tm, D), lambda i: (i, 0)),
                  pl.BlockSpec((1, 6, D), lambda i: (i // tpb, 0, 0)),
                  pl.BlockSpec((1, IN_TOT, D), lambda i: (layer, 0, 0), pipeline_mode=pl.Buffered(1)),
                  pl.BlockSpec((CONV_W, 2 * M_W), lambda i: (0, 0)),
                  pl.BlockSpec((1, 2 * M_W), lambda i: (0, 0)),
                  pl.BlockSpec((LANES, G_KW), lambda i: (0, 0)),
                  pl.BlockSpec((1, G_KW), lambda i: (0, 0)),
                  pl.BlockSpec((2 * SUBLANES, 1), lambda i: (0, 0))],
        out_specs=[pl.BlockSpec((tm, C_SMALL), lambda i: (i, 0)),
                   pl.BlockSpec((tm, G_KW), lambda i: (i, 0)),
                   pl.BlockSpec((tm // lm, 2 * SUBLANES, lm), lambda i: (i, 0, 0))],
        out_shape=[jax.ShapeDtypeStruct((N, C_SMALL), BF16),
                   jax.ShapeDtypeStruct((N, G_KW), F32),
                   jax.ShapeDtypeStruct((N // lm, 2 * SUBLANES, lm), F32)],
        scratch_shapes=[pltpu.VMEM((SUBLANES + tm, 2 * M_W), F32), pltpu.VMEM((C_TOT, D), BF16)],
        compiler_params=_cparams(),
        name="inproj",
    )(x2, mod3, w_in, w_conv, b_conv, wa_pad, b_gla, bg)


def _mlstm_sel():
    sel = np.zeros((2 * LANES, 2 * M_HEADS * M_HD), np.float32)
    for j in range(2 * M_HEADS):
        src = (SUBLANES if j < M_HEADS else 3 * SUBLANES) + j % M_HEADS
        sel[src, M_HD * j:M_HD * (j + 1)] = 1.0
        sel[LANES + src, M_HD * j:M_HD * (j + 1)] = 1.0
    return sel


def _mlstm_kernel(qk_ref, vo_ref, g_ref, u_ref, gain_ref, sel_ref, out_ref, c_ref, zt_ref, a_ref, dec_ref, m_ref,
                  *, L, NC, nb):
    @pl.when(pl.program_id(1) == 0)
    def _():
        c_ref[...] = jnp.zeros_like(c_ref)
        m_ref[...] = jnp.zeros_like(m_ref)

    tril = (lax.broadcasted_iota(jnp.int32, (L, L), 0) >= lax.broadcasted_iota(jnp.int32, (L, L), 1))
    ones_v = jnp.ones((L, M_HD), BF16)
    zpad = jnp.zeros((LANES - 4 * SUBLANES, L), F32)

    order = [(bi, c) for bi in range(nb) for c in range(NC)]
    f_all = jnp.concatenate([g_ref[bi, c, SUBLANES:2 * SUBLANES, :] for bi, c in order], axis=0)
    i_all = jnp.concatenate([g_ref[bi, c, 0:SUBLANES, :] for bi, c in order], axis=0)
    b_all = jnp.dot(f_all, u_ref[...], preferred_element_type=F32, precision=HIGHEST)
    a_all = i_all - b_all
    lane_all = lax.broadcasted_iota(jnp.int32, a_all.shape, 1)
    g_all = a_all
    s = 1
    while s < L:
        g_all = jnp.maximum(g_all, jnp.where(lane_all >= s, pltpu.roll(g_all, s, 1), -jnp.inf))
        s *= 2
    for bi in range(nb):
        m_prev = m_ref[bi][:, 0:1]
        for c in range(NC):
            ci = bi * NC + c
            r8 = slice(SUBLANES * ci, SUBLANES * (ci + 1))
            a, b = a_all[r8], b_all[r8]
            a_ref[ci] = a
            M = jnp.maximum(g_all[r8], m_prev)
            ML = M[:, L - 1:L]
            Z = jnp.concatenate([M, jnp.exp(m_prev - M), jnp.exp(-(b + M)), jnp.exp(a - ML), zpad],
                                axis=0)
            zt_ref[ci] = Z.T
            dec_ref[ci] = jnp.broadcast_to(jnp.exp(m_prev - ML), (SUBLANES, 2 * M_HD))
            m_prev = b[:, L - 1:L] + ML
        m_ref[bi] = jnp.broadcast_to(m_prev, (SUBLANES, LANES))

    chains = [(bi, h) for bi in range(nb) for h in range(M_HEADS)]
    nt = (((1,), (1,)), ((), ()))
    tn = (((0,), (0,)), ((), ()))

    def chunk(c, carry):
        rows = pl.ds(pl.multiple_of(c * L, L), L)
        Zt = [zt_ref[bi * NC + c] for bi in range(nb)]
        a = [a_ref[bi * NC + c] for bi in range(nb)]
        dec = [dec_ref[bi * NC + c] for bi in range(nb)]
        hs = [slice(h * M_HD, (h + 1) * M_HD) for h in range(M_HEADS)]
        hs2 = [slice(M_W + h * M_HD, M_W + (h + 1) * M_HD) for h in range(M_HEADS)]
        q = [qk_ref[bi, rows, hs[h]] for bi, h in chains]
        k = [qk_ref[bi, rows, hs2[h]] for bi, h in chains]
        vext = [jnp.concatenate([vo_ref[bi, rows, hs[h]], ones_v], axis=1) for bi, h in chains]
        cst = [c_ref[bi * M_HEADS + h] for bi, h in chains]
        n = range(len(chains))
        sc = [lax.dot_general(q[i], k[i], nt, preferred_element_type=F32) for i in n]
        qc = [jnp.dot(q[i], cst[i].astype(BF16), preferred_element_type=F32) for i in n]
        pm = [(sc[i] * jnp.exp(jnp.where(tril, a[bi][h:h + 1, :] - Zt[bi][:, h:h + 1], -jnp.inf))).astype(BF16)
              for i, (bi, h) in enumerate(chains)]
        pv = [jnp.dot(pm[i], vext[i], preferred_element_type=F32) for i in n]
        rep = []
        for bi in range(nb):
            zh = Zt[bi].astype(BF16)
            zl = (Zt[bi] - zh.astype(F32)).astype(BF16)
            rep.append(jnp.dot(jnp.concatenate([zh, zl], axis=1), sel_ref[...], preferred_element_type=F32))
        e_inter = [rep[bi][:, M_HD * h:M_HD * (h + 1)] for bi, h in chains]
        w_state = [rep[bi][:, M_HD * (M_HEADS + h):M_HD * (M_HEADS + h + 1)] for bi, h in chains]
        kw = [(w_state[i] * k[i].astype(F32)).astype(BF16) for i in n]
        upd = [lax.dot_general(kw[i], vext[i], tn, preferred_element_type=F32) for i in n]
        for i, (bi, h) in enumerate(chains):
            c_ref[bi * M_HEADS + h] = dec[bi][h:h + 1, :] * cst[i] + upd[i]
            nd = pv[i] + jnp.concatenate([e_inter[i], e_inter[i]], axis=1) * qc[i]
            hh = nd[:, :M_HD] / jnp.maximum(jnp.abs(nd[:, M_HD:]),
                                            Zt[bi][:, 2 * SUBLANES + h:2 * SUBLANES + h + 1])
            hh = _sigmoid(vo_ref[bi, rows, hs2[h]].astype(F32)) * hh
            hn = hh * lax.rsqrt(jnp.mean(hh * hh, axis=-1, keepdims=True) + LN_EPS)
            out_ref[bi, rows, hs[h]] = (hn * gain_ref[:, hs[h]]).astype(BF16)
        return carry

    lax.fori_loop(0, NC, chunk, 0)


def _mlstm(oa, g3, u_tri, gain, *, B, S, L, nb, ts):
    N = oa.shape[0]
    NC = ts // L
    oa3 = oa.reshape(B, S, oa.shape[1])
    g4 = g3.reshape(B, S // L, 2 * SUBLANES, L)
    sel = jnp.asarray(_mlstm_sel(), BF16)
    kern = functools.partial(_mlstm_kernel, L=L, NC=NC, nb=nb)
    out = pl.pallas_call(
        kern,
        grid=(B // nb, S // ts),
        in_specs=[pl.BlockSpec((nb, ts, 2 * M_W), lambda b, t: (b, t, C_QK // (2 * M_W))),
                  pl.BlockSpec((nb, ts, 2 * M_W), lambda b, t: (b, t, C_VO // (2 * M_W))),
                  pl.BlockSpec((nb, NC, 2 * SUBLANES, L), lambda b, t: (b, t, 0, 0)),
                  pl.BlockSpec((L, L), lambda b, t: (0, 0)),
                  pl.BlockSpec((1, M_W), lambda b, t: (0, 0)),
                  pl.BlockSpec(sel.shape, lambda b, t: (0, 0))],
        out_specs=pl.BlockSpec((nb, ts, M_W), lambda b, t: (b, t, 0)),
        out_shape=jax.ShapeDtypeStruct((B, S, M_W), BF16),
        scratch_shapes=[pltpu.VMEM((nb * M_HEADS, M_HD, 2 * M_HD), F32),
                        pltpu.VMEM((nb * NC, L, LANES), F32),
                        pltpu.VMEM((nb * NC, SUBLANES, L), F32),
                        pltpu.VMEM((nb * NC, SUBLANES, 2 * M_HD), F32),
                        pltpu.VMEM((nb, SUBLANES, LANES), F32)],
        compiler_params=_cparams(2),
        name="mlstm",
    )(oa3, oa3, g4, u_tri, gain, sel)
    return out.reshape(N, M_W)


_G_LEVELS = 6
_G_XROW = 2 * G_CHUNK + SUBLANES


def _gla_consts():
    L = G_CHUNK
    t = np.arange(L)
    blocks = [(t[None, :] <= t[:, None]).astype(np.float32),
              (t[None, :] > t[:, None]).astype(np.float32),
              np.ones((SUBLANES, L), np.float32)]
    masks = [np.eye(L, dtype=np.float32)]
    m = 1
    while m < L:
        wl = np.zeros((L, L), np.float32)
        for r in range(L):
            r0 = (r // (2 * m)) * 2 * m + m
            if r % (2 * m) >= m:
                wl[r, r0:r + 1] = 1.0
            else:
                wl[r, r + 1:r0] = 1.0
        blocks.append(wl)
        tt, ss = t[:, None], t[None, :]
        masks.append(((tt // (2 * m) == ss // (2 * m)) & (tt % (2 * m) >= m)
                      & (ss % (2 * m) < m)).astype(np.float32))
        m *= 2
    w = np.concatenate(blocks, axis=0)
    w3 = np.concatenate([w, w, w], axis=1)
    mk = np.stack([np.concatenate([x] * G_HEADS, axis=0) for x in masks])
    return w3, mk


def _gla_kernel(qk_ref, v_ref, gg_ref, la_ref, w3_ref, mk_ref, gain_ref, out_ref, st_ref, *, NC, nb):
    L = G_CHUNK

    @pl.when(pl.program_id(1) == 0)
    def _():
        st_ref[...] = jnp.zeros_like(st_ref)

    lane_head = lax.broadcasted_iota(jnp.int32, (L, G_KW), 1) // G_DK
    br = lax.broadcasted_iota(jnp.int32, (2 * G_DV, LANES), 0) < G_DV
    bl = lax.broadcasted_iota(jnp.int32, (2 * G_DV, LANES), 1) < G_DK
    bmask = br == bl
    nt = (((1,), (1,)), ((), ()))
    tn = (((0,), (0,)), ((), ()))

    def chunk(c, carry):
        rows = pl.ds(pl.multiple_of(c * L, L), L)
        X, q, k = [], [], []
        for bi in range(nb):
            la = la_ref[bi, rows, :]
            hi = la.astype(BF16)
            r1 = la - hi.astype(F32)
            mid = r1.astype(BF16)
            lo = (r1 - mid.astype(F32)).astype(BF16)
            stk = jnp.concatenate([hi, mid, lo], axis=0)
            X.append(jnp.exp(jnp.dot(w3_ref[...], stk, preferred_element_type=F32)))
            q.append(qk_ref[bi, rows, 0:G_KW].astype(F32))
            k.append(qk_ref[bi, rows, G_KW:2 * G_KW].astype(F32))

        sc = [[None] * (_G_LEVELS + 1) for _ in range(nb)]
        for lev in range(_G_LEVELS + 1):
            for bi in range(nb):
                if lev == 0:
                    qt, kt = q[bi], k[bi]
                else:
                    xl = X[bi][_G_XROW + L * (lev - 1):_G_XROW + L * lev, :]
                    qt, kt = q[bi] * xl, k[bi] * xl
                q4 = jnp.concatenate([jnp.where(lane_head == h, qt, 0.0) for h in range(G_HEADS)],
                                     axis=0).astype(BF16)
                sc[bi][lev] = lax.dot_general(q4, kt.astype(BF16), nt, preferred_element_type=F32)
        Ab = []
        for bi in range(nb):
            A = sc[bi][0] * mk_ref[0]
            for lev in range(1, _G_LEVELS + 1):
                A = A + sc[bi][lev] * mk_ref[lev]
            Ab.append(A.astype(BF16))

        for bi in range(nb):
            gg = gg_ref[bi, rows, :].astype(F32)
            gate = gg * _sigmoid(gg)
            for p in range(2):
                ls = slice(LANES * p, LANES * (p + 1))
                vp = v_ref[bi, rows, 2 * G_DV * p:2 * G_DV * (p + 1)]
                oi = [jnp.dot(Ab[bi][L * (2 * p + hh):L * (2 * p + hh + 1)],
                              vp[:, G_DV * hh:G_DV * (hh + 1)], preferred_element_type=F32)
                      for hh in range(2)]
                st = st_ref[bi, p]
                qc = (q[bi][:, ls] * X[bi][0:L, ls]).astype(BF16)
                o_inter = lax.dot_general(qc, st.astype(BF16), nt, preferred_element_type=F32)
                kc = (k[bi][:, ls] * X[bi][L:2 * L, ls]).astype(BF16)
                upd = lax.dot_general(vp, kc, tn, preferred_element_type=F32)
                dec = X[bi][2 * L:2 * L + 1, ls]
                st_ref[bi, p] = jnp.where(bmask, dec * st + upd, 0.0)
                for hh in range(2):
                    o = o_inter[:, G_DV * hh:G_DV * (hh + 1)] + oi[hh]
                    hn = o * lax.rsqrt(jnp.mean(o * o, axis=-1, keepdims=True) + LN_EPS)
                    hs = slice(G_DV * (2 * p + hh), G_DV * (2 * p + hh + 1))
                    out_ref[bi, rows, hs] = (hn * gain_ref[:, hs] * gate[:, hs]).astype(BF16)
        return carry

    lax.fori_loop(0, NC, chunk, 0)


def _gla(oa, la, w3, mk, gain, *, B, S, nb, ts):
    N = oa.shape[0]
    oa3 = oa.reshape(B, S, oa.shape[1])
    la3 = la.reshape(B, S, G_KW)
    kern = functools.partial(_gla_kernel, NC=ts // G_CHUNK, nb=nb)
    out = pl.pallas_call(
        kern,
        grid=(B // nb, S // ts),
        in_specs=[pl.BlockSpec((nb, ts, 2 * G_KW), lambda b, t: (b, t, C_GQK // (2 * G_KW))),
                  pl.BlockSpec((nb, ts, G_W), lambda b, t: (b, t, C_GV // G_W)),
                  pl.BlockSpec((nb, ts, G_W), lambda b, t: (b, t, C_GG // G_W)),
                  pl.BlockSpec((nb, ts, G_KW), lambda b, t: (b, t, 0)),
                  pl.BlockSpec(w3.shape, lambda b, t: (0, 0)),
                  pl.BlockSpec(mk.shape, lambda b, t: (0, 0, 0)),
                  pl.BlockSpec((1, G_W), lambda b, t: (0, 0))],
        out_specs=pl.BlockSpec((nb, ts, G_W), lambda b, t: (b, t, 0)),
        out_shape=jax.ShapeDtypeStruct((B, S, G_W), BF16),
        scratch_shapes=[pltpu.VMEM((nb, 2, 2 * G_DV, LANES), F32)],
        compiler_params=_cparams(2),
        name="gla",
    )(oa3, oa3, oa3, la3, w3, mk, gain)
    return out.reshape(N, G_W)


def _layer_norm(z, g, b):
    mu = jnp.mean(z, axis=-1, keepdims=True)
    zc = z - mu
    var = jnp.mean(zc * zc, axis=-1, keepdims=True)
    return zc * lax.rsqrt(var + LN_EPS) * g + b


def _outproj_kernel(hm_ref, hg_ref, wf_ref, x_ref, mod_ref, g_ref, b_ref, wr_ref, br_ref,
                    x1_ref, u2_ref, rrow_ref, w_ref, *, tb, nh):
    @pl.when(pl.program_id(0) == 0)
    def _():
        w_ref[...] = wf_ref[...].astype(BF16)

    mod = mod_ref[0]
    blocks = [slice(tb * j, tb * (j + 1)) for j in range(nh)]
    y = [jnp.dot(hm_ref[r, :], w_ref[0:M_W, :], preferred_element_type=F32)
         + jnp.dot(hg_ref[r, :], w_ref[M_W:M_W + G_W, :], preferred_element_type=F32) for r in blocks]
    u2 = []
    for j, r in enumerate(blocks):
        z = ALPHA * x_ref[r, :] + (1.0 + mod[2:3, :]) * y[j]
        x1 = _layer_norm(z, g_ref[...], b_ref[...])
        x1_ref[r, :] = x1
        u2.append(x1 * (1.0 + mod[4:5, :]) + mod[3:4, :])
        u2_ref[r, :] = u2[j].astype(BF16)

    u2h = [u.astype(BF16) for u in u2]
    u2l = [(u2[j] - u2h[j].astype(F32)).astype(BF16) for j in range(nh)]
    lh = [jnp.dot(u, wr_ref[...], preferred_element_type=F32) for u in u2h]
    ll = [jnp.dot(u, wr_ref[:, 0:LANES], preferred_element_type=F32) for u in u2l]
    for j in range(nh):
        logits = lh[j][:, 0:LANES] + lh[j][:, LANES:2 * LANES] + ll[j] + br_ref[...]
        rrow_ref[j] = _route_select(logits.T, tb)


def _route_select(lt, tm):
    row = lax.broadcasted_iota(jnp.int32, (SUBLANES, tm), 0)
    gl = jnp.where(row < N_GROUPS, lt[0:SUBLANES, :], -jnp.inf)
    gmax = jnp.max(gl, axis=0, keepdims=True)
    gsel = jnp.min(jnp.where(gl == gmax, row, SUBLANES), axis=0, keepdims=True)
    pg = 1.0 / jnp.sum(jnp.exp(gl - gmax), axis=0, keepdims=True)
    ein = jnp.zeros((SUBLANES, tm), F32)
    for g in range(N_GROUPS):
        ein = jnp.where(gsel == g, lt[SUBLANES * (g + 1):SUBLANES * (g + 2), :], ein)
    v1 = jnp.max(ein, axis=0, keepdims=True)
    i1 = jnp.min(jnp.where(ein == v1, row, SUBLANES), axis=0, keepdims=True)
    rest = jnp.where(row == i1, -jnp.inf, ein)
    v2 = jnp.max(rest, axis=0, keepdims=True)
    i2 = jnp.min(jnp.where(rest == v2, row, SUBLANES), axis=0, keepdims=True)
    t2 = jnp.exp(v2 - v1)
    p1 = 1.0 / (1.0 + t2)
    e0 = (gsel * E_PER_G + i1).astype(F32)
    e1 = (gsel * E_PER_G + i2).astype(F32)
    return jnp.concatenate([e0, e1, pg * p1, pg * (t2 * p1), jnp.zeros((SUBLANES - 4, tm), F32)], axis=0)


def _outproj(hm, hg, w_out, x2, mod3, g, b, wr, br, *, S, tb, nh):
    N, D = x2.shape
    tm = tb * nh
    tpb = S // tm
    kern = functools.partial(_outproj_kernel, tb=tb, nh=nh)
    return pl.pallas_call(
        kern,
        grid=(N // tm,),
        in_specs=[pl.BlockSpec((tm, M_W), lambda i: (i, 0)),
                  pl.BlockSpec((tm, G_W), lambda i: (i, 0)),
                  pl.BlockSpec((M_W + G_W, D), lambda i: (0, 0), pipeline_mode=pl.Buffered(1)),
                  pl.BlockSpec((tm, D), lambda i: (i, 0)),
                  pl.BlockSpec((1, 6, D), lambda i: (i // tpb, 0, 0)),
                  pl.BlockSpec((1, D), lambda i: (0, 0)),
                  pl.BlockSpec((1, D), lambda i: (0, 0)),
                  pl.BlockSpec((D, 2 * LANES), lambda i: (0, 0)),
                  pl.BlockSpec((1, LANES), lambda i: (0, 0))],
        out_specs=[pl.BlockSpec((tm, D), lambda i: (i, 0)),
                   pl.BlockSpec((tm, D), lambda i: (i, 0)),
                   pl.BlockSpec((nh, SUBLANES, tb), lambda i: (i, 0, 0))],
        out_shape=[jax.ShapeDtypeStruct((N, D), F32),
                   jax.ShapeDtypeStruct((N, D), BF16),
                   jax.ShapeDtypeStruct((N // tb, SUBLANES, tb), F32)],
        scratch_shapes=[pltpu.VMEM((M_W + G_W, D), BF16)],
        compiler_params=_cparams(),
        name="outproj",
    )(hm, hg, w_out, x2, mod3, g, b, wr, br)


def _slots_per_tile(tb):
    worst = 2 * tb + N_EXP * (GRAN - 1)
    return -(-worst // LANES) * LANES


def _ffn_tiles(n_tok, tb):
    worst_rows = 2 * n_tok + (n_tok // tb) * N_EXP * (GRAN - 1)
    return -(-worst_rows // FFN_TM) + N_EXP


def _route_kernel(rr_ref, u_ref, lt_ref, srow_ref, col_ref, gd_ref, meta_ref, mg_ref, part_ref,
                  *, NT, tb, TM):
    iota_e = lax.broadcasted_iota(jnp.int32, (N_EXP, tb), 0).astype(F32)
    glane = lax.broadcasted_iota(jnp.int32, (N_EXP, LANES), 1).astype(F32)
    ltri = lt_ref[...]

    def prefix_e(col):
        return jnp.dot(ltri, jnp.broadcast_to(col, (N_EXP, LANES)),
                       preferred_element_type=F32, precision=HIGHEST)[:, 0:1]

    def p1(j, run8):
        r = rr_ref[j]
        oh0 = jnp.where(iota_e == r[0:1, :], 1.0, 0.0)
        oh1 = jnp.where(iota_e == r[1:2, :], 1.0, 0.0)
        cum0 = jnp.dot(oh0.astype(BF16), u_ref[...], preferred_element_type=F32)
        cum1 = jnp.dot(oh1.astype(BF16), u_ref[...], preferred_element_type=F32)
        c0 = jnp.sum(oh0, axis=1, keepdims=True)
        n8 = jnp.floor((c0 + jnp.sum(oh1, axis=1, keepdims=True) + (GRAN - 1.0)) * (1.0 / GRAN))
        lo8 = prefix_e(n8)
        s0 = jnp.sum(oh0 * (GRAN * lo8 + cum0 - 1.0), axis=0, keepdims=True)
        s1 = jnp.sum(oh1 * (GRAN * lo8 + c0 + cum1 - 1.0), axis=0, keepdims=True)
        info = jnp.concatenate([s0, s1, r[2:4, :], jnp.zeros((SUBLANES - 4, tb), F32)], axis=0)
        srow_ref[j] = info
        col_ref[pl.ds(pl.multiple_of(j * tb, tb), tb), :] = jnp.concatenate(
            [info, jnp.zeros((LANES - SUBLANES, tb), F32)], axis=0).T
        mg = jnp.where((lo8 <= glane) & (glane < lo8 + n8), 1.0, 0.0)
        mg_ref[j] = mg
        part = jnp.sum(mg * (run8 + glane - lo8), axis=0, keepdims=True)
        gcnt = jnp.broadcast_to(jnp.sum(n8, axis=0, keepdims=True), (1, LANES))
        part_ref[j] = jnp.concatenate([part, gcnt, jnp.zeros((SUBLANES - 2, LANES), F32)], axis=0)
        return run8 + n8

    tot8 = lax.fori_loop(0, NT, p1, jnp.zeros((N_EXP, 1), F32), unroll=4 if NT % 4 == 0 else 1)
    seg_t = jnp.floor((tot8 * GRAN + (TM - 1.0)) * (1.0 / TM))
    base_t = prefix_e(seg_t)
    base8 = base_t * (TM // GRAN)
    lane1 = lax.broadcasted_iota(jnp.int32, (1, LANES), 1)

    def p2(j, carry):
        pr = part_ref[j]
        dst = (pr[0:1, :] + jnp.sum(mg_ref[j] * base8, axis=0, keepdims=True)) * GRAN
        gd_ref[j] = jnp.where(lane1 == G_LAST, pr[1:2, :], dst).astype(jnp.int32)
        return carry

    lax.fori_loop(0, NT, p2, 0, unroll=4 if NT % 4 == 0 else 1)
    eye = jnp.where(glane == lax.broadcasted_iota(jnp.int32, (N_EXP, LANES), 0).astype(F32), 1.0, 0.0)
    tail_row = jnp.sum(eye * ((base8 + tot8) * GRAN), axis=0, keepdims=True)
    tail_n8 = jnp.sum(eye * (seg_t * (TM // GRAN) - tot8), axis=0, keepdims=True)
    nv_l = jnp.broadcast_to(jnp.sum(seg_t, axis=0, keepdims=True), (1, LANES))
    gd_ref[NT] = jnp.where(lane1 == G_LAST, nv_l, tail_row).astype(jnp.int32)
    gd_ref[NT + 1] = tail_n8.astype(jnp.int32)
    ti = lax.broadcasted_iota(jnp.int32, (N_EXP, tb), 1).astype(F32)
    te = jnp.sum(jnp.where(base_t <= ti, 1.0, 0.0), axis=0, keepdims=True) - 1.0
    nv = jnp.broadcast_to(jnp.sum(seg_t, axis=0, keepdims=True), (1, tb))
    meta_ref[...] = jnp.concatenate([te, nv, jnp.zeros((SUBLANES - 2, tb), F32)],
                                    axis=0).astype(jnp.int32)


def _route(rrow, u_cnt, ltri, *, TM):
    NT, _, tb = rrow.shape
    kern = functools.partial(_route_kernel, NT=NT, tb=tb, TM=TM)
    full3 = lambda i: (0, 0, 0)
    return pl.pallas_call(
        kern,
        grid=(1,),
        in_specs=[pl.BlockSpec((NT, SUBLANES, tb), full3),
                  pl.BlockSpec((tb, tb), lambda i: (0, 0)),
                  pl.BlockSpec((N_EXP, N_EXP), lambda i: (0, 0))],
        out_specs=[pl.BlockSpec((NT, SUBLANES, tb), full3),
                   pl.BlockSpec((NT * tb, LANES), lambda i: (0, 0)),
                   pl.BlockSpec((NT + 2, 1, LANES), full3),
                   pl.BlockSpec((SUBLANES, tb), lambda i: (0, 0))],
        out_shape=[jax.ShapeDtypeStruct((NT, SUBLANES, tb), F32),
                   jax.ShapeDtypeStruct((NT * tb, LANES), F32),
                   jax.ShapeDtypeStruct((NT + 2, 1, LANES), jnp.int32),
                   jax.ShapeDtypeStruct((SUBLANES, tb), jnp.int32)],
        scratch_shapes=[pltpu.VMEM((NT, N_EXP, LANES), F32), pltpu.VMEM((NT, SUBLANES, LANES), F32)],
        compiler_params=_cparams(),
        name="route",
    )(rrow, u_cnt, ltri)


_HI_MASK = 0xFFFF0000


def _pack_halves(x):
    c = x.shape[1] // 2
    lo = lax.bitcast_convert_type(x[:, :c], U32)
    hi = lax.bitcast_convert_type(x[:, c:], U32)
    return (lo >> 16) | (hi & U32(_HI_MASK))


def _unpack_halves(w):
    lo = lax.bitcast_convert_type(w << 16, F32)
    hi = lax.bitcast_convert_type(w & U32(_HI_MASK), F32)
    return jnp.concatenate([lo, hi], axis=1).astype(BF16)


def _granule_copy(src_ref, src_row, dst_ref, dst_row, sem):
    return pltpu.make_async_copy(src_ref.at[pl.ds(src_row, GRAN), :], dst_ref.at[pl.ds(dst_row, GRAN), :], sem)


def _for_granules(n, body, unroll=4):
    def blk(i, carry):
        for t in range(unroll):
            body(i * unroll + t)
        return carry

    def one(g, carry):
        body(g)
        return carry

    nblk = n // unroll
    lax.fori_loop(0, nblk, blk, 0)
    lax.fori_loop(nblk * unroll, n, one, 0)


def _wait_granules(n, src_ref, dst_ref, sem, n_max):
    b = 1
    while b <= n_max:
        @pl.when((n & b) != 0)
        def _(b=b):
            pltpu.make_async_copy(src_ref.at[pl.ds(0, b * GRAN), :], dst_ref.at[pl.ds(0, b * GRAN), :],
                                  sem).wait()
        b *= 2


def _dispatch_kernel(gd_ref, srow_ref, u_ref, xs_ref, buf, zbuf, sems, *, NT, SL, TM, n_tiles):
    j = pl.program_id(0)
    slot = j % 2
    zsem = sems.at[2]

    def drain(tile, sl):
        _wait_granules(gd_ref[tile, G_LAST], buf.at[sl], xs_ref, sems.at[sl], SL // GRAN)

    def tile_fill(t):
        return pltpu.make_async_copy(zbuf, xs_ref.at[pl.ds(pl.multiple_of(t * TM, TM), TM), :], zsem)

    def zero_fill(wait):
        for e in range(N_EXP):
            n, row0 = gd_ref[NT + 1, e], gd_ref[NT, e]
            b = TM // GRAN // 2
            while b >= 1:
                @pl.when((n & b) != 0)
                def _(b=b, n=n, row0=row0):
                    start = pl.multiple_of(row0 + ((n >> b.bit_length()) << b.bit_length()) * GRAN, GRAN)
                    cp = pltpu.make_async_copy(zbuf.at[pl.ds(0, b * GRAN), :],
                                               xs_ref.at[pl.ds(start, b * GRAN), :], zsem)
                    cp.wait() if wait else cp.start()
                b //= 2

        def zt(t, carry):
            tile_fill(t).wait() if wait else tile_fill(t).start()
            return carry
        lax.fori_loop(gd_ref[NT, G_LAST], n_tiles, zt, 0)

    @pl.when(j == 0)
    def _():
        zbuf[...] = jnp.zeros_like(zbuf)
        zero_fill(False)

    @pl.when(j >= 2)
    def _():
        drain(j - 2, slot)

    s = srow_ref[0]
    rows = lax.broadcasted_iota(jnp.int32, (SL, s.shape[1]), 0).astype(F32)
    m0 = rows == s[0:1, :]
    m1 = rows == s[1:2, :]
    oh = jnp.where(m0 | m1, 1.0, 0.0).astype(BF16)
    dw = u_ref.shape[1] // 2
    buf[slot, :, 0:dw] = _pack_halves(jnp.dot(oh, u_ref[...], preferred_element_type=F32))
    wrow = jnp.sum(jnp.where(m0, s[2:3, :], 0.0) + jnp.where(m1, s[3:4, :], 0.0), axis=1, keepdims=True)
    buf[slot, :, dw:dw + LANES] = lax.bitcast_convert_type(jnp.broadcast_to(wrow, (SL, LANES)), U32)

    def issue(g):
        _granule_copy(buf.at[slot], pl.multiple_of(g * GRAN, GRAN), xs_ref,
                      pl.multiple_of(gd_ref[j, g], GRAN), sems.at[slot]).start()

    _for_granules(gd_ref[j, G_LAST], issue)

    @pl.when(j == NT - 1)
    def _():
        drain(j, slot)
        if NT > 1:
            drain(j - 1, 1 - slot)
        zero_fill(True)


def _dispatch(gd, srow, u2, *, n_tiles, TM):
    N, D = u2.shape
    NT, _, tb = srow.shape
    SL = _slots_per_tile(tb)
    n_rows = n_tiles * TM
    kern = functools.partial(_dispatch_kernel, NT=NT, SL=SL, TM=TM, n_tiles=n_tiles)
    grid_spec = pltpu.PrefetchScalarGridSpec(
        num_scalar_prefetch=1,
        grid=(NT,),
        in_specs=[pl.BlockSpec((1, SUBLANES, tb), lambda j, gd: (j, 0, 0)),
                  pl.BlockSpec((tb, D), lambda j, gd: (j, 0))],
        out_specs=pl.BlockSpec(memory_space=pl.ANY),
        scratch_shapes=[pltpu.VMEM((2, SL, D // 2 + LANES), U32), pltpu.VMEM((TM, D // 2 + LANES), U32),
                        pltpu.SemaphoreType.DMA((3,))],
    )
    return pl.pallas_call(
        kern,
        grid_spec=grid_spec,
        out_shape=jax.ShapeDtypeStruct((n_rows, D // 2 + LANES), U32),
        compiler_params=_cparams(),
        name="dispatch",
    )(gd, srow, u2)


def _ffn_kernel(te_ref, nv_ref, xs_ref, wg_ref, wu_ref, wd_ref, o_ref, wgb, wub, wdb, sg, su, sd, slot_ref,
                sems):
    i = pl.program_id(0)
    nv = nv_ref[0]
    e = te_ref[i]

    def weight_copies(ex, sl):
        return (pltpu.make_async_copy(wg_ref.at[ex], sg.at[sl], sems.at[sl]),
                pltpu.make_async_copy(wu_ref.at[ex], su.at[sl], sems.at[sl]),
                pltpu.make_async_copy(wd_ref.at[ex], sd.at[sl], sems.at[sl]))

    @pl.when(i == 0)
    def _():
        slot_ref[0] = 0
        for cp in weight_copies(e, 0):
            cp.start()

    @pl.when((i < nv) & ((i == 0) | (e != te_ref[jnp.maximum(i - 1, 0)])))
    def _():
        sl = slot_ref[0]
        for cp in weight_copies(e, sl):
            cp.wait()
        wgb[...] = sg[sl].astype(BF16)
        wub[...] = su[sl].astype(BF16)
        wdb[...] = sd[sl].astype(BF16)
        nxt = lax.while_loop(lambda t: (t < nv) & (te_ref[jnp.minimum(t, nv - 1)] == e), lambda t: t + 1, i + 1)

        @pl.when(nxt < nv)
        def _():
            for cp in weight_copies(te_ref[nxt], 1 - sl):
                cp.start()
        slot_ref[0] = 1 - sl

    @pl.when(i < nv)
    def _():
        nsub = FFN_SUB
        hm = xs_ref.shape[0] // nsub
        dw = o_ref.shape[1]
        halves = tuple(slice(hm * j, hm * (j + 1)) for j in range(nsub))
        x = [_unpack_halves(xs_ref[r, 0:dw]) for r in halves]
        g = [jnp.dot(x[j], wgb[...], preferred_element_type=F32) for j in range(nsub)]
        u = [jnp.dot(x[j], wub[...], preferred_element_type=F32) for j in range(nsub)]
        h = [(g[j] * _sigmoid(g[j]) * u[j]).astype(BF16) for j in range(nsub)]
        y = [jnp.dot(h[j], wdb[...], preferred_element_type=F32) for j in range(nsub)]
        for j in range(nsub):
            wt = lax.bitcast_convert_type(xs_ref[halves[j], dw:dw + LANES], F32)
            yw = y[j] * jnp.concatenate([wt] * (2 * dw // LANES), axis=1)
            o_ref[halves[j], :] = _pack_halves(yw.astype(BF16).astype(F32))

    @pl.when(i >= nv_ref[0])
    def _():
        o_ref[...] = jnp.zeros_like(o_ref)


def _ffn(te, nv, xs, wg, wu, wd, *, TM):
    P, XW = xs.shape
    DW = XW - LANES
    D = 2 * DW
    n_tiles = P // TM
    grid_spec = pltpu.PrefetchScalarGridSpec(
        num_scalar_prefetch=2,
        grid=(n_tiles,),
        in_specs=[pl.BlockSpec((TM, XW), lambda i, te, nv: (jnp.maximum(jnp.minimum(i, nv[0] - 1), 0), 0)),
                  pl.BlockSpec(memory_space=pl.ANY),
                  pl.BlockSpec(memory_space=pl.ANY),
                  pl.BlockSpec(memory_space=pl.ANY)],
        out_specs=pl.BlockSpec((TM, DW), lambda i, te, nv: (i, 0)),
        scratch_shapes=[pltpu.VMEM((D, D_EXP), BF16), pltpu.VMEM((D, D_EXP), BF16),
                        pltpu.VMEM((D_EXP, D), BF16),
                        pltpu.VMEM((2, D, D_EXP), F32), pltpu.VMEM((2, D, D_EXP), F32),
                        pltpu.VMEM((2, D_EXP, D), F32), pltpu.SMEM((1,), jnp.int32),
                        pltpu.SemaphoreType.DMA((2,))],
    )
    return pl.pallas_call(
        _ffn_kernel,
        grid_spec=grid_spec,
        out_shape=jax.ShapeDtypeStruct((P, DW), U32),
        compiler_params=_cparams(),
        name="ffn",
    )(te, nv, xs, wg, wu, wd)


def _combine_kernel(gd_ref, ys_ref, col_ref, x1_ref, mod_ref, g_ref, b_ref, o_ref, buf, sems, *, NT, SL):
    j = pl.program_id(0)
    slot = j % 2

    def fetch(tile, sl):
        def f(g):
            _granule_copy(ys_ref, pl.multiple_of(gd_ref[tile, g], GRAN), buf.at[sl],
                          pl.multiple_of(g * GRAN, GRAN), sems.at[sl]).start()
        _for_granules(gd_ref[tile, G_LAST], f)

    @pl.when(j == 0)
    def _():
        fetch(0, 0)

    @pl.when(j + 1 < NT)
    def _():
        fetch(j + 1, 1 - slot)

    ng = gd_ref[j, G_LAST]

    _wait_granules(ng, ys_ref, buf.at[slot], sems.at[slot], SL // GRAN)

    rows = lax.broadcasted_iota(jnp.int32, (SL, 1), 0)
    yb = _unpack_halves(jnp.where(rows < ng * GRAN, buf[slot], U32(0)))
    col = col_ref[...]
    tb = col.shape[0]
    lanes = lax.broadcasted_iota(jnp.int32, (tb, SL), 1).astype(F32)
    sel = jnp.where((lanes == col[:, 0:1]) | (lanes == col[:, 1:2]), 1.0, 0.0).astype(BF16)
    y = jnp.dot(sel, yb, preferred_element_type=F32)
    mod = mod_ref[0]
    z = ALPHA * x1_ref[...] + (1.0 + mod[5:6, :]) * y
    o_ref[...] = _layer_norm(z, g_ref[...], b_ref[...])


def _combine(gd, ys, col, x1, mod3, g, b, *, S, tb):
    N, D = x1.shape
    NT = N // tb
    tpb = S // tb
    SL = _slots_per_tile(tb)
    kern = functools.partial(_combine_kernel, NT=NT, SL=SL)
    grid_spec = pltpu.PrefetchScalarGridSpec(
        num_scalar_prefetch=1,
        grid=(NT,),
        in_specs=[pl.BlockSpec(memory_space=pl.ANY),
                  pl.BlockSpec((tb, LANES), lambda j, gd: (j, 0)),
                  pl.BlockSpec((tb, D), lambda j, gd: (j, 0)),
                  pl.BlockSpec((1, 6, D), lambda j, gd: (j // tpb, 0, 0)),
                  pl.BlockSpec((1, D), lambda j, gd: (0, 0)),
                  pl.BlockSpec((1, D), lambda j, gd: (0, 0))],
        out_specs=pl.BlockSpec((tb, D), lambda j, gd: (j, 0)),
        scratch_shapes=[pltpu.VMEM((2, SL, D // 2), U32), pltpu.SemaphoreType.DMA((2,))],
    )
    return pl.pallas_call(
        kern,
        grid_spec=grid_spec,
        out_shape=jax.ShapeDtypeStruct((N, D), F32),
        compiler_params=_cparams(),
        name="combine",
    )(gd, ys, col, x1, mod3, g, b)


def _layer(x, c, l, w_ada, b_ada, w_in, w_conv, b_conv, b_igate, b_fgate, mlstm_norm_g, w_gla_a, b_gla_a,
           gla_norm_g, w_out, ln1_g, ln1_b, w_route_group, b_route_group, w_route_expert, b_route_expert,
           w_gate, w_up, w_down, ln2_g, ln2_b):
    B, S, D = x.shape
    N = B * S
    x2 = x.reshape(N, D)
    tm_in = min(512, S)
    tm = min(256, S)
    lm = min(256, S)

    mod3 = _ada(c, w_ada[l], b_ada[l]).reshape(B, 6, D)

    wa_pad = jnp.zeros((LANES, G_KW), F32).at[SM_A:SM_A + G_RANK].set(w_gla_a[l]).astype(BF16)
    bg = (jnp.zeros((2 * SUBLANES, 1), F32).at[0:M_HEADS, 0].set(b_igate[l])
          .at[SUBLANES:SUBLANES + M_HEADS, 0].set(b_fgate[l]))
    oa, la, g3 = _inproj(x2, mod3, jnp.swapaxes(w_in, 1, 2), w_conv[l], b_conv[l].reshape(1, -1), wa_pad,
                         b_gla_a[l].reshape(1, -1), bg, S=S, tm=tm_in, lm=lm, layer=l)

    u_tri = jnp.asarray(np.triu(np.ones((lm, lm), np.float32)))
    nb = 4 if B % 4 == 0 else (2 if B % 2 == 0 else 1)
    ts = min(512, S)
    hm = _mlstm(oa, g3, u_tri, mlstm_norm_g[l].reshape(1, -1), B=B, S=S, L=lm, nb=nb, ts=ts)
    w3_np, mk_np = _gla_consts()
    hg = _gla(oa, la, jnp.asarray(w3_np, BF16), jnp.asarray(mk_np), gla_norm_g[l].reshape(1, -1), B=B, S=S,
              nb=nb, ts=ts)

    wr = (jnp.zeros((D, LANES), F32).at[:, 0:N_GROUPS].set(w_route_group[l])
          .at[:, SUBLANES:SUBLANES + N_EXP].set(w_route_expert[l]))
    br = (jnp.zeros((1, LANES), F32).at[0, 0:N_GROUPS].set(b_route_group[l])
          .at[0, SUBLANES:SUBLANES + N_EXP].set(b_route_expert[l]))
    wr_hi = wr.astype(BF16)
    wr2 = jnp.concatenate([wr_hi, (wr - wr_hi.astype(F32)).astype(BF16)], axis=1)
    x1, u2, rrow = _outproj(hm, hg, w_out[l], x2, mod3, ln1_g[l].reshape(1, -1),
                            ln1_b[l].reshape(1, -1), wr2, br, S=S, tb=tm, nh=4 if S % (4 * tm) == 0 else 1)

    u_cnt = jnp.asarray(np.triu(np.ones((tm, tm), np.float32)), BF16)
    ltri = jnp.asarray(np.tril(np.ones((N_EXP, N_EXP), np.float32), -1))
    srow, col, gd3, meta = _route(rrow, u_cnt, ltri, TM=FFN_TM)
    gd = gd3.reshape(N // tm + 2, LANES)
    n_tiles = _ffn_tiles(N, tm)
    te, nv = meta[0, :n_tiles], meta[1, 0:1]

    xs = _dispatch(gd, srow, u2, n_tiles=n_tiles, TM=FFN_TM)
    ys = _ffn(te, nv, xs, w_gate[l], w_up[l], w_down[l], TM=FFN_TM)
    out = _combine(gd, ys, col, x1, mod3, ln2_g[l].reshape(1, -1), ln2_b[l].reshape(1, -1), S=S, tb=tm)
    return out.reshape(B, S, D)


def kernel(x, c, w_ada, b_ada, w_in, w_conv, b_conv, b_igate, b_fgate, mlstm_norm_g, w_gla_a, b_gla_a,
           gla_norm_g, w_out, ln1_g, ln1_b, w_route_group, b_route_group, w_route_expert, b_route_expert,
           w_gate, w_up, w_down, ln2_g, ln2_b):
    for l in range(DEPTH):
        x = _layer(x, c, l, w_ada, b_ada, w_in, w_conv, b_conv, b_igate, b_fgate, mlstm_norm_g, w_gla_a,
                   b_gla_a, gla_norm_g, w_out, ln1_g, ln1_b, w_route_group, b_route_group, w_route_expert,
                   b_route_expert, w_gate, w_up, w_down, ln2_g, ln2_b)
    return x
```

```python
import functools

import numpy as np
import jax
import jax.numpy as jnp
from jax import lax
from jax.experimental import pallas as pl
from jax.experimental.pallas import tpu as pltpu

F32 = jnp.float32
BF16 = jnp.bfloat16
U32 = jnp.uint32
HIGHEST = lax.Precision.HIGHEST

DEPTH = 1
M_HEADS = 4
M_HD = 128
M_W = M_HEADS * M_HD
CONV_W = 4
G_HEADS = 4
G_DK = 64
G_DV = 128
G_W = G_HEADS * G_DV
G_KW = G_HEADS * G_DK
G_RANK = 16
G_TAU = 16.0
G_CHUNK = 64
N_GROUPS = 4
E_PER_G = 8
N_EXP = N_GROUPS * E_PER_G
D_EXP = 512
ALPHA = (2 * DEPTH) ** 0.25
LN_EPS = 1e-5

LANES = 128
SUBLANES = 8
VMEM_LIMIT = 48 * 1024 * 1024

C_QK = 0
C_VO = 1024
C_GQK = 2048
C_GV = 2560
C_GG = 3072
C_SMALL = 3584
C_TOT = 3712
SM_I, SM_F, SM_A = 0, 8, 16
IN_GATES = 4 * M_W
IN_G = IN_GATES + 2 * M_HEADS
IN_GA = IN_G + 2 * G_KW + 2 * G_W
IN_TOT = IN_GA + G_RANK

FFN_TM = 512
FFN_SUB = 2
GRAN = SUBLANES
G_LAST = LANES - 1


def _cparams(n_axes=1):
    return pltpu.CompilerParams(dimension_semantics=("arbitrary",) * n_axes,
                                vmem_limit_bytes=VMEM_LIMIT)


def _sigmoid(x):
    return 1.0 / (1.0 + jnp.exp(-x))


def _log_sigmoid(x):
    return jnp.minimum(x, 0.0) - jnp.log(1.0 + jnp.exp(-jnp.abs(x)))


def _ada_kernel(c_ref, w_ref, b_ref, o_ref):
    c = c_ref[...]
    ca = c * _sigmoid(c)
    o_ref[...] = jnp.dot(ca, w_ref[...], preferred_element_type=F32, precision=HIGHEST) + b_ref[...]


def _ada(c, w, b):
    B, D = c.shape
    n_out = w.shape[1]
    tn = 1024
    return pl.pallas_call(
        _ada_kernel,
        grid=(n_out // tn,),
        in_specs=[pl.BlockSpec((B, D), lambda j: (0, 0)),
                  pl.BlockSpec((D, tn), lambda j: (0, j)),
                  pl.BlockSpec((1, tn), lambda j: (0, j))],
        out_specs=pl.BlockSpec((B, tn), lambda j: (0, j)),
        out_shape=jax.ShapeDtypeStruct((B, n_out), F32),
        compiler_params=_cparams(),
        name="ada",
    )(c, w, b.reshape(1, n_out))


def _inproj_kernel(x_ref, mod_ref, win_ref, wc_ref, bc_ref, wa_ref, ba_ref, bg_ref,
                   oa_ref, la_ref, g_ref, halo_ref, w_ref, *, tm, tpb, lm):
    i = pl.program_id(0)

    @pl.when(i == 0)
    def _():
        rc = 2 * LANES
        for r in range(0, IN_GATES, rc):
            w_ref[:, r:r + rc] = win_ref[0, r:r + rc, :].T.astype(BF16)
        for r in range(0, C_SMALL - C_GQK, rc):
            w_ref[:, C_GQK + r:C_GQK + r + rc] = win_ref[0, IN_G + r:IN_G + r + rc, :].T.astype(BF16)
        gates = win_ref[0, IN_GATES:IN_G, :]
        z = lambda n: jnp.zeros((n, gates.shape[1]), F32)
        small = jnp.concatenate([gates[0:M_HEADS], z(SM_F - M_HEADS), gates[M_HEADS:2 * M_HEADS],
                                 z(SM_A - SM_F - M_HEADS), win_ref[0, IN_GA:IN_TOT, :],
                                 z(LANES - SM_A - G_RANK)], axis=0)
        w_ref[:, C_SMALL:C_TOT] = small.T.astype(BF16)

    @pl.when(i % tpb == 0)
    def _():
        halo_ref[0:SUBLANES, :] = jnp.zeros((SUBLANES, halo_ref.shape[1]), F32)

    mod = mod_ref[0]
    u = (x_ref[...] * (1.0 + mod[1:2, :]) + mod[0:1, :]).astype(BF16)

    def proj(c0, c1):
        return jnp.dot(u, w_ref[:, c0:c1], preferred_element_type=F32)

    p = proj(C_QK, C_QK + 2 * M_W)
    halo_ref[SUBLANES:SUBLANES + tm, :] = p
    acc = bc_ref[...] + wc_ref[CONV_W - 1:CONV_W, :] * p
    for j in range(CONV_W - 1):
        acc = acc + wc_ref[j:j + 1, :] * halo_ref[pl.ds(SUBLANES - (CONV_W - 1) + j, tm), :]
    halo_ref[0:SUBLANES, :] = p[tm - SUBLANES:, :]
    qk = acc * _sigmoid(acc)
    oa_ref[:, C_QK:C_QK + M_W] = qk[:, :M_W].astype(BF16)
    oa_ref[:, C_QK + M_W:C_QK + 2 * M_W] = (qk[:, M_W:] * (M_HD ** -0.5)).astype(BF16)

    p = proj(C_VO, C_VO + 2 * M_W)
    oa_ref[:, C_VO:C_VO + 2 * M_W] = p.astype(BF16)

    p = proj(C_GQK, C_GQK + G_KW)
    oa_ref[:, C_GQK:C_GQK + G_KW] = (p * (G_DK ** -0.5)).astype(BF16)
    p = proj(C_GQK + G_KW, C_SMALL)
    oa_ref[:, C_GQK + G_KW:C_SMALL] = p.astype(BF16)

    ps = proj(C_SMALL, C_TOT)
    la = jnp.dot(ps.astype(BF16), wa_ref[...], preferred_element_type=F32) + ba_ref[...]
    la_ref[...] = _log_sigmoid(la) * (1.0 / G_TAU)
    pt = ps.T
    gi = pt[SM_I:SM_I + SUBLANES, :] + bg_ref[0:SUBLANES, :]
    gf = _log_sigmoid(pt[SM_F:SM_F + SUBLANES, :] + bg_ref[SUBLANES:2 * SUBLANES, :])
    for j in range(tm // lm):
        g_ref[j, 0:SUBLANES, :] = gi[:, j * lm:(j + 1) * lm]
        g_ref[j, SUBLANES:2 * SUBLANES, :] = gf[:, j * lm:(j + 1) * lm]


def _inproj(x2, mod3, w_in, w_conv, b_conv, wa_pad, b_gla, bg, *, S, tm, lm, layer):
    N, D = x2.shape
    tpb = S // tm
    kern = functools.partial(_inproj_kernel, tm=tm, tpb=tpb, lm=lm)
    return pl.pallas_call(
        kern,
        grid=(N // tm,),
        in_specs=[pl.BlockSpec((tm, D), lambda i: (i, 0)),
                  pl.BlockSpec((1, 6, D), lambda i: (i // tpb, 0, 0)),
                  pl.BlockSpec((1, IN_TOT, D), lambda i: (layer, 0, 0), pipeline_mode=pl.Buffered(1)),
                  pl.BlockSpec((CONV_W, 2 * M_W), lambda i: (0, 0)),
                  pl.BlockSpec((1, 2 * M_W), lambda i: (0, 0)),
                  pl.BlockSpec((LANES, G_KW), lambda i: (0, 0)),
                  pl.BlockSpec((1, G_KW), lambda i: (0, 0)),
                  pl.BlockSpec((2 * SUBLANES, 1), lambda i: (0, 0))],
        out_specs=[pl.BlockSpec((tm, C_SMALL), lambda i: (i, 0)),
                   pl.BlockSpec((tm, G_KW), lambda i: (i, 0)),
                   pl.BlockSpec((tm // lm, 2 * SUBLANES, lm), lambda i: (i, 0, 0))],
        out_shape=[jax.ShapeDtypeStruct((N, C_SMALL), BF16),
                   jax.ShapeDtypeStruct((N, G_KW), F32),
                   jax.ShapeDtypeStruct((N // lm, 2 * SUBLANES, lm), F32)],
        scratch_shapes=[pltpu.VMEM((SUBLANES + tm, 2 * M_W), F32), pltpu.VMEM((D, C_TOT), BF16)],
        compiler_params=_cparams(),
        name="inproj",
    )(x2, mod3, w_in, w_conv, b_conv, wa_pad, b_gla, bg)


def _mlstm_sel():
    sel = np.zeros((2 * LANES, 2 * M_HEADS * M_HD), np.float32)
    for j in range(2 * M_HEADS):
        src = (SUBLANES if j < M_HEADS else 3 * SUBLANES) + j % M_HEADS
        sel[src, M_HD * j:M_HD * (j + 1)] = 1.0
        sel[LANES + src, M_HD * j:M_HD * (j + 1)] = 1.0
    return sel


def _mlstm_kernel(qk_ref, vo_ref, g_ref, u_ref, gain_ref, sel_ref, out_ref, c_ref, zt_ref, a_ref, dec_ref, m_ref,
                  *, L, NC, nb):
    @pl.when(pl.program_id(1) == 0)
    def _():
        c_ref[...] = jnp.zeros_like(c_ref)
        m_ref[...] = jnp.zeros_like(m_ref)

    tril = (lax.broadcasted_iota(jnp.int32, (L, L), 0) >= lax.broadcasted_iota(jnp.int32, (L, L), 1))
    ones_v = jnp.ones((L, M_HD), BF16)
    zpad = jnp.zeros((LANES - 4 * SUBLANES, L), F32)

    order = [(bi, c) for bi in range(nb) for c in range(NC)]
    f_all = jnp.concatenate([g_ref[bi, c, SUBLANES:2 * SUBLANES, :] for bi, c in order], axis=0)
    i_all = jnp.concatenate([g_ref[bi, c, 0:SUBLANES, :] for bi, c in order], axis=0)
    b_all = jnp.dot(f_all, u_ref[...], preferred_element_type=F32, precision=HIGHEST)
    a_all = i_all - b_all
    lane_all = lax.broadcasted_iota(jnp.int32, a_all.shape, 1)
    g_all = a_all
    s = 1
    while s < L:
        g_all = jnp.maximum(g_all, jnp.where(lane_all >= s, pltpu.roll(g_all, s, 1), -jnp.inf))
        s *= 2
    for bi in range(nb):
        m_prev = m_ref[bi][:, 0:1]
        for c in range(NC):
            ci = bi * NC + c
            r8 = slice(SUBLANES * ci, SUBLANES * (ci + 1))
            a, b = a_all[r8], b_all[r8]
            a_ref[ci] = a
            M = jnp.maximum(g_all[r8], m_prev)
            ML = M[:, L - 1:L]
            Z = jnp.concatenate([M, jnp.exp(m_prev - M), jnp.exp(-(b + M)), jnp.exp(a - ML), zpad],
                                axis=0)
            zt_ref[ci] = Z.T
            dec_ref[ci] = jnp.broadcast_to(jnp.exp(m_prev - ML), (SUBLANES, 2 * M_HD))
            m_prev = b[:, L - 1:L] + ML
        m_ref[bi] = jnp.broadcast_to(m_prev, (SUBLANES, LANES))

    chains = [(bi, h) for bi in range(nb) for h in range(M_HEADS)]
    nt = (((1,), (1,)), ((), ()))
    tn = (((0,), (0,)), ((), ()))

    def chunk(c, carry):
        rows = pl.ds(pl.multiple_of(c * L, L), L)
        Zt = [zt_ref[bi * NC + c] for bi in range(nb)]
        a = [a_ref[bi * NC + c] for bi in range(nb)]
        dec = [dec_ref[bi * NC + c] for bi in range(nb)]
        hs = [slice(h * M_HD, (h + 1) * M_HD) for h in range(M_HEADS)]
        hs2 = [slice(M_W + h * M_HD, M_W + (h + 1) * M_HD) for h in range(M_HEADS)]
        q = [qk_ref[bi, rows, hs[h]] for bi, h in chains]
        k = [qk_ref[bi, rows, hs2[h]] for bi, h in chains]
        vext = [jnp.concatenate([vo_ref[bi, rows, hs[h]], ones_v], axis=1) for bi, h in chains]
        cst = [c_ref[bi * M_HEADS + h] for bi, h in chains]
        n = range(len(chains))
        sc = [lax.dot_general(q[i], k[i], nt, preferred_element_type=F32) for i in n]
        qc = [jnp.dot(q[i], cst[i].astype(BF16), preferred_element_type=F32) for i in n]
        pm = [(sc[i] * jnp.exp(jnp.where(tril, a[bi][h:h + 1, :] - Zt[bi][:, h:h + 1], -jnp.inf))).astype(BF16)
              for i, (bi, h) in enumerate(chains)]
        pv = [jnp.dot(pm[i], vext[i], preferred_element_type=F32) for i in n]
        rep = []
        for bi in range(nb):
            zh = Zt[bi].astype(BF16)
            zl = (Zt[bi] - zh.astype(F32)).astype(BF16)
            rep.append(jnp.dot(jnp.concatenate([zh, zl], axis=1), sel_ref[...], preferred_element_type=F32))
        e_inter = [rep[bi][:, M_HD * h:M_HD * (h + 1)] for bi, h in chains]
        w_state = [rep[bi][:, M_HD * (M_HEADS + h):M_HD * (M_HEADS + h + 1)] for bi, h in chains]
        kw = [(w_state[i] * k[i].astype(F32)).astype(BF16) for i in n]
        upd = [lax.dot_general(kw[i], vext[i], tn, preferred_element_type=F32) for i in n]
        for i, (bi, h) in enumerate(chains):
            c_ref[bi * M_HEADS + h] = dec[bi][h:h + 1, :] * cst[i] + upd[i]
            nd = pv[i] + jnp.concatenate([e_inter[i], e_inter[i]], axis=1) * qc[i]
            hh = nd[:, :M_HD] / jnp.maximum(jnp.abs(nd[:, M_HD:]),
                                            Zt[bi][:, 2 * SUBLANES + h:2 * SUBLANES + h + 1])
            hh = _sigmoid(vo_ref[bi, rows, hs2[h]].astype(F32)) * hh
            hn = hh * lax.rsqrt(jnp.mean(hh * hh, axis=-1, keepdims=True) + LN_EPS)
            out_ref[bi, rows, hs[h]] = (hn * gain_ref[:, hs[h]]).astype(BF16)
        return carry

    lax.fori_loop(0, NC, chunk, 0)


def _mlstm(oa, g3, u_tri, gain, *, B, S, L, nb, ts):
    N = oa.shape[0]
    NC = ts // L
    oa3 = oa.reshape(B, S, oa.shape[1])
    g4 = g3.reshape(B, S // L, 2 * SUBLANES, L)
    sel = jnp.asarray(_mlstm_sel(), BF16)
    kern = functools.partial(_mlstm_kernel, L=L, NC=NC, nb=nb)
    out = pl.pallas_call(
        kern,
        grid=(B // nb, S // ts),
        in_specs=[pl.BlockSpec((nb, ts, 2 * M_W), lambda b, t: (b, t, C_QK // (2 * M_W))),
                  pl.BlockSpec((nb, ts, 2 * M_W), lambda b, t: (b, t, C_VO // (2 * M_W))),
                  pl.BlockSpec((nb, NC, 2 * SUBLANES, L), lambda b, t: (b, t, 0, 0)),
                  pl.BlockSpec((L, L), lambda b, t: (0, 0)),
                  pl.BlockSpec((1, M_W), lambda b, t: (0, 0)),
                  pl.BlockSpec(sel.shape, lambda b, t: (0, 0))],
        out_specs=pl.BlockSpec((nb, ts, M_W), lambda b, t: (b, t, 0)),
        out_shape=jax.ShapeDtypeStruct((B, S, M_W), BF16),
        scratch_shapes=[pltpu.VMEM((nb * M_HEADS, M_HD, 2 * M_HD), F32),
                        pltpu.VMEM((nb * NC, L, LANES), F32),
                        pltpu.VMEM((nb * NC, SUBLANES, L), F32),
                        pltpu.VMEM((nb * NC, SUBLANES, 2 * M_HD), F32),
                        pltpu.VMEM((nb, SUBLANES, LANES), F32)],
        compiler_params=_cparams(2),
        name="mlstm",
    )(oa3, oa3, g4, u_tri, gain, sel)
    return out.reshape(N, M_W)


_G_LEVELS = 6
_G_XROW = 2 * G_CHUNK + SUBLANES


def _gla_consts():
    L = G_CHUNK
    t = np.arange(L)
    blocks = [(t[None, :] <= t[:, None]).astype(np.float32),
              (t[None, :] > t[:, None]).astype(np.float32),
              np.ones((SUBLANES, L), np.float32)]
    masks = [np.eye(L, dtype=np.float32)]
    m = 1
    while m < L:
        wl = np.zeros((L, L), np.float32)
        for r in range(L):
            r0 = (r // (2 * m)) * 2 * m + m
            if r % (2 * m) >= m:
                wl[r, r0:r + 1] = 1.0
            else:
                wl[r, r + 1:r0] = 1.0
        blocks.append(wl)
        tt, ss = t[:, None], t[None, :]
        masks.append(((tt // (2 * m) == ss // (2 * m)) & (tt % (2 * m) >= m)
                      & (ss % (2 * m) < m)).astype(np.float32))
        m *= 2
    w = np.concatenate(blocks, axis=0)
    w3 = np.concatenate([w, w, w], axis=1)
    mk = np.stack([np.concatenate([x] * G_HEADS, axis=0) for x in masks])
    return w3, mk


def _gla_kernel(qk_ref, v_ref, gg_ref, la_ref, w3_ref, mk_ref, gain_ref, out_ref, st_ref, *, NC, nb):
    L = G_CHUNK

    @pl.when(pl.program_id(1) == 0)
    def _():
        st_ref[...] = jnp.zeros_like(st_ref)

    lane_head = lax.broadcasted_iota(jnp.int32, (L, G_KW), 1) // G_DK
    br = lax.broadcasted_iota(jnp.int32, (2 * G_DV, LANES), 0) < G_DV
    bl = lax.broadcasted_iota(jnp.int32, (2 * G_DV, LANES), 1) < G_DK
    bmask = br == bl
    nt = (((1,), (1,)), ((), ()))
    tn = (((0,), (0,)), ((), ()))

    def chunk(c, carry):
        rows = pl.ds(pl.multiple_of(c * L, L), L)
        X, q, k = [], [], []
        for bi in range(nb):
            la = la_ref[bi, rows, :]
            hi = la.astype(BF16)
            r1 = la - hi.astype(F32)
            mid = r1.astype(BF16)
            lo = (r1 - mid.astype(F32)).astype(BF16)
            stk = jnp.concatenate([hi, mid, lo], axis=0)
            X.append(jnp.exp(jnp.dot(w3_ref[...], stk, preferred_element_type=F32)))
            q.append(qk_ref[bi, rows, 0:G_KW].astype(F32))
            k.append(qk_ref[bi, rows, G_KW:2 * G_KW].astype(F32))

        sc = [[None] * (_G_LEVELS + 1) for _ in range(nb)]
        for lev in range(_G_LEVELS + 1):
            for bi in range(nb):
                if lev == 0:
                    qt, kt = q[bi], k[bi]
                else:
                    xl = X[bi][_G_XROW + L * (lev - 1):_G_XROW + L * lev, :]
                    qt, kt = q[bi] * xl, k[bi] * xl
                q4 = jnp.concatenate([jnp.where(lane_head == h, qt, 0.0) for h in range(G_HEADS)],
                                     axis=0).astype(BF16)
                sc[bi][lev] = lax.dot_general(q4, kt.astype(BF16), nt, preferred_element_type=F32)
        Ab = []
        for bi in range(nb):
            A = sc[bi][0] * mk_ref[0]
            for lev in range(1, _G_LEVELS + 1):
                A = A + sc[bi][lev] * mk_ref[lev]
            Ab.append(A.astype(BF16))

        for bi in range(nb):
            gg = gg_ref[bi, rows, :].astype(F32)
            gate = gg * _sigmoid(gg)
            for p in range(2):
                ls = slice(LANES * p, LANES * (p + 1))
                vp = v_ref[bi, rows, 2 * G_DV * p:2 * G_DV * (p + 1)]
                oi = [jnp.dot(Ab[bi][L * (2 * p + hh):L * (2 * p + hh + 1)],
                              vp[:, G_DV * hh:G_DV * (hh + 1)], preferred_element_type=F32)
                      for hh in range(2)]
                st = st_ref[bi, p]
                qc = (q[bi][:, ls] * X[bi][0:L, ls]).astype(BF16)
                o_inter = lax.dot_general(qc, st.astype(BF16), nt, preferred_element_type=F32)
                kc = (k[bi][:, ls] * X[bi][L:2 * L, ls]).astype(BF16)
                upd = lax.dot_general(vp, kc, tn, preferred_element_type=F32)
                dec = X[bi][2 * L:2 * L + 1, ls]
                st_ref[bi, p] = jnp.where(bmask, dec * st + upd, 0.0)
                for hh in range(2):
                    o = o_inter[:, G_DV * hh:G_DV * (hh + 1)] + oi[hh]
                    hn = o * lax.rsqrt(jnp.mean(o * o, axis=-1, keepdims=True) + LN_EPS)
                    hs = slice(G_DV * (2 * p + hh), G_DV * (2 * p + hh + 1))
                    out_ref[bi, rows, hs] = (hn * gain_ref[:, hs] * gate[:, hs]).astype(BF16)
        return carry

    lax.fori_loop(0, NC, chunk, 0)


def _gla(oa, la, w3, mk, gain, *, B, S, nb, ts):
    N = oa.shape[0]
    oa3 = oa.reshape(B, S, oa.shape[1])
    la3 = la.reshape(B, S, G_KW)
    kern = functools.partial(_gla_kernel, NC=ts // G_CHUNK, nb=nb)
    out = pl.pallas_call(
        kern,
        grid=(B // nb, S // ts),
        in_specs=[pl.BlockSpec((nb, ts, 2 * G_KW), lambda b, t: (b, t, C_GQK // (2 * G_KW))),
                  pl.BlockSpec((nb, ts, G_W), lambda b, t: (b, t, C_GV // G_W)),
                  pl.BlockSpec((nb, ts, G_W), lambda b, t: (b, t, C_GG // G_W)),
                  pl.BlockSpec((nb, ts, G_KW), lambda b, t: (b, t, 0)),
                  pl.BlockSpec(w3.shape, lambda b, t: (0, 0)),
                  pl.BlockSpec(mk.shape, lambda b, t: (0, 0, 0)),
                  pl.BlockSpec((1, G_W), lambda b, t: (0, 0))],
        out_specs=pl.BlockSpec((nb, ts, G_W), lambda b, t: (b, t, 0)),
        out_shape=jax.ShapeDtypeStruct((B, S, G_W), BF16),
        scratch_shapes=[pltpu.VMEM((nb, 2, 2 * G_DV, LANES), F32)],
        compiler_params=_cparams(2),
        name="gla",
    )(oa3, oa3, oa3, la3, w3, mk, gain)
    return out.reshape(N, G_W)


def _layer_norm(z, g, b):
    mu = jnp.mean(z, axis=-1, keepdims=True)
    zc = z - mu
    var = jnp.mean(zc * zc, axis=-1, keepdims=True)
    return zc * lax.rsqrt(var + LN_EPS) * g + b


def _outproj_kernel(hm_ref, hg_ref, wf_ref, x_ref, mod_ref, g_ref, b_ref, wr_ref, br_ref,
                    x1_ref, u2_ref, rrow_ref, w_ref, *, tb, nh):
    @pl.when(pl.program_id(0) == 0)
    def _():
        w_ref[...] = wf_ref[...].astype(BF16)

    mod = mod_ref[0]
    blocks = [slice(tb * j, tb * (j + 1)) for j in range(nh)]
    y = [jnp.dot(hm_ref[r, :], w_ref[0:M_W, :], preferred_element_type=F32)
         + jnp.dot(hg_ref[r, :], w_ref[M_W:M_W + G_W, :], preferred_element_type=F32) for r in blocks]
    u2 = []
    for j, r in enumerate(blocks):
        z = ALPHA * x_ref[r, :] + (1.0 + mod[2:3, :]) * y[j]
        x1 = _layer_norm(z, g_ref[...], b_ref[...])
        x1_ref[r, :] = x1
        u2.append(x1 * (1.0 + mod[4:5, :]) + mod[3:4, :])
        u2_ref[r, :] = u2[j].astype(BF16)

    u2h = [u.astype(BF16) for u in u2]
    u2l = [(u2[j] - u2h[j].astype(F32)).astype(BF16) for j in range(nh)]
    lh = [jnp.dot(u, wr_ref[...], preferred_element_type=F32) for u in u2h]
    ll = [jnp.dot(u, wr_ref[:, 0:LANES], preferred_element_type=F32) for u in u2l]
    for j in range(nh):
        logits = lh[j][:, 0:LANES] + lh[j][:, LANES:2 * LANES] + ll[j] + br_ref[...]
        rrow_ref[j] = _route_select(logits.T, tb)


def _route_select(lt, tm):
    row = lax.broadcasted_iota(jnp.int32, (SUBLANES, tm), 0)
    gl = jnp.where(row < N_GROUPS, lt[0:SUBLANES, :], -jnp.inf)
    gmax = jnp.max(gl, axis=0, keepdims=True)
    gsel = jnp.min(jnp.where(gl == gmax, row, SUBLANES), axis=0, keepdims=True)
    pg = 1.0 / jnp.sum(jnp.exp(gl - gmax), axis=0, keepdims=True)
    ein = jnp.zeros((SUBLANES, tm), F32)
    for g in range(N_GROUPS):
        ein = jnp.where(gsel == g, lt[SUBLANES * (g + 1):SUBLANES * (g + 2), :], ein)
    v1 = jnp.max(ein, axis=0, keepdims=True)
    i1 = jnp.min(jnp.where(ein == v1, row, SUBLANES), axis=0, keepdims=True)
    rest = jnp.where(row == i1, -jnp.inf, ein)
    v2 = jnp.max(rest, axis=0, keepdims=True)
    i2 = jnp.min(jnp.where(rest == v2, row, SUBLANES), axis=0, keepdims=True)
    t2 = jnp.exp(v2 - v1)
    p1 = 1.0 / (1.0 + t2)
    e0 = (gsel * E_PER_G + i1).astype(F32)
    e1 = (gsel * E_PER_G + i2).astype(F32)
    return jnp.concatenate([e0, e1, pg * p1, pg * (t2 * p1), jnp.zeros((SUBLANES - 4, tm), F32)], axis=0)


def _outproj(hm, hg, w_out, x2, mod3, g, b, wr, br, *, S, tb, nh):
    N, D = x2.shape
    tm = tb * nh
    tpb = S // tm
    kern = functools.partial(_outproj_kernel, tb=tb, nh=nh)
    return pl.pallas_call(
        kern,
        grid=(N // tm,),
        in_specs=[pl.BlockSpec((tm, M_W), lambda i: (i, 0)),
                  pl.BlockSpec((tm, G_W), lambda i: (i, 0)),
                  pl.BlockSpec((M_W + G_W, D), lambda i: (0, 0), pipeline_mode=pl.Buffered(1)),
                  pl.BlockSpec((tm, D), lambda i: (i, 0)),
                  pl.BlockSpec((1, 6, D), lambda i: (i // tpb, 0, 0)),
                  pl.BlockSpec((1, D), lambda i: (0, 0)),
                  pl.BlockSpec((1, D), lambda i: (0, 0)),
                  pl.BlockSpec((D, 2 * LANES), lambda i: (0, 0)),
                  pl.BlockSpec((1, LANES), lambda i: (0, 0))],
        out_specs=[pl.BlockSpec((tm, D), lambda i: (i, 0)),
                   pl.BlockSpec((tm, D), lambda i: (i, 0)),
                   pl.BlockSpec((nh, SUBLANES, tb), lambda i: (i, 0, 0))],
        out_shape=[jax.ShapeDtypeStruct((N, D), F32),
                   jax.ShapeDtypeStruct((N, D), BF16),
                   jax.ShapeDtypeStruct((N // tb, SUBLANES, tb), F32)],
        scratch_shapes=[pltpu.VMEM((M_W + G_W, D), BF16)],
        compiler_params=_cparams(),
        name="outproj",
    )(hm, hg, w_out, x2, mod3, g, b, wr, br)


def _slots_per_tile(tb):
    worst = 2 * tb + N_EXP * (GRAN - 1)
    return -(-worst // LANES) * LANES


def _ffn_tiles(n_tok, tb):
    worst_rows = 2 * n_tok + (n_tok // tb) * N_EXP * (GRAN - 1)
    return -(-worst_rows // FFN_TM) + N_EXP


def _route_kernel(rr_ref, u_ref, lt_ref, srow_ref, col_ref, gd_ref, meta_ref, mg_ref, part_ref,
                  *, NT, tb, TM):
    iota_e = lax.broadcasted_iota(jnp.int32, (N_EXP, tb), 0).astype(F32)
    glane = lax.broadcasted_iota(jnp.int32, (N_EXP, LANES), 1).astype(F32)
    ltri = lt_ref[...]

    def prefix_e(col):
        return jnp.dot(ltri, jnp.broadcast_to(col, (N_EXP, LANES)),
                       preferred_element_type=F32, precision=HIGHEST)[:, 0:1]

    def p1(j, run8):
        r = rr_ref[j]
        oh0 = jnp.where(iota_e == r[0:1, :], 1.0, 0.0)
        oh1 = jnp.where(iota_e == r[1:2, :], 1.0, 0.0)
        cum0 = jnp.dot(oh0.astype(BF16), u_ref[...], preferred_element_type=F32)
        cum1 = jnp.dot(oh1.astype(BF16), u_ref[...], preferred_element_type=F32)
        c0 = jnp.sum(oh0, axis=1, keepdims=True)
        n8 = jnp.floor((c0 + jnp.sum(oh1, axis=1, keepdims=True) + (GRAN - 1.0)) * (1.0 / GRAN))
        lo8 = prefix_e(n8)
        s0 = jnp.sum(oh0 * (GRAN * lo8 + cum0 - 1.0), axis=0, keepdims=True)
        s1 = jnp.sum(oh1 * (GRAN * lo8 + c0 + cum1 - 1.0), axis=0, keepdims=True)
        info = jnp.concatenate([s0, s1, r[2:4, :], jnp.zeros((SUBLANES - 4, tb), F32)], axis=0)
        srow_ref[j] = info
        col_ref[pl.ds(pl.multiple_of(j * tb, tb), tb), :] = jnp.concatenate(
            [info, jnp.zeros((LANES - SUBLANES, tb), F32)], axis=0).T
        mg = jnp.where((lo8 <= glane) & (glane < lo8 + n8), 1.0, 0.0)
        mg_ref[j] = mg
        part = jnp.sum(mg * (run8 + glane - lo8), axis=0, keepdims=True)
        gcnt = jnp.broadcast_to(jnp.sum(n8, axis=0, keepdims=True), (1, LANES))
        part_ref[j] = jnp.concatenate([part, gcnt, jnp.zeros((SUBLANES - 2, LANES), F32)], axis=0)
        return run8 + n8

    tot8 = lax.fori_loop(0, NT, p1, jnp.zeros((N_EXP, 1), F32), unroll=4 if NT % 4 == 0 else 1)
    seg_t = jnp.floor((tot8 * GRAN + (TM - 1.0)) * (1.0 / TM))
    base_t = prefix_e(seg_t)
    base8 = base_t * (TM // GRAN)
    lane1 = lax.broadcasted_iota(jnp.int32, (1, LANES), 1)

    def p2(j, carry):
        pr = part_ref[j]
        dst = (pr[0:1, :] + jnp.sum(mg_ref[j] * base8, axis=0, keepdims=True)) * GRAN
        gd_ref[j] = jnp.where(lane1 == G_LAST, pr[1:2, :], dst).astype(jnp.int32)
        return carry

    lax.fori_loop(0, NT, p2, 0, unroll=4 if NT % 4 == 0 else 1)
    eye = jnp.where(glane == lax.broadcasted_iota(jnp.int32, (N_EXP, LANES), 0).astype(F32), 1.0, 0.0)
    tail_row = jnp.sum(eye * ((base8 + tot8) * GRAN), axis=0, keepdims=True)
    tail_n8 = jnp.sum(eye * (seg_t * (TM // GRAN) - tot8), axis=0, keepdims=True)
    nv_l = jnp.broadcast_to(jnp.sum(seg_t, axis=0, keepdims=True), (1, LANES))
    gd_ref[NT] = jnp.where(lane1 == G_LAST, nv_l, tail_row).astype(jnp.int32)
    gd_ref[NT + 1] = tail_n8.astype(jnp.int32)
    ti = lax.broadcasted_iota(jnp.int32, (N_EXP, tb), 1).astype(F32)
    te = jnp.sum(jnp.where(base_t <= ti, 1.0, 0.0), axis=0, keepdims=True) - 1.0
    nv = jnp.broadcast_to(jnp.sum(seg_t, axis=0, keepdims=True), (1, tb))
    meta_ref[...] = jnp.concatenate([te, nv, jnp.zeros((SUBLANES - 2, tb), F32)],
                                    axis=0).astype(jnp.int32)


def _route(rrow, u_cnt, ltri, *, TM):
    NT, _, tb = rrow.shape
    kern = functools.partial(_route_kernel, NT=NT, tb=tb, TM=TM)
    full3 = lambda i: (0, 0, 0)
    return pl.pallas_call(
        kern,
        grid=(1,),
        in_specs=[pl.BlockSpec((NT, SUBLANES, tb), full3),
                  pl.BlockSpec((tb, tb), lambda i: (0, 0)),
                  pl.BlockSpec((N_EXP, N_EXP), lambda i: (0, 0))],
        out_specs=[pl.BlockSpec((NT, SUBLANES, tb), full3),
                   pl.BlockSpec((NT * tb, LANES), lambda i: (0, 0)),
                   pl.BlockSpec((NT + 2, 1, LANES), full3),
                   pl.BlockSpec((SUBLANES, tb), lambda i: (0, 0))],
        out_shape=[jax.ShapeDtypeStruct((NT, SUBLANES, tb), F32),
                   jax.ShapeDtypeStruct((NT * tb, LANES), F32),
                   jax.ShapeDtypeStruct((NT + 2, 1, LANES), jnp.int32),
                   jax.ShapeDtypeStruct((SUBLANES, tb), jnp.int32)],
        scratch_shapes=[pltpu.VMEM((NT, N_EXP, LANES), F32), pltpu.VMEM((NT, SUBLANES, LANES), F32)],
        compiler_params=_cparams(),
        name="route",
    )(rrow, u_cnt, ltri)


_HI_MASK = 0xFFFF0000


def _pack_halves(x):
    c = x.shape[1] // 2
    lo = lax.bitcast_convert_type(x[:, :c], U32)
    hi = lax.bitcast_convert_type(x[:, c:], U32)
    return (lo >> 16) | (hi & U32(_HI_MASK))


def _unpack_halves(w):
    lo = lax.bitcast_convert_type(w << 16, F32)
    hi = lax.bitcast_convert_type(w & U32(_HI_MASK), F32)
    return jnp.concatenate([lo, hi], axis=1).astype(BF16)


def _granule_copy(src_ref, src_row, dst_ref, dst_row, sem):
    return pltpu.make_async_copy(src_ref.at[pl.ds(src_row, GRAN), :], dst_ref.at[pl.ds(dst_row, GRAN), :], sem)


def _for_granules(n, body, unroll=4):
    def blk(i, carry):
        for t in range(unroll):
            body(i * unroll + t)
        return carry

    def one(g, carry):
        body(g)
        return carry

    nblk = n // unroll
    lax.fori_loop(0, nblk, blk, 0)
    lax.fori_loop(nblk * unroll, n, one, 0)


def _wait_granules(n, src_ref, dst_ref, sem, n_max):
    b = 1
    while b <= n_max:
        @pl.when((n & b) != 0)
        def _(b=b):
            pltpu.make_async_copy(src_ref.at[pl.ds(0, b * GRAN), :], dst_ref.at[pl.ds(0, b * GRAN), :],
                                  sem).wait()
        b *= 2


def _dispatch_kernel(gd_ref, srow_ref, u_ref, xs_ref, buf, zbuf, sems, *, NT, SL, TM, n_tiles):
    j = pl.program_id(0)
    slot = j % 2
    zsem = sems.at[2]

    def drain(tile, sl):
        _wait_granules(gd_ref[tile, G_LAST], buf.at[sl], xs_ref, sems.at[sl], SL // GRAN)

    def tile_fill(t):
        return pltpu.make_async_copy(zbuf, xs_ref.at[pl.ds(pl.multiple_of(t * TM, TM), TM), :], zsem)

    def zero_fill(wait):
        for e in range(N_EXP):
            n, row0 = gd_ref[NT + 1, e], gd_ref[NT, e]
            b = TM // GRAN // 2
            while b >= 1:
                @pl.when((n & b) != 0)
                def _(b=b, n=n, row0=row0):
                    start = pl.multiple_of(row0 + ((n >> b.bit_length()) << b.bit_length()) * GRAN, GRAN)
                    cp = pltpu.make_async_copy(zbuf.at[pl.ds(0, b * GRAN), :],
                                               xs_ref.at[pl.ds(start, b * GRAN), :], zsem)
                    cp.wait() if wait else cp.start()
                b //= 2

        def zt(t, carry):
            tile_fill(t).wait() if wait else tile_fill(t).start()
            return carry
        lax.fori_loop(gd_ref[NT, G_LAST], n_tiles, zt, 0)

    @pl.when(j == 0)
    def _():
        zbuf[...] = jnp.zeros_like(zbuf)
        zero_fill(False)

    @pl.when(j >= 2)
    def _():
        drain(j - 2, slot)

    s = srow_ref[0]
    rows = lax.broadcasted_iota(jnp.int32, (SL, s.shape[1]), 0).astype(F32)
    m0 = rows == s[0:1, :]
    m1 = rows == s[1:2, :]
    oh = jnp.where(m0 | m1, 1.0, 0.0).astype(BF16)
    dw = u_ref.shape[1] // 2
    buf[slot, :, 0:dw] = _pack_halves(jnp.dot(oh, u_ref[...], preferred_element_type=F32))
    wrow = jnp.sum(jnp.where(m0, s[2:3, :], 0.0) + jnp.where(m1, s[3:4, :], 0.0), axis=1, keepdims=True)
    buf[slot, :, dw:dw + LANES] = lax.bitcast_convert_type(jnp.broadcast_to(wrow, (SL, LANES)), U32)

    def issue(g):
        _granule_copy(buf.at[slot], pl.multiple_of(g * GRAN, GRAN), xs_ref,
                      pl.multiple_of(gd_ref[j, g], GRAN), sems.at[slot]).start()

    _for_granules(gd_ref[j, G_LAST], issue)

    @pl.when(j == NT - 1)
    def _():
        drain(j, slot)
        if NT > 1:
            drain(j - 1, 1 - slot)
        zero_fill(True)


def _dispatch(gd, srow, u2, *, n_tiles, TM):
    N, D = u2.shape
    NT, _, tb = srow.shape
    SL = _slots_per_tile(tb)
    n_rows = n_tiles * TM
    kern = functools.partial(_dispatch_kernel, NT=NT, SL=SL, TM=TM, n_tiles=n_tiles)
    grid_spec = pltpu.PrefetchScalarGridSpec(
        num_scalar_prefetch=1,
        grid=(NT,),
        in_specs=[pl.BlockSpec((1, SUBLANES, tb), lambda j, gd: (j, 0, 0)),
                  pl.BlockSpec((tb, D), lambda j, gd: (j, 0))],
        out_specs=pl.BlockSpec(memory_space=pl.ANY),
        scratch_shapes=[pltpu.VMEM((2, SL, D // 2 + LANES), U32), pltpu.VMEM((TM, D // 2 + LANES), U32),
                        pltpu.SemaphoreType.DMA((3,))],
    )
    return pl.pallas_call(
        kern,
        grid_spec=grid_spec,
        out_shape=jax.ShapeDtypeStruct((n_rows, D // 2 + LANES), U32),
        compiler_params=_cparams(),
        name="dispatch",
    )(gd, srow, u2)


def _ffn_kernel(te_ref, nv_ref, xs_ref, wg_ref, wu_ref, wd_ref, o_ref, wgb, wub, wdb, sg, su, sd, slot_ref,
                sems):
    i = pl.program_id(0)
    nv = nv_ref[0]
    e = te_ref[i]

    def weight_copies(ex, sl):
        return (pltpu.make_async_copy(wg_ref.at[ex], sg.at[sl], sems.at[sl]),
                pltpu.make_async_copy(wu_ref.at[ex], su.at[sl], sems.at[sl]),
                pltpu.make_async_copy(wd_ref.at[ex], sd.at[sl], sems.at[sl]))

    @pl.when(i == 0)
    def _():
        slot_ref[0] = 0
        for cp in weight_copies(e, 0):
            cp.start()

    @pl.when((i < nv) & ((i == 0) | (e != te_ref[jnp.maximum(i - 1, 0)])))
    def _():
        sl = slot_ref[0]
        for cp in weight_copies(e, sl):
            cp.wait()
        wgb[...] = sg[sl].astype(BF16)
        wub[...] = su[sl].astype(BF16)
        wdb[...] = sd[sl].astype(BF16)
        nxt = lax.while_loop(lambda t: (t < nv) & (te_ref[jnp.minimum(t, nv - 1)] == e), lambda t: t + 1, i + 1)

        @pl.when(nxt < nv)
        def _():
            for cp in weight_copies(te_ref[nxt], 1 - sl):
                cp.start()
        slot_ref[0] = 1 - sl

    @pl.when(i < nv)
    def _():
        nsub = FFN_SUB
        hm = xs_ref.shape[0] // nsub
        dw = o_ref.shape[1]
        halves = tuple(slice(hm * j, hm * (j + 1)) for j in range(nsub))
        x = [_unpack_halves(xs_ref[r, 0:dw]) for r in halves]
        g = [jnp.dot(x[j], wgb[...], preferred_element_type=F32) for j in range(nsub)]
        u = [jnp.dot(x[j], wub[...], preferred_element_type=F32) for j in range(nsub)]
        h = [(g[j] * _sigmoid(g[j]) * u[j]).astype(BF16) for j in range(nsub)]
        y = [jnp.dot(h[j], wdb[...], preferred_element_type=F32) for j in range(nsub)]
        for j in range(nsub):
            wt = lax.bitcast_convert_type(xs_ref[halves[j], dw:dw + LANES], F32)
            yw = y[j] * jnp.concatenate([wt] * (2 * dw // LANES), axis=1)
            o_ref[halves[j], :] = _pack_halves(yw.astype(BF16).astype(F32))

    @pl.when(i >= nv_ref[0])
    def _():
        o_ref[...] = jnp.zeros_like(o_ref)


def _ffn(te, nv, xs, wg, wu, wd, *, TM):
    P, XW = xs.shape
    DW = XW - LANES
    D = 2 * DW
    n_tiles = P // TM
    grid_spec = pltpu.PrefetchScalarGridSpec(
        num_scalar_prefetch=2,
        grid=(n_tiles,),
        in_specs=[pl.BlockSpec((TM, XW), lambda i, te, nv: (jnp.maximum(jnp.minimum(i, nv[0] - 1), 0), 0)),
                  pl.BlockSpec(memory_space=pl.ANY),
                  pl.BlockSpec(memory_space=pl.ANY),
                  pl.BlockSpec(memory_space=pl.ANY)],
        out_specs=pl.BlockSpec((TM, DW), lambda i, te, nv: (i, 0)),
        scratch_shapes=[pltpu.VMEM((D, D_EXP), BF16), pltpu.VMEM((D, D_EXP), BF16),
                        pltpu.VMEM((D_EXP, D), BF16),
                        pltpu.VMEM((2, D, D_EXP), F32), pltpu.VMEM((2, D, D_EXP), F32),
                        pltpu.VMEM((2, D_EXP, D), F32), pltpu.SMEM((1,), jnp.int32),
                        pltpu.SemaphoreType.DMA((2,))],
    )
    return pl.pallas_call(
        _ffn_kernel,
        grid_spec=grid_spec,
        out_shape=jax.ShapeDtypeStruct((P, DW), U32),
        compiler_params=_cparams(),
        name="ffn",
    )(te, nv, xs, wg, wu, wd)


def _combine_kernel(gd_ref, ys_ref, col_ref, x1_ref, mod_ref, g_ref, b_ref, o_ref, buf, sems, *, NT, SL):
    j = pl.program_id(0)
    slot = j % 2

    def fetch(tile, sl):
        def f(g):
            _granule_copy(ys_ref, pl.multiple_of(gd_ref[tile, g], GRAN), buf.at[sl],
                          pl.multiple_of(g * GRAN, GRAN), sems.at[sl]).start()
        _for_granules(gd_ref[tile, G_LAST], f)

    @pl.when(j == 0)
    def _():
        fetch(0, 0)

    @pl.when(j + 1 < NT)
    def _():
        fetch(j + 1, 1 - slot)

    ng = gd_ref[j, G_LAST]

    _wait_granules(ng, ys_ref, buf.at[slot], sems.at[slot], SL // GRAN)

    rows = lax.broadcasted_iota(jnp.int32, (SL, 1), 0)
    yb = _unpack_halves(jnp.where(rows < ng * GRAN, buf[slot], U32(0)))
    col = col_ref[...]
    tb = col.shape[0]
    lanes = lax.broadcasted_iota(jnp.int32, (tb, SL), 1).astype(F32)
    sel = jnp.where((lanes == col[:, 0:1]) | (lanes == col[:, 1:2]), 1.0, 0.0).astype(BF16)
    y = jnp.dot(sel, yb, preferred_element_type=F32)
    mod = mod_ref[0]
    z = ALPHA * x1_ref[...] + (1.0 + mod[5:6, :]) * y
    o_ref[...] = _layer_norm(z, g_ref[...], b_ref[...])


def _combine(gd, ys, col, x1, mod3, g, b, *, S, tb):
    N, D = x1.shape
    NT = N // tb
    tpb = S // tb
    SL = _slots_per_tile(tb)
    kern = functools.partial(_combine_kernel, NT=NT, SL=SL)
    grid_spec = pltpu.PrefetchScalarGridSpec(
        num_scalar_prefetch=1,
        grid=(NT,),
        in_specs=[pl.BlockSpec(memory_space=pl.ANY),
                  pl.BlockSpec((tb, LANES), lambda j, gd: (j, 0)),
                  pl.BlockSpec((tb, D), lambda j, gd: (j, 0)),
                  pl.BlockSpec((1, 6, D), lambda j, gd: (j // tpb, 0, 0)),
                  pl.BlockSpec((1, D), lambda j, gd: (0, 0)),
                  pl.BlockSpec((1, D), lambda j, gd: (0, 0))],
        out_specs=pl.BlockSpec((tb, D), lambda j, gd: (j, 0)),
        scratch_shapes=[pltpu.VMEM((2, SL, D // 2), U32), pltpu.SemaphoreType.DMA((2,))],
    )
    return pl.pallas_call(
        kern,
        grid_spec=grid_spec,
        out_shape=jax.ShapeDtypeStruct((N, D), F32),
        compiler_params=_cparams(),
        name="combine",
    )(gd, ys, col, x1, mod3, g, b)


def _layer(x, c, l, w_ada, b_ada, w_in, w_conv, b_conv, b_igate, b_fgate, mlstm_norm_g, w_gla_a, b_gla_a,
           gla_norm_g, w_out, ln1_g, ln1_b, w_route_group, b_route_group, w_route_expert, b_route_expert,
           w_gate, w_up, w_down, ln2_g, ln2_b):
    B, S, D = x.shape
    N = B * S
    x2 = x.reshape(N, D)
    tm_in = min(512, S)
    tm = min(256, S)
    lm = min(256, S)

    mod3 = _ada(c, w_ada[l], b_ada[l]).reshape(B, 6, D)

    wa_pad = jnp.zeros((LANES, G_KW), F32).at[SM_A:SM_A + G_RANK].set(w_gla_a[l]).astype(BF16)
    bg = (jnp.zeros((2 * SUBLANES, 1), F32).at[0:M_HEADS, 0].set(b_igate[l])
          .at[SUBLANES:SUBLANES + M_HEADS, 0].set(b_fgate[l]))
    oa, la, g3 = _inproj(x2, mod3, jnp.swapaxes(w_in, 1, 2), w_conv[l], b_conv[l].reshape(1, -1), wa_pad,
                         b_gla_a[l].reshape(1, -1), bg, S=S, tm=tm_in, lm=lm, layer=l)

    u_tri = jnp.asarray(np.triu(np.ones((lm, lm), np.float32)))
    nb = 4 if B % 4 == 0 else (2 if B % 2 == 0 else 1)
    ts = min(512, S)
    hm = _mlstm(oa, g3, u_tri, mlstm_norm_g[l].reshape(1, -1), B=B, S=S, L=lm, nb=nb, ts=ts)
    w3_np, mk_np = _gla_consts()
    hg = _gla(oa, la, jnp.asarray(w3_np, BF16), jnp.asarray(mk_np), gla_norm_g[l].reshape(1, -1), B=B, S=S,
              nb=nb, ts=ts)

    wr = (jnp.zeros((D, LANES), F32).at[:, 0:N_GROUPS].set(w_route_group[l])
          .at[:, SUBLANES:SUBLANES + N_EXP].set(w_route_expert[l]))
    br = (jnp.zeros((1, LANES), F32).at[0, 0:N_GROUPS].set(b_route_group[l])
          .at[0, SUBLANES:SUBLANES + N_EXP].set(b_route_expert[l]))
    wr_hi = wr.astype(BF16)
    wr2 = jnp.concatenate([wr_hi, (wr - wr_hi.astype(F32)).astype(BF16)], axis=1)
    x1, u2, rrow = _outproj(hm, hg, w_out[l], x2, mod3, ln1_g[l].reshape(1, -1),
                            ln1_b[l].reshape(1, -1), wr2, br, S=S, tb=tm, nh=4 if S % (4 * tm) == 0 else 1)

    u_cnt = jnp.asarray(np.triu(np.ones((tm, tm), np.float32)), BF16)
    ltri = jnp.asarray(np.tril(np.ones((N_EXP, N_EXP), np.float32), -1))
    srow, col, gd3, meta = _route(rrow, u_cnt, ltri, TM=FFN_TM)
    gd = gd3.reshape(N // tm + 2, LANES)
    n_tiles = _ffn_tiles(N, tm)
    te, nv = meta[0, :n_tiles], meta[1, 0:1]

    xs = _dispatch(gd, srow, u2, n_tiles=n_tiles, TM=FFN_TM)
    ys = _ffn(te, nv, xs, w_gate[l], w_up[l], w_down[l], TM=FFN_TM)
    out = _combine(gd, ys, col, x1, mod3, ln2_g[l].reshape(1, -1), ln2_b[l].reshape(1, -1), S=S, tb=tm)
    return out.reshape(B, S, D)


def kernel(x, c, w_ada, b_ada, w_in, w_conv, b_conv, b_igate, b_fgate, mlstm_norm_g, w_gla_a, b_gla_a,
           gla_norm_g, w_out, ln1_g, ln1_b, w_route_group, b_route_group, w_route_expert, b_route_expert,
           w_gate, w_up, w_down, ln2_g, ln2_b):
    for l in range(DEPTH):
        x = _layer(x, c, l, w_ada, b_ada, w_in, w_conv, b_conv, b_igate, b_fgate, mlstm_norm_g, w_gla_a,
                   b_gla_a, gla_norm_g, w_out, ln1_g, ln1_b, w_route_group, b_route_group, w_route_expert,
                   b_route_expert, w_gate, w_up, w_down, ln2_g, ln2_b)
    return x
```

```python
import functools

import numpy as np
import jax
import jax.numpy as jnp
from jax import lax
from jax.experimental import pallas as pl
from jax.experimental.pallas import tpu as pltpu

F32 = jnp.float32
BF16 = jnp.bfloat16
U32 = jnp.uint32
HIGHEST = lax.Precision.HIGHEST

DEPTH = 1
M_HEADS = 4
M_HD = 128
M_W = M_HEADS * M_HD
CONV_W = 4
G_HEADS = 4
G_DK = 64
G_DV = 128
G_W = G_HEADS * G_DV
G_KW = G_HEADS * G_DK
G_RANK = 16
G_TAU = 16.0
G_CHUNK = 64
N_GROUPS = 4
E_PER_G = 8
N_EXP = N_GROUPS * E_PER_G
D_EXP = 512
ALPHA = (2 * DEPTH) ** 0.25
LN_EPS = 1e-5

LANES = 128
SUBLANES = 8
VMEM_LIMIT = 48 * 1024 * 1024

C_QK = 0
C_VO = 1024
C_GQK = 2048
C_GV = 2560
C_GG = 3072
C_SMALL = 3584
C_TOT = 3712
SM_I, SM_F, SM_A = 0, 8, 16
IN_GATES = 4 * M_W
IN_G = IN_GATES + 2 * M_HEADS
IN_GA = IN_G + 2 * G_KW + 2 * G_W
IN_TOT = IN_GA + G_RANK

FFN_TM = 512
FFN_SUB = 2
GRAN = SUBLANES
G_LAST = LANES - 1


def _cparams(n_axes=1):
    return pltpu.CompilerParams(dimension_semantics=("arbitrary",) * n_axes,
                                vmem_limit_bytes=VMEM_LIMIT)


def _sigmoid(x):
    return 1.0 / (1.0 + jnp.exp(-x))


def _log_sigmoid(x):
    return jnp.minimum(x, 0.0) - jnp.log(1.0 + jnp.exp(-jnp.abs(x)))


def _ada_kernel(c_ref, w_ref, b_ref, o_ref):
    c = c_ref[...]
    ca = (c * _sigmoid(c)).astype(BF16)
    o_ref[...] = jnp.dot(ca, w_ref[...].astype(BF16), preferred_element_type=F32) + b_ref[...]


def _ada(c, w, b):
    B, D = c.shape
    n_out = w.shape[1]
    tn = 1024
    return pl.pallas_call(
        _ada_kernel,
        grid=(n_out // tn,),
        in_specs=[pl.BlockSpec((B, D), lambda j: (0, 0)),
                  pl.BlockSpec((D, tn), lambda j: (0, j)),
                  pl.BlockSpec((1, tn), lambda j: (0, j))],
        out_specs=pl.BlockSpec((B, tn), lambda j: (0, j)),
        out_shape=jax.ShapeDtypeStruct((B, n_out), F32),
        compiler_params=_cparams(),
        name="ada",
    )(c, w, b.reshape(1, n_out))


def _inproj_kernel(x_ref, mod_ref, win_ref, wc_ref, bc_ref, wa_ref, ba_ref, bg_ref,
                   oa_ref, la_ref, g_ref, halo_ref, w_ref, *, tm, tpb, lm):
    i = pl.program_id(0)

    @pl.when(i == 0)
    def _():
        rc = 2 * LANES
        for r in range(0, IN_GATES, rc):
            w_ref[:, r:r + rc] = win_ref[0, r:r + rc, :].T.astype(BF16)
        for r in range(0, C_SMALL - C_GQK, rc):
            w_ref[:, C_GQK + r:C_GQK + r + rc] = win_ref[0, IN_G + r:IN_G + r + rc, :].T.astype(BF16)
        gates = win_ref[0, IN_GATES:IN_G, :]
        z = lambda n: jnp.zeros((n, gates.shape[1]), F32)
        small = jnp.concatenate([gates[0:M_HEADS], z(SM_F - M_HEADS), gates[M_HEADS:2 * M_HEADS],
                                 z(SM_A - SM_F - M_HEADS), win_ref[0, IN_GA:IN_TOT, :],
                                 z(LANES - SM_A - G_RANK)], axis=0)
        w_ref[:, C_SMALL:C_TOT] = small.T.astype(BF16)

    @pl.when(i % tpb == 0)
    def _():
        halo_ref[0:SUBLANES, :] = jnp.zeros((SUBLANES, halo_ref.shape[1]), F32)

    mod = mod_ref[0]
    u = (x_ref[...] * (1.0 + mod[1:2, :]) + mod[0:1, :]).astype(BF16)

    def proj(c0, c1):
        return jnp.dot(u, w_ref[:, c0:c1], preferred_element_type=F32)

    p = proj(C_QK, C_QK + 2 * M_W)
    halo_ref[SUBLANES:SUBLANES + tm, :] = p
    acc = bc_ref[...] + wc_ref[CONV_W - 1:CONV_W, :] * p
    for j in range(CONV_W - 1):
        acc = acc + wc_ref[j:j + 1, :] * halo_ref[pl.ds(SUBLANES - (CONV_W - 1) + j, tm), :]
    halo_ref[0:SUBLANES, :] = p[tm - SUBLANES:, :]
    qk = acc * _sigmoid(acc)
    oa_ref[:, C_QK:C_QK + M_W] = qk[:, :M_W].astype(BF16)
    oa_ref[:, C_QK + M_W:C_QK + 2 * M_W] = (qk[:, M_W:] * (M_HD ** -0.5)).astype(BF16)

    p = proj(C_VO, C_VO + 2 * M_W)
    oa_ref[:, C_VO:C_VO + 2 * M_W] = p.astype(BF16)

    p = proj(C_GQK, C_GQK + G_KW)
    oa_ref[:, C_GQK:C_GQK + G_KW] = (p * (G_DK ** -0.5)).astype(BF16)
    p = proj(C_GQK + G_KW, C_SMALL)
    oa_ref[:, C_GQK + G_KW:C_SMALL] = p.astype(BF16)

    ps = proj(C_SMALL, C_TOT)
    la = jnp.dot(ps.astype(BF16), wa_ref[...], preferred_element_type=F32) + ba_ref[...]
    la_ref[...] = _log_sigmoid(la) * (1.0 / G_TAU)
    pt = ps.T
    gi = pt[SM_I:SM_I + SUBLANES, :] + bg_ref[0:SUBLANES, :]
    gf = _log_sigmoid(pt[SM_F:SM_F + SUBLANES, :] + bg_ref[SUBLANES:2 * SUBLANES, :])
    for j in range(tm // lm):
        g_ref[j, 0:SUBLANES, :] = gi[:, j * lm:(j + 1) * lm]
        g_ref[j, SUBLANES:2 * SUBLANES, :] = gf[:, j * lm:(j + 1) * lm]


def _inproj(x2, mod3, w_in, w_conv, b_conv, wa_pad, b_gla, bg, *, S, tm, lm, layer):
    N, D = x2.shape
    tpb = S // tm
    kern = functools.partial(_inproj_kernel, tm=tm, tpb=tpb, lm=lm)
    return pl.pallas_call(
        kern,
        grid=(N // tm,),
        in_specs=[pl.BlockSpec((tm, D), lambda i: (i, 0)),
                  pl.BlockSpec((1, 6, D), lambda i: (i // tpb, 0, 0)),
                  pl.BlockSpec((1, IN_TOT, D), lambda i: (layer, 0, 0), pipeline_mode=pl.Buffered(1)),
                  pl.BlockSpec((CONV_W, 2 * M_W), lambda i: (0, 0)),
                  pl.BlockSpec((1, 2 * M_W), lambda i: (0, 0)),
                  pl.BlockSpec((LANES, G_KW), lambda i: (0, 0)),
                  pl.BlockSpec((1, G_KW), lambda i: (0, 0)),
                  pl.BlockSpec((2 * SUBLANES, 1), lambda i: (0, 0))],
        out_specs=[pl.BlockSpec((tm, C_SMALL), lambda i: (i, 0)),
                   pl.BlockSpec((tm, G_KW), lambda i: (i, 0)),
                   pl.BlockSpec((tm // lm, 2 * SUBLANES, lm), lambda i: (i, 0, 0))],
        out_shape=[jax.ShapeDtypeStruct((N, C_SMALL), BF16),
                   jax.ShapeDtypeStruct((N, G_KW), F32),
                   jax.ShapeDtypeStruct((N // lm, 2 * SUBLANES, lm), F32)],
        scratch_shapes=[pltpu.VMEM((SUBLANES + tm, 2 * M_W), F32), pltpu.VMEM((D, C_TOT), BF16)],
        compiler_params=_cparams(),
        name="inproj",
    )(x2, mod3, w_in, w_conv, b_conv, wa_pad, b_gla, bg)


def _mlstm_sel():
    sel = np.zeros((2 * LANES, 2 * M_HEADS * M_HD), np.float32)
    for j in range(2 * M_HEADS):
        src = (SUBLANES if j < M_HEADS else 3 * SUBLANES) + j % M_HEADS
        sel[src, M_HD * j:M_HD * (j + 1)] = 1.0
        sel[LANES + src, M_HD * j:M_HD * (j + 1)] = 1.0
    return sel


def _mlstm_kernel(qk_ref, vo_ref, g_ref, u_ref, gain_ref, sel_ref, out_ref, c_ref, zt_ref, a_ref, dec_ref, m_ref,
                  *, L, NC, nb):
    @pl.when(pl.program_id(1) == 0)
    def _():
        c_ref[...] = jnp.zeros_like(c_ref)
        m_ref[...] = jnp.zeros_like(m_ref)

    tril = (lax.broadcasted_iota(jnp.int32, (L, L), 0) >= lax.broadcasted_iota(jnp.int32, (L, L), 1))
    ones_v = jnp.ones((L, M_HD), BF16)
    zpad = jnp.zeros((LANES - 4 * SUBLANES, L), F32)

    order = [(bi, c) for bi in range(nb) for c in range(NC)]
    f_all = jnp.concatenate([g_ref[bi, c, SUBLANES:2 * SUBLANES, :] for bi, c in order], axis=0)
    i_all = jnp.concatenate([g_ref[bi, c, 0:SUBLANES, :] for bi, c in order], axis=0)
    b_all = jnp.dot(f_all, u_ref[...], preferred_element_type=F32, precision=HIGHEST)
    a_all = i_all - b_all
    lane_all = lax.broadcasted_iota(jnp.int32, a_all.shape, 1)
    g_all = a_all
    s = 1
    while s < L:
        g_all = jnp.maximum(g_all, jnp.where(lane_all >= s, pltpu.roll(g_all, s, 1), -jnp.inf))
        s *= 2
    for bi in range(nb):
        m_prev = m_ref[bi][:, 0:1]
        for c in range(NC):
            ci = bi * NC + c
            r8 = slice(SUBLANES * ci, SUBLANES * (ci + 1))
            a, b = a_all[r8], b_all[r8]
            a_ref[ci] = a
            M = jnp.maximum(g_all[r8], m_prev)
            ML = M[:, L - 1:L]
            Z = jnp.concatenate([M, jnp.exp(m_prev - M), jnp.exp(-(b + M)), jnp.exp(a - ML), zpad],
                                axis=0)
            zt_ref[ci] = Z.T
            dec_ref[ci] = jnp.broadcast_to(jnp.exp(m_prev - ML), (SUBLANES, 2 * M_HD))
            m_prev = b[:, L - 1:L] + ML
        m_ref[bi] = jnp.broadcast_to(m_prev, (SUBLANES, LANES))

    chains = [(bi, h) for bi in range(nb) for h in range(M_HEADS)]
    nt = (((1,), (1,)), ((), ()))
    tn = (((0,), (0,)), ((), ()))

    def chunk(c, carry):
        rows = pl.ds(pl.multiple_of(c * L, L), L)
        Zt = [zt_ref[bi * NC + c] for bi in range(nb)]
        a = [a_ref[bi * NC + c] for bi in range(nb)]
        dec = [dec_ref[bi * NC + c] for bi in range(nb)]
        hs = [slice(h * M_HD, (h + 1) * M_HD) for h in range(M_HEADS)]
        hs2 = [slice(M_W + h * M_HD, M_W + (h + 1) * M_HD) for h in range(M_HEADS)]
        q = [qk_ref[bi, rows, hs[h]] for bi, h in chains]
        k = [qk_ref[bi, rows, hs2[h]] for bi, h in chains]
        vext = [jnp.concatenate([vo_ref[bi, rows, hs[h]], ones_v], axis=1) for bi, h in chains]
        cst = [c_ref[bi * M_HEADS + h] for bi, h in chains]
        n = range(len(chains))
        sc = [lax.dot_general(q[i], k[i], nt, preferred_element_type=F32) for i in n]
        qc = [jnp.dot(q[i], cst[i].astype(BF16), preferred_element_type=F32) for i in n]
        pm = [(sc[i] * jnp.exp(jnp.where(tril, a[bi][h:h + 1, :] - Zt[bi][:, h:h + 1], -jnp.inf))).astype(BF16)
              for i, (bi, h) in enumerate(chains)]
        pv = [jnp.dot(pm[i], vext[i], preferred_element_type=F32) for i in n]
        rep = []
        for bi in range(nb):
            zh = Zt[bi].astype(BF16)
            zl = (Zt[bi] - zh.astype(F32)).astype(BF16)
            rep.append(jnp.dot(jnp.concatenate([zh, zl], axis=1), sel_ref[...], preferred_element_type=F32))
        e_inter = [rep[bi][:, M_HD * h:M_HD * (h + 1)] for bi, h in chains]
        w_state = [rep[bi][:, M_HD * (M_HEADS + h):M_HD * (M_HEADS + h + 1)] for bi, h in chains]
        kw = [(w_state[i] * k[i].astype(F32)).astype(BF16) for i in n]
        upd = [lax.dot_general(kw[i], vext[i], tn, preferred_element_type=F32) for i in n]
        for i, (bi, h) in enumerate(chains):
            c_ref[bi * M_HEADS + h] = dec[bi][h:h + 1, :] * cst[i] + upd[i]
            nd = pv[i] + jnp.concatenate([e_inter[i], e_inter[i]], axis=1) * qc[i]
            hh = nd[:, :M_HD] / jnp.maximum(jnp.abs(nd[:, M_HD:]),
                                            Zt[bi][:, 2 * SUBLANES + h:2 * SUBLANES + h + 1])
            hh = _sigmoid(vo_ref[bi, rows, hs2[h]].astype(F32)) * hh
            hn = hh * lax.rsqrt(jnp.mean(hh * hh, axis=-1, keepdims=True) + LN_EPS)
            out_ref[bi, rows, hs[h]] = (hn * gain_ref[:, hs[h]]).astype(BF16)
        return carry

    lax.fori_loop(0, NC, chunk, 0)


def _mlstm(oa, g3, u_tri, gain, *, B, S, L, nb, ts):
    N = oa.shape[0]
    NC = ts // L
    oa3 = oa.reshape(B, S, oa.shape[1])
    g4 = g3.reshape(B, S // L, 2 * SUBLANES, L)
    sel = jnp.asarray(_mlstm_sel(), BF16)
    kern = functools.partial(_mlstm_kernel, L=L, NC=NC, nb=nb)
    out = pl.pallas_call(
        kern,
        grid=(B // nb, S // ts),
        in_specs=[pl.BlockSpec((nb, ts, 2 * M_W), lambda b, t: (b, t, C_QK // (2 * M_W))),
                  pl.BlockSpec((nb, ts, 2 * M_W), lambda b, t: (b, t, C_VO // (2 * M_W))),
                  pl.BlockSpec((nb, NC, 2 * SUBLANES, L), lambda b, t: (b, t, 0, 0)),
                  pl.BlockSpec((L, L), lambda b, t: (0, 0)),
                  pl.BlockSpec((1, M_W), lambda b, t: (0, 0)),
                  pl.BlockSpec(sel.shape, lambda b, t: (0, 0))],
        out_specs=pl.BlockSpec((nb, ts, M_W), lambda b, t: (b, t, 0)),
        out_shape=jax.ShapeDtypeStruct((B, S, M_W), BF16),
        scratch_shapes=[pltpu.VMEM((nb * M_HEADS, M_HD, 2 * M_HD), F32),
                        pltpu.VMEM((nb * NC, L, LANES), F32),
                        pltpu.VMEM((nb * NC, SUBLANES, L), F32),
                        pltpu.VMEM((nb * NC, SUBLANES, 2 * M_HD), F32),
                        pltpu.VMEM((nb, SUBLANES, LANES), F32)],
        compiler_params=_cparams(2),
        name="mlstm",
    )(oa3, oa3, g4, u_tri, gain, sel)
    return out.reshape(N, M_W)


_G_LEVELS = 6
_G_XROW = 2 * G_CHUNK + SUBLANES


def _gla_consts():
    L = G_CHUNK
    t = np.arange(L)
    blocks = [(t[None, :] <= t[:, None]).astype(np.float32),
              (t[None, :] > t[:, None]).astype(np.float32),
              np.ones((SUBLANES, L), np.float32)]
    masks = [np.eye(L, dtype=np.float32)]
    m = 1
    while m < L:
        wl = np.zeros((L, L), np.float32)
        for r in range(L):
            r0 = (r // (2 * m)) * 2 * m + m
            if r % (2 * m) >= m:
                wl[r, r0:r + 1] = 1.0
            else:
                wl[r, r + 1:r0] = 1.0
        blocks.append(wl)
        tt, ss = t[:, None], t[None, :]
        masks.append(((tt // (2 * m) == ss // (2 * m)) & (tt % (2 * m) >= m)
                      & (ss % (2 * m) < m)).astype(np.float32))
        m *= 2
    w = np.concatenate(blocks, axis=0)
    w3 = np.concatenate([w, w, w], axis=1)
    mk = np.stack([np.concatenate([x] * G_HEADS, axis=0) for x in masks])
    return w3, mk


def _gla_kernel(qk_ref, v_ref, gg_ref, la_ref, w3_ref, mk_ref, gain_ref, out_ref, st_ref, *, NC, nb):
    L = G_CHUNK

    @pl.when(pl.program_id(1) == 0)
    def _():
        st_ref[...] = jnp.zeros_like(st_ref)

    lane_head = lax.broadcasted_iota(jnp.int32, (L, G_KW), 1) // G_DK
    br = lax.broadcasted_iota(jnp.int32, (2 * G_DV, LANES), 0) < G_DV
    bl = lax.broadcasted_iota(jnp.int32, (2 * G_DV, LANES), 1) < G_DK
    bmask = br == bl
    nt = (((1,), (1,)), ((), ()))
    tn = (((0,), (0,)), ((), ()))

    def chunk(c, carry):
        rows = pl.ds(pl.multiple_of(c * L, L), L)
        X, q, k = [], [], []
        for bi in range(nb):
            la = la_ref[bi, rows, :]
            hi = la.astype(BF16)
            r1 = la - hi.astype(F32)
            mid = r1.astype(BF16)
            lo = (r1 - mid.astype(F32)).astype(BF16)
            stk = jnp.concatenate([hi, mid, lo], axis=0)
            X.append(jnp.exp(jnp.dot(w3_ref[...], stk, preferred_element_type=F32)))
            q.append(qk_ref[bi, rows, 0:G_KW].astype(F32))
            k.append(qk_ref[bi, rows, G_KW:2 * G_KW].astype(F32))

        sc = [[None] * (_G_LEVELS + 1) for _ in range(nb)]
        for lev in range(_G_LEVELS + 1):
            for bi in range(nb):
                if lev == 0:
                    qt, kt = q[bi], k[bi]
                else:
                    xl = X[bi][_G_XROW + L * (lev - 1):_G_XROW + L * lev, :]
                    qt, kt = q[bi] * xl, k[bi] * xl
                q4 = jnp.concatenate([jnp.where(lane_head == h, qt, 0.0) for h in range(G_HEADS)],
                                     axis=0).astype(BF16)
                sc[bi][lev] = lax.dot_general(q4, kt.astype(BF16), nt, preferred_element_type=F32)
        Ab = []
        for bi in range(nb):
            A = sc[bi][0] * mk_ref[0]
            for lev in range(1, _G_LEVELS + 1):
                A = A + sc[bi][lev] * mk_ref[lev]
            Ab.append(A.astype(BF16))

        for bi in range(nb):
            gg = gg_ref[bi, rows, :].astype(F32)
            gate = gg * _sigmoid(gg)
            for p in range(2):
                ls = slice(LANES * p, LANES * (p + 1))
                vp = v_ref[bi, rows, 2 * G_DV * p:2 * G_DV * (p + 1)]
                oi = [jnp.dot(Ab[bi][L * (2 * p + hh):L * (2 * p + hh + 1)],
                              vp[:, G_DV * hh:G_DV * (hh + 1)], preferred_element_type=F32)
                      for hh in range(2)]
                st = st_ref[bi, p]
                qc = (q[bi][:, ls] * X[bi][0:L, ls]).astype(BF16)
                o_inter = lax.dot_general(qc, st.astype(BF16), nt, preferred_element_type=F32)
                kc = (k[bi][:, ls] * X[bi][L:2 * L, ls]).astype(BF16)
                upd = lax.dot_general(vp, kc, tn, preferred_element_type=F32)
                dec = X[bi][2 * L:2 * L + 1, ls]
                st_ref[bi, p] = jnp.where(bmask, dec * st + upd, 0.0)
                for hh in range(2):
                    o = o_inter[:, G_DV * hh:G_DV * (hh + 1)] + oi[hh]
                    hn = o * lax.rsqrt(jnp.mean(o * o, axis=-1, keepdims=True) + LN_EPS)
                    hs = slice(G_DV * (2 * p + hh), G_DV * (2 * p + hh + 1))
                    out_ref[bi, rows, hs] = (hn * gain_ref[:, hs] * gate[:, hs]).astype(BF16)
        return carry

    lax.fori_loop(0, NC, chunk, 0)


def _gla(oa, la, w3, mk, gain, *, B, S, nb, ts):
    N = oa.shape[0]
    oa3 = oa.reshape(B, S, oa.shape[1])
    la3 = la.reshape(B, S, G_KW)
    kern = functools.partial(_gla_kernel, NC=ts // G_CHUNK, nb=nb)
    out = pl.pallas_call(
        kern,
        grid=(B // nb, S // ts),
        in_specs=[pl.BlockSpec((nb, ts, 2 * G_KW), lambda b, t: (b, t, C_GQK // (2 * G_KW))),
                  pl.BlockSpec((nb, ts, G_W), lambda b, t: (b, t, C_GV // G_W)),
                  pl.BlockSpec((nb, ts, G_W), lambda b, t: (b, t, C_GG // G_W)),
                  pl.BlockSpec((nb, ts, G_KW), lambda b, t: (b, t, 0)),
                  pl.BlockSpec(w3.shape, lambda b, t: (0, 0)),
                  pl.BlockSpec(mk.shape, lambda b, t: (0, 0, 0)),
                  pl.BlockSpec((1, G_W), lambda b, t: (0, 0))],
        out_specs=pl.BlockSpec((nb, ts, G_W), lambda b, t: (b, t, 0)),
        out_shape=jax.ShapeDtypeStruct((B, S, G_W), BF16),
        scratch_shapes=[pltpu.VMEM((nb, 2, 2 * G_DV, LANES), F32)],
        compiler_params=_cparams(2),
        name="gla",
    )(oa3, oa3, oa3, la3, w3, mk, gain)
    return out.reshape(N, G_W)


def _layer_norm(z, g, b):
    mu = jnp.mean(z, axis=-1, keepdims=True)
    zc = z - mu
    var = jnp.mean(zc * zc, axis=-1, keepdims=True)
    return zc * lax.rsqrt(var + LN_EPS) * g + b


def _outproj_kernel(hm_ref, hg_ref, wf_ref, x_ref, mod_ref, g_ref, b_ref, wrg_ref, wre_ref, br_ref,
                    x1_ref, u2_ref, rrow_ref, w_ref, wr_ref, *, tb, nh):
    @pl.when(pl.program_id(0) == 0)
    def _():
        w_ref[...] = wf_ref[...].astype(BF16)
        z = lambda n: jnp.zeros((n, wrg_ref.shape[2]), F32)
        wt = jnp.concatenate([wrg_ref[0], z(SUBLANES - N_GROUPS), wre_ref[0],
                              z(LANES - SUBLANES - N_EXP)], axis=0).T
        hi = wt.astype(BF16)
        wr_ref[:, 0:LANES] = hi
        wr_ref[:, LANES:2 * LANES] = (wt - hi.astype(F32)).astype(BF16)

    mod = mod_ref[0]
    blocks = [slice(tb * j, tb * (j + 1)) for j in range(nh)]
    y = [jnp.dot(hm_ref[r, :], w_ref[0:M_W, :], preferred_element_type=F32)
         + jnp.dot(hg_ref[r, :], w_ref[M_W:M_W + G_W, :], preferred_element_type=F32) for r in blocks]
    u2 = []
    for j, r in enumerate(blocks):
        z = ALPHA * x_ref[r, :] + (1.0 + mod[2:3, :]) * y[j]
        x1 = _layer_norm(z, g_ref[...], b_ref[...])
        x1_ref[r, :] = x1
        u2.append(x1 * (1.0 + mod[4:5, :]) + mod[3:4, :])
        u2_ref[r, :] = u2[j].astype(BF16)

    u2h = [u.astype(BF16) for u in u2]
    u2l = [(u2[j] - u2h[j].astype(F32)).astype(BF16) for j in range(nh)]
    lh = [jnp.dot(u, wr_ref[...], preferred_element_type=F32) for u in u2h]
    ll = [jnp.dot(u, wr_ref[:, 0:LANES], preferred_element_type=F32) for u in u2l]
    for j in range(nh):
        logits = lh[j][:, 0:LANES] + lh[j][:, LANES:2 * LANES] + ll[j] + br_ref[...]
        rrow_ref[j] = _route_select(logits.T, tb)


def _route_select(lt, tm):
    row = lax.broadcasted_iota(jnp.int32, (SUBLANES, tm), 0)
    gl = jnp.where(row < N_GROUPS, lt[0:SUBLANES, :], -jnp.inf)
    gmax = jnp.max(gl, axis=0, keepdims=True)
    gsel = jnp.min(jnp.where(gl == gmax, row, SUBLANES), axis=0, keepdims=True)
    pg = 1.0 / jnp.sum(jnp.exp(gl - gmax), axis=0, keepdims=True)
    ein = jnp.zeros((SUBLANES, tm), F32)
    for g in range(N_GROUPS):
        ein = jnp.where(gsel == g, lt[SUBLANES * (g + 1):SUBLANES * (g + 2), :], ein)
    v1 = jnp.max(ein, axis=0, keepdims=True)
    i1 = jnp.min(jnp.where(ein == v1, row, SUBLANES), axis=0, keepdims=True)
    rest = jnp.where(row == i1, -jnp.inf, ein)
    v2 = jnp.max(rest, axis=0, keepdims=True)
    i2 = jnp.min(jnp.where(rest == v2, row, SUBLANES), axis=0, keepdims=True)
    t2 = jnp.exp(v2 - v1)
    p1 = 1.0 / (1.0 + t2)
    e0 = (gsel * E_PER_G + i1).astype(F32)
    e1 = (gsel * E_PER_G + i2).astype(F32)
    return jnp.concatenate([e0, e1, pg * p1, pg * (t2 * p1), jnp.zeros((SUBLANES - 4, tm), F32)], axis=0)


def _outproj(hm, hg, w_out, x2, mod3, g, b, wrg_t, wre_t, br, *, S, tb, nh, layer):
    N, D = x2.shape
    tm = tb * nh
    tpb = S // tm
    kern = functools.partial(_outproj_kernel, tb=tb, nh=nh)
    return pl.pallas_call(
        kern,
        grid=(N // tm,),
        in_specs=[pl.BlockSpec((tm, M_W), lambda i: (i, 0)),
                  pl.BlockSpec((tm, G_W), lambda i: (i, 0)),
                  pl.BlockSpec((M_W + G_W, D), lambda i: (0, 0), pipeline_mode=pl.Buffered(1)),
                  pl.BlockSpec((tm, D), lambda i: (i, 0)),
                  pl.BlockSpec((1, 6, D), lambda i: (i // tpb, 0, 0)),
                  pl.BlockSpec((1, D), lambda i: (0, 0)),
                  pl.BlockSpec((1, D), lambda i: (0, 0)),
                  pl.BlockSpec((1, N_GROUPS, D), lambda i: (layer, 0, 0)),
                  pl.BlockSpec((1, N_EXP, D), lambda i: (layer, 0, 0)),
                  pl.BlockSpec((1, LANES), lambda i: (0, 0))],
        out_specs=[pl.BlockSpec((tm, D), lambda i: (i, 0)),
                   pl.BlockSpec((tm, D), lambda i: (i, 0)),
                   pl.BlockSpec((nh, SUBLANES, tb), lambda i: (i, 0, 0))],
        out_shape=[jax.ShapeDtypeStruct((N, D), F32),
                   jax.ShapeDtypeStruct((N, D), BF16),
                   jax.ShapeDtypeStruct((N // tb, SUBLANES, tb), F32)],
        scratch_shapes=[pltpu.VMEM((M_W + G_W, D), BF16), pltpu.VMEM((D, 2 * LANES), BF16)],
        compiler_params=_cparams(),
        name="outproj",
    )(hm, hg, w_out, x2, mod3, g, b, wrg_t, wre_t, br)


def _slots_per_tile(tb):
    worst = 2 * tb + N_EXP * (GRAN - 1)
    return -(-worst // LANES) * LANES


def _ffn_tiles(n_tok, tb):
    worst_rows = 2 * n_tok + (n_tok // tb) * N_EXP * (GRAN - 1)
    return -(-worst_rows // FFN_TM) + N_EXP


def _route_kernel(rr_ref, u_ref, lt_ref, srow_ref, col_ref, gd_ref, meta_ref, mg_ref, part_ref,
                  *, NT, tb, TM):
    iota_e = lax.broadcasted_iota(jnp.int32, (N_EXP, tb), 0).astype(F32)
    glane = lax.broadcasted_iota(jnp.int32, (N_EXP, LANES), 1).astype(F32)
    ltri = lt_ref[...]

    def prefix_e(col):
        return jnp.dot(ltri, jnp.broadcast_to(col, (N_EXP, LANES)),
                       preferred_element_type=F32, precision=HIGHEST)[:, 0:1]

    def p1(j, run8):
        r = rr_ref[j]
        oh0 = jnp.where(iota_e == r[0:1, :], 1.0, 0.0)
        oh1 = jnp.where(iota_e == r[1:2, :], 1.0, 0.0)
        cum0 = jnp.dot(oh0.astype(BF16), u_ref[...], preferred_element_type=F32)
        cum1 = jnp.dot(oh1.astype(BF16), u_ref[...], preferred_element_type=F32)
        c0 = jnp.sum(oh0, axis=1, keepdims=True)
        n8 = jnp.floor((c0 + jnp.sum(oh1, axis=1, keepdims=True) + (GRAN - 1.0)) * (1.0 / GRAN))
        lo8 = prefix_e(n8)
        s0 = jnp.sum(oh0 * (GRAN * lo8 + cum0 - 1.0), axis=0, keepdims=True)
        s1 = jnp.sum(oh1 * (GRAN * lo8 + c0 + cum1 - 1.0), axis=0, keepdims=True)
        info = jnp.concatenate([s0, s1, r[2:4, :], jnp.zeros((SUBLANES - 4, tb), F32)], axis=0)
        srow_ref[j] = info
        col_ref[pl.ds(pl.multiple_of(j * tb, tb), tb), :] = jnp.concatenate(
            [info, jnp.zeros((LANES - SUBLANES, tb), F32)], axis=0).T
        mg = jnp.where((lo8 <= glane) & (glane < lo8 + n8), 1.0, 0.0)
        mg_ref[j] = mg
        part = jnp.sum(mg * (run8 + glane - lo8), axis=0, keepdims=True)
        gcnt = jnp.broadcast_to(jnp.sum(n8, axis=0, keepdims=True), (1, LANES))
        part_ref[j] = jnp.concatenate([part, gcnt, jnp.zeros((SUBLANES - 2, LANES), F32)], axis=0)
        return run8 + n8

    tot8 = lax.fori_loop(0, NT, p1, jnp.zeros((N_EXP, 1), F32), unroll=4 if NT % 4 == 0 else 1)
    seg_t = jnp.floor((tot8 * GRAN + (TM - 1.0)) * (1.0 / TM))
    base_t = prefix_e(seg_t)
    base8 = base_t * (TM // GRAN)
    lane1 = lax.broadcasted_iota(jnp.int32, (1, LANES), 1)

    def p2(j, carry):
        pr = part_ref[j]
        dst = (pr[0:1, :] + jnp.sum(mg_ref[j] * base8, axis=0, keepdims=True)) * GRAN
        gd_ref[j] = jnp.where(lane1 == G_LAST, pr[1:2, :], dst).astype(jnp.int32)
        return carry

    lax.fori_loop(0, NT, p2, 0, unroll=4 if NT % 4 == 0 else 1)
    eye = jnp.where(glane == lax.broadcasted_iota(jnp.int32, (N_EXP, LANES), 0).astype(F32), 1.0, 0.0)
    tail_row = jnp.sum(eye * ((base8 + tot8) * GRAN), axis=0, keepdims=True)
    tail_n8 = jnp.sum(eye * (seg_t * (TM // GRAN) - tot8), axis=0, keepdims=True)
    nv_l = jnp.broadcast_to(jnp.sum(seg_t, axis=0, keepdims=True), (1, LANES))
    gd_ref[NT] = jnp.where(lane1 == G_LAST, nv_l, tail_row).astype(jnp.int32)
    gd_ref[NT + 1] = tail_n8.astype(jnp.int32)
    ti = lax.broadcasted_iota(jnp.int32, (N_EXP, tb), 1).astype(F32)
    te = jnp.sum(jnp.where(base_t <= ti, 1.0, 0.0), axis=0, keepdims=True) - 1.0
    nv = jnp.broadcast_to(jnp.sum(seg_t, axis=0, keepdims=True), (1, tb))
    meta_ref[...] = jnp.concatenate([te, nv, jnp.zeros((SUBLANES - 2, tb), F32)],
                                    axis=0).astype(jnp.int32)


def _route(rrow, u_cnt, ltri, *, TM):
    NT, _, tb = rrow.shape
    kern = functools.partial(_route_kernel, NT=NT, tb=tb, TM=TM)
    full3 = lambda i: (0, 0, 0)
    return pl.pallas_call(
        kern,
        grid=(1,),
        in_specs=[pl.BlockSpec((NT, SUBLANES, tb), full3),
                  pl.BlockSpec((tb, tb), lambda i: (0, 0)),
                  pl.BlockSpec((N_EXP, N_EXP), lambda i: (0, 0))],
        out_specs=[pl.BlockSpec((NT, SUBLANES, tb), full3),
                   pl.BlockSpec((NT * tb, LANES), lambda i: (0, 0)),
                   pl.BlockSpec((NT + 2, 1, LANES), full3),
                   pl.BlockSpec((SUBLANES, tb), lambda i: (0, 0))],
        out_shape=[jax.ShapeDtypeStruct((NT, SUBLANES, tb), F32),
                   jax.ShapeDtypeStruct((NT * tb, LANES), F32),
                   jax.ShapeDtypeStruct((NT + 2, 1, LANES), jnp.int32),
                   jax.ShapeDtypeStruct((SUBLANES, tb), jnp.int32)],
        scratch_shapes=[pltpu.VMEM((NT, N_EXP, LANES), F32), pltpu.VMEM((NT, SUBLANES, LANES), F32)],
        compiler_params=_cparams(),
        name="route",
    )(rrow, u_cnt, ltri)


_HI_MASK = 0xFFFF0000


def _pack_halves(x):
    c = x.shape[1] // 2
    lo = lax.bitcast_convert_type(x[:, :c], U32)
    hi = lax.bitcast_convert_type(x[:, c:], U32)
    return (lo >> 16) | (hi & U32(_HI_MASK))


def _unpack_halves(w):
    lo = lax.bitcast_convert_type(w << 16, F32)
    hi = lax.bitcast_convert_type(w & U32(_HI_MASK), F32)
    return jnp.concatenate([lo, hi], axis=1).astype(BF16)


def _granule_copy(src_ref, src_row, dst_ref, dst_row, sem):
    return pltpu.make_async_copy(src_ref.at[pl.ds(src_row, GRAN), :], dst_ref.at[pl.ds(dst_row, GRAN), :], sem)


def _for_granules(n, body, unroll=4):
    def blk(i, carry):
        for t in range(unroll):
            body(i * unroll + t)
        return carry

    def one(g, carry):
        body(g)
        return carry

    nblk = n // unroll
    lax.fori_loop(0, nblk, blk, 0)
    lax.fori_loop(nblk * unroll, n, one, 0)


def _wait_granules(n, src_ref, dst_ref, sem, n_max):
    b = 1
    while b <= n_max:
        @pl.when((n & b) != 0)
        def _(b=b):
            pltpu.make_async_copy(src_ref.at[pl.ds(0, b * GRAN), :], dst_ref.at[pl.ds(0, b * GRAN), :],
                                  sem).wait()
        b *= 2


def _dispatch_kernel(gd_ref, srow_ref, u_ref, xs_ref, buf, zbuf, sems, *, NT, SL, TM, n_tiles):
    j = pl.program_id(0)
    slot = j % 2
    zsem = sems.at[2]

    def drain(tile, sl):
        _wait_granules(gd_ref[tile, G_LAST], buf.at[sl], xs_ref, sems.at[sl], SL // GRAN)

    def tile_fill(t):
        return pltpu.make_async_copy(zbuf, xs_ref.at[pl.ds(pl.multiple_of(t * TM, TM), TM), :], zsem)

    def zero_fill(wait):
        for e in range(N_EXP):
            n, row0 = gd_ref[NT + 1, e], gd_ref[NT, e]
            b = TM // GRAN // 2
            while b >= 1:
                @pl.when((n & b) != 0)
                def _(b=b, n=n, row0=row0):
                    start = pl.multiple_of(row0 + ((n >> b.bit_length()) << b.bit_length()) * GRAN, GRAN)
                    cp = pltpu.make_async_copy(zbuf.at[pl.ds(0, b * GRAN), :],
                                               xs_ref.at[pl.ds(start, b * GRAN), :], zsem)
                    cp.wait() if wait else cp.start()
                b //= 2

        def zt(t, carry):
            tile_fill(t).wait() if wait else tile_fill(t).start()
            return carry
        lax.fori_loop(gd_ref[NT, G_LAST], n_tiles, zt, 0)

    @pl.when(j == 0)
    def _():
        zbuf[...] = jnp.zeros_like(zbuf)
        zero_fill(False)

    @pl.when(j >= 2)
    def _():
        drain(j - 2, slot)

    s = srow_ref[0]
    rows = lax.broadcasted_iota(jnp.int32, (SL, s.shape[1]), 0).astype(F32)
    m0 = rows == s[0:1, :]
    m1 = rows == s[1:2, :]
    oh = jnp.where(m0 | m1, 1.0, 0.0).astype(BF16)
    dw = u_ref.shape[1] // 2
    buf[slot, :, 0:dw] = _pack_halves(jnp.dot(oh, u_ref[...], preferred_element_type=F32))
    wrow = jnp.sum(jnp.where(m0, s[2:3, :], 0.0) + jnp.where(m1, s[3:4, :], 0.0), axis=1, keepdims=True)
    buf[slot, :, dw:dw + LANES] = lax.bitcast_convert_type(jnp.broadcast_to(wrow, (SL, LANES)), U32)

    def issue(g):
        _granule_copy(buf.at[slot], pl.multiple_of(g * GRAN, GRAN), xs_ref,
                      pl.multiple_of(gd_ref[j, g], GRAN), sems.at[slot]).start()

    _for_granules(gd_ref[j, G_LAST], issue)

    @pl.when(j == NT - 1)
    def _():
        drain(j, slot)
        if NT > 1:
            drain(j - 1, 1 - slot)
        zero_fill(True)


def _dispatch(gd, srow, u2, *, n_tiles, TM):
    N, D = u2.shape
    NT, _, tb = srow.shape
    SL = _slots_per_tile(tb)
    n_rows = n_tiles * TM
    kern = functools.partial(_dispatch_kernel, NT=NT, SL=SL, TM=TM, n_tiles=n_tiles)
    grid_spec = pltpu.PrefetchScalarGridSpec(
        num_scalar_prefetch=1,
        grid=(NT,),
        in_specs=[pl.BlockSpec((1, SUBLANES, tb), lambda j, gd: (j, 0, 0)),
                  pl.BlockSpec((tb, D), lambda j, gd: (j, 0))],
        out_specs=pl.BlockSpec(memory_space=pl.ANY),
        scratch_shapes=[pltpu.VMEM((2, SL, D // 2 + LANES), U32), pltpu.VMEM((TM, D // 2 + LANES), U32),
                        pltpu.SemaphoreType.DMA((3,))],
    )
    return pl.pallas_call(
        kern,
        grid_spec=grid_spec,
        out_shape=jax.ShapeDtypeStruct((n_rows, D // 2 + LANES), U32),
        compiler_params=_cparams(),
        name="dispatch",
    )(gd, srow, u2)


def _ffn_kernel(te_ref, nv_ref, xs_ref, wg_ref, wu_ref, wd_ref, o_ref, wgb, wub, wdb, sg, su, sd, slot_ref,
                sems):
    i = pl.program_id(0)
    nv = nv_ref[0]
    e = te_ref[i]

    def weight_copies(ex, sl):
        return (pltpu.make_async_copy(wg_ref.at[ex], sg.at[sl], sems.at[sl]),
                pltpu.make_async_copy(wu_ref.at[ex], su.at[sl], sems.at[sl]),
                pltpu.make_async_copy(wd_ref.at[ex], sd.at[sl], sems.at[sl]))

    @pl.when(i == 0)
    def _():
        slot_ref[0] = 0
        for cp in weight_copies(e, 0):
            cp.start()

    @pl.when((i < nv) & ((i == 0) | (e != te_ref[jnp.maximum(i - 1, 0)])))
    def _():
        sl = slot_ref[0]
        for cp in weight_copies(e, sl):
            cp.wait()
        wgb[...] = sg[sl].astype(BF16)
        wub[...] = su[sl].astype(BF16)
        wdb[...] = sd[sl].astype(BF16)
        nxt = lax.while_loop(lambda t: (t < nv) & (te_ref[jnp.minimum(t, nv - 1)] == e), lambda t: t + 1, i + 1)

        @pl.when(nxt < nv)
        def _():
            for cp in weight_copies(te_ref[nxt], 1 - sl):
                cp.start()
        slot_ref[0] = 1 - sl

    @pl.when(i < nv)
    def _():
        nsub = FFN_SUB
        hm = xs_ref.shape[0] // nsub
        dw = o_ref.shape[1]
        halves = tuple(slice(hm * j, hm * (j + 1)) for j in range(nsub))
        x = [_unpack_halves(xs_ref[r, 0:dw]) for r in halves]
        g = [jnp.dot(x[j], wgb[...], preferred_element_type=F32) for j in range(nsub)]
        u = [jnp.dot(x[j], wub[...], preferred_element_type=F32) for j in range(nsub)]
        h = [(g[j] * _sigmoid(g[j]) * u[j]).astype(BF16) for j in range(nsub)]
        y = [jnp.dot(h[j], wdb[...], preferred_element_type=F32) for j in range(nsub)]
        for j in range(nsub):
            wt = lax.bitcast_convert_type(xs_ref[halves[j], dw:dw + LANES], F32)
            yw = y[j] * jnp.concatenate([wt] * (2 * dw // LANES), axis=1)
            o_ref[halves[j], :] = _pack_halves(yw.astype(BF16).astype(F32))

    @pl.when(i >= nv_ref[0])
    def _():
        o_ref[...] = jnp.zeros_like(o_ref)


def _ffn(te, nv, xs, wg, wu, wd, *, TM):
    P, XW = xs.shape
    DW = XW - LANES
    D = 2 * DW
    n_tiles = P // TM
    grid_spec = pltpu.PrefetchScalarGridSpec(
        num_scalar_prefetch=2,
        grid=(n_tiles,),
        in_specs=[pl.BlockSpec((TM, XW), lambda i, te, nv: (jnp.maximum(jnp.minimum(i, nv[0] - 1), 0), 0)),
                  pl.BlockSpec(memory_space=pl.ANY),
                  pl.BlockSpec(memory_space=pl.ANY),
                  pl.BlockSpec(memory_space=pl.ANY)],
        out_specs=pl.BlockSpec((TM, DW), lambda i, te, nv: (i, 0)),
        scratch_shapes=[pltpu.VMEM((D, D_EXP), BF16), pltpu.VMEM((D, D_EXP), BF16),
                        pltpu.VMEM((D_EXP, D), BF16),
                        pltpu.VMEM((2, D, D_EXP), F32), pltpu.VMEM((2, D, D_EXP), F32),
                        pltpu.VMEM((2, D_EXP, D), F32), pltpu.SMEM((1,), jnp.int32),
                        pltpu.SemaphoreType.DMA((2,))],
    )
    return pl.pallas_call(
        _ffn_kernel,
        grid_spec=grid_spec,
        out_shape=jax.ShapeDtypeStruct((P, DW), U32),
        compiler_params=_cparams(),
        name="ffn",
    )(te, nv, xs, wg, wu, wd)


def _combine_kernel(gd_ref, ys_ref, col_ref, x1_ref, mod_ref, g_ref, b_ref, o_ref, buf, sems, *, NT, SL):
    j = pl.program_id(0)
    slot = j % 2

    def fetch(tile, sl):
        def f(g):
            _granule_copy(ys_ref, pl.multiple_of(gd_ref[tile, g], GRAN), buf.at[sl],
                          pl.multiple_of(g * GRAN, GRAN), sems.at[sl]).start()
        _for_granules(gd_ref[tile, G_LAST], f)

    @pl.when(j == 0)
    def _():
        fetch(0, 0)

    @pl.when(j + 1 < NT)
    def _():
        fetch(j + 1, 1 - slot)

    ng = gd_ref[j, G_LAST]

    _wait_granules(ng, ys_ref, buf.at[slot], sems.at[slot], SL // GRAN)

    rows = lax.broadcasted_iota(jnp.int32, (SL, 1), 0)
    yb = _unpack_halves(jnp.where(rows < ng * GRAN, buf[slot], U32(0)))
    col = col_ref[...]
    tb = col.shape[0]
    lanes = lax.broadcasted_iota(jnp.int32, (tb, SL), 1).astype(F32)
    sel = jnp.where((lanes == col[:, 0:1]) | (lanes == col[:, 1:2]), 1.0, 0.0).astype(BF16)
    y = jnp.dot(sel, yb, preferred_element_type=F32)
    mod = mod_ref[0]
    z = ALPHA * x1_ref[...] + (1.0 + mod[5:6, :]) * y
    o_ref[...] = _layer_norm(z, g_ref[...], b_ref[...])


def _combine(gd, ys, col, x1, mod3, g, b, *, S, tb):
    N, D = x1.shape
    NT = N // tb
    tpb = S // tb
    SL = _slots_per_tile(tb)
    kern = functools.partial(_combine_kernel, NT=NT, SL=SL)
    grid_spec = pltpu.PrefetchScalarGridSpec(
        num_scalar_prefetch=1,
        grid=(NT,),
        in_specs=[pl.BlockSpec(memory_space=pl.ANY),
                  pl.BlockSpec((tb, LANES), lambda j, gd: (j, 0)),
                  pl.BlockSpec((tb, D), lambda j, gd: (j, 0)),
                  pl.BlockSpec((1, 6, D), lambda j, gd: (j // tpb, 0, 0)),
                  pl.BlockSpec((1, D), lambda j, gd: (0, 0)),
                  pl.BlockSpec((1, D), lambda j, gd: (0, 0))],
        out_specs=pl.BlockSpec((tb, D), lambda j, gd: (j, 0)),
        scratch_shapes=[pltpu.VMEM((2, SL, D // 2), U32), pltpu.SemaphoreType.DMA((2,))],
    )
    return pl.pallas_call(
        kern,
        grid_spec=grid_spec,
        out_shape=jax.ShapeDtypeStruct((N, D), F32),
        compiler_params=_cparams(),
        name="combine",
    )(gd, ys, col, x1, mod3, g, b)


def _layer(x, c, l, w_ada, b_ada, w_in, w_conv, b_conv, b_igate, b_fgate, mlstm_norm_g, w_gla_a, b_gla_a,
           gla_norm_g, w_out, ln1_g, ln1_b, w_route_group, b_route_group, w_route_expert, b_route_expert,
           w_gate, w_up, w_down, ln2_g, ln2_b):
    B, S, D = x.shape
    N = B * S
    x2 = x.reshape(N, D)
    tm_in = min(512, S)
    tm = min(256, S)
    lm = min(256, S)

    mod3 = _ada(c, w_ada[l], b_ada[l]).reshape(B, 6, D)

    wa_pad = jnp.zeros((LANES, G_KW), F32).at[SM_A:SM_A + G_RANK].set(w_gla_a[l]).astype(BF16)
    bg = (jnp.zeros((2 * SUBLANES, 1), F32).at[0:M_HEADS, 0].set(b_igate[l])
          .at[SUBLANES:SUBLANES + M_HEADS, 0].set(b_fgate[l]))
    oa, la, g3 = _inproj(x2, mod3, jnp.swapaxes(w_in, 1, 2), w_conv[l], b_conv[l].reshape(1, -1), wa_pad,
                         b_gla_a[l].reshape(1, -1), bg, S=S, tm=tm_in, lm=lm, layer=l)

    u_tri = jnp.asarray(np.triu(np.ones((lm, lm), np.float32)))
    nb = 4 if B % 4 == 0 else (2 if B % 2 == 0 else 1)
    ts = min(512, S)
    hm = _mlstm(oa, g3, u_tri, mlstm_norm_g[l].reshape(1, -1), B=B, S=S, L=lm, nb=nb, ts=ts)
    w3_np, mk_np = _gla_consts()
    hg = _gla(oa, la, jnp.asarray(w3_np, BF16), jnp.asarray(mk_np), gla_norm_g[l].reshape(1, -1), B=B, S=S,
              nb=nb, ts=ts)

    br = (jnp.zeros((1, LANES), F32).at[0, 0:N_GROUPS].set(b_route_group[l])
          .at[0, SUBLANES:SUBLANES + N_EXP].set(b_route_expert[l]))
    x1, u2, rrow = _outproj(hm, hg, w_out[l], x2, mod3, ln1_g[l].reshape(1, -1), ln1_b[l].reshape(1, -1),
                            jnp.swapaxes(w_route_group, 1, 2), jnp.swapaxes(w_route_expert, 1, 2), br,
                            S=S, tb=tm, nh=4 if S % (4 * tm) == 0 else 1, layer=l)

    u_cnt = jnp.asarray(np.triu(np.ones((tm, tm), np.float32)), BF16)
    ltri = jnp.asarray(np.tril(np.ones((N_EXP, N_EXP), np.float32), -1))
    srow, col, gd3, meta = _route(rrow, u_cnt, ltri, TM=FFN_TM)
    gd = gd3.reshape(N // tm + 2, LANES)
    n_tiles = _ffn_tiles(N, tm)
    te, nv = meta[0, :n_tiles], meta[1, 0:1]

    xs = _dispatch(gd, srow, u2, n_tiles=n_tiles, TM=FFN_TM)
    ys = _ffn(te, nv, xs, w_gate[l], w_up[l], w_down[l], TM=FFN_TM)
    out = _combine(gd, ys, col, x1, mod3, ln2_g[l].reshape(1, -1), ln2_b[l].reshape(1, -1), S=S, tb=tm)
    return out.reshape(B, S, D)


def kernel(x, c, w_ada, b_ada, w_in, w_conv, b_conv, b_igate, b_fgate, mlstm_norm_g, w_gla_a, b_gla_a,
           gla_norm_g, w_out, ln1_g, ln1_b, w_route_group, b_route_group, w_route_expert, b_route_expert,
           w_gate, w_up, w_down, ln2_g, ln2_b):
    for l in range(DEPTH):
        x = _layer(x, c, l, w_ada, b_ada, w_in, w_conv, b_conv, b_igate, b_fgate, mlstm_norm_g, w_gla_a,
                   b_gla_a, gla_norm_g, w_out, ln1_g, ln1_b, w_route_group, b_route_group, w_route_expert,
                   b_route_expert, w_gate, w_up, w_down, ln2_g, ln2_b)
    return x
```

```python
import functools

import numpy as np
import jax
import jax.numpy as jnp
from jax import lax
from jax.experimental import pallas as pl
from jax.experimental.pallas import tpu as pltpu

F32 = jnp.float32
BF16 = jnp.bfloat16
U32 = jnp.uint32
HIGHEST = lax.Precision.HIGHEST

DEPTH = 1
M_HEADS = 4
M_HD = 128
M_W = M_HEADS * M_HD
CONV_W = 4
G_HEADS = 4
G_DK = 64
G_DV = 128
G_W = G_HEADS * G_DV
G_KW = G_HEADS * G_DK
G_RANK = 16
G_TAU = 16.0
G_CHUNK = 64
N_GROUPS = 4
E_PER_G = 8
N_EXP = N_GROUPS * E_PER_G
D_EXP = 512
ALPHA = (2 * DEPTH) ** 0.25
LN_EPS = 1e-5

LANES = 128
SUBLANES = 8
VMEM_LIMIT = 48 * 1024 * 1024

C_QK = 0
C_VO = 1024
C_GQK = 2048
C_GV = 2560
C_GG = 3072
C_SMALL = 3584
C_TOT = 3712
SM_I, SM_F, SM_A = 0, 8, 16
IN_GATES = 4 * M_W
IN_G = IN_GATES + 2 * M_HEADS
IN_GA = IN_G + 2 * G_KW + 2 * G_W
IN_TOT = IN_GA + G_RANK

FFN_TM = 512
FFN_SUB = 2
GRAN = SUBLANES
G_LAST = LANES - 1


def _cparams(n_axes=1):
    return pltpu.CompilerParams(dimension_semantics=("arbitrary",) * n_axes,
                                vmem_limit_bytes=VMEM_LIMIT)


def _sigmoid(x):
    return 1.0 / (1.0 + jnp.exp(-x))


def _log_sigmoid(x):
    return jnp.minimum(x, 0.0) - jnp.log(1.0 + jnp.exp(-jnp.abs(x)))


def _ada_kernel(c_ref, w_ref, b_ref, o_ref):
    c = c_ref[...]
    ca = (c * _sigmoid(c)).astype(BF16)
    o_ref[...] = jnp.dot(ca, w_ref[...].astype(BF16), preferred_element_type=F32) + b_ref[...]


def _ada(c, w, b):
    B, D = c.shape
    n_out = w.shape[1]
    tn = 1024
    return pl.pallas_call(
        _ada_kernel,
        grid=(n_out // tn,),
        in_specs=[pl.BlockSpec((B, D), lambda j: (0, 0)),
                  pl.BlockSpec((D, tn), lambda j: (0, j)),
                  pl.BlockSpec((1, tn), lambda j: (0, j))],
        out_specs=pl.BlockSpec((B, tn), lambda j: (0, j)),
        out_shape=jax.ShapeDtypeStruct((B, n_out), F32),
        compiler_params=_cparams(),
        name="ada",
    )(c, w, b.reshape(1, n_out))


def _inproj_kernel(x_ref, mod_ref, win_ref, wc_ref, bc_ref, wa_ref, ba_ref, bg_ref,
                   oa_ref, la_ref, g_ref, halo_ref, w_ref, *, tm, tpb, lm):
    i = pl.program_id(0)

    @pl.when(i == 0)
    def _():
        rc = 2 * LANES
        for r in range(0, IN_GATES, rc):
            w_ref[:, r:r + rc] = win_ref[0, r:r + rc, :].T.astype(BF16)
        for r in range(0, C_SMALL - C_GQK, rc):
            w_ref[:, C_GQK + r:C_GQK + r + rc] = win_ref[0, IN_G + r:IN_G + r + rc, :].T.astype(BF16)
        gates = win_ref[0, IN_GATES:IN_G, :]
        z = lambda n: jnp.zeros((n, gates.shape[1]), F32)
        small = jnp.concatenate([gates[0:M_HEADS], z(SM_F - M_HEADS), gates[M_HEADS:2 * M_HEADS],
                                 z(SM_A - SM_F - M_HEADS), win_ref[0, IN_GA:IN_TOT, :],
                                 z(LANES - SM_A - G_RANK)], axis=0)
        w_ref[:, C_SMALL:C_TOT] = small.T.astype(BF16)

    @pl.when(i % tpb == 0)
    def _():
        halo_ref[0:SUBLANES, :] = jnp.zeros((SUBLANES, halo_ref.shape[1]), F32)

    mod = mod_ref[0]
    u = (x_ref[...] * (1.0 + mod[1:2, :]) + mod[0:1, :]).astype(BF16)

    def proj(c0, c1):
        return jnp.dot(u, w_ref[:, c0:c1], preferred_element_type=F32)

    p = proj(C_QK, C_QK + 2 * M_W)
    halo_ref[SUBLANES:SUBLANES + tm, :] = p
    acc = bc_ref[...] + wc_ref[CONV_W - 1:CONV_W, :] * p
    for j in range(CONV_W - 1):
        acc = acc + wc_ref[j:j + 1, :] * halo_ref[pl.ds(SUBLANES - (CONV_W - 1) + j, tm), :]
    halo_ref[0:SUBLANES, :] = p[tm - SUBLANES:, :]
    qk = acc * _sigmoid(acc)
    oa_ref[:, C_QK:C_QK + M_W] = qk[:, :M_W].astype(BF16)
    oa_ref[:, C_QK + M_W:C_QK + 2 * M_W] = (qk[:, M_W:] * (M_HD ** -0.5)).astype(BF16)

    p = proj(C_VO, C_VO + 2 * M_W)
    oa_ref[:, C_VO:C_VO + 2 * M_W] = p.astype(BF16)

    p = proj(C_GQK, C_GQK + G_KW)
    oa_ref[:, C_GQK:C_GQK + G_KW] = (p * (G_DK ** -0.5)).astype(BF16)
    p = proj(C_GQK + G_KW, C_SMALL)
    oa_ref[:, C_GQK + G_KW:C_SMALL] = p.astype(BF16)

    ps = proj(C_SMALL, C_TOT)
    la = jnp.dot(ps.astype(BF16), wa_ref[...], preferred_element_type=F32) + ba_ref[...]
    la_ref[...] = _log_sigmoid(la) * (1.0 / G_TAU)
    pt = ps.T
    gi = pt[SM_I:SM_I + SUBLANES, :] + bg_ref[0:SUBLANES, :]
    gf = _log_sigmoid(pt[SM_F:SM_F + SUBLANES, :] + bg_ref[SUBLANES:2 * SUBLANES, :])
    for j in range(tm // lm):
        g_ref[j, 0:SUBLANES, :] = gi[:, j * lm:(j + 1) * lm]
        g_ref[j, SUBLANES:2 * SUBLANES, :] = gf[:, j * lm:(j + 1) * lm]


def _inproj(x2, mod3, w_in, w_conv, b_conv, wa_pad, b_gla, bg, *, S, tm, lm, layer):
    N, D = x2.shape
    tpb = S // tm
    kern = functools.partial(_inproj_kernel, tm=tm, tpb=tpb, lm=lm)
    return pl.pallas_call(
        kern,
        grid=(N // tm,),
        in_specs=[pl.BlockSpec((tm, D), lambda i: (i, 0)),
                  pl.BlockSpec((1, 6, D), lambda i: (i // tpb, 0, 0)),
                  pl.BlockSpec((1, IN_TOT, D), lambda i: (layer, 0, 0), pipeline_mode=pl.Buffered(1)),
                  pl.BlockSpec((CONV_W, 2 * M_W), lambda i: (0, 0)),
                  pl.BlockSpec((1, 2 * M_W), lambda i: (0, 0)),
                  pl.BlockSpec((LANES, G_KW), lambda i: (0, 0)),
                  pl.BlockSpec((1, G_KW), lambda i: (0, 0)),
                  pl.BlockSpec((2 * SUBLANES, 1), lambda i: (0, 0))],
        out_specs=[pl.BlockSpec((tm, C_SMALL), lambda i: (i, 0)),
                   pl.BlockSpec((tm, G_KW), lambda i: (i, 0)),
                   pl.BlockSpec((tm // lm, 2 * SUBLANES, lm), lambda i: (i, 0, 0))],
        out_shape=[jax.ShapeDtypeStruct((N, C_SMALL), BF16),
                   jax.ShapeDtypeStruct((N, G_KW), F32),
                   jax.ShapeDtypeStruct((N // lm, 2 * SUBLANES, lm), F32)],
        scratch_shapes=[pltpu.VMEM((SUBLANES + tm, 2 * M_W), F32), pltpu.VMEM((D, C_TOT), BF16)],
        compiler_params=_cparams(),
        name="inproj",
    )(x2, mod3, w_in, w_conv, b_conv, wa_pad, b_gla, bg)


def _mlstm_sel():
    sel = np.zeros((2 * LANES, 2 * M_HEADS * M_HD), np.float32)
    for j in range(2 * M_HEADS):
        src = (SUBLANES if j < M_HEADS else 3 * SUBLANES) + j % M_HEADS
        sel[src, M_HD * j:M_HD * (j + 1)] = 1.0
        sel[LANES + src, M_HD * j:M_HD * (j + 1)] = 1.0
    return sel


def _mlstm_kernel(qk_ref, vo_ref, g_ref, u_ref, gain_ref, sel_ref, out_ref, c_ref, zt_ref, a_ref, dec_ref, m_ref,
                  *, L, NC, nb):
    @pl.when(pl.program_id(1) == 0)
    def _():
        c_ref[...] = jnp.zeros_like(c_ref)
        m_ref[...] = jnp.zeros_like(m_ref)

    tril = (lax.broadcasted_iota(jnp.int32, (L, L), 0) >= lax.broadcasted_iota(jnp.int32, (L, L), 1))
    ones_v = jnp.ones((L, M_HD), BF16)
    zpad = jnp.zeros((LANES - 4 * SUBLANES, L), F32)

    order = [(bi, c) for bi in range(nb) for c in range(NC)]
    f_all = jnp.concatenate([g_ref[bi, c, SUBLANES:2 * SUBLANES, :] for bi, c in order], axis=0)
    i_all = jnp.concatenate([g_ref[bi, c, 0:SUBLANES, :] for bi, c in order], axis=0)
    b_all = jnp.dot(f_all, u_ref[...], preferred_element_type=F32, precision=HIGHEST)
    a_all = i_all - b_all
    lane_all = lax.broadcasted_iota(jnp.int32, a_all.shape, 1)
    g_all = a_all
    s = 1
    while s < L:
        g_all = jnp.maximum(g_all, jnp.where(lane_all >= s, pltpu.roll(g_all, s, 1), -jnp.inf))
        s *= 2
    for bi in range(nb):
        m_prev = m_ref[bi][:, 0:1]
        for c in range(NC):
            ci = bi * NC + c
            r8 = slice(SUBLANES * ci, SUBLANES * (ci + 1))
            a, b = a_all[r8], b_all[r8]
            a_ref[ci] = a
            M = jnp.maximum(g_all[r8], m_prev)
            ML = M[:, L - 1:L]
            Z = jnp.concatenate([M, jnp.exp(m_prev - M), jnp.exp(-(b + M)), jnp.exp(a - ML), zpad],
                                axis=0)
            zt_ref[ci] = Z.T
            dec_ref[ci] = jnp.broadcast_to(jnp.exp(m_prev - ML), (SUBLANES, 2 * M_HD))
            m_prev = b[:, L - 1:L] + ML
        m_ref[bi] = jnp.broadcast_to(m_prev, (SUBLANES, LANES))

    chains = [(bi, h) for bi in range(nb) for h in range(M_HEADS)]
    nt = (((1,), (1,)), ((), ()))
    tn = (((0,), (0,)), ((), ()))

    def chunk(c, carry):
        rows = pl.ds(pl.multiple_of(c * L, L), L)
        Zt = [zt_ref[bi * NC + c] for bi in range(nb)]
        a = [a_ref[bi * NC + c] for bi in range(nb)]
        dec = [dec_ref[bi * NC + c] for bi in range(nb)]
        hs = [slice(h * M_HD, (h + 1) * M_HD) for h in range(M_HEADS)]
        hs2 = [slice(M_W + h * M_HD, M_W + (h + 1) * M_HD) for h in range(M_HEADS)]
        q = [qk_ref[bi, rows, hs[h]] for bi, h in chains]
        k = [qk_ref[bi, rows, hs2[h]] for bi, h in chains]
        vext = [jnp.concatenate([vo_ref[bi, rows, hs[h]], ones_v], axis=1) for bi, h in chains]
        cst = [c_ref[bi * M_HEADS + h] for bi, h in chains]
        n = range(len(chains))
        sc = [lax.dot_general(q[i], k[i], nt, preferred_element_type=F32) for i in n]
        qc = [jnp.dot(q[i], cst[i].astype(BF16), preferred_element_type=F32) for i in n]
        pm = [(sc[i] * jnp.exp(jnp.where(tril, a[bi][h:h + 1, :] - Zt[bi][:, h:h + 1], -jnp.inf))).astype(BF16)
              for i, (bi, h) in enumerate(chains)]
        pv = [jnp.dot(pm[i], vext[i], preferred_element_type=F32) for i in n]
        rep = []
        for bi in range(nb):
            zh = Zt[bi].astype(BF16)
            zl = (Zt[bi] - zh.astype(F32)).astype(BF16)
            rep.append(jnp.dot(jnp.concatenate([zh, zl], axis=1), sel_ref[...], preferred_element_type=F32))
        e_inter = [rep[bi][:, M_HD * h:M_HD * (h + 1)] for bi, h in chains]
        w_state = [rep[bi][:, M_HD * (M_HEADS + h):M_HD * (M_HEADS + h + 1)] for bi, h in chains]
        kw = [(w_state[i] * k[i].astype(F32)).astype(BF16) for i in n]
        upd = [lax.dot_general(kw[i], vext[i], tn, preferred_element_type=F32) for i in n]
        for i, (bi, h) in enumerate(chains):
            c_ref[bi * M_HEADS + h] = dec[bi][h:h + 1, :] * cst[i] + upd[i]
            nd = pv[i] + jnp.concatenate([e_inter[i], e_inter[i]], axis=1) * qc[i]
            hh = nd[:, :M_HD] / jnp.maximum(jnp.abs(nd[:, M_HD:]),
                                            Zt[bi][:, 2 * SUBLANES + h:2 * SUBLANES + h + 1])
            hh = _sigmoid(vo_ref[bi, rows, hs2[h]].astype(F32)) * hh
            hn = hh * lax.rsqrt(jnp.mean(hh * hh, axis=-1, keepdims=True) + LN_EPS)
            out_ref[bi, rows, hs[h]] = (hn * gain_ref[:, hs[h]]).astype(BF16)
        return carry

    lax.fori_loop(0, NC, chunk, 0)


def _mlstm(oa, g3, u_tri, gain, *, B, S, L, nb, ts):
    N = oa.shape[0]
    NC = ts // L
    oa3 = oa.reshape(B, S, oa.shape[1])
    g4 = g3.reshape(B, S // L, 2 * SUBLANES, L)
    sel = jnp.asarray(_mlstm_sel(), BF16)
    kern = functools.partial(_mlstm_kernel, L=L, NC=NC, nb=nb)
    out = pl.pallas_call(
        kern,
        grid=(B // nb, S // ts),
        in_specs=[pl.BlockSpec((nb, ts, 2 * M_W), lambda b, t: (b, t, C_QK // (2 * M_W))),
                  pl.BlockSpec((nb, ts, 2 * M_W), lambda b, t: (b, t, C_VO // (2 * M_W))),
                  pl.BlockSpec((nb, NC, 2 * SUBLANES, L), lambda b, t: (b, t, 0, 0)),
                  pl.BlockSpec((L, L), lambda b, t: (0, 0)),
                  pl.BlockSpec((1, M_W), lambda b, t: (0, 0)),
                  pl.BlockSpec(sel.shape, lambda b, t: (0, 0))],
        out_specs=pl.BlockSpec((nb, ts, M_W), lambda b, t: (b, t, 0)),
        out_shape=jax.ShapeDtypeStruct((B, S, M_W), BF16),
        scratch_shapes=[pltpu.VMEM((nb * M_HEADS, M_HD, 2 * M_HD), F32),
                        pltpu.VMEM((nb * NC, L, LANES), F32),
                        pltpu.VMEM((nb * NC, SUBLANES, L), F32),
                        pltpu.VMEM((nb * NC, SUBLANES, 2 * M_HD), F32),
                        pltpu.VMEM((nb, SUBLANES, LANES), F32)],
        compiler_params=_cparams(2),
        name="mlstm",
    )(oa3, oa3, g4, u_tri, gain, sel)
    return out.reshape(N, M_W)


_G_LEVELS = 6
_G_XROW = 2 * G_CHUNK + SUBLANES


def _gla_consts():
    L = G_CHUNK
    t = np.arange(L)
    blocks = [(t[None, :] <= t[:, None]).astype(np.float32),
              (t[None, :] > t[:, None]).astype(np.float32),
              np.ones((SUBLANES, L), np.float32)]
    masks = [np.eye(L, dtype=np.float32)]
    m = 1
    while m < L:
        wl = np.zeros((L, L), np.float32)
        for r in range(L):
            r0 = (r // (2 * m)) * 2 * m + m
            if r % (2 * m) >= m:
                wl[r, r0:r + 1] = 1.0
            else:
                wl[r, r + 1:r0] = 1.0
        blocks.append(wl)
        tt, ss = t[:, None], t[None, :]
        masks.append(((tt // (2 * m) == ss // (2 * m)) & (tt % (2 * m) >= m)
                      & (ss % (2 * m) < m)).astype(np.float32))
        m *= 2
    w = np.concatenate(blocks, axis=0)
    w3 = np.concatenate([w, w, w], axis=1)
    mk = np.stack([np.concatenate([x] * G_HEADS, axis=0) for x in masks])
    return w3, mk


def _gla_kernel(qk_ref, v_ref, gg_ref, la_ref, w3_ref, mk_ref, gain_ref, out_ref, st_ref, *, NC, nb):
    L = G_CHUNK

    @pl.when(pl.program_id(1) == 0)
    def _():
        st_ref[...] = jnp.zeros_like(st_ref)

    lane_head = lax.broadcasted_iota(jnp.int32, (L, G_KW), 1) // G_DK
    br = lax.broadcasted_iota(jnp.int32, (2 * G_DV, LANES), 0) < G_DV
    bl = lax.broadcasted_iota(jnp.int32, (2 * G_DV, LANES), 1) < G_DK
    bmask = br == bl
    nt = (((1,), (1,)), ((), ()))
    tn = (((0,), (0,)), ((), ()))

    def chunk(c, carry):
        rows = pl.ds(pl.multiple_of(c * L, L), L)
        X, q, k = [], [], []
        for bi in range(nb):
            la = la_ref[bi, rows, :]
            hi = la.astype(BF16)
            r1 = la - hi.astype(F32)
            mid = r1.astype(BF16)
            lo = (r1 - mid.astype(F32)).astype(BF16)
            stk = jnp.concatenate([hi, mid, lo], axis=0)
            X.append(jnp.exp(jnp.dot(w3_ref[...], stk, preferred_element_type=F32)))
            q.append(qk_ref[bi, rows, 0:G_KW].astype(F32))
            k.append(qk_ref[bi, rows, G_KW:2 * G_KW].astype(F32))

        sc = [[None] * (_G_LEVELS + 1) for _ in range(nb)]
        for lev in range(_G_LEVELS + 1):
            for bi in range(nb):
                if lev == 0:
                    qt, kt = q[bi], k[bi]
                else:
                    xl = X[bi][_G_XROW + L * (lev - 1):_G_XROW + L * lev, :]
                    qt, kt = q[bi] * xl, k[bi] * xl
                q4 = jnp.concatenate([jnp.where(lane_head == h, qt, 0.0) for h in range(G_HEADS)],
                                     axis=0).astype(BF16)
                sc[bi][lev] = lax.dot_general(q4, kt.astype(BF16), nt, preferred_element_type=F32)
        Ab = []
        for bi in range(nb):
            A = sc[bi][0] * mk_ref[0]
            for lev in range(1, _G_LEVELS + 1):
                A = A + sc[bi][lev] * mk_ref[lev]
            Ab.append(A.astype(BF16))

        for bi in range(nb):
            gg = gg_ref[bi, rows, :].astype(F32)
            gate = gg * _sigmoid(gg)
            for p in range(2):
                ls = slice(LANES * p, LANES * (p + 1))
                vp = v_ref[bi, rows, 2 * G_DV * p:2 * G_DV * (p + 1)]
                oi = [jnp.dot(Ab[bi][L * (2 * p + hh):L * (2 * p + hh + 1)],
                              vp[:, G_DV * hh:G_DV * (hh + 1)], preferred_element_type=F32)
                      for hh in range(2)]
                st = st_ref[bi, p]
                qc = (q[bi][:, ls] * X[bi][0:L, ls]).astype(BF16)
                o_inter = lax.dot_general(qc, st.astype(BF16), nt, preferred_element_type=F32)
                kc = (k[bi][:, ls] * X[bi][L:2 * L, ls]).astype(BF16)
                upd = lax.dot_general(vp, kc, tn, preferred_element_type=F32)
                dec = X[bi][2 * L:2 * L + 1, ls]
                st_ref[bi, p] = jnp.where(bmask, dec * st + upd, 0.0)
                for hh in range(2):
                    o = o_inter[:, G_DV * hh:G_DV * (hh + 1)] + oi[hh]
                    hn = o * lax.rsqrt(jnp.mean(o * o, axis=-1, keepdims=True) + LN_EPS)
                    hs = slice(G_DV * (2 * p + hh), G_DV * (2 * p + hh + 1))
                    out_ref[bi, rows, hs] = (hn * gain_ref[:, hs] * gate[:, hs]).astype(BF16)
        return carry

    lax.fori_loop(0, NC, chunk, 0)


def _gla(oa, la, w3, mk, gain, *, B, S, nb, ts):
    N = oa.shape[0]
    oa3 = oa.reshape(B, S, oa.shape[1])
    la3 = la.reshape(B, S, G_KW)
    kern = functools.partial(_gla_kernel, NC=ts // G_CHUNK, nb=nb)
    out = pl.pallas_call(
        kern,
        grid=(B // nb, S // ts),
        in_specs=[pl.BlockSpec((nb, ts, 2 * G_KW), lambda b, t: (b, t, C_GQK // (2 * G_KW))),
                  pl.BlockSpec((nb, ts, G_W), lambda b, t: (b, t, C_GV // G_W)),
                  pl.BlockSpec((nb, ts, G_W), lambda b, t: (b, t, C_GG // G_W)),
                  pl.BlockSpec((nb, ts, G_KW), lambda b, t: (b, t, 0)),
                  pl.BlockSpec(w3.shape, lambda b, t: (0, 0)),
                  pl.BlockSpec(mk.shape, lambda b, t: (0, 0, 0)),
                  pl.BlockSpec((1, G_W), lambda b, t: (0, 0))],
        out_specs=pl.BlockSpec((nb, ts, G_W), lambda b, t: (b, t, 0)),
        out_shape=jax.ShapeDtypeStruct((B, S, G_W), BF16),
        scratch_shapes=[pltpu.VMEM((nb, 2, 2 * G_DV, LANES), F32)],
        compiler_params=_cparams(2),
        name="gla",
    )(oa3, oa3, oa3, la3, w3, mk, gain)
    return out.reshape(N, G_W)


def _layer_norm(z, g, b):
    mu = jnp.mean(z, axis=-1, keepdims=True)
    zc = z - mu
    var = jnp.mean(zc * zc, axis=-1, keepdims=True)
    return zc * lax.rsqrt(var + LN_EPS) * g + b


def _outproj_kernel(hm_ref, hg_ref, wf_ref, x_ref, mod_ref, g_ref, b_ref, wrg_ref, wre_ref, br_ref,
                    x1_ref, u2_ref, rrow_ref, w_ref, wr_ref, *, tb, nh):
    @pl.when(pl.program_id(0) == 0)
    def _():
        w_ref[...] = wf_ref[...].astype(BF16)
        z = lambda n: jnp.zeros((n, wrg_ref.shape[2]), F32)
        wt = jnp.concatenate([wrg_ref[0], z(SUBLANES - N_GROUPS), wre_ref[0],
                              z(LANES - SUBLANES - N_EXP)], axis=0).T
        hi = wt.astype(BF16)
        wr_ref[:, 0:LANES] = hi
        wr_ref[:, LANES:2 * LANES] = (wt - hi.astype(F32)).astype(BF16)

    mod = mod_ref[0]
    blocks = [slice(tb * j, tb * (j + 1)) for j in range(nh)]
    y = [jnp.dot(hm_ref[r, :], w_ref[0:M_W, :], preferred_element_type=F32)
         + jnp.dot(hg_ref[r, :], w_ref[M_W:M_W + G_W, :], preferred_element_type=F32) for r in blocks]
    u2 = []
    for j, r in enumerate(blocks):
        z = ALPHA * x_ref[r, :] + (1.0 + mod[2:3, :]) * y[j]
        x1 = _layer_norm(z, g_ref[...], b_ref[...])
        x1_ref[r, :] = x1
        u2.append(x1 * (1.0 + mod[4:5, :]) + mod[3:4, :])
        u2_ref[r, :] = u2[j].astype(BF16)

    u2h = [u.astype(BF16) for u in u2]
    u2l = [(u2[j] - u2h[j].astype(F32)).astype(BF16) for j in range(nh)]
    lh = [jnp.dot(u, wr_ref[...], preferred_element_type=F32) for u in u2h]
    ll = [jnp.dot(u, wr_ref[:, 0:LANES], preferred_element_type=F32) for u in u2l]
    for j in range(nh):
        logits = lh[j][:, 0:LANES] + lh[j][:, LANES:2 * LANES] + ll[j] + br_ref[...]
        rrow_ref[j] = _route_select(logits.T, tb)


def _route_select(lt, tm):
    row = lax.broadcasted_iota(jnp.int32, (SUBLANES, tm), 0)
    gl = jnp.where(row < N_GROUPS, lt[0:SUBLANES, :], -jnp.inf)
    gmax = jnp.max(gl, axis=0, keepdims=True)
    gsel = jnp.min(jnp.where(gl == gmax, row, SUBLANES), axis=0, keepdims=True)
    pg = 1.0 / jnp.sum(jnp.exp(gl - gmax), axis=0, keepdims=True)
    ein = jnp.zeros((SUBLANES, tm), F32)
    for g in range(N_GROUPS):
        ein = jnp.where(gsel == g, lt[SUBLANES * (g + 1):SUBLANES * (g + 2), :], ein)
    v1 = jnp.max(ein, axis=0, keepdims=True)
    i1 = jnp.min(jnp.where(ein == v1, row, SUBLANES), axis=0, keepdims=True)
    rest = jnp.where(row == i1, -jnp.inf, ein)
    v2 = jnp.max(rest, axis=0, keepdims=True)
    i2 = jnp.min(jnp.where(rest == v2, row, SUBLANES), axis=0, keepdims=True)
    t2 = jnp.exp(v2 - v1)
    p1 = 1.0 / (1.0 + t2)
    e0 = (gsel * E_PER_G + i1).astype(F32)
    e1 = (gsel * E_PER_G + i2).astype(F32)
    return jnp.concatenate([e0, e1, pg * p1, pg * (t2 * p1), jnp.zeros((SUBLANES - 4, tm), F32)], axis=0)


def _outproj(hm, hg, w_out, x2, mod3, g, b, wrg_t, wre_t, br, *, S, tb, nh, layer):
    N, D = x2.shape
    tm = tb * nh
    tpb = S // tm
    kern = functools.partial(_outproj_kernel, tb=tb, nh=nh)
    return pl.pallas_call(
        kern,
        grid=(N // tm,),
        in_specs=[pl.BlockSpec((tm, M_W), lambda i: (i, 0)),
                  pl.BlockSpec((tm, G_W), lambda i: (i, 0)),
                  pl.BlockSpec((M_W + G_W, D), lambda i: (0, 0), pipeline_mode=pl.Buffered(1)),
                  pl.BlockSpec((tm, D), lambda i: (i, 0)),
                  pl.BlockSpec((1, 6, D), lambda i: (i // tpb, 0, 0)),
                  pl.BlockSpec((1, D), lambda i: (0, 0)),
                  pl.BlockSpec((1, D), lambda i: (0, 0)),
                  pl.BlockSpec((1, N_GROUPS, D), lambda i: (layer, 0, 0)),
                  pl.BlockSpec((1, N_EXP, D), lambda i: (layer, 0, 0)),
                  pl.BlockSpec((1, LANES), lambda i: (0, 0))],
        out_specs=[pl.BlockSpec((tm, D), lambda i: (i, 0)),
                   pl.BlockSpec((tm, D), lambda i: (i, 0)),
                   pl.BlockSpec((nh, SUBLANES, tb), lambda i: (i, 0, 0))],
        out_shape=[jax.ShapeDtypeStruct((N, D), F32),
                   jax.ShapeDtypeStruct((N, D), BF16),
                   jax.ShapeDtypeStruct((N // tb, SUBLANES, tb), F32)],
        scratch_shapes=[pltpu.VMEM((M_W + G_W, D), BF16), pltpu.VMEM((D, 2 * LANES), BF16)],
        compiler_params=_cparams(),
        name="outproj",
    )(hm, hg, w_out, x2, mod3, g, b, wrg_t, wre_t, br)


def _slots_per_tile(tb):
    worst = 2 * tb + N_EXP * (GRAN - 1)
    return -(-worst // LANES) * LANES


def _ffn_tiles(n_tok, tb):
    worst_rows = 2 * n_tok + (n_tok // tb) * N_EXP * (GRAN - 1)
    return -(-worst_rows // FFN_TM) + N_EXP


def _route_kernel(rr_ref, u_ref, lt_ref, srow_ref, col_ref, gd_ref, meta_ref, mg_ref, part_ref,
                  *, NT, tb, TM):
    iota_e = lax.broadcasted_iota(jnp.int32, (N_EXP, tb), 0).astype(F32)
    glane = lax.broadcasted_iota(jnp.int32, (N_EXP, LANES), 1).astype(F32)
    ltri = lt_ref[...]

    def prefix_e(col):
        return jnp.dot(ltri, jnp.broadcast_to(col, (N_EXP, LANES)),
                       preferred_element_type=F32, precision=HIGHEST)[:, 0:1]

    def p1(j, run8):
        r = rr_ref[j]
        oh0 = jnp.where(iota_e == r[0:1, :], 1.0, 0.0)
        oh1 = jnp.where(iota_e == r[1:2, :], 1.0, 0.0)
        cum0 = jnp.dot(oh0.astype(BF16), u_ref[...], preferred_element_type=F32)
        cum1 = jnp.dot(oh1.astype(BF16), u_ref[...], preferred_element_type=F32)
        c0 = jnp.sum(oh0, axis=1, keepdims=True)
        n8 = jnp.floor((c0 + jnp.sum(oh1, axis=1, keepdims=True) + (GRAN - 1.0)) * (1.0 / GRAN))
        lo8 = prefix_e(n8)
        s0 = jnp.sum(oh0 * (GRAN * lo8 + cum0 - 1.0), axis=0, keepdims=True)
        s1 = jnp.sum(oh1 * (GRAN * lo8 + c0 + cum1 - 1.0), axis=0, keepdims=True)
        info = jnp.concatenate([s0, s1, r[2:4, :], jnp.zeros((SUBLANES - 4, tb), F32)], axis=0)
        srow_ref[j] = info
        col_ref[pl.ds(pl.multiple_of(j * tb, tb), tb), :] = jnp.concatenate(
            [info, jnp.zeros((LANES - SUBLANES, tb), F32)], axis=0).T
        mg = jnp.where((lo8 <= glane) & (glane < lo8 + n8), 1.0, 0.0)
        mg_ref[j] = mg
        part = jnp.sum(mg * (run8 + glane - lo8), axis=0, keepdims=True)
        gcnt = jnp.broadcast_to(jnp.sum(n8, axis=0, keepdims=True), (1, LANES))
        part_ref[j] = jnp.concatenate([part, gcnt, jnp.zeros((SUBLANES - 2, LANES), F32)], axis=0)
        return run8 + n8

    tot8 = lax.fori_loop(0, NT, p1, jnp.zeros((N_EXP, 1), F32), unroll=8 if NT % 8 == 0 else 1)
    seg_t = jnp.floor((tot8 * GRAN + (TM - 1.0)) * (1.0 / TM))
    base_t = prefix_e(seg_t)
    base8 = base_t * (TM // GRAN)
    lane1 = lax.broadcasted_iota(jnp.int32, (1, LANES), 1)

    def p2(j, carry):
        pr = part_ref[j]
        dst = (pr[0:1, :] + jnp.sum(mg_ref[j] * base8, axis=0, keepdims=True)) * GRAN
        gd_ref[j] = jnp.where(lane1 == G_LAST, pr[1:2, :], dst).astype(jnp.int32)
        return carry

    lax.fori_loop(0, NT, p2, 0, unroll=8 if NT % 8 == 0 else 1)
    eye = jnp.where(glane == lax.broadcasted_iota(jnp.int32, (N_EXP, LANES), 0).astype(F32), 1.0, 0.0)
    tail_row = jnp.sum(eye * ((base8 + tot8) * GRAN), axis=0, keepdims=True)
    tail_n8 = jnp.sum(eye * (seg_t * (TM // GRAN) - tot8), axis=0, keepdims=True)
    nv_l = jnp.broadcast_to(jnp.sum(seg_t, axis=0, keepdims=True), (1, LANES))
    gd_ref[NT] = jnp.where(lane1 == G_LAST, nv_l, tail_row).astype(jnp.int32)
    gd_ref[NT + 1] = tail_n8.astype(jnp.int32)
    ti = lax.broadcasted_iota(jnp.int32, (N_EXP, tb), 1).astype(F32)
    te = jnp.sum(jnp.where(base_t <= ti, 1.0, 0.0), axis=0, keepdims=True) - 1.0
    nv = jnp.broadcast_to(jnp.sum(seg_t, axis=0, keepdims=True), (1, tb))
    own = jnp.where((base_t <= ti) & (ti < base_t + seg_t), 1.0, 0.0)
    vr = jnp.sum(own * jnp.clip(tot8 * GRAN - (ti - base_t) * TM, 0.0, TM), axis=0, keepdims=True)
    meta_ref[...] = jnp.concatenate([te, nv, vr, jnp.zeros((SUBLANES - 3, tb), F32)],
                                    axis=0).astype(jnp.int32)


def _route(rrow, u_cnt, ltri, *, TM):
    NT, _, tb = rrow.shape
    kern = functools.partial(_route_kernel, NT=NT, tb=tb, TM=TM)
    full3 = lambda i: (0, 0, 0)
    return pl.pallas_call(
        kern,
        grid=(1,),
        in_specs=[pl.BlockSpec((NT, SUBLANES, tb), full3),
                  pl.BlockSpec((tb, tb), lambda i: (0, 0)),
                  pl.BlockSpec((N_EXP, N_EXP), lambda i: (0, 0))],
        out_specs=[pl.BlockSpec((NT, SUBLANES, tb), full3),
                   pl.BlockSpec((NT * tb, LANES), lambda i: (0, 0)),
                   pl.BlockSpec((NT + 2, 1, LANES), full3),
                   pl.BlockSpec((SUBLANES, tb), lambda i: (0, 0))],
        out_shape=[jax.ShapeDtypeStruct((NT, SUBLANES, tb), F32),
                   jax.ShapeDtypeStruct((NT * tb, LANES), F32),
                   jax.ShapeDtypeStruct((NT + 2, 1, LANES), jnp.int32),
                   jax.ShapeDtypeStruct((SUBLANES, tb), jnp.int32)],
        scratch_shapes=[pltpu.VMEM((NT, N_EXP, LANES), F32), pltpu.VMEM((NT, SUBLANES, LANES), F32)],
        compiler_params=_cparams(),
        name="route",
    )(rrow, u_cnt, ltri)


_HI_MASK = 0xFFFF0000


def _pack_halves(x):
    c = x.shape[1] // 2
    lo = lax.bitcast_convert_type(x[:, :c], U32)
    hi = lax.bitcast_convert_type(x[:, c:], U32)
    return (lo >> 16) | (hi & U32(_HI_MASK))


def _unpack_halves(w):
    lo = lax.bitcast_convert_type(w << 16, F32)
    hi = lax.bitcast_convert_type(w & U32(_HI_MASK), F32)
    return jnp.concatenate([lo, hi], axis=1).astype(BF16)


def _granule_copy(src_ref, src_row, dst_ref, dst_row, sem):
    return pltpu.make_async_copy(src_ref.at[pl.ds(src_row, GRAN), :], dst_ref.at[pl.ds(dst_row, GRAN), :], sem)


def _for_granules(n, body, unroll=4):
    def blk(i, carry):
        for t in range(unroll):
            body(i * unroll + t)
        return carry

    def one(g, carry):
        body(g)
        return carry

    nblk = n // unroll
    lax.fori_loop(0, nblk, blk, 0)
    lax.fori_loop(nblk * unroll, n, one, 0)


def _wait_granules(n, src_ref, dst_ref, sem, n_max):
    b = 1
    while b <= n_max:
        @pl.when((n & b) != 0)
        def _(b=b):
            pltpu.make_async_copy(src_ref.at[pl.ds(0, b * GRAN), :], dst_ref.at[pl.ds(0, b * GRAN), :],
                                  sem).wait()
        b *= 2


def _dispatch_kernel(gd_ref, srow_ref, u_ref, xs_ref, buf, zbuf, sems, *, NT, SL, TM, n_tiles):
    j = pl.program_id(0)
    slot = j % 2
    zsem = sems.at[2]

    def drain(tile, sl):
        _wait_granules(gd_ref[tile, G_LAST], buf.at[sl], xs_ref, sems.at[sl], SL // GRAN)

    def tile_fill(t):
        return pltpu.make_async_copy(zbuf, xs_ref.at[pl.ds(pl.multiple_of(t * TM, TM), TM), :], zsem)

    def zero_fill(wait):
        for e in range(N_EXP):
            n, row0 = gd_ref[NT + 1, e], gd_ref[NT, e]
            b = TM // GRAN // 2
            while b >= 1:
                @pl.when((n & b) != 0)
                def _(b=b, n=n, row0=row0):
                    start = pl.multiple_of(row0 + ((n >> b.bit_length()) << b.bit_length()) * GRAN, GRAN)
                    cp = pltpu.make_async_copy(zbuf.at[pl.ds(0, b * GRAN), :],
                                               xs_ref.at[pl.ds(start, b * GRAN), :], zsem)
                    cp.wait() if wait else cp.start()
                b //= 2

        def zt(t, carry):
            tile_fill(t).wait() if wait else tile_fill(t).start()
            return carry
        lax.fori_loop(gd_ref[NT, G_LAST], n_tiles, zt, 0)

    @pl.when(j == 0)
    def _():
        zbuf[...] = jnp.zeros_like(zbuf)
        zero_fill(False)

    @pl.when(j >= 2)
    def _():
        drain(j - 2, slot)

    s = srow_ref[0]
    rows = lax.broadcasted_iota(jnp.int32, (SL, s.shape[1]), 0).astype(F32)
    m0 = rows == s[0:1, :]
    m1 = rows == s[1:2, :]
    oh = jnp.where(m0 | m1, 1.0, 0.0).astype(BF16)
    dw = u_ref.shape[1] // 2
    buf[slot, :, 0:dw] = _pack_halves(jnp.dot(oh, u_ref[...], preferred_element_type=F32))
    wrow = jnp.sum(jnp.where(m0, s[2:3, :], 0.0) + jnp.where(m1, s[3:4, :], 0.0), axis=1, keepdims=True)
    buf[slot, :, dw:dw + LANES] = lax.bitcast_convert_type(jnp.broadcast_to(wrow, (SL, LANES)), U32)

    def issue(g):
        _granule_copy(buf.at[slot], pl.multiple_of(g * GRAN, GRAN), xs_ref,
                      pl.multiple_of(gd_ref[j, g], GRAN), sems.at[slot]).start()

    _for_granules(gd_ref[j, G_LAST], issue)

    @pl.when(j == NT - 1)
    def _():
        drain(j, slot)
        if NT > 1:
            drain(j - 1, 1 - slot)
        zero_fill(True)


def _dispatch(gd, srow, u2, *, n_tiles, TM):
    N, D = u2.shape
    NT, _, tb = srow.shape
    SL = _slots_per_tile(tb)
    n_rows = n_tiles * TM
    kern = functools.partial(_dispatch_kernel, NT=NT, SL=SL, TM=TM, n_tiles=n_tiles)
    grid_spec = pltpu.PrefetchScalarGridSpec(
        num_scalar_prefetch=1,
        grid=(NT,),
        in_specs=[pl.BlockSpec((1, SUBLANES, tb), lambda j, gd: (j, 0, 0)),
                  pl.BlockSpec((tb, D), lambda j, gd: (j, 0))],
        out_specs=pl.BlockSpec(memory_space=pl.ANY),
        scratch_shapes=[pltpu.VMEM((2, SL, D // 2 + LANES), U32), pltpu.VMEM((TM, D // 2 + LANES), U32),
                        pltpu.SemaphoreType.DMA((3,))],
    )
    return pl.pallas_call(
        kern,
        grid_spec=grid_spec,
        out_shape=jax.ShapeDtypeStruct((n_rows, D // 2 + LANES), U32),
        compiler_params=_cparams(),
        name="dispatch",
    )(gd, srow, u2)


def _ffn_kernel(te_ref, nv_ref, vr_ref, xs_ref, wg_ref, wu_ref, wd_ref, o_ref, wgb, wub, wdb, sg, su, sd, slot_ref,
                sems):
    i = pl.program_id(0)
    nv = nv_ref[0]
    e = te_ref[i]

    def weight_copies(ex, sl):
        return (pltpu.make_async_copy(wg_ref.at[ex], sg.at[sl], sems.at[sl]),
                pltpu.make_async_copy(wu_ref.at[ex], su.at[sl], sems.at[sl]),
                pltpu.make_async_copy(wd_ref.at[ex], sd.at[sl], sems.at[sl]))

    @pl.when(i == 0)
    def _():
        slot_ref[0] = 0
        for cp in weight_copies(e, 0):
            cp.start()

    @pl.when((i < nv) & ((i == 0) | (e != te_ref[jnp.maximum(i - 1, 0)])))
    def _():
        sl = slot_ref[0]
        for cp in weight_copies(e, sl):
            cp.wait()
        wgb[...] = sg[sl].astype(BF16)
        wub[...] = su[sl].astype(BF16)
        wdb[...] = sd[sl].astype(BF16)
        nxt = lax.while_loop(lambda t: (t < nv) & (te_ref[jnp.minimum(t, nv - 1)] == e), lambda t: t + 1, i + 1)

        @pl.when(nxt < nv)
        def _():
            for cp in weight_copies(te_ref[nxt], 1 - sl):
                cp.start()
        slot_ref[0] = 1 - sl

    hm = xs_ref.shape[0] // FFN_SUB
    dw = o_ref.shape[1]

    def swiglu_rows(nsub):
        halves = tuple(slice(hm * j, hm * (j + 1)) for j in range(nsub))
        x = [_unpack_halves(xs_ref[r, 0:dw]) for r in halves]
        g = [jnp.dot(x[j], wgb[...], preferred_element_type=F32) for j in range(nsub)]
        u = [jnp.dot(x[j], wub[...], preferred_element_type=F32) for j in range(nsub)]
        h = [(g[j] * _sigmoid(g[j]) * u[j]).astype(BF16) for j in range(nsub)]
        y = [jnp.dot(h[j], wdb[...], preferred_element_type=F32) for j in range(nsub)]
        for j in range(nsub):
            wt = lax.bitcast_convert_type(xs_ref[halves[j], dw:dw + LANES], F32)
            yw = y[j] * jnp.concatenate([wt] * (2 * dw // LANES), axis=1)
            o_ref[halves[j], :] = _pack_halves(yw.astype(BF16).astype(F32))
        if nsub < FFN_SUB:
            o_ref[hm * nsub:, :] = jnp.zeros((hm * (FFN_SUB - nsub), dw), U32)

    used = vr_ref[i]
    for nsub in range(1, FFN_SUB + 1):
        lo, hi = hm * (nsub - 1), hm * nsub
        pl.when((i < nv) & (used > lo) & ((used <= hi) if nsub < FFN_SUB else True))(
            functools.partial(swiglu_rows, nsub))

    @pl.when(i >= nv_ref[0])
    def _():
        o_ref[...] = jnp.zeros_like(o_ref)


def _ffn(te, nv, vr, xs, wg, wu, wd, *, TM):
    P, XW = xs.shape
    DW = XW - LANES
    D = 2 * DW
    n_tiles = P // TM
    grid_spec = pltpu.PrefetchScalarGridSpec(
        num_scalar_prefetch=3,
        grid=(n_tiles,),
        in_specs=[pl.BlockSpec((TM, XW), lambda i, te, nv, vr: (jnp.maximum(jnp.minimum(i, nv[0] - 1), 0), 0)),
                  pl.BlockSpec(memory_space=pl.ANY),
                  pl.BlockSpec(memory_space=pl.ANY),
                  pl.BlockSpec(memory_space=pl.ANY)],
        out_specs=pl.BlockSpec((TM, DW), lambda i, te, nv, vr: (i, 0)),
        scratch_shapes=[pltpu.VMEM((D, D_EXP), BF16), pltpu.VMEM((D, D_EXP), BF16),
                        pltpu.VMEM((D_EXP, D), BF16),
                        pltpu.VMEM((2, D, D_EXP), F32), pltpu.VMEM((2, D, D_EXP), F32),
                        pltpu.VMEM((2, D_EXP, D), F32), pltpu.SMEM((1,), jnp.int32),
                        pltpu.SemaphoreType.DMA((2,))],
    )
    return pl.pallas_call(
        _ffn_kernel,
        grid_spec=grid_spec,
        out_shape=jax.ShapeDtypeStruct((P, DW), U32),
        compiler_params=_cparams(),
        name="ffn",
    )(te, nv, vr, xs, wg, wu, wd)


def _combine_kernel(gd_ref, ys_ref, col_ref, x1_ref, mod_ref, g_ref, b_ref, o_ref, buf, sems, *, NT, SL):
    j = pl.program_id(0)
    slot = j % 2

    def fetch(tile, sl):
        def f(g):
            _granule_copy(ys_ref, pl.multiple_of(gd_ref[tile, g], GRAN), buf.at[sl],
                          pl.multiple_of(g * GRAN, GRAN), sems.at[sl]).start()
        _for_granules(gd_ref[tile, G_LAST], f)

    @pl.when(j == 0)
    def _():
        fetch(0, 0)

    @pl.when(j + 1 < NT)
    def _():
        fetch(j + 1, 1 - slot)

    ng = gd_ref[j, G_LAST]

    _wait_granules(ng, ys_ref, buf.at[slot], sems.at[slot], SL // GRAN)

    rows = lax.broadcasted_iota(jnp.int32, (SL, 1), 0)
    yb = _unpack_halves(jnp.where(rows < ng * GRAN, buf[slot], U32(0)))
    col = col_ref[...]
    tb = col.shape[0]
    lanes = lax.broadcasted_iota(jnp.int32, (tb, SL), 1).astype(F32)
    sel = jnp.where((lanes == col[:, 0:1]) | (lanes == col[:, 1:2]), 1.0, 0.0).astype(BF16)
    y = jnp.dot(sel, yb, preferred_element_type=F32)
    mod = mod_ref[0]
    z = ALPHA * x1_ref[...] + (1.0 + mod[5:6, :]) * y
    o_ref[...] = _layer_norm(z, g_ref[...], b_ref[...])


def _combine(gd, ys, col, x1, mod3, g, b, *, S, tb):
    N, D = x1.shape
    NT = N // tb
    tpb = S // tb
    SL = _slots_per_tile(tb)
    kern = functools.partial(_combine_kernel, NT=NT, SL=SL)
    grid_spec = pltpu.PrefetchScalarGridSpec(
        num_scalar_prefetch=1,
        grid=(NT,),
        in_specs=[pl.BlockSpec(memory_space=pl.ANY),
                  pl.BlockSpec((tb, LANES), lambda j, gd: (j, 0)),
                  pl.BlockSpec((tb, D), lambda j, gd: (j, 0)),
                  pl.BlockSpec((1, 6, D), lambda j, gd: (j // tpb, 0, 0)),
                  pl.BlockSpec((1, D), lambda j, gd: (0, 0)),
                  pl.BlockSpec((1, D), lambda j, gd: (0, 0))],
        out_specs=pl.BlockSpec((tb, D), lambda j, gd: (j, 0)),
        scratch_shapes=[pltpu.VMEM((2, SL, D // 2), U32), pltpu.SemaphoreType.DMA((2,))],
    )
    return pl.pallas_call(
        kern,
        grid_spec=grid_spec,
        out_shape=jax.ShapeDtypeStruct((N, D), F32),
        compiler_params=_cparams(),
        name="combine",
    )(gd, ys, col, x1, mod3, g, b)


def _layer(x, c, l, w_ada, b_ada, w_in, w_conv, b_conv, b_igate, b_fgate, mlstm_norm_g, w_gla_a, b_gla_a,
           gla_norm_g, w_out, ln1_g, ln1_b, w_route_group, b_route_group, w_route_expert, b_route_expert,
           w_gate, w_up, w_down, ln2_g, ln2_b):
    B, S, D = x.shape
    N = B * S
    x2 = x.reshape(N, D)
    tm_in = min(512, S)
    tm = min(256, S)
    lm = min(256, S)

    mod3 = _ada(c, w_ada[l], b_ada[l]).reshape(B, 6, D)

    wa_pad = jnp.zeros((LANES, G_KW), F32).at[SM_A:SM_A + G_RANK].set(w_gla_a[l]).astype(BF16)
    bg = (jnp.zeros((2 * SUBLANES, 1), F32).at[0:M_HEADS, 0].set(b_igate[l])
          .at[SUBLANES:SUBLANES + M_HEADS, 0].set(b_fgate[l]))
    oa, la, g3 = _inproj(x2, mod3, jnp.swapaxes(w_in, 1, 2), w_conv[l], b_conv[l].reshape(1, -1), wa_pad,
                         b_gla_a[l].reshape(1, -1), bg, S=S, tm=tm_in, lm=lm, layer=l)

    u_tri = jnp.asarray(np.triu(np.ones((lm, lm), np.float32)))
    nb = 4 if B % 4 == 0 else (2 if B % 2 == 0 else 1)
    ts = min(512, S)
    hm = _mlstm(oa, g3, u_tri, mlstm_norm_g[l].reshape(1, -1), B=B, S=S, L=lm, nb=nb, ts=ts)
    w3_np, mk_np = _gla_consts()
    hg = _gla(oa, la, jnp.asarray(w3_np, BF16), jnp.asarray(mk_np), gla_norm_g[l].reshape(1, -1), B=B, S=S,
              nb=nb, ts=ts)

    br = (jnp.zeros((1, LANES), F32).at[0, 0:N_GROUPS].set(b_route_group[l])
          .at[0, SUBLANES:SUBLANES + N_EXP].set(b_route_expert[l]))
    x1, u2, rrow = _outproj(hm, hg, w_out[l], x2, mod3, ln1_g[l].reshape(1, -1), ln1_b[l].reshape(1, -1),
                            jnp.swapaxes(w_route_group, 1, 2), jnp.swapaxes(w_route_expert, 1, 2), br,
                            S=S, tb=tm, nh=4 if S % (4 * tm) == 0 else 1, layer=l)

    u_cnt = jnp.asarray(np.triu(np.ones((tm, tm), np.float32)), BF16)
    ltri = jnp.asarray(np.tril(np.ones((N_EXP, N_EXP), np.float32), -1))
    srow, col, gd3, meta = _route(rrow, u_cnt, ltri, TM=FFN_TM)
    gd = gd3.reshape(N // tm + 2, LANES)
    n_tiles = _ffn_tiles(N, tm)
    te, nv, vr = meta[0, :n_tiles], meta[1, 0:1], meta[2, :n_tiles]

    xs = _dispatch(gd, srow, u2, n_tiles=n_tiles, TM=FFN_TM)
    ys = _ffn(te, nv, vr, xs, w_gate[l], w_up[l], w_down[l], TM=FFN_TM)
    out = _combine(gd, ys, col, x1, mod3, ln2_g[l].reshape(1, -1), ln2_b[l].reshape(1, -1), S=S, tb=tm)
    return out.reshape(B, S, D)


def kernel(x, c, w_ada, b_ada, w_in, w_conv, b_conv, b_igate, b_fgate, mlstm_norm_g, w_gla_a, b_gla_a,
           gla_norm_g, w_out, ln1_g, ln1_b, w_route_group, b_route_group, w_route_expert, b_route_expert,
           w_gate, w_up, w_down, ln2_g, ln2_b):
    for l in range(DEPTH):
        x = _layer(x, c, l, w_ada, b_ada, w_in, w_conv, b_conv, b_igate, b_fgate, mlstm_norm_g, w_gla_a,
                   b_gla_a, gla_norm_g, w_out, ln1_g, ln1_b, w_route_group, b_route_group, w_route_expert,
                   b_route_expert, w_gate, w_up, w_down, ln2_g, ln2_b)
    return x
```

```python
import functools

import numpy as np
import jax
import jax.numpy as jnp
from jax import lax
from jax.experimental import pallas as pl
from jax.experimental.pallas import tpu as pltpu

F32 = jnp.float32
BF16 = jnp.bfloat16
U32 = jnp.uint32
HIGHEST = lax.Precision.HIGHEST

DEPTH = 1
M_HEADS = 4
M_HD = 128
M_W = M_HEADS * M_HD
CONV_W = 4
G_HEADS = 4
G_DK = 64
G_DV = 128
G_W = G_HEADS * G_DV
G_KW = G_HEADS * G_DK
G_RANK = 16
G_TAU = 16.0
G_CHUNK = 64
N_GROUPS = 4
E_PER_G = 8
N_EXP = N_GROUPS * E_PER_G
D_EXP = 512
ALPHA = (2 * DEPTH) ** 0.25
LN_EPS = 1e-5

LANES = 128
SUBLANES = 8
VMEM_LIMIT = 48 * 1024 * 1024

C_QK = 0
C_VO = 1024
C_GQK = 2048
C_GV = 2560
C_GG = 3072
C_SMALL = 3584
C_TOT = 3712
SM_I, SM_F, SM_A = 0, 8, 16
IN_GATES = 4 * M_W
IN_G = IN_GATES + 2 * M_HEADS
IN_GA = IN_G + 2 * G_KW + 2 * G_W
IN_TOT = IN_GA + G_RANK

FFN_TM = 512
FFN_SUB = 2
GRAN = SUBLANES
G_LAST = LANES - 1


def _cparams(n_axes=1):
    return pltpu.CompilerParams(dimension_semantics=("arbitrary",) * n_axes,
                                vmem_limit_bytes=VMEM_LIMIT)


def _sigmoid(x):
    return 1.0 / (1.0 + jnp.exp(-x))


def _log_sigmoid(x):
    return jnp.minimum(x, 0.0) - jnp.log(1.0 + jnp.exp(-jnp.abs(x)))


def _ada_kernel(c_ref, w_ref, b_ref, o_ref):
    c = c_ref[...]
    ca = (c * _sigmoid(c)).astype(BF16)
    o_ref[...] = jnp.dot(ca, w_ref[...].astype(BF16), preferred_element_type=F32) + b_ref[...]


def _ada(c, w, b):
    B, D = c.shape
    n_out = w.shape[1]
    tn = 1024
    return pl.pallas_call(
        _ada_kernel,
        grid=(n_out // tn,),
        in_specs=[pl.BlockSpec((B, D), lambda j: (0, 0)),
                  pl.BlockSpec((D, tn), lambda j: (0, j)),
                  pl.BlockSpec((1, tn), lambda j: (0, j))],
        out_specs=pl.BlockSpec((B, tn), lambda j: (0, j)),
        out_shape=jax.ShapeDtypeStruct((B, n_out), F32),
        compiler_params=_cparams(),
        name="ada",
    )(c, w, b.reshape(1, n_out))


def _inproj_kernel(x_ref, mod_ref, win_ref, wc_ref, bc_ref, wa_ref, ba_ref, bg_ref,
                   oa_ref, la_ref, g_ref, halo_ref, w_ref, *, tm, tpb, lm):
    i = pl.program_id(0)

    @pl.when(i == 0)
    def _():
        rc = 2 * LANES
        for r in range(0, IN_GATES, rc):
            w_ref[:, r:r + rc] = win_ref[0, r:r + rc, :].T.astype(BF16)
        for r in range(0, C_SMALL - C_GQK, rc):
            w_ref[:, C_GQK + r:C_GQK + r + rc] = win_ref[0, IN_G + r:IN_G + r + rc, :].T.astype(BF16)
        gates = win_ref[0, IN_GATES:IN_G, :]
        z = lambda n: jnp.zeros((n, gates.shape[1]), F32)
        small = jnp.concatenate([gates[0:M_HEADS], z(SM_F - M_HEADS), gates[M_HEADS:2 * M_HEADS],
                                 z(SM_A - SM_F - M_HEADS), win_ref[0, IN_GA:IN_TOT, :],
                                 z(LANES - SM_A - G_RANK)], axis=0)
        w_ref[:, C_SMALL:C_TOT] = small.T.astype(BF16)

    @pl.when(i % tpb == 0)
    def _():
        halo_ref[0:SUBLANES, :] = jnp.zeros((SUBLANES, halo_ref.shape[1]), F32)

    mod = mod_ref[0]
    u = (x_ref[...] * (1.0 + mod[1:2, :]) + mod[0:1, :]).astype(BF16)

    def proj(c0, c1):
        return jnp.dot(u, w_ref[:, c0:c1], preferred_element_type=F32)

    p = proj(C_QK, C_QK + 2 * M_W)
    halo_ref[SUBLANES:SUBLANES + tm, :] = p
    acc = bc_ref[...] + wc_ref[CONV_W - 1:CONV_W, :] * p
    for j in range(CONV_W - 1):
        acc = acc + wc_ref[j:j + 1, :] * halo_ref[pl.ds(SUBLANES - (CONV_W - 1) + j, tm), :]
    halo_ref[0:SUBLANES, :] = p[tm - SUBLANES:, :]
    qk = acc * _sigmoid(acc)
    oa_ref[:, C_QK:C_QK + M_W] = qk[:, :M_W].astype(BF16)
    oa_ref[:, C_QK + M_W:C_QK + 2 * M_W] = (qk[:, M_W:] * (M_HD ** -0.5)).astype(BF16)

    p = proj(C_VO, C_VO + 2 * M_W)
    oa_ref[:, C_VO:C_VO + 2 * M_W] = p.astype(BF16)

    p = proj(C_GQK, C_GQK + G_KW)
    oa_ref[:, C_GQK:C_GQK + G_KW] = (p * (G_DK ** -0.5)).astype(BF16)
    p = proj(C_GQK + G_KW, C_SMALL)
    oa_ref[:, C_GQK + G_KW:C_SMALL] = p.astype(BF16)

    ps = proj(C_SMALL, C_TOT)
    la = jnp.dot(ps.astype(BF16), wa_ref[...], preferred_element_type=F32) + ba_ref[...]
    la_ref[...] = _log_sigmoid(la) * (1.0 / G_TAU)
    pt = ps.T
    gi = pt[SM_I:SM_I + SUBLANES, :] + bg_ref[0:SUBLANES, :]
    gf = _log_sigmoid(pt[SM_F:SM_F + SUBLANES, :] + bg_ref[SUBLANES:2 * SUBLANES, :])
    for j in range(tm // lm):
        g_ref[j, 0:SUBLANES, :] = gi[:, j * lm:(j + 1) * lm]
        g_ref[j, SUBLANES:2 * SUBLANES, :] = gf[:, j * lm:(j + 1) * lm]


def _inproj(x2, mod3, w_in, w_conv, b_conv, wa_pad, b_gla, bg, *, S, tm, lm, layer):
    N, D = x2.shape
    tpb = S // tm
    kern = functools.partial(_inproj_kernel, tm=tm, tpb=tpb, lm=lm)
    return pl.pallas_call(
        kern,
        grid=(N // tm,),
        in_specs=[pl.BlockSpec((tm, D), lambda i: (i, 0)),
                  pl.BlockSpec((1, 6, D), lambda i: (i // tpb, 0, 0)),
                  pl.BlockSpec((1, IN_TOT, D), lambda i: (layer, 0, 0), pipeline_mode=pl.Buffered(1)),
                  pl.BlockSpec((CONV_W, 2 * M_W), lambda i: (0, 0)),
                  pl.BlockSpec((1, 2 * M_W), lambda i: (0, 0)),
                  pl.BlockSpec((LANES, G_KW), lambda i: (0, 0)),
                  pl.BlockSpec((1, G_KW), lambda i: (0, 0)),
                  pl.BlockSpec((2 * SUBLANES, 1), lambda i: (0, 0))],
        out_specs=[pl.BlockSpec((tm, C_SMALL), lambda i: (i, 0)),
                   pl.BlockSpec((tm, G_KW), lambda i: (i, 0)),
                   pl.BlockSpec((tm // lm, 2 * SUBLANES, lm), lambda i: (i, 0, 0))],
        out_shape=[jax.ShapeDtypeStruct((N, C_SMALL), BF16),
                   jax.ShapeDtypeStruct((N, G_KW), F32),
                   jax.ShapeDtypeStruct((N // lm, 2 * SUBLANES, lm), F32)],
        scratch_shapes=[pltpu.VMEM((SUBLANES + tm, 2 * M_W), F32), pltpu.VMEM((D, C_TOT), BF16)],
        compiler_params=_cparams(),
        name="inproj",
    )(x2, mod3, w_in, w_conv, b_conv, wa_pad, b_gla, bg)


def _mlstm_sel():
    sel = np.zeros((2 * LANES, 2 * M_HEADS * M_HD), np.float32)
    for j in range(2 * M_HEADS):
        src = (SUBLANES if j < M_HEADS else 3 * SUBLANES) + j % M_HEADS
        sel[src, M_HD * j:M_HD * (j + 1)] = 1.0
        sel[LANES + src, M_HD * j:M_HD * (j + 1)] = 1.0
    return sel


def _mlstm_kernel(qk_ref, vo_ref, g_ref, u_ref, gain_ref, sel_ref, out_ref, c_ref, zt_ref, a_ref, dec_ref, m_ref,
                  *, L, NC, nb):
    @pl.when(pl.program_id(1) == 0)
    def _():
        c_ref[...] = jnp.zeros_like(c_ref)
        m_ref[...] = jnp.zeros_like(m_ref)

    tril = (lax.broadcasted_iota(jnp.int32, (L, L), 0) >= lax.broadcasted_iota(jnp.int32, (L, L), 1))
    ones_v = jnp.ones((L, M_HD), BF16)
    zpad = jnp.zeros((LANES - 4 * SUBLANES, L), F32)

    order = [(bi, c) for bi in range(nb) for c in range(NC)]
    f_all = jnp.concatenate([g_ref[bi, c, SUBLANES:2 * SUBLANES, :] for bi, c in order], axis=0)
    i_all = jnp.concatenate([g_ref[bi, c, 0:SUBLANES, :] for bi, c in order], axis=0)
    b_all = jnp.dot(f_all, u_ref[...], preferred_element_type=F32, precision=HIGHEST)
    a_all = i_all - b_all
    lane_all = lax.broadcasted_iota(jnp.int32, a_all.shape, 1)
    g_all = a_all
    s = 1
    while s < L:
        g_all = jnp.maximum(g_all, jnp.where(lane_all >= s, pltpu.roll(g_all, s, 1), -jnp.inf))
        s *= 2
    for bi in range(nb):
        m_prev = m_ref[bi][:, 0:1]
        for c in range(NC):
            ci = bi * NC + c
            r8 = slice(SUBLANES * ci, SUBLANES * (ci + 1))
            a, b = a_all[r8], b_all[r8]
            a_ref[ci] = a
            M = jnp.maximum(g_all[r8], m_prev)
            ML = M[:, L - 1:L]
            Z = jnp.concatenate([M, jnp.exp(m_prev - M), jnp.exp(-(b + M)), jnp.exp(a - ML), zpad],
                                axis=0)
            zt_ref[ci] = Z.T
            dec_ref[ci] = jnp.broadcast_to(jnp.exp(m_prev - ML), (SUBLANES, 2 * M_HD))
            m_prev = b[:, L - 1:L] + ML
        m_ref[bi] = jnp.broadcast_to(m_prev, (SUBLANES, LANES))

    chains = [(bi, h) for bi in range(nb) for h in range(M_HEADS)]
    nt = (((1,), (1,)), ((), ()))
    tn = (((0,), (0,)), ((), ()))

    def chunk(c, carry):
        rows = pl.ds(pl.multiple_of(c * L, L), L)
        Zt = [zt_ref[bi * NC + c] for bi in range(nb)]
        a = [a_ref[bi * NC + c] for bi in range(nb)]
        dec = [dec_ref[bi * NC + c] for bi in range(nb)]
        hs = [slice(h * M_HD, (h + 1) * M_HD) for h in range(M_HEADS)]
        hs2 = [slice(M_W + h * M_HD, M_W + (h + 1) * M_HD) for h in range(M_HEADS)]
        q = [qk_ref[bi, rows, hs[h]] for bi, h in chains]
        k = [qk_ref[bi, rows, hs2[h]] for bi, h in chains]
        vext = [jnp.concatenate([vo_ref[bi, rows, hs[h]], ones_v], axis=1) for bi, h in chains]
        cst = [c_ref[bi * M_HEADS + h] for bi, h in chains]
        n = range(len(chains))
        sc = [lax.dot_general(q[i], k[i], nt, preferred_element_type=F32) for i in n]
        qc = [jnp.dot(q[i], cst[i].astype(BF16), preferred_element_type=F32) for i in n]
        pm = [(sc[i] * jnp.exp(jnp.where(tril, a[bi][h:h + 1, :] - Zt[bi][:, h:h + 1], -jnp.inf))).astype(BF16)
              for i, (bi, h) in enumerate(chains)]
        pv = [jnp.dot(pm[i], vext[i], preferred_element_type=F32) for i in n]
        rep = []
        for bi in range(nb):
            zh = Zt[bi].astype(BF16)
            zl = (Zt[bi] - zh.astype(F32)).astype(BF16)
            rep.append(jnp.dot(jnp.concatenate([zh, zl], axis=1), sel_ref[...], preferred_element_type=F32))
        e_inter = [rep[bi][:, M_HD * h:M_HD * (h + 1)] for bi, h in chains]
        w_state = [rep[bi][:, M_HD * (M_HEADS + h):M_HD * (M_HEADS + h + 1)] for bi, h in chains]
        kw = [(w_state[i] * k[i].astype(F32)).astype(BF16) for i in n]
        upd = [lax.dot_general(kw[i], vext[i], tn, preferred_element_type=F32) for i in n]
        for i, (bi, h) in enumerate(chains):
            c_ref[bi * M_HEADS + h] = dec[bi][h:h + 1, :] * cst[i] + upd[i]
            nd = pv[i] + jnp.concatenate([e_inter[i], e_inter[i]], axis=1) * qc[i]
            hh = nd[:, :M_HD] / jnp.maximum(jnp.abs(nd[:, M_HD:]),
                                            Zt[bi][:, 2 * SUBLANES + h:2 * SUBLANES + h + 1])
            hh = _sigmoid(vo_ref[bi, rows, hs2[h]].astype(F32)) * hh
            hn = hh * lax.rsqrt(jnp.mean(hh * hh, axis=-1, keepdims=True) + LN_EPS)
            out_ref[bi, rows, hs[h]] = (hn * gain_ref[:, hs[h]]).astype(BF16)
        return carry

    lax.fori_loop(0, NC, chunk, 0)


def _mlstm(oa, g3, u_tri, gain, *, B, S, L, nb, ts):
    N = oa.shape[0]
    NC = ts // L
    oa3 = oa.reshape(B, S, oa.shape[1])
    g4 = g3.reshape(B, S // L, 2 * SUBLANES, L)
    sel = jnp.asarray(_mlstm_sel(), BF16)
    kern = functools.partial(_mlstm_kernel, L=L, NC=NC, nb=nb)
    out = pl.pallas_call(
        kern,
        grid=(B // nb, S // ts),
        in_specs=[pl.BlockSpec((nb, ts, 2 * M_W), lambda b, t: (b, t, C_QK // (2 * M_W))),
                  pl.BlockSpec((nb, ts, 2 * M_W), lambda b, t: (b, t, C_VO // (2 * M_W))),
                  pl.BlockSpec((nb, NC, 2 * SUBLANES, L), lambda b, t: (b, t, 0, 0)),
                  pl.BlockSpec((L, L), lambda b, t: (0, 0)),
                  pl.BlockSpec((1, M_W), lambda b, t: (0, 0)),
                  pl.BlockSpec(sel.shape, lambda b, t: (0, 0))],
        out_specs=pl.BlockSpec((nb, ts, M_W), lambda b, t: (b, t, 0)),
        out_shape=jax.ShapeDtypeStruct((B, S, M_W), BF16),
        scratch_shapes=[pltpu.VMEM((nb * M_HEADS, M_HD, 2 * M_HD), F32),
                        pltpu.VMEM((nb * NC, L, LANES), F32),
                        pltpu.VMEM((nb * NC, SUBLANES, L), F32),
                        pltpu.VMEM((nb * NC, SUBLANES, 2 * M_HD), F32),
                        pltpu.VMEM((nb, SUBLANES, LANES), F32)],
        compiler_params=_cparams(2),
        name="mlstm",
    )(oa3, oa3, g4, u_tri, gain, sel)
    return out.reshape(N, M_W)


_G_LEVELS = 6
_G_XROW = 2 * G_CHUNK + SUBLANES


def _gla_consts():
    L = G_CHUNK
    t = np.arange(L)
    blocks = [(t[None, :] <= t[:, None]).astype(np.float32),
              (t[None, :] > t[:, None]).astype(np.float32),
              np.ones((SUBLANES, L), np.float32)]
    masks = [np.eye(L, dtype=np.float32)]
    m = 1
    while m < L:
        wl = np.zeros((L, L), np.float32)
        for r in range(L):
            r0 = (r // (2 * m)) * 2 * m + m
            if r % (2 * m) >= m:
                wl[r, r0:r + 1] = 1.0
            else:
                wl[r, r + 1:r0] = 1.0
        blocks.append(wl)
        tt, ss = t[:, None], t[None, :]
        masks.append(((tt // (2 * m) == ss // (2 * m)) & (tt % (2 * m) >= m)
                      & (ss % (2 * m) < m)).astype(np.float32))
        m *= 2
    w = np.concatenate(blocks, axis=0)
    w3 = np.concatenate([w, w, w], axis=1)
    mk = np.stack([np.concatenate([x] * G_HEADS, axis=0) for x in masks])
    return w3, mk


def _gla_kernel(qk_ref, v_ref, gg_ref, la_ref, w3_ref, mk_ref, gain_ref, out_ref, st_ref, *, NC, nb):
    L = G_CHUNK

    @pl.when(pl.program_id(1) == 0)
    def _():
        st_ref[...] = jnp.zeros_like(st_ref)

    lane_head = lax.broadcasted_iota(jnp.int32, (L, G_KW), 1) // G_DK
    br = lax.broadcasted_iota(jnp.int32, (2 * G_DV, LANES), 0) < G_DV
    bl = lax.broadcasted_iota(jnp.int32, (2 * G_DV, LANES), 1) < G_DK
    bmask = br == bl
    nt = (((1,), (1,)), ((), ()))
    tn = (((0,), (0,)), ((), ()))

    def chunk(c, carry):
        rows = pl.ds(pl.multiple_of(c * L, L), L)
        X, q, k = [], [], []
        for bi in range(nb):
            la = la_ref[bi, rows, :]
            hi = la.astype(BF16)
            r1 = la - hi.astype(F32)
            mid = r1.astype(BF16)
            lo = (r1 - mid.astype(F32)).astype(BF16)
            stk = jnp.concatenate([hi, mid, lo], axis=0)
            X.append(jnp.exp(jnp.dot(w3_ref[...], stk, preferred_element_type=F32)))
            q.append(qk_ref[bi, rows, 0:G_KW].astype(F32))
            k.append(qk_ref[bi, rows, G_KW:2 * G_KW].astype(F32))

        sc = [[None] * (_G_LEVELS + 1) for _ in range(nb)]
        for lev in range(_G_LEVELS + 1):
            for bi in range(nb):
                if lev == 0:
                    qt, kt = q[bi], k[bi]
                else:
                    xl = X[bi][_G_XROW + L * (lev - 1):_G_XROW + L * lev, :]
                    qt, kt = q[bi] * xl, k[bi] * xl
                q4 = jnp.concatenate([jnp.where(lane_head == h, qt, 0.0) for h in range(G_HEADS)],
                                     axis=0).astype(BF16)
                sc[bi][lev] = lax.dot_general(q4, kt.astype(BF16), nt, preferred_element_type=F32)
        Ab = []
        for bi in range(nb):
            A = sc[bi][0] * mk_ref[0]
            for lev in range(1, _G_LEVELS + 1):
                A = A + sc[bi][lev] * mk_ref[lev]
            Ab.append(A.astype(BF16))

        for bi in range(nb):
            gg = gg_ref[bi, rows, :].astype(F32)
            gate = gg * _sigmoid(gg)
            for p in range(2):
                ls = slice(LANES * p, LANES * (p + 1))
                vp = v_ref[bi, rows, 2 * G_DV * p:2 * G_DV * (p + 1)]
                oi = [jnp.dot(Ab[bi][L * (2 * p + hh):L * (2 * p + hh + 1)],
                              vp[:, G_DV * hh:G_DV * (hh + 1)], preferred_element_type=F32)
                      for hh in range(2)]
                st = st_ref[bi, p]
                qc = (q[bi][:, ls] * X[bi][0:L, ls]).astype(BF16)
                o_inter = lax.dot_general(qc, st.astype(BF16), nt, preferred_element_type=F32)
                kc = (k[bi][:, ls] * X[bi][L:2 * L, ls]).astype(BF16)
                upd = lax.dot_general(vp, kc, tn, preferred_element_type=F32)
                dec = X[bi][2 * L:2 * L + 1, ls]
                st_ref[bi, p] = jnp.where(bmask, dec * st + upd, 0.0)
                for hh in range(2):
                    o = o_inter[:, G_DV * hh:G_DV * (hh + 1)] + oi[hh]
                    hn = o * lax.rsqrt(jnp.mean(o * o, axis=-1, keepdims=True) + LN_EPS)
                    hs = slice(G_DV * (2 * p + hh), G_DV * (2 * p + hh + 1))
                    out_ref[bi, rows, hs] = (hn * gain_ref[:, hs] * gate[:, hs]).astype(BF16)
        return carry

    lax.fori_loop(0, NC, chunk, 0)


def _gla(oa, la, w3, mk, gain, *, B, S, nb, ts):
    N = oa.shape[0]
    oa3 = oa.reshape(B, S, oa.shape[1])
    la3 = la.reshape(B, S, G_KW)
    kern = functools.partial(_gla_kernel, NC=ts // G_CHUNK, nb=nb)
    out = pl.pallas_call(
        kern,
        grid=(B // nb, S // ts),
        in_specs=[pl.BlockSpec((nb, ts, 2 * G_KW), lambda b, t: (b, t, C_GQK // (2 * G_KW))),
                  pl.BlockSpec((nb, ts, G_W), lambda b, t: (b, t, C_GV // G_W)),
                  pl.BlockSpec((nb, ts, G_W), lambda b, t: (b, t, C_GG // G_W)),
                  pl.BlockSpec((nb, ts, G_KW), lambda b, t: (b, t, 0)),
                  pl.BlockSpec(w3.shape, lambda b, t: (0, 0)),
                  pl.BlockSpec(mk.shape, lambda b, t: (0, 0, 0)),
                  pl.BlockSpec((1, G_W), lambda b, t: (0, 0))],
        out_specs=pl.BlockSpec((nb, ts, G_W), lambda b, t: (b, t, 0)),
        out_shape=jax.ShapeDtypeStruct((B, S, G_W), BF16),
        scratch_shapes=[pltpu.VMEM((nb, 2, 2 * G_DV, LANES), F32)],
        compiler_params=_cparams(2),
        name="gla",
    )(oa3, oa3, oa3, la3, w3, mk, gain)
    return out.reshape(N, G_W)


def _layer_norm(z, g, b):
    mu = jnp.mean(z, axis=-1, keepdims=True)
    zc = z - mu
    var = jnp.mean(zc * zc, axis=-1, keepdims=True)
    return zc * lax.rsqrt(var + LN_EPS) * g + b


def _outproj_kernel(hm_ref, hg_ref, wf_ref, x_ref, mod_ref, g_ref, b_ref, wrg_ref, wre_ref, br_ref,
                    x1_ref, u2_ref, rrow_ref, w_ref, wr_ref, *, tb, nh):
    @pl.when(pl.program_id(0) == 0)
    def _():
        w_ref[...] = wf_ref[...].astype(BF16)
        z = lambda n: jnp.zeros((n, wrg_ref.shape[2]), F32)
        wt = jnp.concatenate([wrg_ref[0], z(SUBLANES - N_GROUPS), wre_ref[0],
                              z(LANES - SUBLANES - N_EXP)], axis=0).T
        hi = wt.astype(BF16)
        wr_ref[:, 0:LANES] = hi
        wr_ref[:, LANES:2 * LANES] = (wt - hi.astype(F32)).astype(BF16)

    mod = mod_ref[0]
    blocks = [slice(tb * j, tb * (j + 1)) for j in range(nh)]
    y = [jnp.dot(hm_ref[r, :], w_ref[0:M_W, :], preferred_element_type=F32)
         + jnp.dot(hg_ref[r, :], w_ref[M_W:M_W + G_W, :], preferred_element_type=F32) for r in blocks]
    u2 = []
    for j, r in enumerate(blocks):
        z = ALPHA * x_ref[r, :] + (1.0 + mod[2:3, :]) * y[j]
        x1 = _layer_norm(z, g_ref[...], b_ref[...])
        x1_ref[r, :] = x1
        u2.append(x1 * (1.0 + mod[4:5, :]) + mod[3:4, :])
        u2_ref[r, :] = u2[j].astype(BF16)

    u2h = [u.astype(BF16) for u in u2]
    u2l = [(u2[j] - u2h[j].astype(F32)).astype(BF16) for j in range(nh)]
    lh = [jnp.dot(u, wr_ref[...], preferred_element_type=F32) for u in u2h]
    ll = [jnp.dot(u, wr_ref[:, 0:LANES], preferred_element_type=F32) for u in u2l]
    for j in range(nh):
        logits = lh[j][:, 0:LANES] + lh[j][:, LANES:2 * LANES] + ll[j] + br_ref[...]
        rrow_ref[j] = _route_select(logits.T, tb)


def _route_select(lt, tm):
    row = lax.broadcasted_iota(jnp.int32, (SUBLANES, tm), 0)
    gl = jnp.where(row < N_GROUPS, lt[0:SUBLANES, :], -jnp.inf)
    gmax = jnp.max(gl, axis=0, keepdims=True)
    gsel = jnp.min(jnp.where(gl == gmax, row, SUBLANES), axis=0, keepdims=True)
    pg = 1.0 / jnp.sum(jnp.exp(gl - gmax), axis=0, keepdims=True)
    ein = jnp.zeros((SUBLANES, tm), F32)
    for g in range(N_GROUPS):
        ein = jnp.where(gsel == g, lt[SUBLANES * (g + 1):SUBLANES * (g + 2), :], ein)
    v1 = jnp.max(ein, axis=0, keepdims=True)
    i1 = jnp.min(jnp.where(ein == v1, row, SUBLANES), axis=0, keepdims=True)
    rest = jnp.where(row == i1, -jnp.inf, ein)
    v2 = jnp.max(rest, axis=0, keepdims=True)
    i2 = jnp.min(jnp.where(rest == v2, row, SUBLANES), axis=0, keepdims=True)
    t2 = jnp.exp(v2 - v1)
    p1 = 1.0 / (1.0 + t2)
    e0 = (gsel * E_PER_G + i1).astype(F32)
    e1 = (gsel * E_PER_G + i2).astype(F32)
    return jnp.concatenate([e0, e1, pg * p1, pg * (t2 * p1), jnp.zeros((SUBLANES - 4, tm), F32)], axis=0)


def _outproj(hm, hg, w_out, x2, mod3, g, b, wrg_t, wre_t, br, *, S, tb, nh, layer):
    N, D = x2.shape
    tm = tb * nh
    tpb = S // tm
    kern = functools.partial(_outproj_kernel, tb=tb, nh=nh)
    return pl.pallas_call(
        kern,
        grid=(N // tm,),
        in_specs=[pl.BlockSpec((tm, M_W), lambda i: (i, 0)),
                  pl.BlockSpec((tm, G_W), lambda i: (i, 0)),
                  pl.BlockSpec((M_W + G_W, D), lambda i: (0, 0), pipeline_mode=pl.Buffered(1)),
                  pl.BlockSpec((tm, D), lambda i: (i, 0)),
                  pl.BlockSpec((1, 6, D), lambda i: (i // tpb, 0, 0)),
                  pl.BlockSpec((1, D), lambda i: (0, 0)),
                  pl.BlockSpec((1, D), lambda i: (0, 0)),
                  pl.BlockSpec((1, N_GROUPS, D), lambda i: (layer, 0, 0)),
                  pl.BlockSpec((1, N_EXP, D), lambda i: (layer, 0, 0)),
                  pl.BlockSpec((1, LANES), lambda i: (0, 0))],
        out_specs=[pl.BlockSpec((tm, D), lambda i: (i, 0)),
                   pl.BlockSpec((tm, D), lambda i: (i, 0)),
                   pl.BlockSpec((nh, SUBLANES, tb), lambda i: (i, 0, 0))],
        out_shape=[jax.ShapeDtypeStruct((N, D), F32),
                   jax.ShapeDtypeStruct((N, D), BF16),
                   jax.ShapeDtypeStruct((N // tb, SUBLANES, tb), F32)],
        scratch_shapes=[pltpu.VMEM((M_W + G_W, D), BF16), pltpu.VMEM((D, 2 * LANES), BF16)],
        compiler_params=_cparams(),
        name="outproj",
    )(hm, hg, w_out, x2, mod3, g, b, wrg_t, wre_t, br)


def _slots_per_tile(tb):
    worst = 2 * tb + N_EXP * (GRAN - 1)
    return -(-worst // LANES) * LANES


def _ffn_tiles(n_tok, tb):
    worst_rows = 2 * n_tok + (n_tok // tb) * N_EXP * (GRAN - 1)
    return -(-worst_rows // FFN_TM) + N_EXP


def _route_kernel(rr_ref, u_ref, lt_ref, srow_ref, col_ref, gd_ref, meta_ref, mg_ref, part_ref,
                  *, NT, tb, TM):
    iota_e = lax.broadcasted_iota(jnp.int32, (N_EXP, tb), 0).astype(F32)
    glane = lax.broadcasted_iota(jnp.int32, (N_EXP, LANES), 1).astype(F32)
    ltri = lt_ref[...]

    def prefix_e(col):
        return jnp.dot(ltri, jnp.broadcast_to(col, (N_EXP, LANES)),
                       preferred_element_type=F32, precision=HIGHEST)[:, 0:1]

    def p1(j, run8):
        r = rr_ref[j]
        oh0 = jnp.where(iota_e == r[0:1, :], 1.0, 0.0)
        oh1 = jnp.where(iota_e == r[1:2, :], 1.0, 0.0)
        cum0 = jnp.dot(oh0.astype(BF16), u_ref[...], preferred_element_type=F32)
        cum1 = jnp.dot(oh1.astype(BF16), u_ref[...], preferred_element_type=F32)
        c0 = jnp.sum(oh0, axis=1, keepdims=True)
        n8 = jnp.floor((c0 + jnp.sum(oh1, axis=1, keepdims=True) + (GRAN - 1.0)) * (1.0 / GRAN))
        lo8 = prefix_e(n8)
        s0 = jnp.sum(oh0 * (GRAN * lo8 + cum0 - 1.0), axis=0, keepdims=True)
        s1 = jnp.sum(oh1 * (GRAN * lo8 + c0 + cum1 - 1.0), axis=0, keepdims=True)
        info = jnp.concatenate([s0, s1, r[2:4, :], jnp.zeros((SUBLANES - 4, tb), F32)], axis=0)
        srow_ref[j] = info
        col_ref[pl.ds(pl.multiple_of(j * tb, tb), tb), :] = jnp.concatenate(
            [info, jnp.zeros((LANES - SUBLANES, tb), F32)], axis=0).T
        mg = jnp.where((lo8 <= glane) & (glane < lo8 + n8), 1.0, 0.0)
        mg_ref[j] = mg
        part = jnp.sum(mg * (run8 + glane - lo8), axis=0, keepdims=True)
        gcnt = jnp.broadcast_to(jnp.sum(n8, axis=0, keepdims=True), (1, LANES))
        part_ref[j] = jnp.concatenate([part, gcnt, jnp.zeros((SUBLANES - 2, LANES), F32)], axis=0)
        return run8 + n8

    tot8 = lax.fori_loop(0, NT, p1, jnp.zeros((N_EXP, 1), F32), unroll=8 if NT % 8 == 0 else 1)
    seg_t = jnp.floor((tot8 * GRAN + (TM - 1.0)) * (1.0 / TM))
    base_t = prefix_e(seg_t)
    base8 = base_t * (TM // GRAN)
    lane1 = lax.broadcasted_iota(jnp.int32, (1, LANES), 1)

    def p2(j, carry):
        pr = part_ref[j]
        dst = (pr[0:1, :] + jnp.sum(mg_ref[j] * base8, axis=0, keepdims=True)) * GRAN
        gd_ref[j] = jnp.where(lane1 == G_LAST, pr[1:2, :], dst).astype(jnp.int32)
        return carry

    lax.fori_loop(0, NT, p2, 0, unroll=8 if NT % 8 == 0 else 1)
    eye = jnp.where(glane == lax.broadcasted_iota(jnp.int32, (N_EXP, LANES), 0).astype(F32), 1.0, 0.0)
    tail_row = jnp.sum(eye * ((base8 + tot8) * GRAN), axis=0, keepdims=True)
    tail_n8 = jnp.sum(eye * (seg_t * (TM // GRAN) - tot8), axis=0, keepdims=True)
    nv_l = jnp.broadcast_to(jnp.sum(seg_t, axis=0, keepdims=True), (1, LANES))
    gd_ref[NT] = jnp.where(lane1 == G_LAST, nv_l, tail_row).astype(jnp.int32)
    gd_ref[NT + 1] = tail_n8.astype(jnp.int32)
    ti = lax.broadcasted_iota(jnp.int32, (N_EXP, tb), 1).astype(F32)
    te = jnp.sum(jnp.where(base_t <= ti, 1.0, 0.0), axis=0, keepdims=True) - 1.0
    nv = jnp.broadcast_to(jnp.sum(seg_t, axis=0, keepdims=True), (1, tb))
    own = jnp.where((base_t <= ti) & (ti < base_t + seg_t), 1.0, 0.0)
    vr = jnp.sum(own * jnp.clip(tot8 * GRAN - (ti - base_t) * TM, 0.0, TM), axis=0, keepdims=True)
    meta_ref[...] = jnp.concatenate([te, nv, vr, jnp.zeros((SUBLANES - 3, tb), F32)],
                                    axis=0).astype(jnp.int32)


def _route(rrow, u_cnt, ltri, *, TM):
    NT, _, tb = rrow.shape
    kern = functools.partial(_route_kernel, NT=NT, tb=tb, TM=TM)
    full3 = lambda i: (0, 0, 0)
    return pl.pallas_call(
        kern,
        grid=(1,),
        in_specs=[pl.BlockSpec((NT, SUBLANES, tb), full3),
                  pl.BlockSpec((tb, tb), lambda i: (0, 0)),
                  pl.BlockSpec((N_EXP, N_EXP), lambda i: (0, 0))],
        out_specs=[pl.BlockSpec((NT, SUBLANES, tb), full3),
                   pl.BlockSpec((NT * tb, LANES), lambda i: (0, 0)),
                   pl.BlockSpec((NT + 2, 1, LANES), full3),
                   pl.BlockSpec((SUBLANES, tb), lambda i: (0, 0))],
        out_shape=[jax.ShapeDtypeStruct((NT, SUBLANES, tb), F32),
                   jax.ShapeDtypeStruct((NT * tb, LANES), F32),
                   jax.ShapeDtypeStruct((NT + 2, 1, LANES), jnp.int32),
                   jax.ShapeDtypeStruct((SUBLANES, tb), jnp.int32)],
        scratch_shapes=[pltpu.VMEM((NT, N_EXP, LANES), F32), pltpu.VMEM((NT, SUBLANES, LANES), F32)],
        compiler_params=_cparams(),
        name="route",
    )(rrow, u_cnt, ltri)


_HI_MASK = 0xFFFF0000


def _pack_halves(x):
    c = x.shape[1] // 2
    lo = lax.bitcast_convert_type(x[:, :c], U32)
    hi = lax.bitcast_convert_type(x[:, c:], U32)
    return (lo >> 16) | (hi & U32(_HI_MASK))


def _unpack_halves(w):
    lo = lax.bitcast_convert_type(w << 16, F32)
    hi = lax.bitcast_convert_type(w & U32(_HI_MASK), F32)
    return jnp.concatenate([lo, hi], axis=1).astype(BF16)


def _granule_copy(src_ref, src_row, dst_ref, dst_row, sem):
    return pltpu.make_async_copy(src_ref.at[pl.ds(src_row, GRAN), :], dst_ref.at[pl.ds(dst_row, GRAN), :], sem)


def _for_granules(n, body, unroll=4):
    def blk(i, carry):
        for t in range(unroll):
            body(i * unroll + t, t % 2)
        return carry

    def one(g, carry):
        body(g, 0)
        return carry

    nblk = n // unroll
    lax.fori_loop(0, nblk, blk, 0)
    lax.fori_loop(nblk * unroll, n, one, 0)


def _wait_granules(n, src_ref, dst_ref, sem, n_max):
    b = 1
    while b <= n_max:
        @pl.when((n & b) != 0)
        def _(b=b):
            pltpu.make_async_copy(src_ref.at[pl.ds(0, b * GRAN), :], dst_ref.at[pl.ds(0, b * GRAN), :],
                                  sem).wait()
        b *= 2


def _dispatch_kernel(gd_ref, srow_ref, u_ref, xs_ref, buf, zbuf, sems, *, NT, SL, TM, n_tiles):
    j = pl.program_id(0)
    slot = j % 2
    zsem = sems.at[2]

    def drain(tile, sl):
        _wait_granules(gd_ref[tile, G_LAST], buf.at[sl], xs_ref, sems.at[sl], SL // GRAN)

    def tile_fill(t):
        return pltpu.make_async_copy(zbuf, xs_ref.at[pl.ds(pl.multiple_of(t * TM, TM), TM), :], zsem)

    def zero_fill(wait):
        for e in range(N_EXP):
            n, row0 = gd_ref[NT + 1, e], gd_ref[NT, e]
            b = TM // GRAN // 2
            while b >= 1:
                @pl.when((n & b) != 0)
                def _(b=b, n=n, row0=row0):
                    start = pl.multiple_of(row0 + ((n >> b.bit_length()) << b.bit_length()) * GRAN, GRAN)
                    cp = pltpu.make_async_copy(zbuf.at[pl.ds(0, b * GRAN), :],
                                               xs_ref.at[pl.ds(start, b * GRAN), :], zsem)
                    cp.wait() if wait else cp.start()
                b //= 2

        def zt(t, carry):
            tile_fill(t).wait() if wait else tile_fill(t).start()
            return carry
        lax.fori_loop(gd_ref[NT, G_LAST], n_tiles, zt, 0)

    @pl.when(j == 0)
    def _():
        zbuf[...] = jnp.zeros_like(zbuf)
        zero_fill(False)

    @pl.when(j >= 2)
    def _():
        drain(j - 2, slot)

    s = srow_ref[0]
    rows = lax.broadcasted_iota(jnp.int32, (SL, s.shape[1]), 0).astype(F32)
    m0 = rows == s[0:1, :]
    m1 = rows == s[1:2, :]
    oh = jnp.where(m0 | m1, 1.0, 0.0).astype(BF16)
    dw = u_ref.shape[1] // 2
    buf[slot, :, 0:dw] = _pack_halves(jnp.dot(oh, u_ref[...], preferred_element_type=F32))
    wrow = jnp.sum(jnp.where(m0, s[2:3, :], 0.0) + jnp.where(m1, s[3:4, :], 0.0), axis=1, keepdims=True)
    buf[slot, :, dw:dw + LANES] = lax.bitcast_convert_type(jnp.broadcast_to(wrow, (SL, LANES)), U32)

    def issue(g, priority):
        _granule_copy(buf.at[slot], pl.multiple_of(g * GRAN, GRAN), xs_ref,
                      pl.multiple_of(gd_ref[j, g], GRAN), sems.at[slot]).start(priority=priority)

    _for_granules(gd_ref[j, G_LAST], issue)

    @pl.when(j == NT - 1)
    def _():
        drain(j, slot)
        if NT > 1:
            drain(j - 1, 1 - slot)
        zero_fill(True)


def _dispatch(gd, srow, u2, *, n_tiles, TM):
    N, D = u2.shape
    NT, _, tb = srow.shape
    SL = _slots_per_tile(tb)
    n_rows = n_tiles * TM
    kern = functools.partial(_dispatch_kernel, NT=NT, SL=SL, TM=TM, n_tiles=n_tiles)
    grid_spec = pltpu.PrefetchScalarGridSpec(
        num_scalar_prefetch=1,
        grid=(NT,),
        in_specs=[pl.BlockSpec((1, SUBLANES, tb), lambda j, gd: (j, 0, 0)),
                  pl.BlockSpec((tb, D), lambda j, gd: (j, 0))],
        out_specs=pl.BlockSpec(memory_space=pl.ANY),
        scratch_shapes=[pltpu.VMEM((2, SL, D // 2 + LANES), U32), pltpu.VMEM((TM, D // 2 + LANES), U32),
                        pltpu.SemaphoreType.DMA((3,))],
    )
    return pl.pallas_call(
        kern,
        grid_spec=grid_spec,
        out_shape=jax.ShapeDtypeStruct((n_rows, D // 2 + LANES), U32),
        compiler_params=_cparams(),
        name="dispatch",
    )(gd, srow, u2)


def _ffn_kernel(te_ref, nv_ref, vr_ref, xs_ref, wg_ref, wu_ref, wd_ref, o_ref, wgb, wub, wdb, sg, su, sd, slot_ref,
                sems):
    i = pl.program_id(0)
    nv = nv_ref[0]
    e = te_ref[i]

    def weight_copies(ex, sl):
        return (pltpu.make_async_copy(wg_ref.at[ex], sg.at[sl], sems.at[sl]),
                pltpu.make_async_copy(wu_ref.at[ex], su.at[sl], sems.at[sl]),
                pltpu.make_async_copy(wd_ref.at[ex], sd.at[sl], sems.at[sl]))

    @pl.when(i == 0)
    def _():
        slot_ref[0] = 0
        for cp in weight_copies(e, 0):
            cp.start()

    @pl.when((i < nv) & ((i == 0) | (e != te_ref[jnp.maximum(i - 1, 0)])))
    def _():
        sl = slot_ref[0]
        for cp in weight_copies(e, sl):
            cp.wait()
        wgb[...] = sg[sl].astype(BF16)
        wub[...] = su[sl].astype(BF16)
        wdb[...] = sd[sl].astype(BF16)
        nxt = lax.while_loop(lambda t: (t < nv) & (te_ref[jnp.minimum(t, nv - 1)] == e), lambda t: t + 1, i + 1)

        @pl.when(nxt < nv)
        def _():
            for cp in weight_copies(te_ref[nxt], 1 - sl):
                cp.start()
        slot_ref[0] = 1 - sl

    hm = xs_ref.shape[0] // FFN_SUB
    dw = o_ref.shape[1]

    def swiglu_rows(nsub):
        halves = tuple(slice(hm * j, hm * (j + 1)) for j in range(nsub))
        x = [_unpack_halves(xs_ref[r, 0:dw]) for r in halves]
        g = [jnp.dot(x[j], wgb[...], preferred_element_type=F32) for j in range(nsub)]
        u = [jnp.dot(x[j], wub[...], preferred_element_type=F32) for j in range(nsub)]
        h = [(g[j] * _sigmoid(g[j]) * u[j]).astype(BF16) for j in range(nsub)]
        y = [jnp.dot(h[j], wdb[...], preferred_element_type=F32) for j in range(nsub)]
        for j in range(nsub):
            wt = lax.bitcast_convert_type(xs_ref[halves[j], dw:dw + LANES], F32)
            yw = y[j] * jnp.concatenate([wt] * (2 * dw // LANES), axis=1)
            o_ref[halves[j], :] = _pack_halves(yw.astype(BF16).astype(F32))
        if nsub < FFN_SUB:
            o_ref[hm * nsub:, :] = jnp.zeros((hm * (FFN_SUB - nsub), dw), U32)

    used = vr_ref[i]
    for nsub in range(1, FFN_SUB + 1):
        lo, hi = hm * (nsub - 1), hm * nsub
        pl.when((i < nv) & (used > lo) & ((used <= hi) if nsub < FFN_SUB else True))(
            functools.partial(swiglu_rows, nsub))

    @pl.when(i >= nv_ref[0])
    def _():
        o_ref[...] = jnp.zeros_like(o_ref)


def _ffn(te, nv, vr, xs, wg, wu, wd, *, TM):
    P, XW = xs.shape
    DW = XW - LANES
    D = 2 * DW
    n_tiles = P // TM
    grid_spec = pltpu.PrefetchScalarGridSpec(
        num_scalar_prefetch=3,
        grid=(n_tiles,),
        in_specs=[pl.BlockSpec((TM, XW), lambda i, te, nv, vr: (jnp.maximum(jnp.minimum(i, nv[0] - 1), 0), 0)),
                  pl.BlockSpec(memory_space=pl.ANY),
                  pl.BlockSpec(memory_space=pl.ANY),
                  pl.BlockSpec(memory_space=pl.ANY)],
        out_specs=pl.BlockSpec((TM, DW), lambda i, te, nv, vr: (i, 0)),
        scratch_shapes=[pltpu.VMEM((D, D_EXP), BF16), pltpu.VMEM((D, D_EXP), BF16),
                        pltpu.VMEM((D_EXP, D), BF16),
                        pltpu.VMEM((2, D, D_EXP), F32), pltpu.VMEM((2, D, D_EXP), F32),
                        pltpu.VMEM((2, D_EXP, D), F32), pltpu.SMEM((1,), jnp.int32),
                        pltpu.SemaphoreType.DMA((2,))],
    )
    return pl.pallas_call(
        _ffn_kernel,
        grid_spec=grid_spec,
        out_shape=jax.ShapeDtypeStruct((P, DW), U32),
        compiler_params=_cparams(),
        name="ffn",
    )(te, nv, vr, xs, wg, wu, wd)


def _combine_kernel(gd_ref, ys_ref, col_ref, x1_ref, mod_ref, g_ref, b_ref, o_ref, buf, sems, *, NT, SL):
    j = pl.program_id(0)
    slot = j % 2

    def fetch(tile, sl):
        def f(g, priority):
            _granule_copy(ys_ref, pl.multiple_of(gd_ref[tile, g], GRAN), buf.at[sl],
                          pl.multiple_of(g * GRAN, GRAN), sems.at[sl]).start(priority=priority)
        _for_granules(gd_ref[tile, G_LAST], f)

    @pl.when(j == 0)
    def _():
        fetch(0, 0)

    @pl.when(j + 1 < NT)
    def _():
        fetch(j + 1, 1 - slot)

    ng = gd_ref[j, G_LAST]

    _wait_granules(ng, ys_ref, buf.at[slot], sems.at[slot], SL // GRAN)

    rows = lax.broadcasted_iota(jnp.int32, (SL, 1), 0)
    yb = _unpack_halves(jnp.where(rows < ng * GRAN, buf[slot], U32(0)))
    col = col_ref[...]
    tb = col.shape[0]
    lanes = lax.broadcasted_iota(jnp.int32, (tb, SL), 1).astype(F32)
    sel = jnp.where((lanes == col[:, 0:1]) | (lanes == col[:, 1:2]), 1.0, 0.0).astype(BF16)
    y = jnp.dot(sel, yb, preferred_element_type=F32)
    mod = mod_ref[0]
    z = ALPHA * x1_ref[...] + (1.0 + mod[5:6, :]) * y
    o_ref[...] = _layer_norm(z, g_ref[...], b_ref[...])


def _combine(gd, ys, col, x1, mod3, g, b, *, S, tb):
    N, D = x1.shape
    NT = N // tb
    tpb = S // tb
    SL = _slots_per_tile(tb)
    kern = functools.partial(_combine_kernel, NT=NT, SL=SL)
    grid_spec = pltpu.PrefetchScalarGridSpec(
        num_scalar_prefetch=1,
        grid=(NT,),
        in_specs=[pl.BlockSpec(memory_space=pl.ANY),
                  pl.BlockSpec((tb, LANES), lambda j, gd: (j, 0)),
                  pl.BlockSpec((tb, D), lambda j, gd: (j, 0)),
                  pl.BlockSpec((1, 6, D), lambda j, gd: (j // tpb, 0, 0)),
                  pl.BlockSpec((1, D), lambda j, gd: (0, 0)),
                  pl.BlockSpec((1, D), lambda j, gd: (0, 0))],
        out_specs=pl.BlockSpec((tb, D), lambda j, gd: (j, 0)),
        scratch_shapes=[pltpu.VMEM((2, SL, D // 2), U32), pltpu.SemaphoreType.DMA((2,))],
    )
    return pl.pallas_call(
        kern,
        grid_spec=grid_spec,
        out_shape=jax.ShapeDtypeStruct((N, D), F32),
        compiler_params=_cparams(),
        name="combine",
    )(gd, ys, col, x1, mod3, g, b)


def _layer(x, c, l, w_ada, b_ada, w_in, w_conv, b_conv, b_igate, b_fgate, mlstm_norm_g, w_gla_a, b_gla_a,
           gla_norm_g, w_out, ln1_g, ln1_b, w_route_group, b_route_group, w_route_expert, b_route_expert,
           w_gate, w_up, w_down, ln2_g, ln2_b):
    B, S, D = x.shape
    N = B * S
    x2 = x.reshape(N, D)
    tm_in = min(512, S)
    tm = min(256, S)
    lm = min(256, S)

    mod3 = _ada(c, w_ada[l], b_ada[l]).reshape(B, 6, D)

    wa_pad = jnp.zeros((LANES, G_KW), F32).at[SM_A:SM_A + G_RANK].set(w_gla_a[l]).astype(BF16)
    bg = (jnp.zeros((2 * SUBLANES, 1), F32).at[0:M_HEADS, 0].set(b_igate[l])
          .at[SUBLANES:SUBLANES + M_HEADS, 0].set(b_fgate[l]))
    oa, la, g3 = _inproj(x2, mod3, jnp.swapaxes(w_in, 1, 2), w_conv[l], b_conv[l].reshape(1, -1), wa_pad,
                         b_gla_a[l].reshape(1, -1), bg, S=S, tm=tm_in, lm=lm, layer=l)

    u_tri = jnp.asarray(np.triu(np.ones((lm, lm), np.float32)))
    nb = 4 if B % 4 == 0 else (2 if B % 2 == 0 else 1)
    ts = min(512, S)
    hm = _mlstm(oa, g3, u_tri, mlstm_norm_g[l].reshape(1, -1), B=B, S=S, L=lm, nb=nb, ts=ts)
    w3_np, mk_np = _gla_consts()
    hg = _gla(oa, la, jnp.asarray(w3_np, BF16), jnp.asarray(mk_np), gla_norm_g[l].reshape(1, -1), B=B, S=S,
              nb=nb, ts=ts)

    br = (jnp.zeros((1, LANES), F32).at[0, 0:N_GROUPS].set(b_route_group[l])
          .at[0, SUBLANES:SUBLANES + N_EXP].set(b_route_expert[l]))
    x1, u2, rrow = _outproj(hm, hg, w_out[l], x2, mod3, ln1_g[l].reshape(1, -1), ln1_b[l].reshape(1, -1),
                            jnp.swapaxes(w_route_group, 1, 2), jnp.swapaxes(w_route_expert, 1, 2), br,
                            S=S, tb=tm, nh=4 if S % (4 * tm) == 0 else 1, layer=l)

    u_cnt = jnp.asarray(np.triu(np.ones((tm, tm), np.float32)), BF16)
    ltri = jnp.asarray(np.tril(np.ones((N_EXP, N_EXP), np.float32), -1))
    srow, col, gd3, meta = _route(rrow, u_cnt, ltri, TM=FFN_TM)
    gd = gd3.reshape(N // tm + 2, LANES)
    n_tiles = _ffn_tiles(N, tm)
    te, nv, vr = meta[0, :n_tiles], meta[1, 0:1], meta[2, :n_tiles]

    xs = _dispatch(gd, srow, u2, n_tiles=n_tiles, TM=FFN_TM)
    ys = _ffn(te, nv, vr, xs, w_gate[l], w_up[l], w_down[l], TM=FFN_TM)
    out = _combine(gd, ys, col, x1, mod3, ln2_g[l].reshape(1, -1), ln2_b[l].reshape(1, -1), S=S, tb=tm)
    return out.reshape(B, S, D)


def kernel(x, c, w_ada, b_ada, w_in, w_conv, b_conv, b_igate, b_fgate, mlstm_norm_g, w_gla_a, b_gla_a,
           gla_norm_g, w_out, ln1_g, ln1_b, w_route_group, b_route_group, w_route_expert, b_route_expert,
           w_gate, w_up, w_down, ln2_g, ln2_b):
    for l in range(DEPTH):
        x = _layer(x, c, l, w_ada, b_ada, w_in, w_conv, b_conv, b_igate, b_fgate, mlstm_norm_g, w_gla_a,
                   b_gla_a, gla_norm_g, w_out, ln1_g, ln1_b, w_route_group, b_route_group, w_route_expert,
                   b_route_expert, w_gate, w_up, w_down, ln2_g, ln2_b)
    return x
```

```python
import functools

import numpy as np
import jax
import jax.numpy as jnp
from jax import lax
from jax.experimental import pallas as pl
from jax.experimental.pallas import tpu as pltpu

F32 = jnp.float32
BF16 = jnp.bfloat16
U32 = jnp.uint32
HIGHEST = lax.Precision.HIGHEST

DEPTH = 1
M_HEADS = 4
M_HD = 128
M_W = M_HEADS * M_HD
CONV_W = 4
G_HEADS = 4
G_DK = 64
G_DV = 128
G_W = G_HEADS * G_DV
G_KW = G_HEADS * G_DK
G_RANK = 16
G_TAU = 16.0
G_CHUNK = 64
N_GROUPS = 4
E_PER_G = 8
N_EXP = N_GROUPS * E_PER_G
D_EXP = 512
ALPHA = (2 * DEPTH) ** 0.25
LN_EPS = 1e-5

LANES = 128
SUBLANES = 8
VMEM_LIMIT = 48 * 1024 * 1024

C_QK = 0
C_VO = 1024
C_GQK = 2048
C_GV = 2560
C_GG = 3072
C_SMALL = 3584
C_TOT = 3712
SM_I, SM_F, SM_A = 0, 8, 16
IN_GATES = 4 * M_W
IN_G = IN_GATES + 2 * M_HEADS
IN_GA = IN_G + 2 * G_KW + 2 * G_W
IN_TOT = IN_GA + G_RANK

FFN_TM = 512
FFN_SUB = 2
GRAN = SUBLANES
G_LAST = LANES - 1


def _cparams(n_axes=1):
    return pltpu.CompilerParams(dimension_semantics=("arbitrary",) * n_axes,
                                vmem_limit_bytes=VMEM_LIMIT)


def _sigmoid(x):
    return 1.0 / (1.0 + jnp.exp(-x))


def _log_sigmoid(x):
    return jnp.minimum(x, 0.0) - jnp.log(1.0 + jnp.exp(-jnp.abs(x)))


def _ada_kernel(c_ref, w_ref, b_ref, o_ref):
    c = c_ref[...]
    ca = (c * _sigmoid(c)).astype(BF16)
    o_ref[...] = jnp.dot(ca, w_ref[...].astype(BF16), preferred_element_type=F32) + b_ref[...]


def _ada(c, w, b):
    B, D = c.shape
    n_out = w.shape[1]
    tn = 1024
    return pl.pallas_call(
        _ada_kernel,
        grid=(n_out // tn,),
        in_specs=[pl.BlockSpec((B, D), lambda j: (0, 0)),
                  pl.BlockSpec((D, tn), lambda j: (0, j)),
                  pl.BlockSpec((1, tn), lambda j: (0, j))],
        out_specs=pl.BlockSpec((B, tn), lambda j: (0, j)),
        out_shape=jax.ShapeDtypeStruct((B, n_out), F32),
        compiler_params=_cparams(),
        name="ada",
    )(c, w, b.reshape(1, n_out))


def _inproj_kernel(x_ref, mod_ref, win_ref, wc_ref, bc_ref, wa_ref, ba_ref, bg_ref,
                   oa_ref, la_ref, g_ref, halo_ref, w_ref, *, tm, tpb, lm):
    i = pl.program_id(0)

    @pl.when(i == 0)
    def _():
        rc = 2 * LANES
        for r in range(0, IN_GATES, rc):
            w_ref[:, r:r + rc] = win_ref[0, r:r + rc, :].T.astype(BF16)
        for r in range(0, C_SMALL - C_GQK, rc):
            w_ref[:, C_GQK + r:C_GQK + r + rc] = win_ref[0, IN_G + r:IN_G + r + rc, :].T.astype(BF16)
        gates = win_ref[0, IN_GATES:IN_G, :]
        z = lambda n: jnp.zeros((n, gates.shape[1]), F32)
        small = jnp.concatenate([gates[0:M_HEADS], z(SM_F - M_HEADS), gates[M_HEADS:2 * M_HEADS],
                                 z(SM_A - SM_F - M_HEADS), win_ref[0, IN_GA:IN_TOT, :],
                                 z(LANES - SM_A - G_RANK)], axis=0)
        w_ref[:, C_SMALL:C_TOT] = small.T.astype(BF16)

    @pl.when(i % tpb == 0)
    def _():
        halo_ref[0:SUBLANES, :] = jnp.zeros((SUBLANES, halo_ref.shape[1]), F32)

    mod = mod_ref[0]
    u = (x_ref[...] * (1.0 + mod[1:2, :]) + mod[0:1, :]).astype(BF16)

    def proj(c0, c1):
        return jnp.dot(u, w_ref[:, c0:c1], preferred_element_type=F32)

    p = proj(C_QK, C_QK + 2 * M_W)
    halo_ref[SUBLANES:SUBLANES + tm, :] = p
    acc = bc_ref[...] + wc_ref[CONV_W - 1:CONV_W, :] * p
    for j in range(CONV_W - 1):
        acc = acc + wc_ref[j:j + 1, :] * halo_ref[pl.ds(SUBLANES - (CONV_W - 1) + j, tm), :]
    halo_ref[0:SUBLANES, :] = p[tm - SUBLANES:, :]
    qk = acc * _sigmoid(acc)
    oa_ref[:, C_QK:C_QK + M_W] = qk[:, :M_W].astype(BF16)
    oa_ref[:, C_QK + M_W:C_QK + 2 * M_W] = (qk[:, M_W:] * (M_HD ** -0.5)).astype(BF16)

    ps = proj(C_SMALL, C_TOT)
    la = jnp.dot(ps.astype(BF16), wa_ref[...], preferred_element_type=F32) + ba_ref[...]
    la_ref[...] = _log_sigmoid(la) * (1.0 / G_TAU)
    pt = ps.T
    gi = pt[SM_I:SM_I + SUBLANES, :] + bg_ref[0:SUBLANES, :]
    gf = _log_sigmoid(pt[SM_F:SM_F + SUBLANES, :] + bg_ref[SUBLANES:2 * SUBLANES, :])
    for j in range(tm // lm):
        g_ref[j, 0:SUBLANES, :] = gi[:, j * lm:(j + 1) * lm]
        g_ref[j, SUBLANES:2 * SUBLANES, :] = gf[:, j * lm:(j + 1) * lm]

    p = proj(C_VO, C_VO + 2 * M_W)
    oa_ref[:, C_VO:C_VO + 2 * M_W] = p.astype(BF16)

    p = proj(C_GQK, C_GQK + G_KW)
    oa_ref[:, C_GQK:C_GQK + G_KW] = (p * (G_DK ** -0.5)).astype(BF16)
    p = proj(C_GQK + G_KW, C_SMALL)
    oa_ref[:, C_GQK + G_KW:C_SMALL] = p.astype(BF16)


def _inproj(x2, mod3, w_in, w_conv, b_conv, wa_pad, b_gla, bg, *, S, tm, lm, layer):
    N, D = x2.shape
    tpb = S // tm
    kern = functools.partial(_inproj_kernel, tm=tm, tpb=tpb, lm=lm)
    return pl.pallas_call(
        kern,
        grid=(N // tm,),
        in_specs=[pl.BlockSpec((tm, D), lambda i: (i, 0)),
                  pl.BlockSpec((1, 6, D), lambda i: (i // tpb, 0, 0)),
                  pl.BlockSpec((1, IN_TOT, D), lambda i: (layer, 0, 0), pipeline_mode=pl.Buffered(1)),
                  pl.BlockSpec((CONV_W, 2 * M_W), lambda i: (0, 0)),
                  pl.BlockSpec((1, 2 * M_W), lambda i: (0, 0)),
                  pl.BlockSpec((LANES, G_KW), lambda i: (0, 0)),
                  pl.BlockSpec((1, G_KW), lambda i: (0, 0)),
                  pl.BlockSpec((2 * SUBLANES, 1), lambda i: (0, 0))],
        out_specs=[pl.BlockSpec((tm, C_SMALL), lambda i: (i, 0)),
                   pl.BlockSpec((tm, G_KW), lambda i: (i, 0)),
                   pl.BlockSpec((tm // lm, 2 * SUBLANES, lm), lambda i: (i, 0, 0))],
        out_shape=[jax.ShapeDtypeStruct((N, C_SMALL), BF16),
                   jax.ShapeDtypeStruct((N, G_KW), F32),
                   jax.ShapeDtypeStruct((N // lm, 2 * SUBLANES, lm), F32)],
        scratch_shapes=[pltpu.VMEM((SUBLANES + tm, 2 * M_W), F32), pltpu.VMEM((D, C_TOT), BF16)],
        compiler_params=_cparams(),
        name="inproj",
    )(x2, mod3, w_in, w_conv, b_conv, wa_pad, b_gla, bg)


def _mlstm_sel():
    sel = np.zeros((2 * LANES, 2 * M_HEADS * M_HD), np.float32)
    for j in range(2 * M_HEADS):
        src = (SUBLANES if j < M_HEADS else 3 * SUBLANES) + j % M_HEADS
        sel[src, M_HD * j:M_HD * (j + 1)] = 1.0
        sel[LANES + src, M_HD * j:M_HD * (j + 1)] = 1.0
    return sel


def _mlstm_kernel(qk_ref, vo_ref, g_ref, u_ref, gain_ref, sel_ref, out_ref, c_ref, zt_ref, a_ref, dec_ref, m_ref,
                  *, L, NC, nb):
    @pl.when(pl.program_id(1) == 0)
    def _():
        c_ref[...] = jnp.zeros_like(c_ref)
        m_ref[...] = jnp.zeros_like(m_ref)

    tril = (lax.broadcasted_iota(jnp.int32, (L, L), 0) >= lax.broadcasted_iota(jnp.int32, (L, L), 1))
    ones_v = jnp.ones((L, M_HD), BF16)
    zpad = jnp.zeros((LANES - 4 * SUBLANES, L), F32)

    order = [(bi, c) for bi in range(nb) for c in range(NC)]
    f_all = jnp.concatenate([g_ref[bi, c, SUBLANES:2 * SUBLANES, :] for bi, c in order], axis=0)
    i_all = jnp.concatenate([g_ref[bi, c, 0:SUBLANES, :] for bi, c in order], axis=0)
    b_all = jnp.dot(f_all, u_ref[...], preferred_element_type=F32, precision=HIGHEST)
    a_all = i_all - b_all
    lane_all = lax.broadcasted_iota(jnp.int32, a_all.shape, 1)
    g_all = a_all
    s = 1
    while s < L:
        g_all = jnp.maximum(g_all, jnp.where(lane_all >= s, pltpu.roll(g_all, s, 1), -jnp.inf))
        s *= 2
    for bi in range(nb):
        m_prev = m_ref[bi][:, 0:1]
        for c in range(NC):
            ci = bi * NC + c
            r8 = slice(SUBLANES * ci, SUBLANES * (ci + 1))
            a, b = a_all[r8], b_all[r8]
            a_ref[ci] = a
            M = jnp.maximum(g_all[r8], m_prev)
            ML = M[:, L - 1:L]
            Z = jnp.concatenate([M, jnp.exp(m_prev - M), jnp.exp(-(b + M)), jnp.exp(a - ML), zpad],
                                axis=0)
            zt_ref[ci] = Z.T
            dec_ref[ci] = jnp.broadcast_to(jnp.exp(m_prev - ML), (SUBLANES, 2 * M_HD))
            m_prev = b[:, L - 1:L] + ML
        m_ref[bi] = jnp.broadcast_to(m_prev, (SUBLANES, LANES))

    chains = [(bi, h) for bi in range(nb) for h in range(M_HEADS)]
    nt = (((1,), (1,)), ((), ()))
    tn = (((0,), (0,)), ((), ()))

    def chunk(c, carry):
        rows = pl.ds(pl.multiple_of(c * L, L), L)
        Zt = [zt_ref[bi * NC + c] for bi in range(nb)]
        a = [a_ref[bi * NC + c] for bi in range(nb)]
        dec = [dec_ref[bi * NC + c] for bi in range(nb)]
        hs = [slice(h * M_HD, (h + 1) * M_HD) for h in range(M_HEADS)]
        hs2 = [slice(M_W + h * M_HD, M_W + (h + 1) * M_HD) for h in range(M_HEADS)]
        q = [qk_ref[bi, rows, hs[h]] for bi, h in chains]
        k = [qk_ref[bi, rows, hs2[h]] for bi, h in chains]
        vext = [jnp.concatenate([vo_ref[bi, rows, hs[h]], ones_v], axis=1) for bi, h in chains]
        cst = [c_ref[bi * M_HEADS + h] for bi, h in chains]
        n = range(len(chains))
        sc = [lax.dot_general(q[i], k[i], nt, preferred_element_type=F32) for i in n]
        qc = [jnp.dot(q[i], cst[i].astype(BF16), preferred_element_type=F32) for i in n]
        pm = [(sc[i] * jnp.exp(jnp.where(tril, a[bi][h:h + 1, :] - Zt[bi][:, h:h + 1], -jnp.inf))).astype(BF16)
              for i, (bi, h) in enumerate(chains)]
        pv = [jnp.dot(pm[i], vext[i], preferred_element_type=F32) for i in n]
        rep = []
        for bi in range(nb):
            zh = Zt[bi].astype(BF16)
            zl = (Zt[bi] - zh.astype(F32)).astype(BF16)
            rep.append(jnp.dot(jnp.concatenate([zh, zl], axis=1), sel_ref[...], preferred_element_type=F32))
        e_inter = [rep[bi][:, M_HD * h:M_HD * (h + 1)] for bi, h in chains]
        w_state = [rep[bi][:, M_HD * (M_HEADS + h):M_HD * (M_HEADS + h + 1)] for bi, h in chains]
        kw = [(w_state[i] * k[i].astype(F32)).astype(BF16) for i in n]
        upd = [lax.dot_general(kw[i], vext[i], tn, preferred_element_type=F32) for i in n]
        for i, (bi, h) in enumerate(chains):
            c_ref[bi * M_HEADS + h] = dec[bi][h:h + 1, :] * cst[i] + upd[i]
            nd = pv[i] + jnp.concatenate([e_inter[i], e_inter[i]], axis=1) * qc[i]
            hh = nd[:, :M_HD] / jnp.maximum(jnp.abs(nd[:, M_HD:]),
                                            Zt[bi][:, 2 * SUBLANES + h:2 * SUBLANES + h + 1])
            hh = _sigmoid(vo_ref[bi, rows, hs2[h]].astype(F32)) * hh
            hn = hh * lax.rsqrt(jnp.mean(hh * hh, axis=-1, keepdims=True) + LN_EPS)
            out_ref[bi, rows, hs[h]] = (hn * gain_ref[:, hs[h]]).astype(BF16)
        return carry

    lax.fori_loop(0, NC, chunk, 0)


def _mlstm(oa, g3, u_tri, gain, *, B, S, L, nb, ts):
    N = oa.shape[0]
    NC = ts // L
    oa3 = oa.reshape(B, S, oa.shape[1])
    g4 = g3.reshape(B, S // L, 2 * SUBLANES, L)
    sel = jnp.asarray(_mlstm_sel(), BF16)
    kern = functools.partial(_mlstm_kernel, L=L, NC=NC, nb=nb)
    out = pl.pallas_call(
        kern,
        grid=(B // nb, S // ts),
        in_specs=[pl.BlockSpec((nb, ts, 2 * M_W), lambda b, t: (b, t, C_QK // (2 * M_W))),
                  pl.BlockSpec((nb, ts, 2 * M_W), lambda b, t: (b, t, C_VO // (2 * M_W))),
                  pl.BlockSpec((nb, NC, 2 * SUBLANES, L), lambda b, t: (b, t, 0, 0)),
                  pl.BlockSpec((L, L), lambda b, t: (0, 0)),
                  pl.BlockSpec((1, M_W), lambda b, t: (0, 0)),
                  pl.BlockSpec(sel.shape, lambda b, t: (0, 0))],
        out_specs=pl.BlockSpec((nb, ts, M_W), lambda b, t: (b, t, 0)),
        out_shape=jax.ShapeDtypeStruct((B, S, M_W), BF16),
        scratch_shapes=[pltpu.VMEM((nb * M_HEADS, M_HD, 2 * M_HD), F32),
                        pltpu.VMEM((nb * NC, L, LANES), F32),
                        pltpu.VMEM((nb * NC, SUBLANES, L), F32),
                        pltpu.VMEM((nb * NC, SUBLANES, 2 * M_HD), F32),
                        pltpu.VMEM((nb, SUBLANES, LANES), F32)],
        compiler_params=_cparams(2),
        name="mlstm",
    )(oa3, oa3, g4, u_tri, gain, sel)
    return out.reshape(N, M_W)


_G_LEVELS = 6
_G_XROW = 2 * G_CHUNK + SUBLANES


def _gla_consts():
    L = G_CHUNK
    t = np.arange(L)
    blocks = [(t[None, :] <= t[:, None]).astype(np.float32),
              (t[None, :] > t[:, None]).astype(np.float32),
              np.ones((SUBLANES, L), np.float32)]
    masks = [np.eye(L, dtype=np.float32)]
    m = 1
    while m < L:
        wl = np.zeros((L, L), np.float32)
        for r in range(L):
            r0 = (r // (2 * m)) * 2 * m + m
            if r % (2 * m) >= m:
                wl[r, r0:r + 1] = 1.0
            else:
                wl[r, r + 1:r0] = 1.0
        blocks.append(wl)
        tt, ss = t[:, None], t[None, :]
        masks.append(((tt // (2 * m) == ss // (2 * m)) & (tt % (2 * m) >= m)
                      & (ss % (2 * m) < m)).astype(np.float32))
        m *= 2
    w = np.concatenate(blocks, axis=0)
    w3 = np.concatenate([w, w, w], axis=1)
    mk = np.stack([np.concatenate([x] * G_HEADS, axis=0) for x in masks])
    return w3, mk


def _gla_kernel(qk_ref, v_ref, gg_ref, la_ref, w3_ref, mk_ref, gain_ref, out_ref, st_ref, *, NC, nb):
    L = G_CHUNK

    @pl.when(pl.program_id(1) == 0)
    def _():
        st_ref[...] = jnp.zeros_like(st_ref)

    lane_head = lax.broadcasted_iota(jnp.int32, (L, G_KW), 1) // G_DK
    br = lax.broadcasted_iota(jnp.int32, (2 * G_DV, LANES), 0) < G_DV
    bl = lax.broadcasted_iota(jnp.int32, (2 * G_DV, LANES), 1) < G_DK
    bmask = br == bl
    nt = (((1,), (1,)), ((), ()))
    tn = (((0,), (0,)), ((), ()))

    def chunk(c, carry):
        rows = pl.ds(pl.multiple_of(c * L, L), L)
        X, q, k = [], [], []
        for bi in range(nb):
            la = la_ref[bi, rows, :]
            hi = la.astype(BF16)
            r1 = la - hi.astype(F32)
            mid = r1.astype(BF16)
            lo = (r1 - mid.astype(F32)).astype(BF16)
            stk = jnp.concatenate([hi, mid, lo], axis=0)
            X.append(jnp.exp(jnp.dot(w3_ref[...], stk, preferred_element_type=F32)))
            q.append(qk_ref[bi, rows, 0:G_KW].astype(F32))
            k.append(qk_ref[bi, rows, G_KW:2 * G_KW].astype(F32))

        sc = [[None] * (_G_LEVELS + 1) for _ in range(nb)]
        for lev in range(_G_LEVELS + 1):
            for bi in range(nb):
                if lev == 0:
                    qt, kt = q[bi], k[bi]
                else:
                    xl = X[bi][_G_XROW + L * (lev - 1):_G_XROW + L * lev, :]
                    qt, kt = q[bi] * xl, k[bi] * xl
                q4 = jnp.concatenate([jnp.where(lane_head == h, qt, 0.0) for h in range(G_HEADS)],
                                     axis=0).astype(BF16)
                sc[bi][lev] = lax.dot_general(q4, kt.astype(BF16), nt, preferred_element_type=F32)
        Ab = []
        for bi in range(nb):
            A = sc[bi][0] * mk_ref[0]
            for lev in range(1, _G_LEVELS + 1):
                A = A + sc[bi][lev] * mk_ref[lev]
            Ab.append(A.astype(BF16))

        for bi in range(nb):
            gg = gg_ref[bi, rows, :].astype(F32)
            gate = gg * _sigmoid(gg)
            for p in range(2):
                ls = slice(LANES * p, LANES * (p + 1))
                vp = v_ref[bi, rows, 2 * G_DV * p:2 * G_DV * (p + 1)]
                oi = [jnp.dot(Ab[bi][L * (2 * p + hh):L * (2 * p + hh + 1)],
                              vp[:, G_DV * hh:G_DV * (hh + 1)], preferred_element_type=F32)
                      for hh in range(2)]
                st = st_ref[bi, p]
                qc = (q[bi][:, ls] * X[bi][0:L, ls]).astype(BF16)
                o_inter = lax.dot_general(qc, st.astype(BF16), nt, preferred_element_type=F32)
                kc = (k[bi][:, ls] * X[bi][L:2 * L, ls]).astype(BF16)
                upd = lax.dot_general(vp, kc, tn, preferred_element_type=F32)
                dec = X[bi][2 * L:2 * L + 1, ls]
                st_ref[bi, p] = jnp.where(bmask, dec * st + upd, 0.0)
                for hh in range(2):
                    o = o_inter[:, G_DV * hh:G_DV * (hh + 1)] + oi[hh]
                    hn = o * lax.rsqrt(jnp.mean(o * o, axis=-1, keepdims=True) + LN_EPS)
                    hs = slice(G_DV * (2 * p + hh), G_DV * (2 * p + hh + 1))
                    out_ref[bi, rows, hs] = (hn * gain_ref[:, hs] * gate[:, hs]).astype(BF16)
        return carry

    lax.fori_loop(0, NC, chunk, 0)


def _gla(oa, la, w3, mk, gain, *, B, S, nb, ts):
    N = oa.shape[0]
    oa3 = oa.reshape(B, S, oa.shape[1])
    la3 = la.reshape(B, S, G_KW)
    kern = functools.partial(_gla_kernel, NC=ts // G_CHUNK, nb=nb)
    out = pl.pallas_call(
        kern,
        grid=(B // nb, S // ts),
        in_specs=[pl.BlockSpec((nb, ts, 2 * G_KW), lambda b, t: (b, t, C_GQK // (2 * G_KW))),
                  pl.BlockSpec((nb, ts, G_W), lambda b, t: (b, t, C_GV // G_W)),
                  pl.BlockSpec((nb, ts, G_W), lambda b, t: (b, t, C_GG // G_W)),
                  pl.BlockSpec((nb, ts, G_KW), lambda b, t: (b, t, 0)),
                  pl.BlockSpec(w3.shape, lambda b, t: (0, 0)),
                  pl.BlockSpec(mk.shape, lambda b, t: (0, 0, 0)),
                  pl.BlockSpec((1, G_W), lambda b, t: (0, 0))],
        out_specs=pl.BlockSpec((nb, ts, G_W), lambda b, t: (b, t, 0)),
        out_shape=jax.ShapeDtypeStruct((B, S, G_W), BF16),
        scratch_shapes=[pltpu.VMEM((nb, 2, 2 * G_DV, LANES), F32)],
        compiler_params=_cparams(2),
        name="gla",
    )(oa3, oa3, oa3, la3, w3, mk, gain)
    return out.reshape(N, G_W)


def _layer_norm(z, g, b):
    mu = jnp.mean(z, axis=-1, keepdims=True)
    zc = z - mu
    var = jnp.mean(zc * zc, axis=-1, keepdims=True)
    return zc * lax.rsqrt(var + LN_EPS) * g + b


def _outproj_kernel(hm_ref, hg_ref, wf_ref, x_ref, mod_ref, g_ref, b_ref, wrg_ref, wre_ref, br_ref,
                    x1_ref, u2_ref, rrow_ref, w_ref, wr_ref, *, tb, nh):
    @pl.when(pl.program_id(0) == 0)
    def _():
        w_ref[...] = wf_ref[...].astype(BF16)
        z = lambda n: jnp.zeros((n, wrg_ref.shape[2]), F32)
        wt = jnp.concatenate([wrg_ref[0], z(SUBLANES - N_GROUPS), wre_ref[0],
                              z(LANES - SUBLANES - N_EXP)], axis=0).T
        hi = wt.astype(BF16)
        wr_ref[:, 0:LANES] = hi
        wr_ref[:, LANES:2 * LANES] = (wt - hi.astype(F32)).astype(BF16)

    mod = mod_ref[0]
    blocks = [slice(tb * j, tb * (j + 1)) for j in range(nh)]

    def project(j):
        r = blocks[j]
        return (jnp.dot(hm_ref[r, :], w_ref[0:M_W, :], preferred_element_type=F32)
                + jnp.dot(hg_ref[r, :], w_ref[M_W:M_W + G_W, :], preferred_element_type=F32))

    def norm(j, y):
        r = blocks[j]
        z = ALPHA * x_ref[r, :] + (1.0 + mod[2:3, :]) * y
        x1 = _layer_norm(z, g_ref[...], b_ref[...])
        x1_ref[r, :] = x1
        u2 = x1 * (1.0 + mod[4:5, :]) + mod[3:4, :]
        u2_ref[r, :] = u2.astype(BF16)
        return u2

    def logits(u2):
        uh = u2.astype(BF16)
        ul = (u2 - uh.astype(F32)).astype(BF16)
        lh = jnp.dot(uh, wr_ref[...], preferred_element_type=F32)
        ll = jnp.dot(ul, wr_ref[:, 0:LANES], preferred_element_type=F32)
        return lh[:, 0:LANES] + lh[:, LANES:2 * LANES] + ll + br_ref[...]

    y, u2, lg = {}, {}, {}
    for step in range(nh + 3):
        if step < nh:
            y[step] = project(step)
        if 0 <= step - 1 < nh:
            u2[step - 1] = norm(step - 1, y.pop(step - 1))
        if 0 <= step - 2 < nh:
            lg[step - 2] = logits(u2.pop(step - 2))
        if 0 <= step - 3 < nh:
            rrow_ref[step - 3] = _route_select(lg.pop(step - 3).T, tb)


def _route_select(lt, tm):
    row = lax.broadcasted_iota(jnp.int32, (SUBLANES, tm), 0)
    gl = jnp.where(row < N_GROUPS, lt[0:SUBLANES, :], -jnp.inf)
    gmax = jnp.max(gl, axis=0, keepdims=True)
    gsel = jnp.min(jnp.where(gl == gmax, row, SUBLANES), axis=0, keepdims=True)
    pg = 1.0 / jnp.sum(jnp.exp(gl - gmax), axis=0, keepdims=True)
    ein = jnp.zeros((SUBLANES, tm), F32)
    for g in range(N_GROUPS):
        ein = jnp.where(gsel == g, lt[SUBLANES * (g + 1):SUBLANES * (g + 2), :], ein)
    v1 = jnp.max(ein, axis=0, keepdims=True)
    i1 = jnp.min(jnp.where(ein == v1, row, SUBLANES), axis=0, keepdims=True)
    rest = jnp.where(row == i1, -jnp.inf, ein)
    v2 = jnp.max(rest, axis=0, keepdims=True)
    i2 = jnp.min(jnp.where(rest == v2, row, SUBLANES), axis=0, keepdims=True)
    t2 = jnp.exp(v2 - v1)
    p1 = 1.0 / (1.0 + t2)
    e0 = (gsel * E_PER_G + i1).astype(F32)
    e1 = (gsel * E_PER_G + i2).astype(F32)
    return jnp.concatenate([e0, e1, pg * p1, pg * (t2 * p1), jnp.zeros((SUBLANES - 4, tm), F32)], axis=0)


def _outproj(hm, hg, w_out, x2, mod3, g, b, wrg_t, wre_t, br, *, S, tb, nh, layer):
    N, D = x2.shape
    tm = tb * nh
    tpb = S // tm
    kern = functools.partial(_outproj_kernel, tb=tb, nh=nh)
    return pl.pallas_call(
        kern,
        grid=(N // tm,),
        in_specs=[pl.BlockSpec((tm, M_W), lambda i: (i, 0)),
                  pl.BlockSpec((tm, G_W), lambda i: (i, 0)),
                  pl.BlockSpec((M_W + G_W, D), lambda i: (0, 0), pipeline_mode=pl.Buffered(1)),
                  pl.BlockSpec((tm, D), lambda i: (i, 0)),
                  pl.BlockSpec((1, 6, D), lambda i: (i // tpb, 0, 0)),
                  pl.BlockSpec((1, D), lambda i: (0, 0)),
                  pl.BlockSpec((1, D), lambda i: (0, 0)),
                  pl.BlockSpec((1, N_GROUPS, D), lambda i: (layer, 0, 0)),
                  pl.BlockSpec((1, N_EXP, D), lambda i: (layer, 0, 0)),
                  pl.BlockSpec((1, LANES), lambda i: (0, 0))],
        out_specs=[pl.BlockSpec((tm, D), lambda i: (i, 0)),
                   pl.BlockSpec((tm, D), lambda i: (i, 0)),
                   pl.BlockSpec((nh, SUBLANES, tb), lambda i: (i, 0, 0))],
        out_shape=[jax.ShapeDtypeStruct((N, D), F32),
                   jax.ShapeDtypeStruct((N, D), BF16),
                   jax.ShapeDtypeStruct((N // tb, SUBLANES, tb), F32)],
        scratch_shapes=[pltpu.VMEM((M_W + G_W, D), BF16), pltpu.VMEM((D, 2 * LANES), BF16)],
        compiler_params=_cparams(),
        name="outproj",
    )(hm, hg, w_out, x2, mod3, g, b, wrg_t, wre_t, br)


def _slots_per_tile(tb):
    worst = 2 * tb + N_EXP * (GRAN - 1)
    return -(-worst // LANES) * LANES


def _ffn_tiles(n_tok, tb):
    worst_rows = 2 * n_tok + (n_tok // tb) * N_EXP * (GRAN - 1)
    return -(-worst_rows // FFN_TM) + N_EXP


def _route_kernel(rr_ref, u_ref, lt_ref, srow_ref, col_ref, gd_ref, meta_ref, mg_ref, part_ref,
                  *, NT, tb, TM):
    iota_e = lax.broadcasted_iota(jnp.int32, (N_EXP, tb), 0).astype(F32)
    glane = lax.broadcasted_iota(jnp.int32, (N_EXP, LANES), 1).astype(F32)
    ltri = lt_ref[...]

    def prefix_e(col):
        return jnp.dot(ltri, jnp.broadcast_to(col, (N_EXP, LANES)),
                       preferred_element_type=F32, precision=HIGHEST)[:, 0:1]

    def p1(j, run8):
        r = rr_ref[j]
        oh0 = jnp.where(iota_e == r[0:1, :], 1.0, 0.0)
        oh1 = jnp.where(iota_e == r[1:2, :], 1.0, 0.0)
        cum0 = jnp.dot(oh0.astype(BF16), u_ref[...], preferred_element_type=F32)
        cum1 = jnp.dot(oh1.astype(BF16), u_ref[...], preferred_element_type=F32)
        c0 = jnp.sum(oh0, axis=1, keepdims=True)
        n8 = jnp.floor((c0 + jnp.sum(oh1, axis=1, keepdims=True) + (GRAN - 1.0)) * (1.0 / GRAN))
        lo8 = prefix_e(n8)
        s0 = jnp.sum(oh0 * (GRAN * lo8 + cum0 - 1.0), axis=0, keepdims=True)
        s1 = jnp.sum(oh1 * (GRAN * lo8 + c0 + cum1 - 1.0), axis=0, keepdims=True)
        info = jnp.concatenate([s0, s1, r[2:4, :], jnp.zeros((SUBLANES - 4, tb), F32)], axis=0)
        srow_ref[j] = info
        col_ref[pl.ds(pl.multiple_of(j * tb, tb), tb), :] = jnp.concatenate(
            [info, jnp.zeros((LANES - SUBLANES, tb), F32)], axis=0).T
        mg = jnp.where((lo8 <= glane) & (glane < lo8 + n8), 1.0, 0.0)
        mg_ref[j] = mg
        part = jnp.sum(mg * (run8 + glane - lo8), axis=0, keepdims=True)
        gcnt = jnp.broadcast_to(jnp.sum(n8, axis=0, keepdims=True), (1, LANES))
        part_ref[j] = jnp.concatenate([part, gcnt, jnp.zeros((SUBLANES - 2, LANES), F32)], axis=0)
        return run8 + n8

    tot8 = lax.fori_loop(0, NT, p1, jnp.zeros((N_EXP, 1), F32), unroll=8 if NT % 8 == 0 else 1)
    seg_t = jnp.floor((tot8 * GRAN + (TM - 1.0)) * (1.0 / TM))
    base_t = prefix_e(seg_t)
    base8 = base_t * (TM // GRAN)
    lane1 = lax.broadcasted_iota(jnp.int32, (1, LANES), 1)

    def p2(j, carry):
        pr = part_ref[j]
        dst = (pr[0:1, :] + jnp.sum(mg_ref[j] * base8, axis=0, keepdims=True)) * GRAN
        gd_ref[j] = jnp.where(lane1 == G_LAST, pr[1:2, :], dst).astype(jnp.int32)
        return carry

    lax.fori_loop(0, NT, p2, 0, unroll=8 if NT % 8 == 0 else 1)
    eye = jnp.where(glane == lax.broadcasted_iota(jnp.int32, (N_EXP, LANES), 0).astype(F32), 1.0, 0.0)
    tail_row = jnp.sum(eye * ((base8 + tot8) * GRAN), axis=0, keepdims=True)
    tail_n8 = jnp.sum(eye * (seg_t * (TM // GRAN) - tot8), axis=0, keepdims=True)
    nv_l = jnp.broadcast_to(jnp.sum(seg_t, axis=0, keepdims=True), (1, LANES))
    gd_ref[NT] = jnp.where(lane1 == G_LAST, nv_l, tail_row).astype(jnp.int32)
    gd_ref[NT + 1] = tail_n8.astype(jnp.int32)
    ti = lax.broadcasted_iota(jnp.int32, (N_EXP, tb), 1).astype(F32)
    te = jnp.sum(jnp.where(base_t <= ti, 1.0, 0.0), axis=0, keepdims=True) - 1.0
    nv = jnp.broadcast_to(jnp.sum(seg_t, axis=0, keepdims=True), (1, tb))
    own = jnp.where((base_t <= ti) & (ti < base_t + seg_t), 1.0, 0.0)
    vr = jnp.sum(own * jnp.clip(tot8 * GRAN - (ti - base_t) * TM, 0.0, TM), axis=0, keepdims=True)
    meta_ref[...] = jnp.concatenate([te, nv, vr, jnp.zeros((SUBLANES - 3, tb), F32)],
                                    axis=0).astype(jnp.int32)


def _route(rrow, u_cnt, ltri, *, TM):
    NT, _, tb = rrow.shape
    kern = functools.partial(_route_kernel, NT=NT, tb=tb, TM=TM)
    full3 = lambda i: (0, 0, 0)
    return pl.pallas_call(
        kern,
        grid=(1,),
        in_specs=[pl.BlockSpec((NT, SUBLANES, tb), full3),
                  pl.BlockSpec((tb, tb), lambda i: (0, 0)),
                  pl.BlockSpec((N_EXP, N_EXP), lambda i: (0, 0))],
        out_specs=[pl.BlockSpec((NT, SUBLANES, tb), full3),
                   pl.BlockSpec((NT * tb, LANES), lambda i: (0, 0)),
                   pl.BlockSpec((NT + 2, 1, LANES), full3),
                   pl.BlockSpec((SUBLANES, tb), lambda i: (0, 0))],
        out_shape=[jax.ShapeDtypeStruct((NT, SUBLANES, tb), F32),
                   jax.ShapeDtypeStruct((NT * tb, LANES), F32),
                   jax.ShapeDtypeStruct((NT + 2, 1, LANES), jnp.int32),
                   jax.ShapeDtypeStruct((SUBLANES, tb), jnp.int32)],
        scratch_shapes=[pltpu.VMEM((NT, N_EXP, LANES), F32), pltpu.VMEM((NT, SUBLANES, LANES), F32)],
        compiler_params=_cparams(),
        name="route",
    )(rrow, u_cnt, ltri)


_HI_MASK = 0xFFFF0000


def _pack_halves(x):
    c = x.shape[1] // 2
    lo = lax.bitcast_convert_type(x[:, :c], U32)
    hi = lax.bitcast_convert_type(x[:, c:], U32)
    return (lo >> 16) | (hi & U32(_HI_MASK))


def _unpack_halves(w):
    lo = lax.bitcast_convert_type(w << 16, F32)
    hi = lax.bitcast_convert_type(w & U32(_HI_MASK), F32)
    return jnp.concatenate([lo, hi], axis=1).astype(BF16)


def _granule_copy(src_ref, src_row, dst_ref, dst_row, sem):
    return pltpu.make_async_copy(src_ref.at[pl.ds(src_row, GRAN), :], dst_ref.at[pl.ds(dst_row, GRAN), :], sem)


def _for_granules(n, body, unroll=4):
    def blk(i, carry):
        for t in range(unroll):
            body(i * unroll + t)
        return carry

    def one(g, carry):
        body(g)
        return carry

    nblk = n // unroll
    lax.fori_loop(0, nblk, blk, 0)
    lax.fori_loop(nblk * unroll, n, one, 0)


def _wait_granules(n, src_ref, dst_ref, sem, n_max):
    b = 1
    while b <= n_max:
        @pl.when((n & b) != 0)
        def _(b=b):
            pltpu.make_async_copy(src_ref.at[pl.ds(0, b * GRAN), :], dst_ref.at[pl.ds(0, b * GRAN), :],
                                  sem).wait()
        b *= 2


def _dispatch_kernel(gd_ref, srow_ref, u_ref, xs_ref, buf, zbuf, sems, *, NT, SL, TM, n_tiles):
    j = pl.program_id(0)
    slot = j % 2
    zsem = sems.at[2]

    def drain(tile, sl):
        _wait_granules(gd_ref[tile, G_LAST], buf.at[sl], xs_ref, sems.at[sl], SL // GRAN)

    def tile_fill(t):
        return pltpu.make_async_copy(zbuf, xs_ref.at[pl.ds(pl.multiple_of(t * TM, TM), TM), :], zsem)

    def zero_fill(wait):
        for e in range(N_EXP):
            n, row0 = gd_ref[NT + 1, e], gd_ref[NT, e]
            b = TM // GRAN // 2
            while b >= 1:
                @pl.when((n & b) != 0)
                def _(b=b, n=n, row0=row0):
                    start = pl.multiple_of(row0 + ((n >> b.bit_length()) << b.bit_length()) * GRAN, GRAN)
                    cp = pltpu.make_async_copy(zbuf.at[pl.ds(0, b * GRAN), :],
                                               xs_ref.at[pl.ds(start, b * GRAN), :], zsem)
                    cp.wait() if wait else cp.start()
                b //= 2

        def zt(t, carry):
            tile_fill(t).wait() if wait else tile_fill(t).start()
            return carry
        lax.fori_loop(gd_ref[NT, G_LAST], n_tiles, zt, 0)

    @pl.when(j == 0)
    def _():
        zbuf[...] = jnp.zeros_like(zbuf)
        zero_fill(False)

    @pl.when(j >= 2)
    def _():
        drain(j - 2, slot)

    s = srow_ref[0]
    rows = lax.broadcasted_iota(jnp.int32, (SL, s.shape[1]), 0).astype(F32)
    m0 = rows == s[0:1, :]
    m1 = rows == s[1:2, :]
    oh = jnp.where(m0 | m1, 1.0, 0.0).astype(BF16)
    dw = u_ref.shape[1] // 2
    buf[slot, :, 0:dw] = _pack_halves(jnp.dot(oh, u_ref[...], preferred_element_type=F32))
    wrow = jnp.sum(jnp.where(m0, s[2:3, :], 0.0) + jnp.where(m1, s[3:4, :], 0.0), axis=1, keepdims=True)
    buf[slot, :, dw:dw + LANES] = lax.bitcast_convert_type(jnp.broadcast_to(wrow, (SL, LANES)), U32)

    def issue(g):
        _granule_copy(buf.at[slot], pl.multiple_of(g * GRAN, GRAN), xs_ref,
                      pl.multiple_of(gd_ref[j, g], GRAN), sems.at[slot]).start()

    _for_granules(gd_ref[j, G_LAST], issue)

    @pl.when(j == NT - 1)
    def _():
        drain(j, slot)
        if NT > 1:
            drain(j - 1, 1 - slot)
        zero_fill(True)


def _dispatch(gd, srow, u2, *, n_tiles, TM):
    N, D = u2.shape
    NT, _, tb = srow.shape
    SL = _slots_per_tile(tb)
    n_rows = n_tiles * TM
    kern = functools.partial(_dispatch_kernel, NT=NT, SL=SL, TM=TM, n_tiles=n_tiles)
    grid_spec = pltpu.PrefetchScalarGridSpec(
        num_scalar_prefetch=1,
        grid=(NT,),
        in_specs=[pl.BlockSpec((1, SUBLANES, tb), lambda j, gd: (j, 0, 0)),
                  pl.BlockSpec((tb, D), lambda j, gd: (j, 0))],
        out_specs=pl.BlockSpec(memory_space=pl.ANY),
        scratch_shapes=[pltpu.VMEM((2, SL, D // 2 + LANES), U32), pltpu.VMEM((TM, D // 2 + LANES), U32),
                        pltpu.SemaphoreType.DMA((3,))],
    )
    return pl.pallas_call(
        kern,
        grid_spec=grid_spec,
        out_shape=jax.ShapeDtypeStruct((n_rows, D // 2 + LANES), U32),
        compiler_params=_cparams(),
        name="dispatch",
    )(gd, srow, u2)


def _ffn_kernel(te_ref, nv_ref, vr_ref, xs_ref, wg_ref, wu_ref, wd_ref, o_ref, wgb, wub, wdb, sg, su, sd, slot_ref,
                sems):
    i = pl.program_id(0)
    nv = nv_ref[0]
    e = te_ref[i]

    def weight_copies(ex, sl):
        return (pltpu.make_async_copy(wg_ref.at[ex], sg.at[sl], sems.at[sl]),
                pltpu.make_async_copy(wu_ref.at[ex], su.at[sl], sems.at[sl]),
                pltpu.make_async_copy(wd_ref.at[ex], sd.at[sl], sems.at[sl]))

    @pl.when(i == 0)
    def _():
        slot_ref[0] = 0
        for cp in weight_copies(e, 0):
            cp.start()

    @pl.when((i < nv) & ((i == 0) | (e != te_ref[jnp.maximum(i - 1, 0)])))
    def _():
        sl = slot_ref[0]
        for cp in weight_copies(e, sl):
            cp.wait()
        wgb[...] = sg[sl].astype(BF16)
        wub[...] = su[sl].astype(BF16)
        wdb[...] = sd[sl].astype(BF16)
        nxt = lax.while_loop(lambda t: (t < nv) & (te_ref[jnp.minimum(t, nv - 1)] == e), lambda t: t + 1, i + 1)

        @pl.when(nxt < nv)
        def _():
            for cp in weight_copies(te_ref[nxt], 1 - sl):
                cp.start()
        slot_ref[0] = 1 - sl

    hm = xs_ref.shape[0] // FFN_SUB
    dw = o_ref.shape[1]

    def swiglu_rows(nsub):
        halves = tuple(slice(hm * j, hm * (j + 1)) for j in range(nsub))
        x = [_unpack_halves(xs_ref[r, 0:dw]) for r in halves]
        gu = [(jnp.dot(x[j], wgb[...], preferred_element_type=F32),
               jnp.dot(x[j], wub[...], preferred_element_type=F32)) for j in range(nsub)]
        h = [(g * _sigmoid(g) * u).astype(BF16) for g, u in gu]
        y = [jnp.dot(h[j], wdb[...], preferred_element_type=F32) for j in range(nsub)]
        for j in range(nsub):
            wt = lax.bitcast_convert_type(xs_ref[halves[j], dw:dw + LANES], F32)
            yw = y[j] * jnp.concatenate([wt] * (2 * dw // LANES), axis=1)
            o_ref[halves[j], :] = _pack_halves(yw.astype(BF16).astype(F32))
        if nsub < FFN_SUB:
            o_ref[hm * nsub:, :] = jnp.zeros((hm * (FFN_SUB - nsub), dw), U32)

    used = vr_ref[i]
    for nsub in range(1, FFN_SUB + 1):
        lo, hi = hm * (nsub - 1), hm * nsub
        pl.when((i < nv) & (used > lo) & ((used <= hi) if nsub < FFN_SUB else True))(
            functools.partial(swiglu_rows, nsub))

    @pl.when(i >= nv_ref[0])
    def _():
        o_ref[...] = jnp.zeros_like(o_ref)


def _ffn(te, nv, vr, xs, wg, wu, wd, *, TM):
    P, XW = xs.shape
    DW = XW - LANES
    D = 2 * DW
    n_tiles = P // TM
    grid_spec = pltpu.PrefetchScalarGridSpec(
        num_scalar_prefetch=3,
        grid=(n_tiles,),
        in_specs=[pl.BlockSpec((TM, XW), lambda i, te, nv, vr: (jnp.maximum(jnp.minimum(i, nv[0] - 1), 0), 0)),
                  pl.BlockSpec(memory_space=pl.ANY),
                  pl.BlockSpec(memory_space=pl.ANY),
                  pl.BlockSpec(memory_space=pl.ANY)],
        out_specs=pl.BlockSpec((TM, DW), lambda i, te, nv, vr: (i, 0)),
        scratch_shapes=[pltpu.VMEM((D, D_EXP), BF16), pltpu.VMEM((D, D_EXP), BF16),
                        pltpu.VMEM((D_EXP, D), BF16),
                        pltpu.VMEM((2, D, D_EXP), F32), pltpu.VMEM((2, D, D_EXP), F32),
                        pltpu.VMEM((2, D_EXP, D), F32), pltpu.SMEM((1,), jnp.int32),
                        pltpu.SemaphoreType.DMA((2,))],
    )
    return pl.pallas_call(
        _ffn_kernel,
        grid_spec=grid_spec,
        out_shape=jax.ShapeDtypeStruct((P, DW), U32),
        compiler_params=_cparams(),
        name="ffn",
    )(te, nv, vr, xs, wg, wu, wd)


def _combine_kernel(gd_ref, ys_ref, col_ref, x1_ref, mod_ref, g_ref, b_ref, o_ref, buf, sems, *, NT, SL):
    j = pl.program_id(0)
    slot = j % 2

    def fetch(tile, sl):
        def f(g):
            _granule_copy(ys_ref, pl.multiple_of(gd_ref[tile, g], GRAN), buf.at[sl],
                          pl.multiple_of(g * GRAN, GRAN), sems.at[sl]).start()
        _for_granules(gd_ref[tile, G_LAST], f)

    @pl.when(j == 0)
    def _():
        fetch(0, 0)

    @pl.when(j + 1 < NT)
    def _():
        fetch(j + 1, 1 - slot)

    ng = gd_ref[j, G_LAST]

    _wait_granules(ng, ys_ref, buf.at[slot], sems.at[slot], SL // GRAN)

    rows = lax.broadcasted_iota(jnp.int32, (SL, 1), 0)
    yb = _unpack_halves(jnp.where(rows < ng * GRAN, buf[slot], U32(0)))
    col = col_ref[...]
    tb = col.shape[0]
    lanes = lax.broadcasted_iota(jnp.int32, (tb, SL), 1).astype(F32)
    sel = jnp.where((lanes == col[:, 0:1]) | (lanes == col[:, 1:2]), 1.0, 0.0).astype(BF16)
    y = jnp.dot(sel, yb, preferred_element_type=F32)
    mod = mod_ref[0]
    z = ALPHA * x1_ref[...] + (1.0 + mod[5:6, :]) * y
    o_ref[...] = _layer_norm(z, g_ref[...], b_ref[...])


def _combine(gd, ys, col, x1, mod3, g, b, *, S, tb):
    N, D = x1.shape
    NT = N // tb
    tpb = S // tb
    SL = _slots_per_tile(tb)
    kern = functools.partial(_combine_kernel, NT=NT, SL=SL)
    grid_spec = pltpu.PrefetchScalarGridSpec(
        num_scalar_prefetch=1,
        grid=(NT,),
        in_specs=[pl.BlockSpec(memory_space=pl.ANY),
                  pl.BlockSpec((tb, LANES), lambda j, gd: (j, 0)),
                  pl.BlockSpec((tb, D), lambda j, gd: (j, 0)),
                  pl.BlockSpec((1, 6, D), lambda j, gd: (j // tpb, 0, 0)),
                  pl.BlockSpec((1, D), lambda j, gd: (0, 0)),
                  pl.BlockSpec((1, D), lambda j, gd: (0, 0))],
        out_specs=pl.BlockSpec((tb, D), lambda j, gd: (j, 0)),
        scratch_shapes=[pltpu.VMEM((2, SL, D // 2), U32), pltpu.SemaphoreType.DMA((2,))],
    )
    return pl.pallas_call(
        kern,
        grid_spec=grid_spec,
        out_shape=jax.ShapeDtypeStruct((N, D), F32),
        compiler_params=_cparams(),
        name="combine",
    )(gd, ys, col, x1, mod3, g, b)


def _layer(x, c, l, w_ada, b_ada, w_in, w_conv, b_conv, b_igate, b_fgate, mlstm_norm_g, w_gla_a, b_gla_a,
           gla_norm_g, w_out, ln1_g, ln1_b, w_route_group, b_route_group, w_route_expert, b_route_expert,
           w_gate, w_up, w_down, ln2_g, ln2_b):
    B, S, D = x.shape
    N = B * S
    x2 = x.reshape(N, D)
    tm_in = min(512, S)
    tm = min(256, S)
    lm = min(256, S)
    assert S % tm_in == 0 and S % tm == 0 and tm_in % lm == 0 and S % G_CHUNK == 0
    assert w_in.shape[1:] == (D, IN_TOT) and w_gate.shape[1:] == (N_EXP, D, D_EXP)

    mod3 = _ada(c, w_ada[l], b_ada[l]).reshape(B, 6, D)

    wa_pad = jnp.zeros((LANES, G_KW), F32).at[SM_A:SM_A + G_RANK].set(w_gla_a[l]).astype(BF16)
    bg = (jnp.zeros((2 * SUBLANES, 1), F32).at[0:M_HEADS, 0].set(b_igate[l])
          .at[SUBLANES:SUBLANES + M_HEADS, 0].set(b_fgate[l]))
    oa, la, g3 = _inproj(x2, mod3, jnp.swapaxes(w_in, 1, 2), w_conv[l], b_conv[l].reshape(1, -1), wa_pad,
                         b_gla_a[l].reshape(1, -1), bg, S=S, tm=tm_in, lm=lm, layer=l)

    u_tri = jnp.asarray(np.triu(np.ones((lm, lm), np.float32)))
    nb = 4 if B % 4 == 0 else (2 if B % 2 == 0 else 1)
    ts = min(512, S)
    hm = _mlstm(oa, g3, u_tri, mlstm_norm_g[l].reshape(1, -1), B=B, S=S, L=lm, nb=nb, ts=ts)
    w3_np, mk_np = _gla_consts()
    hg = _gla(oa, la, jnp.asarray(w3_np, BF16), jnp.asarray(mk_np), gla_norm_g[l].reshape(1, -1), B=B, S=S,
              nb=nb, ts=ts)

    br = (jnp.zeros((1, LANES), F32).at[0, 0:N_GROUPS].set(b_route_group[l])
          .at[0, SUBLANES:SUBLANES + N_EXP].set(b_route_expert[l]))
    x1, u2, rrow = _outproj(hm, hg, w_out[l], x2, mod3, ln1_g[l].reshape(1, -1), ln1_b[l].reshape(1, -1),
                            jnp.swapaxes(w_route_group, 1, 2), jnp.swapaxes(w_route_expert, 1, 2), br,
                            S=S, tb=tm, nh=4 if S % (4 * tm) == 0 else 1, layer=l)

    u_cnt = jnp.asarray(np.triu(np.ones((tm, tm), np.float32)), BF16)
    ltri = jnp.asarray(np.tril(np.ones((N_EXP, N_EXP), np.float32), -1))
    srow, col, gd3, meta = _route(rrow, u_cnt, ltri, TM=FFN_TM)
    gd = gd3.reshape(N // tm + 2, LANES)
    n_tiles = _ffn_tiles(N, tm)
    te, nv, vr = meta[0, :n_tiles], meta[1, 0:1], meta[2, :n_tiles]

    xs = _dispatch(gd, srow, u2, n_tiles=n_tiles, TM=FFN_TM)
    ys = _ffn(te, nv, vr, xs, w_gate[l], w_up[l], w_down[l], TM=FFN_TM)
    out = _combine(gd, ys, col, x1, mod3, ln2_g[l].reshape(1, -1), ln2_b[l].reshape(1, -1), S=S, tb=tm)
    return out.reshape(B, S, D)


def kernel(x, c, w_ada, b_ada, w_in, w_conv, b_conv, b_igate, b_fgate, mlstm_norm_g, w_gla_a, b_gla_a,
           gla_norm_g, w_out, ln1_g, ln1_b, w_route_group, b_route_group, w_route_expert, b_route_expert,
           w_gate, w_up, w_down, ln2_g, ln2_b):
    for l in range(DEPTH):
        x = _layer(x, c, l, w_ada, b_ada, w_in, w_conv, b_conv, b_igate, b_fgate, mlstm_norm_g, w_gla_a,
                   b_gla_a, gla_norm_g, w_out, ln1_g, ln1_b, w_route_group, b_route_group, w_route_expert,
                   b_route_expert, w_gate, w_up, w_down, ln2_g, ln2_b)
    return x
```

```python
import functools

import numpy as np
import jax
import jax.numpy as jnp
from jax import lax
from jax.experimental import pallas as pl
from jax.experimental.pallas import tpu as pltpu

F32 = jnp.float32
BF16 = jnp.bfloat16
U32 = jnp.uint32
HIGHEST = lax.Precision.HIGHEST

DEPTH = 1
M_HEADS = 4
M_HD = 128
M_W = M_HEADS * M_HD
CONV_W = 4
G_HEADS = 4
G_DK = 64
G_DV = 128
G_W = G_HEADS * G_DV
G_KW = G_HEADS * G_DK
G_RANK = 16
G_TAU = 16.0
G_CHUNK = 64
N_GROUPS = 4
E_PER_G = 8
N_EXP = N_GROUPS * E_PER_G
D_EXP = 512
ALPHA = (2 * DEPTH) ** 0.25
LN_EPS = 1e-5

LANES = 128
SUBLANES = 8
VMEM_LIMIT = 48 * 1024 * 1024

C_QK = 0
C_VO = 1024
C_GQK = 2048
C_GV = 2560
C_GG = 3072
C_SMALL = 3584
C_TOT = 3712
SM_I, SM_F, SM_A = 0, 8, 16
IN_GATES = 4 * M_W
IN_G = IN_GATES + 2 * M_HEADS
IN_GA = IN_G + 2 * G_KW + 2 * G_W
IN_TOT = IN_GA + G_RANK

FFN_TM = 512
FFN_SUB = 2
GRAN = SUBLANES
G_LAST = LANES - 1


def _cparams(n_axes=1):
    return pltpu.CompilerParams(dimension_semantics=("arbitrary",) * n_axes,
                                vmem_limit_bytes=VMEM_LIMIT)


def _sigmoid(x):
    return 1.0 / (1.0 + jnp.exp(-x))


def _log_sigmoid(x):
    return jnp.minimum(x, 0.0) - jnp.log(1.0 + jnp.exp(-jnp.abs(x)))


def _ada_kernel(c_ref, w_ref, b_ref, o_ref):
    c = c_ref[...]
    ca = (c * _sigmoid(c)).astype(BF16)
    o_ref[...] = jnp.dot(ca, w_ref[...].astype(BF16), preferred_element_type=F32) + b_ref[...]


def _ada(c, w, b):
    B, D = c.shape
    n_out = w.shape[1]
    tn = 1024
    return pl.pallas_call(
        _ada_kernel,
        grid=(n_out // tn,),
        in_specs=[pl.BlockSpec((B, D), lambda j: (0, 0)),
                  pl.BlockSpec((D, tn), lambda j: (0, j)),
                  pl.BlockSpec((1, tn), lambda j: (0, j))],
        out_specs=pl.BlockSpec((B, tn), lambda j: (0, j)),
        out_shape=jax.ShapeDtypeStruct((B, n_out), F32),
        compiler_params=_cparams(),
        name="ada",
    )(c, w, b.reshape(1, n_out))


def _inproj_kernel(x_ref, mod_ref, win_ref, wc_ref, bc_ref, wa_ref, ba_ref, bg_ref,
                   oa_ref, la_ref, g_ref, halo_ref, w_ref, *, tm, tpb, lm):
    i = pl.program_id(0)

    @pl.when(i == 0)
    def _():
        rc = 2 * LANES
        for r in range(0, IN_GATES, rc):
            w_ref[:, r:r + rc] = win_ref[0, r:r + rc, :].T.astype(BF16)
        for r in range(0, C_SMALL - C_GQK, rc):
            w_ref[:, C_GQK + r:C_GQK + r + rc] = win_ref[0, IN_G + r:IN_G + r + rc, :].T.astype(BF16)
        gates = win_ref[0, IN_GATES:IN_G, :]
        z = lambda n: jnp.zeros((n, gates.shape[1]), F32)
        small = jnp.concatenate([gates[0:M_HEADS], z(SM_F - M_HEADS), gates[M_HEADS:2 * M_HEADS],
                                 z(SM_A - SM_F - M_HEADS), win_ref[0, IN_GA:IN_TOT, :],
                                 z(LANES - SM_A - G_RANK)], axis=0)
        w_ref[:, C_SMALL:C_TOT] = small.T.astype(BF16)

    @pl.when(i % tpb == 0)
    def _():
        halo_ref[0:SUBLANES, :] = jnp.zeros((SUBLANES, halo_ref.shape[1]), F32)

    mod = mod_ref[0]
    u = (x_ref[...] * (1.0 + mod[1:2, :]) + mod[0:1, :]).astype(BF16)

    def proj(c0, c1):
        return jnp.dot(u, w_ref[:, c0:c1], preferred_element_type=F32)

    p = proj(C_QK, C_QK + 2 * M_W)
    halo_ref[SUBLANES:SUBLANES + tm, :] = p
    acc = bc_ref[...] + wc_ref[CONV_W - 1:CONV_W, :] * p
    for j in range(CONV_W - 1):
        acc = acc + wc_ref[j:j + 1, :] * halo_ref[pl.ds(SUBLANES - (CONV_W - 1) + j, tm), :]
    halo_ref[0:SUBLANES, :] = p[tm - SUBLANES:, :]
    qk = acc * _sigmoid(acc)
    oa_ref[:, C_QK:C_QK + M_W] = qk[:, :M_W].astype(BF16)
    oa_ref[:, C_QK + M_W:C_QK + 2 * M_W] = (qk[:, M_W:] * (M_HD ** -0.5)).astype(BF16)

    ps = proj(C_SMALL, C_TOT)
    la = jnp.dot(ps.astype(BF16), wa_ref[...], preferred_element_type=F32) + ba_ref[...]
    la_ref[...] = _log_sigmoid(la) * (1.0 / G_TAU)
    pt = ps.T
    gi = pt[SM_I:SM_I + SUBLANES, :] + bg_ref[0:SUBLANES, :]
    gf = _log_sigmoid(pt[SM_F:SM_F + SUBLANES, :] + bg_ref[SUBLANES:2 * SUBLANES, :])
    for j in range(tm // lm):
        g_ref[j, 0:SUBLANES, :] = gi[:, j * lm:(j + 1) * lm]
        g_ref[j, SUBLANES:2 * SUBLANES, :] = gf[:, j * lm:(j + 1) * lm]

    p = proj(C_VO, C_VO + 2 * M_W)
    oa_ref[:, C_VO:C_VO + 2 * M_W] = p.astype(BF16)

    p = proj(C_GQK, C_GQK + G_KW)
    oa_ref[:, C_GQK:C_GQK + G_KW] = (p * (G_DK ** -0.5)).astype(BF16)
    p = proj(C_GQK + G_KW, C_SMALL)
    oa_ref[:, C_GQK + G_KW:C_SMALL] = p.astype(BF16)


def _inproj(x2, mod3, w_in, w_conv, b_conv, wa_pad, b_gla, bg, *, S, tm, lm, layer):
    N, D = x2.shape
    tpb = S // tm
    kern = functools.partial(_inproj_kernel, tm=tm, tpb=tpb, lm=lm)
    return pl.pallas_call(
        kern,
        grid=(N // tm,),
        in_specs=[pl.BlockSpec((tm, D), lambda i: (i, 0)),
                  pl.BlockSpec((1, 6, D), lambda i: (i // tpb, 0, 0)),
                  pl.BlockSpec((1, IN_TOT, D), lambda i: (layer, 0, 0), pipeline_mode=pl.Buffered(1)),
                  pl.BlockSpec((CONV_W, 2 * M_W), lambda i: (0, 0)),
                  pl.BlockSpec((1, 2 * M_W), lambda i: (0, 0)),
                  pl.BlockSpec((LANES, G_KW), lambda i: (0, 0)),
                  pl.BlockSpec((1, G_KW), lambda i: (0, 0)),
                  pl.BlockSpec((2 * SUBLANES, 1), lambda i: (0, 0))],
        out_specs=[pl.BlockSpec((tm, C_SMALL), lambda i: (i, 0)),
                   pl.BlockSpec((tm, G_KW), lambda i: (i, 0)),
                   pl.BlockSpec((tm // lm, 2 * SUBLANES, lm), lambda i: (i, 0, 0))],
        out_shape=[jax.ShapeDtypeStruct((N, C_SMALL), BF16),
                   jax.ShapeDtypeStruct((N, G_KW), F32),
                   jax.ShapeDtypeStruct((N // lm, 2 * SUBLANES, lm), F32)],
        scratch_shapes=[pltpu.VMEM((SUBLANES + tm, 2 * M_W), F32), pltpu.VMEM((D, C_TOT), BF16)],
        compiler_params=_cparams(),
        name="inproj",
    )(x2, mod3, w_in, w_conv, b_conv, wa_pad, b_gla, bg)


def _mlstm_sel():
    sel = np.zeros((2 * LANES, 2 * M_HEADS * M_HD), np.float32)
    for j in range(2 * M_HEADS):
        src = (SUBLANES if j < M_HEADS else 3 * SUBLANES) + j % M_HEADS
        sel[src, M_HD * j:M_HD * (j + 1)] = 1.0
        sel[LANES + src, M_HD * j:M_HD * (j + 1)] = 1.0
    return sel


def _mlstm_kernel(qk_ref, vo_ref, g_ref, u_ref, gain_ref, sel_ref, out_ref, c_ref, zt_ref, a_ref, dec_ref, m_ref,
                  *, L, NC, nb):
    @pl.when(pl.program_id(1) == 0)
    def _():
        c_ref[...] = jnp.zeros_like(c_ref)
        m_ref[...] = jnp.zeros_like(m_ref)

    tril = (lax.broadcasted_iota(jnp.int32, (L, L), 0) >= lax.broadcasted_iota(jnp.int32, (L, L), 1))
    ones_v = jnp.ones((L, M_HD), BF16)
    zpad = jnp.zeros((LANES - 4 * SUBLANES, L), F32)

    order = [(bi, c) for bi in range(nb) for c in range(NC)]
    f_all = jnp.concatenate([g_ref[bi, c, SUBLANES:2 * SUBLANES, :] for bi, c in order], axis=0)
    i_all = jnp.concatenate([g_ref[bi, c, 0:SUBLANES, :] for bi, c in order], axis=0)
    b_all = jnp.dot(f_all, u_ref[...], preferred_element_type=F32, precision=HIGHEST)
    a_all = i_all - b_all
    lane_all = lax.broadcasted_iota(jnp.int32, a_all.shape, 1)
    g_all = a_all
    s = 1
    while s < L:
        g_all = jnp.maximum(g_all, jnp.where(lane_all >= s, pltpu.roll(g_all, s, 1), -jnp.inf))
        s *= 2
    for bi in range(nb):
        m_prev = m_ref[bi][:, 0:1]
        for c in range(NC):
            ci = bi * NC + c
            r8 = slice(SUBLANES * ci, SUBLANES * (ci + 1))
            a, b = a_all[r8], b_all[r8]
            a_ref[ci] = a
            M = jnp.maximum(g_all[r8], m_prev)
            ML = M[:, L - 1:L]
            Z = jnp.concatenate([M, jnp.exp(m_prev - M), jnp.exp(-(b + M)), jnp.exp(a - ML), zpad],
                                axis=0)
            zt_ref[ci] = Z.T
            dec_ref[ci] = jnp.broadcast_to(jnp.exp(m_prev - ML), (SUBLANES, 2 * M_HD))
            m_prev = b[:, L - 1:L] + ML
        m_ref[bi] = jnp.broadcast_to(m_prev, (SUBLANES, LANES))

    chains = [(bi, h) for bi in range(nb) for h in range(M_HEADS)]
    nt = (((1,), (1,)), ((), ()))
    tn = (((0,), (0,)), ((), ()))

    def chunk(c, carry):
        rows = pl.ds(pl.multiple_of(c * L, L), L)
        Zt = [zt_ref[bi * NC + c] for bi in range(nb)]
        a = [a_ref[bi * NC + c] for bi in range(nb)]
        dec = [dec_ref[bi * NC + c] for bi in range(nb)]
        hs = [slice(h * M_HD, (h + 1) * M_HD) for h in range(M_HEADS)]
        hs2 = [slice(M_W + h * M_HD, M_W + (h + 1) * M_HD) for h in range(M_HEADS)]
        q = [qk_ref[bi, rows, hs[h]] for bi, h in chains]
        k = [qk_ref[bi, rows, hs2[h]] for bi, h in chains]
        vext = [jnp.concatenate([vo_ref[bi, rows, hs[h]], ones_v], axis=1) for bi, h in chains]
        cst = [c_ref[bi * M_HEADS + h] for bi, h in chains]
        n = range(len(chains))
        sc = [lax.dot_general(q[i], k[i], nt, preferred_element_type=F32) for i in n]
        qc = [jnp.dot(q[i], cst[i].astype(BF16), preferred_element_type=F32) for i in n]
        pm = [(sc[i] * jnp.exp(jnp.where(tril, a[bi][h:h + 1, :] - Zt[bi][:, h:h + 1], -jnp.inf))).astype(BF16)
              for i, (bi, h) in enumerate(chains)]
        pv = [jnp.dot(pm[i], vext[i], preferred_element_type=F32) for i in n]
        rep = []
        for bi in range(nb):
            zh = Zt[bi].astype(BF16)
            zl = (Zt[bi] - zh.astype(F32)).astype(BF16)
            rep.append(jnp.dot(jnp.concatenate([zh, zl], axis=1), sel_ref[...], preferred_element_type=F32))
        e_inter = [rep[bi][:, M_HD * h:M_HD * (h + 1)] for bi, h in chains]
        w_state = [rep[bi][:, M_HD * (M_HEADS + h):M_HD * (M_HEADS + h + 1)] for bi, h in chains]
        for i, (bi, h) in enumerate(chains):
            nd = pv[i] + jnp.concatenate([e_inter[i], e_inter[i]], axis=1) * qc[i]
            hh = nd[:, :M_HD] / jnp.maximum(jnp.abs(nd[:, M_HD:]),
                                            Zt[bi][:, 2 * SUBLANES + h:2 * SUBLANES + h + 1])
            hh = _sigmoid(vo_ref[bi, rows, hs2[h]].astype(F32)) * hh
            hn = hh * lax.rsqrt(jnp.mean(hh * hh, axis=-1, keepdims=True) + LN_EPS)
            out_ref[bi, rows, hs[h]] = (hn * gain_ref[:, hs[h]]).astype(BF16)
        kw = [(w_state[i] * k[i].astype(F32)).astype(BF16) for i in n]
        upd = [lax.dot_general(kw[i], vext[i], tn, preferred_element_type=F32) for i in n]
        for i, (bi, h) in enumerate(chains):
            c_ref[bi * M_HEADS + h] = dec[bi][h:h + 1, :] * cst[i] + upd[i]
        return carry

    lax.fori_loop(0, NC, chunk, 0)


def _mlstm(oa, g3, u_tri, gain, *, B, S, L, nb, ts):
    N = oa.shape[0]
    NC = ts // L
    oa3 = oa.reshape(B, S, oa.shape[1])
    g4 = g3.reshape(B, S // L, 2 * SUBLANES, L)
    sel = jnp.asarray(_mlstm_sel(), BF16)
    kern = functools.partial(_mlstm_kernel, L=L, NC=NC, nb=nb)
    out = pl.pallas_call(
        kern,
        grid=(B // nb, S // ts),
        in_specs=[pl.BlockSpec((nb, ts, 2 * M_W), lambda b, t: (b, t, C_QK // (2 * M_W))),
                  pl.BlockSpec((nb, ts, 2 * M_W), lambda b, t: (b, t, C_VO // (2 * M_W))),
                  pl.BlockSpec((nb, NC, 2 * SUBLANES, L), lambda b, t: (b, t, 0, 0)),
                  pl.BlockSpec((L, L), lambda b, t: (0, 0)),
                  pl.BlockSpec((1, M_W), lambda b, t: (0, 0)),
                  pl.BlockSpec(sel.shape, lambda b, t: (0, 0))],
        out_specs=pl.BlockSpec((nb, ts, M_W), lambda b, t: (b, t, 0)),
        out_shape=jax.ShapeDtypeStruct((B, S, M_W), BF16),
        scratch_shapes=[pltpu.VMEM((nb * M_HEADS, M_HD, 2 * M_HD), F32),
                        pltpu.VMEM((nb * NC, L, LANES), F32),
                        pltpu.VMEM((nb * NC, SUBLANES, L), F32),
                        pltpu.VMEM((nb * NC, SUBLANES, 2 * M_HD), F32),
                        pltpu.VMEM((nb, SUBLANES, LANES), F32)],
        compiler_params=_cparams(2),
        name="mlstm",
    )(oa3, oa3, g4, u_tri, gain, sel)
    return out.reshape(N, M_W)


_G_LEVELS = 6
_G_XROW = 2 * G_CHUNK + SUBLANES


def _gla_consts():
    L = G_CHUNK
    t = np.arange(L)
    blocks = [(t[None, :] <= t[:, None]).astype(np.float32),
              (t[None, :] > t[:, None]).astype(np.float32),
              np.ones((SUBLANES, L), np.float32)]
    masks = [np.eye(L, dtype=np.float32)]
    m = 1
    while m < L:
        wl = np.zeros((L, L), np.float32)
        for r in range(L):
            r0 = (r // (2 * m)) * 2 * m + m
            if r % (2 * m) >= m:
                wl[r, r0:r + 1] = 1.0
            else:
                wl[r, r + 1:r0] = 1.0
        blocks.append(wl)
        tt, ss = t[:, None], t[None, :]
        masks.append(((tt // (2 * m) == ss // (2 * m)) & (tt % (2 * m) >= m)
                      & (ss % (2 * m) < m)).astype(np.float32))
        m *= 2
    w = np.concatenate(blocks, axis=0)
    w3 = np.concatenate([w, w, w], axis=1)
    mk = np.stack([np.concatenate([x] * G_HEADS, axis=0) for x in masks])
    return w3, mk


def _gla_kernel(qk_ref, v_ref, gg_ref, la_ref, w3_ref, mk_ref, gain_ref, out_ref, st_ref, *, NC, nb):
    L = G_CHUNK

    @pl.when(pl.program_id(1) == 0)
    def _():
        st_ref[...] = jnp.zeros_like(st_ref)

    lane_head = lax.broadcasted_iota(jnp.int32, (L, G_KW), 1) // G_DK
    br = lax.broadcasted_iota(jnp.int32, (2 * G_DV, LANES), 0) < G_DV
    bl = lax.broadcasted_iota(jnp.int32, (2 * G_DV, LANES), 1) < G_DK
    bmask = br == bl
    nt = (((1,), (1,)), ((), ()))
    tn = (((0,), (0,)), ((), ()))

    def chunk(c, carry):
        rows = pl.ds(pl.multiple_of(c * L, L), L)
        X, q, k = [], [], []
        for bi in range(nb):
            la = la_ref[bi, rows, :]
            hi = la.astype(BF16)
            r1 = la - hi.astype(F32)
            mid = r1.astype(BF16)
            lo = (r1 - mid.astype(F32)).astype(BF16)
            stk = jnp.concatenate([hi, mid, lo], axis=0)
            X.append(jnp.exp(jnp.dot(w3_ref[...], stk, preferred_element_type=F32)))
            q.append(qk_ref[bi, rows, 0:G_KW].astype(F32))
            k.append(qk_ref[bi, rows, G_KW:2 * G_KW].astype(F32))

        sc = [[None] * (_G_LEVELS + 1) for _ in range(nb)]
        for lev in range(_G_LEVELS + 1):
            for bi in range(nb):
                if lev == 0:
                    qt, kt = q[bi], k[bi]
                else:
                    xl = X[bi][_G_XROW + L * (lev - 1):_G_XROW + L * lev, :]
                    qt, kt = q[bi] * xl, k[bi] * xl
                q4 = jnp.concatenate([jnp.where(lane_head == h, qt, 0.0) for h in range(G_HEADS)],
                                     axis=0).astype(BF16)
                sc[bi][lev] = lax.dot_general(q4, kt.astype(BF16), nt, preferred_element_type=F32)
        Ab = []
        for bi in range(nb):
            A = sc[bi][0] * mk_ref[0]
            for lev in range(1, _G_LEVELS + 1):
                A = A + sc[bi][lev] * mk_ref[lev]
            Ab.append(A.astype(BF16))

        pairs = [(bi, p) for bi in range(nb) for p in range(2)]
        vps, sts = {}, {}
        for bi, p in pairs:
            ls = slice(LANES * p, LANES * (p + 1))
            vps[bi, p] = v_ref[bi, rows, 2 * G_DV * p:2 * G_DV * (p + 1)]
            sts[bi, p] = st_ref[bi, p]
        for bi in range(nb):
            gg = gg_ref[bi, rows, :].astype(F32)
            gate = gg * _sigmoid(gg)
            for p in range(2):
                ls = slice(LANES * p, LANES * (p + 1))
                vp = vps[bi, p]
                oi = [jnp.dot(Ab[bi][L * (2 * p + hh):L * (2 * p + hh + 1)],
                              vp[:, G_DV * hh:G_DV * (hh + 1)], preferred_element_type=F32)
                      for hh in range(2)]
                qc = (q[bi][:, ls] * X[bi][0:L, ls]).astype(BF16)
                o_inter = lax.dot_general(qc, sts[bi, p].astype(BF16), nt, preferred_element_type=F32)
                for hh in range(2):
                    o = o_inter[:, G_DV * hh:G_DV * (hh + 1)] + oi[hh]
                    hn = o * lax.rsqrt(jnp.mean(o * o, axis=-1, keepdims=True) + LN_EPS)
                    hs = slice(G_DV * (2 * p + hh), G_DV * (2 * p + hh + 1))
                    out_ref[bi, rows, hs] = (hn * gain_ref[:, hs] * gate[:, hs]).astype(BF16)
        for bi, p in pairs:
            ls = slice(LANES * p, LANES * (p + 1))
            kc = (k[bi][:, ls] * X[bi][L:2 * L, ls]).astype(BF16)
            upd = lax.dot_general(vps[bi, p], kc, tn, preferred_element_type=F32)
            dec = X[bi][2 * L:2 * L + 1, ls]
            st_ref[bi, p] = jnp.where(bmask, dec * sts[bi, p] + upd, 0.0)
        return carry

    lax.fori_loop(0, NC, chunk, 0)


def _gla(oa, la, w3, mk, gain, *, B, S, nb, ts):
    N = oa.shape[0]
    oa3 = oa.reshape(B, S, oa.shape[1])
    la3 = la.reshape(B, S, G_KW)
    kern = functools.partial(_gla_kernel, NC=ts // G_CHUNK, nb=nb)
    out = pl.pallas_call(
        kern,
        grid=(B // nb, S // ts),
        in_specs=[pl.BlockSpec((nb, ts, 2 * G_KW), lambda b, t: (b, t, C_GQK // (2 * G_KW))),
                  pl.BlockSpec((nb, ts, G_W), lambda b, t: (b, t, C_GV // G_W)),
                  pl.BlockSpec((nb, ts, G_W), lambda b, t: (b, t, C_GG // G_W)),
                  pl.BlockSpec((nb, ts, G_KW), lambda b, t: (b, t, 0)),
                  pl.BlockSpec(w3.shape, lambda b, t: (0, 0)),
                  pl.BlockSpec(mk.shape, lambda b, t: (0, 0, 0)),
                  pl.BlockSpec((1, G_W), lambda b, t: (0, 0))],
        out_specs=pl.BlockSpec((nb, ts, G_W), lambda b, t: (b, t, 0)),
        out_shape=jax.ShapeDtypeStruct((B, S, G_W), BF16),
        scratch_shapes=[pltpu.VMEM((nb, 2, 2 * G_DV, LANES), F32)],
        compiler_params=_cparams(2),
        name="gla",
    )(oa3, oa3, oa3, la3, w3, mk, gain)
    return out.reshape(N, G_W)


def _layer_norm(z, g, b):
    mu = jnp.mean(z, axis=-1, keepdims=True)
    zc = z - mu
    var = jnp.mean(zc * zc, axis=-1, keepdims=True)
    return zc * lax.rsqrt(var + LN_EPS) * g + b


def _outproj_kernel(hm_ref, hg_ref, wf_ref, x_ref, mod_ref, g_ref, b_ref, wrg_ref, wre_ref, br_ref,
                    x1_ref, u2_ref, rrow_ref, w_ref, wr_ref, *, tb, nh):
    @pl.when(pl.program_id(0) == 0)
    def _():
        w_ref[...] = wf_ref[...].astype(BF16)
        z = lambda n: jnp.zeros((n, wrg_ref.shape[2]), F32)
        wt = jnp.concatenate([wrg_ref[0], z(SUBLANES - N_GROUPS), wre_ref[0],
                              z(LANES - SUBLANES - N_EXP)], axis=0).T
        hi = wt.astype(BF16)
        wr_ref[:, 0:LANES] = hi
        wr_ref[:, LANES:2 * LANES] = (wt - hi.astype(F32)).astype(BF16)

    mod = mod_ref[0]
    blocks = [slice(tb * j, tb * (j + 1)) for j in range(nh)]
    y = [jnp.dot(hm_ref[r, :], w_ref[0:M_W, :], preferred_element_type=F32)
         + jnp.dot(hg_ref[r, :], w_ref[M_W:M_W + G_W, :], preferred_element_type=F32) for r in blocks]
    u2 = []
    for j, r in enumerate(blocks):
        z = ALPHA * x_ref[r, :] + (1.0 + mod[2:3, :]) * y[j]
        x1 = _layer_norm(z, g_ref[...], b_ref[...])
        x1_ref[r, :] = x1
        u2.append(x1 * (1.0 + mod[4:5, :]) + mod[3:4, :])
        u2_ref[r, :] = u2[j].astype(BF16)

    u2h = [u.astype(BF16) for u in u2]
    u2l = [(u2[j] - u2h[j].astype(F32)).astype(BF16) for j in range(nh)]
    lh = [jnp.dot(u, wr_ref[...], preferred_element_type=F32) for u in u2h]
    ll = [jnp.dot(u, wr_ref[:, 0:LANES], preferred_element_type=F32) for u in u2l]
    for j in range(nh):
        logits = lh[j][:, 0:LANES] + lh[j][:, LANES:2 * LANES] + ll[j] + br_ref[...]
        rrow_ref[j] = _route_select(logits.T, tb)


def _route_select(lt, tm):
    row = lax.broadcasted_iota(jnp.int32, (SUBLANES, tm), 0)
    gl = jnp.where(row < N_GROUPS, lt[0:SUBLANES, :], -jnp.inf)
    gmax = jnp.max(gl, axis=0, keepdims=True)
    gsel = jnp.min(jnp.where(gl == gmax, row, SUBLANES), axis=0, keepdims=True)
    pg = 1.0 / jnp.sum(jnp.exp(gl - gmax), axis=0, keepdims=True)
    ein = jnp.zeros((SUBLANES, tm), F32)
    for g in range(N_GROUPS):
        ein = jnp.where(gsel == g, lt[SUBLANES * (g + 1):SUBLANES * (g + 2), :], ein)
    v1 = jnp.max(ein, axis=0, keepdims=True)
    i1 = jnp.min(jnp.where(ein == v1, row, SUBLANES), axis=0, keepdims=True)
    rest = jnp.where(row == i1, -jnp.inf, ein)
    v2 = jnp.max(rest, axis=0, keepdims=True)
    i2 = jnp.min(jnp.where(rest == v2, row, SUBLANES), axis=0, keepdims=True)
    t2 = jnp.exp(v2 - v1)
    p1 = 1.0 / (1.0 + t2)
    e0 = (gsel * E_PER_G + i1).astype(F32)
    e1 = (gsel * E_PER_G + i2).astype(F32)
    return jnp.concatenate([e0, e1, pg * p1, pg * (t2 * p1), jnp.zeros((SUBLANES - 4, tm), F32)], axis=0)


def _outproj(hm, hg, w_out, x2, mod3, g, b, wrg_t, wre_t, br, *, S, tb, nh, layer):
    N, D = x2.shape
    tm = tb * nh
    tpb = S // tm
    kern = functools.partial(_outproj_kernel, tb=tb, nh=nh)
    return pl.pallas_call(
        kern,
        grid=(N // tm,),
        in_specs=[pl.BlockSpec((tm, M_W), lambda i: (i, 0)),
                  pl.BlockSpec((tm, G_W), lambda i: (i, 0)),
                  pl.BlockSpec((M_W + G_W, D), lambda i: (0, 0), pipeline_mode=pl.Buffered(1)),
                  pl.BlockSpec((tm, D), lambda i: (i, 0)),
                  pl.BlockSpec((1, 6, D), lambda i: (i // tpb, 0, 0)),
                  pl.BlockSpec((1, D), lambda i: (0, 0)),
                  pl.BlockSpec((1, D), lambda i: (0, 0)),
                  pl.BlockSpec((1, N_GROUPS, D), lambda i: (layer, 0, 0)),
                  pl.BlockSpec((1, N_EXP, D), lambda i: (layer, 0, 0)),
                  pl.BlockSpec((1, LANES), lambda i: (0, 0))],
        out_specs=[pl.BlockSpec((tm, D), lambda i: (i, 0)),
                   pl.BlockSpec((tm, D), lambda i: (i, 0)),
                   pl.BlockSpec((nh, SUBLANES, tb), lambda i: (i, 0, 0))],
        out_shape=[jax.ShapeDtypeStruct((N, D), F32),
                   jax.ShapeDtypeStruct((N, D), BF16),
                   jax.ShapeDtypeStruct((N // tb, SUBLANES, tb), F32)],
        scratch_shapes=[pltpu.VMEM((M_W + G_W, D), BF16), pltpu.VMEM((D, 2 * LANES), BF16)],
        compiler_params=_cparams(),
        name="outproj",
    )(hm, hg, w_out, x2, mod3, g, b, wrg_t, wre_t, br)


def _slots_per_tile(tb):
    worst = 2 * tb + N_EXP * (GRAN - 1)
    return -(-worst // LANES) * LANES


def _ffn_tiles(n_tok, tb):
    worst_rows = 2 * n_tok + (n_tok // tb) * N_EXP * (GRAN - 1)
    return -(-worst_rows // FFN_TM) + N_EXP


def _route_kernel(rr_ref, u_ref, lt_ref, srow_ref, col_ref, gd_ref, meta_ref, mg_ref, part_ref,
                  *, NT, tb, TM):
    iota_e = lax.broadcasted_iota(jnp.int32, (N_EXP, tb), 0).astype(F32)
    glane = lax.broadcasted_iota(jnp.int32, (N_EXP, LANES), 1).astype(F32)
    ltri = lt_ref[...]

    def prefix_e(col):
        return jnp.dot(ltri, jnp.broadcast_to(col, (N_EXP, LANES)),
                       preferred_element_type=F32, precision=HIGHEST)[:, 0:1]

    def p1(j, run8):
        r = rr_ref[j]
        oh0 = jnp.where(iota_e == r[0:1, :], 1.0, 0.0)
        oh1 = jnp.where(iota_e == r[1:2, :], 1.0, 0.0)
        cum0 = jnp.dot(oh0.astype(BF16), u_ref[...], preferred_element_type=F32)
        cum1 = jnp.dot(oh1.astype(BF16), u_ref[...], preferred_element_type=F32)
        c0 = jnp.sum(oh0, axis=1, keepdims=True)
        n8 = jnp.floor((c0 + jnp.sum(oh1, axis=1, keepdims=True) + (GRAN - 1.0)) * (1.0 / GRAN))
        lo8 = prefix_e(n8)
        s0 = jnp.sum(oh0 * (GRAN * lo8 + cum0 - 1.0), axis=0, keepdims=True)
        s1 = jnp.sum(oh1 * (GRAN * lo8 + c0 + cum1 - 1.0), axis=0, keepdims=True)
        info = jnp.concatenate([s0, s1, r[2:4, :], jnp.zeros((SUBLANES - 4, tb), F32)], axis=0)
        srow_ref[j] = info
        col_ref[pl.ds(pl.multiple_of(j * tb, tb), tb), :] = jnp.concatenate(
            [info, jnp.zeros((LANES - SUBLANES, tb), F32)], axis=0).T
        mg = jnp.where((lo8 <= glane) & (glane < lo8 + n8), 1.0, 0.0)
        mg_ref[j] = mg
        part = jnp.sum(mg * (run8 + glane - lo8), axis=0, keepdims=True)
        gcnt = jnp.broadcast_to(jnp.sum(n8, axis=0, keepdims=True), (1, LANES))
        part_ref[j] = jnp.concatenate([part, gcnt, jnp.zeros((SUBLANES - 2, LANES), F32)], axis=0)
        return run8 + n8

    tot8 = lax.fori_loop(0, NT, p1, jnp.zeros((N_EXP, 1), F32), unroll=8 if NT % 8 == 0 else 1)
    seg_t = jnp.floor((tot8 * GRAN + (TM - 1.0)) * (1.0 / TM))
    base_t = prefix_e(seg_t)
    base8 = base_t * (TM // GRAN)
    lane1 = lax.broadcasted_iota(jnp.int32, (1, LANES), 1)

    def p2(j, carry):
        pr = part_ref[j]
        dst = (pr[0:1, :] + jnp.sum(mg_ref[j] * base8, axis=0, keepdims=True)) * GRAN
        gd_ref[j] = jnp.where(lane1 == G_LAST, pr[1:2, :], dst).astype(jnp.int32)
        return carry

    lax.fori_loop(0, NT, p2, 0, unroll=8 if NT % 8 == 0 else 1)
    eye = jnp.where(glane == lax.broadcasted_iota(jnp.int32, (N_EXP, LANES), 0).astype(F32), 1.0, 0.0)
    tail_row = jnp.sum(eye * ((base8 + tot8) * GRAN), axis=0, keepdims=True)
    tail_n8 = jnp.sum(eye * (seg_t * (TM // GRAN) - tot8), axis=0, keepdims=True)
    nv_l = jnp.broadcast_to(jnp.sum(seg_t, axis=0, keepdims=True), (1, LANES))
    gd_ref[NT] = jnp.where(lane1 == G_LAST, nv_l, tail_row).astype(jnp.int32)
    gd_ref[NT + 1] = tail_n8.astype(jnp.int32)
    ti = lax.broadcasted_iota(jnp.int32, (N_EXP, tb), 1).astype(F32)
    te = jnp.sum(jnp.where(base_t <= ti, 1.0, 0.0), axis=0, keepdims=True) - 1.0
    nv = jnp.broadcast_to(jnp.sum(seg_t, axis=0, keepdims=True), (1, tb))
    own = jnp.where((base_t <= ti) & (ti < base_t + seg_t), 1.0, 0.0)
    vr = jnp.sum(own * jnp.clip(tot8 * GRAN - (ti - base_t) * TM, 0.0, TM), axis=0, keepdims=True)
    meta_ref[...] = jnp.concatenate([te, nv, vr, jnp.zeros((SUBLANES - 3, tb), F32)],
                                    axis=0).astype(jnp.int32)


def _route(rrow, u_cnt, ltri, *, TM):
    NT, _, tb = rrow.shape
    kern = functools.partial(_route_kernel, NT=NT, tb=tb, TM=TM)
    full3 = lambda i: (0, 0, 0)
    return pl.pallas_call(
        kern,
        grid=(1,),
        in_specs=[pl.BlockSpec((NT, SUBLANES, tb), full3),
                  pl.BlockSpec((tb, tb), lambda i: (0, 0)),
                  pl.BlockSpec((N_EXP, N_EXP), lambda i: (0, 0))],
        out_specs=[pl.BlockSpec((NT, SUBLANES, tb), full3),
                   pl.BlockSpec((NT * tb, LANES), lambda i: (0, 0)),
                   pl.BlockSpec((NT + 2, 1, LANES), full3),
                   pl.BlockSpec((SUBLANES, tb), lambda i: (0, 0))],
        out_shape=[jax.ShapeDtypeStruct((NT, SUBLANES, tb), F32),
                   jax.ShapeDtypeStruct((NT * tb, LANES), F32),
                   jax.ShapeDtypeStruct((NT + 2, 1, LANES), jnp.int32),
                   jax.ShapeDtypeStruct((SUBLANES, tb), jnp.int32)],
        scratch_shapes=[pltpu.VMEM((NT, N_EXP, LANES), F32), pltpu.VMEM((NT, SUBLANES, LANES), F32)],
        compiler_params=_cparams(),
        name="route",
    )(rrow, u_cnt, ltri)


_HI_MASK = 0xFFFF0000


def _pack_halves(x):
    c = x.shape[1] // 2
    lo = lax.bitcast_convert_type(x[:, :c], U32)
    hi = lax.bitcast_convert_type(x[:, c:], U32)
    return (lo >> 16) | (hi & U32(_HI_MASK))


def _unpack_halves(w):
    lo = lax.bitcast_convert_type(w << 16, F32)
    hi = lax.bitcast_convert_type(w & U32(_HI_MASK), F32)
    return jnp.concatenate([lo, hi], axis=1).astype(BF16)


def _granule_copy(src_ref, src_row, dst_ref, dst_row, sem):
    return pltpu.make_async_copy(src_ref.at[pl.ds(src_row, GRAN), :], dst_ref.at[pl.ds(dst_row, GRAN), :], sem)


def _for_granules(n, body, unroll=4):
    def blk(i, carry):
        for t in range(unroll):
            body(i * unroll + t)
        return carry

    def one(g, carry):
        body(g)
        return carry

    nblk = n // unroll
    lax.fori_loop(0, nblk, blk, 0)
    lax.fori_loop(nblk * unroll, n, one, 0)


def _wait_granules(n, src_ref, dst_ref, sem, n_max):
    b = 1
    while b <= n_max:
        @pl.when((n & b) != 0)
        def _(b=b):
            pltpu.make_async_copy(src_ref.at[pl.ds(0, b * GRAN), :], dst_ref.at[pl.ds(0, b * GRAN), :],
                                  sem).wait()
        b *= 2


def _dispatch_kernel(gd_ref, srow_ref, u_ref, xs_ref, buf, zbuf, sems, *, NT, SL, TM, n_tiles):
    j = pl.program_id(0)
    slot = j % 2
    zsem = sems.at[2]

    def drain(tile, sl):
        _wait_granules(gd_ref[tile, G_LAST], buf.at[sl], xs_ref, sems.at[sl], SL // GRAN)

    def tile_fill(t):
        return pltpu.make_async_copy(zbuf, xs_ref.at[pl.ds(pl.multiple_of(t * TM, TM), TM), :], zsem)

    def zero_fill(wait):
        for e in range(N_EXP):
            n, row0 = gd_ref[NT + 1, e], gd_ref[NT, e]
            b = TM // GRAN // 2
            while b >= 1:
                @pl.when((n & b) != 0)
                def _(b=b, n=n, row0=row0):
                    start = pl.multiple_of(row0 + ((n >> b.bit_length()) << b.bit_length()) * GRAN, GRAN)
                    cp = pltpu.make_async_copy(zbuf.at[pl.ds(0, b * GRAN), :],
                                               xs_ref.at[pl.ds(start, b * GRAN), :], zsem)
                    cp.wait() if wait else cp.start()
                b //= 2

        def zt(t, carry):
            tile_fill(t).wait() if wait else tile_fill(t).start()
            return carry
        lax.fori_loop(gd_ref[NT, G_LAST], n_tiles, zt, 0)

    @pl.when(j == 0)
    def _():
        zbuf[...] = jnp.zeros_like(zbuf)
        zero_fill(False)

    @pl.when(j >= 2)
    def _():
        drain(j - 2, slot)

    s = srow_ref[0]
    rows = lax.broadcasted_iota(jnp.int32, (SL, s.shape[1]), 0).astype(F32)
    m0 = rows == s[0:1, :]
    m1 = rows == s[1:2, :]
    oh = jnp.where(m0 | m1, 1.0, 0.0).astype(BF16)
    dw = u_ref.shape[1] // 2
    buf[slot, :, 0:dw] = _pack_halves(jnp.dot(oh, u_ref[...], preferred_element_type=F32))
    wrow = jnp.sum(jnp.where(m0, s[2:3, :], 0.0) + jnp.where(m1, s[3:4, :], 0.0), axis=1, keepdims=True)
    buf[slot, :, dw:dw + LANES] = lax.bitcast_convert_type(jnp.broadcast_to(wrow, (SL, LANES)), U32)

    def issue(g):
        _granule_copy(buf.at[slot], pl.multiple_of(g * GRAN, GRAN), xs_ref,
                      pl.multiple_of(gd_ref[j, g], GRAN), sems.at[slot]).start()

    _for_granules(gd_ref[j, G_LAST], issue)

    @pl.when(j == NT - 1)
    def _():
        drain(j, slot)
        if NT > 1:
            drain(j - 1, 1 - slot)
        zero_fill(True)


def _dispatch(gd, srow, u2, *, n_tiles, TM):
    N, D = u2.shape
    NT, _, tb = srow.shape
    SL = _slots_per_tile(tb)
    n_rows = n_tiles * TM
    kern = functools.partial(_dispatch_kernel, NT=NT, SL=SL, TM=TM, n_tiles=n_tiles)
    grid_spec = pltpu.PrefetchScalarGridSpec(
        num_scalar_prefetch=1,
        grid=(NT,),
        in_specs=[pl.BlockSpec((1, SUBLANES, tb), lambda j, gd: (j, 0, 0)),
                  pl.BlockSpec((tb, D), lambda j, gd: (j, 0))],
        out_specs=pl.BlockSpec(memory_space=pl.ANY),
        scratch_shapes=[pltpu.VMEM((2, SL, D // 2 + LANES), U32), pltpu.VMEM((TM, D // 2 + LANES), U32),
                        pltpu.SemaphoreType.DMA((3,))],
    )
    return pl.pallas_call(
        kern,
        grid_spec=grid_spec,
        out_shape=jax.ShapeDtypeStruct((n_rows, D // 2 + LANES), U32),
        compiler_params=_cparams(),
        name="dispatch",
    )(gd, srow, u2)


def _ffn_kernel(te_ref, nv_ref, vr_ref, xs_ref, wg_ref, wu_ref, wd_ref, o_ref, wgb, wub, wdb, sg, su, sd, slot_ref,
                sems):
    i = pl.program_id(0)
    nv = nv_ref[0]
    e = te_ref[i]

    def weight_copies(ex, sl):
        return (pltpu.make_async_copy(wg_ref.at[ex], sg.at[sl], sems.at[sl]),
                pltpu.make_async_copy(wu_ref.at[ex], su.at[sl], sems.at[sl]),
                pltpu.make_async_copy(wd_ref.at[ex], sd.at[sl], sems.at[sl]))

    @pl.when(i == 0)
    def _():
        slot_ref[0] = 0
        for cp in weight_copies(e, 0):
            cp.start()

    @pl.when((i < nv) & ((i == 0) | (e != te_ref[jnp.maximum(i - 1, 0)])))
    def _():
        sl = slot_ref[0]
        for cp in weight_copies(e, sl):
            cp.wait()
        wgb[...] = sg[sl].astype(BF16)
        wub[...] = su[sl].astype(BF16)
        wdb[...] = sd[sl].astype(BF16)
        nxt = lax.while_loop(lambda t: (t < nv) & (te_ref[jnp.minimum(t, nv - 1)] == e), lambda t: t + 1, i + 1)

        @pl.when(nxt < nv)
        def _():
            for cp in weight_copies(te_ref[nxt], 1 - sl):
                cp.start()
        slot_ref[0] = 1 - sl

    hm = xs_ref.shape[0] // FFN_SUB
    dw = o_ref.shape[1]

    def swiglu_rows(nsub):
        halves = tuple(slice(hm * j, hm * (j + 1)) for j in range(nsub))
        x = [_unpack_halves(xs_ref[r, 0:dw]) for r in halves]
        g = [jnp.dot(x[j], wgb[...], preferred_element_type=F32) for j in range(nsub)]
        u = [jnp.dot(x[j], wub[...], preferred_element_type=F32) for j in range(nsub)]
        h = [(g[j] * _sigmoid(g[j]) * u[j]).astype(BF16) for j in range(nsub)]
        y = [jnp.dot(h[j], wdb[...], preferred_element_type=F32) for j in range(nsub)]
        for j in range(nsub):
            wt = lax.bitcast_convert_type(xs_ref[halves[j], dw:dw + LANES], F32)
            yw = y[j] * jnp.concatenate([wt] * (2 * dw // LANES), axis=1)
            o_ref[halves[j], :] = _pack_halves(yw.astype(BF16).astype(F32))
        if nsub < FFN_SUB:
            o_ref[hm * nsub:, :] = jnp.zeros((hm * (FFN_SUB - nsub), dw), U32)

    used = vr_ref[i]
    for nsub in range(1, FFN_SUB + 1):
        lo, hi = hm * (nsub - 1), hm * nsub
        pl.when((i < nv) & (used > lo) & ((used <= hi) if nsub < FFN_SUB else True))(
            functools.partial(swiglu_rows, nsub))

    @pl.when(i >= nv_ref[0])
    def _():
        o_ref[...] = jnp.zeros_like(o_ref)


def _ffn(te, nv, vr, xs, wg, wu, wd, *, TM):
    P, XW = xs.shape
    DW = XW - LANES
    D = 2 * DW
    n_tiles = P // TM
    grid_spec = pltpu.PrefetchScalarGridSpec(
        num_scalar_prefetch=3,
        grid=(n_tiles,),
        in_specs=[pl.BlockSpec((TM, XW), lambda i, te, nv, vr: (jnp.maximum(jnp.minimum(i, nv[0] - 1), 0), 0)),
                  pl.BlockSpec(memory_space=pl.ANY),
                  pl.BlockSpec(memory_space=pl.ANY),
                  pl.BlockSpec(memory_space=pl.ANY)],
        out_specs=pl.BlockSpec((TM, DW), lambda i, te, nv, vr: (i, 0)),
        scratch_shapes=[pltpu.VMEM((D, D_EXP), BF16), pltpu.VMEM((D, D_EXP), BF16),
                        pltpu.VMEM((D_EXP, D), BF16),
                        pltpu.VMEM((2, D, D_EXP), F32), pltpu.VMEM((2, D, D_EXP), F32),
                        pltpu.VMEM((2, D_EXP, D), F32), pltpu.SMEM((1,), jnp.int32),
                        pltpu.SemaphoreType.DMA((2,))],
    )
    return pl.pallas_call(
        _ffn_kernel,
        grid_spec=grid_spec,
        out_shape=jax.ShapeDtypeStruct((P, DW), U32),
        compiler_params=_cparams(),
        name="ffn",
    )(te, nv, vr, xs, wg, wu, wd)


def _combine_kernel(gd_ref, ys_ref, col_ref, x1_ref, mod_ref, g_ref, b_ref, o_ref, buf, sems, *, NT, SL):
    j = pl.program_id(0)
    slot = j % 2

    def fetch(tile, sl):
        def f(g):
            _granule_copy(ys_ref, pl.multiple_of(gd_ref[tile, g], GRAN), buf.at[sl],
                          pl.multiple_of(g * GRAN, GRAN), sems.at[sl]).start()
        _for_granules(gd_ref[tile, G_LAST], f)

    @pl.when(j == 0)
    def _():
        fetch(0, 0)

    @pl.when(j + 1 < NT)
    def _():
        fetch(j + 1, 1 - slot)

    ng = gd_ref[j, G_LAST]

    _wait_granules(ng, ys_ref, buf.at[slot], sems.at[slot], SL // GRAN)

    rows = lax.broadcasted_iota(jnp.int32, (SL, 1), 0)
    yb = _unpack_halves(jnp.where(rows < ng * GRAN, buf[slot], U32(0)))
    col = col_ref[...]
    tb = col.shape[0]
    lanes = lax.broadcasted_iota(jnp.int32, (tb, SL), 1).astype(F32)
    sel = jnp.where((lanes == col[:, 0:1]) | (lanes == col[:, 1:2]), 1.0, 0.0).astype(BF16)
    y = jnp.dot(sel, yb, preferred_element_type=F32)
    mod = mod_ref[0]
    z = ALPHA * x1_ref[...] + (1.0 + mod[5:6, :]) * y
    o_ref[...] = _layer_norm(z, g_ref[...], b_ref[...])


def _combine(gd, ys, col, x1, mod3, g, b, *, S, tb):
    N, D = x1.shape
    NT = N // tb
    tpb = S // tb
    SL = _slots_per_tile(tb)
    kern = functools.partial(_combine_kernel, NT=NT, SL=SL)
    grid_spec = pltpu.PrefetchScalarGridSpec(
        num_scalar_prefetch=1,
        grid=(NT,),
        in_specs=[pl.BlockSpec(memory_space=pl.ANY),
                  pl.BlockSpec((tb, LANES), lambda j, gd: (j, 0)),
                  pl.BlockSpec((tb, D), lambda j, gd: (j, 0)),
                  pl.BlockSpec((1, 6, D), lambda j, gd: (j // tpb, 0, 0)),
                  pl.BlockSpec((1, D), lambda j, gd: (0, 0)),
                  pl.BlockSpec((1, D), lambda j, gd: (0, 0))],
        out_specs=pl.BlockSpec((tb, D), lambda j, gd: (j, 0)),
        scratch_shapes=[pltpu.VMEM((2, SL, D // 2), U32), pltpu.SemaphoreType.DMA((2,))],
    )
    return pl.pallas_call(
        kern,
        grid_spec=grid_spec,
        out_shape=jax.ShapeDtypeStruct((N, D), F32),
        compiler_params=_cparams(),
        name="combine",
    )(gd, ys, col, x1, mod3, g, b)


def _layer(x, c, l, w_ada, b_ada, w_in, w_conv, b_conv, b_igate, b_fgate, mlstm_norm_g, w_gla_a, b_gla_a,
           gla_norm_g, w_out, ln1_g, ln1_b, w_route_group, b_route_group, w_route_expert, b_route_expert,
           w_gate, w_up, w_down, ln2_g, ln2_b):
    B, S, D = x.shape
    N = B * S
    x2 = x.reshape(N, D)
    tm_in = min(512, S)
    tm = min(256, S)
    lm = min(256, S)
    assert S % tm_in == 0 and S % tm == 0 and tm_in % lm == 0 and S % G_CHUNK == 0
    assert w_in.shape[1:] == (D, IN_TOT) and w_gate.shape[1:] == (N_EXP, D, D_EXP)

    mod3 = _ada(c, w_ada[l], b_ada[l]).reshape(B, 6, D)

    wa_pad = jnp.zeros((LANES, G_KW), F32).at[SM_A:SM_A + G_RANK].set(w_gla_a[l]).astype(BF16)
    bg = (jnp.zeros((2 * SUBLANES, 1), F32).at[0:M_HEADS, 0].set(b_igate[l])
          .at[SUBLANES:SUBLANES + M_HEADS, 0].set(b_fgate[l]))
    oa, la, g3 = _inproj(x2, mod3, jnp.swapaxes(w_in, 1, 2), w_conv[l], b_conv[l].reshape(1, -1), wa_pad,
                         b_gla_a[l].reshape(1, -1), bg, S=S, tm=tm_in, lm=lm, layer=l)

    u_tri = jnp.asarray(np.triu(np.ones((lm, lm), np.float32)))
    nb = 4 if B % 4 == 0 else (2 if B % 2 == 0 else 1)
    ts = min(512, S)
    hm = _mlstm(oa, g3, u_tri, mlstm_norm_g[l].reshape(1, -1), B=B, S=S, L=lm, nb=nb, ts=ts)
    w3_np, mk_np = _gla_consts()
    hg = _gla(oa, la, jnp.asarray(w3_np, BF16), jnp.asarray(mk_np), gla_norm_g[l].reshape(1, -1), B=B, S=S,
              nb=nb, ts=ts)

    br = (jnp.zeros((1, LANES), F32).at[0, 0:N_GROUPS].set(b_route_group[l])
          .at[0, SUBLANES:SUBLANES + N_EXP].set(b_route_expert[l]))
    x1, u2, rrow = _outproj(hm, hg, w_out[l], x2, mod3, ln1_g[l].reshape(1, -1), ln1_b[l].reshape(1, -1),
                            jnp.swapaxes(w_route_group, 1, 2), jnp.swapaxes(w_route_expert, 1, 2), br,
                            S=S, tb=tm, nh=4 if S % (4 * tm) == 0 else 1, layer=l)

    u_cnt = jnp.asarray(np.triu(np.ones((tm, tm), np.float32)), BF16)
    ltri = jnp.asarray(np.tril(np.ones((N_EXP, N_EXP), np.float32), -1))
    srow, col, gd3, meta = _route(rrow, u_cnt, ltri, TM=FFN_TM)
    gd = gd3.reshape(N // tm + 2, LANES)
    n_tiles = _ffn_tiles(N, tm)
    te, nv, vr = meta[0, :n_tiles], meta[1, 0:1], meta[2, :n_tiles]

    xs = _dispatch(gd, srow, u2, n_tiles=n_tiles, TM=FFN_TM)
    ys = _ffn(te, nv, vr, xs, w_gate[l], w_up[l], w_down[l], TM=FFN_TM)
    out = _combine(gd, ys, col, x1, mod3, ln2_g[l].reshape(1, -1), ln2_b[l].reshape(1, -1), S=S, tb=tm)
    return out.reshape(B, S, D)


def kernel(x, c, w_ada, b_ada, w_in, w_conv, b_conv, b_igate, b_fgate, mlstm_norm_g, w_gla_a, b_gla_a,
           gla_norm_g, w_out, ln1_g, ln1_b, w_route_group, b_route_group, w_route_expert, b_route_expert,
           w_gate, w_up, w_down, ln2_g, ln2_b):
    for l in range(DEPTH):
        x = _layer(x, c, l, w_ada, b_ada, w_in, w_conv, b_conv, b_igate, b_fgate, mlstm_norm_g, w_gla_a,
                   b_gla_a, gla_norm_g, w_out, ln1_g, ln1_b, w_route_group, b_route_group, w_route_expert,
                   b_route_expert, w_gate, w_up, w_down, ln2_g, ln2_b)
    return x
```

```python
import functools

import numpy as np
import jax
import jax.numpy as jnp
from jax import lax
from jax.experimental import pallas as pl
from jax.experimental.pallas import tpu as pltpu

F32 = jnp.float32
BF16 = jnp.bfloat16
U32 = jnp.uint32
HIGHEST = lax.Precision.HIGHEST

DEPTH = 1
M_HEADS = 4
M_HD = 128
M_W = M_HEADS * M_HD
CONV_W = 4
G_HEADS = 4
G_DK = 64
G_DV = 128
G_W = G_HEADS * G_DV
G_KW = G_HEADS * G_DK
G_RANK = 16
G_TAU = 16.0
G_CHUNK = 64
N_GROUPS = 4
E_PER_G = 8
N_EXP = N_GROUPS * E_PER_G
D_EXP = 512
ALPHA = (2 * DEPTH) ** 0.25
LN_EPS = 1e-5

LANES = 128
SUBLANES = 8
VMEM_LIMIT = 48 * 1024 * 1024

C_QK = 0
C_VO = 1024
C_GQK = 2048
C_GV = 2560
C_GG = 3072
C_SMALL = 3584
C_TOT = 3712
SM_I, SM_F, SM_A = 0, 8, 16
IN_GATES = 4 * M_W
IN_G = IN_GATES + 2 * M_HEADS
IN_GA = IN_G + 2 * G_KW + 2 * G_W
IN_TOT = IN_GA + G_RANK

FFN_TM = 512
FFN_SUB = 2
GRAN = SUBLANES
G_LAST = LANES - 1


def _cparams(n_axes=1):
    return pltpu.CompilerParams(dimension_semantics=("arbitrary",) * n_axes,
                                vmem_limit_bytes=VMEM_LIMIT)


def _sigmoid(x):
    return 1.0 / (1.0 + jnp.exp(-x))


def _log_sigmoid(x):
    return jnp.minimum(x, 0.0) - jnp.log(1.0 + jnp.exp(-jnp.abs(x)))


def _ada_kernel(c_ref, w_ref, b_ref, o_ref):
    c = c_ref[...]
    ca = (c * _sigmoid(c)).astype(BF16)
    o_ref[...] = jnp.dot(ca, w_ref[...].astype(BF16), preferred_element_type=F32) + b_ref[...]


def _ada(c, w, b):
    B, D = c.shape
    n_out = w.shape[1]
    tn = 1024
    return pl.pallas_call(
        _ada_kernel,
        grid=(n_out // tn,),
        in_specs=[pl.BlockSpec((B, D), lambda j: (0, 0)),
                  pl.BlockSpec((D, tn), lambda j: (0, j)),
                  pl.BlockSpec((1, tn), lambda j: (0, j))],
        out_specs=pl.BlockSpec((B, tn), lambda j: (0, j)),
        out_shape=jax.ShapeDtypeStruct((B, n_out), F32),
        compiler_params=_cparams(),
        name="ada",
    )(c, w, b.reshape(1, n_out))


def _inproj_kernel(x_ref, mod_ref, win_ref, wc_ref, bc_ref, wa_ref, ba_ref, bg_ref,
                   oa_ref, la_ref, g_ref, halo_ref, w_ref, *, tm, tpb, lm):
    i = pl.program_id(0)

    @pl.when(i == 0)
    def _():
        rc = 2 * LANES
        for r in range(0, IN_GATES, rc):
            w_ref[:, r:r + rc] = win_ref[0, r:r + rc, :].T.astype(BF16)
        for r in range(0, C_SMALL - C_GQK, rc):
            w_ref[:, C_GQK + r:C_GQK + r + rc] = win_ref[0, IN_G + r:IN_G + r + rc, :].T.astype(BF16)
        gates = win_ref[0, IN_GATES:IN_G, :]
        z = lambda n: jnp.zeros((n, gates.shape[1]), F32)
        small = jnp.concatenate([gates[0:M_HEADS], z(SM_F - M_HEADS), gates[M_HEADS:2 * M_HEADS],
                                 z(SM_A - SM_F - M_HEADS), win_ref[0, IN_GA:IN_TOT, :],
                                 z(LANES - SM_A - G_RANK)], axis=0)
        w_ref[:, C_SMALL:C_TOT] = small.T.astype(BF16)

    @pl.when(i % tpb == 0)
    def _():
        halo_ref[0:SUBLANES, :] = jnp.zeros((SUBLANES, halo_ref.shape[1]), F32)

    mod = mod_ref[0]
    u = (x_ref[...] * (1.0 + mod[1:2, :]) + mod[0:1, :]).astype(BF16)

    def proj(c0, c1):
        return jnp.dot(u, w_ref[:, c0:c1], preferred_element_type=F32)

    p = proj(C_QK, C_QK + 2 * M_W)
    halo_ref[SUBLANES:SUBLANES + tm, :] = p
    acc = bc_ref[...] + wc_ref[CONV_W - 1:CONV_W, :] * p
    for j in range(CONV_W - 1):
        acc = acc + wc_ref[j:j + 1, :] * halo_ref[pl.ds(SUBLANES - (CONV_W - 1) + j, tm), :]
    halo_ref[0:SUBLANES, :] = p[tm - SUBLANES:, :]
    qk = acc * _sigmoid(acc)
    oa_ref[:, C_QK:C_QK + M_W] = qk[:, :M_W].astype(BF16)
    oa_ref[:, C_QK + M_W:C_QK + 2 * M_W] = (qk[:, M_W:] * (M_HD ** -0.5)).astype(BF16)

    ps = proj(C_SMALL, C_TOT)
    la = jnp.dot(ps.astype(BF16), wa_ref[...], preferred_element_type=F32) + ba_ref[...]
    la_ref[...] = _log_sigmoid(la) * (1.0 / G_TAU)
    pt = ps.T
    gi = pt[SM_I:SM_I + SUBLANES, :] + bg_ref[0:SUBLANES, :]
    gf = _log_sigmoid(pt[SM_F:SM_F + SUBLANES, :] + bg_ref[SUBLANES:2 * SUBLANES, :])
    for j in range(tm // lm):
        g_ref[j, 0:SUBLANES, :] = gi[:, j * lm:(j + 1) * lm]
        g_ref[j, SUBLANES:2 * SUBLANES, :] = gf[:, j * lm:(j + 1) * lm]

    p = proj(C_VO, C_VO + 2 * M_W)
    oa_ref[:, C_VO:C_VO + 2 * M_W] = p.astype(BF16)

    p = proj(C_GQK, C_GQK + G_KW)
    oa_ref[:, C_GQK:C_GQK + G_KW] = (p * (G_DK ** -0.5)).astype(BF16)
    p = proj(C_GQK + G_KW, C_SMALL)
    oa_ref[:, C_GQK + G_KW:C_SMALL] = p.astype(BF16)


def _inproj(x2, mod3, w_in, w_conv, b_conv, wa_pad, b_gla, bg, *, S, tm, lm, layer):
    N, D = x2.shape
    tpb = S // tm
    kern = functools.partial(_inproj_kernel, tm=tm, tpb=tpb, lm=lm)
    return pl.pallas_call(
        kern,
        grid=(N // tm,),
        in_specs=[pl.BlockSpec((tm, D), lambda i: (i, 0)),
                  pl.BlockSpec((1, 6, D), lambda i: (i // tpb, 0, 0)),
                  pl.BlockSpec((1, IN_TOT, D), lambda i: (layer, 0, 0), pipeline_mode=pl.Buffered(1)),
                  pl.BlockSpec((CONV_W, 2 * M_W), lambda i: (0, 0)),
                  pl.BlockSpec((1, 2 * M_W), lambda i: (0, 0)),
                  pl.BlockSpec((LANES, G_KW), lambda i: (0, 0)),
                  pl.BlockSpec((1, G_KW), lambda i: (0, 0)),
                  pl.BlockSpec((2 * SUBLANES, 1), lambda i: (0, 0))],
        out_specs=[pl.BlockSpec((tm, C_SMALL), lambda i: (i, 0)),
                   pl.BlockSpec((tm, G_KW), lambda i: (i, 0)),
                   pl.BlockSpec((tm // lm, 2 * SUBLANES, lm), lambda i: (i, 0, 0))],
        out_shape=[jax.ShapeDtypeStruct((N, C_SMALL), BF16),
                   jax.ShapeDtypeStruct((N, G_KW), F32),
                   jax.ShapeDtypeStruct((N // lm, 2 * SUBLANES, lm), F32)],
        scratch_shapes=[pltpu.VMEM((SUBLANES + tm, 2 * M_W), F32), pltpu.VMEM((D, C_TOT), BF16)],
        compiler_params=_cparams(),
        name="inproj",
    )(x2, mod3, w_in, w_conv, b_conv, wa_pad, b_gla, bg)


def _mlstm_sel():
    sel = np.zeros((2 * LANES, 2 * M_HEADS * M_HD), np.float32)
    for j in range(2 * M_HEADS):
        src = (SUBLANES if j < M_HEADS else 3 * SUBLANES) + j % M_HEADS
        sel[src, M_HD * j:M_HD * (j + 1)] = 1.0
        sel[LANES + src, M_HD * j:M_HD * (j + 1)] = 1.0
    return sel


def _mlstm_kernel(qk_ref, vo_ref, g_ref, u_ref, gain_ref, sel_ref, out_ref, c_ref, zt_ref, a_ref, dec_ref, m_ref,
                  *, L, NC, nb):
    @pl.when(pl.program_id(1) == 0)
    def _():
        c_ref[...] = jnp.zeros_like(c_ref)
        m_ref[...] = jnp.zeros_like(m_ref)

    tril = (lax.broadcasted_iota(jnp.int32, (L, L), 0) >= lax.broadcasted_iota(jnp.int32, (L, L), 1))
    ones_v = jnp.ones((L, M_HD), BF16)
    zpad = jnp.zeros((LANES - 4 * SUBLANES, L), F32)
    zgroup = lax.broadcasted_iota(jnp.int32, (L, LANES), 1) // SUBLANES
    factor_cols = (zgroup == 1) | (zgroup == 3)

    order = [(bi, c) for bi in range(nb) for c in range(NC)]
    f_all = jnp.concatenate([g_ref[bi, c, SUBLANES:2 * SUBLANES, :] for bi, c in order], axis=0)
    i_all = jnp.concatenate([g_ref[bi, c, 0:SUBLANES, :] for bi, c in order], axis=0)
    b_all = jnp.dot(f_all, u_ref[...], preferred_element_type=F32, precision=HIGHEST)
    a_all = i_all - b_all
    lane_all = lax.broadcasted_iota(jnp.int32, a_all.shape, 1)
    g_all = a_all
    s = 1
    while s < L:
        g_all = jnp.maximum(g_all, jnp.where(lane_all >= s, pltpu.roll(g_all, s, 1), -jnp.inf))
        s *= 2
    for bi in range(nb):
        m_prev = m_ref[bi][:, 0:1]
        for c in range(NC):
            ci = bi * NC + c
            r8 = slice(SUBLANES * ci, SUBLANES * (ci + 1))
            a, b = a_all[r8], b_all[r8]
            a_ref[ci] = a
            M = jnp.maximum(g_all[r8], m_prev)
            ML = M[:, L - 1:L]
            Z = jnp.concatenate([M, jnp.exp(m_prev - M), jnp.exp(-(b + M)), jnp.exp(a - ML), zpad],
                                axis=0)
            zt_ref[ci] = Z.T
            dec_ref[ci] = jnp.broadcast_to(jnp.exp(m_prev - ML), (SUBLANES, 2 * M_HD))
            m_prev = b[:, L - 1:L] + ML
        m_ref[bi] = jnp.broadcast_to(m_prev, (SUBLANES, LANES))

    chains = [(bi, h) for bi in range(nb) for h in range(M_HEADS)]
    nt = (((1,), (1,)), ((), ()))
    tn = (((0,), (0,)), ((), ()))

    def chunk(c, carry):
        rows = pl.ds(pl.multiple_of(c * L, L), L)
        Zt = [zt_ref[bi * NC + c] for bi in range(nb)]
        a = [a_ref[bi * NC + c] for bi in range(nb)]
        dec = [dec_ref[bi * NC + c] for bi in range(nb)]
        hs = [slice(h * M_HD, (h + 1) * M_HD) for h in range(M_HEADS)]
        hs2 = [slice(M_W + h * M_HD, M_W + (h + 1) * M_HD) for h in range(M_HEADS)]
        q = [qk_ref[bi, rows, hs[h]] for bi, h in chains]
        k = [qk_ref[bi, rows, hs2[h]] for bi, h in chains]
        vext = [jnp.concatenate([vo_ref[bi, rows, hs[h]], ones_v], axis=1) for bi, h in chains]
        cst = [c_ref[bi * M_HEADS + h] for bi, h in chains]
        n = range(len(chains))
        sc = [lax.dot_general(q[i], k[i], nt, preferred_element_type=F32) for i in n]
        qc = [jnp.dot(q[i], cst[i].astype(BF16), preferred_element_type=F32) for i in n]
        pm = [(sc[i] * jnp.exp(jnp.where(tril, a[bi][h:h + 1, :] - Zt[bi][:, h:h + 1], -jnp.inf))).astype(BF16)
              for i, (bi, h) in enumerate(chains)]
        pv = [jnp.dot(pm[i], vext[i], preferred_element_type=F32) for i in n]
        rep = []
        for bi in range(nb):
            zf = jnp.where(factor_cols, Zt[bi], 0.0)
            zh = zf.astype(BF16)
            zl = (zf - zh.astype(F32)).astype(BF16)
            rep.append(jnp.dot(jnp.concatenate([zh, zl], axis=1), sel_ref[...], preferred_element_type=F32))
        e_inter = [rep[bi][:, M_HD * h:M_HD * (h + 1)] for bi, h in chains]
        w_state = [rep[bi][:, M_HD * (M_HEADS + h):M_HD * (M_HEADS + h + 1)] for bi, h in chains]
        kw = [(w_state[i] * k[i].astype(F32)).astype(BF16) for i in n]
        upd = [lax.dot_general(kw[i], vext[i], tn, preferred_element_type=F32) for i in n]
        for i, (bi, h) in enumerate(chains):
            c_ref[bi * M_HEADS + h] = dec[bi][h:h + 1, :] * cst[i] + upd[i]
            nd = pv[i] + jnp.concatenate([e_inter[i], e_inter[i]], axis=1) * qc[i]
            hh = nd[:, :M_HD] / jnp.maximum(jnp.abs(nd[:, M_HD:]),
                                            Zt[bi][:, 2 * SUBLANES + h:2 * SUBLANES + h + 1])
            hh = _sigmoid(vo_ref[bi, rows, hs2[h]].astype(F32)) * hh
            hn = hh * lax.rsqrt(jnp.mean(hh * hh, axis=-1, keepdims=True) + LN_EPS)
            out_ref[bi, rows, hs[h]] = (hn * gain_ref[:, hs[h]]).astype(BF16)
        return carry

    lax.fori_loop(0, NC, chunk, 0)


def _mlstm(oa, g3, u_tri, gain, *, B, S, L, nb, ts):
    N = oa.shape[0]
    NC = ts // L
    oa3 = oa.reshape(B, S, oa.shape[1])
    g4 = g3.reshape(B, S // L, 2 * SUBLANES, L)
    sel = jnp.asarray(_mlstm_sel(), BF16)
    kern = functools.partial(_mlstm_kernel, L=L, NC=NC, nb=nb)
    out = pl.pallas_call(
        kern,
        grid=(B // nb, S // ts),
        in_specs=[pl.BlockSpec((nb, ts, 2 * M_W), lambda b, t: (b, t, C_QK // (2 * M_W))),
                  pl.BlockSpec((nb, ts, 2 * M_W), lambda b, t: (b, t, C_VO // (2 * M_W))),
                  pl.BlockSpec((nb, NC, 2 * SUBLANES, L), lambda b, t: (b, t, 0, 0)),
                  pl.BlockSpec((L, L), lambda b, t: (0, 0)),
                  pl.BlockSpec((1, M_W), lambda b, t: (0, 0)),
                  pl.BlockSpec(sel.shape, lambda b, t: (0, 0))],
        out_specs=pl.BlockSpec((nb, ts, M_W), lambda b, t: (b, t, 0)),
        out_shape=jax.ShapeDtypeStruct((B, S, M_W), BF16),
        scratch_shapes=[pltpu.VMEM((nb * M_HEADS, M_HD, 2 * M_HD), F32),
                        pltpu.VMEM((nb * NC, L, LANES), F32),
                        pltpu.VMEM((nb * NC, SUBLANES, L), F32),
                        pltpu.VMEM((nb * NC, SUBLANES, 2 * M_HD), F32),
                        pltpu.VMEM((nb, SUBLANES, LANES), F32)],
        compiler_params=_cparams(2),
        name="mlstm",
    )(oa3, oa3, g4, u_tri, gain, sel)
    return out.reshape(N, M_W)


_G_LEVELS = 6
_G_XROW = 2 * G_CHUNK + SUBLANES


def _gla_consts():
    L = G_CHUNK
    t = np.arange(L)
    blocks = [(t[None, :] <= t[:, None]).astype(np.float32),
              (t[None, :] > t[:, None]).astype(np.float32),
              np.ones((SUBLANES, L), np.float32)]
    masks = [np.eye(L, dtype=np.float32)]
    m = 1
    while m < L:
        wl = np.zeros((L, L), np.float32)
        for r in range(L):
            r0 = (r // (2 * m)) * 2 * m + m
            if r % (2 * m) >= m:
                wl[r, r0:r + 1] = 1.0
            else:
                wl[r, r + 1:r0] = 1.0
        blocks.append(wl)
        tt, ss = t[:, None], t[None, :]
        masks.append(((tt // (2 * m) == ss // (2 * m)) & (tt % (2 * m) >= m)
                      & (ss % (2 * m) < m)).astype(np.float32))
        m *= 2
    w = np.concatenate(blocks, axis=0)
    w3 = np.concatenate([w, w, w], axis=1)
    mk = np.stack([np.concatenate([x] * G_HEADS, axis=0) for x in masks])
    return w3, mk


def _gla_kernel(qk_ref, v_ref, gg_ref, la_ref, w3_ref, mk_ref, gain_ref, out_ref, st_ref, *, NC, nb):
    L = G_CHUNK

    @pl.when(pl.program_id(1) == 0)
    def _():
        st_ref[...] = jnp.zeros_like(st_ref)

    lane_head = lax.broadcasted_iota(jnp.int32, (L, G_KW), 1) // G_DK
    br = lax.broadcasted_iota(jnp.int32, (2 * G_DV, LANES), 0) < G_DV
    bl = lax.broadcasted_iota(jnp.int32, (2 * G_DV, LANES), 1) < G_DK
    bmask = br == bl
    nt = (((1,), (1,)), ((), ()))
    tn = (((0,), (0,)), ((), ()))

    def chunk(c, carry):
        rows = pl.ds(pl.multiple_of(c * L, L), L)
        X, q, k = [], [], []
        for bi in range(nb):
            la = la_ref[bi, rows, :]
            hi = la.astype(BF16)
            r1 = la - hi.astype(F32)
            mid = r1.astype(BF16)
            lo = (r1 - mid.astype(F32)).astype(BF16)
            stk = jnp.concatenate([hi, mid, lo], axis=0)
            X.append(jnp.exp(jnp.dot(w3_ref[...], stk, preferred_element_type=F32)))
            q.append(qk_ref[bi, rows, 0:G_KW].astype(F32))
            k.append(qk_ref[bi, rows, G_KW:2 * G_KW].astype(F32))

        sc = [[None] * (_G_LEVELS + 1) for _ in range(nb)]
        for lev in range(_G_LEVELS + 1):
            for bi in range(nb):
                if lev == 0:
                    qt, kt = q[bi], k[bi]
                else:
                    xl = X[bi][_G_XROW + L * (lev - 1):_G_XROW + L * lev, :]
                    qt, kt = q[bi] * xl, k[bi] * xl
                q4 = jnp.concatenate([jnp.where(lane_head == h, qt, 0.0) for h in range(G_HEADS)],
                                     axis=0).astype(BF16)
                sc[bi][lev] = lax.dot_general(q4, kt.astype(BF16), nt, preferred_element_type=F32)
        Ab = []
        for bi in range(nb):
            A = sc[bi][0] * mk_ref[0]
            for lev in range(1, _G_LEVELS + 1):
                A = A + sc[bi][lev] * mk_ref[lev]
            Ab.append(A.astype(BF16))

        for bi in range(nb):
            gg = gg_ref[bi, rows, :].astype(F32)
            gate = gg * _sigmoid(gg)
            for p in range(2):
                ls = slice(LANES * p, LANES * (p + 1))
                vp = v_ref[bi, rows, 2 * G_DV * p:2 * G_DV * (p + 1)]
                oi = [jnp.dot(Ab[bi][L * (2 * p + hh):L * (2 * p + hh + 1)],
                              vp[:, G_DV * hh:G_DV * (hh + 1)], preferred_element_type=F32)
                      for hh in range(2)]
                st = st_ref[bi, p]
                qc = (q[bi][:, ls] * X[bi][0:L, ls]).astype(BF16)
                o_inter = lax.dot_general(qc, st.astype(BF16), nt, preferred_element_type=F32)
                kc = (k[bi][:, ls] * X[bi][L:2 * L, ls]).astype(BF16)
                upd = lax.dot_general(vp, kc, tn, preferred_element_type=F32)
                dec = X[bi][2 * L:2 * L + 1, ls]
                st_ref[bi, p] = jnp.where(bmask, dec * st + upd, 0.0)
                for hh in range(2):
                    o = o_inter[:, G_DV * hh:G_DV * (hh + 1)] + oi[hh]
                    hn = o * lax.rsqrt(jnp.mean(o * o, axis=-1, keepdims=True) + LN_EPS)
                    hs = slice(G_DV * (2 * p + hh), G_DV * (2 * p + hh + 1))
                    out_ref[bi, rows, hs] = (hn * gain_ref[:, hs] * gate[:, hs]).astype(BF16)
        return carry

    lax.fori_loop(0, NC, chunk, 0)


def _gla(oa, la, w3, mk, gain, *, B, S, nb, ts):
    N = oa.shape[0]
    oa3 = oa.reshape(B, S, oa.shape[1])
    la3 = la.reshape(B, S, G_KW)
    kern = functools.partial(_gla_kernel, NC=ts // G_CHUNK, nb=nb)
    out = pl.pallas_call(
        kern,
        grid=(B // nb, S // ts),
        in_specs=[pl.BlockSpec((nb, ts, 2 * G_KW), lambda b, t: (b, t, C_GQK // (2 * G_KW))),
                  pl.BlockSpec((nb, ts, G_W), lambda b, t: (b, t, C_GV // G_W)),
                  pl.BlockSpec((nb, ts, G_W), lambda b, t: (b, t, C_GG // G_W)),
                  pl.BlockSpec((nb, ts, G_KW), lambda b, t: (b, t, 0)),
                  pl.BlockSpec(w3.shape, lambda b, t: (0, 0)),
                  pl.BlockSpec(mk.shape, lambda b, t: (0, 0, 0)),
                  pl.BlockSpec((1, G_W), lambda b, t: (0, 0))],
        out_specs=pl.BlockSpec((nb, ts, G_W), lambda b, t: (b, t, 0)),
        out_shape=jax.ShapeDtypeStruct((B, S, G_W), BF16),
        scratch_shapes=[pltpu.VMEM((nb, 2, 2 * G_DV, LANES), F32)],
        compiler_params=_cparams(2),
        name="gla",
    )(oa3, oa3, oa3, la3, w3, mk, gain)
    return out.reshape(N, G_W)


def _layer_norm(z, g, b):
    mu = jnp.mean(z, axis=-1, keepdims=True)
    zc = z - mu
    var = jnp.mean(zc * zc, axis=-1, keepdims=True)
    return zc * lax.rsqrt(var + LN_EPS) * g + b


def _outproj_kernel(hm_ref, hg_ref, wf_ref, x_ref, mod_ref, g_ref, b_ref, wrg_ref, wre_ref, br_ref,
                    x1_ref, u2_ref, rrow_ref, w_ref, wr_ref, *, tb, nh):
    @pl.when(pl.program_id(0) == 0)
    def _():
        w_ref[...] = wf_ref[...].astype(BF16)
        z = lambda n: jnp.zeros((n, wrg_ref.shape[2]), F32)
        wt = jnp.concatenate([wrg_ref[0], z(SUBLANES - N_GROUPS), wre_ref[0],
                              z(LANES - SUBLANES - N_EXP)], axis=0).T
        hi = wt.astype(BF16)
        wr_ref[:, 0:LANES] = hi
        wr_ref[:, LANES:2 * LANES] = (wt - hi.astype(F32)).astype(BF16)

    mod = mod_ref[0]
    blocks = [slice(tb * j, tb * (j + 1)) for j in range(nh)]
    y = [jnp.dot(hm_ref[r, :], w_ref[0:M_W, :], preferred_element_type=F32)
         + jnp.dot(hg_ref[r, :], w_ref[M_W:M_W + G_W, :], preferred_element_type=F32) for r in blocks]
    u2 = []
    for j, r in enumerate(blocks):
        z = ALPHA * x_ref[r, :] + (1.0 + mod[2:3, :]) * y[j]
        x1 = _layer_norm(z, g_ref[...], b_ref[...])
        x1_ref[r, :] = x1
        u2.append(x1 * (1.0 + mod[4:5, :]) + mod[3:4, :])
        u2_ref[r, :] = u2[j].astype(BF16)

    u2h = [u.astype(BF16) for u in u2]
    u2l = [(u2[j] - u2h[j].astype(F32)).astype(BF16) for j in range(nh)]
    lh = [jnp.dot(u, wr_ref[...], preferred_element_type=F32) for u in u2h]
    ll = [jnp.dot(u, wr_ref[:, 0:LANES], preferred_element_type=F32) for u in u2l]
    for j in range(nh):
        logits = lh[j][:, 0:LANES] + lh[j][:, LANES:2 * LANES] + ll[j] + br_ref[...]
        rrow_ref[j] = _route_select(logits.T, tb)


def _route_select(lt, tm):
    row = lax.broadcasted_iota(jnp.int32, (SUBLANES, tm), 0)
    gl = jnp.where(row < N_GROUPS, lt[0:SUBLANES, :], -jnp.inf)
    gmax = jnp.max(gl, axis=0, keepdims=True)
    gsel = jnp.min(jnp.where(gl == gmax, row, SUBLANES), axis=0, keepdims=True)
    pg = 1.0 / jnp.sum(jnp.exp(gl - gmax), axis=0, keepdims=True)
    ein = jnp.zeros((SUBLANES, tm), F32)
    for g in range(N_GROUPS):
        ein = jnp.where(gsel == g, lt[SUBLANES * (g + 1):SUBLANES * (g + 2), :], ein)
    v1 = jnp.max(ein, axis=0, keepdims=True)
    i1 = jnp.min(jnp.where(ein == v1, row, SUBLANES), axis=0, keepdims=True)
    rest = jnp.where(row == i1, -jnp.inf, ein)
    v2 = jnp.max(rest, axis=0, keepdims=True)
    i2 = jnp.min(jnp.where(rest == v2, row, SUBLANES), axis=0, keepdims=True)
    t2 = jnp.exp(v2 - v1)
    p1 = 1.0 / (1.0 + t2)
    e0 = (gsel * E_PER_G + i1).astype(F32)
    e1 = (gsel * E_PER_G + i2).astype(F32)
    return jnp.concatenate([e0, e1, pg * p1, pg * (t2 * p1), jnp.zeros((SUBLANES - 4, tm), F32)], axis=0)


def _outproj(hm, hg, w_out, x2, mod3, g, b, wrg_t, wre_t, br, *, S, tb, nh, layer):
    N, D = x2.shape
    tm = tb * nh
    tpb = S // tm
    kern = functools.partial(_outproj_kernel, tb=tb, nh=nh)
    return pl.pallas_call(
        kern,
        grid=(N // tm,),
        in_specs=[pl.BlockSpec((tm, M_W), lambda i: (i, 0)),
                  pl.BlockSpec((tm, G_W), lambda i: (i, 0)),
                  pl.BlockSpec((M_W + G_W, D), lambda i: (0, 0), pipeline_mode=pl.Buffered(1)),
                  pl.BlockSpec((tm, D), lambda i: (i, 0)),
                  pl.BlockSpec((1, 6, D), lambda i: (i // tpb, 0, 0)),
                  pl.BlockSpec((1, D), lambda i: (0, 0)),
                  pl.BlockSpec((1, D), lambda i: (0, 0)),
                  pl.BlockSpec((1, N_GROUPS, D), lambda i: (layer, 0, 0)),
                  pl.BlockSpec((1, N_EXP, D), lambda i: (layer, 0, 0)),
                  pl.BlockSpec((1, LANES), lambda i: (0, 0))],
        out_specs=[pl.BlockSpec((tm, D), lambda i: (i, 0)),
                   pl.BlockSpec((tm, D), lambda i: (i, 0)),
                   pl.BlockSpec((nh, SUBLANES, tb), lambda i: (i, 0, 0))],
        out_shape=[jax.ShapeDtypeStruct((N, D), F32),
                   jax.ShapeDtypeStruct((N, D), BF16),
                   jax.ShapeDtypeStruct((N // tb, SUBLANES, tb), F32)],
        scratch_shapes=[pltpu.VMEM((M_W + G_W, D), BF16), pltpu.VMEM((D, 2 * LANES), BF16)],
        compiler_params=_cparams(),
        name="outproj",
    )(hm, hg, w_out, x2, mod3, g, b, wrg_t, wre_t, br)


def _slots_per_tile(tb):
    worst = 2 * tb + N_EXP * (GRAN - 1)
    return -(-worst // LANES) * LANES


def _ffn_tiles(n_tok, tb):
    worst_rows = 2 * n_tok + (n_tok // tb) * N_EXP * (GRAN - 1)
    return -(-worst_rows // FFN_TM) + N_EXP


def _route_kernel(rr_ref, u_ref, lt_ref, srow_ref, col_ref, gd_ref, meta_ref, mg_ref, part_ref,
                  *, NT, tb, TM):
    iota_e = lax.broadcasted_iota(jnp.int32, (N_EXP, tb), 0).astype(F32)
    glane = lax.broadcasted_iota(jnp.int32, (N_EXP, LANES), 1).astype(F32)
    ltri = lt_ref[...]

    def prefix_e(col):
        return jnp.dot(ltri, jnp.broadcast_to(col, (N_EXP, LANES)),
                       preferred_element_type=F32, precision=HIGHEST)[:, 0:1]

    def p1(j, run8):
        r = rr_ref[j]
        oh0 = jnp.where(iota_e == r[0:1, :], 1.0, 0.0)
        oh1 = jnp.where(iota_e == r[1:2, :], 1.0, 0.0)
        cum0 = jnp.dot(oh0.astype(BF16), u_ref[...], preferred_element_type=F32)
        cum1 = jnp.dot(oh1.astype(BF16), u_ref[...], preferred_element_type=F32)
        c0 = jnp.sum(oh0, axis=1, keepdims=True)
        n8 = jnp.floor((c0 + jnp.sum(oh1, axis=1, keepdims=True) + (GRAN - 1.0)) * (1.0 / GRAN))
        lo8 = prefix_e(n8)
        s0 = jnp.sum(oh0 * (GRAN * lo8 + cum0 - 1.0), axis=0, keepdims=True)
        s1 = jnp.sum(oh1 * (GRAN * lo8 + c0 + cum1 - 1.0), axis=0, keepdims=True)
        info = jnp.concatenate([s0, s1, r[2:4, :], jnp.zeros((SUBLANES - 4, tb), F32)], axis=0)
        srow_ref[j] = info
        col_ref[pl.ds(pl.multiple_of(j * tb, tb), tb), :] = jnp.concatenate(
            [info, jnp.zeros((LANES - SUBLANES, tb), F32)], axis=0).T
        mg = jnp.where((lo8 <= glane) & (glane < lo8 + n8), 1.0, 0.0)
        mg_ref[j] = mg
        part = jnp.sum(mg * (run8 + glane - lo8), axis=0, keepdims=True)
        gcnt = jnp.broadcast_to(jnp.sum(n8, axis=0, keepdims=True), (1, LANES))
        part_ref[j] = jnp.concatenate([part, gcnt, jnp.zeros((SUBLANES - 2, LANES), F32)], axis=0)
        return run8 + n8

    tot8 = lax.fori_loop(0, NT, p1, jnp.zeros((N_EXP, 1), F32), unroll=8 if NT % 8 == 0 else 1)
    seg_t = jnp.floor((tot8 * GRAN + (TM - 1.0)) * (1.0 / TM))
    base_t = prefix_e(seg_t)
    base8 = base_t * (TM // GRAN)
    lane1 = lax.broadcasted_iota(jnp.int32, (1, LANES), 1)

    def p2(j, carry):
        pr = part_ref[j]
        dst = (pr[0:1, :] + jnp.sum(mg_ref[j] * base8, axis=0, keepdims=True)) * GRAN
        gd_ref[j] = jnp.where(lane1 == G_LAST, pr[1:2, :], dst).astype(jnp.int32)
        return carry

    lax.fori_loop(0, NT, p2, 0, unroll=8 if NT % 8 == 0 else 1)
    eye = jnp.where(glane == lax.broadcasted_iota(jnp.int32, (N_EXP, LANES), 0).astype(F32), 1.0, 0.0)
    tail_row = jnp.sum(eye * ((base8 + tot8) * GRAN), axis=0, keepdims=True)
    tail_n8 = jnp.sum(eye * (seg_t * (TM // GRAN) - tot8), axis=0, keepdims=True)
    nv_l = jnp.broadcast_to(jnp.sum(seg_t, axis=0, keepdims=True), (1, LANES))
    gd_ref[NT] = jnp.where(lane1 == G_LAST, nv_l, tail_row).astype(jnp.int32)
    gd_ref[NT + 1] = tail_n8.astype(jnp.int32)
    ti = lax.broadcasted_iota(jnp.int32, (N_EXP, tb), 1).astype(F32)
    te = jnp.sum(jnp.where(base_t <= ti, 1.0, 0.0), axis=0, keepdims=True) - 1.0
    nv = jnp.broadcast_to(jnp.sum(seg_t, axis=0, keepdims=True), (1, tb))
    own = jnp.where((base_t <= ti) & (ti < base_t + seg_t), 1.0, 0.0)
    vr = jnp.sum(own * jnp.clip(tot8 * GRAN - (ti - base_t) * TM, 0.0, TM), axis=0, keepdims=True)
    meta_ref[...] = jnp.concatenate([te, nv, vr, jnp.zeros((SUBLANES - 3, tb), F32)],
                                    axis=0).astype(jnp.int32)


def _route(rrow, u_cnt, ltri, *, TM):
    NT, _, tb = rrow.shape
    kern = functools.partial(_route_kernel, NT=NT, tb=tb, TM=TM)
    full3 = lambda i: (0, 0, 0)
    return pl.pallas_call(
        kern,
        grid=(1,),
        in_specs=[pl.BlockSpec((NT, SUBLANES, tb), full3),
                  pl.BlockSpec((tb, tb), lambda i: (0, 0)),
                  pl.BlockSpec((N_EXP, N_EXP), lambda i: (0, 0))],
        out_specs=[pl.BlockSpec((NT, SUBLANES, tb), full3),
                   pl.BlockSpec((NT * tb, LANES), lambda i: (0, 0)),
                   pl.BlockSpec((NT + 2, 1, LANES), full3),
                   pl.BlockSpec((SUBLANES, tb), lambda i: (0, 0))],
        out_shape=[jax.ShapeDtypeStruct((NT, SUBLANES, tb), F32),
                   jax.ShapeDtypeStruct((NT * tb, LANES), F32),
                   jax.ShapeDtypeStruct((NT + 2, 1, LANES), jnp.int32),
                   jax.ShapeDtypeStruct((SUBLANES, tb), jnp.int32)],
        scratch_shapes=[pltpu.VMEM((NT, N_EXP, LANES), F32), pltpu.VMEM((NT, SUBLANES, LANES), F32)],
        compiler_params=_cparams(),
        name="route",
    )(rrow, u_cnt, ltri)


_HI_MASK = 0xFFFF0000


def _pack_halves(x):
    c = x.shape[1] // 2
    lo = lax.bitcast_convert_type(x[:, :c], U32)
    hi = lax.bitcast_convert_type(x[:, c:], U32)
    return (lo >> 16) | (hi & U32(_HI_MASK))


def _unpack_halves(w):
    lo = lax.bitcast_convert_type(w << 16, F32)
    hi = lax.bitcast_convert_type(w & U32(_HI_MASK), F32)
    return jnp.concatenate([lo, hi], axis=1).astype(BF16)


def _granule_copy(src_ref, src_row, dst_ref, dst_row, sem):
    return pltpu.make_async_copy(src_ref.at[pl.ds(src_row, GRAN), :], dst_ref.at[pl.ds(dst_row, GRAN), :], sem)


def _for_granules(n, body, unroll=4):
    def blk(i, carry):
        for t in range(unroll):
            body(i * unroll + t)
        return carry

    def one(g, carry):
        body(g)
        return carry

    nblk = n // unroll
    lax.fori_loop(0, nblk, blk, 0)
    lax.fori_loop(nblk * unroll, n, one, 0)


def _wait_granules(n, src_ref, dst_ref, sem, n_max):
    b = 1
    while b <= n_max:
        @pl.when((n & b) != 0)
        def _(b=b):
            pltpu.make_async_copy(src_ref.at[pl.ds(0, b * GRAN), :], dst_ref.at[pl.ds(0, b * GRAN), :],
                                  sem).wait()
        b *= 2


def _dispatch_kernel(gd_ref, srow_ref, u_ref, xs_ref, buf, zbuf, sems, *, NT, SL, TM, n_tiles):
    j = pl.program_id(0)
    slot = j % 2
    zsem = sems.at[2]

    def drain(tile, sl):
        _wait_granules(gd_ref[tile, G_LAST], buf.at[sl], xs_ref, sems.at[sl], SL // GRAN)

    def tile_fill(t):
        return pltpu.make_async_copy(zbuf, xs_ref.at[pl.ds(pl.multiple_of(t * TM, TM), TM), :], zsem)

    def zero_fill(wait):
        for e in range(N_EXP):
            n, row0 = gd_ref[NT + 1, e], gd_ref[NT, e]
            b = TM // GRAN // 2
            while b >= 1:
                @pl.when((n & b) != 0)
                def _(b=b, n=n, row0=row0):
                    start = pl.multiple_of(row0 + ((n >> b.bit_length()) << b.bit_length()) * GRAN, GRAN)
                    cp = pltpu.make_async_copy(zbuf.at[pl.ds(0, b * GRAN), :],
                                               xs_ref.at[pl.ds(start, b * GRAN), :], zsem)
                    cp.wait() if wait else cp.start()
                b //= 2

        def zt(t, carry):
            tile_fill(t).wait() if wait else tile_fill(t).start()
            return carry
        lax.fori_loop(gd_ref[NT, G_LAST], n_tiles, zt, 0)

    @pl.when(j == 0)
    def _():
        zbuf[...] = jnp.zeros_like(zbuf)
        zero_fill(False)

    @pl.when(j >= 2)
    def _():
        drain(j - 2, slot)

    s = srow_ref[0]
    rows = lax.broadcasted_iota(jnp.int32, (SL, s.shape[1]), 0).astype(F32)
    m0 = rows == s[0:1, :]
    m1 = rows == s[1:2, :]
    oh = jnp.where(m0 | m1, 1.0, 0.0).astype(BF16)
    dw = u_ref.shape[1] // 2
    buf[slot, :, 0:dw] = _pack_halves(jnp.dot(oh, u_ref[...], preferred_element_type=F32))
    wrow = jnp.sum(jnp.where(m0, s[2:3, :], 0.0) + jnp.where(m1, s[3:4, :], 0.0), axis=1, keepdims=True)
    buf[slot, :, dw:dw + LANES] = lax.bitcast_convert_type(jnp.broadcast_to(wrow, (SL, LANES)), U32)

    def issue(g):
        _granule_copy(buf.at[slot], pl.multiple_of(g * GRAN, GRAN), xs_ref,
                      pl.multiple_of(gd_ref[j, g], GRAN), sems.at[slot]).start()

    _for_granules(gd_ref[j, G_LAST], issue)

    @pl.when(j == NT - 1)
    def _():
        drain(j, slot)
        if NT > 1:
            drain(j - 1, 1 - slot)
        zero_fill(True)


def _dispatch(gd, srow, u2, *, n_tiles, TM):
    N, D = u2.shape
    NT, _, tb = srow.shape
    SL = _slots_per_tile(tb)
    n_rows = n_tiles * TM
    kern = functools.partial(_dispatch_kernel, NT=NT, SL=SL, TM=TM, n_tiles=n_tiles)
    grid_spec = pltpu.PrefetchScalarGridSpec(
        num_scalar_prefetch=1,
        grid=(NT,),
        in_specs=[pl.BlockSpec((1, SUBLANES, tb), lambda j, gd: (j, 0, 0)),
                  pl.BlockSpec((tb, D), lambda j, gd: (j, 0))],
        out_specs=pl.BlockSpec(memory_space=pl.ANY),
        scratch_shapes=[pltpu.VMEM((2, SL, D // 2 + LANES), U32), pltpu.VMEM((TM, D // 2 + LANES), U32),
                        pltpu.SemaphoreType.DMA((3,))],
    )
    return pl.pallas_call(
        kern,
        grid_spec=grid_spec,
        out_shape=jax.ShapeDtypeStruct((n_rows, D // 2 + LANES), U32),
        compiler_params=_cparams(),
        name="dispatch",
    )(gd, srow, u2)


def _ffn_kernel(te_ref, nv_ref, vr_ref, xs_ref, wg_ref, wu_ref, wd_ref, o_ref, wgb, wub, wdb, sg, su, sd, slot_ref,
                sems):
    i = pl.program_id(0)
    nv = nv_ref[0]
    e = te_ref[i]

    def weight_copies(ex, sl):
        return (pltpu.make_async_copy(wg_ref.at[ex], sg.at[sl], sems.at[sl]),
                pltpu.make_async_copy(wu_ref.at[ex], su.at[sl], sems.at[sl]),
                pltpu.make_async_copy(wd_ref.at[ex], sd.at[sl], sems.at[sl]))

    @pl.when(i == 0)
    def _():
        slot_ref[0] = 0
        for cp in weight_copies(e, 0):
            cp.start()

    @pl.when((i < nv) & ((i == 0) | (e != te_ref[jnp.maximum(i - 1, 0)])))
    def _():
        sl = slot_ref[0]
        for cp in weight_copies(e, sl):
            cp.wait()
        wgb[...] = sg[sl].astype(BF16)
        wub[...] = su[sl].astype(BF16)
        wdb[...] = sd[sl].astype(BF16)
        nxt = lax.while_loop(lambda t: (t < nv) & (te_ref[jnp.minimum(t, nv - 1)] == e), lambda t: t + 1, i + 1)

        @pl.when(nxt < nv)
        def _():
            for cp in weight_copies(te_ref[nxt], 1 - sl):
                cp.start()
        slot_ref[0] = 1 - sl

    hm = xs_ref.shape[0] // FFN_SUB
    dw = o_ref.shape[1]

    def swiglu_rows(nsub):
        halves = tuple(slice(hm * j, hm * (j + 1)) for j in range(nsub))
        x = [_unpack_halves(xs_ref[r, 0:dw]) for r in halves]
        g = [jnp.dot(x[j], wgb[...], preferred_element_type=F32) for j in range(nsub)]
        u = [jnp.dot(x[j], wub[...], preferred_element_type=F32) for j in range(nsub)]
        h = [(g[j] * _sigmoid(g[j]) * u[j]).astype(BF16) for j in range(nsub)]
        y = [jnp.dot(h[j], wdb[...], preferred_element_type=F32) for j in range(nsub)]
        for j in range(nsub):
            wt = lax.bitcast_convert_type(xs_ref[halves[j], dw:dw + LANES], F32)
            yw = y[j] * jnp.concatenate([wt] * (2 * dw // LANES), axis=1)
            o_ref[halves[j], :] = _pack_halves(yw.astype(BF16).astype(F32))
        if nsub < FFN_SUB:
            o_ref[hm * nsub:, :] = jnp.zeros((hm * (FFN_SUB - nsub), dw), U32)

    used = vr_ref[i]
    for nsub in range(1, FFN_SUB + 1):
        lo, hi = hm * (nsub - 1), hm * nsub
        pl.when((i < nv) & (used > lo) & ((used <= hi) if nsub < FFN_SUB else True))(
            functools.partial(swiglu_rows, nsub))

    @pl.when(i >= nv_ref[0])
    def _():
        o_ref[...] = jnp.zeros_like(o_ref)


def _ffn(te, nv, vr, xs, wg, wu, wd, *, TM):
    P, XW = xs.shape
    DW = XW - LANES
    D = 2 * DW
    n_tiles = P // TM
    grid_spec = pltpu.PrefetchScalarGridSpec(
        num_scalar_prefetch=3,
        grid=(n_tiles,),
        in_specs=[pl.BlockSpec((TM, XW), lambda i, te, nv, vr: (jnp.maximum(jnp.minimum(i, nv[0] - 1), 0), 0)),
                  pl.BlockSpec(memory_space=pl.ANY),
                  pl.BlockSpec(memory_space=pl.ANY),
                  pl.BlockSpec(memory_space=pl.ANY)],
        out_specs=pl.BlockSpec((TM, DW), lambda i, te, nv, vr: (i, 0)),
        scratch_shapes=[pltpu.VMEM((D, D_EXP), BF16), pltpu.VMEM((D, D_EXP), BF16),
                        pltpu.VMEM((D_EXP, D), BF16),
                        pltpu.VMEM((2, D, D_EXP), F32), pltpu.VMEM((2, D, D_EXP), F32),
                        pltpu.VMEM((2, D_EXP, D), F32), pltpu.SMEM((1,), jnp.int32),
                        pltpu.SemaphoreType.DMA((2,))],
    )
    return pl.pallas_call(
        _ffn_kernel,
        grid_spec=grid_spec,
        out_shape=jax.ShapeDtypeStruct((P, DW), U32),
        compiler_params=_cparams(),
        name="ffn",
    )(te, nv, vr, xs, wg, wu, wd)


def _combine_kernel(gd_ref, ys_ref, col_ref, x1_ref, mod_ref, g_ref, b_ref, o_ref, buf, sems, *, NT, SL):
    j = pl.program_id(0)
    slot = j % 2

    def fetch(tile, sl):
        def f(g):
            _granule_copy(ys_ref, pl.multiple_of(gd_ref[tile, g], GRAN), buf.at[sl],
                          pl.multiple_of(g * GRAN, GRAN), sems.at[sl]).start()
        _for_granules(gd_ref[tile, G_LAST], f)

    @pl.when(j == 0)
    def _():
        fetch(0, 0)

    @pl.when(j + 1 < NT)
    def _():
        fetch(j + 1, 1 - slot)

    ng = gd_ref[j, G_LAST]

    _wait_granules(ng, ys_ref, buf.at[slot], sems.at[slot], SL // GRAN)

    rows = lax.broadcasted_iota(jnp.int32, (SL, 1), 0)
    yb = _unpack_halves(jnp.where(rows < ng * GRAN, buf[slot], U32(0)))
    col = col_ref[...]
    tb = col.shape[0]
    lanes = lax.broadcasted_iota(jnp.int32, (tb, SL), 1).astype(F32)
    sel = jnp.where((lanes == col[:, 0:1]) | (lanes == col[:, 1:2]), 1.0, 0.0).astype(BF16)
    y = jnp.dot(sel, yb, preferred_element_type=F32)
    mod = mod_ref[0]
    z = ALPHA * x1_ref[...] + (1.0 + mod[5:6, :]) * y
    o_ref[...] = _layer_norm(z, g_ref[...], b_ref[...])


def _combine(gd, ys, col, x1, mod3, g, b, *, S, tb):
    N, D = x1.shape
    NT = N // tb
    tpb = S // tb
    SL = _slots_per_tile(tb)
    kern = functools.partial(_combine_kernel, NT=NT, SL=SL)
    grid_spec = pltpu.PrefetchScalarGridSpec(
        num_scalar_prefetch=1,
        grid=(NT,),
        in_specs=[pl.BlockSpec(memory_space=pl.ANY),
                  pl.BlockSpec((tb, LANES), lambda j, gd: (j, 0)),
                  pl.BlockSpec((tb, D), lambda j, gd: (j, 0)),
                  pl.BlockSpec((1, 6, D), lambda j, gd: (j // tpb, 0, 0)),
                  pl.BlockSpec((1, D), lambda j, gd: (0, 0)),
                  pl.BlockSpec((1, D), lambda j, gd: (0, 0))],
        out_specs=pl.BlockSpec((tb, D), lambda j, gd: (j, 0)),
        scratch_shapes=[pltpu.VMEM((2, SL, D // 2), U32), pltpu.SemaphoreType.DMA((2,))],
    )
    return pl.pallas_call(
        kern,
        grid_spec=grid_spec,
        out_shape=jax.ShapeDtypeStruct((N, D), F32),
        compiler_params=_cparams(),
        name="combine",
    )(gd, ys, col, x1, mod3, g, b)


def _layer(x, c, l, w_ada, b_ada, w_in, w_conv, b_conv, b_igate, b_fgate, mlstm_norm_g, w_gla_a, b_gla_a,
           gla_norm_g, w_out, ln1_g, ln1_b, w_route_group, b_route_group, w_route_expert, b_route_expert,
           w_gate, w_up, w_down, ln2_g, ln2_b):
    B, S, D = x.shape
    N = B * S
    x2 = x.reshape(N, D)
    tm_in = min(512, S)
    tm = min(256, S)
    lm = min(256, S)
    assert S % tm_in == 0 and S % tm == 0 and tm_in % lm == 0 and S % G_CHUNK == 0
    assert w_in.shape[1:] == (D, IN_TOT) and w_gate.shape[1:] == (N_EXP, D, D_EXP)

    mod3 = _ada(c, w_ada[l], b_ada[l]).reshape(B, 6, D)

    wa_pad = jnp.zeros((LANES, G_KW), F32).at[SM_A:SM_A + G_RANK].set(w_gla_a[l]).astype(BF16)
    bg = (jnp.zeros((2 * SUBLANES, 1), F32).at[0:M_HEADS, 0].set(b_igate[l])
          .at[SUBLANES:SUBLANES + M_HEADS, 0].set(b_fgate[l]))
    oa, la, g3 = _inproj(x2, mod3, jnp.swapaxes(w_in, 1, 2), w_conv[l], b_conv[l].reshape(1, -1), wa_pad,
                         b_gla_a[l].reshape(1, -1), bg, S=S, tm=tm_in, lm=lm, layer=l)

    u_tri = jnp.asarray(np.triu(np.ones((lm, lm), np.float32)))
    nb = 4 if B % 4 == 0 else (2 if B % 2 == 0 else 1)
    ts = min(512, S)
    hm = _mlstm(oa, g3, u_tri, mlstm_norm_g[l].reshape(1, -1), B=B, S=S, L=lm, nb=nb, ts=ts)
    w3_np, mk_np = _gla_consts()
    hg = _gla(oa, la, jnp.asarray(w3_np, BF16), jnp.asarray(mk_np), gla_norm_g[l].reshape(1, -1), B=B, S=S,
              nb=nb, ts=ts)

    br = (jnp.zeros((1, LANES), F32).at[0, 0:N_GROUPS].set(b_route_group[l])
          .at[0, SUBLANES:SUBLANES + N_EXP].set(b_route_expert[l]))
    x1, u2, rrow = _outproj(hm, hg, w_out[l], x2, mod3, ln1_g[l].reshape(1, -1), ln1_b[l].reshape(1, -1),
                            jnp.swapaxes(w_route_group, 1, 2), jnp.swapaxes(w_route_expert, 1, 2), br,
                            S=S, tb=tm, nh=4 if S % (4 * tm) == 0 else 1, layer=l)

    u_cnt = jnp.asarray(np.triu(np.ones((tm, tm), np.float32)), BF16)
    ltri = jnp.asarray(np.tril(np.ones((N_EXP, N_EXP), np.float32), -1))
    srow, col, gd3, meta = _route(rrow, u_cnt, ltri, TM=FFN_TM)
    gd = gd3.reshape(N // tm + 2, LANES)
    n_tiles = _ffn_tiles(N, tm)
    te, nv, vr = meta[0, :n_tiles], meta[1, 0:1], meta[2, :n_tiles]

    xs = _dispatch(gd, srow, u2, n_tiles=n_tiles, TM=FFN_TM)
    ys = _ffn(te, nv, vr, xs, w_gate[l], w_up[l], w_down[l], TM=FFN_TM)
    out = _combine(gd, ys, col, x1, mod3, ln2_g[l].reshape(1, -1), ln2_b[l].reshape(1, -1), S=S, tb=tm)
    return out.reshape(B, S, D)


def kernel(x, c, w_ada, b_ada, w_in, w_conv, b_conv, b_igate, b_fgate, mlstm_norm_g, w_gla_a, b_gla_a,
           gla_norm_g, w_out, ln1_g, ln1_b, w_route_group, b_route_group, w_route_expert, b_route_expert,
           w_gate, w_up, w_down, ln2_g, ln2_b):
    for l in range(DEPTH):
        x = _layer(x, c, l, w_ada, b_ada, w_in, w_conv, b_conv, b_igate, b_fgate, mlstm_norm_g, w_gla_a,
                   b_gla_a, gla_norm_g, w_out, ln1_g, ln1_b, w_route_group, b_route_group, w_route_expert,
                   b_route_expert, w_gate, w_up, w_down, ln2_g, ln2_b)
    return x
```

```python
import functools

import numpy as np
import jax
import jax.numpy as jnp
from jax import lax
from jax.experimental import pallas as pl
from jax.experimental.pallas import tpu as pltpu

F32 = jnp.float32
BF16 = jnp.bfloat16
U32 = jnp.uint32
HIGHEST = lax.Precision.HIGHEST

DEPTH = 1
M_HEADS = 4
M_HD = 128
M_W = M_HEADS * M_HD
CONV_W = 4
G_HEADS = 4
G_DK = 64
G_DV = 128
G_W = G_HEADS * G_DV
G_KW = G_HEADS * G_DK
G_RANK = 16
G_TAU = 16.0
G_CHUNK = 64
N_GROUPS = 4
E_PER_G = 8
N_EXP = N_GROUPS * E_PER_G
D_EXP = 512
ALPHA = (2 * DEPTH) ** 0.25
LN_EPS = 1e-5

LANES = 128
SUBLANES = 8
VMEM_LIMIT = 48 * 1024 * 1024

C_QK = 0
C_VO = 1024
C_GQK = 2048
C_GV = 2560
C_GG = 3072
C_SMALL = 3584
C_TOT = 3712
SM_I, SM_F, SM_A = 0, 8, 16
IN_GATES = 4 * M_W
IN_G = IN_GATES + 2 * M_HEADS
IN_GA = IN_G + 2 * G_KW + 2 * G_W
IN_TOT = IN_GA + G_RANK

FFN_TM = 512
FFN_SUB = 2
GRAN = SUBLANES
G_LAST = LANES - 1


def _cparams(n_axes=1):
    return pltpu.CompilerParams(dimension_semantics=("arbitrary",) * n_axes,
                                vmem_limit_bytes=VMEM_LIMIT)


def _sigmoid(x):
    return 1.0 / (1.0 + jnp.exp(-x))


def _log_sigmoid(x):
    return jnp.minimum(x, 0.0) - jnp.log(1.0 + jnp.exp(-jnp.abs(x)))


def _ada_kernel(c_ref, w_ref, b_ref, o_ref):
    c = c_ref[...]
    ca = (c * _sigmoid(c)).astype(BF16)
    o_ref[...] = jnp.dot(ca, w_ref[...].astype(BF16), preferred_element_type=F32) + b_ref[...]


def _ada(c, w, b):
    B, D = c.shape
    n_out = w.shape[1]
    tn = 1024
    return pl.pallas_call(
        _ada_kernel,
        grid=(n_out // tn,),
        in_specs=[pl.BlockSpec((B, D), lambda j: (0, 0)),
                  pl.BlockSpec((D, tn), lambda j: (0, j)),
                  pl.BlockSpec((1, tn), lambda j: (0, j))],
        out_specs=pl.BlockSpec((B, tn), lambda j: (0, j)),
        out_shape=jax.ShapeDtypeStruct((B, n_out), F32),
        compiler_params=_cparams(),
        name="ada",
    )(c, w, b.reshape(1, n_out))


def _inproj_kernel(x_ref, mod_ref, win_ref, wc_ref, bc_ref, wa_ref, ba_ref, bg_ref,
                   oa_ref, la_ref, g_ref, halo_ref, w_ref, *, tm, tpb, lm):
    i = pl.program_id(0)

    @pl.when(i == 0)
    def _():
        rc = 2 * LANES
        for r in range(0, IN_GATES, rc):
            w_ref[:, r:r + rc] = win_ref[0, r:r + rc, :].T.astype(BF16)
        for r in range(0, C_SMALL - C_GQK, rc):
            w_ref[:, C_GQK + r:C_GQK + r + rc] = win_ref[0, IN_G + r:IN_G + r + rc, :].T.astype(BF16)
        gates = win_ref[0, IN_GATES:IN_G, :]
        z = lambda n: jnp.zeros((n, gates.shape[1]), F32)
        small = jnp.concatenate([gates[0:M_HEADS], z(SM_F - M_HEADS), gates[M_HEADS:2 * M_HEADS],
                                 z(SM_A - SM_F - M_HEADS), win_ref[0, IN_GA:IN_TOT, :],
                                 z(LANES - SM_A - G_RANK)], axis=0)
        w_ref[:, C_SMALL:C_TOT] = small.T.astype(BF16)

    @pl.when(i % tpb == 0)
    def _():
        halo_ref[0:SUBLANES, :] = jnp.zeros((SUBLANES, halo_ref.shape[1]), F32)

    mod = mod_ref[0]
    u = (x_ref[...] * (1.0 + mod[1:2, :]) + mod[0:1, :]).astype(BF16)

    def proj(c0, c1):
        return jnp.dot(u, w_ref[:, c0:c1], preferred_element_type=F32)

    p = proj(C_QK, C_QK + 2 * M_W)
    halo_ref[SUBLANES:SUBLANES + tm, :] = p
    acc = bc_ref[...] + wc_ref[CONV_W - 1:CONV_W, :] * p
    for j in range(CONV_W - 1):
        acc = acc + wc_ref[j:j + 1, :] * halo_ref[pl.ds(SUBLANES - (CONV_W - 1) + j, tm), :]
    halo_ref[0:SUBLANES, :] = p[tm - SUBLANES:, :]
    qk = acc * _sigmoid(acc)
    oa_ref[:, C_QK:C_QK + M_W] = qk[:, :M_W].astype(BF16)
    oa_ref[:, C_QK + M_W:C_QK + 2 * M_W] = (qk[:, M_W:] * (M_HD ** -0.5)).astype(BF16)

    ps = proj(C_SMALL, C_TOT)
    la = jnp.dot(ps.astype(BF16), wa_ref[...], preferred_element_type=F32) + ba_ref[...]
    la_ref[...] = _log_sigmoid(la) * (1.0 / G_TAU)
    pt = ps.T
    gi = pt[SM_I:SM_I + SUBLANES, :] + bg_ref[0:SUBLANES, :]
    gf = _log_sigmoid(pt[SM_F:SM_F + SUBLANES, :] + bg_ref[SUBLANES:2 * SUBLANES, :])
    for j in range(tm // lm):
        g_ref[j, 0:SUBLANES, :] = gi[:, j * lm:(j + 1) * lm]
        g_ref[j, SUBLANES:2 * SUBLANES, :] = gf[:, j * lm:(j + 1) * lm]

    p = proj(C_VO, C_VO + 2 * M_W)
    oa_ref[:, C_VO:C_VO + 2 * M_W] = p.astype(BF16)

    p = proj(C_GQK, C_GQK + G_KW)
    oa_ref[:, C_GQK:C_GQK + G_KW] = (p * (G_DK ** -0.5)).astype(BF16)
    p = proj(C_GQK + G_KW, C_SMALL)
    oa_ref[:, C_GQK + G_KW:C_SMALL] = p.astype(BF16)


def _inproj(x2, mod3, w_in, w_conv, b_conv, wa_pad, b_gla, bg, *, S, tm, lm, layer):
    N, D = x2.shape
    tpb = S // tm
    kern = functools.partial(_inproj_kernel, tm=tm, tpb=tpb, lm=lm)
    return pl.pallas_call(
        kern,
        grid=(N // tm,),
        in_specs=[pl.BlockSpec((tm, D), lambda i: (i, 0)),
                  pl.BlockSpec((1, 6, D), lambda i: (i // tpb, 0, 0)),
                  pl.BlockSpec((1, IN_TOT, D), lambda i: (layer, 0, 0), pipeline_mode=pl.Buffered(1)),
                  pl.BlockSpec((CONV_W, 2 * M_W), lambda i: (0, 0)),
                  pl.BlockSpec((1, 2 * M_W), lambda i: (0, 0)),
                  pl.BlockSpec((LANES, G_KW), lambda i: (0, 0)),
                  pl.BlockSpec((1, G_KW), lambda i: (0, 0)),
                  pl.BlockSpec((2 * SUBLANES, 1), lambda i: (0, 0))],
        out_specs=[pl.BlockSpec((tm, C_SMALL), lambda i: (i, 0)),
                   pl.BlockSpec((tm, G_KW), lambda i: (i, 0)),
                   pl.BlockSpec((tm // lm, 2 * SUBLANES, lm), lambda i: (i, 0, 0))],
        out_shape=[jax.ShapeDtypeStruct((N, C_SMALL), BF16),
                   jax.ShapeDtypeStruct((N, G_KW), F32),
                   jax.ShapeDtypeStruct((N // lm, 2 * SUBLANES, lm), F32)],
        scratch_shapes=[pltpu.VMEM((SUBLANES + tm, 2 * M_W), F32), pltpu.VMEM((D, C_TOT), BF16)],
        compiler_params=_cparams(),
        name="inproj",
    )(x2, mod3, w_in, w_conv, b_conv, wa_pad, b_gla, bg)


def _mlstm_sel():
    sel = np.zeros((2 * LANES, 2 * M_HEADS * M_HD), np.float32)
    for j in range(2 * M_HEADS):
        src = (SUBLANES if j < M_HEADS else 3 * SUBLANES) + j % M_HEADS
        sel[src, M_HD * j:M_HD * (j + 1)] = 1.0
        sel[LANES + src, M_HD * j:M_HD * (j + 1)] = 1.0
    return sel


def _mlstm_kernel(qk_ref, vo_ref, g_ref, u_ref, gain_ref, sel_ref, out_ref, c_ref, zt_ref, a_ref, dec_ref, m_ref,
                  *, L, NC, nb):
    @pl.when(pl.program_id(1) == 0)
    def _():
        c_ref[...] = jnp.zeros_like(c_ref)
        m_ref[...] = jnp.zeros_like(m_ref)

    tril = (lax.broadcasted_iota(jnp.int32, (L, L), 0) >= lax.broadcasted_iota(jnp.int32, (L, L), 1))
    ones_v = jnp.ones((L, M_HD), BF16)
    zpad = jnp.zeros((LANES - 4 * SUBLANES, L), F32)
    zgroup = lax.broadcasted_iota(jnp.int32, (L, LANES), 1) // SUBLANES
    factor_cols = (zgroup == 1) | (zgroup == 3)

    order = [(bi, c) for bi in range(nb) for c in range(NC)]
    f_all = jnp.concatenate([g_ref[bi, c, SUBLANES:2 * SUBLANES, :] for bi, c in order], axis=0)
    i_all = jnp.concatenate([g_ref[bi, c, 0:SUBLANES, :] for bi, c in order], axis=0)
    b_all = jnp.dot(f_all, u_ref[...], preferred_element_type=F32, precision=HIGHEST)
    a_all = i_all - b_all
    lane_all = lax.broadcasted_iota(jnp.int32, a_all.shape, 1)
    g_all = a_all
    s = 1
    while s < L:
        g_all = jnp.maximum(g_all, jnp.where(lane_all >= s, pltpu.roll(g_all, s, 1), -jnp.inf))
        s *= 2
    for bi in range(nb):
        m_prev = m_ref[bi][:, 0:1]
        for c in range(NC):
            ci = bi * NC + c
            r8 = slice(SUBLANES * ci, SUBLANES * (ci + 1))
            a, b = a_all[r8], b_all[r8]
            a_ref[ci] = a
            M = jnp.maximum(g_all[r8], m_prev)
            ML = M[:, L - 1:L]
            Z = jnp.concatenate([M, jnp.exp(m_prev - M), jnp.exp(-(b + M)), jnp.exp(a - ML), zpad],
                                axis=0)
            zt_ref[ci] = Z.T
            dec_ref[ci] = jnp.broadcast_to(jnp.exp(m_prev - ML), (SUBLANES, 2 * M_HD))
            m_prev = b[:, L - 1:L] + ML
        m_ref[bi] = jnp.broadcast_to(m_prev, (SUBLANES, LANES))

    chains = [(bi, h) for bi in range(nb) for h in range(M_HEADS)]
    nt = (((1,), (1,)), ((), ()))
    tn = (((0,), (0,)), ((), ()))

    def chunk(c, carry):
        rows = pl.ds(pl.multiple_of(c * L, L), L)
        Zt = [zt_ref[bi * NC + c] for bi in range(nb)]
        a = [a_ref[bi * NC + c] for bi in range(nb)]
        dec = [dec_ref[bi * NC + c] for bi in range(nb)]
        hs = [slice(h * M_HD, (h + 1) * M_HD) for h in range(M_HEADS)]
        hs2 = [slice(M_W + h * M_HD, M_W + (h + 1) * M_HD) for h in range(M_HEADS)]
        q = [qk_ref[bi, rows, hs[h]] for bi, h in chains]
        k = [qk_ref[bi, rows, hs2[h]] for bi, h in chains]
        vext = [jnp.concatenate([vo_ref[bi, rows, hs[h]], ones_v], axis=1) for bi, h in chains]
        cst = [c_ref[bi * M_HEADS + h] for bi, h in chains]
        n = range(len(chains))
        sc = [lax.dot_general(q[i], k[i], nt, preferred_element_type=F32) for i in n]
        qc = [jnp.dot(q[i], cst[i].astype(BF16), preferred_element_type=F32) for i in n]
        pm = [(sc[i] * jnp.exp(jnp.where(tril, a[bi][h:h + 1, :] - Zt[bi][:, h:h + 1], -jnp.inf))).astype(BF16)
              for i, (bi, h) in enumerate(chains)]
        pv = [jnp.dot(pm[i], vext[i], preferred_element_type=F32) for i in n]
        rep = []
        for bi in range(nb):
            zf = jnp.where(factor_cols, Zt[bi], 0.0)
            zh = zf.astype(BF16)
            zl = (zf - zh.astype(F32)).astype(BF16)
            rep.append(jnp.dot(jnp.concatenate([zh, zl], axis=1), sel_ref[...], preferred_element_type=F32))
        e_inter = [rep[bi][:, M_HD * h:M_HD * (h + 1)] for bi, h in chains]
        w_state = [rep[bi][:, M_HD * (M_HEADS + h):M_HD * (M_HEADS + h + 1)] for bi, h in chains]
        kw = [(w_state[i] * k[i].astype(F32)).astype(BF16) for i in n]
        upd = [lax.dot_general(kw[i], vext[i], tn, preferred_element_type=F32) for i in n]
        for i, (bi, h) in enumerate(chains):
            c_ref[bi * M_HEADS + h] = dec[bi][h:h + 1, :] * cst[i] + upd[i]
            nd = pv[i] + jnp.concatenate([e_inter[i], e_inter[i]], axis=1) * qc[i]
            hh = nd[:, :M_HD] / jnp.maximum(jnp.abs(nd[:, M_HD:]),
                                            Zt[bi][:, 2 * SUBLANES + h:2 * SUBLANES + h + 1])
            hh = _sigmoid(vo_ref[bi, rows, hs2[h]].astype(F32)) * hh
            hn = hh * lax.rsqrt(jnp.mean(hh * hh, axis=-1, keepdims=True) + LN_EPS)
            out_ref[bi, rows, hs[h]] = (hn * gain_ref[:, hs[h]]).astype(BF16)
        return carry

    lax.fori_loop(0, NC, chunk, 0)


def _mlstm(oa, g3, u_tri, gain, *, B, S, L, nb, ts):
    N = oa.shape[0]
    NC = ts // L
    oa3 = oa.reshape(B, S, oa.shape[1])
    g4 = g3.reshape(B, S // L, 2 * SUBLANES, L)
    sel = jnp.asarray(_mlstm_sel(), BF16)
    kern = functools.partial(_mlstm_kernel, L=L, NC=NC, nb=nb)
    out = pl.pallas_call(
        kern,
        grid=(B // nb, S // ts),
        in_specs=[pl.BlockSpec((nb, ts, 2 * M_W), lambda b, t: (b, t, C_QK // (2 * M_W))),
                  pl.BlockSpec((nb, ts, 2 * M_W), lambda b, t: (b, t, C_VO // (2 * M_W))),
                  pl.BlockSpec((nb, NC, 2 * SUBLANES, L), lambda b, t: (b, t, 0, 0)),
                  pl.BlockSpec((L, L), lambda b, t: (0, 0)),
                  pl.BlockSpec((1, M_W), lambda b, t: (0, 0)),
                  pl.BlockSpec(sel.shape, lambda b, t: (0, 0))],
        out_specs=pl.BlockSpec((nb, ts, M_W), lambda b, t: (b, t, 0)),
        out_shape=jax.ShapeDtypeStruct((B, S, M_W), BF16),
        scratch_shapes=[pltpu.VMEM((nb * M_HEADS, M_HD, 2 * M_HD), F32),
                        pltpu.VMEM((nb * NC, L, LANES), F32),
                        pltpu.VMEM((nb * NC, SUBLANES, L), F32),
                        pltpu.VMEM((nb * NC, SUBLANES, 2 * M_HD), F32),
                        pltpu.VMEM((nb, SUBLANES, LANES), F32)],
        compiler_params=_cparams(2),
        name="mlstm",
    )(oa3, oa3, g4, u_tri, gain, sel)
    return out.reshape(N, M_W)


_G_LEVELS = 6
_G_XROW = 2 * G_CHUNK + SUBLANES


def _gla_consts():
    L = G_CHUNK
    t = np.arange(L)
    blocks = [(t[None, :] <= t[:, None]).astype(np.float32),
              (t[None, :] > t[:, None]).astype(np.float32),
              np.ones((SUBLANES, L), np.float32)]
    masks = [np.eye(L, dtype=np.float32)]
    m = 1
    while m < L:
        wl = np.zeros((L, L), np.float32)
        for r in range(L):
            r0 = (r // (2 * m)) * 2 * m + m
            if r % (2 * m) >= m:
                wl[r, r0:r + 1] = 1.0
            else:
                wl[r, r + 1:r0] = 1.0
        blocks.append(wl)
        tt, ss = t[:, None], t[None, :]
        masks.append(((tt // (2 * m) == ss // (2 * m)) & (tt % (2 * m) >= m)
                      & (ss % (2 * m) < m)).astype(np.float32))
        m *= 2
    w = np.concatenate(blocks, axis=0)
    w3 = np.concatenate([w, w, w], axis=1)
    mk = np.stack([np.concatenate([x] * G_HEADS, axis=0) for x in masks])
    return w3, mk


def _gla_kernel(qk_ref, v_ref, gg_ref, la_ref, w3_ref, mk_ref, gain_ref, out_ref, st_ref, *, NC, nb):
    L = G_CHUNK

    @pl.when(pl.program_id(1) == 0)
    def _():
        st_ref[...] = jnp.zeros_like(st_ref)

    lane_head = lax.broadcasted_iota(jnp.int32, (L, G_KW), 1) // G_DK
    br = lax.broadcasted_iota(jnp.int32, (2 * G_DV, LANES), 0) < G_DV
    bl = lax.broadcasted_iota(jnp.int32, (2 * G_DV, LANES), 1) < G_DK
    bmask = br == bl
    nt = (((1,), (1,)), ((), ()))
    tn = (((0,), (0,)), ((), ()))

    def chunk(c, carry):
        rows = pl.ds(pl.multiple_of(c * L, L), L)
        X, q, k = [], [], []
        for bi in range(nb):
            la = la_ref[bi, rows, :]
            hi = la.astype(BF16)
            r1 = la - hi.astype(F32)
            mid = r1.astype(BF16)
            lo = (r1 - mid.astype(F32)).astype(BF16)
            stk = jnp.concatenate([hi, mid, lo], axis=0)
            X.append(jnp.exp(jnp.dot(w3_ref[...], stk, preferred_element_type=F32)))
            q.append(qk_ref[bi, rows, 0:G_KW].astype(F32))
            k.append(qk_ref[bi, rows, G_KW:2 * G_KW].astype(F32))

        sc = [[None] * (_G_LEVELS + 1) for _ in range(nb)]
        for lev in range(_G_LEVELS + 1):
            for bi in range(nb):
                if lev == 0:
                    qt, kt = q[bi], k[bi]
                else:
                    xl = X[bi][_G_XROW + L * (lev - 1):_G_XROW + L * lev, :]
                    qt, kt = q[bi] * xl, k[bi] * xl
                q4 = jnp.concatenate([jnp.where(lane_head == h, qt, 0.0) for h in range(G_HEADS)],
                                     axis=0).astype(BF16)
                sc[bi][lev] = lax.dot_general(q4, kt.astype(BF16), nt, preferred_element_type=F32)
        Ab = []
        for bi in range(nb):
            A = sc[bi][0] * mk_ref[0]
            for lev in range(1, _G_LEVELS + 1):
                A = A + sc[bi][lev] * mk_ref[lev]
            Ab.append(A.astype(BF16))

        for bi in range(nb):
            gg = gg_ref[bi, rows, :].astype(F32)
            gate = gg * _sigmoid(gg)
            for p in range(2):
                ls = slice(LANES * p, LANES * (p + 1))
                vp = v_ref[bi, rows, 2 * G_DV * p:2 * G_DV * (p + 1)]
                oi = [jnp.dot(Ab[bi][L * (2 * p + hh):L * (2 * p + hh + 1)],
                              vp[:, G_DV * hh:G_DV * (hh + 1)], preferred_element_type=F32)
                      for hh in range(2)]
                st = st_ref[bi, p]
                qc = (q[bi][:, ls] * X[bi][0:L, ls]).astype(BF16)
                o_inter = lax.dot_general(qc, st.astype(BF16), nt, preferred_element_type=F32)
                kc = (k[bi][:, ls] * X[bi][L:2 * L, ls]).astype(BF16)
                upd = lax.dot_general(vp, kc, tn, preferred_element_type=F32)
                dec = X[bi][2 * L:2 * L + 1, ls]
                st_ref[bi, p] = jnp.where(bmask, dec * st + upd, 0.0)
                for hh in range(2):
                    o = o_inter[:, G_DV * hh:G_DV * (hh + 1)] + oi[hh]
                    hn = o * lax.rsqrt(jnp.mean(o * o, axis=-1, keepdims=True) + LN_EPS)
                    hs = slice(G_DV * (2 * p + hh), G_DV * (2 * p + hh + 1))
                    out_ref[bi, rows, hs] = (hn * gain_ref[:, hs] * gate[:, hs]).astype(BF16)
        return carry

    lax.fori_loop(0, NC, chunk, 0)


def _gla(oa, la, w3, mk, gain, *, B, S, nb, ts):
    N = oa.shape[0]
    oa3 = oa.reshape(B, S, oa.shape[1])
    la3 = la.reshape(B, S, G_KW)
    kern = functools.partial(_gla_kernel, NC=ts // G_CHUNK, nb=nb)
    out = pl.pallas_call(
        kern,
        grid=(B // nb, S // ts),
        in_specs=[pl.BlockSpec((nb, ts, 2 * G_KW), lambda b, t: (b, t, C_GQK // (2 * G_KW))),
                  pl.BlockSpec((nb, ts, G_W), lambda b, t: (b, t, C_GV // G_W)),
                  pl.BlockSpec((nb, ts, G_W), lambda b, t: (b, t, C_GG // G_W)),
                  pl.BlockSpec((nb, ts, G_KW), lambda b, t: (b, t, 0)),
                  pl.BlockSpec(w3.shape, lambda b, t: (0, 0)),
                  pl.BlockSpec(mk.shape, lambda b, t: (0, 0, 0)),
                  pl.BlockSpec((1, G_W), lambda b, t: (0, 0))],
        out_specs=pl.BlockSpec((nb, ts, G_W), lambda b, t: (b, t, 0)),
        out_shape=jax.ShapeDtypeStruct((B, S, G_W), BF16),
        scratch_shapes=[pltpu.VMEM((nb, 2, 2 * G_DV, LANES), F32)],
        compiler_params=_cparams(2),
        name="gla",
    )(oa3, oa3, oa3, la3, w3, mk, gain)
    return out.reshape(N, G_W)


def _layer_norm(z, g, b):
    mu = jnp.mean(z, axis=-1, keepdims=True)
    zc = z - mu
    var = jnp.mean(zc * zc, axis=-1, keepdims=True)
    return zc * lax.rsqrt(var + LN_EPS) * g + b


def _outproj_kernel(hm_ref, hg_ref, wf_ref, x_ref, mod_ref, g_ref, b_ref, wrg_ref, wre_ref, br_ref,
                    x1_ref, u2_ref, rrow_ref, w_ref, wr_ref, *, tb, nh):
    @pl.when(pl.program_id(0) == 0)
    def _():
        w_ref[...] = wf_ref[...].astype(BF16)
        z = lambda n: jnp.zeros((n, wrg_ref.shape[2]), F32)
        wt = jnp.concatenate([wrg_ref[0], z(SUBLANES - N_GROUPS), wre_ref[0],
                              z(LANES - SUBLANES - N_EXP)], axis=0).T
        hi = wt.astype(BF16)
        wr_ref[:, 0:LANES] = hi
        wr_ref[:, LANES:2 * LANES] = (wt - hi.astype(F32)).astype(BF16)

    mod = mod_ref[0]
    blocks = [slice(tb * j, tb * (j + 1)) for j in range(nh)]
    y = [jnp.dot(hm_ref[r, :], w_ref[0:M_W, :], preferred_element_type=F32)
         + jnp.dot(hg_ref[r, :], w_ref[M_W:M_W + G_W, :], preferred_element_type=F32) for r in blocks]
    u2 = []
    for j, r in enumerate(blocks):
        z = ALPHA * x_ref[r, :] + (1.0 + mod[2:3, :]) * y[j]
        x1 = _layer_norm(z, g_ref[...], b_ref[...])
        x1_ref[r, :] = x1
        u2.append(x1 * (1.0 + mod[4:5, :]) + mod[3:4, :])
        u2_ref[r, :] = u2[j].astype(BF16)

    u2h = [u.astype(BF16) for u in u2]
    u2l = [(u2[j] - u2h[j].astype(F32)).astype(BF16) for j in range(nh)]
    lh = [jnp.dot(u, wr_ref[...], preferred_element_type=F32) for u in u2h]
    ll = [jnp.dot(u, wr_ref[:, 0:LANES], preferred_element_type=F32) for u in u2l]
    for j in range(nh):
        logits = lh[j][:, 0:LANES] + lh[j][:, LANES:2 * LANES] + ll[j] + br_ref[...]
        rrow_ref[j] = _route_select(logits.T, tb)


def _route_select(lt, tm):
    row = lax.broadcasted_iota(jnp.int32, (SUBLANES, tm), 0)
    gl = jnp.where(row < N_GROUPS, lt[0:SUBLANES, :], -jnp.inf)
    gmax = jnp.max(gl, axis=0, keepdims=True)
    gsel = jnp.min(jnp.where(gl == gmax, row, SUBLANES), axis=0, keepdims=True)
    pg = 1.0 / jnp.sum(jnp.exp(gl - gmax), axis=0, keepdims=True)
    ein = jnp.zeros((SUBLANES, tm), F32)
    for g in range(N_GROUPS):
        ein = jnp.where(gsel == g, lt[SUBLANES * (g + 1):SUBLANES * (g + 2), :], ein)
    v1 = jnp.max(ein, axis=0, keepdims=True)
    i1 = jnp.min(jnp.where(ein == v1, row, SUBLANES), axis=0, keepdims=True)
    rest = jnp.where(row == i1, -jnp.inf, ein)
    v2 = jnp.max(rest, axis=0, keepdims=True)
    i2 = jnp.min(jnp.where(rest == v2, row, SUBLANES), axis=0, keepdims=True)
    t2 = jnp.exp(v2 - v1)
    p1 = 1.0 / (1.0 + t2)
    e0 = (gsel * E_PER_G + i1).astype(F32)
    e1 = (gsel * E_PER_G + i2).astype(F32)
    return jnp.concatenate([e0, e1, pg * p1, pg * (t2 * p1), jnp.zeros((SUBLANES - 4, tm), F32)], axis=0)


def _outproj(hm, hg, w_out, x2, mod3, g, b, wrg_t, wre_t, br, *, S, tb, nh, layer):
    N, D = x2.shape
    tm = tb * nh
    tpb = S // tm
    kern = functools.partial(_outproj_kernel, tb=tb, nh=nh)
    return pl.pallas_call(
        kern,
        grid=(N // tm,),
        in_specs=[pl.BlockSpec((tm, M_W), lambda i: (i, 0)),
                  pl.BlockSpec((tm, G_W), lambda i: (i, 0)),
                  pl.BlockSpec((M_W + G_W, D), lambda i: (0, 0), pipeline_mode=pl.Buffered(1)),
                  pl.BlockSpec((tm, D), lambda i: (i, 0)),
                  pl.BlockSpec((1, 6, D), lambda i: (i // tpb, 0, 0)),
                  pl.BlockSpec((1, D), lambda i: (0, 0)),
                  pl.BlockSpec((1, D), lambda i: (0, 0)),
                  pl.BlockSpec((1, N_GROUPS, D), lambda i: (layer, 0, 0)),
                  pl.BlockSpec((1, N_EXP, D), lambda i: (layer, 0, 0)),
                  pl.BlockSpec((1, LANES), lambda i: (0, 0))],
        out_specs=[pl.BlockSpec((tm, D), lambda i: (i, 0)),
                   pl.BlockSpec((tm, D), lambda i: (i, 0)),
                   pl.BlockSpec((nh, SUBLANES, tb), lambda i: (i, 0, 0))],
        out_shape=[jax.ShapeDtypeStruct((N, D), F32),
                   jax.ShapeDtypeStruct((N, D), BF16),
                   jax.ShapeDtypeStruct((N // tb, SUBLANES, tb), F32)],
        scratch_shapes=[pltpu.VMEM((M_W + G_W, D), BF16), pltpu.VMEM((D, 2 * LANES), BF16)],
        compiler_params=_cparams(),
        name="outproj",
    )(hm, hg, w_out, x2, mod3, g, b, wrg_t, wre_t, br)


def _slots_per_tile(tb):
    worst = 2 * tb + N_EXP * (GRAN - 1)
    return -(-worst // LANES) * LANES


def _ffn_tiles(n_tok, tb):
    worst_rows = 2 * n_tok + (n_tok // tb) * N_EXP * (GRAN - 1)
    return -(-worst_rows // FFN_TM) + N_EXP


def _route_kernel(rr_ref, u_ref, lt_ref, srow_ref, col_ref, gd_ref, meta_ref, mg_ref, part_ref,
                  *, NT, tb, TM):
    iota_e = lax.broadcasted_iota(jnp.int32, (N_EXP, tb), 0).astype(F32)
    glane = lax.broadcasted_iota(jnp.int32, (N_EXP, LANES), 1).astype(F32)
    ltri = lt_ref[...]

    def prefix_e(col):
        return jnp.dot(ltri, jnp.broadcast_to(col, (N_EXP, LANES)),
                       preferred_element_type=F32, precision=HIGHEST)[:, 0:1]

    def p1(j, run8):
        r = rr_ref[j]
        oh0 = jnp.where(iota_e == r[0:1, :], 1.0, 0.0)
        oh1 = jnp.where(iota_e == r[1:2, :], 1.0, 0.0)
        cum0 = jnp.dot(oh0.astype(BF16), u_ref[...], preferred_element_type=F32)
        cum1 = jnp.dot(oh1.astype(BF16), u_ref[...], preferred_element_type=F32)
        c0 = jnp.sum(oh0, axis=1, keepdims=True)
        n8 = jnp.floor((c0 + jnp.sum(oh1, axis=1, keepdims=True) + (GRAN - 1.0)) * (1.0 / GRAN))
        lo8 = prefix_e(n8)
        s0 = jnp.sum(oh0 * (GRAN * lo8 + cum0 - 1.0), axis=0, keepdims=True)
        s1 = jnp.sum(oh1 * (GRAN * lo8 + c0 + cum1 - 1.0), axis=0, keepdims=True)
        info = jnp.concatenate([s0, s1, r[2:4, :], jnp.zeros((SUBLANES - 4, tb), F32)], axis=0)
        srow_ref[j] = info
        col_ref[pl.ds(pl.multiple_of(j * tb, tb), tb), :] = jnp.concatenate(
            [info, jnp.zeros((LANES - SUBLANES, tb), F32)], axis=0).T
        mg = jnp.where((lo8 <= glane) & (glane < lo8 + n8), 1.0, 0.0)
        mg_ref[j] = mg
        part = jnp.sum(mg * (run8 + glane - lo8), axis=0, keepdims=True)
        gcnt = jnp.broadcast_to(jnp.sum(n8, axis=0, keepdims=True), (1, LANES))
        part_ref[j] = jnp.concatenate([part, gcnt, jnp.zeros((SUBLANES - 2, LANES), F32)], axis=0)
        return run8 + n8

    tot8 = lax.fori_loop(0, NT, p1, jnp.zeros((N_EXP, 1), F32), unroll=8 if NT % 8 == 0 else 1)
    seg_t = jnp.floor((tot8 * GRAN + (TM - 1.0)) * (1.0 / TM))
    base_t = prefix_e(seg_t)
    base8 = base_t * (TM // GRAN)
    lane1 = lax.broadcasted_iota(jnp.int32, (1, LANES), 1)

    def p2(j, carry):
        pr = part_ref[j]
        dst = (pr[0:1, :] + jnp.sum(mg_ref[j] * base8, axis=0, keepdims=True)) * GRAN
        gd_ref[j] = jnp.where(lane1 == G_LAST, pr[1:2, :], dst).astype(jnp.int32)
        return carry

    lax.fori_loop(0, NT, p2, 0, unroll=8 if NT % 8 == 0 else 1)
    eye = jnp.where(glane == lax.broadcasted_iota(jnp.int32, (N_EXP, LANES), 0).astype(F32), 1.0, 0.0)
    tail_row = jnp.sum(eye * ((base8 + tot8) * GRAN), axis=0, keepdims=True)
    tail_n8 = jnp.sum(eye * (seg_t * (TM // GRAN) - tot8), axis=0, keepdims=True)
    nv_l = jnp.broadcast_to(jnp.sum(seg_t, axis=0, keepdims=True), (1, LANES))
    gd_ref[NT] = jnp.where(lane1 == G_LAST, nv_l, tail_row).astype(jnp.int32)
    gd_ref[NT + 1] = tail_n8.astype(jnp.int32)
    ti = lax.broadcasted_iota(jnp.int32, (N_EXP, tb), 1).astype(F32)
    te = jnp.sum(jnp.where(base_t <= ti, 1.0, 0.0), axis=0, keepdims=True) - 1.0
    nv = jnp.broadcast_to(jnp.sum(seg_t, axis=0, keepdims=True), (1, tb))
    own = jnp.where((base_t <= ti) & (ti < base_t + seg_t), 1.0, 0.0)
    vr = jnp.sum(own * jnp.clip(tot8 * GRAN - (ti - base_t) * TM, 0.0, TM), axis=0, keepdims=True)
    meta_ref[...] = jnp.concatenate([te, nv, vr, jnp.zeros((SUBLANES - 3, tb), F32)],
                                    axis=0).astype(jnp.int32)


def _route(rrow, u_cnt, ltri, *, TM):
    NT, _, tb = rrow.shape
    kern = functools.partial(_route_kernel, NT=NT, tb=tb, TM=TM)
    full3 = lambda i: (0, 0, 0)
    return pl.pallas_call(
        kern,
        grid=(1,),
        in_specs=[pl.BlockSpec((NT, SUBLANES, tb), full3),
                  pl.BlockSpec((tb, tb), lambda i: (0, 0)),
                  pl.BlockSpec((N_EXP, N_EXP), lambda i: (0, 0))],
        out_specs=[pl.BlockSpec((NT, SUBLANES, tb), full3),
                   pl.BlockSpec((NT * tb, LANES), lambda i: (0, 0)),
                   pl.BlockSpec((NT + 2, 1, LANES), full3),
                   pl.BlockSpec((SUBLANES, tb), lambda i: (0, 0))],
        out_shape=[jax.ShapeDtypeStruct((NT, SUBLANES, tb), F32),
                   jax.ShapeDtypeStruct((NT * tb, LANES), F32),
                   jax.ShapeDtypeStruct((NT + 2, 1, LANES), jnp.int32),
                   jax.ShapeDtypeStruct((SUBLANES, tb), jnp.int32)],
        scratch_shapes=[pltpu.VMEM((NT, N_EXP, LANES), F32), pltpu.VMEM((NT, SUBLANES, LANES), F32)],
        compiler_params=_cparams(),
        name="route",
    )(rrow, u_cnt, ltri)


_HI_MASK = 0xFFFF0000


def _pack_halves(x):
    c = x.shape[1] // 2
    lo = lax.bitcast_convert_type(x[:, :c], U32)
    hi = lax.bitcast_convert_type(x[:, c:], U32)
    return (lo >> 16) | (hi & U32(_HI_MASK))


def _unpack_halves(w):
    lo = lax.bitcast_convert_type(w << 16, F32)
    hi = lax.bitcast_convert_type(w & U32(_HI_MASK), F32)
    return jnp.concatenate([lo, hi], axis=1).astype(BF16)


def _granule_copy(src_ref, src_row, dst_ref, dst_row, sem):
    return pltpu.make_async_copy(src_ref.at[pl.ds(src_row, GRAN), :], dst_ref.at[pl.ds(dst_row, GRAN), :], sem)


def _for_granules(n, body, unroll=4):
    def blk(i, carry):
        for t in range(unroll):
            body(i * unroll + t)
        return carry

    def one(g, carry):
        body(g)
        return carry

    nblk = n // unroll
    lax.fori_loop(0, nblk, blk, 0)
    lax.fori_loop(nblk * unroll, n, one, 0)


def _wait_granules(n, src_ref, dst_ref, sem, n_max):
    b = 1
    while b <= n_max:
        @pl.when((n & b) != 0)
        def _(b=b):
            pltpu.make_async_copy(src_ref.at[pl.ds(0, b * GRAN), :], dst_ref.at[pl.ds(0, b * GRAN), :],
                                  sem).wait()
        b *= 2


def _dispatch_kernel(gd_ref, srow_ref, u_ref, xs_ref, buf, zbuf, sems, *, NT, SL, TM, n_tiles):
    j = pl.program_id(0)
    slot = j % 2
    zsem = sems.at[2]

    def drain(tile, sl):
        _wait_granules(gd_ref[tile, G_LAST], buf.at[sl], xs_ref, sems.at[sl], SL // GRAN)

    def tile_fill(t):
        return pltpu.make_async_copy(zbuf, xs_ref.at[pl.ds(pl.multiple_of(t * TM, TM), TM), :], zsem)

    def zero_fill(wait):
        for e in range(N_EXP):
            n, row0 = gd_ref[NT + 1, e], gd_ref[NT, e]
            b = TM // GRAN // 2
            while b >= 1:
                @pl.when((n & b) != 0)
                def _(b=b, n=n, row0=row0):
                    start = pl.multiple_of(row0 + ((n >> b.bit_length()) << b.bit_length()) * GRAN, GRAN)
                    cp = pltpu.make_async_copy(zbuf.at[pl.ds(0, b * GRAN), :],
                                               xs_ref.at[pl.ds(start, b * GRAN), :], zsem)
                    cp.wait() if wait else cp.start()
                b //= 2

        def zt(t, carry):
            tile_fill(t).wait() if wait else tile_fill(t).start()
            return carry
        lax.fori_loop(gd_ref[NT, G_LAST], n_tiles, zt, 0)

    @pl.when(j == 0)
    def _():
        zbuf[...] = jnp.zeros_like(zbuf)
        zero_fill(False)

    @pl.when(j >= 2)
    def _():
        drain(j - 2, slot)

    s = srow_ref[0]
    rows = lax.broadcasted_iota(jnp.int32, (SL, s.shape[1]), 0).astype(F32)
    m0 = rows == s[0:1, :]
    m1 = rows == s[1:2, :]
    oh = jnp.where(m0 | m1, 1.0, 0.0).astype(BF16)
    dw = u_ref.shape[1] // 2
    buf[slot, :, 0:dw] = _pack_halves(jnp.dot(oh, u_ref[...], preferred_element_type=F32))
    wrow = jnp.sum(jnp.where(m0, s[2:3, :], 0.0) + jnp.where(m1, s[3:4, :], 0.0), axis=1, keepdims=True)
    buf[slot, :, dw:dw + LANES] = lax.bitcast_convert_type(jnp.broadcast_to(wrow, (SL, LANES)), U32)

    def issue(g):
        _granule_copy(buf.at[slot], pl.multiple_of(g * GRAN, GRAN), xs_ref,
                      pl.multiple_of(gd_ref[j, g], GRAN), sems.at[slot]).start()

    _for_granules(gd_ref[j, G_LAST], issue)

    @pl.when(j == NT - 1)
    def _():
        drain(j, slot)
        if NT > 1:
            drain(j - 1, 1 - slot)
        zero_fill(True)


def _dispatch(gd, srow, u2, *, n_tiles, TM):
    N, D = u2.shape
    NT, _, tb = srow.shape
    SL = _slots_per_tile(tb)
    n_rows = n_tiles * TM
    kern = functools.partial(_dispatch_kernel, NT=NT, SL=SL, TM=TM, n_tiles=n_tiles)
    grid_spec = pltpu.PrefetchScalarGridSpec(
        num_scalar_prefetch=1,
        grid=(NT,),
        in_specs=[pl.BlockSpec((1, SUBLANES, tb), lambda j, gd: (j, 0, 0)),
                  pl.BlockSpec((tb, D), lambda j, gd: (j, 0))],
        out_specs=pl.BlockSpec(memory_space=pl.ANY),
        scratch_shapes=[pltpu.VMEM((2, SL, D // 2 + LANES), U32), pltpu.VMEM((TM, D // 2 + LANES), U32),
                        pltpu.SemaphoreType.DMA((3,))],
    )
    return pl.pallas_call(
        kern,
        grid_spec=grid_spec,
        out_shape=jax.ShapeDtypeStruct((n_rows, D // 2 + LANES), U32),
        compiler_params=_cparams(),
        name="dispatch",
    )(gd, srow, u2)


def _ffn_kernel(te_ref, nv_ref, vr_ref, xs_ref, wg_ref, wu_ref, wd_ref, o_ref, wgb, wub, wdb, sg, su, sd, slot_ref,
                sems):
    i = pl.program_id(0)
    nv = nv_ref[0]
    e = te_ref[i]

    def weight_copies(ex, sl):
        return (pltpu.make_async_copy(wg_ref.at[ex], sg.at[sl], sems.at[sl]),
                pltpu.make_async_copy(wu_ref.at[ex], su.at[sl], sems.at[sl]),
                pltpu.make_async_copy(wd_ref.at[ex], sd.at[sl], sems.at[sl]))

    @pl.when(i == 0)
    def _():
        slot_ref[0] = 0
        for cp in weight_copies(e, 0):
            cp.start()

    @pl.when((i < nv) & ((i == 0) | (e != te_ref[jnp.maximum(i - 1, 0)])))
    def _():
        sl = slot_ref[0]
        for cp in weight_copies(e, sl):
            cp.wait()
        wgb[...] = sg[sl].astype(BF16)
        wub[...] = su[sl].astype(BF16)
        wdb[...] = sd[sl].astype(BF16)
        nxt = lax.while_loop(lambda t: (t < nv) & (te_ref[jnp.minimum(t, nv - 1)] == e), lambda t: t + 1, i + 1)

        @pl.when(nxt < nv)
        def _():
            for cp in weight_copies(te_ref[nxt], 1 - sl):
                cp.start()
        slot_ref[0] = 1 - sl

    hm = xs_ref.shape[0] // FFN_SUB
    dw = xs_ref.shape[1] - LANES

    def swiglu_rows(nsub):
        halves = tuple(slice(hm * j, hm * (j + 1)) for j in range(nsub))
        x = [_unpack_halves(xs_ref[r, 0:dw]) for r in halves]
        g = [jnp.dot(x[j], wgb[...], preferred_element_type=F32) for j in range(nsub)]
        u = [jnp.dot(x[j], wub[...], preferred_element_type=F32) for j in range(nsub)]
        h = [(g[j] * _sigmoid(g[j]) * u[j]).astype(BF16) for j in range(nsub)]
        y = [jnp.dot(h[j], wdb[...], preferred_element_type=F32) for j in range(nsub)]
        for j in range(nsub):
            wt = lax.bitcast_convert_type(xs_ref[halves[j], dw:dw + LANES], F32)
            yw = y[j] * jnp.concatenate([wt] * (2 * dw // LANES), axis=1)
            o_ref[halves[j], 0:dw] = _pack_halves(yw.astype(BF16).astype(F32))
            o_ref[halves[j], dw:dw + LANES] = xs_ref[halves[j], dw:dw + LANES]
        if nsub < FFN_SUB:
            o_ref[hm * nsub:, :] = jnp.zeros((hm * (FFN_SUB - nsub), dw + LANES), U32)

    used = vr_ref[i]
    for nsub in range(1, FFN_SUB + 1):
        lo, hi = hm * (nsub - 1), hm * nsub
        pl.when((i < nv) & (used > lo) & ((used <= hi) if nsub < FFN_SUB else True))(
            functools.partial(swiglu_rows, nsub))


def _ffn(te, nv, vr, xs, wg, wu, wd, *, TM):
    P, XW = xs.shape
    DW = XW - LANES
    D = 2 * DW
    n_tiles = P // TM
    used_tile = lambda i, te, nv, vr: (jnp.maximum(jnp.minimum(i, nv[0] - 1), 0), 0)
    grid_spec = pltpu.PrefetchScalarGridSpec(
        num_scalar_prefetch=3,
        grid=(n_tiles,),
        in_specs=[pl.BlockSpec((TM, XW), used_tile),
                  pl.BlockSpec(memory_space=pl.ANY),
                  pl.BlockSpec(memory_space=pl.ANY),
                  pl.BlockSpec(memory_space=pl.ANY)],
        out_specs=pl.BlockSpec((TM, XW), used_tile),
        scratch_shapes=[pltpu.VMEM((D, D_EXP), BF16), pltpu.VMEM((D, D_EXP), BF16),
                        pltpu.VMEM((D_EXP, D), BF16),
                        pltpu.VMEM((2, D, D_EXP), F32), pltpu.VMEM((2, D, D_EXP), F32),
                        pltpu.VMEM((2, D_EXP, D), F32), pltpu.SMEM((1,), jnp.int32),
                        pltpu.SemaphoreType.DMA((2,))],
    )
    return pl.pallas_call(
        _ffn_kernel,
        grid_spec=grid_spec,
        out_shape=jax.ShapeDtypeStruct((P, XW), U32),
        input_output_aliases={3: 0},
        compiler_params=_cparams(),
        name="ffn",
    )(te, nv, vr, xs, wg, wu, wd)


def _combine_kernel(gd_ref, ys_ref, col_ref, x1_ref, mod_ref, g_ref, b_ref, o_ref, buf, sems, *, NT, SL):
    j = pl.program_id(0)
    slot = j % 2

    def fetch(tile, sl):
        def f(g):
            _granule_copy(ys_ref, pl.multiple_of(gd_ref[tile, g], GRAN), buf.at[sl],
                          pl.multiple_of(g * GRAN, GRAN), sems.at[sl]).start()
        _for_granules(gd_ref[tile, G_LAST], f)

    @pl.when(j == 0)
    def _():
        fetch(0, 0)

    @pl.when(j + 1 < NT)
    def _():
        fetch(j + 1, 1 - slot)

    ng = gd_ref[j, G_LAST]

    _wait_granules(ng, ys_ref, buf.at[slot], sems.at[slot], SL // GRAN)

    rows = lax.broadcasted_iota(jnp.int32, (SL, 1), 0)
    dw = x1_ref.shape[1] // 2
    yb = _unpack_halves(jnp.where(rows < ng * GRAN, buf[slot, :, 0:dw], U32(0)))
    col = col_ref[...]
    tb = col.shape[0]
    lanes = lax.broadcasted_iota(jnp.int32, (tb, SL), 1).astype(F32)
    sel = jnp.where((lanes == col[:, 0:1]) | (lanes == col[:, 1:2]), 1.0, 0.0).astype(BF16)
    y = jnp.dot(sel, yb, preferred_element_type=F32)
    mod = mod_ref[0]
    z = ALPHA * x1_ref[...] + (1.0 + mod[5:6, :]) * y
    o_ref[...] = _layer_norm(z, g_ref[...], b_ref[...])


def _combine(gd, ys, col, x1, mod3, g, b, *, S, tb):
    N, D = x1.shape
    NT = N // tb
    tpb = S // tb
    SL = _slots_per_tile(tb)
    kern = functools.partial(_combine_kernel, NT=NT, SL=SL)
    grid_spec = pltpu.PrefetchScalarGridSpec(
        num_scalar_prefetch=1,
        grid=(NT,),
        in_specs=[pl.BlockSpec(memory_space=pl.ANY),
                  pl.BlockSpec((tb, LANES), lambda j, gd: (j, 0)),
                  pl.BlockSpec((tb, D), lambda j, gd: (j, 0)),
                  pl.BlockSpec((1, 6, D), lambda j, gd: (j // tpb, 0, 0)),
                  pl.BlockSpec((1, D), lambda j, gd: (0, 0)),
                  pl.BlockSpec((1, D), lambda j, gd: (0, 0))],
        out_specs=pl.BlockSpec((tb, D), lambda j, gd: (j, 0)),
        scratch_shapes=[pltpu.VMEM((2, SL, ys.shape[1]), U32), pltpu.SemaphoreType.DMA((2,))],
    )
    return pl.pallas_call(
        kern,
        grid_spec=grid_spec,
        out_shape=jax.ShapeDtypeStruct((N, D), F32),
        compiler_params=_cparams(),
        name="combine",
    )(gd, ys, col, x1, mod3, g, b)


def _layer(x, c, l, w_ada, b_ada, w_in, w_conv, b_conv, b_igate, b_fgate, mlstm_norm_g, w_gla_a, b_gla_a,
           gla_norm_g, w_out, ln1_g, ln1_b, w_route_group, b_route_group, w_route_expert, b_route_expert,
           w_gate, w_up, w_down, ln2_g, ln2_b):
    B, S, D = x.shape
    N = B * S
    x2 = x.reshape(N, D)
    tm_in = min(512, S)
    tm = min(256, S)
    lm = min(256, S)
    assert S % tm_in == 0 and S % tm == 0 and tm_in % lm == 0 and S % G_CHUNK == 0
    assert w_in.shape[1:] == (D, IN_TOT) and w_gate.shape[1:] == (N_EXP, D, D_EXP)

    mod3 = _ada(c, w_ada[l], b_ada[l]).reshape(B, 6, D)

    wa_pad = jnp.zeros((LANES, G_KW), F32).at[SM_A:SM_A + G_RANK].set(w_gla_a[l]).astype(BF16)
    bg = (jnp.zeros((2 * SUBLANES, 1), F32).at[0:M_HEADS, 0].set(b_igate[l])
          .at[SUBLANES:SUBLANES + M_HEADS, 0].set(b_fgate[l]))
    oa, la, g3 = _inproj(x2, mod3, jnp.swapaxes(w_in, 1, 2), w_conv[l], b_conv[l].reshape(1, -1), wa_pad,
                         b_gla_a[l].reshape(1, -1), bg, S=S, tm=tm_in, lm=lm, layer=l)

    u_tri = jnp.asarray(np.triu(np.ones((lm, lm), np.float32)))
    nb = 4 if B % 4 == 0 else (2 if B % 2 == 0 else 1)
    ts = min(512, S)
    hm = _mlstm(oa, g3, u_tri, mlstm_norm_g[l].reshape(1, -1), B=B, S=S, L=lm, nb=nb, ts=ts)
    w3_np, mk_np = _gla_consts()
    hg = _gla(oa, la, jnp.asarray(w3_np, BF16), jnp.asarray(mk_np), gla_norm_g[l].reshape(1, -1), B=B, S=S,
              nb=nb, ts=ts)

    br = (jnp.zeros((1, LANES), F32).at[0, 0:N_GROUPS].set(b_route_group[l])
          .at[0, SUBLANES:SUBLANES + N_EXP].set(b_route_expert[l]))
    x1, u2, rrow = _outproj(hm, hg, w_out[l], x2, mod3, ln1_g[l].reshape(1, -1), ln1_b[l].reshape(1, -1),
                            jnp.swapaxes(w_route_group, 1, 2), jnp.swapaxes(w_route_expert, 1, 2), br,
                            S=S, tb=tm, nh=4 if S % (4 * tm) == 0 else 1, layer=l)

    u_cnt = jnp.asarray(np.triu(np.ones((tm, tm), np.float32)), BF16)
    ltri = jnp.asarray(np.tril(np.ones((N_EXP, N_EXP), np.float32), -1))
    srow, col, gd3, meta = _route(rrow, u_cnt, ltri, TM=FFN_TM)
    gd = gd3.reshape(N // tm + 2, LANES)
    n_tiles = _ffn_tiles(N, tm)
    te, nv, vr = meta[0, :n_tiles], meta[1, 0:1], meta[2, :n_tiles]

    xs = _dispatch(gd, srow, u2, n_tiles=n_tiles, TM=FFN_TM)
    ys = _ffn(te, nv, vr, xs, w_gate[l], w_up[l], w_down[l], TM=FFN_TM)
    out = _combine(gd, ys, col, x1, mod3, ln2_g[l].reshape(1, -1), ln2_b[l].reshape(1, -1), S=S, tb=tm)
    return out.reshape(B, S, D)


def kernel(x, c, w_ada, b_ada, w_in, w_conv, b_conv, b_igate, b_fgate, mlstm_norm_g, w_gla_a, b_gla_a,
           gla_norm_g, w_out, ln1_g, ln1_b, w_route_group, b_route_group, w_route_expert, b_route_expert,
           w_gate, w_up, w_down, ln2_g, ln2_b):
    for l in range(DEPTH):
        x = _layer(x, c, l, w_ada, b_ada, w_in, w_conv, b_conv, b_igate, b_fgate, mlstm_norm_g, w_gla_a,
                   b_gla_a, gla_norm_g, w_out, ln1_g, ln1_b, w_route_group, b_route_group, w_route_expert,
                   b_route_expert, w_gate, w_up, w_down, ln2_g, ln2_b)
    return x
```

```python
import functools

import numpy as np
import jax
import jax.numpy as jnp
from jax import lax
from jax.experimental import pallas as pl
from jax.experimental.pallas import tpu as pltpu

F32 = jnp.float32
BF16 = jnp.bfloat16
U32 = jnp.uint32
HIGHEST = lax.Precision.HIGHEST

DEPTH = 1
M_HEADS = 4
M_HD = 128
M_W = M_HEADS * M_HD
CONV_W = 4
G_HEADS = 4
G_DK = 64
G_DV = 128
G_W = G_HEADS * G_DV
G_KW = G_HEADS * G_DK
G_RANK = 16
G_TAU = 16.0
G_CHUNK = 64
N_GROUPS = 4
E_PER_G = 8
N_EXP = N_GROUPS * E_PER_G
D_EXP = 512
ALPHA = (2 * DEPTH) ** 0.25
LN_EPS = 1e-5

LANES = 128
SUBLANES = 8
VMEM_LIMIT = 48 * 1024 * 1024

C_QK = 0
C_VO = 1024
C_GQK = 2048
C_GV = 2560
C_GG = 3072
C_SMALL = 3584
C_TOT = 3712
SM_I, SM_F, SM_A = 0, 8, 16
IN_GATES = 4 * M_W
IN_G = IN_GATES + 2 * M_HEADS
IN_GA = IN_G + 2 * G_KW + 2 * G_W
IN_TOT = IN_GA + G_RANK

FFN_TM = 512
FFN_SUB = 2
GRAN = SUBLANES
G_LAST = LANES - 1


def _cparams(n_axes=1):
    return pltpu.CompilerParams(dimension_semantics=("arbitrary",) * n_axes,
                                vmem_limit_bytes=VMEM_LIMIT)


def _sigmoid(x):
    return 1.0 / (1.0 + jnp.exp(-x))


def _log_sigmoid(x):
    return jnp.minimum(x, 0.0) - jnp.log(1.0 + jnp.exp(-jnp.abs(x)))


def _ada_kernel(c_ref, w_ref, b_ref, o_ref):
    c = c_ref[...]
    ca = (c * _sigmoid(c)).astype(BF16)
    o_ref[...] = jnp.dot(ca, w_ref[...].astype(BF16), preferred_element_type=F32) + b_ref[...]


def _ada(c, w, b):
    B, D = c.shape
    n_out = w.shape[1]
    tn = 1024
    return pl.pallas_call(
        _ada_kernel,
        grid=(n_out // tn,),
        in_specs=[pl.BlockSpec((B, D), lambda j: (0, 0)),
                  pl.BlockSpec((D, tn), lambda j: (0, j)),
                  pl.BlockSpec((1, tn), lambda j: (0, j))],
        out_specs=pl.BlockSpec((B, tn), lambda j: (0, j)),
        out_shape=jax.ShapeDtypeStruct((B, n_out), F32),
        compiler_params=_cparams(),
        name="ada",
    )(c, w, b.reshape(1, n_out))


def _inproj_kernel(x_ref, mod_ref, win_ref, wc_ref, bc_ref, wa_ref, ba_ref, bg_ref,
                   oa_ref, la_ref, g_ref, halo_ref, w_ref, *, tm, tpb, lm):
    i = pl.program_id(0)

    @pl.when(i == 0)
    def _():
        rc = 2 * LANES
        for r in range(0, IN_GATES, rc):
            w_ref[:, r:r + rc] = win_ref[0, r:r + rc, :].T.astype(BF16)
        for r in range(0, C_SMALL - C_GQK, rc):
            w_ref[:, C_GQK + r:C_GQK + r + rc] = win_ref[0, IN_G + r:IN_G + r + rc, :].T.astype(BF16)
        gates = win_ref[0, IN_GATES:IN_G, :]
        z = lambda n: jnp.zeros((n, gates.shape[1]), F32)
        small = jnp.concatenate([gates[0:M_HEADS], z(SM_F - M_HEADS), gates[M_HEADS:2 * M_HEADS],
                                 z(SM_A - SM_F - M_HEADS), win_ref[0, IN_GA:IN_TOT, :],
                                 z(LANES - SM_A - G_RANK)], axis=0)
        w_ref[:, C_SMALL:C_TOT] = small.T.astype(BF16)

    @pl.when(i % tpb == 0)
    def _():
        halo_ref[0:SUBLANES, :] = jnp.zeros((SUBLANES, halo_ref.shape[1]), F32)

    mod = mod_ref[0]
    u = (x_ref[...] * (1.0 + mod[1:2, :]) + mod[0:1, :]).astype(BF16)

    def proj(c0, c1):
        return jnp.dot(u, w_ref[:, c0:c1], preferred_element_type=F32)

    p = proj(C_QK, C_QK + 2 * M_W)
    halo_ref[SUBLANES:SUBLANES + tm, :] = p
    acc = bc_ref[...] + wc_ref[CONV_W - 1:CONV_W, :] * p
    for j in range(CONV_W - 1):
        acc = acc + wc_ref[j:j + 1, :] * halo_ref[pl.ds(SUBLANES - (CONV_W - 1) + j, tm), :]
    halo_ref[0:SUBLANES, :] = p[tm - SUBLANES:, :]
    qk = acc * _sigmoid(acc)
    oa_ref[:, C_QK:C_QK + M_W] = qk[:, :M_W].astype(BF16)
    oa_ref[:, C_QK + M_W:C_QK + 2 * M_W] = (qk[:, M_W:] * (M_HD ** -0.5)).astype(BF16)

    ps = proj(C_SMALL, C_TOT)
    la = jnp.dot(ps.astype(BF16), wa_ref[...], preferred_element_type=F32) + ba_ref[...]
    la_ref[...] = _log_sigmoid(la) * (1.0 / G_TAU)
    pt = ps.T
    gi = pt[SM_I:SM_I + SUBLANES, :] + bg_ref[0:SUBLANES, :]
    gf = _log_sigmoid(pt[SM_F:SM_F + SUBLANES, :] + bg_ref[SUBLANES:2 * SUBLANES, :])
    for j in range(tm // lm):
        g_ref[j, 0:SUBLANES, :] = gi[:, j * lm:(j + 1) * lm]
        g_ref[j, SUBLANES:2 * SUBLANES, :] = gf[:, j * lm:(j + 1) * lm]

    p = proj(C_VO, C_VO + 2 * M_W)
    oa_ref[:, C_VO:C_VO + 2 * M_W] = p.astype(BF16)

    p = proj(C_GQK, C_GQK + G_KW)
    oa_ref[:, C_GQK:C_GQK + G_KW] = (p * (G_DK ** -0.5)).astype(BF16)
    p = proj(C_GQK + G_KW, C_SMALL)
    oa_ref[:, C_GQK + G_KW:C_SMALL] = p.astype(BF16)


def _inproj(x2, mod3, w_in, w_conv, b_conv, wa_pad, b_gla, bg, *, S, tm, lm, layer):
    N, D = x2.shape
    tpb = S // tm
    kern = functools.partial(_inproj_kernel, tm=tm, tpb=tpb, lm=lm)
    return pl.pallas_call(
        kern,
        grid=(N // tm,),
        in_specs=[pl.BlockSpec((tm, D), lambda i: (i, 0)),
                  pl.BlockSpec((1, 6, D), lambda i: (i // tpb, 0, 0)),
                  pl.BlockSpec((1, IN_TOT, D), lambda i: (layer, 0, 0), pipeline_mode=pl.Buffered(1)),
                  pl.BlockSpec((CONV_W, 2 * M_W), lambda i: (0, 0)),
                  pl.BlockSpec((1, 2 * M_W), lambda i: (0, 0)),
                  pl.BlockSpec((LANES, G_KW), lambda i: (0, 0)),
                  pl.BlockSpec((1, G_KW), lambda i: (0, 0)),
                  pl.BlockSpec((2 * SUBLANES, 1), lambda i: (0, 0))],
        out_specs=[pl.BlockSpec((tm, C_SMALL), lambda i: (i, 0)),
                   pl.BlockSpec((tm, G_KW), lambda i: (i, 0)),
                   pl.BlockSpec((tm // lm, 2 * SUBLANES, lm), lambda i: (i, 0, 0))],
        out_shape=[jax.ShapeDtypeStruct((N, C_SMALL), BF16),
                   jax.ShapeDtypeStruct((N, G_KW), F32),
                   jax.ShapeDtypeStruct((N // lm, 2 * SUBLANES, lm), F32)],
        scratch_shapes=[pltpu.VMEM((SUBLANES + tm, 2 * M_W), F32), pltpu.VMEM((D, C_TOT), BF16)],
        compiler_params=_cparams(),
        name="inproj",
    )(x2, mod3, w_in, w_conv, b_conv, wa_pad, b_gla, bg)


def _mlstm_sel():
    sel = np.zeros((2 * LANES, 2 * M_HEADS * M_HD), np.float32)
    for j in range(2 * M_HEADS):
        src = (SUBLANES if j < M_HEADS else 3 * SUBLANES) + j % M_HEADS
        sel[src, M_HD * j:M_HD * (j + 1)] = 1.0
        sel[LANES + src, M_HD * j:M_HD * (j + 1)] = 1.0
    return sel


def _mlstm_kernel(qk_ref, vo_ref, g_ref, u_ref, gain_ref, sel_ref, out_ref, c_ref, zt_ref, a_ref, dec_ref, m_ref,
                  *, L, NC, nb):
    @pl.when(pl.program_id(1) == 0)
    def _():
        c_ref[...] = jnp.zeros_like(c_ref)
        m_ref[...] = jnp.zeros_like(m_ref)

    tril = (lax.broadcasted_iota(jnp.int32, (L, L), 0) >= lax.broadcasted_iota(jnp.int32, (L, L), 1))
    ones_v = jnp.ones((L, M_HD), BF16)
    zpad = jnp.zeros((LANES - 4 * SUBLANES, L), F32)
    zgroup = lax.broadcasted_iota(jnp.int32, (L, LANES), 1) // SUBLANES
    factor_cols = (zgroup == 1) | (zgroup == 3)

    order = [(bi, c) for bi in range(nb) for c in range(NC)]
    f_all = jnp.concatenate([g_ref[bi, c, SUBLANES:2 * SUBLANES, :] for bi, c in order], axis=0)
    i_all = jnp.concatenate([g_ref[bi, c, 0:SUBLANES, :] for bi, c in order], axis=0)
    b_all = jnp.dot(f_all, u_ref[...], preferred_element_type=F32, precision=HIGHEST)
    a_all = i_all - b_all
    lane_all = lax.broadcasted_iota(jnp.int32, a_all.shape, 1)
    g_all = a_all
    s = 1
    while s < L:
        g_all = jnp.maximum(g_all, jnp.where(lane_all >= s, pltpu.roll(g_all, s, 1), -jnp.inf))
        s *= 2
    for bi in range(nb):
        m_prev = m_ref[bi][:, 0:1]
        for c in range(NC):
            ci = bi * NC + c
            r8 = slice(SUBLANES * ci, SUBLANES * (ci + 1))
            a, b = a_all[r8], b_all[r8]
            a_ref[ci] = a
            M = jnp.maximum(g_all[r8], m_prev)
            ML = M[:, L - 1:L]
            Z = jnp.concatenate([M, jnp.exp(m_prev - M), jnp.exp(-(b + M)), jnp.exp(a - ML), zpad],
                                axis=0)
            zt_ref[ci] = Z.T
            dec_ref[ci] = jnp.broadcast_to(jnp.exp(m_prev - ML), (SUBLANES, 2 * M_HD))
            m_prev = b[:, L - 1:L] + ML
        m_ref[bi] = jnp.broadcast_to(m_prev, (SUBLANES, LANES))

    chains = [(bi, h) for bi in range(nb) for h in range(M_HEADS)]
    nt = (((1,), (1,)), ((), ()))
    tn = (((0,), (0,)), ((), ()))

    def chunk(c, carry):
        rows = pl.ds(pl.multiple_of(c * L, L), L)
        Zt = [zt_ref[bi * NC + c] for bi in range(nb)]
        a = [a_ref[bi * NC + c] for bi in range(nb)]
        dec = [dec_ref[bi * NC + c] for bi in range(nb)]
        hs = [slice(h * M_HD, (h + 1) * M_HD) for h in range(M_HEADS)]
        hs2 = [slice(M_W + h * M_HD, M_W + (h + 1) * M_HD) for h in range(M_HEADS)]
        q = [qk_ref[bi, rows, hs[h]] for bi, h in chains]
        k = [qk_ref[bi, rows, hs2[h]] for bi, h in chains]
        vext = [jnp.concatenate([vo_ref[bi, rows, hs[h]], ones_v], axis=1) for bi, h in chains]
        cst = [c_ref[bi * M_HEADS + h] for bi, h in chains]
        n = range(len(chains))
        sc = [lax.dot_general(q[i], k[i], nt, preferred_element_type=F32) for i in n]
        qc = [jnp.dot(q[i], cst[i].astype(BF16), preferred_element_type=F32) for i in n]
        pm = [(sc[i] * jnp.exp(jnp.where(tril, a[bi][h:h + 1, :] - Zt[bi][:, h:h + 1], -jnp.inf))).astype(BF16)
              for i, (bi, h) in enumerate(chains)]
        pv = [jnp.dot(pm[i], vext[i], preferred_element_type=F32) for i in n]
        rep = []
        for bi in range(nb):
            zf = jnp.where(factor_cols, Zt[bi], 0.0)
            zh = zf.astype(BF16)
            zl = (zf - zh.astype(F32)).astype(BF16)
            rep.append(jnp.dot(jnp.concatenate([zh, zl], axis=1), sel_ref[...], preferred_element_type=F32))
        e_inter = [rep[bi][:, M_HD * h:M_HD * (h + 1)] for bi, h in chains]
        w_state = [rep[bi][:, M_HD * (M_HEADS + h):M_HD * (M_HEADS + h + 1)] for bi, h in chains]
        kw = [(w_state[i] * k[i].astype(F32)).astype(BF16) for i in n]
        upd = [lax.dot_general(kw[i], vext[i], tn, preferred_element_type=F32) for i in n]
        for i, (bi, h) in enumerate(chains):
            c_ref[bi * M_HEADS + h] = dec[bi][h:h + 1, :] * cst[i] + upd[i]
            nd = pv[i] + jnp.concatenate([e_inter[i], e_inter[i]], axis=1) * qc[i]
            hh = nd[:, :M_HD] / jnp.maximum(jnp.abs(nd[:, M_HD:]),
                                            Zt[bi][:, 2 * SUBLANES + h:2 * SUBLANES + h + 1])
            hh = _sigmoid(vo_ref[bi, rows, hs2[h]].astype(F32)) * hh
            hn = hh * lax.rsqrt(jnp.mean(hh * hh, axis=-1, keepdims=True) + LN_EPS)
            out_ref[bi, rows, hs[h]] = (hn * gain_ref[:, hs[h]]).astype(BF16)
        return carry

    lax.fori_loop(0, NC, chunk, 0)


def _mlstm(oa, g3, u_tri, gain, *, B, S, L, nb, ts):
    N = oa.shape[0]
    NC = ts // L
    oa3 = oa.reshape(B, S, oa.shape[1])
    g4 = g3.reshape(B, S // L, 2 * SUBLANES, L)
    sel = jnp.asarray(_mlstm_sel(), BF16)
    kern = functools.partial(_mlstm_kernel, L=L, NC=NC, nb=nb)
    out = pl.pallas_call(
        kern,
        grid=(B // nb, S // ts),
        in_specs=[pl.BlockSpec((nb, ts, 2 * M_W), lambda b, t: (b, t, C_QK // (2 * M_W))),
                  pl.BlockSpec((nb, ts, 2 * M_W), lambda b, t: (b, t, C_VO // (2 * M_W))),
                  pl.BlockSpec((nb, NC, 2 * SUBLANES, L), lambda b, t: (b, t, 0, 0)),
                  pl.BlockSpec((L, L), lambda b, t: (0, 0)),
                  pl.BlockSpec((1, M_W), lambda b, t: (0, 0)),
                  pl.BlockSpec(sel.shape, lambda b, t: (0, 0))],
        out_specs=pl.BlockSpec((nb, ts, M_W), lambda b, t: (b, t, 0)),
        out_shape=jax.ShapeDtypeStruct((B, S, M_W), BF16),
        scratch_shapes=[pltpu.VMEM((nb * M_HEADS, M_HD, 2 * M_HD), F32),
                        pltpu.VMEM((nb * NC, L, LANES), F32),
                        pltpu.VMEM((nb * NC, SUBLANES, L), F32),
                        pltpu.VMEM((nb * NC, SUBLANES, 2 * M_HD), F32),
                        pltpu.VMEM((nb, SUBLANES, LANES), F32)],
        compiler_params=_cparams(2),
        name="mlstm",
    )(oa3, oa3, g4, u_tri, gain, sel)
    return out.reshape(N, M_W)


_G_LEVELS = 6
_G_XROW = 2 * G_CHUNK + SUBLANES


def _gla_consts():
    L = G_CHUNK
    t = np.arange(L)
    blocks = [(t[None, :] <= t[:, None]).astype(np.float32),
              (t[None, :] > t[:, None]).astype(np.float32),
              np.ones((SUBLANES, L), np.float32)]
    masks = [np.eye(L, dtype=np.float32)]
    m = 1
    while m < L:
        wl = np.zeros((L, L), np.float32)
        for r in range(L):
            r0 = (r // (2 * m)) * 2 * m + m
            if r % (2 * m) >= m:
                wl[r, r0:r + 1] = 1.0
            else:
                wl[r, r + 1:r0] = 1.0
        blocks.append(wl)
        tt, ss = t[:, None], t[None, :]
        masks.append(((tt // (2 * m) == ss // (2 * m)) & (tt % (2 * m) >= m)
                      & (ss % (2 * m) < m)).astype(np.float32))
        m *= 2
    w = np.concatenate(blocks, axis=0)
    w3 = np.concatenate([w, w, w], axis=1)
    mk = np.stack([np.concatenate([x] * G_HEADS, axis=0) for x in masks])
    return w3, mk


def _gla_kernel(qk_ref, v_ref, gg_ref, la_ref, w3_ref, mk_ref, gain_ref, out_ref, st_ref, *, NC, nb):
    L = G_CHUNK

    @pl.when(pl.program_id(1) == 0)
    def _():
        st_ref[...] = jnp.zeros_like(st_ref)

    lane_head = lax.broadcasted_iota(jnp.int32, (L, G_KW), 1) // G_DK
    br = lax.broadcasted_iota(jnp.int32, (2 * G_DV, LANES), 0) < G_DV
    bl = lax.broadcasted_iota(jnp.int32, (2 * G_DV, LANES), 1) < G_DK
    bmask = br == bl
    nt = (((1,), (1,)), ((), ()))
    tn = (((0,), (0,)), ((), ()))

    def chunk(c, carry):
        rows = pl.ds(pl.multiple_of(c * L, L), L)
        X, q, k = [], [], []
        for bi in range(nb):
            la = la_ref[bi, rows, :]
            hi = la.astype(BF16)
            r1 = la - hi.astype(F32)
            mid = r1.astype(BF16)
            lo = (r1 - mid.astype(F32)).astype(BF16)
            stk = jnp.concatenate([hi, mid, lo], axis=0)
            X.append(jnp.exp(jnp.dot(w3_ref[...], stk, preferred_element_type=F32)))
            q.append(qk_ref[bi, rows, 0:G_KW].astype(F32))
            k.append(qk_ref[bi, rows, G_KW:2 * G_KW].astype(F32))

        sc = [[None] * (_G_LEVELS + 1) for _ in range(nb)]
        for lev in range(_G_LEVELS + 1):
            for bi in range(nb):
                if lev == 0:
                    qt, kt = q[bi], k[bi]
                else:
                    xl = X[bi][_G_XROW + L * (lev - 1):_G_XROW + L * lev, :]
                    qt, kt = q[bi] * xl, k[bi] * xl
                q4 = jnp.concatenate([jnp.where(lane_head == h, qt, 0.0) for h in range(G_HEADS)],
                                     axis=0).astype(BF16)
                sc[bi][lev] = lax.dot_general(q4, kt.astype(BF16), nt, preferred_element_type=F32)
        Ab = []
        for bi in range(nb):
            A = sc[bi][0] * mk_ref[0]
            for lev in range(1, _G_LEVELS + 1):
                A = A + sc[bi][lev] * mk_ref[lev]
            Ab.append(A.astype(BF16))

        for bi in range(nb):
            gg = gg_ref[bi, rows, :].astype(F32)
            gate = gg * _sigmoid(gg)
            for p in range(2):
                ls = slice(LANES * p, LANES * (p + 1))
                vp = v_ref[bi, rows, 2 * G_DV * p:2 * G_DV * (p + 1)]
                oi = [jnp.dot(Ab[bi][L * (2 * p + hh):L * (2 * p + hh + 1)],
                              vp[:, G_DV * hh:G_DV * (hh + 1)], preferred_element_type=F32)
                      for hh in range(2)]
                st = st_ref[bi, p]
                qc = (q[bi][:, ls] * X[bi][0:L, ls]).astype(BF16)
                o_inter = lax.dot_general(qc, st.astype(BF16), nt, preferred_element_type=F32)
                kc = (k[bi][:, ls] * X[bi][L:2 * L, ls]).astype(BF16)
                upd = lax.dot_general(vp, kc, tn, preferred_element_type=F32)
                dec = X[bi][2 * L:2 * L + 1, ls]
                st_ref[bi, p] = jnp.where(bmask, dec * st + upd, 0.0)
                for hh in range(2):
                    o = o_inter[:, G_DV * hh:G_DV * (hh + 1)] + oi[hh]
                    hn = o * lax.rsqrt(jnp.mean(o * o, axis=-1, keepdims=True) + LN_EPS)
                    hs = slice(G_DV * (2 * p + hh), G_DV * (2 * p + hh + 1))
                    out_ref[bi, rows, hs] = (hn * gain_ref[:, hs] * gate[:, hs]).astype(BF16)
        return carry

    lax.fori_loop(0, NC, chunk, 0)


def _gla(oa, la, w3, mk, gain, *, B, S, nb, ts):
    N = oa.shape[0]
    oa3 = oa.reshape(B, S, oa.shape[1])
    la3 = la.reshape(B, S, G_KW)
    kern = functools.partial(_gla_kernel, NC=ts // G_CHUNK, nb=nb)
    out = pl.pallas_call(
        kern,
        grid=(B // nb, S // ts),
        in_specs=[pl.BlockSpec((nb, ts, 2 * G_KW), lambda b, t: (b, t, C_GQK // (2 * G_KW))),
                  pl.BlockSpec((nb, ts, G_W), lambda b, t: (b, t, C_GV // G_W)),
                  pl.BlockSpec((nb, ts, G_W), lambda b, t: (b, t, C_GG // G_W)),
                  pl.BlockSpec((nb, ts, G_KW), lambda b, t: (b, t, 0)),
                  pl.BlockSpec(w3.shape, lambda b, t: (0, 0)),
                  pl.BlockSpec(mk.shape, lambda b, t: (0, 0, 0)),
                  pl.BlockSpec((1, G_W), lambda b, t: (0, 0))],
        out_specs=pl.BlockSpec((nb, ts, G_W), lambda b, t: (b, t, 0)),
        out_shape=jax.ShapeDtypeStruct((B, S, G_W), BF16),
        scratch_shapes=[pltpu.VMEM((nb, 2, 2 * G_DV, LANES), F32)],
        compiler_params=_cparams(2),
        name="gla",
    )(oa3, oa3, oa3, la3, w3, mk, gain)
    return out.reshape(N, G_W)


def _layer_norm(z, g, b):
    mu = jnp.mean(z, axis=-1, keepdims=True)
    zc = z - mu
    var = jnp.mean(zc * zc, axis=-1, keepdims=True)
    return zc * lax.rsqrt(var + LN_EPS) * g + b


def _outproj_kernel(hm_ref, hg_ref, wf_ref, x_ref, mod_ref, g_ref, b_ref, wrg_ref, wre_ref, br_ref,
                    x1_ref, u2_ref, rrow_ref, w_ref, wr_ref, *, tb, nh):
    @pl.when(pl.program_id(0) == 0)
    def _():
        w_ref[...] = wf_ref[...].astype(BF16)
        z = lambda n: jnp.zeros((n, wrg_ref.shape[2]), F32)
        wt = jnp.concatenate([wrg_ref[0], z(SUBLANES - N_GROUPS), wre_ref[0],
                              z(LANES - SUBLANES - N_EXP)], axis=0).T
        hi = wt.astype(BF16)
        wr_ref[:, 0:LANES] = hi
        wr_ref[:, LANES:2 * LANES] = (wt - hi.astype(F32)).astype(BF16)

    mod = mod_ref[0]
    blocks = [slice(tb * j, tb * (j + 1)) for j in range(nh)]
    y = [jnp.dot(hm_ref[r, :], w_ref[0:M_W, :], preferred_element_type=F32)
         + jnp.dot(hg_ref[r, :], w_ref[M_W:M_W + G_W, :], preferred_element_type=F32) for r in blocks]
    u2 = []
    for j, r in enumerate(blocks):
        z = ALPHA * x_ref[r, :] + (1.0 + mod[2:3, :]) * y[j]
        x1 = _layer_norm(z, g_ref[...], b_ref[...])
        x1_ref[r, :] = x1
        u2.append(x1 * (1.0 + mod[4:5, :]) + mod[3:4, :])
        u2_ref[r, :] = u2[j].astype(BF16)

    u2h = [u.astype(BF16) for u in u2]
    u2l = [(u2[j] - u2h[j].astype(F32)).astype(BF16) for j in range(nh)]
    lh = [jnp.dot(u, wr_ref[...], preferred_element_type=F32) for u in u2h]
    ll = [jnp.dot(u, wr_ref[:, 0:LANES], preferred_element_type=F32) for u in u2l]
    for j in range(nh):
        logits = lh[j][:, 0:LANES] + lh[j][:, LANES:2 * LANES] + ll[j] + br_ref[...]
        rrow_ref[j] = _route_select(logits.T, tb)


def _route_select(lt, tm):
    row = lax.broadcasted_iota(jnp.int32, (SUBLANES, tm), 0)
    gl = jnp.where(row < N_GROUPS, lt[0:SUBLANES, :], -jnp.inf)
    gmax = jnp.max(gl, axis=0, keepdims=True)
    gsel = jnp.min(jnp.where(gl == gmax, row, SUBLANES), axis=0, keepdims=True)
    pg = 1.0 / jnp.sum(jnp.exp(gl - gmax), axis=0, keepdims=True)
    ein = jnp.zeros((SUBLANES, tm), F32)
    for g in range(N_GROUPS):
        ein = jnp.where(gsel == g, lt[SUBLANES * (g + 1):SUBLANES * (g + 2), :], ein)
    v1 = jnp.max(ein, axis=0, keepdims=True)
    i1 = jnp.min(jnp.where(ein == v1, row, SUBLANES), axis=0, keepdims=True)
    rest = jnp.where(row == i1, -jnp.inf, ein)
    v2 = jnp.max(rest, axis=0, keepdims=True)
    i2 = jnp.min(jnp.where(rest == v2, row, SUBLANES), axis=0, keepdims=True)
    t2 = jnp.exp(v2 - v1)
    p1 = 1.0 / (1.0 + t2)
    e0 = (gsel * E_PER_G + i1).astype(F32)
    e1 = (gsel * E_PER_G + i2).astype(F32)
    return jnp.concatenate([e0, e1, pg * p1, pg * (t2 * p1), jnp.zeros((SUBLANES - 4, tm), F32)], axis=0)


def _outproj(hm, hg, w_out, x2, mod3, g, b, wrg_t, wre_t, br, *, S, tb, nh, layer):
    N, D = x2.shape
    tm = tb * nh
    tpb = S // tm
    kern = functools.partial(_outproj_kernel, tb=tb, nh=nh)
    return pl.pallas_call(
        kern,
        grid=(N // tm,),
        in_specs=[pl.BlockSpec((tm, M_W), lambda i: (i, 0)),
                  pl.BlockSpec((tm, G_W), lambda i: (i, 0)),
                  pl.BlockSpec((M_W + G_W, D), lambda i: (0, 0), pipeline_mode=pl.Buffered(1)),
                  pl.BlockSpec((tm, D), lambda i: (i, 0)),
                  pl.BlockSpec((1, 6, D), lambda i: (i // tpb, 0, 0)),
                  pl.BlockSpec((1, D), lambda i: (0, 0)),
                  pl.BlockSpec((1, D), lambda i: (0, 0)),
                  pl.BlockSpec((1, N_GROUPS, D), lambda i: (layer, 0, 0)),
                  pl.BlockSpec((1, N_EXP, D), lambda i: (layer, 0, 0)),
                  pl.BlockSpec((1, LANES), lambda i: (0, 0))],
        out_specs=[pl.BlockSpec((tm, D), lambda i: (i, 0)),
                   pl.BlockSpec((tm, D), lambda i: (i, 0)),
                   pl.BlockSpec((nh, SUBLANES, tb), lambda i: (i, 0, 0))],
        out_shape=[jax.ShapeDtypeStruct((N, D), F32),
                   jax.ShapeDtypeStruct((N, D), BF16),
                   jax.ShapeDtypeStruct((N // tb, SUBLANES, tb), F32)],
        scratch_shapes=[pltpu.VMEM((M_W + G_W, D), BF16), pltpu.VMEM((D, 2 * LANES), BF16)],
        compiler_params=_cparams(),
        name="outproj",
    )(hm, hg, w_out, x2, mod3, g, b, wrg_t, wre_t, br)


def _slots_per_tile(tb):
    worst = 2 * tb + N_EXP * (GRAN - 1)
    return -(-worst // LANES) * LANES


def _ffn_tiles(n_tok, tb):
    worst_rows = 2 * n_tok + (n_tok // tb) * N_EXP * (GRAN - 1)
    return -(-worst_rows // FFN_TM) + N_EXP


def _route_kernel(rr_ref, u_ref, lt_ref, srow_ref, col_ref, gd_ref, meta_ref, mg_ref, part_ref,
                  *, NT, tb, TM):
    iota_e = lax.broadcasted_iota(jnp.int32, (N_EXP, tb), 0).astype(F32)
    glane = lax.broadcasted_iota(jnp.int32, (N_EXP, LANES), 1).astype(F32)
    ltri = lt_ref[...]

    def prefix_e(col):
        return jnp.dot(ltri, jnp.broadcast_to(col, (N_EXP, LANES)),
                       preferred_element_type=F32, precision=HIGHEST)[:, 0:1]

    def p1(j, run8):
        r = rr_ref[j]
        oh0 = jnp.where(iota_e == r[0:1, :], 1.0, 0.0)
        oh1 = jnp.where(iota_e == r[1:2, :], 1.0, 0.0)
        cum0 = jnp.dot(oh0.astype(BF16), u_ref[...], preferred_element_type=F32)
        cum1 = jnp.dot(oh1.astype(BF16), u_ref[...], preferred_element_type=F32)
        c0 = jnp.sum(oh0, axis=1, keepdims=True)
        n8 = jnp.floor((c0 + jnp.sum(oh1, axis=1, keepdims=True) + (GRAN - 1.0)) * (1.0 / GRAN))
        lo8 = prefix_e(n8)
        s0 = jnp.sum(oh0 * (GRAN * lo8 + cum0 - 1.0), axis=0, keepdims=True)
        s1 = jnp.sum(oh1 * (GRAN * lo8 + c0 + cum1 - 1.0), axis=0, keepdims=True)
        info = jnp.concatenate([s0, s1, r[2:4, :], jnp.zeros((SUBLANES - 4, tb), F32)], axis=0)
        srow_ref[j] = info
        col_ref[pl.ds(pl.multiple_of(j * tb, tb), tb), :] = jnp.concatenate(
            [info, jnp.zeros((LANES - SUBLANES, tb), F32)], axis=0).T
        mg = jnp.where((lo8 <= glane) & (glane < lo8 + n8), 1.0, 0.0)
        mg_ref[j] = mg
        part = jnp.sum(mg * (run8 + glane - lo8), axis=0, keepdims=True)
        gcnt = jnp.broadcast_to(jnp.sum(n8, axis=0, keepdims=True), (1, LANES))
        part_ref[j] = jnp.concatenate([part, gcnt, jnp.zeros((SUBLANES - 2, LANES), F32)], axis=0)
        return run8 + n8

    tot8 = lax.fori_loop(0, NT, p1, jnp.zeros((N_EXP, 1), F32), unroll=8 if NT % 8 == 0 else 1)
    seg_t = jnp.floor((tot8 * GRAN + (TM - 1.0)) * (1.0 / TM))
    base_t = prefix_e(seg_t)
    base8 = base_t * (TM // GRAN)
    lane1 = lax.broadcasted_iota(jnp.int32, (1, LANES), 1)

    def p2(j, carry):
        pr = part_ref[j]
        dst = (pr[0:1, :] + jnp.sum(mg_ref[j] * base8, axis=0, keepdims=True)) * GRAN
        gd_ref[j] = jnp.where(lane1 == G_LAST, pr[1:2, :], dst).astype(jnp.int32)
        return carry

    lax.fori_loop(0, NT, p2, 0, unroll=8 if NT % 8 == 0 else 1)
    eye = jnp.where(glane == lax.broadcasted_iota(jnp.int32, (N_EXP, LANES), 0).astype(F32), 1.0, 0.0)
    tail_row = jnp.sum(eye * ((base8 + tot8) * GRAN), axis=0, keepdims=True)
    tail_n8 = jnp.sum(eye * (seg_t * (TM // GRAN) - tot8), axis=0, keepdims=True)
    nv_l = jnp.broadcast_to(jnp.sum(seg_t, axis=0, keepdims=True), (1, LANES))
    gd_ref[NT] = jnp.where(lane1 == G_LAST, nv_l, tail_row).astype(jnp.int32)
    gd_ref[NT + 1] = tail_n8.astype(jnp.int32)
    ti = lax.broadcasted_iota(jnp.int32, (N_EXP, tb), 1).astype(F32)
    te = jnp.sum(jnp.where(base_t <= ti, 1.0, 0.0), axis=0, keepdims=True) - 1.0
    nv = jnp.broadcast_to(jnp.sum(seg_t, axis=0, keepdims=True), (1, tb))
    own = jnp.where((base_t <= ti) & (ti < base_t + seg_t), 1.0, 0.0)
    vr = jnp.sum(own * jnp.clip(tot8 * GRAN - (ti - base_t) * TM, 0.0, TM), axis=0, keepdims=True)
    meta_ref[...] = jnp.concatenate([te, nv, vr, jnp.zeros((SUBLANES - 3, tb), F32)],
                                    axis=0).astype(jnp.int32)


def _route(rrow, u_cnt, ltri, *, TM):
    NT, _, tb = rrow.shape
    kern = functools.partial(_route_kernel, NT=NT, tb=tb, TM=TM)
    full3 = lambda i: (0, 0, 0)
    return pl.pallas_call(
        kern,
        grid=(1,),
        in_specs=[pl.BlockSpec((NT, SUBLANES, tb), full3),
                  pl.BlockSpec((tb, tb), lambda i: (0, 0)),
                  pl.BlockSpec((N_EXP, N_EXP), lambda i: (0, 0))],
        out_specs=[pl.BlockSpec((NT, SUBLANES, tb), full3),
                   pl.BlockSpec((NT * tb, LANES), lambda i: (0, 0)),
                   pl.BlockSpec((NT + 2, 1, LANES), full3),
                   pl.BlockSpec((SUBLANES, tb), lambda i: (0, 0))],
        out_shape=[jax.ShapeDtypeStruct((NT, SUBLANES, tb), F32),
                   jax.ShapeDtypeStruct((NT * tb, LANES), F32),
                   jax.ShapeDtypeStruct((NT + 2, 1, LANES), jnp.int32),
                   jax.ShapeDtypeStruct((SUBLANES, tb), jnp.int32)],
        scratch_shapes=[pltpu.VMEM((NT, N_EXP, LANES), F32), pltpu.VMEM((NT, SUBLANES, LANES), F32)],
        compiler_params=_cparams(),
        name="route",
    )(rrow, u_cnt, ltri)


_HI_MASK = 0xFFFF0000


def _pack_halves(x):
    c = x.shape[1] // 2
    lo = lax.bitcast_convert_type(x[:, :c], U32)
    hi = lax.bitcast_convert_type(x[:, c:], U32)
    return (lo >> 16) | (hi & U32(_HI_MASK))


def _unpack_halves(w):
    lo = lax.bitcast_convert_type(w << 16, F32)
    hi = lax.bitcast_convert_type(w & U32(_HI_MASK), F32)
    return jnp.concatenate([lo, hi], axis=1).astype(BF16)


def _granule_copy(src_ref, src_row, dst_ref, dst_row, sem, n=1):
    cols = pl.ds(0, min(src_ref.shape[-1], dst_ref.shape[-1]))
    return pltpu.make_async_copy(src_ref.at[pl.ds(src_row, n * GRAN), cols],
                                 dst_ref.at[pl.ds(dst_row, n * GRAN), cols], sem)


def _for_granules(n, body, unroll=4):
    def blk(i, carry):
        for t in range(unroll):
            body(i * unroll + t)
        return carry

    def one(g, carry):
        body(g)
        return carry

    nblk = n // unroll
    lax.fori_loop(0, nblk, blk, 0)
    lax.fori_loop(nblk * unroll, n, one, 0)


def _wait_granules(n, src_ref, dst_ref, sem, n_max):
    b = 1
    while b <= n_max:
        @pl.when((n & b) != 0)
        def _(b=b):
            _granule_copy(src_ref, 0, dst_ref, 0, sem, n=b).wait()
        b *= 2


def _dispatch_kernel(gd_ref, srow_ref, u_ref, xs_ref, buf, zbuf, sems, *, NT, SL, TM, n_tiles):
    j = pl.program_id(0)
    slot = j % 2
    zsem = sems.at[2]

    def drain(tile, sl):
        _wait_granules(gd_ref[tile, G_LAST], buf.at[sl], xs_ref, sems.at[sl], SL // GRAN)

    def tile_fill(t):
        return pltpu.make_async_copy(zbuf, xs_ref.at[pl.ds(pl.multiple_of(t * TM, TM), TM), :], zsem)

    def zero_fill(wait):
        for e in range(N_EXP):
            n, row0 = gd_ref[NT + 1, e], gd_ref[NT, e]
            b = TM // GRAN // 2
            while b >= 1:
                @pl.when((n & b) != 0)
                def _(b=b, n=n, row0=row0):
                    start = pl.multiple_of(row0 + ((n >> b.bit_length()) << b.bit_length()) * GRAN, GRAN)
                    cp = pltpu.make_async_copy(zbuf.at[pl.ds(0, b * GRAN), :],
                                               xs_ref.at[pl.ds(start, b * GRAN), :], zsem)
                    cp.wait() if wait else cp.start()
                b //= 2

        def zt(t, carry):
            tile_fill(t).wait() if wait else tile_fill(t).start()
            return carry
        lax.fori_loop(gd_ref[NT, G_LAST], n_tiles, zt, 0)

    @pl.when(j == 0)
    def _():
        zbuf[...] = jnp.zeros_like(zbuf)
        zero_fill(False)

    @pl.when(j >= 2)
    def _():
        drain(j - 2, slot)

    s = srow_ref[0]
    rows = lax.broadcasted_iota(jnp.int32, (SL, s.shape[1]), 0).astype(F32)
    m0 = rows == s[0:1, :]
    m1 = rows == s[1:2, :]
    oh = jnp.where(m0 | m1, 1.0, 0.0).astype(BF16)
    dw = u_ref.shape[1] // 2
    buf[slot, :, 0:dw] = _pack_halves(jnp.dot(oh, u_ref[...], preferred_element_type=F32))
    wrow = jnp.sum(jnp.where(m0, s[2:3, :], 0.0) + jnp.where(m1, s[3:4, :], 0.0), axis=1, keepdims=True)
    buf[slot, :, dw:dw + LANES] = lax.bitcast_convert_type(jnp.broadcast_to(wrow, (SL, LANES)), U32)

    def issue(g):
        _granule_copy(buf.at[slot], pl.multiple_of(g * GRAN, GRAN), xs_ref,
                      pl.multiple_of(gd_ref[j, g], GRAN), sems.at[slot]).start()

    _for_granules(gd_ref[j, G_LAST], issue)

    @pl.when(j == NT - 1)
    def _():
        drain(j, slot)
        if NT > 1:
            drain(j - 1, 1 - slot)
        zero_fill(True)


def _dispatch(gd, srow, u2, *, n_tiles, TM):
    N, D = u2.shape
    NT, _, tb = srow.shape
    SL = _slots_per_tile(tb)
    n_rows = n_tiles * TM
    kern = functools.partial(_dispatch_kernel, NT=NT, SL=SL, TM=TM, n_tiles=n_tiles)
    grid_spec = pltpu.PrefetchScalarGridSpec(
        num_scalar_prefetch=1,
        grid=(NT,),
        in_specs=[pl.BlockSpec((1, SUBLANES, tb), lambda j, gd: (j, 0, 0)),
                  pl.BlockSpec((tb, D), lambda j, gd: (j, 0))],
        out_specs=pl.BlockSpec(memory_space=pl.ANY),
        scratch_shapes=[pltpu.VMEM((2, SL, D // 2 + LANES), U32), pltpu.VMEM((TM, D // 2 + LANES), U32),
                        pltpu.SemaphoreType.DMA((3,))],
    )
    return pl.pallas_call(
        kern,
        grid_spec=grid_spec,
        out_shape=jax.ShapeDtypeStruct((n_rows, D // 2 + LANES), U32),
        compiler_params=_cparams(),
        name="dispatch",
    )(gd, srow, u2)


def _ffn_kernel(te_ref, nv_ref, vr_ref, xs_ref, wg_ref, wu_ref, wd_ref, o_ref, wgb, wub, wdb, sg, su, sd, slot_ref,
                sems):
    i = pl.program_id(0)
    nv = nv_ref[0]
    e = te_ref[i]

    def weight_copies(ex, sl):
        return (pltpu.make_async_copy(wg_ref.at[ex], sg.at[sl], sems.at[sl]),
                pltpu.make_async_copy(wu_ref.at[ex], su.at[sl], sems.at[sl]),
                pltpu.make_async_copy(wd_ref.at[ex], sd.at[sl], sems.at[sl]))

    @pl.when(i == 0)
    def _():
        slot_ref[0] = 0
        for cp in weight_copies(e, 0):
            cp.start()

    @pl.when((i < nv) & ((i == 0) | (e != te_ref[jnp.maximum(i - 1, 0)])))
    def _():
        sl = slot_ref[0]
        for cp in weight_copies(e, sl):
            cp.wait()
        wgb[...] = sg[sl].astype(BF16)
        wub[...] = su[sl].astype(BF16)
        wdb[...] = sd[sl].astype(BF16)
        nxt = lax.while_loop(lambda t: (t < nv) & (te_ref[jnp.minimum(t, nv - 1)] == e), lambda t: t + 1, i + 1)

        @pl.when(nxt < nv)
        def _():
            for cp in weight_copies(te_ref[nxt], 1 - sl):
                cp.start()
        slot_ref[0] = 1 - sl

    hm = xs_ref.shape[0] // FFN_SUB
    dw = xs_ref.shape[1] - LANES

    def swiglu_rows(nsub):
        halves = tuple(slice(hm * j, hm * (j + 1)) for j in range(nsub))
        x = [_unpack_halves(xs_ref[r, 0:dw]) for r in halves]
        g = [jnp.dot(x[j], wgb[...], preferred_element_type=F32) for j in range(nsub)]
        u = [jnp.dot(x[j], wub[...], preferred_element_type=F32) for j in range(nsub)]
        h = [(g[j] * _sigmoid(g[j]) * u[j]).astype(BF16) for j in range(nsub)]
        y = [jnp.dot(h[j], wdb[...], preferred_element_type=F32) for j in range(nsub)]
        for j in range(nsub):
            wt = lax.bitcast_convert_type(xs_ref[halves[j], dw:dw + LANES], F32)
            yw = y[j] * jnp.concatenate([wt] * (2 * dw // LANES), axis=1)
            o_ref[halves[j], 0:dw] = _pack_halves(yw.astype(BF16).astype(F32))
            o_ref[halves[j], dw:dw + LANES] = xs_ref[halves[j], dw:dw + LANES]
        if nsub < FFN_SUB:
            o_ref[hm * nsub:, :] = jnp.zeros((hm * (FFN_SUB - nsub), dw + LANES), U32)

    used = vr_ref[i]
    for nsub in range(1, FFN_SUB + 1):
        lo, hi = hm * (nsub - 1), hm * nsub
        pl.when((i < nv) & (used > lo) & ((used <= hi) if nsub < FFN_SUB else True))(
            functools.partial(swiglu_rows, nsub))


def _ffn(te, nv, vr, xs, wg, wu, wd, *, TM):
    P, XW = xs.shape
    DW = XW - LANES
    D = 2 * DW
    n_tiles = P // TM
    used_tile = lambda i, te, nv, vr: (jnp.maximum(jnp.minimum(i, nv[0] - 1), 0), 0)
    grid_spec = pltpu.PrefetchScalarGridSpec(
        num_scalar_prefetch=3,
        grid=(n_tiles,),
        in_specs=[pl.BlockSpec((TM, XW), used_tile),
                  pl.BlockSpec(memory_space=pl.ANY),
                  pl.BlockSpec(memory_space=pl.ANY),
                  pl.BlockSpec(memory_space=pl.ANY)],
        out_specs=pl.BlockSpec((TM, XW), used_tile),
        scratch_shapes=[pltpu.VMEM((D, D_EXP), BF16), pltpu.VMEM((D, D_EXP), BF16),
                        pltpu.VMEM((D_EXP, D), BF16),
                        pltpu.VMEM((2, D, D_EXP), F32), pltpu.VMEM((2, D, D_EXP), F32),
                        pltpu.VMEM((2, D_EXP, D), F32), pltpu.SMEM((1,), jnp.int32),
                        pltpu.SemaphoreType.DMA((2,))],
    )
    return pl.pallas_call(
        _ffn_kernel,
        grid_spec=grid_spec,
        out_shape=jax.ShapeDtypeStruct((P, XW), U32),
        input_output_aliases={3: 0},
        compiler_params=_cparams(),
        name="ffn",
    )(te, nv, vr, xs, wg, wu, wd)


def _combine_kernel(gd_ref, ys_ref, col_ref, x1_ref, mod_ref, g_ref, b_ref, o_ref, buf, sems, *, NT, SL):
    j = pl.program_id(0)
    slot = j % 2

    def fetch(tile, sl):
        def f(g):
            _granule_copy(ys_ref, pl.multiple_of(gd_ref[tile, g], GRAN), buf.at[sl],
                          pl.multiple_of(g * GRAN, GRAN), sems.at[sl]).start()
        _for_granules(gd_ref[tile, G_LAST], f)

    @pl.when(j == 0)
    def _():
        fetch(0, 0)

    @pl.when(j + 1 < NT)
    def _():
        fetch(j + 1, 1 - slot)

    ng = gd_ref[j, G_LAST]

    _wait_granules(ng, ys_ref, buf.at[slot], sems.at[slot], SL // GRAN)

    rows = lax.broadcasted_iota(jnp.int32, (SL, 1), 0)
    yb = _unpack_halves(jnp.where(rows < ng * GRAN, buf[slot], U32(0)))
    col = col_ref[...]
    tb = col.shape[0]
    lanes = lax.broadcasted_iota(jnp.int32, (tb, SL), 1).astype(F32)
    sel = jnp.where((lanes == col[:, 0:1]) | (lanes == col[:, 1:2]), 1.0, 0.0).astype(BF16)
    y = jnp.dot(sel, yb, preferred_element_type=F32)
    mod = mod_ref[0]
    z = ALPHA * x1_ref[...] + (1.0 + mod[5:6, :]) * y
    o_ref[...] = _layer_norm(z, g_ref[...], b_ref[...])


def _combine(gd, ys, col, x1, mod3, g, b, *, S, tb):
    N, D = x1.shape
    NT = N // tb
    tpb = S // tb
    SL = _slots_per_tile(tb)
    kern = functools.partial(_combine_kernel, NT=NT, SL=SL)
    grid_spec = pltpu.PrefetchScalarGridSpec(
        num_scalar_prefetch=1,
        grid=(NT,),
        in_specs=[pl.BlockSpec(memory_space=pl.ANY),
                  pl.BlockSpec((tb, LANES), lambda j, gd: (j, 0)),
                  pl.BlockSpec((tb, D), lambda j, gd: (j, 0)),
                  pl.BlockSpec((1, 6, D), lambda j, gd: (j // tpb, 0, 0)),
                  pl.BlockSpec((1, D), lambda j, gd: (0, 0)),
                  pl.BlockSpec((1, D), lambda j, gd: (0, 0))],
        out_specs=pl.BlockSpec((tb, D), lambda j, gd: (j, 0)),
        scratch_shapes=[pltpu.VMEM((2, SL, D // 2), U32), pltpu.SemaphoreType.DMA((2,))],
    )
    return pl.pallas_call(
        kern,
        grid_spec=grid_spec,
        out_shape=jax.ShapeDtypeStruct((N, D), F32),
        compiler_params=_cparams(),
        name="combine",
    )(gd, ys, col, x1, mod3, g, b)


def _layer(x, c, l, w_ada, b_ada, w_in, w_conv, b_conv, b_igate, b_fgate, mlstm_norm_g, w_gla_a, b_gla_a,
           gla_norm_g, w_out, ln1_g, ln1_b, w_route_group, b_route_group, w_route_expert, b_route_expert,
           w_gate, w_up, w_down, ln2_g, ln2_b):
    B, S, D = x.shape
    N = B * S
    x2 = x.reshape(N, D)
    tm_in = min(512, S)
    tm = min(256, S)
    lm = min(256, S)
    assert S % tm_in == 0 and S % tm == 0 and tm_in % lm == 0 and S % G_CHUNK == 0
    assert w_in.shape[1:] == (D, IN_TOT) and w_gate.shape[1:] == (N_EXP, D, D_EXP)

    mod3 = _ada(c, w_ada[l], b_ada[l]).reshape(B, 6, D)

    wa_pad = jnp.zeros((LANES, G_KW), F32).at[SM_A:SM_A + G_RANK].set(w_gla_a[l]).astype(BF16)
    bg = (jnp.zeros((2 * SUBLANES, 1), F32).at[0:M_HEADS, 0].set(b_igate[l])
          .at[SUBLANES:SUBLANES + M_HEADS, 0].set(b_fgate[l]))
    oa, la, g3 = _inproj(x2, mod3, jnp.swapaxes(w_in, 1, 2), w_conv[l], b_conv[l].reshape(1, -1), wa_pad,
                         b_gla_a[l].reshape(1, -1), bg, S=S, tm=tm_in, lm=lm, layer=l)

    u_tri = jnp.asarray(np.triu(np.ones((lm, lm), np.float32)))
    nb = 4 if B % 4 == 0 else (2 if B % 2 == 0 else 1)
    ts = min(512, S)
    hm = _mlstm(oa, g3, u_tri, mlstm_norm_g[l].reshape(1, -1), B=B, S=S, L=lm, nb=nb, ts=ts)
    w3_np, mk_np = _gla_consts()
    hg = _gla(oa, la, jnp.asarray(w3_np, BF16), jnp.asarray(mk_np), gla_norm_g[l].reshape(1, -1), B=B, S=S,
              nb=nb, ts=ts)

    br = (jnp.zeros((1, LANES), F32).at[0, 0:N_GROUPS].set(b_route_group[l])
          .at[0, SUBLANES:SUBLANES + N_EXP].set(b_route_expert[l]))
    x1, u2, rrow = _outproj(hm, hg, w_out[l], x2, mod3, ln1_g[l].reshape(1, -1), ln1_b[l].reshape(1, -1),
                            jnp.swapaxes(w_route_group, 1, 2), jnp.swapaxes(w_route_expert, 1, 2), br,
                            S=S, tb=tm, nh=4 if S % (4 * tm) == 0 else 1, layer=l)

    u_cnt = jnp.asarray(np.triu(np.ones((tm, tm), np.float32)), BF16)
    ltri = jnp.asarray(np.tril(np.ones((N_EXP, N_EXP), np.float32), -1))
    srow, col, gd3, meta = _route(rrow, u_cnt, ltri, TM=FFN_TM)
    gd = gd3.reshape(N // tm + 2, LANES)
    n_tiles = _ffn_tiles(N, tm)
    te, nv, vr = meta[0, :n_tiles], meta[1, 0:1], meta[2, :n_tiles]

    xs = _dispatch(gd, srow, u2, n_tiles=n_tiles, TM=FFN_TM)
    ys = _ffn(te, nv, vr, xs, w_gate[l], w_up[l], w_down[l], TM=FFN_TM)
    out = _combine(gd, ys, col, x1, mod3, ln2_g[l].reshape(1, -1), ln2_b[l].reshape(1, -1), S=S, tb=tm)
    return out.reshape(B, S, D)


def kernel(x, c, w_ada, b_ada, w_in, w_conv, b_conv, b_igate, b_fgate, mlstm_norm_g, w_gla_a, b_gla_a,
           gla_norm_g, w_out, ln1_g, ln1_b, w_route_group, b_route_group, w_route_expert, b_route_expert,
           w_gate, w_up, w_down, ln2_g, ln2_b):
    for l in range(DEPTH):
        x = _layer(x, c, l, w_ada, b_ada, w_in, w_conv, b_conv, b_igate, b_fgate, mlstm_norm_g, w_gla_a,
                   b_gla_a, gla_norm_g, w_out, ln1_g, ln1_b, w_route_group, b_route_group, w_route_expert,
                   b_route_expert, w_gate, w_up, w_down, ln2_g, ln2_b)
    return x
```

```python
import functools

import numpy as np
import jax
import jax.numpy as jnp
from jax import lax
from jax.experimental import pallas as pl
from jax.experimental.pallas import tpu as pltpu

F32 = jnp.float32
BF16 = jnp.bfloat16
U32 = jnp.uint32
HIGHEST = lax.Precision.HIGHEST

DEPTH = 1
M_HEADS = 4
M_HD = 128
M_W = M_HEADS * M_HD
CONV_W = 4
G_HEADS = 4
G_DK = 64
G_DV = 128
G_W = G_HEADS * G_DV
G_KW = G_HEADS * G_DK
G_RANK = 16
G_TAU = 16.0
G_CHUNK = 64
N_GROUPS = 4
E_PER_G = 8
N_EXP = N_GROUPS * E_PER_G
D_EXP = 512
ALPHA = (2 * DEPTH) ** 0.25
LN_EPS = 1e-5

LANES = 128
SUBLANES = 8
VMEM_LIMIT = 48 * 1024 * 1024

C_QK = 0
C_VO = 1024
C_GQK = 2048
C_GV = 2560
C_GG = 3072
C_SMALL = 3584
C_TOT = 3712
SM_I, SM_F, SM_A = 0, 8, 16
IN_GATES = 4 * M_W
IN_G = IN_GATES + 2 * M_HEADS
IN_GA = IN_G + 2 * G_KW + 2 * G_W
IN_TOT = IN_GA + G_RANK

FFN_TM = 512
FFN_SUB = 2
GRAN = SUBLANES
G_LAST = LANES - 1


def _cparams(n_axes=1):
    return pltpu.CompilerParams(dimension_semantics=("arbitrary",) * n_axes,
                                vmem_limit_bytes=VMEM_LIMIT)


def _sigmoid(x):
    return 1.0 / (1.0 + jnp.exp(-x))


def _log_sigmoid(x):
    return jnp.minimum(x, 0.0) - jnp.log(1.0 + jnp.exp(-jnp.abs(x)))


def _ada_kernel(c_ref, w_ref, b_ref, o_ref):
    c = c_ref[...]
    ca = (c * _sigmoid(c)).astype(BF16)
    o_ref[...] = jnp.dot(ca, w_ref[...].astype(BF16), preferred_element_type=F32) + b_ref[...]


def _ada(c, w, b):
    B, D = c.shape
    n_out = w.shape[1]
    tn = 1024
    return pl.pallas_call(
        _ada_kernel,
        grid=(n_out // tn,),
        in_specs=[pl.BlockSpec((B, D), lambda j: (0, 0)),
                  pl.BlockSpec((D, tn), lambda j: (0, j)),
                  pl.BlockSpec((1, tn), lambda j: (0, j))],
        out_specs=pl.BlockSpec((B, tn), lambda j: (0, j)),
        out_shape=jax.ShapeDtypeStruct((B, n_out), F32),
        compiler_params=_cparams(),
        name="ada",
    )(c, w, b.reshape(1, n_out))


def _inproj_kernel(x_ref, mod_ref, win_ref, wc_ref, bc_ref, wa_ref, ba_ref, bg_ref,
                   oa_ref, la_ref, g_ref, halo_ref, w_ref, *, tm, tpb, lm):
    i = pl.program_id(0)

    @pl.when(i == 0)
    def _():
        rc = 2 * LANES
        for r in range(0, IN_GATES, rc):
            w_ref[:, r:r + rc] = win_ref[0, r:r + rc, :].T.astype(BF16)
        for r in range(0, C_SMALL - C_GQK, rc):
            w_ref[:, C_GQK + r:C_GQK + r + rc] = win_ref[0, IN_G + r:IN_G + r + rc, :].T.astype(BF16)
        gates = win_ref[0, IN_GATES:IN_G, :]
        z = lambda n: jnp.zeros((n, gates.shape[1]), F32)
        small = jnp.concatenate([gates[0:M_HEADS], z(SM_F - M_HEADS), gates[M_HEADS:2 * M_HEADS],
                                 z(SM_A - SM_F - M_HEADS), win_ref[0, IN_GA:IN_TOT, :],
                                 z(LANES - SM_A - G_RANK)], axis=0)
        w_ref[:, C_SMALL:C_TOT] = small.T.astype(BF16)

    @pl.when(i % tpb == 0)
    def _():
        halo_ref[0:SUBLANES, :] = jnp.zeros((SUBLANES, halo_ref.shape[1]), F32)

    mod = mod_ref[0]
    u = (x_ref[...] * (1.0 + mod[1:2, :]) + mod[0:1, :]).astype(BF16)

    def proj(c0, c1):
        return jnp.dot(u, w_ref[:, c0:c1], preferred_element_type=F32)

    p = proj(C_QK, C_QK + 2 * M_W)
    halo_ref[SUBLANES:SUBLANES + tm, :] = p
    acc = bc_ref[...] + wc_ref[CONV_W - 1:CONV_W, :] * p
    for j in range(CONV_W - 1):
        acc = acc + wc_ref[j:j + 1, :] * halo_ref[pl.ds(SUBLANES - (CONV_W - 1) + j, tm), :]
    halo_ref[0:SUBLANES, :] = p[tm - SUBLANES:, :]
    qk = acc * _sigmoid(acc)
    oa_ref[:, C_QK:C_QK + M_W] = qk[:, :M_W].astype(BF16)
    oa_ref[:, C_QK + M_W:C_QK + 2 * M_W] = (qk[:, M_W:] * (M_HD ** -0.5)).astype(BF16)

    ps = proj(C_SMALL, C_TOT)
    la = jnp.dot(ps.astype(BF16), wa_ref[...], preferred_element_type=F32) + ba_ref[...]
    la_ref[...] = _log_sigmoid(la) * (1.0 / G_TAU)
    pt = ps.T
    gi = pt[SM_I:SM_I + SUBLANES, :] + bg_ref[0:SUBLANES, :]
    gf = _log_sigmoid(pt[SM_F:SM_F + SUBLANES, :] + bg_ref[SUBLANES:2 * SUBLANES, :])
    for j in range(tm // lm):
        g_ref[j, 0:SUBLANES, :] = gi[:, j * lm:(j + 1) * lm]
        g_ref[j, SUBLANES:2 * SUBLANES, :] = gf[:, j * lm:(j + 1) * lm]

    p = proj(C_VO, C_VO + 2 * M_W)
    oa_ref[:, C_VO:C_VO + 2 * M_W] = p.astype(BF16)

    p = proj(C_GQK, C_GQK + G_KW)
    oa_ref[:, C_GQK:C_GQK + G_KW] = (p * (G_DK ** -0.5)).astype(BF16)
    p = proj(C_GQK + G_KW, C_SMALL)
    oa_ref[:, C_GQK + G_KW:C_SMALL] = p.astype(BF16)


def _inproj(x2, mod3, w_in, w_conv, b_conv, wa_pad, b_gla, bg, *, S, tm, lm, layer):
    N, D = x2.shape
    tpb = S // tm
    kern = functools.partial(_inproj_kernel, tm=tm, tpb=tpb, lm=lm)
    return pl.pallas_call(
        kern,
        grid=(N // tm,),
        in_specs=[pl.BlockSpec((tm, D), lambda i: (i, 0)),
                  pl.BlockSpec((1, 6, D), lambda i: (i // tpb, 0, 0)),
                  pl.BlockSpec((1, IN_TOT, D), lambda i: (layer, 0, 0), pipeline_mode=pl.Buffered(1)),
                  pl.BlockSpec((CONV_W, 2 * M_W), lambda i: (0, 0)),
                  pl.BlockSpec((1, 2 * M_W), lambda i: (0, 0)),
                  pl.BlockSpec((LANES, G_KW), lambda i: (0, 0)),
                  pl.BlockSpec((1, G_KW), lambda i: (0, 0)),
                  pl.BlockSpec((2 * SUBLANES, 1), lambda i: (0, 0))],
        out_specs=[pl.BlockSpec((tm, C_SMALL), lambda i: (i, 0)),
                   pl.BlockSpec((tm, G_KW), lambda i: (i, 0)),
                   pl.BlockSpec((tm // lm, 2 * SUBLANES, lm), lambda i: (i, 0, 0))],
        out_shape=[jax.ShapeDtypeStruct((N, C_SMALL), BF16),
                   jax.ShapeDtypeStruct((N, G_KW), F32),
                   jax.ShapeDtypeStruct((N // lm, 2 * SUBLANES, lm), F32)],
        scratch_shapes=[pltpu.VMEM((SUBLANES + tm, 2 * M_W), F32), pltpu.VMEM((D, C_TOT), BF16)],
        compiler_params=_cparams(),
        name="inproj",
    )(x2, mod3, w_in, w_conv, b_conv, wa_pad, b_gla, bg)


def _mlstm_sel():
    sel = np.zeros((2 * LANES, 2 * M_HEADS * M_HD), np.float32)
    for j in range(2 * M_HEADS):
        src = (SUBLANES if j < M_HEADS else 3 * SUBLANES) + j % M_HEADS
        sel[src, M_HD * j:M_HD * (j + 1)] = 1.0
        sel[LANES + src, M_HD * j:M_HD * (j + 1)] = 1.0
    return sel


def _mlstm_kernel(qk_ref, vo_ref, g_ref, u_ref, gain_ref, sel_ref, out_ref, c_ref, zt_ref, a_ref, dec_ref, m_ref,
                  *, L, NC, nb):
    @pl.when(pl.program_id(1) == 0)
    def _():
        c_ref[...] = jnp.zeros_like(c_ref)
        m_ref[...] = jnp.zeros_like(m_ref)

    tril = (lax.broadcasted_iota(jnp.int32, (L, L), 0) >= lax.broadcasted_iota(jnp.int32, (L, L), 1))
    ones_v = jnp.ones((L, M_HD), BF16)
    zpad = jnp.zeros((LANES - 4 * SUBLANES, L), F32)
    zgroup = lax.broadcasted_iota(jnp.int32, (L, LANES), 1) // SUBLANES
    factor_cols = (zgroup == 1) | (zgroup == 3)

    order = [(bi, c) for bi in range(nb) for c in range(NC)]
    f_all = jnp.concatenate([g_ref[bi, c, SUBLANES:2 * SUBLANES, :] for bi, c in order], axis=0)
    i_all = jnp.concatenate([g_ref[bi, c, 0:SUBLANES, :] for bi, c in order], axis=0)
    b_all = jnp.dot(f_all, u_ref[...], preferred_element_type=F32, precision=HIGHEST)
    a_all = i_all - b_all
    lane_all = lax.broadcasted_iota(jnp.int32, a_all.shape, 1)
    g_all = a_all
    s = 1
    while s < L:
        g_all = jnp.maximum(g_all, jnp.where(lane_all >= s, pltpu.roll(g_all, s, 1), -jnp.inf))
        s *= 2
    for bi in range(nb):
        m_prev = m_ref[bi][:, 0:1]
        for c in range(NC):
            ci = bi * NC + c
            r8 = slice(SUBLANES * ci, SUBLANES * (ci + 1))
            a, b = a_all[r8], b_all[r8]
            a_ref[ci] = a
            M = jnp.maximum(g_all[r8], m_prev)
            ML = M[:, L - 1:L]
            Z = jnp.concatenate([M, jnp.exp(m_prev - M), jnp.exp(-(b + M)), jnp.exp(a - ML), zpad],
                                axis=0)
            zt_ref[ci] = Z.T
            dec_ref[ci] = jnp.broadcast_to(jnp.exp(m_prev - ML), (SUBLANES, 2 * M_HD))
            m_prev = b[:, L - 1:L] + ML
        m_ref[bi] = jnp.broadcast_to(m_prev, (SUBLANES, LANES))

    chains = [(bi, h) for bi in range(nb) for h in range(M_HEADS)]
    nt = (((1,), (1,)), ((), ()))
    tn = (((0,), (0,)), ((), ()))

    def chunk(c, carry):
        rows = pl.ds(pl.multiple_of(c * L, L), L)
        Zt = [zt_ref[bi * NC + c] for bi in range(nb)]
        a = [a_ref[bi * NC + c] for bi in range(nb)]
        dec = [dec_ref[bi * NC + c] for bi in range(nb)]
        hs = [slice(h * M_HD, (h + 1) * M_HD) for h in range(M_HEADS)]
        hs2 = [slice(M_W + h * M_HD, M_W + (h + 1) * M_HD) for h in range(M_HEADS)]
        q = [qk_ref[bi, rows, hs[h]] for bi, h in chains]
        k = [qk_ref[bi, rows, hs2[h]] for bi, h in chains]
        vext = [jnp.concatenate([vo_ref[bi, rows, hs[h]], ones_v], axis=1) for bi, h in chains]
        cst = [c_ref[bi * M_HEADS + h] for bi, h in chains]
        n = range(len(chains))
        sc = [lax.dot_general(q[i], k[i], nt, preferred_element_type=F32) for i in n]
        qc = [jnp.dot(q[i], cst[i].astype(BF16), preferred_element_type=F32) for i in n]
        pm = [(sc[i] * jnp.exp(jnp.where(tril, a[bi][h:h + 1, :] - Zt[bi][:, h:h + 1], -jnp.inf))).astype(BF16)
              for i, (bi, h) in enumerate(chains)]
        pv = [jnp.dot(pm[i], vext[i], preferred_element_type=F32) for i in n]
        rep = []
        for bi in range(nb):
            zf = jnp.where(factor_cols, Zt[bi], 0.0)
            zh = zf.astype(BF16)
            zl = (zf - zh.astype(F32)).astype(BF16)
            rep.append(jnp.dot(jnp.concatenate([zh, zl], axis=1), sel_ref[...], preferred_element_type=F32))
        e_inter = [rep[bi][:, M_HD * h:M_HD * (h + 1)] for bi, h in chains]
        w_state = [rep[bi][:, M_HD * (M_HEADS + h):M_HD * (M_HEADS + h + 1)] for bi, h in chains]
        kw = [(w_state[i] * k[i].astype(F32)).astype(BF16) for i in n]
        upd = [lax.dot_general(kw[i], vext[i], tn, preferred_element_type=F32) for i in n]
        for i, (bi, h) in enumerate(chains):
            c_ref[bi * M_HEADS + h] = dec[bi][h:h + 1, :] * cst[i] + upd[i]
            nd = pv[i] + jnp.concatenate([e_inter[i], e_inter[i]], axis=1) * qc[i]
            hh = nd[:, :M_HD] / jnp.maximum(jnp.abs(nd[:, M_HD:]),
                                            Zt[bi][:, 2 * SUBLANES + h:2 * SUBLANES + h + 1])
            hh = _sigmoid(vo_ref[bi, rows, hs2[h]].astype(F32)) * hh
            hn = hh * lax.rsqrt(jnp.mean(hh * hh, axis=-1, keepdims=True) + LN_EPS)
            out_ref[bi, rows, hs[h]] = (hn * gain_ref[:, hs[h]]).astype(BF16)
        return carry

    lax.fori_loop(0, NC, chunk, 0)


def _mlstm(oa, g3, u_tri, gain, *, B, S, L, nb, ts):
    N = oa.shape[0]
    NC = ts // L
    oa3 = oa.reshape(B, S, oa.shape[1])
    g4 = g3.reshape(B, S // L, 2 * SUBLANES, L)
    sel = jnp.asarray(_mlstm_sel(), BF16)
    kern = functools.partial(_mlstm_kernel, L=L, NC=NC, nb=nb)
    out = pl.pallas_call(
        kern,
        grid=(B // nb, S // ts),
        in_specs=[pl.BlockSpec((nb, ts, 2 * M_W), lambda b, t: (b, t, C_QK // (2 * M_W))),
                  pl.BlockSpec((nb, ts, 2 * M_W), lambda b, t: (b, t, C_VO // (2 * M_W))),
                  pl.BlockSpec((nb, NC, 2 * SUBLANES, L), lambda b, t: (b, t, 0, 0)),
                  pl.BlockSpec((L, L), lambda b, t: (0, 0)),
                  pl.BlockSpec((1, M_W), lambda b, t: (0, 0)),
                  pl.BlockSpec(sel.shape, lambda b, t: (0, 0))],
        out_specs=pl.BlockSpec((nb, ts, M_W), lambda b, t: (b, t, 0)),
        out_shape=jax.ShapeDtypeStruct((B, S, M_W), BF16),
        scratch_shapes=[pltpu.VMEM((nb * M_HEADS, M_HD, 2 * M_HD), F32),
                        pltpu.VMEM((nb * NC, L, LANES), F32),
                        pltpu.VMEM((nb * NC, SUBLANES, L), F32),
                        pltpu.VMEM((nb * NC, SUBLANES, 2 * M_HD), F32),
                        pltpu.VMEM((nb, SUBLANES, LANES), F32)],
        compiler_params=_cparams(2),
        name="mlstm",
    )(oa3, oa3, g4, u_tri, gain, sel)
    return out.reshape(N, M_W)


_G_LEVELS = 6
_G_XROW = 2 * G_CHUNK + SUBLANES


def _gla_consts():
    L = G_CHUNK
    t = np.arange(L)
    blocks = [(t[None, :] <= t[:, None]).astype(np.float32),
              (t[None, :] > t[:, None]).astype(np.float32),
              np.ones((SUBLANES, L), np.float32)]
    masks = [np.eye(L, dtype=np.float32)]
    m = 1
    while m < L:
        wl = np.zeros((L, L), np.float32)
        for r in range(L):
            r0 = (r // (2 * m)) * 2 * m + m
            if r % (2 * m) >= m:
                wl[r, r0:r + 1] = 1.0
            else:
                wl[r, r + 1:r0] = 1.0
        blocks.append(wl)
        tt, ss = t[:, None], t[None, :]
        masks.append(((tt // (2 * m) == ss // (2 * m)) & (tt % (2 * m) >= m)
                      & (ss % (2 * m) < m)).astype(np.float32))
        m *= 2
    w = np.concatenate(blocks, axis=0)
    w3 = np.concatenate([w, w, w], axis=1)
    mk = np.stack([np.concatenate([x] * G_HEADS, axis=0) for x in masks])
    return w3, mk


def _gla_kernel(qk_ref, v_ref, gg_ref, la_ref, w3_ref, mk_ref, gain_ref, out_ref, st_ref, *, NC, nb):
    L = G_CHUNK

    @pl.when(pl.program_id(1) == 0)
    def _():
        st_ref[...] = jnp.zeros_like(st_ref)

    lane_head = lax.broadcasted_iota(jnp.int32, (L, G_KW), 1) // G_DK
    br = lax.broadcasted_iota(jnp.int32, (2 * G_DV, LANES), 0) < G_DV
    bl = lax.broadcasted_iota(jnp.int32, (2 * G_DV, LANES), 1) < G_DK
    bmask = br == bl
    nt = (((1,), (1,)), ((), ()))
    tn = (((0,), (0,)), ((), ()))

    def chunk(c, carry):
        rows = pl.ds(pl.multiple_of(c * L, L), L)
        X, q, k = [], [], []
        for bi in range(nb):
            la = la_ref[bi, rows, :]
            hi = la.astype(BF16)
            r1 = la - hi.astype(F32)
            mid = r1.astype(BF16)
            lo = (r1 - mid.astype(F32)).astype(BF16)
            stk = jnp.concatenate([hi, mid, lo], axis=0)
            X.append(jnp.exp(jnp.dot(w3_ref[...], stk, preferred_element_type=F32)))
            q.append(qk_ref[bi, rows, 0:G_KW].astype(F32))
            k.append(qk_ref[bi, rows, G_KW:2 * G_KW].astype(F32))

        sc = [[None] * (_G_LEVELS + 1) for _ in range(nb)]
        for lev in range(_G_LEVELS + 1):
            for bi in range(nb):
                if lev == 0:
                    qt, kt = q[bi], k[bi]
                else:
                    xl = X[bi][_G_XROW + L * (lev - 1):_G_XROW + L * lev, :]
                    qt, kt = q[bi] * xl, k[bi] * xl
                q4 = jnp.concatenate([jnp.where(lane_head == h, qt, 0.0) for h in range(G_HEADS)],
                                     axis=0).astype(BF16)
                sc[bi][lev] = lax.dot_general(q4, kt.astype(BF16), nt, preferred_element_type=F32)
        Ab = []
        for bi in range(nb):
            A = sc[bi][0] * mk_ref[0]
            for lev in range(1, _G_LEVELS + 1):
                A = A + sc[bi][lev] * mk_ref[lev]
            Ab.append(A.astype(BF16))

        for bi in range(nb):
            gg = gg_ref[bi, rows, :].astype(F32)
            gate = gg * _sigmoid(gg)
            for p in range(2):
                ls = slice(LANES * p, LANES * (p + 1))
                vp = v_ref[bi, rows, 2 * G_DV * p:2 * G_DV * (p + 1)]
                oi = [jnp.dot(Ab[bi][L * (2 * p + hh):L * (2 * p + hh + 1)],
                              vp[:, G_DV * hh:G_DV * (hh + 1)], preferred_element_type=F32)
                      for hh in range(2)]
                st = st_ref[bi, p]
                qc = (q[bi][:, ls] * X[bi][0:L, ls]).astype(BF16)
                o_inter = lax.dot_general(qc, st.astype(BF16), nt, preferred_element_type=F32)
                kc = (k[bi][:, ls] * X[bi][L:2 * L, ls]).astype(BF16)
                upd = lax.dot_general(vp, kc, tn, preferred_element_type=F32)
                dec = X[bi][2 * L:2 * L + 1, ls]
                st_ref[bi, p] = jnp.where(bmask, dec * st + upd, 0.0)
                for hh in range(2):
                    o = o_inter[:, G_DV * hh:G_DV * (hh + 1)] + oi[hh]
                    hn = o * lax.rsqrt(jnp.mean(o * o, axis=-1, keepdims=True) + LN_EPS)
                    hs = slice(G_DV * (2 * p + hh), G_DV * (2 * p + hh + 1))
                    out_ref[bi, rows, hs] = (hn * gain_ref[:, hs] * gate[:, hs]).astype(BF16)
        return carry

    lax.fori_loop(0, NC, chunk, 0)


def _gla(oa, la, w3, mk, gain, *, B, S, nb, ts):
    N = oa.shape[0]
    oa3 = oa.reshape(B, S, oa.shape[1])
    la3 = la.reshape(B, S, G_KW)
    kern = functools.partial(_gla_kernel, NC=ts // G_CHUNK, nb=nb)
    out = pl.pallas_call(
        kern,
        grid=(B // nb, S // ts),
        in_specs=[pl.BlockSpec((nb, ts, 2 * G_KW), lambda b, t: (b, t, C_GQK // (2 * G_KW))),
                  pl.BlockSpec((nb, ts, G_W), lambda b, t: (b, t, C_GV // G_W)),
                  pl.BlockSpec((nb, ts, G_W), lambda b, t: (b, t, C_GG // G_W)),
                  pl.BlockSpec((nb, ts, G_KW), lambda b, t: (b, t, 0)),
                  pl.BlockSpec(w3.shape, lambda b, t: (0, 0)),
                  pl.BlockSpec(mk.shape, lambda b, t: (0, 0, 0)),
                  pl.BlockSpec((1, G_W), lambda b, t: (0, 0))],
        out_specs=pl.BlockSpec((nb, ts, G_W), lambda b, t: (b, t, 0)),
        out_shape=jax.ShapeDtypeStruct((B, S, G_W), BF16),
        scratch_shapes=[pltpu.VMEM((nb, 2, 2 * G_DV, LANES), F32)],
        compiler_params=_cparams(2),
        name="gla",
    )(oa3, oa3, oa3, la3, w3, mk, gain)
    return out.reshape(N, G_W)


def _layer_norm(z, g, b):
    mu = jnp.mean(z, axis=-1, keepdims=True)
    zc = z - mu
    var = jnp.mean(zc * zc, axis=-1, keepdims=True)
    return zc * lax.rsqrt(var + LN_EPS) * g + b


def _outproj_kernel(hm_ref, hg_ref, wf_ref, x_ref, mod_ref, g_ref, b_ref, wrg_ref, wre_ref, br_ref,
                    x1_ref, u2_ref, rrow_ref, w_ref, wr_ref, *, tb, nh):
    @pl.when(pl.program_id(0) == 0)
    def _():
        w_ref[...] = wf_ref[...].astype(BF16)
        z = lambda n: jnp.zeros((n, wrg_ref.shape[2]), F32)
        wt = jnp.concatenate([wrg_ref[0], z(SUBLANES - N_GROUPS), wre_ref[0],
                              z(LANES - SUBLANES - N_EXP)], axis=0).T
        hi = wt.astype(BF16)
        wr_ref[:, 0:LANES] = hi
        wr_ref[:, LANES:2 * LANES] = (wt - hi.astype(F32)).astype(BF16)

    mod = mod_ref[0]
    blocks = [slice(tb * j, tb * (j + 1)) for j in range(nh)]
    y = [jnp.dot(hm_ref[r, :], w_ref[0:M_W, :], preferred_element_type=F32)
         + jnp.dot(hg_ref[r, :], w_ref[M_W:M_W + G_W, :], preferred_element_type=F32) for r in blocks]
    u2 = []
    for j, r in enumerate(blocks):
        z = ALPHA * x_ref[r, :] + (1.0 + mod[2:3, :]) * y[j]
        x1 = _layer_norm(z, g_ref[...], b_ref[...])
        x1_ref[r, :] = x1
        u2.append(x1 * (1.0 + mod[4:5, :]) + mod[3:4, :])
        u2_ref[r, :] = u2[j].astype(BF16)

    u2h = [u.astype(BF16) for u in u2]
    u2l = [(u2[j] - u2h[j].astype(F32)).astype(BF16) for j in range(nh)]
    lh = [jnp.dot(u, wr_ref[...], preferred_element_type=F32) for u in u2h]
    ll = [jnp.dot(u, wr_ref[:, 0:LANES], preferred_element_type=F32) for u in u2l]
    for j in range(nh):
        logits = lh[j][:, 0:LANES] + lh[j][:, LANES:2 * LANES] + ll[j] + br_ref[...]
        rrow_ref[j] = _route_select(logits.T, tb)


def _route_select(lt, tm):
    row = lax.broadcasted_iota(jnp.int32, (SUBLANES, tm), 0)
    gl = jnp.where(row < N_GROUPS, lt[0:SUBLANES, :], -jnp.inf)
    gmax = jnp.max(gl, axis=0, keepdims=True)
    gsel = jnp.min(jnp.where(gl == gmax, row, SUBLANES), axis=0, keepdims=True)
    pg = 1.0 / jnp.sum(jnp.exp(gl - gmax), axis=0, keepdims=True)
    ein = jnp.zeros((SUBLANES, tm), F32)
    for g in range(N_GROUPS):
        ein = jnp.where(gsel == g, lt[SUBLANES * (g + 1):SUBLANES * (g + 2), :], ein)
    v1 = jnp.max(ein, axis=0, keepdims=True)
    i1 = jnp.min(jnp.where(ein == v1, row, SUBLANES), axis=0, keepdims=True)
    rest = jnp.where(row == i1, -jnp.inf, ein)
    v2 = jnp.max(rest, axis=0, keepdims=True)
    i2 = jnp.min(jnp.where(rest == v2, row, SUBLANES), axis=0, keepdims=True)
    t2 = jnp.exp(v2 - v1)
    p1 = 1.0 / (1.0 + t2)
    e0 = (gsel * E_PER_G + i1).astype(F32)
    e1 = (gsel * E_PER_G + i2).astype(F32)
    return jnp.concatenate([e0, e1, pg * p1, pg * (t2 * p1), jnp.zeros((SUBLANES - 4, tm), F32)], axis=0)


def _outproj(hm, hg, w_out, x2, mod3, g, b, wrg_t, wre_t, br, *, S, tb, nh, layer):
    N, D = x2.shape
    tm = tb * nh
    tpb = S // tm
    kern = functools.partial(_outproj_kernel, tb=tb, nh=nh)
    return pl.pallas_call(
        kern,
        grid=(N // tm,),
        in_specs=[pl.BlockSpec((tm, M_W), lambda i: (i, 0)),
                  pl.BlockSpec((tm, G_W), lambda i: (i, 0)),
                  pl.BlockSpec((M_W + G_W, D), lambda i: (0, 0), pipeline_mode=pl.Buffered(1)),
                  pl.BlockSpec((tm, D), lambda i: (i, 0)),
                  pl.BlockSpec((1, 6, D), lambda i: (i // tpb, 0, 0)),
                  pl.BlockSpec((1, D), lambda i: (0, 0)),
                  pl.BlockSpec((1, D), lambda i: (0, 0)),
                  pl.BlockSpec((1, N_GROUPS, D), lambda i: (layer, 0, 0)),
                  pl.BlockSpec((1, N_EXP, D), lambda i: (layer, 0, 0)),
                  pl.BlockSpec((1, LANES), lambda i: (0, 0))],
        out_specs=[pl.BlockSpec((tm, D), lambda i: (i, 0)),
                   pl.BlockSpec((tm, D), lambda i: (i, 0)),
                   pl.BlockSpec((nh, SUBLANES, tb), lambda i: (i, 0, 0))],
        out_shape=[jax.ShapeDtypeStruct((N, D), F32),
                   jax.ShapeDtypeStruct((N, D), BF16),
                   jax.ShapeDtypeStruct((N // tb, SUBLANES, tb), F32)],
        scratch_shapes=[pltpu.VMEM((M_W + G_W, D), BF16), pltpu.VMEM((D, 2 * LANES), BF16)],
        compiler_params=_cparams(),
        name="outproj",
    )(hm, hg, w_out, x2, mod3, g, b, wrg_t, wre_t, br)


def _slots_per_tile(tb):
    worst = 2 * tb + N_EXP * (GRAN - 1)
    return -(-worst // LANES) * LANES


def _ffn_tiles(n_tok, tb):
    worst_rows = 2 * n_tok + (n_tok // tb) * N_EXP * (GRAN - 1)
    return -(-worst_rows // FFN_TM) + N_EXP


def _route_kernel(rr_ref, u_ref, lt_ref, srow_ref, col_ref, gd_ref, meta_ref, mg_ref, part_ref,
                  *, NT, tb, TM):
    iota_e = lax.broadcasted_iota(jnp.int32, (N_EXP, tb), 0).astype(F32)
    glane = lax.broadcasted_iota(jnp.int32, (N_EXP, LANES), 1).astype(F32)
    ltri = lt_ref[...]

    def prefix_e(col):
        return jnp.dot(ltri, jnp.broadcast_to(col, (N_EXP, LANES)),
                       preferred_element_type=F32, precision=HIGHEST)[:, 0:1]

    def p1(j, run8):
        r = rr_ref[j]
        oh0 = jnp.where(iota_e == r[0:1, :], 1.0, 0.0)
        oh1 = jnp.where(iota_e == r[1:2, :], 1.0, 0.0)
        cum0 = jnp.dot(oh0.astype(BF16), u_ref[...], preferred_element_type=F32)
        cum1 = jnp.dot(oh1.astype(BF16), u_ref[...], preferred_element_type=F32)
        c0 = jnp.sum(oh0, axis=1, keepdims=True)
        n8 = jnp.floor((c0 + jnp.sum(oh1, axis=1, keepdims=True) + (GRAN - 1.0)) * (1.0 / GRAN))
        lo8 = prefix_e(n8)
        s0 = jnp.sum(oh0 * (GRAN * lo8 + cum0 - 1.0), axis=0, keepdims=True)
        s1 = jnp.sum(oh1 * (GRAN * lo8 + c0 + cum1 - 1.0), axis=0, keepdims=True)
        info = jnp.concatenate([s0, s1, r[2:4, :], jnp.zeros((SUBLANES - 4, tb), F32)], axis=0)
        srow_ref[j] = info
        col_ref[pl.ds(pl.multiple_of(j * tb, tb), tb), :] = jnp.concatenate(
            [info, jnp.zeros((LANES - SUBLANES, tb), F32)], axis=0).T
        mg = jnp.where((lo8 <= glane) & (glane < lo8 + n8), 1.0, 0.0)
        mg_ref[j] = mg
        part = jnp.sum(mg * (run8 + glane - lo8), axis=0, keepdims=True)
        gcnt = jnp.broadcast_to(jnp.sum(n8, axis=0, keepdims=True), (1, LANES))
        part_ref[j] = jnp.concatenate([part, gcnt, jnp.zeros((SUBLANES - 2, LANES), F32)], axis=0)
        return run8 + n8

    tot8 = lax.fori_loop(0, NT, p1, jnp.zeros((N_EXP, 1), F32), unroll=8 if NT % 8 == 0 else 1)
    seg_t = jnp.floor((tot8 * GRAN + (TM - 1.0)) * (1.0 / TM))
    base_t = prefix_e(seg_t)
    base8 = base_t * (TM // GRAN)
    lane1 = lax.broadcasted_iota(jnp.int32, (1, LANES), 1)

    def p2(j, carry):
        pr = part_ref[j]
        dst = (pr[0:1, :] + jnp.sum(mg_ref[j] * base8, axis=0, keepdims=True)) * GRAN
        gd_ref[j] = jnp.where(lane1 == G_LAST, pr[1:2, :], dst).astype(jnp.int32)
        return carry

    lax.fori_loop(0, NT, p2, 0, unroll=8 if NT % 8 == 0 else 1)
    eye = jnp.where(glane == lax.broadcasted_iota(jnp.int32, (N_EXP, LANES), 0).astype(F32), 1.0, 0.0)
    tail_row = jnp.sum(eye * ((base8 + tot8) * GRAN), axis=0, keepdims=True)
    tail_n8 = jnp.sum(eye * (seg_t * (TM // GRAN) - tot8), axis=0, keepdims=True)
    nv_l = jnp.broadcast_to(jnp.sum(seg_t, axis=0, keepdims=True), (1, LANES))
    gd_ref[NT] = jnp.where(lane1 == G_LAST, nv_l, tail_row).astype(jnp.int32)
    gd_ref[NT + 1] = tail_n8.astype(jnp.int32)
    ti = lax.broadcasted_iota(jnp.int32, (N_EXP, tb), 1).astype(F32)
    te = jnp.sum(jnp.where(base_t <= ti, 1.0, 0.0), axis=0, keepdims=True) - 1.0
    nv = jnp.broadcast_to(jnp.sum(seg_t, axis=0, keepdims=True), (1, tb))
    own = jnp.where((base_t <= ti) & (ti < base_t + seg_t), 1.0, 0.0)
    vr = jnp.sum(own * jnp.clip(tot8 * GRAN - (ti - base_t) * TM, 0.0, TM), axis=0, keepdims=True)
    meta_ref[...] = jnp.concatenate([te, nv, vr, jnp.zeros((SUBLANES - 3, tb), F32)],
                                    axis=0).astype(jnp.int32)


def _route(rrow, u_cnt, ltri, *, TM):
    NT, _, tb = rrow.shape
    kern = functools.partial(_route_kernel, NT=NT, tb=tb, TM=TM)
    full3 = lambda i: (0, 0, 0)
    return pl.pallas_call(
        kern,
        grid=(1,),
        in_specs=[pl.BlockSpec((NT, SUBLANES, tb), full3),
                  pl.BlockSpec((tb, tb), lambda i: (0, 0)),
                  pl.BlockSpec((N_EXP, N_EXP), lambda i: (0, 0))],
        out_specs=[pl.BlockSpec((NT, SUBLANES, tb), full3),
                   pl.BlockSpec((NT * tb, LANES), lambda i: (0, 0)),
                   pl.BlockSpec((NT + 2, 1, LANES), full3),
                   pl.BlockSpec((SUBLANES, tb), lambda i: (0, 0))],
        out_shape=[jax.ShapeDtypeStruct((NT, SUBLANES, tb), F32),
                   jax.ShapeDtypeStruct((NT * tb, LANES), F32),
                   jax.ShapeDtypeStruct((NT + 2, 1, LANES), jnp.int32),
                   jax.ShapeDtypeStruct((SUBLANES, tb), jnp.int32)],
        scratch_shapes=[pltpu.VMEM((NT, N_EXP, LANES), F32), pltpu.VMEM((NT, SUBLANES, LANES), F32)],
        compiler_params=_cparams(),
        name="route",
    )(rrow, u_cnt, ltri)


_HI_MASK = 0xFFFF0000


def _pack_halves(x):
    c = x.shape[1] // 2
    lo = lax.bitcast_convert_type(x[:, :c], U32)
    hi = lax.bitcast_convert_type(x[:, c:], U32)
    return (lo >> 16) | (hi & U32(_HI_MASK))


def _unpack_halves(w):
    lo = lax.bitcast_convert_type(w << 16, F32)
    hi = lax.bitcast_convert_type(w & U32(_HI_MASK), F32)
    return jnp.concatenate([lo, hi], axis=1).astype(BF16)


def _granule_copy(src_ref, src_row, dst_ref, dst_row, sem, n=1):
    cols = pl.ds(0, min(src_ref.shape[-1], dst_ref.shape[-1]))
    return pltpu.make_async_copy(src_ref.at[pl.ds(src_row, n * GRAN), cols],
                                 dst_ref.at[pl.ds(dst_row, n * GRAN), cols], sem)


def _for_granules(n, body, unroll=4):
    def blk(i, carry):
        for t in range(unroll):
            body(i * unroll + t)
        return carry

    def one(g, carry):
        body(g)
        return carry

    nblk = n // unroll
    lax.fori_loop(0, nblk, blk, 0)
    lax.fori_loop(nblk * unroll, n, one, 0)


def _wait_granules(n, src_ref, dst_ref, sem, n_max):
    b = 1
    while b <= n_max:
        @pl.when((n & b) != 0)
        def _(b=b):
            _granule_copy(src_ref, 0, dst_ref, 0, sem, n=b).wait()
        b *= 2


def _dispatch_kernel(gd_ref, srow_ref, u_ref, xs_ref, buf, zbuf, sems, *, NT, SL, TM, n_tiles):
    j = pl.program_id(0)
    slot = j % 2
    zsem = sems.at[2]

    def drain(tile, sl):
        _wait_granules(gd_ref[tile, G_LAST], buf.at[sl], xs_ref, sems.at[sl], SL // GRAN)

    def tile_fill(t):
        return pltpu.make_async_copy(zbuf, xs_ref.at[pl.ds(pl.multiple_of(t * TM, TM), TM), :], zsem)

    def zero_fill(wait):
        for e in range(N_EXP):
            n, row0 = gd_ref[NT + 1, e], gd_ref[NT, e]
            b = TM // GRAN // 2
            while b >= 1:
                @pl.when((n & b) != 0)
                def _(b=b, n=n, row0=row0):
                    start = pl.multiple_of(row0 + ((n >> b.bit_length()) << b.bit_length()) * GRAN, GRAN)
                    cp = pltpu.make_async_copy(zbuf.at[pl.ds(0, b * GRAN), :],
                                               xs_ref.at[pl.ds(start, b * GRAN), :], zsem)
                    cp.wait() if wait else cp.start()
                b //= 2

        def zt(t, carry):
            tile_fill(t).wait() if wait else tile_fill(t).start()
            return carry
        lax.fori_loop(gd_ref[NT, G_LAST], n_tiles, zt, 0)

    @pl.when(j == 0)
    def _():
        zbuf[...] = jnp.zeros_like(zbuf)
        zero_fill(False)

    @pl.when(j >= 2)
    def _():
        drain(j - 2, slot)

    s = srow_ref[0]
    rows = lax.broadcasted_iota(jnp.int32, (SL, s.shape[1]), 0).astype(F32)
    m0 = rows == s[0:1, :]
    m1 = rows == s[1:2, :]
    oh = jnp.where(m0 | m1, 1.0, 0.0).astype(BF16)
    dw = u_ref.shape[1] // 2
    buf[slot, :, 0:dw] = _pack_halves(jnp.dot(oh, u_ref[...], preferred_element_type=F32))
    wrow = jnp.sum(jnp.where(m0, s[2:3, :], 0.0) + jnp.where(m1, s[3:4, :], 0.0), axis=1, keepdims=True)
    buf[slot, :, dw:dw + LANES] = lax.bitcast_convert_type(jnp.broadcast_to(wrow, (SL, LANES)), U32)

    def issue(g):
        _granule_copy(buf.at[slot], pl.multiple_of(g * GRAN, GRAN), xs_ref,
                      pl.multiple_of(gd_ref[j, g], GRAN), sems.at[slot]).start()

    _for_granules(gd_ref[j, G_LAST], issue)

    @pl.when(j == NT - 1)
    def _():
        drain(j, slot)
        if NT > 1:
            drain(j - 1, 1 - slot)
        zero_fill(True)


def _dispatch(gd, srow, u2, *, n_tiles, TM):
    N, D = u2.shape
    NT, _, tb = srow.shape
    SL = _slots_per_tile(tb)
    n_rows = n_tiles * TM
    kern = functools.partial(_dispatch_kernel, NT=NT, SL=SL, TM=TM, n_tiles=n_tiles)
    grid_spec = pltpu.PrefetchScalarGridSpec(
        num_scalar_prefetch=1,
        grid=(NT,),
        in_specs=[pl.BlockSpec((1, SUBLANES, tb), lambda j, gd: (j, 0, 0)),
                  pl.BlockSpec((tb, D), lambda j, gd: (j, 0))],
        out_specs=pl.BlockSpec(memory_space=pl.ANY),
        scratch_shapes=[pltpu.VMEM((2, SL, D // 2 + LANES), U32), pltpu.VMEM((TM, D // 2 + LANES), U32),
                        pltpu.SemaphoreType.DMA((3,))],
    )
    return pl.pallas_call(
        kern,
        grid_spec=grid_spec,
        out_shape=jax.ShapeDtypeStruct((n_rows, D // 2 + LANES), U32),
        compiler_params=_cparams(),
        name="dispatch",
    )(gd, srow, u2)


def _ffn_kernel(te_ref, nv_ref, vr_ref, xs_ref, wg_ref, wu_ref, wd_ref, o_ref, wgb, wub, wdb, sg, su, sd, slot_ref,
                sems):
    i = pl.program_id(0)
    nv = nv_ref[0]
    e = te_ref[i]

    def weight_copies(ex, sl):
        return (pltpu.make_async_copy(wg_ref.at[ex], sg.at[sl], sems.at[sl]),
                pltpu.make_async_copy(wu_ref.at[ex], su.at[sl], sems.at[sl]),
                pltpu.make_async_copy(wd_ref.at[ex], sd.at[sl], sems.at[sl]))

    @pl.when(i == 0)
    def _():
        slot_ref[0] = 0
        for cp in weight_copies(e, 0):
            cp.start()

    new_expert = (i < nv) & ((i == 0) | (e != te_ref[jnp.maximum(i - 1, 0)]))

    @pl.when(new_expert)
    def _():
        sl = slot_ref[0]
        for cp in weight_copies(e, sl):
            cp.wait()
        nxt = lax.while_loop(lambda t: (t < nv) & (te_ref[jnp.minimum(t, nv - 1)] == e), lambda t: t + 1, i + 1)

        @pl.when(nxt < nv)
        def _():
            for cp in weight_copies(te_ref[nxt], 1 - sl):
                cp.start()

    hm = xs_ref.shape[0] // FFN_SUB
    dw = xs_ref.shape[1] - LANES

    def swiglu_rows(nsub, cast):
        if cast:
            sl = slot_ref[0]
            wgb[...] = sg[sl].astype(BF16)
            wub[...] = su[sl].astype(BF16)
            wdb[...] = sd[sl].astype(BF16)
            slot_ref[0] = 1 - sl
        halves = tuple(slice(hm * j, hm * (j + 1)) for j in range(nsub))
        x = [_unpack_halves(xs_ref[r, 0:dw]) for r in halves]
        g = [jnp.dot(x[j], wgb[...], preferred_element_type=F32) for j in range(nsub)]
        u = [jnp.dot(x[j], wub[...], preferred_element_type=F32) for j in range(nsub)]
        h = [(g[j] * _sigmoid(g[j]) * u[j]).astype(BF16) for j in range(nsub)]
        y = [jnp.dot(h[j], wdb[...], preferred_element_type=F32) for j in range(nsub)]
        for j in range(nsub):
            wt = lax.bitcast_convert_type(xs_ref[halves[j], dw:dw + LANES], F32)
            yw = y[j] * jnp.concatenate([wt] * (2 * dw // LANES), axis=1)
            o_ref[halves[j], 0:dw] = _pack_halves(yw.astype(BF16).astype(F32))
            o_ref[halves[j], dw:dw + LANES] = xs_ref[halves[j], dw:dw + LANES]
        if nsub < FFN_SUB:
            o_ref[hm * nsub:, :] = jnp.zeros((hm * (FFN_SUB - nsub), dw + LANES), U32)

    used = vr_ref[i]
    for nsub in range(1, FFN_SUB + 1):
        lo, hi = hm * (nsub - 1), hm * nsub
        rows_here = (i < nv) & (used > lo) & ((used <= hi) if nsub < FFN_SUB else True)
        pl.when(rows_here & new_expert)(functools.partial(swiglu_rows, nsub, True))
        pl.when(rows_here & jnp.logical_not(new_expert))(functools.partial(swiglu_rows, nsub, False))


def _ffn(te, nv, vr, xs, wg, wu, wd, *, TM):
    P, XW = xs.shape
    DW = XW - LANES
    D = 2 * DW
    n_tiles = P // TM
    used_tile = lambda i, te, nv, vr: (jnp.maximum(jnp.minimum(i, nv[0] - 1), 0), 0)
    grid_spec = pltpu.PrefetchScalarGridSpec(
        num_scalar_prefetch=3,
        grid=(n_tiles,),
        in_specs=[pl.BlockSpec((TM, XW), used_tile),
                  pl.BlockSpec(memory_space=pl.ANY),
                  pl.BlockSpec(memory_space=pl.ANY),
                  pl.BlockSpec(memory_space=pl.ANY)],
        out_specs=pl.BlockSpec((TM, XW), used_tile),
        scratch_shapes=[pltpu.VMEM((D, D_EXP), BF16), pltpu.VMEM((D, D_EXP), BF16),
                        pltpu.VMEM((D_EXP, D), BF16),
                        pltpu.VMEM((2, D, D_EXP), F32), pltpu.VMEM((2, D, D_EXP), F32),
                        pltpu.VMEM((2, D_EXP, D), F32), pltpu.SMEM((1,), jnp.int32),
                        pltpu.SemaphoreType.DMA((2,))],
    )
    return pl.pallas_call(
        _ffn_kernel,
        grid_spec=grid_spec,
        out_shape=jax.ShapeDtypeStruct((P, XW), U32),
        input_output_aliases={3: 0},
        compiler_params=_cparams(),
        name="ffn",
    )(te, nv, vr, xs, wg, wu, wd)


def _combine_kernel(gd_ref, ys_ref, col_ref, x1_ref, mod_ref, g_ref, b_ref, o_ref, buf, sems, *, NT, SL):
    j = pl.program_id(0)
    slot = j % 2

    def fetch(tile, sl):
        def f(g):
            _granule_copy(ys_ref, pl.multiple_of(gd_ref[tile, g], GRAN), buf.at[sl],
                          pl.multiple_of(g * GRAN, GRAN), sems.at[sl]).start()
        _for_granules(gd_ref[tile, G_LAST], f)

    @pl.when(j == 0)
    def _():
        fetch(0, 0)

    @pl.when(j + 1 < NT)
    def _():
        fetch(j + 1, 1 - slot)

    ng = gd_ref[j, G_LAST]

    _wait_granules(ng, ys_ref, buf.at[slot], sems.at[slot], SL // GRAN)

    rows = lax.broadcasted_iota(jnp.int32, (SL, 1), 0)
    yb = _unpack_halves(jnp.where(rows < ng * GRAN, buf[slot], U32(0)))
    col = col_ref[...]
    tb = col.shape[0]
    lanes = lax.broadcasted_iota(jnp.int32, (tb, SL), 1).astype(F32)
    sel = jnp.where((lanes == col[:, 0:1]) | (lanes == col[:, 1:2]), 1.0, 0.0).astype(BF16)
    y = jnp.dot(sel, yb, preferred_element_type=F32)
    mod = mod_ref[0]
    z = ALPHA * x1_ref[...] + (1.0 + mod[5:6, :]) * y
    o_ref[...] = _layer_norm(z, g_ref[...], b_ref[...])


def _combine(gd, ys, col, x1, mod3, g, b, *, S, tb):
    N, D = x1.shape
    NT = N // tb
    tpb = S // tb
    SL = _slots_per_tile(tb)
    kern = functools.partial(_combine_kernel, NT=NT, SL=SL)
    grid_spec = pltpu.PrefetchScalarGridSpec(
        num_scalar_prefetch=1,
        grid=(NT,),
        in_specs=[pl.BlockSpec(memory_space=pl.ANY),
                  pl.BlockSpec((tb, LANES), lambda j, gd: (j, 0)),
                  pl.BlockSpec((tb, D), lambda j, gd: (j, 0)),
                  pl.BlockSpec((1, 6, D), lambda j, gd: (j // tpb, 0, 0)),
                  pl.BlockSpec((1, D), lambda j, gd: (0, 0)),
                  pl.BlockSpec((1, D), lambda j, gd: (0, 0))],
        out_specs=pl.BlockSpec((tb, D), lambda j, gd: (j, 0)),
        scratch_shapes=[pltpu.VMEM((2, SL, D // 2), U32), pltpu.SemaphoreType.DMA((2,))],
    )
    return pl.pallas_call(
        kern,
        grid_spec=grid_spec,
        out_shape=jax.ShapeDtypeStruct((N, D), F32),
        compiler_params=_cparams(),
        name="combine",
    )(gd, ys, col, x1, mod3, g, b)


def _layer(x, c, l, w_ada, b_ada, w_in, w_conv, b_conv, b_igate, b_fgate, mlstm_norm_g, w_gla_a, b_gla_a,
           gla_norm_g, w_out, ln1_g, ln1_b, w_route_group, b_route_group, w_route_expert, b_route_expert,
           w_gate, w_up, w_down, ln2_g, ln2_b):
    B, S, D = x.shape
    N = B * S
    x2 = x.reshape(N, D)
    tm_in = min(512, S)
    tm = min(256, S)
    lm = min(256, S)
    assert S % tm_in == 0 and S % tm == 0 and tm_in % lm == 0 and S % G_CHUNK == 0
    assert w_in.shape[1:] == (D, IN_TOT) and w_gate.shape[1:] == (N_EXP, D, D_EXP)

    mod3 = _ada(c, w_ada[l], b_ada[l]).reshape(B, 6, D)

    wa_pad = jnp.zeros((LANES, G_KW), F32).at[SM_A:SM_A + G_RANK].set(w_gla_a[l]).astype(BF16)
    bg = (jnp.zeros((2 * SUBLANES, 1), F32).at[0:M_HEADS, 0].set(b_igate[l])
          .at[SUBLANES:SUBLANES + M_HEADS, 0].set(b_fgate[l]))
    oa, la, g3 = _inproj(x2, mod3, jnp.swapaxes(w_in, 1, 2), w_conv[l], b_conv[l].reshape(1, -1), wa_pad,
                         b_gla_a[l].reshape(1, -1), bg, S=S, tm=tm_in, lm=lm, layer=l)

    u_tri = jnp.asarray(np.triu(np.ones((lm, lm), np.float32)))
    nb = 4 if B % 4 == 0 else (2 if B % 2 == 0 else 1)
    ts = min(512, S)
    hm = _mlstm(oa, g3, u_tri, mlstm_norm_g[l].reshape(1, -1), B=B, S=S, L=lm, nb=nb, ts=ts)
    w3_np, mk_np = _gla_consts()
    hg = _gla(oa, la, jnp.asarray(w3_np, BF16), jnp.asarray(mk_np), gla_norm_g[l].reshape(1, -1), B=B, S=S,
              nb=nb, ts=ts)

    br = (jnp.zeros((1, LANES), F32).at[0, 0:N_GROUPS].set(b_route_group[l])
          .at[0, SUBLANES:SUBLANES + N_EXP].set(b_route_expert[l]))
    x1, u2, rrow = _outproj(hm, hg, w_out[l], x2, mod3, ln1_g[l].reshape(1, -1), ln1_b[l].reshape(1, -1),
                            jnp.swapaxes(w_route_group, 1, 2), jnp.swapaxes(w_route_expert, 1, 2), br,
                            S=S, tb=tm, nh=4 if S % (4 * tm) == 0 else 1, layer=l)

    u_cnt = jnp.asarray(np.triu(np.ones((tm, tm), np.float32)), BF16)
    ltri = jnp.asarray(np.tril(np.ones((N_EXP, N_EXP), np.float32), -1))
    srow, col, gd3, meta = _route(rrow, u_cnt, ltri, TM=FFN_TM)
    gd = gd3.reshape(N // tm + 2, LANES)
    n_tiles = _ffn_tiles(N, tm)
    te, nv, vr = meta[0, :n_tiles], meta[1, 0:1], meta[2, :n_tiles]

    xs = _dispatch(gd, srow, u2, n_tiles=n_tiles, TM=FFN_TM)
    ys = _ffn(te, nv, vr, xs, w_gate[l], w_up[l], w_down[l], TM=FFN_TM)
    out = _combine(gd, ys, col, x1, mod3, ln2_g[l].reshape(1, -1), ln2_b[l].reshape(1, -1), S=S, tb=tm)
    return out.reshape(B, S, D)


def kernel(x, c, w_ada, b_ada, w_in, w_conv, b_conv, b_igate, b_fgate, mlstm_norm_g, w_gla_a, b_gla_a,
           gla_norm_g, w_out, ln1_g, ln1_b, w_route_group, b_route_group, w_route_expert, b_route_expert,
           w_gate, w_up, w_down, ln2_g, ln2_b):
    for l in range(DEPTH):
        x = _layer(x, c, l, w_ada, b_ada, w_in, w_conv, b_conv, b_igate, b_fgate, mlstm_norm_g, w_gla_a,
                   b_gla_a, gla_norm_g, w_out, ln1_g, ln1_b, w_route_group, b_route_group, w_route_expert,
                   b_route_expert, w_gate, w_up, w_down, ln2_g, ln2_b)
    return x
```

```python
import functools

import numpy as np
import jax
import jax.numpy as jnp
from jax import lax
from jax.experimental import pallas as pl
from jax.experimental.pallas import tpu as pltpu

F32 = jnp.float32
BF16 = jnp.bfloat16
U32 = jnp.uint32
HIGHEST = lax.Precision.HIGHEST

DEPTH = 1
M_HEADS = 4
M_HD = 128
M_W = M_HEADS * M_HD
CONV_W = 4
G_HEADS = 4
G_DK = 64
G_DV = 128
G_W = G_HEADS * G_DV
G_KW = G_HEADS * G_DK
G_RANK = 16
G_TAU = 16.0
G_CHUNK = 64
N_GROUPS = 4
E_PER_G = 8
N_EXP = N_GROUPS * E_PER_G
D_EXP = 512
ALPHA = (2 * DEPTH) ** 0.25
LN_EPS = 1e-5

LANES = 128
SUBLANES = 8
VMEM_LIMIT = 48 * 1024 * 1024

C_QK = 0
C_VO = 1024
C_GQK = 2048
C_GV = 2560
C_GG = 3072
C_SMALL = 3584
C_TOT = 3712
SM_I, SM_F, SM_A = 0, 8, 16
IN_GATES = 4 * M_W
IN_G = IN_GATES + 2 * M_HEADS
IN_GA = IN_G + 2 * G_KW + 2 * G_W
IN_TOT = IN_GA + G_RANK

FFN_TM = 512
FFN_SUB = 2
GRAN = SUBLANES
G_LAST = LANES - 1


def _cparams(n_axes=1):
    return pltpu.CompilerParams(dimension_semantics=("arbitrary",) * n_axes,
                                vmem_limit_bytes=VMEM_LIMIT)


def _sigmoid(x):
    return 1.0 / (1.0 + jnp.exp(-x))


def _log_sigmoid(x):
    return jnp.minimum(x, 0.0) - jnp.log(1.0 + jnp.exp(-jnp.abs(x)))


def _ada_kernel(c_ref, w_ref, b_ref, o_ref):
    c = c_ref[...]
    ca = (c * _sigmoid(c)).astype(BF16)
    o_ref[...] = jnp.dot(ca, w_ref[...].astype(BF16), preferred_element_type=F32) + b_ref[...]


def _ada(c, w, b):
    B, D = c.shape
    n_out = w.shape[1]
    tn = 1024
    return pl.pallas_call(
        _ada_kernel,
        grid=(n_out // tn,),
        in_specs=[pl.BlockSpec((B, D), lambda j: (0, 0)),
                  pl.BlockSpec((D, tn), lambda j: (0, j)),
                  pl.BlockSpec((1, tn), lambda j: (0, j))],
        out_specs=pl.BlockSpec((B, tn), lambda j: (0, j)),
        out_shape=jax.ShapeDtypeStruct((B, n_out), F32),
        compiler_params=_cparams(),
        name="ada",
    )(c, w, b.reshape(1, n_out))


def _inproj_kernel(x_ref, mod_ref, win_ref, wc_ref, bc_ref, wa_ref, ba_ref, bg_ref,
                   oa_ref, la_ref, g_ref, halo_ref, w_ref, *, tm, tpb, lm):
    i = pl.program_id(0)

    @pl.when(i == 0)
    def _():
        rc = 2 * LANES
        for r in range(0, IN_GATES, rc):
            w_ref[:, r:r + rc] = win_ref[0, r:r + rc, :].T.astype(BF16)
        for r in range(0, C_SMALL - C_GQK, rc):
            w_ref[:, C_GQK + r:C_GQK + r + rc] = win_ref[0, IN_G + r:IN_G + r + rc, :].T.astype(BF16)
        gates = win_ref[0, IN_GATES:IN_G, :]
        z = lambda n: jnp.zeros((n, gates.shape[1]), F32)
        small = jnp.concatenate([gates[0:M_HEADS], z(SM_F - M_HEADS), gates[M_HEADS:2 * M_HEADS],
                                 z(SM_A - SM_F - M_HEADS), win_ref[0, IN_GA:IN_TOT, :],
                                 z(LANES - SM_A - G_RANK)], axis=0)
        w_ref[:, C_SMALL:C_TOT] = small.T.astype(BF16)

    @pl.when(i % tpb == 0)
    def _():
        halo_ref[0:SUBLANES, :] = jnp.zeros((SUBLANES, halo_ref.shape[1]), F32)

    mod = mod_ref[0]
    u = (x_ref[...] * (1.0 + mod[1:2, :]) + mod[0:1, :]).astype(BF16)

    def proj(c0, c1):
        return jnp.dot(u, w_ref[:, c0:c1], preferred_element_type=F32)

    p = proj(C_QK, C_QK + 2 * M_W)
    halo_ref[SUBLANES:SUBLANES + tm, :] = p
    acc = bc_ref[...] + wc_ref[CONV_W - 1:CONV_W, :] * p
    for j in range(CONV_W - 1):
        acc = acc + wc_ref[j:j + 1, :] * halo_ref[pl.ds(SUBLANES - (CONV_W - 1) + j, tm), :]
    halo_ref[0:SUBLANES, :] = p[tm - SUBLANES:, :]
    qk = acc * _sigmoid(acc)
    oa_ref[:, C_QK:C_QK + M_W] = qk[:, :M_W].astype(BF16)
    oa_ref[:, C_QK + M_W:C_QK + 2 * M_W] = (qk[:, M_W:] * (M_HD ** -0.5)).astype(BF16)

    ps = proj(C_SMALL, C_TOT)
    la = jnp.dot(ps.astype(BF16), wa_ref[...], preferred_element_type=F32) + ba_ref[...]
    la_ref[...] = _log_sigmoid(la) * (1.0 / G_TAU)
    pt = ps.T
    gi = pt[SM_I:SM_I + SUBLANES, :] + bg_ref[0:SUBLANES, :]
    gf = _log_sigmoid(pt[SM_F:SM_F + SUBLANES, :] + bg_ref[SUBLANES:2 * SUBLANES, :])
    for j in range(tm // lm):
        g_ref[j, 0:SUBLANES, :] = gi[:, j * lm:(j + 1) * lm]
        g_ref[j, SUBLANES:2 * SUBLANES, :] = gf[:, j * lm:(j + 1) * lm]

    p = proj(C_VO, C_VO + 2 * M_W)
    oa_ref[:, C_VO:C_VO + 2 * M_W] = p.astype(BF16)

    p = proj(C_GQK, C_GQK + G_KW)
    oa_ref[:, C_GQK:C_GQK + G_KW] = (p * (G_DK ** -0.5)).astype(BF16)
    p = proj(C_GQK + G_KW, C_SMALL)
    oa_ref[:, C_GQK + G_KW:C_SMALL] = p.astype(BF16)


def _inproj(x2, mod3, w_in, w_conv, b_conv, wa_pad, b_gla, bg, *, S, tm, lm, layer):
    N, D = x2.shape
    tpb = S // tm
    kern = functools.partial(_inproj_kernel, tm=tm, tpb=tpb, lm=lm)
    return pl.pallas_call(
        kern,
        grid=(N // tm,),
        in_specs=[pl.BlockSpec((tm, D), lambda i: (i, 0)),
                  pl.BlockSpec((1, 6, D), lambda i: (i // tpb, 0, 0)),
                  pl.BlockSpec((1, IN_TOT, D), lambda i: (layer, 0, 0), pipeline_mode=pl.Buffered(1)),
                  pl.BlockSpec((CONV_W, 2 * M_W), lambda i: (0, 0)),
                  pl.BlockSpec((1, 2 * M_W), lambda i: (0, 0)),
                  pl.BlockSpec((LANES, G_KW), lambda i: (0, 0)),
                  pl.BlockSpec((1, G_KW), lambda i: (0, 0)),
                  pl.BlockSpec((2 * SUBLANES, 1), lambda i: (0, 0))],
        out_specs=[pl.BlockSpec((tm, C_SMALL), lambda i: (i, 0)),
                   pl.BlockSpec((tm, G_KW), lambda i: (i, 0)),
                   pl.BlockSpec((tm // lm, 2 * SUBLANES, lm), lambda i: (i, 0, 0))],
        out_shape=[jax.ShapeDtypeStruct((N, C_SMALL), BF16),
                   jax.ShapeDtypeStruct((N, G_KW), F32),
                   jax.ShapeDtypeStruct((N // lm, 2 * SUBLANES, lm), F32)],
        scratch_shapes=[pltpu.VMEM((SUBLANES + tm, 2 * M_W), F32), pltpu.VMEM((D, C_TOT), BF16)],
        compiler_params=_cparams(),
        name="inproj",
    )(x2, mod3, w_in, w_conv, b_conv, wa_pad, b_gla, bg)


def _mlstm_sel():
    sel = np.zeros((2 * LANES, 2 * M_HEADS * M_HD), np.float32)
    for j in range(2 * M_HEADS):
        src = (SUBLANES if j < M_HEADS else 3 * SUBLANES) + j % M_HEADS
        sel[src, M_HD * j:M_HD * (j + 1)] = 1.0
        sel[LANES + src, M_HD * j:M_HD * (j + 1)] = 1.0
    return sel


def _mlstm_kernel(qk_ref, vo_ref, g_ref, u_ref, gain_ref, sel_ref, out_ref, c_ref, zt_ref, a_ref, dec_ref, m_ref,
                  *, L, NC, nb):
    @pl.when(pl.program_id(1) == 0)
    def _():
        c_ref[...] = jnp.zeros_like(c_ref)
        m_ref[...] = jnp.zeros_like(m_ref)

    tril = (lax.broadcasted_iota(jnp.int32, (L, L), 0) >= lax.broadcasted_iota(jnp.int32, (L, L), 1))
    ones_v = jnp.ones((L, M_HD), BF16)
    zpad = jnp.zeros((LANES - 4 * SUBLANES, L), F32)
    zgroup = lax.broadcasted_iota(jnp.int32, (L, LANES), 1) // SUBLANES
    factor_cols = (zgroup == 1) | (zgroup == 3)

    order = [(bi, c) for bi in range(nb) for c in range(NC)]
    f_all = jnp.concatenate([g_ref[bi, c, SUBLANES:2 * SUBLANES, :] for bi, c in order], axis=0)
    i_all = jnp.concatenate([g_ref[bi, c, 0:SUBLANES, :] for bi, c in order], axis=0)
    b_all = jnp.dot(f_all, u_ref[...], preferred_element_type=F32, precision=HIGHEST)
    a_all = i_all - b_all
    lane_all = lax.broadcasted_iota(jnp.int32, a_all.shape, 1)
    g_all = a_all
    s = 1
    while s < L:
        g_all = jnp.maximum(g_all, jnp.where(lane_all >= s, pltpu.roll(g_all, s, 1), -jnp.inf))
        s *= 2
    for bi in range(nb):
        m_prev = m_ref[bi][:, 0:1]
        for c in range(NC):
            ci = bi * NC + c
            r8 = slice(SUBLANES * ci, SUBLANES * (ci + 1))
            a, b = a_all[r8], b_all[r8]
            a_ref[ci] = a
            M = jnp.maximum(g_all[r8], m_prev)
            ML = M[:, L - 1:L]
            Z = jnp.concatenate([M, jnp.exp(m_prev - M), jnp.exp(-(b + M)), jnp.exp(a - ML), zpad],
                                axis=0)
            zt_ref[ci] = Z.T
            dec_ref[ci] = jnp.broadcast_to(jnp.exp(m_prev - ML), (SUBLANES, 2 * M_HD))
            m_prev = b[:, L - 1:L] + ML
        m_ref[bi] = jnp.broadcast_to(m_prev, (SUBLANES, LANES))

    chains = [(bi, h) for bi in range(nb) for h in range(M_HEADS)]
    nt = (((1,), (1,)), ((), ()))
    tn = (((0,), (0,)), ((), ()))

    def chunk(c, carry):
        rows = pl.ds(pl.multiple_of(c * L, L), L)
        Zt = [zt_ref[bi * NC + c] for bi in range(nb)]
        a = [a_ref[bi * NC + c] for bi in range(nb)]
        dec = [dec_ref[bi * NC + c] for bi in range(nb)]
        hs = [slice(h * M_HD, (h + 1) * M_HD) for h in range(M_HEADS)]
        hs2 = [slice(M_W + h * M_HD, M_W + (h + 1) * M_HD) for h in range(M_HEADS)]
        q = [qk_ref[bi, rows, hs[h]] for bi, h in chains]
        k = [qk_ref[bi, rows, hs2[h]] for bi, h in chains]
        vext = [jnp.concatenate([vo_ref[bi, rows, hs[h]], ones_v], axis=1) for bi, h in chains]
        cst = [c_ref[bi * M_HEADS + h] for bi, h in chains]
        n = range(len(chains))
        sc = [lax.dot_general(q[i], k[i], nt, preferred_element_type=F32) for i in n]
        qc = [jnp.dot(q[i], cst[i].astype(BF16), preferred_element_type=F32) for i in n]
        pm = [(sc[i] * jnp.exp(jnp.where(tril, a[bi][h:h + 1, :] - Zt[bi][:, h:h + 1], -jnp.inf))).astype(BF16)
              for i, (bi, h) in enumerate(chains)]
        pv = [jnp.dot(pm[i], vext[i], preferred_element_type=F32) for i in n]
        rep = []
        for bi in range(nb):
            zf = jnp.where(factor_cols, Zt[bi], 0.0)
            zh = zf.astype(BF16)
            zl = (zf - zh.astype(F32)).astype(BF16)
            rep.append(jnp.dot(jnp.concatenate([zh, zl], axis=1), sel_ref[...], preferred_element_type=F32))
        e_inter = [rep[bi][:, M_HD * h:M_HD * (h + 1)] for bi, h in chains]
        w_state = [rep[bi][:, M_HD * (M_HEADS + h):M_HD * (M_HEADS + h + 1)] for bi, h in chains]
        kw = [(w_state[i] * k[i].astype(F32)).astype(BF16) for i in n]
        upd = [lax.dot_general(kw[i], vext[i], tn, preferred_element_type=F32) for i in n]
        for i, (bi, h) in enumerate(chains):
            c_ref[bi * M_HEADS + h] = dec[bi][h:h + 1, :] * cst[i] + upd[i]
            nd = pv[i] + jnp.concatenate([e_inter[i], e_inter[i]], axis=1) * qc[i]
            hh = nd[:, :M_HD] / jnp.maximum(jnp.abs(nd[:, M_HD:]),
                                            Zt[bi][:, 2 * SUBLANES + h:2 * SUBLANES + h + 1])
            hh = _sigmoid(vo_ref[bi, rows, hs2[h]].astype(F32)) * hh
            hn = hh * lax.rsqrt(jnp.mean(hh * hh, axis=-1, keepdims=True) + LN_EPS)
            out_ref[bi, rows, hs[h]] = (hn * gain_ref[:, hs[h]]).astype(BF16)
        return carry

    lax.fori_loop(0, NC, chunk, 0, unroll=True)


def _mlstm(oa, g3, u_tri, gain, *, B, S, L, nb, ts):
    N = oa.shape[0]
    NC = ts // L
    oa3 = oa.reshape(B, S, oa.shape[1])
    g4 = g3.reshape(B, S // L, 2 * SUBLANES, L)
    sel = jnp.asarray(_mlstm_sel(), BF16)
    kern = functools.partial(_mlstm_kernel, L=L, NC=NC, nb=nb)
    out = pl.pallas_call(
        kern,
        grid=(B // nb, S // ts),
        in_specs=[pl.BlockSpec((nb, ts, 2 * M_W), lambda b, t: (b, t, C_QK // (2 * M_W))),
                  pl.BlockSpec((nb, ts, 2 * M_W), lambda b, t: (b, t, C_VO // (2 * M_W))),
                  pl.BlockSpec((nb, NC, 2 * SUBLANES, L), lambda b, t: (b, t, 0, 0)),
                  pl.BlockSpec((L, L), lambda b, t: (0, 0)),
                  pl.BlockSpec((1, M_W), lambda b, t: (0, 0)),
                  pl.BlockSpec(sel.shape, lambda b, t: (0, 0))],
        out_specs=pl.BlockSpec((nb, ts, M_W), lambda b, t: (b, t, 0)),
        out_shape=jax.ShapeDtypeStruct((B, S, M_W), BF16),
        scratch_shapes=[pltpu.VMEM((nb * M_HEADS, M_HD, 2 * M_HD), F32),
                        pltpu.VMEM((nb * NC, L, LANES), F32),
                        pltpu.VMEM((nb * NC, SUBLANES, L), F32),
                        pltpu.VMEM((nb * NC, SUBLANES, 2 * M_HD), F32),
                        pltpu.VMEM((nb, SUBLANES, LANES), F32)],
        compiler_params=_cparams(2),
        name="mlstm",
    )(oa3, oa3, g4, u_tri, gain, sel)
    return out.reshape(N, M_W)


_G_LEVELS = 6
_G_XROW = 2 * G_CHUNK + SUBLANES


def _gla_consts():
    L = G_CHUNK
    t = np.arange(L)
    blocks = [(t[None, :] <= t[:, None]).astype(np.float32),
              (t[None, :] > t[:, None]).astype(np.float32),
              np.ones((SUBLANES, L), np.float32)]
    masks = [np.eye(L, dtype=np.float32)]
    m = 1
    while m < L:
        wl = np.zeros((L, L), np.float32)
        for r in range(L):
            r0 = (r // (2 * m)) * 2 * m + m
            if r % (2 * m) >= m:
                wl[r, r0:r + 1] = 1.0
            else:
                wl[r, r + 1:r0] = 1.0
        blocks.append(wl)
        tt, ss = t[:, None], t[None, :]
        masks.append(((tt // (2 * m) == ss // (2 * m)) & (tt % (2 * m) >= m)
                      & (ss % (2 * m) < m)).astype(np.float32))
        m *= 2
    w = np.concatenate(blocks, axis=0)
    w3 = np.concatenate([w, w, w], axis=1)
    mk = np.stack([np.concatenate([x] * G_HEADS, axis=0) for x in masks])
    return w3, mk


def _gla_kernel(qk_ref, v_ref, gg_ref, la_ref, w3_ref, mk_ref, gain_ref, out_ref, st_ref, *, NC, nb):
    L = G_CHUNK

    @pl.when(pl.program_id(1) == 0)
    def _():
        st_ref[...] = jnp.zeros_like(st_ref)

    lane_head = lax.broadcasted_iota(jnp.int32, (L, G_KW), 1) // G_DK
    br = lax.broadcasted_iota(jnp.int32, (2 * G_DV, LANES), 0) < G_DV
    bl = lax.broadcasted_iota(jnp.int32, (2 * G_DV, LANES), 1) < G_DK
    bmask = br == bl
    nt = (((1,), (1,)), ((), ()))
    tn = (((0,), (0,)), ((), ()))

    def chunk(c, carry):
        rows = pl.ds(pl.multiple_of(c * L, L), L)
        X, q, k = [], [], []
        for bi in range(nb):
            la = la_ref[bi, rows, :]
            hi = la.astype(BF16)
            r1 = la - hi.astype(F32)
            mid = r1.astype(BF16)
            lo = (r1 - mid.astype(F32)).astype(BF16)
            stk = jnp.concatenate([hi, mid, lo], axis=0)
            X.append(jnp.exp(jnp.dot(w3_ref[...], stk, preferred_element_type=F32)))
            q.append(qk_ref[bi, rows, 0:G_KW].astype(F32))
            k.append(qk_ref[bi, rows, G_KW:2 * G_KW].astype(F32))

        sc = [[None] * (_G_LEVELS + 1) for _ in range(nb)]
        for lev in range(_G_LEVELS + 1):
            for bi in range(nb):
                if lev == 0:
                    qt, kt = q[bi], k[bi]
                else:
                    xl = X[bi][_G_XROW + L * (lev - 1):_G_XROW + L * lev, :]
                    qt, kt = q[bi] * xl, k[bi] * xl
                q4 = jnp.concatenate([jnp.where(lane_head == h, qt, 0.0) for h in range(G_HEADS)],
                                     axis=0).astype(BF16)
                sc[bi][lev] = lax.dot_general(q4, kt.astype(BF16), nt, preferred_element_type=F32)
        Ab = []
        for bi in range(nb):
            A = sc[bi][0] * mk_ref[0]
            for lev in range(1, _G_LEVELS + 1):
                A = A + sc[bi][lev] * mk_ref[lev]
            Ab.append(A.astype(BF16))

        for bi in range(nb):
            gg = gg_ref[bi, rows, :].astype(F32)
            gate = gg * _sigmoid(gg)
            for p in range(2):
                ls = slice(LANES * p, LANES * (p + 1))
                vp = v_ref[bi, rows, 2 * G_DV * p:2 * G_DV * (p + 1)]
                oi = [jnp.dot(Ab[bi][L * (2 * p + hh):L * (2 * p + hh + 1)],
                              vp[:, G_DV * hh:G_DV * (hh + 1)], preferred_element_type=F32)
                      for hh in range(2)]
                st = st_ref[bi, p]
                qc = (q[bi][:, ls] * X[bi][0:L, ls]).astype(BF16)
                o_inter = lax.dot_general(qc, st.astype(BF16), nt, preferred_element_type=F32)
                kc = (k[bi][:, ls] * X[bi][L:2 * L, ls]).astype(BF16)
                upd = lax.dot_general(vp, kc, tn, preferred_element_type=F32)
                dec = X[bi][2 * L:2 * L + 1, ls]
                st_ref[bi, p] = jnp.where(bmask, dec * st + upd, 0.0)
                for hh in range(2):
                    o = o_inter[:, G_DV * hh:G_DV * (hh + 1)] + oi[hh]
                    hn = o * lax.rsqrt(jnp.mean(o * o, axis=-1, keepdims=True) + LN_EPS)
                    hs = slice(G_DV * (2 * p + hh), G_DV * (2 * p + hh + 1))
                    out_ref[bi, rows, hs] = (hn * gain_ref[:, hs] * gate[:, hs]).astype(BF16)
        return carry

    lax.fori_loop(0, NC, chunk, 0)


def _gla(oa, la, w3, mk, gain, *, B, S, nb, ts):
    N = oa.shape[0]
    oa3 = oa.reshape(B, S, oa.shape[1])
    la3 = la.reshape(B, S, G_KW)
    kern = functools.partial(_gla_kernel, NC=ts // G_CHUNK, nb=nb)
    out = pl.pallas_call(
        kern,
        grid=(B // nb, S // ts),
        in_specs=[pl.BlockSpec((nb, ts, 2 * G_KW), lambda b, t: (b, t, C_GQK // (2 * G_KW))),
                  pl.BlockSpec((nb, ts, G_W), lambda b, t: (b, t, C_GV // G_W)),
                  pl.BlockSpec((nb, ts, G_W), lambda b, t: (b, t, C_GG // G_W)),
                  pl.BlockSpec((nb, ts, G_KW), lambda b, t: (b, t, 0)),
                  pl.BlockSpec(w3.shape, lambda b, t: (0, 0)),
                  pl.BlockSpec(mk.shape, lambda b, t: (0, 0, 0)),
                  pl.BlockSpec((1, G_W), lambda b, t: (0, 0))],
        out_specs=pl.BlockSpec((nb, ts, G_W), lambda b, t: (b, t, 0)),
        out_shape=jax.ShapeDtypeStruct((B, S, G_W), BF16),
        scratch_shapes=[pltpu.VMEM((nb, 2, 2 * G_DV, LANES), F32)],
        compiler_params=_cparams(2),
        name="gla",
    )(oa3, oa3, oa3, la3, w3, mk, gain)
    return out.reshape(N, G_W)


def _layer_norm(z, g, b):
    mu = jnp.mean(z, axis=-1, keepdims=True)
    zc = z - mu
    var = jnp.mean(zc * zc, axis=-1, keepdims=True)
    return zc * lax.rsqrt(var + LN_EPS) * g + b


def _outproj_kernel(hm_ref, hg_ref, wf_ref, x_ref, mod_ref, g_ref, b_ref, wrg_ref, wre_ref, br_ref,
                    x1_ref, u2_ref, rrow_ref, w_ref, wr_ref, *, tb, nh):
    @pl.when(pl.program_id(0) == 0)
    def _():
        w_ref[...] = wf_ref[...].astype(BF16)
        z = lambda n: jnp.zeros((n, wrg_ref.shape[2]), F32)
        wt = jnp.concatenate([wrg_ref[0], z(SUBLANES - N_GROUPS), wre_ref[0],
                              z(LANES - SUBLANES - N_EXP)], axis=0).T
        hi = wt.astype(BF16)
        wr_ref[:, 0:LANES] = hi
        wr_ref[:, LANES:2 * LANES] = (wt - hi.astype(F32)).astype(BF16)

    mod = mod_ref[0]
    blocks = [slice(tb * j, tb * (j + 1)) for j in range(nh)]
    y = [jnp.dot(hm_ref[r, :], w_ref[0:M_W, :], preferred_element_type=F32)
         + jnp.dot(hg_ref[r, :], w_ref[M_W:M_W + G_W, :], preferred_element_type=F32) for r in blocks]
    u2 = []
    for j, r in enumerate(blocks):
        z = ALPHA * x_ref[r, :] + (1.0 + mod[2:3, :]) * y[j]
        x1 = _layer_norm(z, g_ref[...], b_ref[...])
        x1_ref[r, :] = x1
        u2.append(x1 * (1.0 + mod[4:5, :]) + mod[3:4, :])
        u2_ref[r, :] = u2[j].astype(BF16)

    u2h = [u.astype(BF16) for u in u2]
    u2l = [(u2[j] - u2h[j].astype(F32)).astype(BF16) for j in range(nh)]
    lh = [jnp.dot(u, wr_ref[...], preferred_element_type=F32) for u in u2h]
    ll = [jnp.dot(u, wr_ref[:, 0:LANES], preferred_element_type=F32) for u in u2l]
    for j in range(nh):
        logits = lh[j][:, 0:LANES] + lh[j][:, LANES:2 * LANES] + ll[j] + br_ref[...]
        rrow_ref[j] = _route_select(logits.T, tb)


def _route_select(lt, tm):
    row = lax.broadcasted_iota(jnp.int32, (SUBLANES, tm), 0)
    gl = jnp.where(row < N_GROUPS, lt[0:SUBLANES, :], -jnp.inf)
    gmax = jnp.max(gl, axis=0, keepdims=True)
    gsel = jnp.min(jnp.where(gl == gmax, row, SUBLANES), axis=0, keepdims=True)
    pg = 1.0 / jnp.sum(jnp.exp(gl - gmax), axis=0, keepdims=True)
    ein = jnp.zeros((SUBLANES, tm), F32)
    for g in range(N_GROUPS):
        ein = jnp.where(gsel == g, lt[SUBLANES * (g + 1):SUBLANES * (g + 2), :], ein)
    v1 = jnp.max(ein, axis=0, keepdims=True)
    i1 = jnp.min(jnp.where(ein == v1, row, SUBLANES), axis=0, keepdims=True)
    rest = jnp.where(row == i1, -jnp.inf, ein)
    v2 = jnp.max(rest, axis=0, keepdims=True)
    i2 = jnp.min(jnp.where(rest == v2, row, SUBLANES), axis=0, keepdims=True)
    t2 = jnp.exp(v2 - v1)
    p1 = 1.0 / (1.0 + t2)
    e0 = (gsel * E_PER_G + i1).astype(F32)
    e1 = (gsel * E_PER_G + i2).astype(F32)
    return jnp.concatenate([e0, e1, pg * p1, pg * (t2 * p1), jnp.zeros((SUBLANES - 4, tm), F32)], axis=0)


def _outproj(hm, hg, w_out, x2, mod3, g, b, wrg_t, wre_t, br, *, S, tb, nh, layer):
    N, D = x2.shape
    tm = tb * nh
    tpb = S // tm
    kern = functools.partial(_outproj_kernel, tb=tb, nh=nh)
    return pl.pallas_call(
        kern,
        grid=(N // tm,),
        in_specs=[pl.BlockSpec((tm, M_W), lambda i: (i, 0)),
                  pl.BlockSpec((tm, G_W), lambda i: (i, 0)),
                  pl.BlockSpec((M_W + G_W, D), lambda i: (0, 0), pipeline_mode=pl.Buffered(1)),
                  pl.BlockSpec((tm, D), lambda i: (i, 0)),
                  pl.BlockSpec((1, 6, D), lambda i: (i // tpb, 0, 0)),
                  pl.BlockSpec((1, D), lambda i: (0, 0)),
                  pl.BlockSpec((1, D), lambda i: (0, 0)),
                  pl.BlockSpec((1, N_GROUPS, D), lambda i: (layer, 0, 0)),
                  pl.BlockSpec((1, N_EXP, D), lambda i: (layer, 0, 0)),
                  pl.BlockSpec((1, LANES), lambda i: (0, 0))],
        out_specs=[pl.BlockSpec((tm, D), lambda i: (i, 0)),
                   pl.BlockSpec((tm, D), lambda i: (i, 0)),
                   pl.BlockSpec((nh, SUBLANES, tb), lambda i: (i, 0, 0))],
        out_shape=[jax.ShapeDtypeStruct((N, D), F32),
                   jax.ShapeDtypeStruct((N, D), BF16),
                   jax.ShapeDtypeStruct((N // tb, SUBLANES, tb), F32)],
        scratch_shapes=[pltpu.VMEM((M_W + G_W, D), BF16), pltpu.VMEM((D, 2 * LANES), BF16)],
        compiler_params=_cparams(),
        name="outproj",
    )(hm, hg, w_out, x2, mod3, g, b, wrg_t, wre_t, br)


def _slots_per_tile(tb):
    worst = 2 * tb + N_EXP * (GRAN - 1)
    return -(-worst // LANES) * LANES


def _ffn_tiles(n_tok, tb):
    worst_rows = 2 * n_tok + (n_tok // tb) * N_EXP * (GRAN - 1)
    return -(-worst_rows // FFN_TM) + N_EXP


def _route_kernel(rr_ref, u_ref, lt_ref, srow_ref, col_ref, gd_ref, meta_ref, mg_ref, part_ref,
                  *, NT, tb, TM):
    iota_e = lax.broadcasted_iota(jnp.int32, (N_EXP, tb), 0).astype(F32)
    glane = lax.broadcasted_iota(jnp.int32, (N_EXP, LANES), 1).astype(F32)
    ltri = lt_ref[...]

    def prefix_e(col):
        return jnp.dot(ltri, jnp.broadcast_to(col, (N_EXP, LANES)),
                       preferred_element_type=F32, precision=HIGHEST)[:, 0:1]

    def p1(j, run8):
        r = rr_ref[j]
        oh0 = jnp.where(iota_e == r[0:1, :], 1.0, 0.0)
        oh1 = jnp.where(iota_e == r[1:2, :], 1.0, 0.0)
        cum0 = jnp.dot(oh0.astype(BF16), u_ref[...], preferred_element_type=F32)
        cum1 = jnp.dot(oh1.astype(BF16), u_ref[...], preferred_element_type=F32)
        c0 = jnp.sum(oh0, axis=1, keepdims=True)
        n8 = jnp.floor((c0 + jnp.sum(oh1, axis=1, keepdims=True) + (GRAN - 1.0)) * (1.0 / GRAN))
        lo8 = prefix_e(n8)
        s0 = jnp.sum(oh0 * (GRAN * lo8 + cum0 - 1.0), axis=0, keepdims=True)
        s1 = jnp.sum(oh1 * (GRAN * lo8 + c0 + cum1 - 1.0), axis=0, keepdims=True)
        info = jnp.concatenate([s0, s1, r[2:4, :], jnp.zeros((SUBLANES - 4, tb), F32)], axis=0)
        srow_ref[j] = info
        col_ref[pl.ds(pl.multiple_of(j * tb, tb), tb), :] = jnp.concatenate(
            [info, jnp.zeros((LANES - SUBLANES, tb), F32)], axis=0).T
        mg = jnp.where((lo8 <= glane) & (glane < lo8 + n8), 1.0, 0.0)
        mg_ref[j] = mg
        part = jnp.sum(mg * (run8 + glane - lo8), axis=0, keepdims=True)
        gcnt = jnp.broadcast_to(jnp.sum(n8, axis=0, keepdims=True), (1, LANES))
        part_ref[j] = jnp.concatenate([part, gcnt, jnp.zeros((SUBLANES - 2, LANES), F32)], axis=0)
        return run8 + n8

    tot8 = lax.fori_loop(0, NT, p1, jnp.zeros((N_EXP, 1), F32), unroll=8 if NT % 8 == 0 else 1)
    seg_t = jnp.floor((tot8 * GRAN + (TM - 1.0)) * (1.0 / TM))
    base_t = prefix_e(seg_t)
    base8 = base_t * (TM // GRAN)
    lane1 = lax.broadcasted_iota(jnp.int32, (1, LANES), 1)

    def p2(j, carry):
        pr = part_ref[j]
        dst = (pr[0:1, :] + jnp.sum(mg_ref[j] * base8, axis=0, keepdims=True)) * GRAN
        gd_ref[j] = jnp.where(lane1 == G_LAST, pr[1:2, :], dst).astype(jnp.int32)
        return carry

    lax.fori_loop(0, NT, p2, 0, unroll=8 if NT % 8 == 0 else 1)
    eye = jnp.where(glane == lax.broadcasted_iota(jnp.int32, (N_EXP, LANES), 0).astype(F32), 1.0, 0.0)
    tail_row = jnp.sum(eye * ((base8 + tot8) * GRAN), axis=0, keepdims=True)
    tail_n8 = jnp.sum(eye * (seg_t * (TM // GRAN) - tot8), axis=0, keepdims=True)
    nv_l = jnp.broadcast_to(jnp.sum(seg_t, axis=0, keepdims=True), (1, LANES))
    gd_ref[NT] = jnp.where(lane1 == G_LAST, nv_l, tail_row).astype(jnp.int32)
    gd_ref[NT + 1] = tail_n8.astype(jnp.int32)
    ti = lax.broadcasted_iota(jnp.int32, (N_EXP, tb), 1).astype(F32)
    te = jnp.sum(jnp.where(base_t <= ti, 1.0, 0.0), axis=0, keepdims=True) - 1.0
    nv = jnp.broadcast_to(jnp.sum(seg_t, axis=0, keepdims=True), (1, tb))
    own = jnp.where((base_t <= ti) & (ti < base_t + seg_t), 1.0, 0.0)
    vr = jnp.sum(own * jnp.clip(tot8 * GRAN - (ti - base_t) * TM, 0.0, TM), axis=0, keepdims=True)
    meta_ref[...] = jnp.concatenate([te, nv, vr, jnp.zeros((SUBLANES - 3, tb), F32)],
                                    axis=0).astype(jnp.int32)


def _route(rrow, u_cnt, ltri, *, TM):
    NT, _, tb = rrow.shape
    kern = functools.partial(_route_kernel, NT=NT, tb=tb, TM=TM)
    full3 = lambda i: (0, 0, 0)
    return pl.pallas_call(
        kern,
        grid=(1,),
        in_specs=[pl.BlockSpec((NT, SUBLANES, tb), full3),
                  pl.BlockSpec((tb, tb), lambda i: (0, 0)),
                  pl.BlockSpec((N_EXP, N_EXP), lambda i: (0, 0))],
        out_specs=[pl.BlockSpec((NT, SUBLANES, tb), full3),
                   pl.BlockSpec((NT * tb, LANES), lambda i: (0, 0)),
                   pl.BlockSpec((NT + 2, 1, LANES), full3),
                   pl.BlockSpec((SUBLANES, tb), lambda i: (0, 0))],
        out_shape=[jax.ShapeDtypeStruct((NT, SUBLANES, tb), F32),
                   jax.ShapeDtypeStruct((NT * tb, LANES), F32),
                   jax.ShapeDtypeStruct((NT + 2, 1, LANES), jnp.int32),
                   jax.ShapeDtypeStruct((SUBLANES, tb), jnp.int32)],
        scratch_shapes=[pltpu.VMEM((NT, N_EXP, LANES), F32), pltpu.VMEM((NT, SUBLANES, LANES), F32)],
        compiler_params=_cparams(),
        name="route",
    )(rrow, u_cnt, ltri)


_HI_MASK = 0xFFFF0000


def _pack_halves(x):
    c = x.shape[1] // 2
    lo = lax.bitcast_convert_type(x[:, :c], U32)
    hi = lax.bitcast_convert_type(x[:, c:], U32)
    return (lo >> 16) | (hi & U32(_HI_MASK))


def _unpack_halves(w):
    lo = lax.bitcast_convert_type(w << 16, F32)
    hi = lax.bitcast_convert_type(w & U32(_HI_MASK), F32)
    return jnp.concatenate([lo, hi], axis=1).astype(BF16)


def _granule_copy(src_ref, src_row, dst_ref, dst_row, sem, n=1):
    cols = pl.ds(0, min(src_ref.shape[-1], dst_ref.shape[-1]))
    return pltpu.make_async_copy(src_ref.at[pl.ds(src_row, n * GRAN), cols],
                                 dst_ref.at[pl.ds(dst_row, n * GRAN), cols], sem)


def _for_granules(n, body, unroll=4):
    def blk(i, carry):
        for t in range(unroll):
            body(i * unroll + t)
        return carry

    def one(g, carry):
        body(g)
        return carry

    nblk = n // unroll
    lax.fori_loop(0, nblk, blk, 0)
    lax.fori_loop(nblk * unroll, n, one, 0)


def _wait_granules(n, src_ref, dst_ref, sem, n_max):
    b = 1
    while b <= n_max:
        @pl.when((n & b) != 0)
        def _(b=b):
            _granule_copy(src_ref, 0, dst_ref, 0, sem, n=b).wait()
        b *= 2


def _dispatch_kernel(gd_ref, srow_ref, u_ref, xs_ref, buf, zbuf, sems, *, NT, SL, TM, n_tiles):
    j = pl.program_id(0)
    slot = j % 2
    zsem = sems.at[2]

    def drain(tile, sl):
        _wait_granules(gd_ref[tile, G_LAST], buf.at[sl], xs_ref, sems.at[sl], SL // GRAN)

    def tile_fill(t):
        return pltpu.make_async_copy(zbuf, xs_ref.at[pl.ds(pl.multiple_of(t * TM, TM), TM), :], zsem)

    def zero_fill(wait):
        for e in range(N_EXP):
            n, row0 = gd_ref[NT + 1, e], gd_ref[NT, e]
            b = TM // GRAN // 2
            while b >= 1:
                @pl.when((n & b) != 0)
                def _(b=b, n=n, row0=row0):
                    start = pl.multiple_of(row0 + ((n >> b.bit_length()) << b.bit_length()) * GRAN, GRAN)
                    cp = pltpu.make_async_copy(zbuf.at[pl.ds(0, b * GRAN), :],
                                               xs_ref.at[pl.ds(start, b * GRAN), :], zsem)
                    cp.wait() if wait else cp.start()
                b //= 2

        def zt(t, carry):
            tile_fill(t).wait() if wait else tile_fill(t).start()
            return carry
        lax.fori_loop(gd_ref[NT, G_LAST], n_tiles, zt, 0)

    @pl.when(j == 0)
    def _():
        zbuf[...] = jnp.zeros_like(zbuf)
        zero_fill(False)

    @pl.when(j >= 2)
    def _():
        drain(j - 2, slot)

    s = srow_ref[0]
    rows = lax.broadcasted_iota(jnp.int32, (SL, s.shape[1]), 0).astype(F32)
    m0 = rows == s[0:1, :]
    m1 = rows == s[1:2, :]
    oh = jnp.where(m0 | m1, 1.0, 0.0).astype(BF16)
    dw = u_ref.shape[1] // 2
    buf[slot, :, 0:dw] = _pack_halves(jnp.dot(oh, u_ref[...], preferred_element_type=F32))
    wrow = jnp.sum(jnp.where(m0, s[2:3, :], 0.0) + jnp.where(m1, s[3:4, :], 0.0), axis=1, keepdims=True)
    buf[slot, :, dw:dw + LANES] = lax.bitcast_convert_type(jnp.broadcast_to(wrow, (SL, LANES)), U32)

    def issue(g):
        _granule_copy(buf.at[slot], pl.multiple_of(g * GRAN, GRAN), xs_ref,
                      pl.multiple_of(gd_ref[j, g], GRAN), sems.at[slot]).start()

    _for_granules(gd_ref[j, G_LAST], issue)

    @pl.when(j == NT - 1)
    def _():
        drain(j, slot)
        if NT > 1:
            drain(j - 1, 1 - slot)
        zero_fill(True)


def _dispatch(gd, srow, u2, *, n_tiles, TM):
    N, D = u2.shape
    NT, _, tb = srow.shape
    SL = _slots_per_tile(tb)
    n_rows = n_tiles * TM
    kern = functools.partial(_dispatch_kernel, NT=NT, SL=SL, TM=TM, n_tiles=n_tiles)
    grid_spec = pltpu.PrefetchScalarGridSpec(
        num_scalar_prefetch=1,
        grid=(NT,),
        in_specs=[pl.BlockSpec((1, SUBLANES, tb), lambda j, gd: (j, 0, 0)),
                  pl.BlockSpec((tb, D), lambda j, gd: (j, 0))],
        out_specs=pl.BlockSpec(memory_space=pl.ANY),
        scratch_shapes=[pltpu.VMEM((2, SL, D // 2 + LANES), U32), pltpu.VMEM((TM, D // 2 + LANES), U32),
                        pltpu.SemaphoreType.DMA((3,))],
    )
    return pl.pallas_call(
        kern,
        grid_spec=grid_spec,
        out_shape=jax.ShapeDtypeStruct((n_rows, D // 2 + LANES), U32),
        compiler_params=_cparams(),
        name="dispatch",
    )(gd, srow, u2)


def _ffn_kernel(te_ref, nv_ref, vr_ref, xs_ref, wg_ref, wu_ref, wd_ref, o_ref, wgb, wub, wdb, sg, su, sd, slot_ref,
                sems):
    i = pl.program_id(0)
    nv = nv_ref[0]
    e = te_ref[i]

    def weight_copies(ex, sl):
        return (pltpu.make_async_copy(wg_ref.at[ex], sg.at[sl], sems.at[sl]),
                pltpu.make_async_copy(wu_ref.at[ex], su.at[sl], sems.at[sl]),
                pltpu.make_async_copy(wd_ref.at[ex], sd.at[sl], sems.at[sl]))

    @pl.when(i == 0)
    def _():
        slot_ref[0] = 0
        for cp in weight_copies(e, 0):
            cp.start()

    new_expert = (i < nv) & ((i == 0) | (e != te_ref[jnp.maximum(i - 1, 0)]))

    @pl.when(new_expert)
    def _():
        sl = slot_ref[0]
        for cp in weight_copies(e, sl):
            cp.wait()
        nxt = lax.while_loop(lambda t: (t < nv) & (te_ref[jnp.minimum(t, nv - 1)] == e), lambda t: t + 1, i + 1)

        @pl.when(nxt < nv)
        def _():
            for cp in weight_copies(te_ref[nxt], 1 - sl):
                cp.start()

    hm = xs_ref.shape[0] // FFN_SUB
    dw = xs_ref.shape[1] - LANES

    def swiglu_rows(nsub, cast):
        if cast:
            sl = slot_ref[0]
            wgb[...] = sg[sl].astype(BF16)
            wub[...] = su[sl].astype(BF16)
            wdb[...] = sd[sl].astype(BF16)
            slot_ref[0] = 1 - sl
        halves = tuple(slice(hm * j, hm * (j + 1)) for j in range(nsub))
        x = [_unpack_halves(xs_ref[r, 0:dw]) for r in halves]
        g = [jnp.dot(x[j], wgb[...], preferred_element_type=F32) for j in range(nsub)]
        u = [jnp.dot(x[j], wub[...], preferred_element_type=F32) for j in range(nsub)]
        h = [(g[j] * _sigmoid(g[j]) * u[j]).astype(BF16) for j in range(nsub)]
        y = [jnp.dot(h[j], wdb[...], preferred_element_type=F32) for j in range(nsub)]
        for j in range(nsub):
            wt = lax.bitcast_convert_type(xs_ref[halves[j], dw:dw + LANES], F32)
            yw = y[j] * jnp.concatenate([wt] * (2 * dw // LANES), axis=1)
            o_ref[halves[j], 0:dw] = _pack_halves(yw.astype(BF16).astype(F32))
            o_ref[halves[j], dw:dw + LANES] = xs_ref[halves[j], dw:dw + LANES]
        if nsub < FFN_SUB:
            o_ref[hm * nsub:, :] = jnp.zeros((hm * (FFN_SUB - nsub), dw + LANES), U32)

    used = vr_ref[i]
    for nsub in range(1, FFN_SUB + 1):
        lo, hi = hm * (nsub - 1), hm * nsub
        rows_here = (i < nv) & (used > lo) & ((used <= hi) if nsub < FFN_SUB else True)
        pl.when(rows_here & new_expert)(functools.partial(swiglu_rows, nsub, True))
        pl.when(rows_here & jnp.logical_not(new_expert))(functools.partial(swiglu_rows, nsub, False))


def _ffn(te, nv, vr, xs, wg, wu, wd, *, TM):
    P, XW = xs.shape
    DW = XW - LANES
    D = 2 * DW
    n_tiles = P // TM
    used_tile = lambda i, te, nv, vr: (jnp.maximum(jnp.minimum(i, nv[0] - 1), 0), 0)
    grid_spec = pltpu.PrefetchScalarGridSpec(
        num_scalar_prefetch=3,
        grid=(n_tiles,),
        in_specs=[pl.BlockSpec((TM, XW), used_tile),
                  pl.BlockSpec(memory_space=pl.ANY),
                  pl.BlockSpec(memory_space=pl.ANY),
                  pl.BlockSpec(memory_space=pl.ANY)],
        out_specs=pl.BlockSpec((TM, XW), used_tile),
        scratch_shapes=[pltpu.VMEM((D, D_EXP), BF16), pltpu.VMEM((D, D_EXP), BF16),
                        pltpu.VMEM((D_EXP, D), BF16),
                        pltpu.VMEM((2, D, D_EXP), F32), pltpu.VMEM((2, D, D_EXP), F32),
                        pltpu.VMEM((2, D_EXP, D), F32), pltpu.SMEM((1,), jnp.int32),
                        pltpu.SemaphoreType.DMA((2,))],
    )
    return pl.pallas_call(
        _ffn_kernel,
        grid_spec=grid_spec,
        out_shape=jax.ShapeDtypeStruct((P, XW), U32),
        input_output_aliases={3: 0},
        compiler_params=_cparams(),
        name="ffn",
    )(te, nv, vr, xs, wg, wu, wd)


def _combine_kernel(gd_ref, ys_ref, col_ref, x1_ref, mod_ref, g_ref, b_ref, o_ref, buf, sems, *, NT, SL):
    j = pl.program_id(0)
    slot = j % 2

    def fetch(tile, sl):
        def f(g):
            _granule_copy(ys_ref, pl.multiple_of(gd_ref[tile, g], GRAN), buf.at[sl],
                          pl.multiple_of(g * GRAN, GRAN), sems.at[sl]).start()
        _for_granules(gd_ref[tile, G_LAST], f)

    @pl.when(j == 0)
    def _():
        fetch(0, 0)

    @pl.when(j + 1 < NT)
    def _():
        fetch(j + 1, 1 - slot)

    ng = gd_ref[j, G_LAST]

    _wait_granules(ng, ys_ref, buf.at[slot], sems.at[slot], SL // GRAN)

    rows = lax.broadcasted_iota(jnp.int32, (SL, 1), 0)
    yb = _unpack_halves(jnp.where(rows < ng * GRAN, buf[slot], U32(0)))
    col = col_ref[...]
    tb = col.shape[0]
    lanes = lax.broadcasted_iota(jnp.int32, (tb, SL), 1).astype(F32)
    sel = jnp.where((lanes == col[:, 0:1]) | (lanes == col[:, 1:2]), 1.0, 0.0).astype(BF16)
    y = jnp.dot(sel, yb, preferred_element_type=F32)
    mod = mod_ref[0]
    z = ALPHA * x1_ref[...] + (1.0 + mod[5:6, :]) * y
    o_ref[...] = _layer_norm(z, g_ref[...], b_ref[...])


def _combine(gd, ys, col, x1, mod3, g, b, *, S, tb):
    N, D = x1.shape
    NT = N // tb
    tpb = S // tb
    SL = _slots_per_tile(tb)
    kern = functools.partial(_combine_kernel, NT=NT, SL=SL)
    grid_spec = pltpu.PrefetchScalarGridSpec(
        num_scalar_prefetch=1,
        grid=(NT,),
        in_specs=[pl.BlockSpec(memory_space=pl.ANY),
                  pl.BlockSpec((tb, LANES), lambda j, gd: (j, 0)),
                  pl.BlockSpec((tb, D), lambda j, gd: (j, 0)),
                  pl.BlockSpec((1, 6, D), lambda j, gd: (j // tpb, 0, 0)),
                  pl.BlockSpec((1, D), lambda j, gd: (0, 0)),
                  pl.BlockSpec((1, D), lambda j, gd: (0, 0))],
        out_specs=pl.BlockSpec((tb, D), lambda j, gd: (j, 0)),
        scratch_shapes=[pltpu.VMEM((2, SL, D // 2), U32), pltpu.SemaphoreType.DMA((2,))],
    )
    return pl.pallas_call(
        kern,
        grid_spec=grid_spec,
        out_shape=jax.ShapeDtypeStruct((N, D), F32),
        compiler_params=_cparams(),
        name="combine",
    )(gd, ys, col, x1, mod3, g, b)


def _layer(x, c, l, w_ada, b_ada, w_in, w_conv, b_conv, b_igate, b_fgate, mlstm_norm_g, w_gla_a, b_gla_a,
           gla_norm_g, w_out, ln1_g, ln1_b, w_route_group, b_route_group, w_route_expert, b_route_expert,
           w_gate, w_up, w_down, ln2_g, ln2_b):
    B, S, D = x.shape
    N = B * S
    x2 = x.reshape(N, D)
    tm_in = min(512, S)
    tm = min(256, S)
    lm = min(256, S)
    assert S % tm_in == 0 and S % tm == 0 and tm_in % lm == 0 and S % G_CHUNK == 0
    assert w_in.shape[1:] == (D, IN_TOT) and w_gate.shape[1:] == (N_EXP, D, D_EXP)

    mod3 = _ada(c, w_ada[l], b_ada[l]).reshape(B, 6, D)

    wa_pad = jnp.zeros((LANES, G_KW), F32).at[SM_A:SM_A + G_RANK].set(w_gla_a[l]).astype(BF16)
    bg = (jnp.zeros((2 * SUBLANES, 1), F32).at[0:M_HEADS, 0].set(b_igate[l])
          .at[SUBLANES:SUBLANES + M_HEADS, 0].set(b_fgate[l]))
    oa, la, g3 = _inproj(x2, mod3, jnp.swapaxes(w_in, 1, 2), w_conv[l], b_conv[l].reshape(1, -1), wa_pad,
                         b_gla_a[l].reshape(1, -1), bg, S=S, tm=tm_in, lm=lm, layer=l)

    u_tri = jnp.asarray(np.triu(np.ones((lm, lm), np.float32)))
    nb = 4 if B % 4 == 0 else (2 if B % 2 == 0 else 1)
    ts = min(512, S)
    hm = _mlstm(oa, g3, u_tri, mlstm_norm_g[l].reshape(1, -1), B=B, S=S, L=lm, nb=nb, ts=ts)
    w3_np, mk_np = _gla_consts()
    hg = _gla(oa, la, jnp.asarray(w3_np, BF16), jnp.asarray(mk_np), gla_norm_g[l].reshape(1, -1), B=B, S=S,
              nb=nb, ts=ts)

    br = (jnp.zeros((1, LANES), F32).at[0, 0:N_GROUPS].set(b_route_group[l])
          .at[0, SUBLANES:SUBLANES + N_EXP].set(b_route_expert[l]))
    x1, u2, rrow = _outproj(hm, hg, w_out[l], x2, mod3, ln1_g[l].reshape(1, -1), ln1_b[l].reshape(1, -1),
                            jnp.swapaxes(w_route_group, 1, 2), jnp.swapaxes(w_route_expert, 1, 2), br,
                            S=S, tb=tm, nh=4 if S % (4 * tm) == 0 else 1, layer=l)

    u_cnt = jnp.asarray(np.triu(np.ones((tm, tm), np.float32)), BF16)
    ltri = jnp.asarray(np.tril(np.ones((N_EXP, N_EXP), np.float32), -1))
    srow, col, gd3, meta = _route(rrow, u_cnt, ltri, TM=FFN_TM)
    gd = gd3.reshape(N // tm + 2, LANES)
    n_tiles = _ffn_tiles(N, tm)
    te, nv, vr = meta[0, :n_tiles], meta[1, 0:1], meta[2, :n_tiles]

    xs = _dispatch(gd, srow, u2, n_tiles=n_tiles, TM=FFN_TM)
    ys = _ffn(te, nv, vr, xs, w_gate[l], w_up[l], w_down[l], TM=FFN_TM)
    out = _combine(gd, ys, col, x1, mod3, ln2_g[l].reshape(1, -1), ln2_b[l].reshape(1, -1), S=S, tb=tm)
    return out.reshape(B, S, D)


def kernel(x, c, w_ada, b_ada, w_in, w_conv, b_conv, b_igate, b_fgate, mlstm_norm_g, w_gla_a, b_gla_a,
           gla_norm_g, w_out, ln1_g, ln1_b, w_route_group, b_route_group, w_route_expert, b_route_expert,
           w_gate, w_up, w_down, ln2_g, ln2_b):
    for l in range(DEPTH):
        x = _layer(x, c, l, w_ada, b_ada, w_in, w_conv, b_conv, b_igate, b_fgate, mlstm_norm_g, w_gla_a,
                   b_gla_a, gla_norm_g, w_out, ln1_g, ln1_b, w_route_group, b_route_group, w_route_expert,
                   b_route_expert, w_gate, w_up, w_down, ln2_g, ln2_b)
    return x
```

```python
import functools

import numpy as np
import jax
import jax.numpy as jnp
from jax import lax
from jax.experimental import pallas as pl
from jax.experimental.pallas import tpu as pltpu

F32 = jnp.float32
BF16 = jnp.bfloat16
U32 = jnp.uint32
HIGHEST = lax.Precision.HIGHEST

DEPTH = 1
M_HEADS = 4
M_HD = 128
M_W = M_HEADS * M_HD
CONV_W = 4
G_HEADS = 4
G_DK = 64
G_DV = 128
G_W = G_HEADS * G_DV
G_KW = G_HEADS * G_DK
G_RANK = 16
G_TAU = 16.0
G_CHUNK = 64
N_GROUPS = 4
E_PER_G = 8
N_EXP = N_GROUPS * E_PER_G
D_EXP = 512
ALPHA = (2 * DEPTH) ** 0.25
LN_EPS = 1e-5

LANES = 128
SUBLANES = 8
VMEM_LIMIT = 48 * 1024 * 1024

C_QK = 0
C_VO = 1024
C_GQK = 2048
C_GV = 2560
C_GG = 3072
C_SMALL = 3584
C_TOT = 3712
SM_I, SM_F, SM_A = 0, 8, 16
IN_GATES = 4 * M_W
IN_G = IN_GATES + 2 * M_HEADS
IN_GA = IN_G + 2 * G_KW + 2 * G_W
IN_TOT = IN_GA + G_RANK

FFN_TM = 512
FFN_SUB = 2
GRAN = SUBLANES
G_LAST = LANES - 1


def _cparams(n_axes=1):
    return pltpu.CompilerParams(dimension_semantics=("arbitrary",) * n_axes,
                                vmem_limit_bytes=VMEM_LIMIT)


def _sigmoid(x):
    return 1.0 / (1.0 + jnp.exp(-x))


def _log_sigmoid(x):
    return jnp.minimum(x, 0.0) - jnp.log(1.0 + jnp.exp(-jnp.abs(x)))


def _ada_kernel(c_ref, w_ref, b_ref, o_ref):
    c = c_ref[...]
    ca = (c * _sigmoid(c)).astype(BF16)
    o_ref[...] = jnp.dot(ca, w_ref[...].astype(BF16), preferred_element_type=F32) + b_ref[...]


def _ada(c, w, b):
    B, D = c.shape
    n_out = w.shape[1]
    tn = 1024
    return pl.pallas_call(
        _ada_kernel,
        grid=(n_out // tn,),
        in_specs=[pl.BlockSpec((B, D), lambda j: (0, 0)),
                  pl.BlockSpec((D, tn), lambda j: (0, j)),
                  pl.BlockSpec((1, tn), lambda j: (0, j))],
        out_specs=pl.BlockSpec((B, tn), lambda j: (0, j)),
        out_shape=jax.ShapeDtypeStruct((B, n_out), F32),
        compiler_params=_cparams(),
        name="ada",
    )(c, w, b.reshape(1, n_out))


def _inproj_kernel(x_ref, mod_ref, win_ref, wc_ref, bc_ref, wa_ref, ba_ref, bg_ref,
                   oa_ref, la_ref, g_ref, halo_ref, w_ref, *, tm, tpb, lm):
    i = pl.program_id(0)

    @pl.when(i == 0)
    def _():
        rc = 2 * LANES
        for r in range(0, IN_GATES, rc):
            w_ref[:, r:r + rc] = win_ref[0, r:r + rc, :].T.astype(BF16)
        for r in range(0, C_SMALL - C_GQK, rc):
            w_ref[:, C_GQK + r:C_GQK + r + rc] = win_ref[0, IN_G + r:IN_G + r + rc, :].T.astype(BF16)
        gates = win_ref[0, IN_GATES:IN_G, :]
        z = lambda n: jnp.zeros((n, gates.shape[1]), F32)
        small = jnp.concatenate([gates[0:M_HEADS], z(SM_F - M_HEADS), gates[M_HEADS:2 * M_HEADS],
                                 z(SM_A - SM_F - M_HEADS), win_ref[0, IN_GA:IN_TOT, :],
                                 z(LANES - SM_A - G_RANK)], axis=0)
        w_ref[:, C_SMALL:C_TOT] = small.T.astype(BF16)

    @pl.when(i % tpb == 0)
    def _():
        halo_ref[0:SUBLANES, :] = jnp.zeros((SUBLANES, halo_ref.shape[1]), F32)

    mod = mod_ref[0]
    u = (x_ref[...] * (1.0 + mod[1:2, :]) + mod[0:1, :]).astype(BF16)

    def proj(c0, c1):
        return jnp.dot(u, w_ref[:, c0:c1], preferred_element_type=F32)

    p = proj(C_QK, C_QK + 2 * M_W)
    halo_ref[SUBLANES:SUBLANES + tm, :] = p
    acc = bc_ref[...] + wc_ref[CONV_W - 1:CONV_W, :] * p
    for j in range(CONV_W - 1):
        acc = acc + wc_ref[j:j + 1, :] * halo_ref[pl.ds(SUBLANES - (CONV_W - 1) + j, tm), :]
    halo_ref[0:SUBLANES, :] = p[tm - SUBLANES:, :]
    qk = acc * _sigmoid(acc)
    oa_ref[:, C_QK:C_QK + M_W] = qk[:, :M_W].astype(BF16)
    oa_ref[:, C_QK + M_W:C_QK + 2 * M_W] = (qk[:, M_W:] * (M_HD ** -0.5)).astype(BF16)

    ps = proj(C_SMALL, C_TOT)
    la = jnp.dot(ps.astype(BF16), wa_ref[...], preferred_element_type=F32) + ba_ref[...]
    la_ref[...] = _log_sigmoid(la) * (1.0 / G_TAU)
    pt = ps.T
    gi = pt[SM_I:SM_I + SUBLANES, :] + bg_ref[0:SUBLANES, :]
    gf = _log_sigmoid(pt[SM_F:SM_F + SUBLANES, :] + bg_ref[SUBLANES:2 * SUBLANES, :])
    for j in range(tm // lm):
        g_ref[j, 0:SUBLANES, :] = gi[:, j * lm:(j + 1) * lm]
        g_ref[j, SUBLANES:2 * SUBLANES, :] = gf[:, j * lm:(j + 1) * lm]

    p = proj(C_VO, C_VO + 2 * M_W)
    oa_ref[:, C_VO:C_VO + 2 * M_W] = p.astype(BF16)

    p = proj(C_GQK, C_GQK + G_KW)
    oa_ref[:, C_GQK:C_GQK + G_KW] = (p * (G_DK ** -0.5)).astype(BF16)
    p = proj(C_GQK + G_KW, C_SMALL)
    oa_ref[:, C_GQK + G_KW:C_SMALL] = p.astype(BF16)


def _inproj(x2, mod3, w_in, w_conv, b_conv, wa_pad, b_gla, bg, *, S, tm, lm, layer):
    N, D = x2.shape
    tpb = S // tm
    kern = functools.partial(_inproj_kernel, tm=tm, tpb=tpb, lm=lm)
    return pl.pallas_call(
        kern,
        grid=(N // tm,),
        in_specs=[pl.BlockSpec((tm, D), lambda i: (i, 0)),
                  pl.BlockSpec((1, 6, D), lambda i: (i // tpb, 0, 0)),
                  pl.BlockSpec((1, IN_TOT, D), lambda i: (layer, 0, 0), pipeline_mode=pl.Buffered(1)),
                  pl.BlockSpec((CONV_W, 2 * M_W), lambda i: (0, 0)),
                  pl.BlockSpec((1, 2 * M_W), lambda i: (0, 0)),
                  pl.BlockSpec((LANES, G_KW), lambda i: (0, 0)),
                  pl.BlockSpec((1, G_KW), lambda i: (0, 0)),
                  pl.BlockSpec((2 * SUBLANES, 1), lambda i: (0, 0))],
        out_specs=[pl.BlockSpec((tm, C_SMALL), lambda i: (i, 0)),
                   pl.BlockSpec((tm, G_KW), lambda i: (i, 0)),
                   pl.BlockSpec((tm // lm, 2 * SUBLANES, lm), lambda i: (i, 0, 0))],
        out_shape=[jax.ShapeDtypeStruct((N, C_SMALL), BF16),
                   jax.ShapeDtypeStruct((N, G_KW), F32),
                   jax.ShapeDtypeStruct((N // lm, 2 * SUBLANES, lm), F32)],
        scratch_shapes=[pltpu.VMEM((SUBLANES + tm, 2 * M_W), F32), pltpu.VMEM((D, C_TOT), BF16)],
        compiler_params=_cparams(),
        name="inproj",
    )(x2, mod3, w_in, w_conv, b_conv, wa_pad, b_gla, bg)


def _mlstm_sel():
    sel = np.zeros((2 * LANES, 2 * M_HEADS * M_HD), np.float32)
    for j in range(2 * M_HEADS):
        src = (SUBLANES if j < M_HEADS else 3 * SUBLANES) + j % M_HEADS
        sel[src, M_HD * j:M_HD * (j + 1)] = 1.0
        sel[LANES + src, M_HD * j:M_HD * (j + 1)] = 1.0
    return sel


def _mlstm_kernel(qk_ref, vo_ref, g_ref, u_ref, gain_ref, sel_ref, out_ref, c_ref, zt_ref, a_ref, dec_ref, m_ref,
                  *, L, NC, nb):
    @pl.when(pl.program_id(1) == 0)
    def _():
        c_ref[...] = jnp.zeros_like(c_ref)
        m_ref[...] = jnp.zeros_like(m_ref)

    tril = (lax.broadcasted_iota(jnp.int32, (L, L), 0) >= lax.broadcasted_iota(jnp.int32, (L, L), 1))
    ones_v = jnp.ones((L, M_HD), BF16)
    zpad = jnp.zeros((LANES - 4 * SUBLANES, L), F32)
    zgroup = lax.broadcasted_iota(jnp.int32, (L, LANES), 1) // SUBLANES
    factor_cols = (zgroup == 1) | (zgroup == 3)

    order = [(bi, c) for bi in range(nb) for c in range(NC)]
    f_all = jnp.concatenate([g_ref[bi, c, SUBLANES:2 * SUBLANES, :] for bi, c in order], axis=0)
    i_all = jnp.concatenate([g_ref[bi, c, 0:SUBLANES, :] for bi, c in order], axis=0)
    b_all = jnp.dot(f_all, u_ref[...], preferred_element_type=F32, precision=HIGHEST)
    a_all = i_all - b_all
    lane_all = lax.broadcasted_iota(jnp.int32, a_all.shape, 1)
    g_all = a_all
    s = 1
    while s < L:
        g_all = jnp.maximum(g_all, jnp.where(lane_all >= s, pltpu.roll(g_all, s, 1), -jnp.inf))
        s *= 2
    for bi in range(nb):
        m_prev = m_ref[bi][:, 0:1]
        for c in range(NC):
            ci = bi * NC + c
            r8 = slice(SUBLANES * ci, SUBLANES * (ci + 1))
            a, b = a_all[r8], b_all[r8]
            a_ref[ci] = a
            M = jnp.maximum(g_all[r8], m_prev)
            ML = M[:, L - 1:L]
            Z = jnp.concatenate([M, jnp.exp(m_prev - M), jnp.exp(-(b + M)), jnp.exp(a - ML), zpad],
                                axis=0)
            zt_ref[ci] = Z.T
            dec_ref[ci] = jnp.broadcast_to(jnp.exp(m_prev - ML), (SUBLANES, 2 * M_HD))
            m_prev = b[:, L - 1:L] + ML
        m_ref[bi] = jnp.broadcast_to(m_prev, (SUBLANES, LANES))

    chains = [(bi, h) for bi in range(nb) for h in range(M_HEADS)]
    nt = (((1,), (1,)), ((), ()))
    tn = (((0,), (0,)), ((), ()))

    def chunk(c, carry):
        rows = pl.ds(pl.multiple_of(c * L, L), L)
        Zt = [zt_ref[bi * NC + c] for bi in range(nb)]
        a = [a_ref[bi * NC + c] for bi in range(nb)]
        dec = [dec_ref[bi * NC + c] for bi in range(nb)]
        hs = [slice(h * M_HD, (h + 1) * M_HD) for h in range(M_HEADS)]
        hs2 = [slice(M_W + h * M_HD, M_W + (h + 1) * M_HD) for h in range(M_HEADS)]
        q = [qk_ref[bi, rows, hs[h]] for bi, h in chains]
        k = [qk_ref[bi, rows, hs2[h]] for bi, h in chains]
        vext = [jnp.concatenate([vo_ref[bi, rows, hs[h]], ones_v], axis=1) for bi, h in chains]
        cst = [c_ref[bi * M_HEADS + h] for bi, h in chains]
        n = range(len(chains))
        sc = [lax.dot_general(q[i], k[i], nt, preferred_element_type=F32) for i in n]
        qc = [jnp.dot(q[i], cst[i].astype(BF16), preferred_element_type=F32) for i in n]
        pm = [(sc[i] * jnp.exp(jnp.where(tril, a[bi][h:h + 1, :] - Zt[bi][:, h:h + 1], -jnp.inf))).astype(BF16)
              for i, (bi, h) in enumerate(chains)]
        pv = [jnp.dot(pm[i], vext[i], preferred_element_type=F32) for i in n]
        rep = []
        for bi in range(nb):
            zf = jnp.where(factor_cols, Zt[bi], 0.0)
            zh = zf.astype(BF16)
            zl = (zf - zh.astype(F32)).astype(BF16)
            rep.append(jnp.dot(jnp.concatenate([zh, zl], axis=1), sel_ref[...], preferred_element_type=F32))
        e_inter = [rep[bi][:, M_HD * h:M_HD * (h + 1)] for bi, h in chains]
        w_state = [rep[bi][:, M_HD * (M_HEADS + h):M_HD * (M_HEADS + h + 1)] for bi, h in chains]
        kw = [(w_state[i] * k[i].astype(F32)).astype(BF16) for i in n]
        upd = [lax.dot_general(kw[i], vext[i], tn, preferred_element_type=F32) for i in n]
        for i, (bi, h) in enumerate(chains):
            c_ref[bi * M_HEADS + h] = dec[bi][h:h + 1, :] * cst[i] + upd[i]
            nd = pv[i] + jnp.concatenate([e_inter[i], e_inter[i]], axis=1) * qc[i]
            hh = nd[:, :M_HD] / jnp.maximum(jnp.abs(nd[:, M_HD:]),
                                            Zt[bi][:, 2 * SUBLANES + h:2 * SUBLANES + h + 1])
            hh = _sigmoid(vo_ref[bi, rows, hs2[h]].astype(F32)) * hh
            hn = hh * lax.rsqrt(jnp.mean(hh * hh, axis=-1, keepdims=True) + LN_EPS)
            out_ref[bi, rows, hs[h]] = (hn * gain_ref[:, hs[h]]).astype(BF16)
        return carry

    lax.fori_loop(0, NC, chunk, 0, unroll=True)


def _mlstm(oa, g3, u_tri, gain, *, B, S, L, nb, ts):
    N = oa.shape[0]
    NC = ts // L
    oa3 = oa.reshape(B, S, oa.shape[1])
    g4 = g3.reshape(B, S // L, 2 * SUBLANES, L)
    sel = jnp.asarray(_mlstm_sel(), BF16)
    kern = functools.partial(_mlstm_kernel, L=L, NC=NC, nb=nb)
    out = pl.pallas_call(
        kern,
        grid=(B // nb, S // ts),
        in_specs=[pl.BlockSpec((nb, ts, 2 * M_W), lambda b, t: (b, t, C_QK // (2 * M_W))),
                  pl.BlockSpec((nb, ts, 2 * M_W), lambda b, t: (b, t, C_VO // (2 * M_W))),
                  pl.BlockSpec((nb, NC, 2 * SUBLANES, L), lambda b, t: (b, t, 0, 0)),
                  pl.BlockSpec((L, L), lambda b, t: (0, 0)),
                  pl.BlockSpec((1, M_W), lambda b, t: (0, 0)),
                  pl.BlockSpec(sel.shape, lambda b, t: (0, 0))],
        out_specs=pl.BlockSpec((nb, ts, M_W), lambda b, t: (b, t, 0)),
        out_shape=jax.ShapeDtypeStruct((B, S, M_W), BF16),
        scratch_shapes=[pltpu.VMEM((nb * M_HEADS, M_HD, 2 * M_HD), F32),
                        pltpu.VMEM((nb * NC, L, LANES), F32),
                        pltpu.VMEM((nb * NC, SUBLANES, L), F32),
                        pltpu.VMEM((nb * NC, SUBLANES, 2 * M_HD), F32),
                        pltpu.VMEM((nb, SUBLANES, LANES), F32)],
        compiler_params=_cparams(2),
        name="mlstm",
    )(oa3, oa3, g4, u_tri, gain, sel)
    return out.reshape(N, M_W)


_G_LEVELS = 6
_G_XROW = 2 * G_CHUNK + SUBLANES


def _gla_consts():
    L = G_CHUNK
    t = np.arange(L)
    blocks = [(t[None, :] <= t[:, None]).astype(np.float32),
              (t[None, :] > t[:, None]).astype(np.float32),
              np.ones((SUBLANES, L), np.float32)]
    masks = [np.eye(L, dtype=np.float32)]
    m = 1
    while m < L:
        wl = np.zeros((L, L), np.float32)
        for r in range(L):
            r0 = (r // (2 * m)) * 2 * m + m
            if r % (2 * m) >= m:
                wl[r, r0:r + 1] = 1.0
            else:
                wl[r, r + 1:r0] = 1.0
        blocks.append(wl)
        tt, ss = t[:, None], t[None, :]
        masks.append(((tt // (2 * m) == ss // (2 * m)) & (tt % (2 * m) >= m)
                      & (ss % (2 * m) < m)).astype(np.float32))
        m *= 2
    w = np.concatenate(blocks, axis=0)
    w3 = np.concatenate([w, w, w], axis=1)
    mk = np.stack([np.concatenate([x] * G_HEADS, axis=0) for x in masks])
    return w3, mk


def _gla_kernel(qk_ref, v_ref, gg_ref, la_ref, w3_ref, mk_ref, gain_ref, out_ref, st_ref, *, NC, nb):
    L = G_CHUNK

    @pl.when(pl.program_id(1) == 0)
    def _():
        st_ref[...] = jnp.zeros_like(st_ref)

    lane_head = lax.broadcasted_iota(jnp.int32, (L, G_KW), 1) // G_DK
    br = lax.broadcasted_iota(jnp.int32, (2 * G_DV, LANES), 0) < G_DV
    bl = lax.broadcasted_iota(jnp.int32, (2 * G_DV, LANES), 1) < G_DK
    bmask = br == bl
    nt = (((1,), (1,)), ((), ()))
    tn = (((0,), (0,)), ((), ()))

    def chunk(c, carry):
        rows = pl.ds(pl.multiple_of(c * L, L), L)
        X, q, k = [], [], []
        for bi in range(nb):
            la = la_ref[bi, rows, :]
            hi = la.astype(BF16)
            r1 = la - hi.astype(F32)
            mid = r1.astype(BF16)
            lo = (r1 - mid.astype(F32)).astype(BF16)
            stk = jnp.concatenate([hi, mid, lo], axis=0)
            X.append(jnp.exp(jnp.dot(w3_ref[...], stk, preferred_element_type=F32)))
            q.append(qk_ref[bi, rows, 0:G_KW].astype(F32))
            k.append(qk_ref[bi, rows, G_KW:2 * G_KW].astype(F32))

        sc = [[None] * (_G_LEVELS + 1) for _ in range(nb)]
        for lev in range(_G_LEVELS + 1):
            for bi in range(nb):
                if lev == 0:
                    qt, kt = q[bi], k[bi]
                else:
                    xl = X[bi][_G_XROW + L * (lev - 1):_G_XROW + L * lev, :]
                    qt, kt = q[bi] * xl, k[bi] * xl
                q4 = jnp.concatenate([jnp.where(lane_head == h, qt, 0.0) for h in range(G_HEADS)],
                                     axis=0).astype(BF16)
                sc[bi][lev] = lax.dot_general(q4, kt.astype(BF16), nt, preferred_element_type=F32)
        Ab = []
        for bi in range(nb):
            A = sc[bi][0] * mk_ref[0]
            for lev in range(1, _G_LEVELS + 1):
                A = A + sc[bi][lev] * mk_ref[lev]
            Ab.append(A.astype(BF16))

        for bi in range(nb):
            gg = gg_ref[bi, rows, :].astype(F32)
            gate = gg * _sigmoid(gg)
            for p in range(2):
                ls = slice(LANES * p, LANES * (p + 1))
                vp = v_ref[bi, rows, 2 * G_DV * p:2 * G_DV * (p + 1)]
                oi = [jnp.dot(Ab[bi][L * (2 * p + hh):L * (2 * p + hh + 1)],
                              vp[:, G_DV * hh:G_DV * (hh + 1)], preferred_element_type=F32)
                      for hh in range(2)]
                st = st_ref[bi, p]
                qc = (q[bi][:, ls] * X[bi][0:L, ls]).astype(BF16)
                o_inter = lax.dot_general(qc, st.astype(BF16), nt, preferred_element_type=F32)
                kc = (k[bi][:, ls] * X[bi][L:2 * L, ls]).astype(BF16)
                upd = lax.dot_general(vp, kc, tn, preferred_element_type=F32)
                dec = X[bi][2 * L:2 * L + 1, ls]
                st_ref[bi, p] = jnp.where(bmask, dec * st + upd, 0.0)
                for hh in range(2):
                    o = o_inter[:, G_DV * hh:G_DV * (hh + 1)] + oi[hh]
                    hn = o * lax.rsqrt(jnp.mean(o * o, axis=-1, keepdims=True) + LN_EPS)
                    hs = slice(G_DV * (2 * p + hh), G_DV * (2 * p + hh + 1))
                    out_ref[bi, rows, hs] = (hn * gain_ref[:, hs] * gate[:, hs]).astype(BF16)
        return carry

    lax.fori_loop(0, NC, chunk, 0, unroll=2 if NC % 2 == 0 else 1)


def _gla(oa, la, w3, mk, gain, *, B, S, nb, ts):
    N = oa.shape[0]
    oa3 = oa.reshape(B, S, oa.shape[1])
    la3 = la.reshape(B, S, G_KW)
    kern = functools.partial(_gla_kernel, NC=ts // G_CHUNK, nb=nb)
    out = pl.pallas_call(
        kern,
        grid=(B // nb, S // ts),
        in_specs=[pl.BlockSpec((nb, ts, 2 * G_KW), lambda b, t: (b, t, C_GQK // (2 * G_KW))),
                  pl.BlockSpec((nb, ts, G_W), lambda b, t: (b, t, C_GV // G_W)),
                  pl.BlockSpec((nb, ts, G_W), lambda b, t: (b, t, C_GG // G_W)),
                  pl.BlockSpec((nb, ts, G_KW), lambda b, t: (b, t, 0)),
                  pl.BlockSpec(w3.shape, lambda b, t: (0, 0)),
                  pl.BlockSpec(mk.shape, lambda b, t: (0, 0, 0)),
                  pl.BlockSpec((1, G_W), lambda b, t: (0, 0))],
        out_specs=pl.BlockSpec((nb, ts, G_W), lambda b, t: (b, t, 0)),
        out_shape=jax.ShapeDtypeStruct((B, S, G_W), BF16),
        scratch_shapes=[pltpu.VMEM((nb, 2, 2 * G_DV, LANES), F32)],
        compiler_params=_cparams(2),
        name="gla",
    )(oa3, oa3, oa3, la3, w3, mk, gain)
    return out.reshape(N, G_W)


def _layer_norm(z, g, b):
    mu = jnp.mean(z, axis=-1, keepdims=True)
    zc = z - mu
    var = jnp.mean(zc * zc, axis=-1, keepdims=True)
    return zc * lax.rsqrt(var + LN_EPS) * g + b


def _outproj_kernel(hm_ref, hg_ref, wf_ref, x_ref, mod_ref, g_ref, b_ref, wrg_ref, wre_ref, br_ref,
                    x1_ref, u2_ref, rrow_ref, w_ref, wr_ref, *, tb, nh):
    @pl.when(pl.program_id(0) == 0)
    def _():
        w_ref[...] = wf_ref[...].astype(BF16)
        z = lambda n: jnp.zeros((n, wrg_ref.shape[2]), F32)
        wt = jnp.concatenate([wrg_ref[0], z(SUBLANES - N_GROUPS), wre_ref[0],
                              z(LANES - SUBLANES - N_EXP)], axis=0).T
        hi = wt.astype(BF16)
        wr_ref[:, 0:LANES] = hi
        wr_ref[:, LANES:2 * LANES] = (wt - hi.astype(F32)).astype(BF16)

    mod = mod_ref[0]
    blocks = [slice(tb * j, tb * (j + 1)) for j in range(nh)]
    y = [jnp.dot(hm_ref[r, :], w_ref[0:M_W, :], preferred_element_type=F32)
         + jnp.dot(hg_ref[r, :], w_ref[M_W:M_W + G_W, :], preferred_element_type=F32) for r in blocks]
    u2 = []
    for j, r in enumerate(blocks):
        z = ALPHA * x_ref[r, :] + (1.0 + mod[2:3, :]) * y[j]
        x1 = _layer_norm(z, g_ref[...], b_ref[...])
        x1_ref[r, :] = x1
        u2.append(x1 * (1.0 + mod[4:5, :]) + mod[3:4, :])
        u2_ref[r, :] = u2[j].astype(BF16)

    u2h = [u.astype(BF16) for u in u2]
    u2l = [(u2[j] - u2h[j].astype(F32)).astype(BF16) for j in range(nh)]
    lh = [jnp.dot(u, wr_ref[...], preferred_element_type=F32) for u in u2h]
    ll = [jnp.dot(u, wr_ref[:, 0:LANES], preferred_element_type=F32) for u in u2l]
    for j in range(nh):
        logits = lh[j][:, 0:LANES] + lh[j][:, LANES:2 * LANES] + ll[j] + br_ref[...]
        rrow_ref[j] = _route_select(logits.T, tb)


def _route_select(lt, tm):
    row = lax.broadcasted_iota(jnp.int32, (SUBLANES, tm), 0)
    gl = jnp.where(row < N_GROUPS, lt[0:SUBLANES, :], -jnp.inf)
    gmax = jnp.max(gl, axis=0, keepdims=True)
    gsel = jnp.min(jnp.where(gl == gmax, row, SUBLANES), axis=0, keepdims=True)
    pg = 1.0 / jnp.sum(jnp.exp(gl - gmax), axis=0, keepdims=True)
    ein = jnp.zeros((SUBLANES, tm), F32)
    for g in range(N_GROUPS):
        ein = jnp.where(gsel == g, lt[SUBLANES * (g + 1):SUBLANES * (g + 2), :], ein)
    v1 = jnp.max(ein, axis=0, keepdims=True)
    i1 = jnp.min(jnp.where(ein == v1, row, SUBLANES), axis=0, keepdims=True)
    rest = jnp.where(row == i1, -jnp.inf, ein)
    v2 = jnp.max(rest, axis=0, keepdims=True)
    i2 = jnp.min(jnp.where(rest == v2, row, SUBLANES), axis=0, keepdims=True)
    t2 = jnp.exp(v2 - v1)
    p1 = 1.0 / (1.0 + t2)
    e0 = (gsel * E_PER_G + i1).astype(F32)
    e1 = (gsel * E_PER_G + i2).astype(F32)
    return jnp.concatenate([e0, e1, pg * p1, pg * (t2 * p1), jnp.zeros((SUBLANES - 4, tm), F32)], axis=0)


def _outproj(hm, hg, w_out, x2, mod3, g, b, wrg_t, wre_t, br, *, S, tb, nh, layer):
    N, D = x2.shape
    tm = tb * nh
    tpb = S // tm
    kern = functools.partial(_outproj_kernel, tb=tb, nh=nh)
    return pl.pallas_call(
        kern,
        grid=(N // tm,),
        in_specs=[pl.BlockSpec((tm, M_W), lambda i: (i, 0)),
                  pl.BlockSpec((tm, G_W), lambda i: (i, 0)),
                  pl.BlockSpec((M_W + G_W, D), lambda i: (0, 0), pipeline_mode=pl.Buffered(1)),
                  pl.BlockSpec((tm, D), lambda i: (i, 0)),
                  pl.BlockSpec((1, 6, D), lambda i: (i // tpb, 0, 0)),
                  pl.BlockSpec((1, D), lambda i: (0, 0)),
                  pl.BlockSpec((1, D), lambda i: (0, 0)),
                  pl.BlockSpec((1, N_GROUPS, D), lambda i: (layer, 0, 0)),
                  pl.BlockSpec((1, N_EXP, D), lambda i: (layer, 0, 0)),
                  pl.BlockSpec((1, LANES), lambda i: (0, 0))],
        out_specs=[pl.BlockSpec((tm, D), lambda i: (i, 0)),
                   pl.BlockSpec((tm, D), lambda i: (i, 0)),
                   pl.BlockSpec((nh, SUBLANES, tb), lambda i: (i, 0, 0))],
        out_shape=[jax.ShapeDtypeStruct((N, D), F32),
                   jax.ShapeDtypeStruct((N, D), BF16),
                   jax.ShapeDtypeStruct((N // tb, SUBLANES, tb), F32)],
        scratch_shapes=[pltpu.VMEM((M_W + G_W, D), BF16), pltpu.VMEM((D, 2 * LANES), BF16)],
        compiler_params=_cparams(),
        name="outproj",
    )(hm, hg, w_out, x2, mod3, g, b, wrg_t, wre_t, br)


def _slots_per_tile(tb):
    worst = 2 * tb + N_EXP * (GRAN - 1)
    return -(-worst // LANES) * LANES


def _ffn_tiles(n_tok, tb):
    worst_rows = 2 * n_tok + (n_tok // tb) * N_EXP * (GRAN - 1)
    return -(-worst_rows // FFN_TM) + N_EXP


def _route_kernel(rr_ref, u_ref, lt_ref, srow_ref, col_ref, gd_ref, meta_ref, mg_ref, part_ref,
                  *, NT, tb, TM):
    iota_e = lax.broadcasted_iota(jnp.int32, (N_EXP, tb), 0).astype(F32)
    glane = lax.broadcasted_iota(jnp.int32, (N_EXP, LANES), 1).astype(F32)
    ltri = lt_ref[...]

    def prefix_e(col):
        return jnp.dot(ltri, jnp.broadcast_to(col, (N_EXP, LANES)),
                       preferred_element_type=F32, precision=HIGHEST)[:, 0:1]

    def p1(j, run8):
        r = rr_ref[j]
        oh0 = jnp.where(iota_e == r[0:1, :], 1.0, 0.0)
        oh1 = jnp.where(iota_e == r[1:2, :], 1.0, 0.0)
        cum0 = jnp.dot(oh0.astype(BF16), u_ref[...], preferred_element_type=F32)
        cum1 = jnp.dot(oh1.astype(BF16), u_ref[...], preferred_element_type=F32)
        c0 = jnp.sum(oh0, axis=1, keepdims=True)
        n8 = jnp.floor((c0 + jnp.sum(oh1, axis=1, keepdims=True) + (GRAN - 1.0)) * (1.0 / GRAN))
        lo8 = prefix_e(n8)
        s0 = jnp.sum(oh0 * (GRAN * lo8 + cum0 - 1.0), axis=0, keepdims=True)
        s1 = jnp.sum(oh1 * (GRAN * lo8 + c0 + cum1 - 1.0), axis=0, keepdims=True)
        info = jnp.concatenate([s0, s1, r[2:4, :], jnp.zeros((SUBLANES - 4, tb), F32)], axis=0)
        srow_ref[j] = info
        col_ref[pl.ds(pl.multiple_of(j * tb, tb), tb), :] = jnp.concatenate(
            [info, jnp.zeros((LANES - SUBLANES, tb), F32)], axis=0).T
        mg = jnp.where((lo8 <= glane) & (glane < lo8 + n8), 1.0, 0.0)
        mg_ref[j] = mg
        part = jnp.sum(mg * (run8 + glane - lo8), axis=0, keepdims=True)
        gcnt = jnp.broadcast_to(jnp.sum(n8, axis=0, keepdims=True), (1, LANES))
        part_ref[j] = jnp.concatenate([part, gcnt, jnp.zeros((SUBLANES - 2, LANES), F32)], axis=0)
        return run8 + n8

    tot8 = lax.fori_loop(0, NT, p1, jnp.zeros((N_EXP, 1), F32), unroll=8 if NT % 8 == 0 else 1)
    seg_t = jnp.floor((tot8 * GRAN + (TM - 1.0)) * (1.0 / TM))
    base_t = prefix_e(seg_t)
    base8 = base_t * (TM // GRAN)
    lane1 = lax.broadcasted_iota(jnp.int32, (1, LANES), 1)

    def p2(j, carry):
        pr = part_ref[j]
        dst = (pr[0:1, :] + jnp.sum(mg_ref[j] * base8, axis=0, keepdims=True)) * GRAN
        gd_ref[j] = jnp.where(lane1 == G_LAST, pr[1:2, :], dst).astype(jnp.int32)
        return carry

    lax.fori_loop(0, NT, p2, 0, unroll=8 if NT % 8 == 0 else 1)
    eye = jnp.where(glane == lax.broadcasted_iota(jnp.int32, (N_EXP, LANES), 0).astype(F32), 1.0, 0.0)
    tail_row = jnp.sum(eye * ((base8 + tot8) * GRAN), axis=0, keepdims=True)
    tail_n8 = jnp.sum(eye * (seg_t * (TM // GRAN) - tot8), axis=0, keepdims=True)
    nv_l = jnp.broadcast_to(jnp.sum(seg_t, axis=0, keepdims=True), (1, LANES))
    gd_ref[NT] = jnp.where(lane1 == G_LAST, nv_l, tail_row).astype(jnp.int32)
    gd_ref[NT + 1] = tail_n8.astype(jnp.int32)
    ti = lax.broadcasted_iota(jnp.int32, (N_EXP, tb), 1).astype(F32)
    te = jnp.sum(jnp.where(base_t <= ti, 1.0, 0.0), axis=0, keepdims=True) - 1.0
    nv = jnp.broadcast_to(jnp.sum(seg_t, axis=0, keepdims=True), (1, tb))
    own = jnp.where((base_t <= ti) & (ti < base_t + seg_t), 1.0, 0.0)
    vr = jnp.sum(own * jnp.clip(tot8 * GRAN - (ti - base_t) * TM, 0.0, TM), axis=0, keepdims=True)
    meta_ref[...] = jnp.concatenate([te, nv, vr, jnp.zeros((SUBLANES - 3, tb), F32)],
                                    axis=0).astype(jnp.int32)


def _route(rrow, u_cnt, ltri, *, TM):
    NT, _, tb = rrow.shape
    kern = functools.partial(_route_kernel, NT=NT, tb=tb, TM=TM)
    full3 = lambda i: (0, 0, 0)
    return pl.pallas_call(
        kern,
        grid=(1,),
        in_specs=[pl.BlockSpec((NT, SUBLANES, tb), full3),
                  pl.BlockSpec((tb, tb), lambda i: (0, 0)),
                  pl.BlockSpec((N_EXP, N_EXP), lambda i: (0, 0))],
        out_specs=[pl.BlockSpec((NT, SUBLANES, tb), full3),
                   pl.BlockSpec((NT * tb, LANES), lambda i: (0, 0)),
                   pl.BlockSpec((NT + 2, 1, LANES), full3),
                   pl.BlockSpec((SUBLANES, tb), lambda i: (0, 0))],
        out_shape=[jax.ShapeDtypeStruct((NT, SUBLANES, tb), F32),
                   jax.ShapeDtypeStruct((NT * tb, LANES), F32),
                   jax.ShapeDtypeStruct((NT + 2, 1, LANES), jnp.int32),
                   jax.ShapeDtypeStruct((SUBLANES, tb), jnp.int32)],
        scratch_shapes=[pltpu.VMEM((NT, N_EXP, LANES), F32), pltpu.VMEM((NT, SUBLANES, LANES), F32)],
        compiler_params=_cparams(),
        name="route",
    )(rrow, u_cnt, ltri)


_HI_MASK = 0xFFFF0000


def _pack_halves(x):
    c = x.shape[1] // 2
    lo = lax.bitcast_convert_type(x[:, :c], U32)
    hi = lax.bitcast_convert_type(x[:, c:], U32)
    return (lo >> 16) | (hi & U32(_HI_MASK))


def _unpack_halves(w):
    lo = lax.bitcast_convert_type(w << 16, F32)
    hi = lax.bitcast_convert_type(w & U32(_HI_MASK), F32)
    return jnp.concatenate([lo, hi], axis=1).astype(BF16)


def _granule_copy(src_ref, src_row, dst_ref, dst_row, sem, n=1):
    cols = pl.ds(0, min(src_ref.shape[-1], dst_ref.shape[-1]))
    return pltpu.make_async_copy(src_ref.at[pl.ds(src_row, n * GRAN), cols],
                                 dst_ref.at[pl.ds(dst_row, n * GRAN), cols], sem)


def _for_granules(n, body, unroll=4):
    def blk(i, carry):
        for t in range(unroll):
            body(i * unroll + t)
        return carry

    def one(g, carry):
        body(g)
        return carry

    nblk = n // unroll
    lax.fori_loop(0, nblk, blk, 0)
    lax.fori_loop(nblk * unroll, n, one, 0)


def _wait_granules(n, src_ref, dst_ref, sem, n_max):
    b = 1
    while b <= n_max:
        @pl.when((n & b) != 0)
        def _(b=b):
            _granule_copy(src_ref, 0, dst_ref, 0, sem, n=b).wait()
        b *= 2


def _dispatch_kernel(gd_ref, srow_ref, u_ref, xs_ref, buf, zbuf, sems, *, NT, SL, TM, n_tiles):
    j = pl.program_id(0)
    slot = j % 2
    zsem = sems.at[2]

    def drain(tile, sl):
        _wait_granules(gd_ref[tile, G_LAST], buf.at[sl], xs_ref, sems.at[sl], SL // GRAN)

    def tile_fill(t):
        return pltpu.make_async_copy(zbuf, xs_ref.at[pl.ds(pl.multiple_of(t * TM, TM), TM), :], zsem)

    def zero_fill(wait):
        for e in range(N_EXP):
            n, row0 = gd_ref[NT + 1, e], gd_ref[NT, e]
            b = TM // GRAN // 2
            while b >= 1:
                @pl.when((n & b) != 0)
                def _(b=b, n=n, row0=row0):
                    start = pl.multiple_of(row0 + ((n >> b.bit_length()) << b.bit_length()) * GRAN, GRAN)
                    cp = pltpu.make_async_copy(zbuf.at[pl.ds(0, b * GRAN), :],
                                               xs_ref.at[pl.ds(start, b * GRAN), :], zsem)
                    cp.wait() if wait else cp.start()
                b //= 2

        def zt(t, carry):
            tile_fill(t).wait() if wait else tile_fill(t).start()
            return carry
        lax.fori_loop(gd_ref[NT, G_LAST], n_tiles, zt, 0)

    @pl.when(j == 0)
    def _():
        zbuf[...] = jnp.zeros_like(zbuf)
        zero_fill(False)

    @pl.when(j >= 2)
    def _():
        drain(j - 2, slot)

    s = srow_ref[0]
    rows = lax.broadcasted_iota(jnp.int32, (SL, s.shape[1]), 0).astype(F32)
    m0 = rows == s[0:1, :]
    m1 = rows == s[1:2, :]
    oh = jnp.where(m0 | m1, 1.0, 0.0).astype(BF16)
    dw = u_ref.shape[1] // 2
    buf[slot, :, 0:dw] = _pack_halves(jnp.dot(oh, u_ref[...], preferred_element_type=F32))
    wrow = jnp.sum(jnp.where(m0, s[2:3, :], 0.0) + jnp.where(m1, s[3:4, :], 0.0), axis=1, keepdims=True)
    buf[slot, :, dw:dw + LANES] = lax.bitcast_convert_type(jnp.broadcast_to(wrow, (SL, LANES)), U32)

    def issue(g):
        _granule_copy(buf.at[slot], pl.multiple_of(g * GRAN, GRAN), xs_ref,
                      pl.multiple_of(gd_ref[j, g], GRAN), sems.at[slot]).start()

    _for_granules(gd_ref[j, G_LAST], issue)

    @pl.when(j == NT - 1)
    def _():
        drain(j, slot)
        if NT > 1:
            drain(j - 1, 1 - slot)
        zero_fill(True)


def _dispatch(gd, srow, u2, *, n_tiles, TM):
    N, D = u2.shape
    NT, _, tb = srow.shape
    SL = _slots_per_tile(tb)
    n_rows = n_tiles * TM
    kern = functools.partial(_dispatch_kernel, NT=NT, SL=SL, TM=TM, n_tiles=n_tiles)
    grid_spec = pltpu.PrefetchScalarGridSpec(
        num_scalar_prefetch=1,
        grid=(NT,),
        in_specs=[pl.BlockSpec((1, SUBLANES, tb), lambda j, gd: (j, 0, 0)),
                  pl.BlockSpec((tb, D), lambda j, gd: (j, 0))],
        out_specs=pl.BlockSpec(memory_space=pl.ANY),
        scratch_shapes=[pltpu.VMEM((2, SL, D // 2 + LANES), U32), pltpu.VMEM((TM, D // 2 + LANES), U32),
                        pltpu.SemaphoreType.DMA((3,))],
    )
    return pl.pallas_call(
        kern,
        grid_spec=grid_spec,
        out_shape=jax.ShapeDtypeStruct((n_rows, D // 2 + LANES), U32),
        compiler_params=_cparams(),
        name="dispatch",
    )(gd, srow, u2)


def _ffn_kernel(te_ref, nv_ref, vr_ref, xs_ref, wg_ref, wu_ref, wd_ref, o_ref, wgb, wub, wdb, sg, su, sd, slot_ref,
                sems):
    i = pl.program_id(0)
    nv = nv_ref[0]
    e = te_ref[i]

    def weight_copies(ex, sl):
        return (pltpu.make_async_copy(wg_ref.at[ex], sg.at[sl], sems.at[sl]),
                pltpu.make_async_copy(wu_ref.at[ex], su.at[sl], sems.at[sl]),
                pltpu.make_async_copy(wd_ref.at[ex], sd.at[sl], sems.at[sl]))

    @pl.when(i == 0)
    def _():
        slot_ref[0] = 0
        for cp in weight_copies(e, 0):
            cp.start()

    new_expert = (i < nv) & ((i == 0) | (e != te_ref[jnp.maximum(i - 1, 0)]))

    @pl.when(new_expert)
    def _():
        sl = slot_ref[0]
        for cp in weight_copies(e, sl):
            cp.wait()
        nxt = lax.while_loop(lambda t: (t < nv) & (te_ref[jnp.minimum(t, nv - 1)] == e), lambda t: t + 1, i + 1)

        @pl.when(nxt < nv)
        def _():
            for cp in weight_copies(te_ref[nxt], 1 - sl):
                cp.start()

    hm = xs_ref.shape[0] // FFN_SUB
    dw = xs_ref.shape[1] - LANES

    def swiglu_rows(nsub, cast):
        if cast:
            sl = slot_ref[0]
            wgb[...] = sg[sl].astype(BF16)
            wub[...] = su[sl].astype(BF16)
            wdb[...] = sd[sl].astype(BF16)
            slot_ref[0] = 1 - sl
        halves = tuple(slice(hm * j, hm * (j + 1)) for j in range(nsub))
        x = [_unpack_halves(xs_ref[r, 0:dw]) for r in halves]
        g = [jnp.dot(x[j], wgb[...], preferred_element_type=F32) for j in range(nsub)]
        u = [jnp.dot(x[j], wub[...], preferred_element_type=F32) for j in range(nsub)]
        h = [(g[j] * _sigmoid(g[j]) * u[j]).astype(BF16) for j in range(nsub)]
        y = [jnp.dot(h[j], wdb[...], preferred_element_type=F32) for j in range(nsub)]
        for j in range(nsub):
            wt = lax.bitcast_convert_type(xs_ref[halves[j], dw:dw + LANES], F32)
            yw = y[j] * jnp.concatenate([wt] * (2 * dw // LANES), axis=1)
            o_ref[halves[j], 0:dw] = _pack_halves(yw.astype(BF16).astype(F32))
            o_ref[halves[j], dw:dw + LANES] = xs_ref[halves[j], dw:dw + LANES]
        if nsub < FFN_SUB:
            o_ref[hm * nsub:, :] = jnp.zeros((hm * (FFN_SUB - nsub), dw + LANES), U32)

    used = vr_ref[i]
    for nsub in range(1, FFN_SUB + 1):
        lo, hi = hm * (nsub - 1), hm * nsub
        rows_here = (i < nv) & (used > lo) & ((used <= hi) if nsub < FFN_SUB else True)
        pl.when(rows_here & new_expert)(functools.partial(swiglu_rows, nsub, True))
        pl.when(rows_here & jnp.logical_not(new_expert))(functools.partial(swiglu_rows, nsub, False))


def _ffn(te, nv, vr, xs, wg, wu, wd, *, TM):
    P, XW = xs.shape
    DW = XW - LANES
    D = 2 * DW
    n_tiles = P // TM
    used_tile = lambda i, te, nv, vr: (jnp.maximum(jnp.minimum(i, nv[0] - 1), 0), 0)
    grid_spec = pltpu.PrefetchScalarGridSpec(
        num_scalar_prefetch=3,
        grid=(n_tiles,),
        in_specs=[pl.BlockSpec((TM, XW), used_tile),
                  pl.BlockSpec(memory_space=pl.ANY),
                  pl.BlockSpec(memory_space=pl.ANY),
                  pl.BlockSpec(memory_space=pl.ANY)],
        out_specs=pl.BlockSpec((TM, XW), used_tile),
        scratch_shapes=[pltpu.VMEM((D, D_EXP), BF16), pltpu.VMEM((D, D_EXP), BF16),
                        pltpu.VMEM((D_EXP, D), BF16),
                        pltpu.VMEM((2, D, D_EXP), F32), pltpu.VMEM((2, D, D_EXP), F32),
                        pltpu.VMEM((2, D_EXP, D), F32), pltpu.SMEM((1,), jnp.int32),
                        pltpu.SemaphoreType.DMA((2,))],
    )
    return pl.pallas_call(
        _ffn_kernel,
        grid_spec=grid_spec,
        out_shape=jax.ShapeDtypeStruct((P, XW), U32),
        input_output_aliases={3: 0},
        compiler_params=_cparams(),
        name="ffn",
    )(te, nv, vr, xs, wg, wu, wd)


def _combine_kernel(gd_ref, ys_ref, col_ref, x1_ref, mod_ref, g_ref, b_ref, o_ref, buf, sems, *, NT, SL):
    j = pl.program_id(0)
    slot = j % 2

    def fetch(tile, sl):
        def f(g):
            _granule_copy(ys_ref, pl.multiple_of(gd_ref[tile, g], GRAN), buf.at[sl],
                          pl.multiple_of(g * GRAN, GRAN), sems.at[sl]).start()
        _for_granules(gd_ref[tile, G_LAST], f)

    @pl.when(j == 0)
    def _():
        fetch(0, 0)

    @pl.when(j + 1 < NT)
    def _():
        fetch(j + 1, 1 - slot)

    ng = gd_ref[j, G_LAST]

    _wait_granules(ng, ys_ref, buf.at[slot], sems.at[slot], SL // GRAN)

    rows = lax.broadcasted_iota(jnp.int32, (SL, 1), 0)
    yb = _unpack_halves(jnp.where(rows < ng * GRAN, buf[slot], U32(0)))
    col = col_ref[...]
    tb = col.shape[0]
    lanes = lax.broadcasted_iota(jnp.int32, (tb, SL), 1).astype(F32)
    sel = jnp.where((lanes == col[:, 0:1]) | (lanes == col[:, 1:2]), 1.0, 0.0).astype(BF16)
    y = jnp.dot(sel, yb, preferred_element_type=F32)
    mod = mod_ref[0]
    z = ALPHA * x1_ref[...] + (1.0 + mod[5:6, :]) * y
    o_ref[...] = _layer_norm(z, g_ref[...], b_ref[...])


def _combine(gd, ys, col, x1, mod3, g, b, *, S, tb):
    N, D = x1.shape
    NT = N // tb
    tpb = S // tb
    SL = _slots_per_tile(tb)
    kern = functools.partial(_combine_kernel, NT=NT, SL=SL)
    grid_spec = pltpu.PrefetchScalarGridSpec(
        num_scalar_prefetch=1,
        grid=(NT,),
        in_specs=[pl.BlockSpec(memory_space=pl.ANY),
                  pl.BlockSpec((tb, LANES), lambda j, gd: (j, 0)),
                  pl.BlockSpec((tb, D), lambda j, gd: (j, 0)),
                  pl.BlockSpec((1, 6, D), lambda j, gd: (j // tpb, 0, 0)),
                  pl.BlockSpec((1, D), lambda j, gd: (0, 0)),
                  pl.BlockSpec((1, D), lambda j, gd: (0, 0))],
        out_specs=pl.BlockSpec((tb, D), lambda j, gd: (j, 0)),
        scratch_shapes=[pltpu.VMEM((2, SL, D // 2), U32), pltpu.SemaphoreType.DMA((2,))],
    )
    return pl.pallas_call(
        kern,
        grid_spec=grid_spec,
        out_shape=jax.ShapeDtypeStruct((N, D), F32),
        compiler_params=_cparams(),
        name="combine",
    )(gd, ys, col, x1, mod3, g, b)


def _layer(x, c, l, w_ada, b_ada, w_in, w_conv, b_conv, b_igate, b_fgate, mlstm_norm_g, w_gla_a, b_gla_a,
           gla_norm_g, w_out, ln1_g, ln1_b, w_route_group, b_route_group, w_route_expert, b_route_expert,
           w_gate, w_up, w_down, ln2_g, ln2_b):
    B, S, D = x.shape
    N = B * S
    x2 = x.reshape(N, D)
    tm_in = min(512, S)
    tm = min(256, S)
    lm = min(256, S)
    assert S % tm_in == 0 and S % tm == 0 and tm_in % lm == 0 and S % G_CHUNK == 0
    assert w_in.shape[1:] == (D, IN_TOT) and w_gate.shape[1:] == (N_EXP, D, D_EXP)

    mod3 = _ada(c, w_ada[l], b_ada[l]).reshape(B, 6, D)

    wa_pad = jnp.zeros((LANES, G_KW), F32).at[SM_A:SM_A + G_RANK].set(w_gla_a[l]).astype(BF16)
    bg = (jnp.zeros((2 * SUBLANES, 1), F32).at[0:M_HEADS, 0].set(b_igate[l])
          .at[SUBLANES:SUBLANES + M_HEADS, 0].set(b_fgate[l]))
    oa, la, g3 = _inproj(x2, mod3, jnp.swapaxes(w_in, 1, 2), w_conv[l], b_conv[l].reshape(1, -1), wa_pad,
                         b_gla_a[l].reshape(1, -1), bg, S=S, tm=tm_in, lm=lm, layer=l)

    u_tri = jnp.asarray(np.triu(np.ones((lm, lm), np.float32)))
    nb = 4 if B % 4 == 0 else (2 if B % 2 == 0 else 1)
    ts = min(512, S)
    hm = _mlstm(oa, g3, u_tri, mlstm_norm_g[l].reshape(1, -1), B=B, S=S, L=lm, nb=nb, ts=ts)
    w3_np, mk_np = _gla_consts()
    hg = _gla(oa, la, jnp.asarray(w3_np, BF16), jnp.asarray(mk_np), gla_norm_g[l].reshape(1, -1), B=B, S=S,
              nb=nb, ts=ts)

    br = (jnp.zeros((1, LANES), F32).at[0, 0:N_GROUPS].set(b_route_group[l])
          .at[0, SUBLANES:SUBLANES + N_EXP].set(b_route_expert[l]))
    x1, u2, rrow = _outproj(hm, hg, w_out[l], x2, mod3, ln1_g[l].reshape(1, -1), ln1_b[l].reshape(1, -1),
                            jnp.swapaxes(w_route_group, 1, 2), jnp.swapaxes(w_route_expert, 1, 2), br,
                            S=S, tb=tm, nh=4 if S % (4 * tm) == 0 else 1, layer=l)

    u_cnt = jnp.asarray(np.triu(np.ones((tm, tm), np.float32)), BF16)
    ltri = jnp.asarray(np.tril(np.ones((N_EXP, N_EXP), np.float32), -1))
    srow, col, gd3, meta = _route(rrow, u_cnt, ltri, TM=FFN_TM)
    gd = gd3.reshape(N // tm + 2, LANES)
    n_tiles = _ffn_tiles(N, tm)
    te, nv, vr = meta[0, :n_tiles], meta[1, 0:1], meta[2, :n_tiles]

    xs = _dispatch(gd, srow, u2, n_tiles=n_tiles, TM=FFN_TM)
    ys = _ffn(te, nv, vr, xs, w_gate[l], w_up[l], w_down[l], TM=FFN_TM)
    out = _combine(gd, ys, col, x1, mod3, ln2_g[l].reshape(1, -1), ln2_b[l].reshape(1, -1), S=S, tb=tm)
    return out.reshape(B, S, D)


def kernel(x, c, w_ada, b_ada, w_in, w_conv, b_conv, b_igate, b_fgate, mlstm_norm_g, w_gla_a, b_gla_a,
           gla_norm_g, w_out, ln1_g, ln1_b, w_route_group, b_route_group, w_route_expert, b_route_expert,
           w_gate, w_up, w_down, ln2_g, ln2_b):
    for l in range(DEPTH):
        x = _layer(x, c, l, w_ada, b_ada, w_in, w_conv, b_conv, b_igate, b_fgate, mlstm_norm_g, w_gla_a,
                   b_gla_a, gla_norm_g, w_out, ln1_g, ln1_b, w_route_group, b_route_group, w_route_expert,
                   b_route_expert, w_gate, w_up, w_down, ln2_g, ln2_b)
    return x
```

```python
import functools

import numpy as np
import jax
import jax.numpy as jnp
from jax import lax
from jax.experimental import pallas as pl
from jax.experimental.pallas import tpu as pltpu

F32 = jnp.float32
BF16 = jnp.bfloat16
U32 = jnp.uint32
HIGHEST = lax.Precision.HIGHEST

DEPTH = 1
M_HEADS = 4
M_HD = 128
M_W = M_HEADS * M_HD
CONV_W = 4
G_HEADS = 4
G_DK = 64
G_DV = 128
G_W = G_HEADS * G_DV
G_KW = G_HEADS * G_DK
G_RANK = 16
G_TAU = 16.0
G_CHUNK = 64
N_GROUPS = 4
E_PER_G = 8
N_EXP = N_GROUPS * E_PER_G
D_EXP = 512
ALPHA = (2 * DEPTH) ** 0.25
LN_EPS = 1e-5

LANES = 128
SUBLANES = 8
VMEM_LIMIT = 48 * 1024 * 1024

C_QK = 0
C_VO = 1024
C_GQK = 2048
C_GV = 2560
C_GG = 3072
C_SMALL = 3584
C_TOT = 3712
SM_I, SM_F, SM_A = 0, 8, 16
IN_GATES = 4 * M_W
IN_G = IN_GATES + 2 * M_HEADS
IN_GA = IN_G + 2 * G_KW + 2 * G_W
IN_TOT = IN_GA + G_RANK

FFN_TM = 512
FFN_SUB = 2
GRAN = SUBLANES
G_LAST = LANES - 1


def _cparams(n_axes=1):
    return pltpu.CompilerParams(dimension_semantics=("arbitrary",) * n_axes,
                                vmem_limit_bytes=VMEM_LIMIT)


def _sigmoid(x):
    return 1.0 / (1.0 + jnp.exp(-x))


def _log_sigmoid(x):
    return jnp.minimum(x, 0.0) - jnp.log(1.0 + jnp.exp(-jnp.abs(x)))


def _ada_kernel(c_ref, w_ref, b_ref, o_ref):
    c = c_ref[...]
    ca = (c * _sigmoid(c)).astype(BF16)
    o_ref[...] = jnp.dot(ca, w_ref[...].astype(BF16), preferred_element_type=F32) + b_ref[...]


def _ada(c, w, b):
    B, D = c.shape
    n_out = w.shape[1]
    tn = 1024
    return pl.pallas_call(
        _ada_kernel,
        grid=(n_out // tn,),
        in_specs=[pl.BlockSpec((B, D), lambda j: (0, 0)),
                  pl.BlockSpec((D, tn), lambda j: (0, j)),
                  pl.BlockSpec((1, tn), lambda j: (0, j))],
        out_specs=pl.BlockSpec((B, tn), lambda j: (0, j)),
        out_shape=jax.ShapeDtypeStruct((B, n_out), F32),
        compiler_params=_cparams(),
        name="ada",
    )(c, w, b.reshape(1, n_out))


def _inproj_kernel(x_ref, mod_ref, win_ref, wc_ref, bc_ref, wa_ref, ba_ref, bg_ref,
                   oa_ref, la_ref, g_ref, halo_ref, w_ref, *, tm, tpb, lm):
    i = pl.program_id(0)

    @pl.when(i == 0)
    def _():
        rc = 2 * LANES
        for r in range(0, IN_GATES, rc):
            w_ref[:, r:r + rc] = win_ref[0, r:r + rc, :].T.astype(BF16)
        for r in range(0, C_SMALL - C_GQK, rc):
            w_ref[:, C_GQK + r:C_GQK + r + rc] = win_ref[0, IN_G + r:IN_G + r + rc, :].T.astype(BF16)
        gates = win_ref[0, IN_GATES:IN_G, :]
        z = lambda n: jnp.zeros((n, gates.shape[1]), F32)
        small = jnp.concatenate([gates[0:M_HEADS], z(SM_F - M_HEADS), gates[M_HEADS:2 * M_HEADS],
                                 z(SM_A - SM_F - M_HEADS), win_ref[0, IN_GA:IN_TOT, :],
                                 z(LANES - SM_A - G_RANK)], axis=0)
        w_ref[:, C_SMALL:C_TOT] = small.T.astype(BF16)

    @pl.when(i % tpb == 0)
    def _():
        halo_ref[0:SUBLANES, :] = jnp.zeros((SUBLANES, halo_ref.shape[1]), F32)

    mod = mod_ref[0]
    u = (x_ref[...] * (1.0 + mod[1:2, :]) + mod[0:1, :]).astype(BF16)

    def proj(c0, c1):
        return jnp.dot(u, w_ref[:, c0:c1], preferred_element_type=F32)

    p = proj(C_QK, C_QK + 2 * M_W)
    halo_ref[SUBLANES:SUBLANES + tm, :] = p
    acc = bc_ref[...] + wc_ref[CONV_W - 1:CONV_W, :] * p
    for j in range(CONV_W - 1):
        acc = acc + wc_ref[j:j + 1, :] * halo_ref[pl.ds(SUBLANES - (CONV_W - 1) + j, tm), :]
    halo_ref[0:SUBLANES, :] = p[tm - SUBLANES:, :]
    qk = acc * _sigmoid(acc)
    oa_ref[:, C_QK:C_QK + M_W] = qk[:, :M_W].astype(BF16)
    oa_ref[:, C_QK + M_W:C_QK + 2 * M_W] = (qk[:, M_W:] * (M_HD ** -0.5)).astype(BF16)

    ps = proj(C_SMALL, C_TOT)
    la = jnp.dot(ps.astype(BF16), wa_ref[...], preferred_element_type=F32) + ba_ref[...]
    la_ref[...] = _log_sigmoid(la) * (1.0 / G_TAU)
    pt = ps.T
    gi = pt[SM_I:SM_I + SUBLANES, :] + bg_ref[0:SUBLANES, :]
    gf = _log_sigmoid(pt[SM_F:SM_F + SUBLANES, :] + bg_ref[SUBLANES:2 * SUBLANES, :])
    for j in range(tm // lm):
        g_ref[j, 0:SUBLANES, :] = gi[:, j * lm:(j + 1) * lm]
        g_ref[j, SUBLANES:2 * SUBLANES, :] = gf[:, j * lm:(j + 1) * lm]

    p = proj(C_VO, C_VO + 2 * M_W)
    oa_ref[:, C_VO:C_VO + 2 * M_W] = p.astype(BF16)

    p = proj(C_GQK, C_GQK + G_KW)
    oa_ref[:, C_GQK:C_GQK + G_KW] = (p * (G_DK ** -0.5)).astype(BF16)
    p = proj(C_GQK + G_KW, C_SMALL)
    oa_ref[:, C_GQK + G_KW:C_SMALL] = p.astype(BF16)


def _inproj(x2, mod3, w_in, w_conv, b_conv, wa_pad, b_gla, bg, *, S, tm, lm, layer):
    N, D = x2.shape
    tpb = S // tm
    kern = functools.partial(_inproj_kernel, tm=tm, tpb=tpb, lm=lm)
    return pl.pallas_call(
        kern,
        grid=(N // tm,),
        in_specs=[pl.BlockSpec((tm, D), lambda i: (i, 0)),
                  pl.BlockSpec((1, 6, D), lambda i: (i // tpb, 0, 0)),
                  pl.BlockSpec((1, IN_TOT, D), lambda i: (layer, 0, 0), pipeline_mode=pl.Buffered(1)),
                  pl.BlockSpec((CONV_W, 2 * M_W), lambda i: (0, 0)),
                  pl.BlockSpec((1, 2 * M_W), lambda i: (0, 0)),
                  pl.BlockSpec((LANES, G_KW), lambda i: (0, 0)),
                  pl.BlockSpec((1, G_KW), lambda i: (0, 0)),
                  pl.BlockSpec((2 * SUBLANES, 1), lambda i: (0, 0))],
        out_specs=[pl.BlockSpec((tm, C_SMALL), lambda i: (i, 0)),
                   pl.BlockSpec((tm, G_KW), lambda i: (i, 0)),
                   pl.BlockSpec((tm // lm, 2 * SUBLANES, lm), lambda i: (i, 0, 0))],
        out_shape=[jax.ShapeDtypeStruct((N, C_SMALL), BF16),
                   jax.ShapeDtypeStruct((N, G_KW), F32),
                   jax.ShapeDtypeStruct((N // lm, 2 * SUBLANES, lm), F32)],
        scratch_shapes=[pltpu.VMEM((SUBLANES + tm, 2 * M_W), F32), pltpu.VMEM((D, C_TOT), BF16)],
        compiler_params=_cparams(),
        name="inproj",
    )(x2, mod3, w_in, w_conv, b_conv, wa_pad, b_gla, bg)


def _mlstm_sel():
    sel = np.zeros((2 * LANES, 2 * M_HEADS * M_HD), np.float32)
    for j in range(2 * M_HEADS):
        src = (SUBLANES if j < M_HEADS else 3 * SUBLANES) + j % M_HEADS
        sel[src, M_HD * j:M_HD * (j + 1)] = 1.0
        sel[LANES + src, M_HD * j:M_HD * (j + 1)] = 1.0
    return sel


def _mlstm_kernel(qk_ref, vo_ref, g_ref, u_ref, gain_ref, sel_ref, out_ref, c_ref, zt_ref, a_ref, dec_ref, m_ref,
                  *, L, NC, nb):
    @pl.when(pl.program_id(1) == 0)
    def _():
        c_ref[...] = jnp.zeros_like(c_ref)
        m_ref[...] = jnp.zeros_like(m_ref)

    tril = (lax.broadcasted_iota(jnp.int32, (L, L), 0) >= lax.broadcasted_iota(jnp.int32, (L, L), 1))
    ones_v = jnp.ones((L, M_HD), BF16)
    zpad = jnp.zeros((LANES - 4 * SUBLANES, L), F32)
    zgroup = lax.broadcasted_iota(jnp.int32, (L, LANES), 1) // SUBLANES
    factor_cols = (zgroup == 1) | (zgroup == 3)

    order = [(bi, c) for bi in range(nb) for c in range(NC)]
    f_all = jnp.concatenate([g_ref[bi, c, SUBLANES:2 * SUBLANES, :] for bi, c in order], axis=0)
    i_all = jnp.concatenate([g_ref[bi, c, 0:SUBLANES, :] for bi, c in order], axis=0)
    b_all = jnp.dot(f_all, u_ref[...], preferred_element_type=F32, precision=HIGHEST)
    a_all = i_all - b_all
    lane_all = lax.broadcasted_iota(jnp.int32, a_all.shape, 1)
    g_all = a_all
    s = 1
    while s < L:
        g_all = jnp.maximum(g_all, jnp.where(lane_all >= s, pltpu.roll(g_all, s, 1), -jnp.inf))
        s *= 2
    for bi in range(nb):
        m_prev = m_ref[bi][:, 0:1]
        for c in range(NC):
            ci = bi * NC + c
            r8 = slice(SUBLANES * ci, SUBLANES * (ci + 1))
            a, b = a_all[r8], b_all[r8]
            a_ref[ci] = a
            M = jnp.maximum(g_all[r8], m_prev)
            ML = M[:, L - 1:L]
            Z = jnp.concatenate([M, jnp.exp(m_prev - M), jnp.exp(-(b + M)), jnp.exp(a - ML), zpad],
                                axis=0)
            zt_ref[ci] = Z.T
            dec_ref[ci] = jnp.broadcast_to(jnp.exp(m_prev - ML), (SUBLANES, 2 * M_HD))
            m_prev = b[:, L - 1:L] + ML
        m_ref[bi] = jnp.broadcast_to(m_prev, (SUBLANES, LANES))

    chains = [(bi, h) for bi in range(nb) for h in range(M_HEADS)]
    nt = (((1,), (1,)), ((), ()))
    tn = (((0,), (0,)), ((), ()))

    def chunk(c, carry):
        rows = pl.ds(pl.multiple_of(c * L, L), L)
        Zt = [zt_ref[bi * NC + c] for bi in range(nb)]
        a = [a_ref[bi * NC + c] for bi in range(nb)]
        dec = [dec_ref[bi * NC + c] for bi in range(nb)]
        hs = [slice(h * M_HD, (h + 1) * M_HD) for h in range(M_HEADS)]
        hs2 = [slice(M_W + h * M_HD, M_W + (h + 1) * M_HD) for h in range(M_HEADS)]
        q = [qk_ref[bi, rows, hs[h]] for bi, h in chains]
        k = [qk_ref[bi, rows, hs2[h]] for bi, h in chains]
        vext = [jnp.concatenate([vo_ref[bi, rows, hs[h]], ones_v], axis=1) for bi, h in chains]
        cst = [c_ref[bi * M_HEADS + h] for bi, h in chains]
        n = range(len(chains))
        sc = [lax.dot_general(q[i], k[i], nt, preferred_element_type=F32) for i in n]
        qc = [jnp.dot(q[i], cst[i].astype(BF16), preferred_element_type=F32) for i in n]
        pm = [(sc[i] * jnp.exp(jnp.where(tril, a[bi][h:h + 1, :] - Zt[bi][:, h:h + 1], -jnp.inf))).astype(BF16)
              for i, (bi, h) in enumerate(chains)]
        pv = [jnp.dot(pm[i], vext[i], preferred_element_type=F32) for i in n]
        rep = []
        for bi in range(nb):
            zf = jnp.where(factor_cols, Zt[bi], 0.0)
            zh = zf.astype(BF16)
            zl = (zf - zh.astype(F32)).astype(BF16)
            rep.append(jnp.dot(jnp.concatenate([zh, zl], axis=1), sel_ref[...], preferred_element_type=F32))
        e_inter = [rep[bi][:, M_HD * h:M_HD * (h + 1)] for bi, h in chains]
        w_state = [rep[bi][:, M_HD * (M_HEADS + h):M_HD * (M_HEADS + h + 1)] for bi, h in chains]
        kw = [(w_state[i] * k[i].astype(F32)).astype(BF16) for i in n]
        upd = [lax.dot_general(kw[i], vext[i], tn, preferred_element_type=F32) for i in n]
        for i, (bi, h) in enumerate(chains):
            c_ref[bi * M_HEADS + h] = dec[bi][h:h + 1, :] * cst[i] + upd[i]
            nd = pv[i] + jnp.concatenate([e_inter[i], e_inter[i]], axis=1) * qc[i]
            hh = nd[:, :M_HD] / jnp.maximum(jnp.abs(nd[:, M_HD:]),
                                            Zt[bi][:, 2 * SUBLANES + h:2 * SUBLANES + h + 1])
            hh = _sigmoid(vo_ref[bi, rows, hs2[h]].astype(F32)) * hh
            hn = hh * lax.rsqrt(jnp.mean(hh * hh, axis=-1, keepdims=True) + LN_EPS)
            out_ref[bi, rows, hs[h]] = (hn * gain_ref[:, hs[h]]).astype(BF16)
        return carry

    lax.fori_loop(0, NC, chunk, 0, unroll=True)


def _mlstm(oa, g3, u_tri, gain, *, B, S, L, nb, ts):
    N = oa.shape[0]
    NC = ts // L
    oa3 = oa.reshape(B, S, oa.shape[1])
    g4 = g3.reshape(B, S // L, 2 * SUBLANES, L)
    sel = jnp.asarray(_mlstm_sel(), BF16)
    kern = functools.partial(_mlstm_kernel, L=L, NC=NC, nb=nb)
    out = pl.pallas_call(
        kern,
        grid=(B // nb, S // ts),
        in_specs=[pl.BlockSpec((nb, ts, 2 * M_W), lambda b, t: (b, t, C_QK // (2 * M_W))),
                  pl.BlockSpec((nb, ts, 2 * M_W), lambda b, t: (b, t, C_VO // (2 * M_W))),
                  pl.BlockSpec((nb, NC, 2 * SUBLANES, L), lambda b, t: (b, t, 0, 0)),
                  pl.BlockSpec((L, L), lambda b, t: (0, 0)),
                  pl.BlockSpec((1, M_W), lambda b, t: (0, 0)),
                  pl.BlockSpec(sel.shape, lambda b, t: (0, 0))],
        out_specs=pl.BlockSpec((nb, ts, M_W), lambda b, t: (b, t, 0)),
        out_shape=jax.ShapeDtypeStruct((B, S, M_W), BF16),
        scratch_shapes=[pltpu.VMEM((nb * M_HEADS, M_HD, 2 * M_HD), F32),
                        pltpu.VMEM((nb * NC, L, LANES), F32),
                        pltpu.VMEM((nb * NC, SUBLANES, L), F32),
                        pltpu.VMEM((nb * NC, SUBLANES, 2 * M_HD), F32),
                        pltpu.VMEM((nb, SUBLANES, LANES), F32)],
        compiler_params=_cparams(2),
        name="mlstm",
    )(oa3, oa3, g4, u_tri, gain, sel)
    return out.reshape(N, M_W)


_G_LEVELS = 6
_G_XROW = 2 * G_CHUNK + SUBLANES


def _gla_consts():
    L = G_CHUNK
    t = np.arange(L)
    blocks = [(t[None, :] <= t[:, None]).astype(np.float32),
              (t[None, :] > t[:, None]).astype(np.float32),
              np.ones((SUBLANES, L), np.float32)]
    masks = [np.eye(L, dtype=np.float32)]
    m = 1
    while m < L:
        wl = np.zeros((L, L), np.float32)
        for r in range(L):
            r0 = (r // (2 * m)) * 2 * m + m
            if r % (2 * m) >= m:
                wl[r, r0:r + 1] = 1.0
            else:
                wl[r, r + 1:r0] = 1.0
        blocks.append(wl)
        tt, ss = t[:, None], t[None, :]
        masks.append(((tt // (2 * m) == ss // (2 * m)) & (tt % (2 * m) >= m)
                      & (ss % (2 * m) < m)).astype(np.float32))
        m *= 2
    w = np.concatenate(blocks, axis=0)
    w3 = np.concatenate([w, w, w], axis=1)
    mk = np.stack([np.concatenate([x] * G_HEADS, axis=0) for x in masks])
    return w3, mk


def _gla_kernel(qk_ref, v_ref, gg_ref, la_ref, w3_ref, mk_ref, gain_ref, out_ref, st_ref, *, NC, nb):
    L = G_CHUNK

    @pl.when(pl.program_id(1) == 0)
    def _():
        st_ref[...] = jnp.zeros_like(st_ref)

    lane_head = lax.broadcasted_iota(jnp.int32, (L, G_KW), 1) // G_DK
    br = lax.broadcasted_iota(jnp.int32, (2 * G_DV, LANES), 0) < G_DV
    bl = lax.broadcasted_iota(jnp.int32, (2 * G_DV, LANES), 1) < G_DK
    bmask = br == bl
    nt = (((1,), (1,)), ((), ()))
    tn = (((0,), (0,)), ((), ()))

    def chunk(c, carry):
        rows = pl.ds(pl.multiple_of(c * L, L), L)
        X, q, k = [], [], []
        for bi in range(nb):
            la = la_ref[bi, rows, :]
            hi = la.astype(BF16)
            r1 = la - hi.astype(F32)
            mid = r1.astype(BF16)
            lo = (r1 - mid.astype(F32)).astype(BF16)
            stk = jnp.concatenate([hi, mid, lo], axis=0)
            X.append(jnp.exp(jnp.dot(w3_ref[...], stk, preferred_element_type=F32)))
            q.append(qk_ref[bi, rows, 0:G_KW].astype(F32))
            k.append(qk_ref[bi, rows, G_KW:2 * G_KW].astype(F32))

        sc = [[None] * (_G_LEVELS + 1) for _ in range(nb)]
        for lev in range(_G_LEVELS + 1):
            for bi in range(nb):
                if lev == 0:
                    qt, kt = q[bi], k[bi]
                else:
                    xl = X[bi][_G_XROW + L * (lev - 1):_G_XROW + L * lev, :]
                    qt, kt = q[bi] * xl, k[bi] * xl
                q4 = jnp.concatenate([jnp.where(lane_head == h, qt, 0.0) for h in range(G_HEADS)],
                                     axis=0).astype(BF16)
                sc[bi][lev] = lax.dot_general(q4, kt.astype(BF16), nt, preferred_element_type=F32)
        Ab = []
        for bi in range(nb):
            A = sc[bi][0] * mk_ref[0]
            for lev in range(1, _G_LEVELS + 1):
                A = A + sc[bi][lev] * mk_ref[lev]
            Ab.append(A.astype(BF16))

        for bi in range(nb):
            gg = gg_ref[bi, rows, :].astype(F32)
            gate = gg * _sigmoid(gg)
            for p in range(2):
                ls = slice(LANES * p, LANES * (p + 1))
                vp = v_ref[bi, rows, 2 * G_DV * p:2 * G_DV * (p + 1)]
                oi = [jnp.dot(Ab[bi][L * (2 * p + hh):L * (2 * p + hh + 1)],
                              vp[:, G_DV * hh:G_DV * (hh + 1)], preferred_element_type=F32)
                      for hh in range(2)]
                st = st_ref[bi, p]
                qc = (q[bi][:, ls] * X[bi][0:L, ls]).astype(BF16)
                o_inter = lax.dot_general(qc, st.astype(BF16), nt, preferred_element_type=F32)
                kc = (k[bi][:, ls] * X[bi][L:2 * L, ls]).astype(BF16)
                upd = lax.dot_general(vp, kc, tn, preferred_element_type=F32)
                dec = X[bi][2 * L:2 * L + 1, ls]
                st_ref[bi, p] = jnp.where(bmask, dec * st + upd, 0.0)
                for hh in range(2):
                    o = o_inter[:, G_DV * hh:G_DV * (hh + 1)] + oi[hh]
                    hn = o * lax.rsqrt(jnp.mean(o * o, axis=-1, keepdims=True) + LN_EPS)
                    hs = slice(G_DV * (2 * p + hh), G_DV * (2 * p + hh + 1))
                    out_ref[bi, rows, hs] = (hn * gain_ref[:, hs] * gate[:, hs]).astype(BF16)
        return carry

    lax.fori_loop(0, NC, chunk, 0, unroll=2 if NC % 2 == 0 else 1)


def _gla(oa, la, w3, mk, gain, *, B, S, nb, ts):
    N = oa.shape[0]
    oa3 = oa.reshape(B, S, oa.shape[1])
    la3 = la.reshape(B, S, G_KW)
    kern = functools.partial(_gla_kernel, NC=ts // G_CHUNK, nb=nb)
    out = pl.pallas_call(
        kern,
        grid=(B // nb, S // ts),
        in_specs=[pl.BlockSpec((nb, ts, 2 * G_KW), lambda b, t: (b, t, C_GQK // (2 * G_KW))),
                  pl.BlockSpec((nb, ts, G_W), lambda b, t: (b, t, C_GV // G_W)),
                  pl.BlockSpec((nb, ts, G_W), lambda b, t: (b, t, C_GG // G_W)),
                  pl.BlockSpec((nb, ts, G_KW), lambda b, t: (b, t, 0)),
                  pl.BlockSpec(w3.shape, lambda b, t: (0, 0)),
                  pl.BlockSpec(mk.shape, lambda b, t: (0, 0, 0)),
                  pl.BlockSpec((1, G_W), lambda b, t: (0, 0))],
        out_specs=pl.BlockSpec((nb, ts, G_W), lambda b, t: (b, t, 0)),
        out_shape=jax.ShapeDtypeStruct((B, S, G_W), BF16),
        scratch_shapes=[pltpu.VMEM((nb, 2, 2 * G_DV, LANES), F32)],
        compiler_params=_cparams(2),
        name="gla",
    )(oa3, oa3, oa3, la3, w3, mk, gain)
    return out.reshape(N, G_W)


def _layer_norm(z, g, b):
    mu = jnp.mean(z, axis=-1, keepdims=True)
    zc = z - mu
    var = jnp.mean(zc * zc, axis=-1, keepdims=True)
    return zc * lax.rsqrt(var + LN_EPS) * g + b


def _outproj_kernel(hm_ref, hg_ref, wf_ref, x_ref, mod_ref, g_ref, b_ref, wrg_ref, wre_ref, br_ref,
                    x1_ref, u2_ref, rrow_ref, w_ref, wr_ref, *, tb, nh):
    @pl.when(pl.program_id(0) == 0)
    def _():
        w_ref[...] = wf_ref[...].astype(BF16)
        z = lambda n: jnp.zeros((n, wrg_ref.shape[2]), F32)
        wt = jnp.concatenate([wrg_ref[0], z(SUBLANES - N_GROUPS), wre_ref[0],
                              z(LANES - SUBLANES - N_EXP)], axis=0).T
        hi = wt.astype(BF16)
        wr_ref[:, 0:LANES] = hi
        wr_ref[:, LANES:2 * LANES] = (wt - hi.astype(F32)).astype(BF16)

    mod = mod_ref[0]
    blocks = [slice(tb * j, tb * (j + 1)) for j in range(nh)]
    y = [jnp.dot(hm_ref[r, :], w_ref[0:M_W, :], preferred_element_type=F32)
         + jnp.dot(hg_ref[r, :], w_ref[M_W:M_W + G_W, :], preferred_element_type=F32) for r in blocks]
    u2 = []
    for j, r in enumerate(blocks):
        z = ALPHA * x_ref[r, :] + (1.0 + mod[2:3, :]) * y[j]
        x1 = _layer_norm(z, g_ref[...], b_ref[...])
        x1_ref[r, :] = x1
        u2.append(x1 * (1.0 + mod[4:5, :]) + mod[3:4, :])
        u2_ref[r, :] = u2[j].astype(BF16)

    u2h = [u.astype(BF16) for u in u2]
    u2l = [(u2[j] - u2h[j].astype(F32)).astype(BF16) for j in range(nh)]
    lh = [jnp.dot(u, wr_ref[...], preferred_element_type=F32) for u in u2h]
    ll = [jnp.dot(u, wr_ref[:, 0:LANES], preferred_element_type=F32) for u in u2l]
    for j in range(nh):
        logits = lh[j][:, 0:LANES] + lh[j][:, LANES:2 * LANES] + ll[j] + br_ref[...]
        rrow_ref[j] = _route_select(logits.T, tb)


def _route_select(lt, tm):
    row = lax.broadcasted_iota(jnp.int32, (SUBLANES, tm), 0)
    gl = jnp.where(row < N_GROUPS, lt[0:SUBLANES, :], -jnp.inf)
    gmax = jnp.max(gl, axis=0, keepdims=True)
    gsel = jnp.min(jnp.where(gl == gmax, row, SUBLANES), axis=0, keepdims=True)
    pg = 1.0 / jnp.sum(jnp.exp(gl - gmax), axis=0, keepdims=True)
    ein = jnp.zeros((SUBLANES, tm), F32)
    for g in range(N_GROUPS):
        ein = jnp.where(gsel == g, lt[SUBLANES * (g + 1):SUBLANES * (g + 2), :], ein)
    v1 = jnp.max(ein, axis=0, keepdims=True)
    i1 = jnp.min(jnp.where(ein == v1, row, SUBLANES), axis=0, keepdims=True)
    rest = jnp.where(row == i1, -jnp.inf, ein)
    v2 = jnp.max(rest, axis=0, keepdims=True)
    i2 = jnp.min(jnp.where(rest == v2, row, SUBLANES), axis=0, keepdims=True)
    t2 = jnp.exp(v2 - v1)
    p1 = 1.0 / (1.0 + t2)
    e0 = (gsel * E_PER_G + i1).astype(F32)
    e1 = (gsel * E_PER_G + i2).astype(F32)
    return jnp.concatenate([e0, e1, pg * p1, pg * (t2 * p1), jnp.zeros((SUBLANES - 4, tm), F32)], axis=0)


def _outproj(hm, hg, w_out, x2, mod3, g, b, wrg_t, wre_t, br, *, S, tb, nh, layer):
    N, D = x2.shape
    tm = tb * nh
    tpb = S // tm
    kern = functools.partial(_outproj_kernel, tb=tb, nh=nh)
    return pl.pallas_call(
        kern,
        grid=(N // tm,),
        in_specs=[pl.BlockSpec((tm, M_W), lambda i: (i, 0)),
                  pl.BlockSpec((tm, G_W), lambda i: (i, 0)),
                  pl.BlockSpec((M_W + G_W, D), lambda i: (0, 0), pipeline_mode=pl.Buffered(1)),
                  pl.BlockSpec((tm, D), lambda i: (i, 0)),
                  pl.BlockSpec((1, 6, D), lambda i: (i // tpb, 0, 0)),
                  pl.BlockSpec((1, D), lambda i: (0, 0)),
                  pl.BlockSpec((1, D), lambda i: (0, 0)),
                  pl.BlockSpec((1, N_GROUPS, D), lambda i: (layer, 0, 0)),
                  pl.BlockSpec((1, N_EXP, D), lambda i: (layer, 0, 0)),
                  pl.BlockSpec((1, LANES), lambda i: (0, 0))],
        out_specs=[pl.BlockSpec((tm, D), lambda i: (i, 0)),
                   pl.BlockSpec((tm, D), lambda i: (i, 0)),
                   pl.BlockSpec((nh, SUBLANES, tb), lambda i: (i, 0, 0))],
        out_shape=[jax.ShapeDtypeStruct((N, D), F32),
                   jax.ShapeDtypeStruct((N, D), BF16),
                   jax.ShapeDtypeStruct((N // tb, SUBLANES, tb), F32)],
        scratch_shapes=[pltpu.VMEM((M_W + G_W, D), BF16), pltpu.VMEM((D, 2 * LANES), BF16)],
        compiler_params=_cparams(),
        name="outproj",
    )(hm, hg, w_out, x2, mod3, g, b, wrg_t, wre_t, br)


def _slots_per_tile(tb):
    worst = 2 * tb + N_EXP * (GRAN - 1)
    return -(-worst // LANES) * LANES


def _ffn_tiles(n_tok, tb):
    worst_rows = 2 * n_tok + (n_tok // tb) * N_EXP * (GRAN - 1)
    return -(-worst_rows // FFN_TM) + N_EXP


def _route_kernel(rr_ref, u_ref, lt_ref, srow_ref, col_ref, gd_ref, meta_ref, mg_ref, part_ref,
                  *, NT, tb, TM):
    iota_e = lax.broadcasted_iota(jnp.int32, (N_EXP, tb), 0).astype(F32)
    glane = lax.broadcasted_iota(jnp.int32, (N_EXP, LANES), 1).astype(F32)
    ltri = lt_ref[...]

    def prefix_e(col):
        return jnp.dot(ltri, jnp.broadcast_to(col, (N_EXP, LANES)),
                       preferred_element_type=F32, precision=HIGHEST)[:, 0:1]

    def p1(j, run8):
        r = rr_ref[j]
        oh0 = jnp.where(iota_e == r[0:1, :], 1.0, 0.0)
        oh1 = jnp.where(iota_e == r[1:2, :], 1.0, 0.0)
        cum0 = jnp.dot(oh0.astype(BF16), u_ref[...], preferred_element_type=F32)
        cum1 = jnp.dot(oh1.astype(BF16), u_ref[...], preferred_element_type=F32)
        c0 = jnp.sum(oh0, axis=1, keepdims=True)
        n8 = jnp.floor((c0 + jnp.sum(oh1, axis=1, keepdims=True) + (GRAN - 1.0)) * (1.0 / GRAN))
        lo8 = prefix_e(n8)
        s0 = jnp.sum(oh0 * (GRAN * lo8 + cum0 - 1.0), axis=0, keepdims=True)
        s1 = jnp.sum(oh1 * (GRAN * lo8 + c0 + cum1 - 1.0), axis=0, keepdims=True)
        info = jnp.concatenate([s0, s1, r[2:4, :], jnp.zeros((SUBLANES - 4, tb), F32)], axis=0)
        srow_ref[j] = info
        col_ref[pl.ds(pl.multiple_of(j * tb, tb), tb), :] = jnp.concatenate(
            [info, jnp.zeros((LANES - SUBLANES, tb), F32)], axis=0).T
        mg = jnp.where((lo8 <= glane) & (glane < lo8 + n8), 1.0, 0.0)
        mg_ref[j] = mg
        part = jnp.sum(mg * (run8 + glane - lo8), axis=0, keepdims=True)
        gcnt = jnp.broadcast_to(jnp.sum(n8, axis=0, keepdims=True), (1, LANES))
        part_ref[j] = jnp.concatenate([part, gcnt, jnp.zeros((SUBLANES - 2, LANES), F32)], axis=0)
        return run8 + n8

    tot8 = lax.fori_loop(0, NT, p1, jnp.zeros((N_EXP, 1), F32), unroll=8 if NT % 8 == 0 else 1)
    seg_t = jnp.floor((tot8 * GRAN + (TM - 1.0)) * (1.0 / TM))
    base_t = prefix_e(seg_t)
    base8 = base_t * (TM // GRAN)
    lane1 = lax.broadcasted_iota(jnp.int32, (1, LANES), 1)

    def p2(j, carry):
        pr = part_ref[j]
        dst = (pr[0:1, :] + jnp.sum(mg_ref[j] * base8, axis=0, keepdims=True)) * GRAN
        gd_ref[j] = jnp.where(lane1 == G_LAST, pr[1:2, :], dst).astype(jnp.int32)
        return carry

    lax.fori_loop(0, NT, p2, 0, unroll=8 if NT % 8 == 0 else 1)
    eye = jnp.where(glane == lax.broadcasted_iota(jnp.int32, (N_EXP, LANES), 0).astype(F32), 1.0, 0.0)
    tail_row = jnp.sum(eye * ((base8 + tot8) * GRAN), axis=0, keepdims=True)
    tail_n8 = jnp.sum(eye * (seg_t * (TM // GRAN) - tot8), axis=0, keepdims=True)
    nv_l = jnp.broadcast_to(jnp.sum(seg_t, axis=0, keepdims=True), (1, LANES))
    gd_ref[NT] = jnp.where(lane1 == G_LAST, nv_l, tail_row).astype(jnp.int32)
    gd_ref[NT + 1] = tail_n8.astype(jnp.int32)
    ti = lax.broadcasted_iota(jnp.int32, (N_EXP, tb), 1).astype(F32)
    te = jnp.sum(jnp.where(base_t <= ti, 1.0, 0.0), axis=0, keepdims=True) - 1.0
    nv = jnp.broadcast_to(jnp.sum(seg_t, axis=0, keepdims=True), (1, tb))
    own = jnp.where((base_t <= ti) & (ti < base_t + seg_t), 1.0, 0.0)
    vr = jnp.sum(own * jnp.clip(tot8 * GRAN - (ti - base_t) * TM, 0.0, TM), axis=0, keepdims=True)
    meta_ref[...] = jnp.concatenate([te, nv, vr, jnp.zeros((SUBLANES - 3, tb), F32)],
                                    axis=0).astype(jnp.int32)


def _route(rrow, u_cnt, ltri, *, TM):
    NT, _, tb = rrow.shape
    kern = functools.partial(_route_kernel, NT=NT, tb=tb, TM=TM)
    full3 = lambda i: (0, 0, 0)
    return pl.pallas_call(
        kern,
        grid=(1,),
        in_specs=[pl.BlockSpec((NT, SUBLANES, tb), full3),
                  pl.BlockSpec((tb, tb), lambda i: (0, 0)),
                  pl.BlockSpec((N_EXP, N_EXP), lambda i: (0, 0))],
        out_specs=[pl.BlockSpec((NT, SUBLANES, tb), full3),
                   pl.BlockSpec((NT * tb, LANES), lambda i: (0, 0)),
                   pl.BlockSpec((NT + 2, 1, LANES), full3),
                   pl.BlockSpec((SUBLANES, tb), lambda i: (0, 0))],
        out_shape=[jax.ShapeDtypeStruct((NT, SUBLANES, tb), F32),
                   jax.ShapeDtypeStruct((NT * tb, LANES), F32),
                   jax.ShapeDtypeStruct((NT + 2, 1, LANES), jnp.int32),
                   jax.ShapeDtypeStruct((SUBLANES, tb), jnp.int32)],
        scratch_shapes=[pltpu.VMEM((NT, N_EXP, LANES), F32), pltpu.VMEM((NT, SUBLANES, LANES), F32)],
        compiler_params=_cparams(),
        name="route",
    )(rrow, u_cnt, ltri)


_HI_MASK = 0xFFFF0000


def _pack_halves(x):
    c = x.shape[1] // 2
    lo = lax.bitcast_convert_type(x[:, :c], U32)
    hi = lax.bitcast_convert_type(x[:, c:], U32)
    return (lo >> 16) | (hi & U32(_HI_MASK))


def _unpack_halves(w):
    lo = lax.bitcast_convert_type(w << 16, F32)
    hi = lax.bitcast_convert_type(w & U32(_HI_MASK), F32)
    return jnp.concatenate([lo, hi], axis=1).astype(BF16)


def _granule_copy(src_ref, src_row, dst_ref, dst_row, sem, n=1):
    cols = pl.ds(0, min(src_ref.shape[-1], dst_ref.shape[-1]))
    return pltpu.make_async_copy(src_ref.at[pl.ds(src_row, n * GRAN), cols],
                                 dst_ref.at[pl.ds(dst_row, n * GRAN), cols], sem)


def _for_granules(n, body, unroll=8):
    def blk(i, carry):
        for t in range(unroll):
            body(i * unroll + t)
        return carry

    def one(g, carry):
        body(g)
        return carry

    nblk = n // unroll
    lax.fori_loop(0, nblk, blk, 0)
    lax.fori_loop(nblk * unroll, n, one, 0)


def _wait_granules(n, src_ref, dst_ref, sem, n_max):
    b = 1
    while b <= n_max:
        @pl.when((n & b) != 0)
        def _(b=b):
            _granule_copy(src_ref, 0, dst_ref, 0, sem, n=b).wait()
        b *= 2


def _dispatch_kernel(gd_ref, srow_ref, u_ref, xs_ref, buf, zbuf, sems, *, NT, SL, TM, n_tiles):
    j = pl.program_id(0)
    slot = j % 2
    zsem = sems.at[2]

    def drain(tile, sl):
        _wait_granules(gd_ref[tile, G_LAST], buf.at[sl], xs_ref, sems.at[sl], SL // GRAN)

    def tile_fill(t):
        return pltpu.make_async_copy(zbuf, xs_ref.at[pl.ds(pl.multiple_of(t * TM, TM), TM), :], zsem)

    def zero_fill(wait):
        for e in range(N_EXP):
            n, row0 = gd_ref[NT + 1, e], gd_ref[NT, e]
            b = TM // GRAN // 2
            while b >= 1:
                @pl.when((n & b) != 0)
                def _(b=b, n=n, row0=row0):
                    start = pl.multiple_of(row0 + ((n >> b.bit_length()) << b.bit_length()) * GRAN, GRAN)
                    cp = pltpu.make_async_copy(zbuf.at[pl.ds(0, b * GRAN), :],
                                               xs_ref.at[pl.ds(start, b * GRAN), :], zsem)
                    cp.wait() if wait else cp.start()
                b //= 2

        def zt(t, carry):
            tile_fill(t).wait() if wait else tile_fill(t).start()
            return carry
        lax.fori_loop(gd_ref[NT, G_LAST], n_tiles, zt, 0)

    @pl.when(j == 0)
    def _():
        zbuf[...] = jnp.zeros_like(zbuf)
        zero_fill(False)

    @pl.when(j >= 2)
    def _():
        drain(j - 2, slot)

    s = srow_ref[0]
    rows = lax.broadcasted_iota(jnp.int32, (SL, s.shape[1]), 0).astype(F32)
    m0 = rows == s[0:1, :]
    m1 = rows == s[1:2, :]
    oh = jnp.where(m0 | m1, 1.0, 0.0).astype(BF16)
    dw = u_ref.shape[1] // 2
    buf[slot, :, 0:dw] = _pack_halves(jnp.dot(oh, u_ref[...], preferred_element_type=F32))
    wrow = jnp.sum(jnp.where(m0, s[2:3, :], 0.0) + jnp.where(m1, s[3:4, :], 0.0), axis=1, keepdims=True)
    buf[slot, :, dw:dw + LANES] = lax.bitcast_convert_type(jnp.broadcast_to(wrow, (SL, LANES)), U32)

    def issue(g):
        _granule_copy(buf.at[slot], pl.multiple_of(g * GRAN, GRAN), xs_ref,
                      pl.multiple_of(gd_ref[j, g], GRAN), sems.at[slot]).start()

    _for_granules(gd_ref[j, G_LAST], issue)

    @pl.when(j == NT - 1)
    def _():
        drain(j, slot)
        if NT > 1:
            drain(j - 1, 1 - slot)
        zero_fill(True)


def _dispatch(gd, srow, u2, *, n_tiles, TM):
    N, D = u2.shape
    NT, _, tb = srow.shape
    SL = _slots_per_tile(tb)
    n_rows = n_tiles * TM
    kern = functools.partial(_dispatch_kernel, NT=NT, SL=SL, TM=TM, n_tiles=n_tiles)
    grid_spec = pltpu.PrefetchScalarGridSpec(
        num_scalar_prefetch=1,
        grid=(NT,),
        in_specs=[pl.BlockSpec((1, SUBLANES, tb), lambda j, gd: (j, 0, 0)),
                  pl.BlockSpec((tb, D), lambda j, gd: (j, 0))],
        out_specs=pl.BlockSpec(memory_space=pl.ANY),
        scratch_shapes=[pltpu.VMEM((2, SL, D // 2 + LANES), U32), pltpu.VMEM((TM, D // 2 + LANES), U32),
                        pltpu.SemaphoreType.DMA((3,))],
    )
    return pl.pallas_call(
        kern,
        grid_spec=grid_spec,
        out_shape=jax.ShapeDtypeStruct((n_rows, D // 2 + LANES), U32),
        compiler_params=_cparams(),
        name="dispatch",
    )(gd, srow, u2)


def _ffn_kernel(te_ref, nv_ref, vr_ref, xs_ref, wg_ref, wu_ref, wd_ref, o_ref, wgb, wub, wdb, sg, su, sd, slot_ref,
                sems):
    i = pl.program_id(0)
    nv = nv_ref[0]
    e = te_ref[i]

    def weight_copies(ex, sl):
        return (pltpu.make_async_copy(wg_ref.at[ex], sg.at[sl], sems.at[sl]),
                pltpu.make_async_copy(wu_ref.at[ex], su.at[sl], sems.at[sl]),
                pltpu.make_async_copy(wd_ref.at[ex], sd.at[sl], sems.at[sl]))

    @pl.when(i == 0)
    def _():
        slot_ref[0] = 0
        for cp in weight_copies(e, 0):
            cp.start()

    new_expert = (i < nv) & ((i == 0) | (e != te_ref[jnp.maximum(i - 1, 0)]))

    @pl.when(new_expert)
    def _():
        sl = slot_ref[0]
        for cp in weight_copies(e, sl):
            cp.wait()
        nxt = lax.while_loop(lambda t: (t < nv) & (te_ref[jnp.minimum(t, nv - 1)] == e), lambda t: t + 1, i + 1)

        @pl.when(nxt < nv)
        def _():
            for cp in weight_copies(te_ref[nxt], 1 - sl):
                cp.start()

    hm = xs_ref.shape[0] // FFN_SUB
    dw = xs_ref.shape[1] - LANES

    def swiglu_rows(nsub, cast):
        if cast:
            sl = slot_ref[0]
            wgb[...] = sg[sl].astype(BF16)
            wub[...] = su[sl].astype(BF16)
            wdb[...] = sd[sl].astype(BF16)
            slot_ref[0] = 1 - sl
        halves = tuple(slice(hm * j, hm * (j + 1)) for j in range(nsub))
        x = [_unpack_halves(xs_ref[r, 0:dw]) for r in halves]
        g = [jnp.dot(x[j], wgb[...], preferred_element_type=F32) for j in range(nsub)]
        u = [jnp.dot(x[j], wub[...], preferred_element_type=F32) for j in range(nsub)]
        h = [(g[j] * _sigmoid(g[j]) * u[j]).astype(BF16) for j in range(nsub)]
        y = [jnp.dot(h[j], wdb[...], preferred_element_type=F32) for j in range(nsub)]
        for j in range(nsub):
            wt = lax.bitcast_convert_type(xs_ref[halves[j], dw:dw + LANES], F32)
            yw = y[j] * jnp.concatenate([wt] * (2 * dw // LANES), axis=1)
            o_ref[halves[j], 0:dw] = _pack_halves(yw.astype(BF16).astype(F32))
            o_ref[halves[j], dw:dw + LANES] = xs_ref[halves[j], dw:dw + LANES]
        if nsub < FFN_SUB:
            o_ref[hm * nsub:, :] = jnp.zeros((hm * (FFN_SUB - nsub), dw + LANES), U32)

    used = vr_ref[i]
    for nsub in range(1, FFN_SUB + 1):
        lo, hi = hm * (nsub - 1), hm * nsub
        rows_here = (i < nv) & (used > lo) & ((used <= hi) if nsub < FFN_SUB else True)
        pl.when(rows_here & new_expert)(functools.partial(swiglu_rows, nsub, True))
        pl.when(rows_here & jnp.logical_not(new_expert))(functools.partial(swiglu_rows, nsub, False))


def _ffn(te, nv, vr, xs, wg, wu, wd, *, TM):
    P, XW = xs.shape
    DW = XW - LANES
    D = 2 * DW
    n_tiles = P // TM
    used_tile = lambda i, te, nv, vr: (jnp.maximum(jnp.minimum(i, nv[0] - 1), 0), 0)
    grid_spec = pltpu.PrefetchScalarGridSpec(
        num_scalar_prefetch=3,
        grid=(n_tiles,),
        in_specs=[pl.BlockSpec((TM, XW), used_tile),
                  pl.BlockSpec(memory_space=pl.ANY),
                  pl.BlockSpec(memory_space=pl.ANY),
                  pl.BlockSpec(memory_space=pl.ANY)],
        out_specs=pl.BlockSpec((TM, XW), used_tile),
        scratch_shapes=[pltpu.VMEM((D, D_EXP), BF16), pltpu.VMEM((D, D_EXP), BF16),
                        pltpu.VMEM((D_EXP, D), BF16),
                        pltpu.VMEM((2, D, D_EXP), F32), pltpu.VMEM((2, D, D_EXP), F32),
                        pltpu.VMEM((2, D_EXP, D), F32), pltpu.SMEM((1,), jnp.int32),
                        pltpu.SemaphoreType.DMA((2,))],
    )
    return pl.pallas_call(
        _ffn_kernel,
        grid_spec=grid_spec,
        out_shape=jax.ShapeDtypeStruct((P, XW), U32),
        input_output_aliases={3: 0},
        compiler_params=_cparams(),
        name="ffn",
    )(te, nv, vr, xs, wg, wu, wd)


def _combine_kernel(gd_ref, ys_ref, col_ref, x1_ref, mod_ref, g_ref, b_ref, o_ref, buf, sems, *, NT, SL):
    j = pl.program_id(0)
    slot = j % 2

    def fetch(tile, sl):
        def f(g):
            _granule_copy(ys_ref, pl.multiple_of(gd_ref[tile, g], GRAN), buf.at[sl],
                          pl.multiple_of(g * GRAN, GRAN), sems.at[sl]).start()
        _for_granules(gd_ref[tile, G_LAST], f)

    @pl.when(j == 0)
    def _():
        fetch(0, 0)

    @pl.when(j + 1 < NT)
    def _():
        fetch(j + 1, 1 - slot)

    ng = gd_ref[j, G_LAST]

    _wait_granules(ng, ys_ref, buf.at[slot], sems.at[slot], SL // GRAN)

    rows = lax.broadcasted_iota(jnp.int32, (SL, 1), 0)
    yb = _unpack_halves(jnp.where(rows < ng * GRAN, buf[slot], U32(0)))
    col = col_ref[...]
    tb = col.shape[0]
    lanes = lax.broadcasted_iota(jnp.int32, (tb, SL), 1).astype(F32)
    sel = jnp.where((lanes == col[:, 0:1]) | (lanes == col[:, 1:2]), 1.0, 0.0).astype(BF16)
    y = jnp.dot(sel, yb, preferred_element_type=F32)
    mod = mod_ref[0]
    z = ALPHA * x1_ref[...] + (1.0 + mod[5:6, :]) * y
    o_ref[...] = _layer_norm(z, g_ref[...], b_ref[...])


def _combine(gd, ys, col, x1, mod3, g, b, *, S, tb):
    N, D = x1.shape
    NT = N // tb
    tpb = S // tb
    SL = _slots_per_tile(tb)
    kern = functools.partial(_combine_kernel, NT=NT, SL=SL)
    grid_spec = pltpu.PrefetchScalarGridSpec(
        num_scalar_prefetch=1,
        grid=(NT,),
        in_specs=[pl.BlockSpec(memory_space=pl.ANY),
                  pl.BlockSpec((tb, LANES), lambda j, gd: (j, 0)),
                  pl.BlockSpec((tb, D), lambda j, gd: (j, 0)),
                  pl.BlockSpec((1, 6, D), lambda j, gd: (j // tpb, 0, 0)),
                  pl.BlockSpec((1, D), lambda j, gd: (0, 0)),
                  pl.BlockSpec((1, D), lambda j, gd: (0, 0))],
        out_specs=pl.BlockSpec((tb, D), lambda j, gd: (j, 0)),
        scratch_shapes=[pltpu.VMEM((2, SL, D // 2), U32), pltpu.SemaphoreType.DMA((2,))],
    )
    return pl.pallas_call(
        kern,
        grid_spec=grid_spec,
        out_shape=jax.ShapeDtypeStruct((N, D), F32),
        compiler_params=_cparams(),
        name="combine",
    )(gd, ys, col, x1, mod3, g, b)


def _layer(x, c, l, w_ada, b_ada, w_in, w_conv, b_conv, b_igate, b_fgate, mlstm_norm_g, w_gla_a, b_gla_a,
           gla_norm_g, w_out, ln1_g, ln1_b, w_route_group, b_route_group, w_route_expert, b_route_expert,
           w_gate, w_up, w_down, ln2_g, ln2_b):
    B, S, D = x.shape
    N = B * S
    x2 = x.reshape(N, D)
    tm_in = min(512, S)
    tm = min(256, S)
    lm = min(256, S)
    assert S % tm_in == 0 and S % tm == 0 and tm_in % lm == 0 and S % G_CHUNK == 0
    assert w_in.shape[1:] == (D, IN_TOT) and w_gate.shape[1:] == (N_EXP, D, D_EXP)

    mod3 = _ada(c, w_ada[l], b_ada[l]).reshape(B, 6, D)

    wa_pad = jnp.zeros((LANES, G_KW), F32).at[SM_A:SM_A + G_RANK].set(w_gla_a[l]).astype(BF16)
    bg = (jnp.zeros((2 * SUBLANES, 1), F32).at[0:M_HEADS, 0].set(b_igate[l])
          .at[SUBLANES:SUBLANES + M_HEADS, 0].set(b_fgate[l]))
    oa, la, g3 = _inproj(x2, mod3, jnp.swapaxes(w_in, 1, 2), w_conv[l], b_conv[l].reshape(1, -1), wa_pad,
                         b_gla_a[l].reshape(1, -1), bg, S=S, tm=tm_in, lm=lm, layer=l)

    u_tri = jnp.asarray(np.triu(np.ones((lm, lm), np.float32)))
    nb = 4 if B % 4 == 0 else (2 if B % 2 == 0 else 1)
    ts = min(512, S)
    hm = _mlstm(oa, g3, u_tri, mlstm_norm_g[l].reshape(1, -1), B=B, S=S, L=lm, nb=nb, ts=ts)
    w3_np, mk_np = _gla_consts()
    hg = _gla(oa, la, jnp.asarray(w3_np, BF16), jnp.asarray(mk_np), gla_norm_g[l].reshape(1, -1), B=B, S=S,
              nb=nb, ts=ts)

    br = (jnp.zeros((1, LANES), F32).at[0, 0:N_GROUPS].set(b_route_group[l])
          .at[0, SUBLANES:SUBLANES + N_EXP].set(b_route_expert[l]))
    x1, u2, rrow = _outproj(hm, hg, w_out[l], x2, mod3, ln1_g[l].reshape(1, -1), ln1_b[l].reshape(1, -1),
                            jnp.swapaxes(w_route_group, 1, 2), jnp.swapaxes(w_route_expert, 1, 2), br,
                            S=S, tb=tm, nh=4 if S % (4 * tm) == 0 else 1, layer=l)

    u_cnt = jnp.asarray(np.triu(np.ones((tm, tm), np.float32)), BF16)
    ltri = jnp.asarray(np.tril(np.ones((N_EXP, N_EXP), np.float32), -1))
    srow, col, gd3, meta = _route(rrow, u_cnt, ltri, TM=FFN_TM)
    gd = gd3.reshape(N // tm + 2, LANES)
    n_tiles = _ffn_tiles(N, tm)
    te, nv, vr = meta[0, :n_tiles], meta[1, 0:1], meta[2, :n_tiles]

    xs = _dispatch(gd, srow, u2, n_tiles=n_tiles, TM=FFN_TM)
    ys = _ffn(te, nv, vr, xs, w_gate[l], w_up[l], w_down[l], TM=FFN_TM)
    out = _combine(gd, ys, col, x1, mod3, ln2_g[l].reshape(1, -1), ln2_b[l].reshape(1, -1), S=S, tb=tm)
    return out.reshape(B, S, D)


def kernel(x, c, w_ada, b_ada, w_in, w_conv, b_conv, b_igate, b_fgate, mlstm_norm_g, w_gla_a, b_gla_a,
           gla_norm_g, w_out, ln1_g, ln1_b, w_route_group, b_route_group, w_route_expert, b_route_expert,
           w_gate, w_up, w_down, ln2_g, ln2_b):
    for l in range(DEPTH):
        x = _layer(x, c, l, w_ada, b_ada, w_in, w_conv, b_conv, b_igate, b_fgate, mlstm_norm_g, w_gla_a,
                   b_gla_a, gla_norm_g, w_out, ln1_g, ln1_b, w_route_group, b_route_group, w_route_expert,
                   b_route_expert, w_gate, w_up, w_down, ln2_g, ln2_b)
    return x
```

```python
import functools

import numpy as np
import jax
import jax.numpy as jnp
from jax import lax
from jax.experimental import pallas as pl
from jax.experimental.pallas import tpu as pltpu

F32 = jnp.float32
BF16 = jnp.bfloat16
U32 = jnp.uint32
HIGHEST = lax.Precision.HIGHEST

DEPTH = 1
M_HEADS = 4
M_HD = 128
M_W = M_HEADS * M_HD
CONV_W = 4
G_HEADS = 4
G_DK = 64
G_DV = 128
G_W = G_HEADS * G_DV
G_KW = G_HEADS * G_DK
G_RANK = 16
G_TAU = 16.0
G_CHUNK = 64
N_GROUPS = 4
E_PER_G = 8
N_EXP = N_GROUPS * E_PER_G
D_EXP = 512
ALPHA = (2 * DEPTH) ** 0.25
LN_EPS = 1e-5

LANES = 128
SUBLANES = 8
VMEM_LIMIT = 48 * 1024 * 1024

C_QK = 0
C_VO = 1024
C_GQK = 2048
C_GV = 2560
C_GG = 3072
C_SMALL = 3584
C_TOT = 3712
SM_I, SM_F, SM_A = 0, 8, 16
IN_GATES = 4 * M_W
IN_G = IN_GATES + 2 * M_HEADS
IN_GA = IN_G + 2 * G_KW + 2 * G_W
IN_TOT = IN_GA + G_RANK

FFN_TM = 512
FFN_SUB = 2
GRAN = SUBLANES
G_LAST = LANES - 1
META_EXPERT, META_USED, META_ROWS = 0, 1, 2


def _cparams(n_axes=1):
    return pltpu.CompilerParams(dimension_semantics=("arbitrary",) * n_axes,
                                vmem_limit_bytes=VMEM_LIMIT)


def _sigmoid(x):
    return 1.0 / (1.0 + jnp.exp(-x))


def _log_sigmoid(x):
    return jnp.minimum(x, 0.0) - jnp.log(1.0 + jnp.exp(-jnp.abs(x)))


def _ada_kernel(c_ref, w_ref, b_ref, o_ref):
    c = c_ref[...]
    ca = (c * _sigmoid(c)).astype(BF16)
    o_ref[...] = jnp.dot(ca, w_ref[...].astype(BF16), preferred_element_type=F32) + b_ref[...]


def _ada(c, w, b):
    B, D = c.shape
    n_out = w.shape[1]
    tn = 1024
    return pl.pallas_call(
        _ada_kernel,
        grid=(n_out // tn,),
        in_specs=[pl.BlockSpec((B, D), lambda j: (0, 0)),
                  pl.BlockSpec((D, tn), lambda j: (0, j)),
                  pl.BlockSpec((1, tn), lambda j: (0, j))],
        out_specs=pl.BlockSpec((B, tn), lambda j: (0, j)),
        out_shape=jax.ShapeDtypeStruct((B, n_out), F32),
        compiler_params=_cparams(),
        name="ada",
    )(c, w, b.reshape(1, n_out))


def _inproj_kernel(x_ref, mod_ref, win_ref, wc_ref, bc_ref, wa_ref, ba_ref, bg_ref,
                   oa_ref, la_ref, g_ref, halo_ref, w_ref, *, tm, tpb, lm):
    i = pl.program_id(0)

    @pl.when(i == 0)
    def _():
        rc = 2 * LANES
        for r in range(0, IN_GATES, rc):
            w_ref[:, r:r + rc] = win_ref[0, r:r + rc, :].T.astype(BF16)
        for r in range(0, C_SMALL - C_GQK, rc):
            w_ref[:, C_GQK + r:C_GQK + r + rc] = win_ref[0, IN_G + r:IN_G + r + rc, :].T.astype(BF16)
        gates = win_ref[0, IN_GATES:IN_G, :]
        z = lambda n: jnp.zeros((n, gates.shape[1]), F32)
        small = jnp.concatenate([gates[0:M_HEADS], z(SM_F - M_HEADS), gates[M_HEADS:2 * M_HEADS],
                                 z(SM_A - SM_F - M_HEADS), win_ref[0, IN_GA:IN_TOT, :],
                                 z(LANES - SM_A - G_RANK)], axis=0)
        w_ref[:, C_SMALL:C_TOT] = small.T.astype(BF16)

    @pl.when(i % tpb == 0)
    def _():
        halo_ref[0:SUBLANES, :] = jnp.zeros((SUBLANES, halo_ref.shape[1]), F32)

    mod = mod_ref[0]
    u = (x_ref[...] * (1.0 + mod[1:2, :]) + mod[0:1, :]).astype(BF16)

    def proj(c0, c1):
        return jnp.dot(u, w_ref[:, c0:c1], preferred_element_type=F32)

    p = proj(C_QK, C_QK + 2 * M_W)
    halo_ref[SUBLANES:SUBLANES + tm, :] = p
    acc = bc_ref[...] + wc_ref[CONV_W - 1:CONV_W, :] * p
    for j in range(CONV_W - 1):
        acc = acc + wc_ref[j:j + 1, :] * halo_ref[pl.ds(SUBLANES - (CONV_W - 1) + j, tm), :]
    halo_ref[0:SUBLANES, :] = p[tm - SUBLANES:, :]
    qk = acc * _sigmoid(acc)
    oa_ref[:, C_QK:C_QK + M_W] = qk[:, :M_W].astype(BF16)
    oa_ref[:, C_QK + M_W:C_QK + 2 * M_W] = (qk[:, M_W:] * (M_HD ** -0.5)).astype(BF16)

    ps = proj(C_SMALL, C_TOT)
    la = jnp.dot(ps.astype(BF16), wa_ref[...], preferred_element_type=F32) + ba_ref[...]
    la_ref[...] = _log_sigmoid(la) * (1.0 / G_TAU)
    pt = ps.T
    gi = pt[SM_I:SM_I + SUBLANES, :] + bg_ref[0:SUBLANES, :]
    gf = _log_sigmoid(pt[SM_F:SM_F + SUBLANES, :] + bg_ref[SUBLANES:2 * SUBLANES, :])
    for j in range(tm // lm):
        g_ref[j, 0:SUBLANES, :] = gi[:, j * lm:(j + 1) * lm]
        g_ref[j, SUBLANES:2 * SUBLANES, :] = gf[:, j * lm:(j + 1) * lm]

    p = proj(C_VO, C_VO + 2 * M_W)
    oa_ref[:, C_VO:C_VO + 2 * M_W] = p.astype(BF16)

    p = proj(C_GQK, C_GQK + G_KW)
    oa_ref[:, C_GQK:C_GQK + G_KW] = (p * (G_DK ** -0.5)).astype(BF16)
    p = proj(C_GQK + G_KW, C_SMALL)
    oa_ref[:, C_GQK + G_KW:C_SMALL] = p.astype(BF16)


def _inproj(x2, mod3, w_in, w_conv, b_conv, wa_pad, b_gla, bg, *, S, tm, lm, layer):
    N, D = x2.shape
    tpb = S // tm
    kern = functools.partial(_inproj_kernel, tm=tm, tpb=tpb, lm=lm)
    return pl.pallas_call(
        kern,
        grid=(N // tm,),
        in_specs=[pl.BlockSpec((tm, D), lambda i: (i, 0)),
                  pl.BlockSpec((1, 6, D), lambda i: (i // tpb, 0, 0)),
                  pl.BlockSpec((1, IN_TOT, D), lambda i: (layer, 0, 0), pipeline_mode=pl.Buffered(1)),
                  pl.BlockSpec((CONV_W, 2 * M_W), lambda i: (0, 0)),
                  pl.BlockSpec((1, 2 * M_W), lambda i: (0, 0)),
                  pl.BlockSpec((LANES, G_KW), lambda i: (0, 0)),
                  pl.BlockSpec((1, G_KW), lambda i: (0, 0)),
                  pl.BlockSpec((2 * SUBLANES, 1), lambda i: (0, 0))],
        out_specs=[pl.BlockSpec((tm, C_SMALL), lambda i: (i, 0)),
                   pl.BlockSpec((tm, G_KW), lambda i: (i, 0)),
                   pl.BlockSpec((tm // lm, 2 * SUBLANES, lm), lambda i: (i, 0, 0))],
        out_shape=[jax.ShapeDtypeStruct((N, C_SMALL), BF16),
                   jax.ShapeDtypeStruct((N, G_KW), F32),
                   jax.ShapeDtypeStruct((N // lm, 2 * SUBLANES, lm), F32)],
        scratch_shapes=[pltpu.VMEM((SUBLANES + tm, 2 * M_W), F32), pltpu.VMEM((D, C_TOT), BF16)],
        compiler_params=_cparams(),
        name="inproj",
    )(x2, mod3, w_in, w_conv, b_conv, wa_pad, b_gla, bg)


def _mlstm_sel():
    sel = np.zeros((2 * LANES, 2 * M_HEADS * M_HD), np.float32)
    for j in range(2 * M_HEADS):
        src = (SUBLANES if j < M_HEADS else 3 * SUBLANES) + j % M_HEADS
        sel[src, M_HD * j:M_HD * (j + 1)] = 1.0
        sel[LANES + src, M_HD * j:M_HD * (j + 1)] = 1.0
    return sel


def _mlstm_kernel(qk_ref, vo_ref, g_ref, u_ref, gain_ref, sel_ref, out_ref, c_ref, zt_ref, a_ref, dec_ref, m_ref,
                  *, L, NC, nb):
    @pl.when(pl.program_id(1) == 0)
    def _():
        c_ref[...] = jnp.zeros_like(c_ref)
        m_ref[...] = jnp.zeros_like(m_ref)

    tril = (lax.broadcasted_iota(jnp.int32, (L, L), 0) >= lax.broadcasted_iota(jnp.int32, (L, L), 1))
    ones_v = jnp.ones((L, M_HD), BF16)
    zpad = jnp.zeros((LANES - 4 * SUBLANES, L), F32)
    zgroup = lax.broadcasted_iota(jnp.int32, (L, LANES), 1) // SUBLANES
    factor_cols = (zgroup == 1) | (zgroup == 3)

    order = [(bi, c) for bi in range(nb) for c in range(NC)]
    f_all = jnp.concatenate([g_ref[bi, c, SUBLANES:2 * SUBLANES, :] for bi, c in order], axis=0)
    i_all = jnp.concatenate([g_ref[bi, c, 0:SUBLANES, :] for bi, c in order], axis=0)
    b_all = jnp.dot(f_all, u_ref[...], preferred_element_type=F32, precision=HIGHEST)
    a_all = i_all - b_all
    lane_all = lax.broadcasted_iota(jnp.int32, a_all.shape, 1)
    g_all = a_all
    s = 1
    while s < L:
        g_all = jnp.maximum(g_all, jnp.where(lane_all >= s, pltpu.roll(g_all, s, 1), -jnp.inf))
        s *= 2
    for bi in range(nb):
        m_prev = m_ref[bi][:, 0:1]
        for c in range(NC):
            ci = bi * NC + c
            r8 = slice(SUBLANES * ci, SUBLANES * (ci + 1))
            a, b = a_all[r8], b_all[r8]
            a_ref[ci] = a
            M = jnp.maximum(g_all[r8], m_prev)
            ML = M[:, L - 1:L]
            Z = jnp.concatenate([M, jnp.exp(m_prev - M), jnp.exp(-(b + M)), jnp.exp(a - ML), zpad],
                                axis=0)
            zt_ref[ci] = Z.T
            dec_ref[ci] = jnp.broadcast_to(jnp.exp(m_prev - ML), (SUBLANES, 2 * M_HD))
            m_prev = b[:, L - 1:L] + ML
        m_ref[bi] = jnp.broadcast_to(m_prev, (SUBLANES, LANES))

    chains = [(bi, h) for bi in range(nb) for h in range(M_HEADS)]
    nt = (((1,), (1,)), ((), ()))
    tn = (((0,), (0,)), ((), ()))

    def chunk(c, carry):
        rows = pl.ds(pl.multiple_of(c * L, L), L)
        Zt = [zt_ref[bi * NC + c] for bi in range(nb)]
        a = [a_ref[bi * NC + c] for bi in range(nb)]
        dec = [dec_ref[bi * NC + c] for bi in range(nb)]
        hs = [slice(h * M_HD, (h + 1) * M_HD) for h in range(M_HEADS)]
        hs2 = [slice(M_W + h * M_HD, M_W + (h + 1) * M_HD) for h in range(M_HEADS)]
        q = [qk_ref[bi, rows, hs[h]] for bi, h in chains]
        k = [qk_ref[bi, rows, hs2[h]] for bi, h in chains]
        vext = [jnp.concatenate([vo_ref[bi, rows, hs[h]], ones_v], axis=1) for bi, h in chains]
        cst = [c_ref[bi * M_HEADS + h] for bi, h in chains]
        n = range(len(chains))
        sc = [lax.dot_general(q[i], k[i], nt, preferred_element_type=F32) for i in n]
        qc = [jnp.dot(q[i], cst[i].astype(BF16), preferred_element_type=F32) for i in n]
        pm = [(sc[i] * jnp.exp(jnp.where(tril, a[bi][h:h + 1, :] - Zt[bi][:, h:h + 1], -jnp.inf))).astype(BF16)
              for i, (bi, h) in enumerate(chains)]
        pv = [jnp.dot(pm[i], vext[i], preferred_element_type=F32) for i in n]
        rep = []
        for bi in range(nb):
            zf = jnp.where(factor_cols, Zt[bi], 0.0)
            zh = zf.astype(BF16)
            zl = (zf - zh.astype(F32)).astype(BF16)
            rep.append(jnp.dot(jnp.concatenate([zh, zl], axis=1), sel_ref[...], preferred_element_type=F32))
        e_inter = [rep[bi][:, M_HD * h:M_HD * (h + 1)] for bi, h in chains]
        w_state = [rep[bi][:, M_HD * (M_HEADS + h):M_HD * (M_HEADS + h + 1)] for bi, h in chains]
        kw = [(w_state[i] * k[i].astype(F32)).astype(BF16) for i in n]
        upd = [lax.dot_general(kw[i], vext[i], tn, preferred_element_type=F32) for i in n]
        for i, (bi, h) in enumerate(chains):
            c_ref[bi * M_HEADS + h] = dec[bi][h:h + 1, :] * cst[i] + upd[i]
            nd = pv[i] + jnp.concatenate([e_inter[i], e_inter[i]], axis=1) * qc[i]
            hh = nd[:, :M_HD] / jnp.maximum(jnp.abs(nd[:, M_HD:]),
                                            Zt[bi][:, 2 * SUBLANES + h:2 * SUBLANES + h + 1])
            hh = _sigmoid(vo_ref[bi, rows, hs2[h]].astype(F32)) * hh
            hn = hh * lax.rsqrt(jnp.mean(hh * hh, axis=-1, keepdims=True) + LN_EPS)
            out_ref[bi, rows, hs[h]] = (hn * gain_ref[:, hs[h]]).astype(BF16)
        return carry

    lax.fori_loop(0, NC, chunk, 0, unroll=True)


def _mlstm(oa, g3, u_tri, gain, *, B, S, L, nb, ts):
    N = oa.shape[0]
    NC = ts // L
    oa3 = oa.reshape(B, S, oa.shape[1])
    g4 = g3.reshape(B, S // L, 2 * SUBLANES, L)
    sel = jnp.asarray(_mlstm_sel(), BF16)
    kern = functools.partial(_mlstm_kernel, L=L, NC=NC, nb=nb)
    out = pl.pallas_call(
        kern,
        grid=(B // nb, S // ts),
        in_specs=[pl.BlockSpec((nb, ts, 2 * M_W), lambda b, t: (b, t, C_QK // (2 * M_W))),
                  pl.BlockSpec((nb, ts, 2 * M_W), lambda b, t: (b, t, C_VO // (2 * M_W))),
                  pl.BlockSpec((nb, NC, 2 * SUBLANES, L), lambda b, t: (b, t, 0, 0)),
                  pl.BlockSpec((L, L), lambda b, t: (0, 0)),
                  pl.BlockSpec((1, M_W), lambda b, t: (0, 0)),
                  pl.BlockSpec(sel.shape, lambda b, t: (0, 0))],
        out_specs=pl.BlockSpec((nb, ts, M_W), lambda b, t: (b, t, 0)),
        out_shape=jax.ShapeDtypeStruct((B, S, M_W), BF16),
        scratch_shapes=[pltpu.VMEM((nb * M_HEADS, M_HD, 2 * M_HD), F32),
                        pltpu.VMEM((nb * NC, L, LANES), F32),
                        pltpu.VMEM((nb * NC, SUBLANES, L), F32),
                        pltpu.VMEM((nb * NC, SUBLANES, 2 * M_HD), F32),
                        pltpu.VMEM((nb, SUBLANES, LANES), F32)],
        compiler_params=_cparams(2),
        name="mlstm",
    )(oa3, oa3, g4, u_tri, gain, sel)
    return out.reshape(N, M_W)


_G_LEVELS = 6
_G_XROW = 2 * G_CHUNK + SUBLANES


def _gla_consts():
    L = G_CHUNK
    t = np.arange(L)
    blocks = [(t[None, :] <= t[:, None]).astype(np.float32),
              (t[None, :] > t[:, None]).astype(np.float32),
              np.ones((SUBLANES, L), np.float32)]
    masks = [np.eye(L, dtype=np.float32)]
    m = 1
    while m < L:
        wl = np.zeros((L, L), np.float32)
        for r in range(L):
            r0 = (r // (2 * m)) * 2 * m + m
            if r % (2 * m) >= m:
                wl[r, r0:r + 1] = 1.0
            else:
                wl[r, r + 1:r0] = 1.0
        blocks.append(wl)
        tt, ss = t[:, None], t[None, :]
        masks.append(((tt // (2 * m) == ss // (2 * m)) & (tt % (2 * m) >= m)
                      & (ss % (2 * m) < m)).astype(np.float32))
        m *= 2
    w = np.concatenate(blocks, axis=0)
    w3 = np.concatenate([w, w, w], axis=1)
    mk = np.stack([np.concatenate([x] * G_HEADS, axis=0) for x in masks])
    return w3, mk


def _gla_kernel(qk_ref, v_ref, gg_ref, la_ref, w3_ref, mk_ref, gain_ref, out_ref, st_ref, *, NC, nb):
    L = G_CHUNK

    @pl.when(pl.program_id(1) == 0)
    def _():
        st_ref[...] = jnp.zeros_like(st_ref)

    lane_head = lax.broadcasted_iota(jnp.int32, (L, G_KW), 1) // G_DK
    br = lax.broadcasted_iota(jnp.int32, (2 * G_DV, LANES), 0) < G_DV
    bl = lax.broadcasted_iota(jnp.int32, (2 * G_DV, LANES), 1) < G_DK
    bmask = br == bl
    nt = (((1,), (1,)), ((), ()))
    tn = (((0,), (0,)), ((), ()))

    def chunk(c, carry):
        rows = pl.ds(pl.multiple_of(c * L, L), L)
        X, q, k = [], [], []
        for bi in range(nb):
            la = la_ref[bi, rows, :]
            hi = la.astype(BF16)
            r1 = la - hi.astype(F32)
            mid = r1.astype(BF16)
            lo = (r1 - mid.astype(F32)).astype(BF16)
            stk = jnp.concatenate([hi, mid, lo], axis=0)
            X.append(jnp.exp(jnp.dot(w3_ref[...], stk, preferred_element_type=F32)))
            q.append(qk_ref[bi, rows, 0:G_KW].astype(F32))
            k.append(qk_ref[bi, rows, G_KW:2 * G_KW].astype(F32))

        sc = [[None] * (_G_LEVELS + 1) for _ in range(nb)]
        for lev in range(_G_LEVELS + 1):
            for bi in range(nb):
                if lev == 0:
                    qt, kt = q[bi], k[bi]
                else:
                    xl = X[bi][_G_XROW + L * (lev - 1):_G_XROW + L * lev, :]
                    qt, kt = q[bi] * xl, k[bi] * xl
                q4 = jnp.concatenate([jnp.where(lane_head == h, qt, 0.0) for h in range(G_HEADS)],
                                     axis=0).astype(BF16)
                sc[bi][lev] = lax.dot_general(q4, kt.astype(BF16), nt, preferred_element_type=F32)
        Ab = []
        for bi in range(nb):
            A = sc[bi][0] * mk_ref[0]
            for lev in range(1, _G_LEVELS + 1):
                A = A + sc[bi][lev] * mk_ref[lev]
            Ab.append(A.astype(BF16))

        for bi in range(nb):
            gg = gg_ref[bi, rows, :].astype(F32)
            gate = gg * _sigmoid(gg)
            for p in range(2):
                ls = slice(LANES * p, LANES * (p + 1))
                vp = v_ref[bi, rows, 2 * G_DV * p:2 * G_DV * (p + 1)]
                oi = [jnp.dot(Ab[bi][L * (2 * p + hh):L * (2 * p + hh + 1)],
                              vp[:, G_DV * hh:G_DV * (hh + 1)], preferred_element_type=F32)
                      for hh in range(2)]
                st = st_ref[bi, p]
                qc = (q[bi][:, ls] * X[bi][0:L, ls]).astype(BF16)
                o_inter = lax.dot_general(qc, st.astype(BF16), nt, preferred_element_type=F32)
                kc = (k[bi][:, ls] * X[bi][L:2 * L, ls]).astype(BF16)
                upd = lax.dot_general(vp, kc, tn, preferred_element_type=F32)
                dec = X[bi][2 * L:2 * L + 1, ls]
                st_ref[bi, p] = jnp.where(bmask, dec * st + upd, 0.0)
                for hh in range(2):
                    o = o_inter[:, G_DV * hh:G_DV * (hh + 1)] + oi[hh]
                    hn = o * lax.rsqrt(jnp.mean(o * o, axis=-1, keepdims=True) + LN_EPS)
                    hs = slice(G_DV * (2 * p + hh), G_DV * (2 * p + hh + 1))
                    out_ref[bi, rows, hs] = (hn * gain_ref[:, hs] * gate[:, hs]).astype(BF16)
        return carry

    lax.fori_loop(0, NC, chunk, 0, unroll=2 if NC % 2 == 0 else 1)


def _gla(oa, la, w3, mk, gain, *, B, S, nb, ts):
    N = oa.shape[0]
    oa3 = oa.reshape(B, S, oa.shape[1])
    la3 = la.reshape(B, S, G_KW)
    kern = functools.partial(_gla_kernel, NC=ts // G_CHUNK, nb=nb)
    out = pl.pallas_call(
        kern,
        grid=(B // nb, S // ts),
        in_specs=[pl.BlockSpec((nb, ts, 2 * G_KW), lambda b, t: (b, t, C_GQK // (2 * G_KW))),
                  pl.BlockSpec((nb, ts, G_W), lambda b, t: (b, t, C_GV // G_W)),
                  pl.BlockSpec((nb, ts, G_W), lambda b, t: (b, t, C_GG // G_W)),
                  pl.BlockSpec((nb, ts, G_KW), lambda b, t: (b, t, 0)),
                  pl.BlockSpec(w3.shape, lambda b, t: (0, 0)),
                  pl.BlockSpec(mk.shape, lambda b, t: (0, 0, 0)),
                  pl.BlockSpec((1, G_W), lambda b, t: (0, 0))],
        out_specs=pl.BlockSpec((nb, ts, G_W), lambda b, t: (b, t, 0)),
        out_shape=jax.ShapeDtypeStruct((B, S, G_W), BF16),
        scratch_shapes=[pltpu.VMEM((nb, 2, 2 * G_DV, LANES), F32)],
        compiler_params=_cparams(2),
        name="gla",
    )(oa3, oa3, oa3, la3, w3, mk, gain)
    return out.reshape(N, G_W)


def _layer_norm(z, g, b):
    mu = jnp.mean(z, axis=-1, keepdims=True)
    zc = z - mu
    var = jnp.mean(zc * zc, axis=-1, keepdims=True)
    return zc * lax.rsqrt(var + LN_EPS) * g + b


def _outproj_kernel(hm_ref, hg_ref, wf_ref, x_ref, mod_ref, g_ref, b_ref, wrg_ref, wre_ref, br_ref,
                    x1_ref, u2_ref, rrow_ref, w_ref, wr_ref, *, tb, nh):
    @pl.when(pl.program_id(0) == 0)
    def _():
        w_ref[...] = wf_ref[...].astype(BF16)
        z = lambda n: jnp.zeros((n, wrg_ref.shape[2]), F32)
        wt = jnp.concatenate([wrg_ref[0], z(SUBLANES - N_GROUPS), wre_ref[0],
                              z(LANES - SUBLANES - N_EXP)], axis=0).T
        hi = wt.astype(BF16)
        wr_ref[:, 0:LANES] = hi
        wr_ref[:, LANES:2 * LANES] = (wt - hi.astype(F32)).astype(BF16)

    mod = mod_ref[0]
    blocks = [slice(tb * j, tb * (j + 1)) for j in range(nh)]
    y = [jnp.dot(hm_ref[r, :], w_ref[0:M_W, :], preferred_element_type=F32)
         + jnp.dot(hg_ref[r, :], w_ref[M_W:M_W + G_W, :], preferred_element_type=F32) for r in blocks]
    u2 = []
    for j, r in enumerate(blocks):
        z = ALPHA * x_ref[r, :] + (1.0 + mod[2:3, :]) * y[j]
        x1 = _layer_norm(z, g_ref[...], b_ref[...])
        x1_ref[r, :] = x1
        u2.append(x1 * (1.0 + mod[4:5, :]) + mod[3:4, :])
        u2_ref[r, :] = u2[j].astype(BF16)

    u2h = [u.astype(BF16) for u in u2]
    u2l = [(u2[j] - u2h[j].astype(F32)).astype(BF16) for j in range(nh)]
    lh = [jnp.dot(u, wr_ref[...], preferred_element_type=F32) for u in u2h]
    ll = [jnp.dot(u, wr_ref[:, 0:LANES], preferred_element_type=F32) for u in u2l]
    for j in range(nh):
        logits = lh[j][:, 0:LANES] + lh[j][:, LANES:2 * LANES] + ll[j] + br_ref[...]
        rrow_ref[j] = _route_select(logits.T, tb)


def _route_select(lt, tm):
    row = lax.broadcasted_iota(jnp.int32, (SUBLANES, tm), 0)
    gl = jnp.where(row < N_GROUPS, lt[0:SUBLANES, :], -jnp.inf)
    gmax = jnp.max(gl, axis=0, keepdims=True)
    gsel = jnp.min(jnp.where(gl == gmax, row, SUBLANES), axis=0, keepdims=True)
    pg = 1.0 / jnp.sum(jnp.exp(gl - gmax), axis=0, keepdims=True)
    ein = jnp.zeros((SUBLANES, tm), F32)
    for g in range(N_GROUPS):
        ein = jnp.where(gsel == g, lt[SUBLANES * (g + 1):SUBLANES * (g + 2), :], ein)
    v1 = jnp.max(ein, axis=0, keepdims=True)
    i1 = jnp.min(jnp.where(ein == v1, row, SUBLANES), axis=0, keepdims=True)
    rest = jnp.where(row == i1, -jnp.inf, ein)
    v2 = jnp.max(rest, axis=0, keepdims=True)
    i2 = jnp.min(jnp.where(rest == v2, row, SUBLANES), axis=0, keepdims=True)
    t2 = jnp.exp(v2 - v1)
    p1 = 1.0 / (1.0 + t2)
    e0 = (gsel * E_PER_G + i1).astype(F32)
    e1 = (gsel * E_PER_G + i2).astype(F32)
    return jnp.concatenate([e0, e1, pg * p1, pg * (t2 * p1), jnp.zeros((SUBLANES - 4, tm), F32)], axis=0)


def _outproj(hm, hg, w_out, x2, mod3, g, b, wrg_t, wre_t, br, *, S, tb, nh, layer):
    N, D = x2.shape
    tm = tb * nh
    tpb = S // tm
    kern = functools.partial(_outproj_kernel, tb=tb, nh=nh)
    return pl.pallas_call(
        kern,
        grid=(N // tm,),
        in_specs=[pl.BlockSpec((tm, M_W), lambda i: (i, 0)),
                  pl.BlockSpec((tm, G_W), lambda i: (i, 0)),
                  pl.BlockSpec((M_W + G_W, D), lambda i: (0, 0), pipeline_mode=pl.Buffered(1)),
                  pl.BlockSpec((tm, D), lambda i: (i, 0)),
                  pl.BlockSpec((1, 6, D), lambda i: (i // tpb, 0, 0)),
                  pl.BlockSpec((1, D), lambda i: (0, 0)),
                  pl.BlockSpec((1, D), lambda i: (0, 0)),
                  pl.BlockSpec((1, N_GROUPS, D), lambda i: (layer, 0, 0)),
                  pl.BlockSpec((1, N_EXP, D), lambda i: (layer, 0, 0)),
                  pl.BlockSpec((1, LANES), lambda i: (0, 0))],
        out_specs=[pl.BlockSpec((tm, D), lambda i: (i, 0)),
                   pl.BlockSpec((tm, D), lambda i: (i, 0)),
                   pl.BlockSpec((nh, SUBLANES, tb), lambda i: (i, 0, 0))],
        out_shape=[jax.ShapeDtypeStruct((N, D), F32),
                   jax.ShapeDtypeStruct((N, D), BF16),
                   jax.ShapeDtypeStruct((N // tb, SUBLANES, tb), F32)],
        scratch_shapes=[pltpu.VMEM((M_W + G_W, D), BF16), pltpu.VMEM((D, 2 * LANES), BF16)],
        compiler_params=_cparams(),
        name="outproj",
    )(hm, hg, w_out, x2, mod3, g, b, wrg_t, wre_t, br)


def _slots_per_tile(tb):
    worst = 2 * tb + N_EXP * (GRAN - 1)
    return -(-worst // LANES) * LANES


def _ffn_tiles(n_tok, tb):
    worst_rows = 2 * n_tok + (n_tok // tb) * N_EXP * (GRAN - 1)
    return -(-worst_rows // FFN_TM) + N_EXP


def _route_kernel(rr_ref, u_ref, lt_ref, srow_ref, col_ref, gd_ref, meta_ref, mg_ref, part_ref,
                  *, NT, tb, TM):
    iota_e = lax.broadcasted_iota(jnp.int32, (N_EXP, tb), 0).astype(F32)
    glane = lax.broadcasted_iota(jnp.int32, (N_EXP, LANES), 1).astype(F32)
    ltri = lt_ref[...]

    def prefix_e(col):
        return jnp.dot(ltri, jnp.broadcast_to(col, (N_EXP, LANES)),
                       preferred_element_type=F32, precision=HIGHEST)[:, 0:1]

    def p1(j, run8):
        r = rr_ref[j]
        oh0 = jnp.where(iota_e == r[0:1, :], 1.0, 0.0)
        oh1 = jnp.where(iota_e == r[1:2, :], 1.0, 0.0)
        cum0 = jnp.dot(oh0.astype(BF16), u_ref[...], preferred_element_type=F32)
        cum1 = jnp.dot(oh1.astype(BF16), u_ref[...], preferred_element_type=F32)
        c0 = jnp.sum(oh0, axis=1, keepdims=True)
        n8 = jnp.floor((c0 + jnp.sum(oh1, axis=1, keepdims=True) + (GRAN - 1.0)) * (1.0 / GRAN))
        lo8 = prefix_e(n8)
        s0 = jnp.sum(oh0 * (GRAN * lo8 + cum0 - 1.0), axis=0, keepdims=True)
        s1 = jnp.sum(oh1 * (GRAN * lo8 + c0 + cum1 - 1.0), axis=0, keepdims=True)
        info = jnp.concatenate([s0, s1, r[2:4, :], jnp.zeros((SUBLANES - 4, tb), F32)], axis=0)
        srow_ref[j] = info
        col_ref[pl.ds(pl.multiple_of(j * tb, tb), tb), :] = jnp.concatenate(
            [info, jnp.zeros((LANES - SUBLANES, tb), F32)], axis=0).T
        mg = jnp.where((lo8 <= glane) & (glane < lo8 + n8), 1.0, 0.0)
        mg_ref[j] = mg
        part = jnp.sum(mg * (run8 + glane - lo8), axis=0, keepdims=True)
        gcnt = jnp.broadcast_to(jnp.sum(n8, axis=0, keepdims=True), (1, LANES))
        part_ref[j] = jnp.concatenate([part, gcnt, jnp.zeros((SUBLANES - 2, LANES), F32)], axis=0)
        return run8 + n8

    tot8 = lax.fori_loop(0, NT, p1, jnp.zeros((N_EXP, 1), F32), unroll=8 if NT % 8 == 0 else 1)
    seg_t = jnp.floor((tot8 * GRAN + (TM - 1.0)) * (1.0 / TM))
    base_t = prefix_e(seg_t)
    base8 = base_t * (TM // GRAN)
    lane1 = lax.broadcasted_iota(jnp.int32, (1, LANES), 1)

    def p2(j, carry):
        pr = part_ref[j]
        dst = (pr[0:1, :] + jnp.sum(mg_ref[j] * base8, axis=0, keepdims=True)) * GRAN
        gd_ref[j] = jnp.where(lane1 == G_LAST, pr[1:2, :], dst).astype(jnp.int32)
        return carry

    lax.fori_loop(0, NT, p2, 0, unroll=8 if NT % 8 == 0 else 1)
    eye = jnp.where(glane == lax.broadcasted_iota(jnp.int32, (N_EXP, LANES), 0).astype(F32), 1.0, 0.0)
    tail_row = jnp.sum(eye * ((base8 + tot8) * GRAN), axis=0, keepdims=True)
    tail_n8 = jnp.sum(eye * (seg_t * (TM // GRAN) - tot8), axis=0, keepdims=True)
    nv_l = jnp.broadcast_to(jnp.sum(seg_t, axis=0, keepdims=True), (1, LANES))
    gd_ref[NT] = jnp.where(lane1 == G_LAST, nv_l, tail_row).astype(jnp.int32)
    gd_ref[NT + 1] = tail_n8.astype(jnp.int32)
    ti = lax.broadcasted_iota(jnp.int32, (N_EXP, tb), 1).astype(F32)
    te = jnp.sum(jnp.where(base_t <= ti, 1.0, 0.0), axis=0, keepdims=True) - 1.0
    nv = jnp.broadcast_to(jnp.sum(seg_t, axis=0, keepdims=True), (1, tb))
    own = jnp.where((base_t <= ti) & (ti < base_t + seg_t), 1.0, 0.0)
    vr = jnp.sum(own * jnp.clip(tot8 * GRAN - (ti - base_t) * TM, 0.0, TM), axis=0, keepdims=True)
    meta_ref[...] = jnp.concatenate([te, nv, vr, jnp.zeros((SUBLANES - 3, tb), F32)],
                                    axis=0).astype(jnp.int32)


def _route(rrow, u_cnt, ltri, *, TM):
    NT, _, tb = rrow.shape
    kern = functools.partial(_route_kernel, NT=NT, tb=tb, TM=TM)
    full3 = lambda i: (0, 0, 0)
    return pl.pallas_call(
        kern,
        grid=(1,),
        in_specs=[pl.BlockSpec((NT, SUBLANES, tb), full3),
                  pl.BlockSpec((tb, tb), lambda i: (0, 0)),
                  pl.BlockSpec((N_EXP, N_EXP), lambda i: (0, 0))],
        out_specs=[pl.BlockSpec((NT, SUBLANES, tb), full3),
                   pl.BlockSpec((NT * tb, LANES), lambda i: (0, 0)),
                   pl.BlockSpec((NT + 2, 1, LANES), full3),
                   pl.BlockSpec((SUBLANES, tb), lambda i: (0, 0))],
        out_shape=[jax.ShapeDtypeStruct((NT, SUBLANES, tb), F32),
                   jax.ShapeDtypeStruct((NT * tb, LANES), F32),
                   jax.ShapeDtypeStruct((NT + 2, 1, LANES), jnp.int32),
                   jax.ShapeDtypeStruct((SUBLANES, tb), jnp.int32)],
        scratch_shapes=[pltpu.VMEM((NT, N_EXP, LANES), F32), pltpu.VMEM((NT, SUBLANES, LANES), F32)],
        compiler_params=_cparams(),
        name="route",
    )(rrow, u_cnt, ltri)


_HI_MASK = 0xFFFF0000


def _pack_halves(x):
    c = x.shape[1] // 2
    lo = lax.bitcast_convert_type(x[:, :c], U32)
    hi = lax.bitcast_convert_type(x[:, c:], U32)
    return (lo >> 16) | (hi & U32(_HI_MASK))


def _unpack_halves(w):
    lo = lax.bitcast_convert_type(w << 16, F32)
    hi = lax.bitcast_convert_type(w & U32(_HI_MASK), F32)
    return jnp.concatenate([lo, hi], axis=1).astype(BF16)


def _granule_copy(src_ref, src_row, dst_ref, dst_row, sem, n=1):
    cols = pl.ds(0, min(src_ref.shape[-1], dst_ref.shape[-1]))
    return pltpu.make_async_copy(src_ref.at[pl.ds(src_row, n * GRAN), cols],
                                 dst_ref.at[pl.ds(dst_row, n * GRAN), cols], sem)


def _for_granules(n, body, unroll=8):
    def blk(i, carry):
        for t in range(unroll):
            body(i * unroll + t)
        return carry

    def one(g, carry):
        body(g)
        return carry

    nblk = n // unroll
    lax.fori_loop(0, nblk, blk, 0)
    lax.fori_loop(nblk * unroll, n, one, 0)


def _wait_granules(n, src_ref, dst_ref, sem, n_max):
    b = 1
    while b <= n_max:
        @pl.when((n & b) != 0)
        def _(b=b):
            _granule_copy(src_ref, 0, dst_ref, 0, sem, n=b).wait()
        b *= 2


def _dispatch_kernel(gd_ref, srow_ref, u_ref, xs_ref, buf, zbuf, sems, *, NT, SL, TM, n_tiles):
    j = pl.program_id(0)
    slot = j % 2
    zsem = sems.at[2]

    def drain(tile, sl):
        _wait_granules(gd_ref[tile, G_LAST], buf.at[sl], xs_ref, sems.at[sl], SL // GRAN)

    def tile_fill(t):
        return pltpu.make_async_copy(zbuf, xs_ref.at[pl.ds(pl.multiple_of(t * TM, TM), TM), :], zsem)

    def zero_fill(wait):
        for e in range(N_EXP):
            n, row0 = gd_ref[NT + 1, e], gd_ref[NT, e]
            b = TM // GRAN // 2
            while b >= 1:
                @pl.when((n & b) != 0)
                def _(b=b, n=n, row0=row0):
                    start = pl.multiple_of(row0 + ((n >> b.bit_length()) << b.bit_length()) * GRAN, GRAN)
                    cp = pltpu.make_async_copy(zbuf.at[pl.ds(0, b * GRAN), :],
                                               xs_ref.at[pl.ds(start, b * GRAN), :], zsem)
                    cp.wait() if wait else cp.start()
                b //= 2

        def zt(t, carry):
            tile_fill(t).wait() if wait else tile_fill(t).start()
            return carry
        lax.fori_loop(gd_ref[NT, G_LAST], n_tiles, zt, 0)

    @pl.when(j == 0)
    def _():
        zbuf[...] = jnp.zeros_like(zbuf)
        zero_fill(False)

    @pl.when(j >= 2)
    def _():
        drain(j - 2, slot)

    s = srow_ref[0]
    rows = lax.broadcasted_iota(jnp.int32, (SL, s.shape[1]), 0).astype(F32)
    m0 = rows == s[0:1, :]
    m1 = rows == s[1:2, :]
    oh = jnp.where(m0 | m1, 1.0, 0.0).astype(BF16)
    dw = u_ref.shape[1] // 2
    buf[slot, :, 0:dw] = _pack_halves(jnp.dot(oh, u_ref[...], preferred_element_type=F32))
    wrow = jnp.sum(jnp.where(m0, s[2:3, :], 0.0) + jnp.where(m1, s[3:4, :], 0.0), axis=1, keepdims=True)
    buf[slot, :, dw:dw + LANES] = lax.bitcast_convert_type(jnp.broadcast_to(wrow, (SL, LANES)), U32)

    def issue(g):
        _granule_copy(buf.at[slot], pl.multiple_of(g * GRAN, GRAN), xs_ref,
                      pl.multiple_of(gd_ref[j, g], GRAN), sems.at[slot]).start()

    _for_granules(gd_ref[j, G_LAST], issue)

    @pl.when(j == NT - 1)
    def _():
        drain(j, slot)
        if NT > 1:
            drain(j - 1, 1 - slot)
        zero_fill(True)


def _dispatch(gd, srow, u2, *, n_tiles, TM):
    N, D = u2.shape
    NT, _, tb = srow.shape
    SL = _slots_per_tile(tb)
    n_rows = n_tiles * TM
    kern = functools.partial(_dispatch_kernel, NT=NT, SL=SL, TM=TM, n_tiles=n_tiles)
    grid_spec = pltpu.PrefetchScalarGridSpec(
        num_scalar_prefetch=1,
        grid=(NT,),
        in_specs=[pl.BlockSpec((1, SUBLANES, tb), lambda j, gd: (j, 0, 0)),
                  pl.BlockSpec((tb, D), lambda j, gd: (j, 0))],
        out_specs=pl.BlockSpec(memory_space=pl.ANY),
        scratch_shapes=[pltpu.VMEM((2, SL, D // 2 + LANES), U32), pltpu.VMEM((TM, D // 2 + LANES), U32),
                        pltpu.SemaphoreType.DMA((3,))],
    )
    return pl.pallas_call(
        kern,
        grid_spec=grid_spec,
        out_shape=jax.ShapeDtypeStruct((n_rows, D // 2 + LANES), U32),
        compiler_params=_cparams(),
        name="dispatch",
    )(gd, srow, u2)


def _ffn_kernel(meta_ref, xs_ref, wg_ref, wu_ref, wd_ref, o_ref, wgb, wub, wdb, sg, su, sd, slot_ref, sems):
    i = pl.program_id(0)
    nv = meta_ref[META_USED, 0]
    tile_expert = lambda t: meta_ref[META_EXPERT, t]
    e = tile_expert(i)

    def weight_copies(ex, sl):
        return (pltpu.make_async_copy(wg_ref.at[ex], sg.at[sl], sems.at[sl]),
                pltpu.make_async_copy(wu_ref.at[ex], su.at[sl], sems.at[sl]),
                pltpu.make_async_copy(wd_ref.at[ex], sd.at[sl], sems.at[sl]))

    @pl.when(i == 0)
    def _():
        slot_ref[0] = 0
        for cp in weight_copies(e, 0):
            cp.start()

    new_expert = (i < nv) & ((i == 0) | (e != tile_expert(jnp.maximum(i - 1, 0))))

    @pl.when(new_expert)
    def _():
        sl = slot_ref[0]
        for cp in weight_copies(e, sl):
            cp.wait()
        nxt = lax.while_loop(lambda t: (t < nv) & (tile_expert(jnp.minimum(t, nv - 1)) == e), lambda t: t + 1, i + 1)

        @pl.when(nxt < nv)
        def _():
            for cp in weight_copies(tile_expert(nxt), 1 - sl):
                cp.start()

    hm = xs_ref.shape[0] // FFN_SUB
    dw = xs_ref.shape[1] - LANES

    def swiglu_rows(nsub, cast):
        if cast:
            sl = slot_ref[0]
            wgb[...] = sg[sl].astype(BF16)
            wub[...] = su[sl].astype(BF16)
            wdb[...] = sd[sl].astype(BF16)
            slot_ref[0] = 1 - sl
        halves = tuple(slice(hm * j, hm * (j + 1)) for j in range(nsub))
        x = [_unpack_halves(xs_ref[r, 0:dw]) for r in halves]
        g = [jnp.dot(x[j], wgb[...], preferred_element_type=F32) for j in range(nsub)]
        u = [jnp.dot(x[j], wub[...], preferred_element_type=F32) for j in range(nsub)]
        h = [(g[j] * _sigmoid(g[j]) * u[j]).astype(BF16) for j in range(nsub)]
        y = [jnp.dot(h[j], wdb[...], preferred_element_type=F32) for j in range(nsub)]
        for j in range(nsub):
            wt = lax.bitcast_convert_type(xs_ref[halves[j], dw:dw + LANES], F32)
            yw = y[j] * jnp.concatenate([wt] * (2 * dw // LANES), axis=1)
            o_ref[halves[j], 0:dw] = _pack_halves(yw.astype(BF16).astype(F32))
            o_ref[halves[j], dw:dw + LANES] = xs_ref[halves[j], dw:dw + LANES]
        if nsub < FFN_SUB:
            o_ref[hm * nsub:, :] = jnp.zeros((hm * (FFN_SUB - nsub), dw + LANES), U32)

    used = meta_ref[META_ROWS, i]
    for nsub in range(1, FFN_SUB + 1):
        lo, hi = hm * (nsub - 1), hm * nsub
        rows_here = (i < nv) & (used > lo) & ((used <= hi) if nsub < FFN_SUB else True)
        pl.when(rows_here & new_expert)(functools.partial(swiglu_rows, nsub, True))
        pl.when(rows_here & jnp.logical_not(new_expert))(functools.partial(swiglu_rows, nsub, False))


def _ffn(meta, xs, wg, wu, wd, *, TM):
    P, XW = xs.shape
    DW = XW - LANES
    D = 2 * DW
    n_tiles = P // TM
    assert n_tiles <= meta.shape[1]
    used_tile = lambda i, meta: (jnp.maximum(jnp.minimum(i, meta[META_USED, 0] - 1), 0), 0)
    grid_spec = pltpu.PrefetchScalarGridSpec(
        num_scalar_prefetch=1,
        grid=(n_tiles,),
        in_specs=[pl.BlockSpec((TM, XW), used_tile),
                  pl.BlockSpec(memory_space=pl.ANY),
                  pl.BlockSpec(memory_space=pl.ANY),
                  pl.BlockSpec(memory_space=pl.ANY)],
        out_specs=pl.BlockSpec((TM, XW), used_tile),
        scratch_shapes=[pltpu.VMEM((D, D_EXP), BF16), pltpu.VMEM((D, D_EXP), BF16),
                        pltpu.VMEM((D_EXP, D), BF16),
                        pltpu.VMEM((2, D, D_EXP), F32), pltpu.VMEM((2, D, D_EXP), F32),
                        pltpu.VMEM((2, D_EXP, D), F32), pltpu.SMEM((1,), jnp.int32),
                        pltpu.SemaphoreType.DMA((2,))],
    )
    return pl.pallas_call(
        _ffn_kernel,
        grid_spec=grid_spec,
        out_shape=jax.ShapeDtypeStruct((P, XW), U32),
        input_output_aliases={1: 0},
        compiler_params=_cparams(),
        name="ffn",
    )(meta, xs, wg, wu, wd)


def _combine_kernel(gd_ref, ys_ref, col_ref, x1_ref, mod_ref, g_ref, b_ref, o_ref, buf, sems, *, NT, SL):
    j = pl.program_id(0)
    slot = j % 2

    def fetch(tile, sl):
        def f(g):
            _granule_copy(ys_ref, pl.multiple_of(gd_ref[tile, g], GRAN), buf.at[sl],
                          pl.multiple_of(g * GRAN, GRAN), sems.at[sl]).start()
        _for_granules(gd_ref[tile, G_LAST], f)

    @pl.when(j == 0)
    def _():
        fetch(0, 0)

    @pl.when(j + 1 < NT)
    def _():
        fetch(j + 1, 1 - slot)

    ng = gd_ref[j, G_LAST]

    _wait_granules(ng, ys_ref, buf.at[slot], sems.at[slot], SL // GRAN)

    rows = lax.broadcasted_iota(jnp.int32, (SL, 1), 0)
    yb = _unpack_halves(jnp.where(rows < ng * GRAN, buf[slot], U32(0)))
    col = col_ref[...]
    tb = col.shape[0]
    lanes = lax.broadcasted_iota(jnp.int32, (tb, SL), 1).astype(F32)
    sel = jnp.where((lanes == col[:, 0:1]) | (lanes == col[:, 1:2]), 1.0, 0.0).astype(BF16)
    y = jnp.dot(sel, yb, preferred_element_type=F32)
    mod = mod_ref[0]
    z = ALPHA * x1_ref[...] + (1.0 + mod[5:6, :]) * y
    o_ref[...] = _layer_norm(z, g_ref[...], b_ref[...])


def _combine(gd, ys, col, x1, mod3, g, b, *, S, tb):
    N, D = x1.shape
    NT = N // tb
    tpb = S // tb
    SL = _slots_per_tile(tb)
    kern = functools.partial(_combine_kernel, NT=NT, SL=SL)
    grid_spec = pltpu.PrefetchScalarGridSpec(
        num_scalar_prefetch=1,
        grid=(NT,),
        in_specs=[pl.BlockSpec(memory_space=pl.ANY),
                  pl.BlockSpec((tb, LANES), lambda j, gd: (j, 0)),
                  pl.BlockSpec((tb, D), lambda j, gd: (j, 0)),
                  pl.BlockSpec((1, 6, D), lambda j, gd: (j // tpb, 0, 0)),
                  pl.BlockSpec((1, D), lambda j, gd: (0, 0)),
                  pl.BlockSpec((1, D), lambda j, gd: (0, 0))],
        out_specs=pl.BlockSpec((tb, D), lambda j, gd: (j, 0)),
        scratch_shapes=[pltpu.VMEM((2, SL, D // 2), U32), pltpu.SemaphoreType.DMA((2,))],
    )
    return pl.pallas_call(
        kern,
        grid_spec=grid_spec,
        out_shape=jax.ShapeDtypeStruct((N, D), F32),
        compiler_params=_cparams(),
        name="combine",
    )(gd, ys, col, x1, mod3, g, b)


def _layer(x, c, l, w_ada, b_ada, w_in, w_conv, b_conv, b_igate, b_fgate, mlstm_norm_g, w_gla_a, b_gla_a,
           gla_norm_g, w_out, ln1_g, ln1_b, w_route_group, b_route_group, w_route_expert, b_route_expert,
           w_gate, w_up, w_down, ln2_g, ln2_b):
    B, S, D = x.shape
    N = B * S
    x2 = x.reshape(N, D)
    tm_in = min(512, S)
    tm = min(256, S)
    lm = min(256, S)
    assert S % tm_in == 0 and S % tm == 0 and tm_in % lm == 0 and S % G_CHUNK == 0
    assert w_in.shape[1:] == (D, IN_TOT) and w_gate.shape[1:] == (N_EXP, D, D_EXP)

    mod3 = _ada(c, w_ada[l], b_ada[l]).reshape(B, 6, D)

    wa_pad = jnp.pad(w_gla_a[l], ((SM_A, LANES - SM_A - G_RANK), (0, 0))).astype(BF16)
    bg = jnp.concatenate([jnp.pad(b_igate[l], (0, SUBLANES - M_HEADS)),
                          jnp.pad(b_fgate[l], (0, SUBLANES - M_HEADS))]).reshape(2 * SUBLANES, 1)
    oa, la, g3 = _inproj(x2, mod3, jnp.swapaxes(w_in, 1, 2), w_conv[l], b_conv[l].reshape(1, -1), wa_pad,
                         b_gla_a[l].reshape(1, -1), bg, S=S, tm=tm_in, lm=lm, layer=l)

    u_tri = jnp.asarray(np.triu(np.ones((lm, lm), np.float32)))
    nb = 4 if B % 4 == 0 else (2 if B % 2 == 0 else 1)
    ts = min(512, S)
    hm = _mlstm(oa, g3, u_tri, mlstm_norm_g[l].reshape(1, -1), B=B, S=S, L=lm, nb=nb, ts=ts)
    w3_np, mk_np = _gla_consts()
    hg = _gla(oa, la, jnp.asarray(w3_np, BF16), jnp.asarray(mk_np), gla_norm_g[l].reshape(1, -1), B=B, S=S,
              nb=nb, ts=ts)

    br = jnp.concatenate([jnp.pad(b_route_group[l], (0, SUBLANES - N_GROUPS)),
                          jnp.pad(b_route_expert[l], (0, LANES - SUBLANES - N_EXP))]).reshape(1, LANES)
    x1, u2, rrow = _outproj(hm, hg, w_out[l], x2, mod3, ln1_g[l].reshape(1, -1), ln1_b[l].reshape(1, -1),
                            jnp.swapaxes(w_route_group, 1, 2), jnp.swapaxes(w_route_expert, 1, 2), br,
                            S=S, tb=tm, nh=4 if S % (4 * tm) == 0 else 1, layer=l)

    u_cnt = jnp.asarray(np.triu(np.ones((tm, tm), np.float32)), BF16)
    ltri = jnp.asarray(np.tril(np.ones((N_EXP, N_EXP), np.float32), -1))
    srow, col, gd3, meta = _route(rrow, u_cnt, ltri, TM=FFN_TM)
    gd = gd3.reshape(N // tm + 2, LANES)
    n_tiles = _ffn_tiles(N, tm)

    xs = _dispatch(gd, srow, u2, n_tiles=n_tiles, TM=FFN_TM)
    ys = _ffn(meta, xs, w_gate[l], w_up[l], w_down[l], TM=FFN_TM)
    out = _combine(gd, ys, col, x1, mod3, ln2_g[l].reshape(1, -1), ln2_b[l].reshape(1, -1), S=S, tb=tm)
    return out.reshape(B, S, D)


def kernel(x, c, w_ada, b_ada, w_in, w_conv, b_conv, b_igate, b_fgate, mlstm_norm_g, w_gla_a, b_gla_a,
           gla_norm_g, w_out, ln1_g, ln1_b, w_route_group, b_route_group, w_route_expert, b_route_expert,
           w_gate, w_up, w_down, ln2_g, ln2_b):
    for l in range(DEPTH):
        x = _layer(x, c, l, w_ada, b_ada, w_in, w_conv, b_conv, b_igate, b_fgate, mlstm_norm_g, w_gla_a,
                   b_gla_a, gla_norm_g, w_out, ln1_g, ln1_b, w_route_group, b_route_group, w_route_expert,
                   b_route_expert, w_gate, w_up, w_down, ln2_g, ln2_b)
    return x
```

```python
import functools

import numpy as np
import jax
import jax.numpy as jnp
from jax import lax
from jax.experimental import pallas as pl
from jax.experimental.pallas import tpu as pltpu

F32 = jnp.float32
BF16 = jnp.bfloat16
U32 = jnp.uint32
HIGHEST = lax.Precision.HIGHEST

DEPTH = 1
M_HEADS = 4
M_HD = 128
M_W = M_HEADS * M_HD
CONV_W = 4
G_HEADS = 4
G_DK = 64
G_DV = 128
G_W = G_HEADS * G_DV
G_KW = G_HEADS * G_DK
G_RANK = 16
G_TAU = 16.0
G_CHUNK = 64
N_GROUPS = 4
E_PER_G = 8
N_EXP = N_GROUPS * E_PER_G
D_EXP = 512
ALPHA = (2 * DEPTH) ** 0.25
LN_EPS = 1e-5

LANES = 128
SUBLANES = 8
VMEM_LIMIT = 48 * 1024 * 1024

C_QK = 0
C_VO = 1024
C_GQK = 2048
C_GV = 2560
C_GG = 3072
C_SMALL = 3584
C_TOT = 3712
SM_I, SM_F, SM_A = 0, 8, 16
IN_GATES = 4 * M_W
IN_G = IN_GATES + 2 * M_HEADS
IN_GA = IN_G + 2 * G_KW + 2 * G_W
IN_TOT = IN_GA + G_RANK

FFN_TM = 512
FFN_SUB = 2
GRAN = SUBLANES
G_LAST = LANES - 1
TILE_EXPERT, TILES_USED, TILE_ROWS = 0, 1, 2


def _cparams(n_axes=1):
    return pltpu.CompilerParams(dimension_semantics=("arbitrary",) * n_axes,
                                vmem_limit_bytes=VMEM_LIMIT)


def _sigmoid(x):
    return 1.0 / (1.0 + jnp.exp(-x))


def _log_sigmoid(x):
    return jnp.minimum(x, 0.0) - jnp.log(1.0 + jnp.exp(-jnp.abs(x)))


def _ada_kernel(c_ref, w_ref, b_ref, o_ref):
    c = c_ref[...]
    ca = (c * _sigmoid(c)).astype(BF16)
    o_ref[...] = jnp.dot(ca, w_ref[...].astype(BF16), preferred_element_type=F32) + b_ref[...]


def _ada(c, w, b):
    B, D = c.shape
    n_out = w.shape[1]
    tn = 1024
    return pl.pallas_call(
        _ada_kernel,
        grid=(n_out // tn,),
        in_specs=[pl.BlockSpec((B, D), lambda j: (0, 0)),
                  pl.BlockSpec((D, tn), lambda j: (0, j)),
                  pl.BlockSpec((1, tn), lambda j: (0, j))],
        out_specs=pl.BlockSpec((B, tn), lambda j: (0, j)),
        out_shape=jax.ShapeDtypeStruct((B, n_out), F32),
        compiler_params=_cparams(),
        name="ada",
    )(c, w, b.reshape(1, n_out))


def _inproj_kernel(x_ref, mod_ref, win_ref, wc_ref, bc_ref, wa_ref, ba_ref, bg_ref,
                   oa_ref, la_ref, g_ref, halo_ref, w_ref, *, tm, tpb, lm):
    i = pl.program_id(0)

    @pl.when(i == 0)
    def _():
        rc = 2 * LANES
        for r in range(0, IN_GATES, rc):
            w_ref[:, r:r + rc] = win_ref[0, r:r + rc, :].T.astype(BF16)
        for r in range(0, C_SMALL - C_GQK, rc):
            w_ref[:, C_GQK + r:C_GQK + r + rc] = win_ref[0, IN_G + r:IN_G + r + rc, :].T.astype(BF16)
        gates = win_ref[0, IN_GATES:IN_G, :]
        z = lambda n: jnp.zeros((n, gates.shape[1]), F32)
        small = jnp.concatenate([gates[0:M_HEADS], z(SM_F - M_HEADS), gates[M_HEADS:2 * M_HEADS],
                                 z(SM_A - SM_F - M_HEADS), win_ref[0, IN_GA:IN_TOT, :],
                                 z(LANES - SM_A - G_RANK)], axis=0)
        w_ref[:, C_SMALL:C_TOT] = small.T.astype(BF16)

    @pl.when(i % tpb == 0)
    def _():
        halo_ref[0:SUBLANES, :] = jnp.zeros((SUBLANES, halo_ref.shape[1]), F32)

    mod = mod_ref[0]
    u = (x_ref[...] * (1.0 + mod[1:2, :]) + mod[0:1, :]).astype(BF16)

    def proj(c0, c1):
        return jnp.dot(u, w_ref[:, c0:c1], preferred_element_type=F32)

    p = proj(C_QK, C_QK + 2 * M_W)
    halo_ref[SUBLANES:SUBLANES + tm, :] = p
    acc = bc_ref[...] + wc_ref[CONV_W - 1:CONV_W, :] * p
    for j in range(CONV_W - 1):
        acc = acc + wc_ref[j:j + 1, :] * halo_ref[pl.ds(SUBLANES - (CONV_W - 1) + j, tm), :]
    halo_ref[0:SUBLANES, :] = p[tm - SUBLANES:, :]
    qk = acc * _sigmoid(acc)
    oa_ref[:, C_QK:C_QK + M_W] = qk[:, :M_W].astype(BF16)
    oa_ref[:, C_QK + M_W:C_QK + 2 * M_W] = (qk[:, M_W:] * (M_HD ** -0.5)).astype(BF16)

    ps = proj(C_SMALL, C_TOT)
    la = jnp.dot(ps.astype(BF16), wa_ref[...], preferred_element_type=F32) + ba_ref[...]
    la_ref[...] = _log_sigmoid(la) * (1.0 / G_TAU)
    pt = ps.T
    gi = pt[SM_I:SM_I + SUBLANES, :] + bg_ref[0:SUBLANES, :]
    gf = _log_sigmoid(pt[SM_F:SM_F + SUBLANES, :] + bg_ref[SUBLANES:2 * SUBLANES, :])
    for j in range(tm // lm):
        g_ref[j, 0:SUBLANES, :] = gi[:, j * lm:(j + 1) * lm]
        g_ref[j, SUBLANES:2 * SUBLANES, :] = gf[:, j * lm:(j + 1) * lm]

    p = proj(C_VO, C_VO + 2 * M_W)
    oa_ref[:, C_VO:C_VO + 2 * M_W] = p.astype(BF16)

    p = proj(C_GQK, C_GQK + G_KW)
    oa_ref[:, C_GQK:C_GQK + G_KW] = (p * (G_DK ** -0.5)).astype(BF16)
    p = proj(C_GQK + G_KW, C_SMALL)
    oa_ref[:, C_GQK + G_KW:C_SMALL] = p.astype(BF16)


def _inproj(x2, mod3, w_in, w_conv, b_conv, wa_pad, b_gla, bg, *, S, tm, lm, layer):
    N, D = x2.shape
    tpb = S // tm
    kern = functools.partial(_inproj_kernel, tm=tm, tpb=tpb, lm=lm)
    return pl.pallas_call(
        kern,
        grid=(N // tm,),
        in_specs=[pl.BlockSpec((tm, D), lambda i: (i, 0)),
                  pl.BlockSpec((1, 6, D), lambda i: (i // tpb, 0, 0)),
                  pl.BlockSpec((1, IN_TOT, D), lambda i: (layer, 0, 0), pipeline_mode=pl.Buffered(1)),
                  pl.BlockSpec((CONV_W, 2 * M_W), lambda i: (0, 0)),
                  pl.BlockSpec((1, 2 * M_W), lambda i: (0, 0)),
                  pl.BlockSpec((LANES, G_KW), lambda i: (0, 0)),
                  pl.BlockSpec((1, G_KW), lambda i: (0, 0)),
                  pl.BlockSpec((2 * SUBLANES, 1), lambda i: (0, 0))],
        out_specs=[pl.BlockSpec((tm, C_SMALL), lambda i: (i, 0)),
                   pl.BlockSpec((tm, G_KW), lambda i: (i, 0)),
                   pl.BlockSpec((tm // lm, 2 * SUBLANES, lm), lambda i: (i, 0, 0))],
        out_shape=[jax.ShapeDtypeStruct((N, C_SMALL), BF16),
                   jax.ShapeDtypeStruct((N, G_KW), F32),
                   jax.ShapeDtypeStruct((N // lm, 2 * SUBLANES, lm), F32)],
        scratch_shapes=[pltpu.VMEM((SUBLANES + tm, 2 * M_W), F32), pltpu.VMEM((D, C_TOT), BF16)],
        compiler_params=_cparams(),
        name="inproj",
    )(x2, mod3, w_in, w_conv, b_conv, wa_pad, b_gla, bg)


def _mlstm_sel():
    sel = np.zeros((2 * LANES, 2 * M_HEADS * M_HD), np.float32)
    for j in range(2 * M_HEADS):
        src = (SUBLANES if j < M_HEADS else 3 * SUBLANES) + j % M_HEADS
        sel[src, M_HD * j:M_HD * (j + 1)] = 1.0
        sel[LANES + src, M_HD * j:M_HD * (j + 1)] = 1.0
    return sel


def _mlstm_kernel(qk_ref, vo_ref, g_ref, u_ref, gain_ref, sel_ref, out_ref, c_ref, zt_ref, a_ref, dec_ref, m_ref,
                  *, L, NC, nb):
    @pl.when(pl.program_id(1) == 0)
    def _():
        c_ref[...] = jnp.zeros_like(c_ref)
        m_ref[...] = jnp.zeros_like(m_ref)

    tril = (lax.broadcasted_iota(jnp.int32, (L, L), 0) >= lax.broadcasted_iota(jnp.int32, (L, L), 1))
    ones_v = jnp.ones((L, M_HD), BF16)
    zpad = jnp.zeros((LANES - 4 * SUBLANES, L), F32)
    zgroup = lax.broadcasted_iota(jnp.int32, (L, LANES), 1) // SUBLANES
    factor_cols = (zgroup == 1) | (zgroup == 3)

    order = [(bi, c) for bi in range(nb) for c in range(NC)]
    f_all = jnp.concatenate([g_ref[bi, c, SUBLANES:2 * SUBLANES, :] for bi, c in order], axis=0)
    i_all = jnp.concatenate([g_ref[bi, c, 0:SUBLANES, :] for bi, c in order], axis=0)
    b_all = jnp.dot(f_all, u_ref[...], preferred_element_type=F32, precision=HIGHEST)
    a_all = i_all - b_all
    lane_all = lax.broadcasted_iota(jnp.int32, a_all.shape, 1)
    g_all = a_all
    s = 1
    while s < L:
        g_all = jnp.maximum(g_all, jnp.where(lane_all >= s, pltpu.roll(g_all, s, 1), -jnp.inf))
        s *= 2
    for bi in range(nb):
        m_prev = m_ref[bi][:, 0:1]
        for c in range(NC):
            ci = bi * NC + c
            r8 = slice(SUBLANES * ci, SUBLANES * (ci + 1))
            a, b = a_all[r8], b_all[r8]
            a_ref[ci] = a
            M = jnp.maximum(g_all[r8], m_prev)
            ML = M[:, L - 1:L]
            Z = jnp.concatenate([M, jnp.exp(m_prev - M), jnp.exp(-(b + M)), jnp.exp(a - ML), zpad],
                                axis=0)
            zt_ref[ci] = Z.T
            dec_ref[ci] = jnp.broadcast_to(jnp.exp(m_prev - ML), (SUBLANES, 2 * M_HD))
            m_prev = b[:, L - 1:L] + ML
        m_ref[bi] = jnp.broadcast_to(m_prev, (SUBLANES, LANES))

    chains = [(bi, h) for bi in range(nb) for h in range(M_HEADS)]
    nt = (((1,), (1,)), ((), ()))
    tn = (((0,), (0,)), ((), ()))

    def chunk(c, carry):
        rows = pl.ds(pl.multiple_of(c * L, L), L)
        Zt = [zt_ref[bi * NC + c] for bi in range(nb)]
        a = [a_ref[bi * NC + c] for bi in range(nb)]
        dec = [dec_ref[bi * NC + c] for bi in range(nb)]
        hs = [slice(h * M_HD, (h + 1) * M_HD) for h in range(M_HEADS)]
        hs2 = [slice(M_W + h * M_HD, M_W + (h + 1) * M_HD) for h in range(M_HEADS)]
        q = [qk_ref[bi, rows, hs[h]] for bi, h in chains]
        k = [qk_ref[bi, rows, hs2[h]] for bi, h in chains]
        vext = [jnp.concatenate([vo_ref[bi, rows, hs[h]], ones_v], axis=1) for bi, h in chains]
        cst = [c_ref[bi * M_HEADS + h] for bi, h in chains]
        n = range(len(chains))
        sc = [lax.dot_general(q[i], k[i], nt, preferred_element_type=F32) for i in n]
        qc = [jnp.dot(q[i], cst[i].astype(BF16), preferred_element_type=F32) for i in n]
        pm = [(sc[i] * jnp.exp(jnp.where(tril, a[bi][h:h + 1, :] - Zt[bi][:, h:h + 1], -jnp.inf))).astype(BF16)
              for i, (bi, h) in enumerate(chains)]
        pv = [jnp.dot(pm[i], vext[i], preferred_element_type=F32) for i in n]
        rep = []
        for bi in range(nb):
            zf = jnp.where(factor_cols, Zt[bi], 0.0)
            zh = zf.astype(BF16)
            zl = (zf - zh.astype(F32)).astype(BF16)
            rep.append(jnp.dot(jnp.concatenate([zh, zl], axis=1), sel_ref[...], preferred_element_type=F32))
        e_inter = [rep[bi][:, M_HD * h:M_HD * (h + 1)] for bi, h in chains]
        w_state = [rep[bi][:, M_HD * (M_HEADS + h):M_HD * (M_HEADS + h + 1)] for bi, h in chains]
        kw = [(w_state[i] * k[i].astype(F32)).astype(BF16) for i in n]
        upd = [lax.dot_general(kw[i], vext[i], tn, preferred_element_type=F32) for i in n]
        for i, (bi, h) in enumerate(chains):
            c_ref[bi * M_HEADS + h] = dec[bi][h:h + 1, :] * cst[i] + upd[i]
            nd = pv[i] + jnp.concatenate([e_inter[i], e_inter[i]], axis=1) * qc[i]
            hh = nd[:, :M_HD] / jnp.maximum(jnp.abs(nd[:, M_HD:]),
                                            Zt[bi][:, 2 * SUBLANES + h:2 * SUBLANES + h + 1])
            hh = _sigmoid(vo_ref[bi, rows, hs2[h]].astype(F32)) * hh
            hn = hh * lax.rsqrt(jnp.mean(hh * hh, axis=-1, keepdims=True) + LN_EPS)
            out_ref[bi, rows, hs[h]] = (hn * gain_ref[:, hs[h]]).astype(BF16)
        return carry

    lax.fori_loop(0, NC, chunk, 0, unroll=True)


def _mlstm(oa, g3, u_tri, gain, *, B, S, L, nb, ts):
    N = oa.shape[0]
    NC = ts // L
    oa3 = oa.reshape(B, S, oa.shape[1])
    g4 = g3.reshape(B, S // L, 2 * SUBLANES, L)
    sel = jnp.asarray(_mlstm_sel(), BF16)
    kern = functools.partial(_mlstm_kernel, L=L, NC=NC, nb=nb)
    out = pl.pallas_call(
        kern,
        grid=(B // nb, S // ts),
        in_specs=[pl.BlockSpec((nb, ts, 2 * M_W), lambda b, t: (b, t, C_QK // (2 * M_W))),
                  pl.BlockSpec((nb, ts, 2 * M_W), lambda b, t: (b, t, C_VO // (2 * M_W))),
                  pl.BlockSpec((nb, NC, 2 * SUBLANES, L), lambda b, t: (b, t, 0, 0)),
                  pl.BlockSpec((L, L), lambda b, t: (0, 0)),
                  pl.BlockSpec((1, M_W), lambda b, t: (0, 0)),
                  pl.BlockSpec(sel.shape, lambda b, t: (0, 0))],
        out_specs=pl.BlockSpec((nb, ts, M_W), lambda b, t: (b, t, 0)),
        out_shape=jax.ShapeDtypeStruct((B, S, M_W), BF16),
        scratch_shapes=[pltpu.VMEM((nb * M_HEADS, M_HD, 2 * M_HD), F32),
                        pltpu.VMEM((nb * NC, L, LANES), F32),
                        pltpu.VMEM((nb * NC, SUBLANES, L), F32),
                        pltpu.VMEM((nb * NC, SUBLANES, 2 * M_HD), F32),
                        pltpu.VMEM((nb, SUBLANES, LANES), F32)],
        compiler_params=_cparams(2),
        name="mlstm",
    )(oa3, oa3, g4, u_tri, gain, sel)
    return out.reshape(N, M_W)


_G_LEVELS = 6
_G_XROW = 2 * G_CHUNK + SUBLANES


def _gla_consts():
    L = G_CHUNK
    t = np.arange(L)
    blocks = [(t[None, :] <= t[:, None]).astype(np.float32),
              (t[None, :] > t[:, None]).astype(np.float32),
              np.ones((SUBLANES, L), np.float32)]
    masks = [np.eye(L, dtype=np.float32)]
    m = 1
    while m < L:
        wl = np.zeros((L, L), np.float32)
        for r in range(L):
            r0 = (r // (2 * m)) * 2 * m + m
            if r % (2 * m) >= m:
                wl[r, r0:r + 1] = 1.0
            else:
                wl[r, r + 1:r0] = 1.0
        blocks.append(wl)
        tt, ss = t[:, None], t[None, :]
        masks.append(((tt // (2 * m) == ss // (2 * m)) & (tt % (2 * m) >= m)
                      & (ss % (2 * m) < m)).astype(np.float32))
        m *= 2
    w = np.concatenate(blocks, axis=0)
    w3 = np.concatenate([w, w, w], axis=1)
    mk = np.stack([np.concatenate([x] * G_HEADS, axis=0) for x in masks])
    return w3, mk


def _gla_kernel(qk_ref, v_ref, gg_ref, la_ref, w3_ref, mk_ref, gain_ref, out_ref, st_ref, *, NC, nb):
    L = G_CHUNK

    @pl.when(pl.program_id(1) == 0)
    def _():
        st_ref[...] = jnp.zeros_like(st_ref)

    lane_head = lax.broadcasted_iota(jnp.int32, (L, G_KW), 1) // G_DK
    br = lax.broadcasted_iota(jnp.int32, (2 * G_DV, LANES), 0) < G_DV
    bl = lax.broadcasted_iota(jnp.int32, (2 * G_DV, LANES), 1) < G_DK
    bmask = br == bl
    nt = (((1,), (1,)), ((), ()))
    tn = (((0,), (0,)), ((), ()))

    def chunk(c, carry):
        rows = pl.ds(pl.multiple_of(c * L, L), L)
        X, q, k = [], [], []
        for bi in range(nb):
            la = la_ref[bi, rows, :]
            hi = la.astype(BF16)
            r1 = la - hi.astype(F32)
            mid = r1.astype(BF16)
            lo = (r1 - mid.astype(F32)).astype(BF16)
            stk = jnp.concatenate([hi, mid, lo], axis=0)
            X.append(jnp.exp(jnp.dot(w3_ref[...], stk, preferred_element_type=F32)))
            q.append(qk_ref[bi, rows, 0:G_KW].astype(F32))
            k.append(qk_ref[bi, rows, G_KW:2 * G_KW].astype(F32))

        sc = [[None] * (_G_LEVELS + 1) for _ in range(nb)]
        for lev in range(_G_LEVELS + 1):
            for bi in range(nb):
                if lev == 0:
                    qt, kt = q[bi], k[bi]
                else:
                    xl = X[bi][_G_XROW + L * (lev - 1):_G_XROW + L * lev, :]
                    qt, kt = q[bi] * xl, k[bi] * xl
                q4 = jnp.concatenate([jnp.where(lane_head == h, qt, 0.0) for h in range(G_HEADS)],
                                     axis=0).astype(BF16)
                sc[bi][lev] = lax.dot_general(q4, kt.astype(BF16), nt, preferred_element_type=F32)
        Ab = []
        for bi in range(nb):
            A = sc[bi][0] * mk_ref[0]
            for lev in range(1, _G_LEVELS + 1):
                A = A + sc[bi][lev] * mk_ref[lev]
            Ab.append(A.astype(BF16))

        for bi in range(nb):
            gg = gg_ref[bi, rows, :].astype(F32)
            gate = gg * _sigmoid(gg)
            for p in range(2):
                ls = slice(LANES * p, LANES * (p + 1))
                vp = v_ref[bi, rows, 2 * G_DV * p:2 * G_DV * (p + 1)]
                oi = [jnp.dot(Ab[bi][L * (2 * p + hh):L * (2 * p + hh + 1)],
                              vp[:, G_DV * hh:G_DV * (hh + 1)], preferred_element_type=F32)
                      for hh in range(2)]
                st = st_ref[bi, p]
                qc = (q[bi][:, ls] * X[bi][0:L, ls]).astype(BF16)
                o_inter = lax.dot_general(qc, st.astype(BF16), nt, preferred_element_type=F32)
                kc = (k[bi][:, ls] * X[bi][L:2 * L, ls]).astype(BF16)
                upd = lax.dot_general(vp, kc, tn, preferred_element_type=F32)
                dec = X[bi][2 * L:2 * L + 1, ls]
                st_ref[bi, p] = jnp.where(bmask, dec * st + upd, 0.0)
                for hh in range(2):
                    o = o_inter[:, G_DV * hh:G_DV * (hh + 1)] + oi[hh]
                    hn = o * lax.rsqrt(jnp.mean(o * o, axis=-1, keepdims=True) + LN_EPS)
                    hs = slice(G_DV * (2 * p + hh), G_DV * (2 * p + hh + 1))
                    out_ref[bi, rows, hs] = (hn * gain_ref[:, hs] * gate[:, hs]).astype(BF16)
        return carry

    lax.fori_loop(0, NC, chunk, 0, unroll=2 if NC % 2 == 0 else 1)


def _gla(oa, la, w3, mk, gain, *, B, S, nb, ts):
    N = oa.shape[0]
    oa3 = oa.reshape(B, S, oa.shape[1])
    la3 = la.reshape(B, S, G_KW)
    kern = functools.partial(_gla_kernel, NC=ts // G_CHUNK, nb=nb)
    out = pl.pallas_call(
        kern,
        grid=(B // nb, S // ts),
        in_specs=[pl.BlockSpec((nb, ts, 2 * G_KW), lambda b, t: (b, t, C_GQK // (2 * G_KW))),
                  pl.BlockSpec((nb, ts, G_W), lambda b, t: (b, t, C_GV // G_W)),
                  pl.BlockSpec((nb, ts, G_W), lambda b, t: (b, t, C_GG // G_W)),
                  pl.BlockSpec((nb, ts, G_KW), lambda b, t: (b, t, 0)),
                  pl.BlockSpec(w3.shape, lambda b, t: (0, 0)),
                  pl.BlockSpec(mk.shape, lambda b, t: (0, 0, 0)),
                  pl.BlockSpec((1, G_W), lambda b, t: (0, 0))],
        out_specs=pl.BlockSpec((nb, ts, G_W), lambda b, t: (b, t, 0)),
        out_shape=jax.ShapeDtypeStruct((B, S, G_W), BF16),
        scratch_shapes=[pltpu.VMEM((nb, 2, 2 * G_DV, LANES), F32)],
        compiler_params=_cparams(2),
        name="gla",
    )(oa3, oa3, oa3, la3, w3, mk, gain)
    return out.reshape(N, G_W)


def _layer_norm(z, g, b):
    mu = jnp.mean(z, axis=-1, keepdims=True)
    zc = z - mu
    var = jnp.mean(zc * zc, axis=-1, keepdims=True)
    return zc * lax.rsqrt(var + LN_EPS) * g + b


def _outproj_kernel(hm_ref, hg_ref, wf_ref, x_ref, mod_ref, g_ref, b_ref, wrg_ref, wre_ref, br_ref,
                    x1_ref, u2_ref, rrow_ref, w_ref, wr_ref, *, tb, nh):
    @pl.when(pl.program_id(0) == 0)
    def _():
        w_ref[...] = wf_ref[...].astype(BF16)
        z = lambda n: jnp.zeros((n, wrg_ref.shape[2]), F32)
        wt = jnp.concatenate([wrg_ref[0], z(SUBLANES - N_GROUPS), wre_ref[0],
                              z(LANES - SUBLANES - N_EXP)], axis=0).T
        hi = wt.astype(BF16)
        wr_ref[:, 0:LANES] = hi
        wr_ref[:, LANES:2 * LANES] = (wt - hi.astype(F32)).astype(BF16)

    mod = mod_ref[0]
    blocks = [slice(tb * j, tb * (j + 1)) for j in range(nh)]
    y = [jnp.dot(hm_ref[r, :], w_ref[0:M_W, :], preferred_element_type=F32)
         + jnp.dot(hg_ref[r, :], w_ref[M_W:M_W + G_W, :], preferred_element_type=F32) for r in blocks]
    u2 = []
    for j, r in enumerate(blocks):
        z = ALPHA * x_ref[r, :] + (1.0 + mod[2:3, :]) * y[j]
        x1 = _layer_norm(z, g_ref[...], b_ref[...])
        x1_ref[r, :] = x1
        u2.append(x1 * (1.0 + mod[4:5, :]) + mod[3:4, :])
        u2_ref[r, :] = u2[j].astype(BF16)

    u2h = [u.astype(BF16) for u in u2]
    u2l = [(u2[j] - u2h[j].astype(F32)).astype(BF16) for j in range(nh)]
    lh = [jnp.dot(u, wr_ref[...], preferred_element_type=F32) for u in u2h]
    ll = [jnp.dot(u, wr_ref[:, 0:LANES], preferred_element_type=F32) for u in u2l]
    for j in range(nh):
        logits = lh[j][:, 0:LANES] + lh[j][:, LANES:2 * LANES] + ll[j] + br_ref[...]
        rrow_ref[j] = _route_select(logits.T, tb)


def _route_select(lt, tm):
    row = lax.broadcasted_iota(jnp.int32, (SUBLANES, tm), 0)
    gl = jnp.where(row < N_GROUPS, lt[0:SUBLANES, :], -jnp.inf)
    gmax = jnp.max(gl, axis=0, keepdims=True)
    gsel = jnp.min(jnp.where(gl == gmax, row, SUBLANES), axis=0, keepdims=True)
    pg = 1.0 / jnp.sum(jnp.exp(gl - gmax), axis=0, keepdims=True)
    ein = jnp.zeros((SUBLANES, tm), F32)
    for g in range(N_GROUPS):
        ein = jnp.where(gsel == g, lt[SUBLANES * (g + 1):SUBLANES * (g + 2), :], ein)
    v1 = jnp.max(ein, axis=0, keepdims=True)
    i1 = jnp.min(jnp.where(ein == v1, row, SUBLANES), axis=0, keepdims=True)
    rest = jnp.where(row == i1, -jnp.inf, ein)
    v2 = jnp.max(rest, axis=0, keepdims=True)
    i2 = jnp.min(jnp.where(rest == v2, row, SUBLANES), axis=0, keepdims=True)
    t2 = jnp.exp(v2 - v1)
    p1 = 1.0 / (1.0 + t2)
    e0 = (gsel * E_PER_G + i1).astype(F32)
    e1 = (gsel * E_PER_G + i2).astype(F32)
    return jnp.concatenate([e0, e1, pg * p1, pg * (t2 * p1), jnp.zeros((SUBLANES - 4, tm), F32)], axis=0)


def _outproj(hm, hg, w_out, x2, mod3, g, b, wrg_t, wre_t, br, *, S, tb, nh, layer):
    N, D = x2.shape
    tm = tb * nh
    tpb = S // tm
    kern = functools.partial(_outproj_kernel, tb=tb, nh=nh)
    return pl.pallas_call(
        kern,
        grid=(N // tm,),
        in_specs=[pl.BlockSpec((tm, M_W), lambda i: (i, 0)),
                  pl.BlockSpec((tm, G_W), lambda i: (i, 0)),
                  pl.BlockSpec((M_W + G_W, D), lambda i: (0, 0), pipeline_mode=pl.Buffered(1)),
                  pl.BlockSpec((tm, D), lambda i: (i, 0)),
                  pl.BlockSpec((1, 6, D), lambda i: (i // tpb, 0, 0)),
                  pl.BlockSpec((1, D), lambda i: (0, 0)),
                  pl.BlockSpec((1, D), lambda i: (0, 0)),
                  pl.BlockSpec((1, N_GROUPS, D), lambda i: (layer, 0, 0)),
                  pl.BlockSpec((1, N_EXP, D), lambda i: (layer, 0, 0)),
                  pl.BlockSpec((1, LANES), lambda i: (0, 0))],
        out_specs=[pl.BlockSpec((tm, D), lambda i: (i, 0)),
                   pl.BlockSpec((tm, D), lambda i: (i, 0)),
                   pl.BlockSpec((nh, SUBLANES, tb), lambda i: (i, 0, 0))],
        out_shape=[jax.ShapeDtypeStruct((N, D), F32),
                   jax.ShapeDtypeStruct((N, D), BF16),
                   jax.ShapeDtypeStruct((N // tb, SUBLANES, tb), F32)],
        scratch_shapes=[pltpu.VMEM((M_W + G_W, D), BF16), pltpu.VMEM((D, 2 * LANES), BF16)],
        compiler_params=_cparams(),
        name="outproj",
    )(hm, hg, w_out, x2, mod3, g, b, wrg_t, wre_t, br)


def _slots_per_tile(tb):
    worst = 2 * tb + N_EXP * (GRAN - 1)
    return -(-worst // LANES) * LANES


def _ffn_tiles(n_tok, tb):
    worst_rows = 2 * n_tok + (n_tok // tb) * N_EXP * (GRAN - 1)
    return -(-worst_rows // FFN_TM) + N_EXP


def _route_kernel(rr_ref, u_ref, lt_ref, srow_ref, col_ref, gd_ref, meta_ref, mg_ref, part_ref,
                  *, NT, tb, TM):
    iota_e = lax.broadcasted_iota(jnp.int32, (N_EXP, tb), 0).astype(F32)
    glane = lax.broadcasted_iota(jnp.int32, (N_EXP, LANES), 1).astype(F32)
    ltri = lt_ref[...]

    def prefix_e(col):
        return jnp.dot(ltri, jnp.broadcast_to(col, (N_EXP, LANES)),
                       preferred_element_type=F32, precision=HIGHEST)[:, 0:1]

    def p1(j, run8):
        r = rr_ref[j]
        oh0 = jnp.where(iota_e == r[0:1, :], 1.0, 0.0)
        oh1 = jnp.where(iota_e == r[1:2, :], 1.0, 0.0)
        cum0 = jnp.dot(oh0.astype(BF16), u_ref[...], preferred_element_type=F32)
        cum1 = jnp.dot(oh1.astype(BF16), u_ref[...], preferred_element_type=F32)
        c0 = jnp.sum(oh0, axis=1, keepdims=True)
        n8 = jnp.floor((c0 + jnp.sum(oh1, axis=1, keepdims=True) + (GRAN - 1.0)) * (1.0 / GRAN))
        lo8 = prefix_e(n8)
        s0 = jnp.sum(oh0 * (GRAN * lo8 + cum0 - 1.0), axis=0, keepdims=True)
        s1 = jnp.sum(oh1 * (GRAN * lo8 + c0 + cum1 - 1.0), axis=0, keepdims=True)
        info = jnp.concatenate([s0, s1, r[2:4, :], jnp.zeros((SUBLANES - 4, tb), F32)], axis=0)
        srow_ref[j] = info
        col_ref[pl.ds(pl.multiple_of(j * tb, tb), tb), :] = jnp.concatenate(
            [info, jnp.zeros((LANES - SUBLANES, tb), F32)], axis=0).T
        mg = jnp.where((lo8 <= glane) & (glane < lo8 + n8), 1.0, 0.0)
        mg_ref[j] = mg
        part = jnp.sum(mg * (run8 + glane - lo8), axis=0, keepdims=True)
        gcnt = jnp.broadcast_to(jnp.sum(n8, axis=0, keepdims=True), (1, LANES))
        part_ref[j] = jnp.concatenate([part, gcnt, jnp.zeros((SUBLANES - 2, LANES), F32)], axis=0)
        return run8 + n8

    tot8 = lax.fori_loop(0, NT, p1, jnp.zeros((N_EXP, 1), F32), unroll=8 if NT % 8 == 0 else 1)
    seg_t = jnp.floor((tot8 * GRAN + (TM - 1.0)) * (1.0 / TM))
    base_t = prefix_e(seg_t)
    base8 = base_t * (TM // GRAN)
    lane1 = lax.broadcasted_iota(jnp.int32, (1, LANES), 1)

    def p2(j, carry):
        pr = part_ref[j]
        dst = (pr[0:1, :] + jnp.sum(mg_ref[j] * base8, axis=0, keepdims=True)) * GRAN
        gd_ref[j] = jnp.where(lane1 == G_LAST, pr[1:2, :], dst).astype(jnp.int32)
        return carry

    lax.fori_loop(0, NT, p2, 0, unroll=8 if NT % 8 == 0 else 1)
    eye = jnp.where(glane == lax.broadcasted_iota(jnp.int32, (N_EXP, LANES), 0).astype(F32), 1.0, 0.0)
    tail_row = jnp.sum(eye * ((base8 + tot8) * GRAN), axis=0, keepdims=True)
    tail_n8 = jnp.sum(eye * (seg_t * (TM // GRAN) - tot8), axis=0, keepdims=True)
    nv_l = jnp.broadcast_to(jnp.sum(seg_t, axis=0, keepdims=True), (1, LANES))
    gd_ref[NT] = jnp.where(lane1 == G_LAST, nv_l, tail_row).astype(jnp.int32)
    gd_ref[NT + 1] = tail_n8.astype(jnp.int32)
    ti = lax.broadcasted_iota(jnp.int32, (N_EXP, tb), 1).astype(F32)
    te = jnp.sum(jnp.where(base_t <= ti, 1.0, 0.0), axis=0, keepdims=True) - 1.0
    nv = jnp.broadcast_to(jnp.sum(seg_t, axis=0, keepdims=True), (1, tb))
    own = jnp.where((base_t <= ti) & (ti < base_t + seg_t), 1.0, 0.0)
    vr = jnp.sum(own * jnp.clip(tot8 * GRAN - (ti - base_t) * TM, 0.0, TM), axis=0, keepdims=True)
    meta_ref[...] = jnp.concatenate([te, nv, vr, jnp.zeros((SUBLANES - 3, tb), F32)],
                                    axis=0).astype(jnp.int32)


def _route(rrow, u_cnt, ltri, *, TM):
    NT, _, tb = rrow.shape
    kern = functools.partial(_route_kernel, NT=NT, tb=tb, TM=TM)
    full3 = lambda i: (0, 0, 0)
    return pl.pallas_call(
        kern,
        grid=(1,),
        in_specs=[pl.BlockSpec((NT, SUBLANES, tb), full3),
                  pl.BlockSpec((tb, tb), lambda i: (0, 0)),
                  pl.BlockSpec((N_EXP, N_EXP), lambda i: (0, 0))],
        out_specs=[pl.BlockSpec((NT, SUBLANES, tb), full3),
                   pl.BlockSpec((NT * tb, LANES), lambda i: (0, 0)),
                   pl.BlockSpec((NT + 2, 1, LANES), full3),
                   pl.BlockSpec((SUBLANES, tb), lambda i: (0, 0))],
        out_shape=[jax.ShapeDtypeStruct((NT, SUBLANES, tb), F32),
                   jax.ShapeDtypeStruct((NT * tb, LANES), F32),
                   jax.ShapeDtypeStruct((NT + 2, 1, LANES), jnp.int32),
                   jax.ShapeDtypeStruct((SUBLANES, tb), jnp.int32)],
        scratch_shapes=[pltpu.VMEM((NT, N_EXP, LANES), F32), pltpu.VMEM((NT, SUBLANES, LANES), F32)],
        compiler_params=_cparams(),
        name="route",
    )(rrow, u_cnt, ltri)


_HI_MASK = 0xFFFF0000


def _pack_halves(x):
    c = x.shape[1] // 2
    lo = lax.bitcast_convert_type(x[:, :c], U32)
    hi = lax.bitcast_convert_type(x[:, c:], U32)
    return (lo >> 16) | (hi & U32(_HI_MASK))


def _unpack_halves(w):
    lo = lax.bitcast_convert_type(w << 16, F32)
    hi = lax.bitcast_convert_type(w & U32(_HI_MASK), F32)
    return jnp.concatenate([lo, hi], axis=1).astype(BF16)


def _granule_copy(src_ref, src_row, dst_ref, dst_row, sem, n=1):
    cols = pl.ds(0, min(src_ref.shape[-1], dst_ref.shape[-1]))
    return pltpu.make_async_copy(src_ref.at[pl.ds(src_row, n * GRAN), cols],
                                 dst_ref.at[pl.ds(dst_row, n * GRAN), cols], sem)


def _for_granules(n, body, unroll=8):
    def blk(i, carry):
        for t in range(unroll):
            body(i * unroll + t)
        return carry

    def one(g, carry):
        body(g)
        return carry

    nblk = n // unroll
    lax.fori_loop(0, nblk, blk, 0)
    lax.fori_loop(nblk * unroll, n, one, 0)


def _wait_granules(n, src_ref, dst_ref, sem, n_max):
    b = 1
    while b <= n_max:
        @pl.when((n & b) != 0)
        def _(b=b):
            _granule_copy(src_ref, 0, dst_ref, 0, sem, n=b).wait()
        b *= 2


def _dispatch_kernel(gd_ref, srow_ref, u_ref, xs_ref, buf, zbuf, sems, *, NT, SL, TM, n_tiles):
    j = pl.program_id(0)
    slot = j % 2
    zsem = sems.at[2]

    def drain(tile, sl):
        _wait_granules(gd_ref[tile, G_LAST], buf.at[sl], xs_ref, sems.at[sl], SL // GRAN)

    def tile_fill(t):
        return pltpu.make_async_copy(zbuf, xs_ref.at[pl.ds(pl.multiple_of(t * TM, TM), TM), :], zsem)

    def zero_fill(wait):
        for e in range(N_EXP):
            n, row0 = gd_ref[NT + 1, e], gd_ref[NT, e]
            b = TM // GRAN // 2
            while b >= 1:
                @pl.when((n & b) != 0)
                def _(b=b, n=n, row0=row0):
                    start = pl.multiple_of(row0 + ((n >> b.bit_length()) << b.bit_length()) * GRAN, GRAN)
                    cp = pltpu.make_async_copy(zbuf.at[pl.ds(0, b * GRAN), :],
                                               xs_ref.at[pl.ds(start, b * GRAN), :], zsem)
                    cp.wait() if wait else cp.start()
                b //= 2

        def zt(t, carry):
            tile_fill(t).wait() if wait else tile_fill(t).start()
            return carry
        lax.fori_loop(gd_ref[NT, G_LAST], n_tiles, zt, 0)

    @pl.when(j == 0)
    def _():
        zbuf[...] = jnp.zeros_like(zbuf)
        zero_fill(False)

    @pl.when(j >= 2)
    def _():
        drain(j - 2, slot)

    s = srow_ref[0]
    rows = lax.broadcasted_iota(jnp.int32, (SL, s.shape[1]), 0).astype(F32)
    m0 = rows == s[0:1, :]
    m1 = rows == s[1:2, :]
    oh = jnp.where(m0 | m1, 1.0, 0.0).astype(BF16)
    dw = u_ref.shape[1] // 2
    buf[slot, :, 0:dw] = _pack_halves(jnp.dot(oh, u_ref[...], preferred_element_type=F32))
    wrow = jnp.sum(jnp.where(m0, s[2:3, :], 0.0) + jnp.where(m1, s[3:4, :], 0.0), axis=1, keepdims=True)
    buf[slot, :, dw:dw + LANES] = lax.bitcast_convert_type(jnp.broadcast_to(wrow, (SL, LANES)), U32)

    def issue(g):
        _granule_copy(buf.at[slot], pl.multiple_of(g * GRAN, GRAN), xs_ref,
                      pl.multiple_of(gd_ref[j, g], GRAN), sems.at[slot]).start()

    _for_granules(gd_ref[j, G_LAST], issue)

    @pl.when(j == NT - 1)
    def _():
        drain(j, slot)
        if NT > 1:
            drain(j - 1, 1 - slot)
        zero_fill(True)


def _dispatch(gd, srow, u2, *, n_tiles, TM):
    N, D = u2.shape
    NT, _, tb = srow.shape
    SL = _slots_per_tile(tb)
    n_rows = n_tiles * TM
    kern = functools.partial(_dispatch_kernel, NT=NT, SL=SL, TM=TM, n_tiles=n_tiles)
    grid_spec = pltpu.PrefetchScalarGridSpec(
        num_scalar_prefetch=1,
        grid=(NT,),
        in_specs=[pl.BlockSpec((1, SUBLANES, tb), lambda j, gd: (j, 0, 0)),
                  pl.BlockSpec((tb, D), lambda j, gd: (j, 0))],
        out_specs=pl.BlockSpec(memory_space=pl.ANY),
        scratch_shapes=[pltpu.VMEM((2, SL, D // 2 + LANES), U32), pltpu.VMEM((TM, D // 2 + LANES), U32),
                        pltpu.SemaphoreType.DMA((3,))],
    )
    return pl.pallas_call(
        kern,
        grid_spec=grid_spec,
        out_shape=jax.ShapeDtypeStruct((n_rows, D // 2 + LANES), U32),
        compiler_params=_cparams(),
        name="dispatch",
    )(gd, srow, u2)


def _ffn_kernel(meta_ref, xs_ref, wg_ref, wu_ref, wd_ref, o_ref, wgb, wub, wdb, sg, su, sd, slot_ref, sems):
    i = pl.program_id(0)
    nv = meta_ref[TILES_USED, 0]
    tile_expert = lambda t: meta_ref[TILE_EXPERT, t]
    e = tile_expert(i)

    def weight_copies(ex, sl):
        return (pltpu.make_async_copy(wg_ref.at[ex], sg.at[sl], sems.at[sl]),
                pltpu.make_async_copy(wu_ref.at[ex], su.at[sl], sems.at[sl]),
                pltpu.make_async_copy(wd_ref.at[ex], sd.at[sl], sems.at[sl]))

    @pl.when(i == 0)
    def _():
        slot_ref[0] = 0
        for cp in weight_copies(e, 0):
            cp.start()

    new_expert = (i < nv) & ((i == 0) | (e != tile_expert(jnp.maximum(i - 1, 0))))

    @pl.when(new_expert)
    def _():
        sl = slot_ref[0]
        for cp in weight_copies(e, sl):
            cp.wait()
        nxt = lax.while_loop(lambda t: (t < nv) & (tile_expert(jnp.minimum(t, nv - 1)) == e), lambda t: t + 1, i + 1)

        @pl.when(nxt < nv)
        def _():
            for cp in weight_copies(tile_expert(nxt), 1 - sl):
                cp.start()

    hm = xs_ref.shape[0] // FFN_SUB
    dw = xs_ref.shape[1] - LANES

    def swiglu_rows(nsub, cast):
        if cast:
            sl = slot_ref[0]
            wgb[...] = sg[sl].astype(BF16)
            wub[...] = su[sl].astype(BF16)
            wdb[...] = sd[sl].astype(BF16)
            slot_ref[0] = 1 - sl
        halves = tuple(slice(hm * j, hm * (j + 1)) for j in range(nsub))
        x = [_unpack_halves(xs_ref[r, 0:dw]) for r in halves]
        g = [jnp.dot(x[j], wgb[...], preferred_element_type=F32) for j in range(nsub)]
        u = [jnp.dot(x[j], wub[...], preferred_element_type=F32) for j in range(nsub)]
        h = [(g[j] * _sigmoid(g[j]) * u[j]).astype(BF16) for j in range(nsub)]
        y = [jnp.dot(h[j], wdb[...], preferred_element_type=F32) for j in range(nsub)]
        for j in range(nsub):
            wt = lax.bitcast_convert_type(xs_ref[halves[j], dw:dw + LANES], F32)
            yw = y[j] * jnp.concatenate([wt] * (2 * dw // LANES), axis=1)
            o_ref[halves[j], 0:dw] = _pack_halves(yw.astype(BF16).astype(F32))
            o_ref[halves[j], dw:dw + LANES] = xs_ref[halves[j], dw:dw + LANES]
        if nsub < FFN_SUB:
            o_ref[hm * nsub:, :] = jnp.zeros((hm * (FFN_SUB - nsub), dw + LANES), U32)

    used = meta_ref[TILE_ROWS, i]
    for nsub in range(1, FFN_SUB + 1):
        lo, hi = hm * (nsub - 1), hm * nsub
        rows_here = (i < nv) & (used > lo) & ((used <= hi) if nsub < FFN_SUB else True)
        pl.when(rows_here & new_expert)(functools.partial(swiglu_rows, nsub, True))
        pl.when(rows_here & jnp.logical_not(new_expert))(functools.partial(swiglu_rows, nsub, False))


def _ffn(meta, xs, wg, wu, wd, *, TM):
    P, XW = xs.shape
    DW = XW - LANES
    D = 2 * DW
    n_tiles = P // TM
    assert n_tiles <= meta.shape[1]
    used_tile = lambda i, meta: (jnp.maximum(jnp.minimum(i, meta[TILES_USED, 0] - 1), 0), 0)
    grid_spec = pltpu.PrefetchScalarGridSpec(
        num_scalar_prefetch=1,
        grid=(n_tiles,),
        in_specs=[pl.BlockSpec((TM, XW), used_tile),
                  pl.BlockSpec(memory_space=pl.ANY),
                  pl.BlockSpec(memory_space=pl.ANY),
                  pl.BlockSpec(memory_space=pl.ANY)],
        out_specs=pl.BlockSpec((TM, XW), used_tile),
        scratch_shapes=[pltpu.VMEM((D, D_EXP), BF16), pltpu.VMEM((D, D_EXP), BF16),
                        pltpu.VMEM((D_EXP, D), BF16),
                        pltpu.VMEM((2, D, D_EXP), F32), pltpu.VMEM((2, D, D_EXP), F32),
                        pltpu.VMEM((2, D_EXP, D), F32), pltpu.SMEM((1,), jnp.int32),
                        pltpu.SemaphoreType.DMA((2,))],
    )
    return pl.pallas_call(
        _ffn_kernel,
        grid_spec=grid_spec,
        out_shape=jax.ShapeDtypeStruct((P, XW), U32),
        input_output_aliases={1: 0},
        compiler_params=_cparams(),
        name="ffn",
    )(meta, xs, wg, wu, wd)


def _combine_kernel(gd_ref, ys_ref, col_ref, x1_ref, mod_ref, g_ref, b_ref, o_ref, buf, sems, *, NT, SL):
    j = pl.program_id(0)
    slot = j % 2

    def fetch(tile, sl):
        def f(g):
            _granule_copy(ys_ref, pl.multiple_of(gd_ref[tile, g], GRAN), buf.at[sl],
                          pl.multiple_of(g * GRAN, GRAN), sems.at[sl]).start()
        _for_granules(gd_ref[tile, G_LAST], f)

    @pl.when(j == 0)
    def _():
        fetch(0, 0)

    @pl.when(j + 1 < NT)
    def _():
        fetch(j + 1, 1 - slot)

    ng = gd_ref[j, G_LAST]

    _wait_granules(ng, ys_ref, buf.at[slot], sems.at[slot], SL // GRAN)

    rows = lax.broadcasted_iota(jnp.int32, (SL, 1), 0)
    yb = _unpack_halves(jnp.where(rows < ng * GRAN, buf[slot], U32(0)))
    col = col_ref[...]
    tb = col.shape[0]
    lanes = lax.broadcasted_iota(jnp.int32, (tb, SL), 1).astype(F32)
    sel = jnp.where((lanes == col[:, 0:1]) | (lanes == col[:, 1:2]), 1.0, 0.0).astype(BF16)
    y = jnp.dot(sel, yb, preferred_element_type=F32)
    mod = mod_ref[0]
    z = ALPHA * x1_ref[...] + (1.0 + mod[5:6, :]) * y
    o_ref[...] = _layer_norm(z, g_ref[...], b_ref[...])


def _combine(gd, ys, col, x1, mod3, g, b, *, S, tb):
    N, D = x1.shape
    NT = N // tb
    tpb = S // tb
    SL = _slots_per_tile(tb)
    kern = functools.partial(_combine_kernel, NT=NT, SL=SL)
    grid_spec = pltpu.PrefetchScalarGridSpec(
        num_scalar_prefetch=1,
        grid=(NT,),
        in_specs=[pl.BlockSpec(memory_space=pl.ANY),
                  pl.BlockSpec((tb, LANES), lambda j, gd: (j, 0)),
                  pl.BlockSpec((tb, D), lambda j, gd: (j, 0)),
                  pl.BlockSpec((1, 6, D), lambda j, gd: (j // tpb, 0, 0)),
                  pl.BlockSpec((1, D), lambda j, gd: (0, 0)),
                  pl.BlockSpec((1, D), lambda j, gd: (0, 0))],
        out_specs=pl.BlockSpec((tb, D), lambda j, gd: (j, 0)),
        scratch_shapes=[pltpu.VMEM((2, SL, D // 2), U32), pltpu.SemaphoreType.DMA((2,))],
    )
    return pl.pallas_call(
        kern,
        grid_spec=grid_spec,
        out_shape=jax.ShapeDtypeStruct((N, D), F32),
        compiler_params=_cparams(),
        name="combine",
    )(gd, ys, col, x1, mod3, g, b)


def _layer(x, c, l, w_ada, b_ada, w_in, w_conv, b_conv, b_igate, b_fgate, mlstm_norm_g, w_gla_a, b_gla_a,
           gla_norm_g, w_out, ln1_g, ln1_b, w_route_group, b_route_group, w_route_expert, b_route_expert,
           w_gate, w_up, w_down, ln2_g, ln2_b):
    B, S, D = x.shape
    N = B * S
    x2 = x.reshape(N, D)
    tm_in = min(512, S)
    tm = min(256, S)
    lm = min(256, S)
    assert S % tm_in == 0 and S % tm == 0 and tm_in % lm == 0 and S % G_CHUNK == 0
    assert w_in.shape[1:] == (D, IN_TOT) and w_gate.shape[1:] == (N_EXP, D, D_EXP)

    mod3 = _ada(c, w_ada[l], b_ada[l]).reshape(B, 6, D)

    wa_pad = jnp.pad(w_gla_a[l], ((SM_A, LANES - SM_A - G_RANK), (0, 0))).astype(BF16)
    bg = jnp.concatenate([jnp.pad(b_igate[l], (0, SUBLANES - M_HEADS)),
                          jnp.pad(b_fgate[l], (0, SUBLANES - M_HEADS))]).reshape(2 * SUBLANES, 1)
    oa, la, g3 = _inproj(x2, mod3, jnp.swapaxes(w_in, 1, 2), w_conv[l], b_conv[l].reshape(1, -1), wa_pad,
                         b_gla_a[l].reshape(1, -1), bg, S=S, tm=tm_in, lm=lm, layer=l)

    u_tri = jnp.asarray(np.triu(np.ones((lm, lm), np.float32)))
    nb = 4 if B % 4 == 0 else (2 if B % 2 == 0 else 1)
    ts = min(512, S)
    hm = _mlstm(oa, g3, u_tri, mlstm_norm_g[l].reshape(1, -1), B=B, S=S, L=lm, nb=nb, ts=ts)
    w3_np, mk_np = _gla_consts()
    hg = _gla(oa, la, jnp.asarray(w3_np, BF16), jnp.asarray(mk_np), gla_norm_g[l].reshape(1, -1), B=B, S=S,
              nb=nb, ts=ts)

    br = jnp.concatenate([jnp.pad(b_route_group[l], (0, SUBLANES - N_GROUPS)),
                          jnp.pad(b_route_expert[l], (0, LANES - SUBLANES - N_EXP))]).reshape(1, LANES)
    x1, u2, rrow = _outproj(hm, hg, w_out[l], x2, mod3, ln1_g[l].reshape(1, -1), ln1_b[l].reshape(1, -1),
                            jnp.swapaxes(w_route_group, 1, 2), jnp.swapaxes(w_route_expert, 1, 2), br,
                            S=S, tb=tm, nh=4 if S % (4 * tm) == 0 else 1, layer=l)

    u_cnt = jnp.asarray(np.triu(np.ones((tm, tm), np.float32)), BF16)
    ltri = jnp.asarray(np.tril(np.ones((N_EXP, N_EXP), np.float32), -1))
    srow, col, gd3, meta = _route(rrow, u_cnt, ltri, TM=FFN_TM)
    gd = gd3.reshape(N // tm + 2, LANES)
    n_tiles = _ffn_tiles(N, tm)

    xs = _dispatch(gd, srow, u2, n_tiles=n_tiles, TM=FFN_TM)
    ys = _ffn(meta, xs, w_gate[l], w_up[l], w_down[l], TM=FFN_TM)
    out = _combine(gd, ys, col, x1, mod3, ln2_g[l].reshape(1, -1), ln2_b[l].reshape(1, -1), S=S, tb=tm)
    return out.reshape(B, S, D)


def kernel(x, c, w_ada, b_ada, w_in, w_conv, b_conv, b_igate, b_fgate, mlstm_norm_g, w_gla_a, b_gla_a,
           gla_norm_g, w_out, ln1_g, ln1_b, w_route_group, b_route_group, w_route_expert, b_route_expert,
           w_gate, w_up, w_down, ln2_g, ln2_b):
    for l in range(DEPTH):
        x = _layer(x, c, l, w_ada, b_ada, w_in, w_conv, b_conv, b_igate, b_fgate, mlstm_norm_g, w_gla_a,
                   b_gla_a, gla_norm_g, w_out, ln1_g, ln1_b, w_route_group, b_route_group, w_route_expert,
                   b_route_expert, w_gate, w_up, w_down, ln2_g, ln2_b)
    return x
```

```python
import functools

import numpy as np
import jax
import jax.numpy as jnp
from jax import lax
from jax.experimental import pallas as pl
from jax.experimental.pallas import tpu as pltpu

F32 = jnp.float32
BF16 = jnp.bfloat16
U32 = jnp.uint32
HIGHEST = lax.Precision.HIGHEST

DEPTH = 1
M_HEADS = 4
M_HD = 128
M_W = M_HEADS * M_HD
CONV_W = 4
G_HEADS = 4
G_DK = 64
G_DV = 128
G_W = G_HEADS * G_DV
G_KW = G_HEADS * G_DK
G_RANK = 16
G_TAU = 16.0
G_CHUNK = 64
N_GROUPS = 4
E_PER_G = 8
N_EXP = N_GROUPS * E_PER_G
D_EXP = 512
ALPHA = (2 * DEPTH) ** 0.25
LN_EPS = 1e-5

LANES = 128
SUBLANES = 8
VMEM_LIMIT = 48 * 1024 * 1024

C_QK = 0
C_VO = 1024
C_GQK = 2048
C_GV = 2560
C_GG = 3072
C_SMALL = 3584
C_TOT = 3712
SM_I, SM_F, SM_A = 0, 8, 16
IN_GATES = 4 * M_W
IN_G = IN_GATES + 2 * M_HEADS
IN_GA = IN_G + 2 * G_KW + 2 * G_W
IN_TOT = IN_GA + G_RANK

FFN_TM = 512
FFN_SUB = 2
GRAN = 2 * SUBLANES
G_LAST = LANES - 1
TILE_EXPERT, TILES_USED, TILE_ROWS = 0, 1, 2


def _cparams(n_axes=1):
    return pltpu.CompilerParams(dimension_semantics=("arbitrary",) * n_axes,
                                vmem_limit_bytes=VMEM_LIMIT)


def _sigmoid(x):
    return 1.0 / (1.0 + jnp.exp(-x))


def _log_sigmoid(x):
    return jnp.minimum(x, 0.0) - jnp.log(1.0 + jnp.exp(-jnp.abs(x)))


def _ada_kernel(c_ref, w_ref, b_ref, o_ref):
    c = c_ref[...]
    ca = (c * _sigmoid(c)).astype(BF16)
    o_ref[...] = jnp.dot(ca, w_ref[...].astype(BF16), preferred_element_type=F32) + b_ref[...]


def _ada(c, w, b):
    B, D = c.shape
    n_out = w.shape[1]
    tn = 1024
    return pl.pallas_call(
        _ada_kernel,
        grid=(n_out // tn,),
        in_specs=[pl.BlockSpec((B, D), lambda j: (0, 0)),
                  pl.BlockSpec((D, tn), lambda j: (0, j)),
                  pl.BlockSpec((1, tn), lambda j: (0, j))],
        out_specs=pl.BlockSpec((B, tn), lambda j: (0, j)),
        out_shape=jax.ShapeDtypeStruct((B, n_out), F32),
        compiler_params=_cparams(),
        name="ada",
    )(c, w, b.reshape(1, n_out))


def _inproj_kernel(x_ref, mod_ref, win_ref, wc_ref, bc_ref, wa_ref, ba_ref, bg_ref,
                   oa_ref, la_ref, g_ref, halo_ref, w_ref, *, tm, tpb, lm):
    i = pl.program_id(0)

    @pl.when(i == 0)
    def _():
        rc = 2 * LANES
        for r in range(0, IN_GATES, rc):
            w_ref[:, r:r + rc] = win_ref[0, r:r + rc, :].T.astype(BF16)
        for r in range(0, C_SMALL - C_GQK, rc):
            w_ref[:, C_GQK + r:C_GQK + r + rc] = win_ref[0, IN_G + r:IN_G + r + rc, :].T.astype(BF16)
        gates = win_ref[0, IN_GATES:IN_G, :]
        z = lambda n: jnp.zeros((n, gates.shape[1]), F32)
        small = jnp.concatenate([gates[0:M_HEADS], z(SM_F - M_HEADS), gates[M_HEADS:2 * M_HEADS],
                                 z(SM_A - SM_F - M_HEADS), win_ref[0, IN_GA:IN_TOT, :],
                                 z(LANES - SM_A - G_RANK)], axis=0)
        w_ref[:, C_SMALL:C_TOT] = small.T.astype(BF16)

    @pl.when(i % tpb == 0)
    def _():
        halo_ref[0:SUBLANES, :] = jnp.zeros((SUBLANES, halo_ref.shape[1]), F32)

    mod = mod_ref[0]
    u = (x_ref[...] * (1.0 + mod[1:2, :]) + mod[0:1, :]).astype(BF16)

    def proj(c0, c1):
        return jnp.dot(u, w_ref[:, c0:c1], preferred_element_type=F32)

    p = proj(C_QK, C_QK + 2 * M_W)
    halo_ref[SUBLANES:SUBLANES + tm, :] = p
    acc = bc_ref[...] + wc_ref[CONV_W - 1:CONV_W, :] * p
    for j in range(CONV_W - 1):
        acc = acc + wc_ref[j:j + 1, :] * halo_ref[pl.ds(SUBLANES - (CONV_W - 1) + j, tm), :]
    halo_ref[0:SUBLANES, :] = p[tm - SUBLANES:, :]
    qk = acc * _sigmoid(acc)
    oa_ref[:, C_QK:C_QK + M_W] = qk[:, :M_W].astype(BF16)
    oa_ref[:, C_QK + M_W:C_QK + 2 * M_W] = (qk[:, M_W:] * (M_HD ** -0.5)).astype(BF16)

    ps = proj(C_SMALL, C_TOT)
    la = jnp.dot(ps.astype(BF16), wa_ref[...], preferred_element_type=F32) + ba_ref[...]
    la_ref[...] = _log_sigmoid(la) * (1.0 / G_TAU)
    pt = ps.T
    gi = pt[SM_I:SM_I + SUBLANES, :] + bg_ref[0:SUBLANES, :]
    gf = _log_sigmoid(pt[SM_F:SM_F + SUBLANES, :] + bg_ref[SUBLANES:2 * SUBLANES, :])
    for j in range(tm // lm):
        g_ref[j, 0:SUBLANES, :] = gi[:, j * lm:(j + 1) * lm]
        g_ref[j, SUBLANES:2 * SUBLANES, :] = gf[:, j * lm:(j + 1) * lm]

    p = proj(C_VO, C_VO + 2 * M_W)
    oa_ref[:, C_VO:C_VO + 2 * M_W] = p.astype(BF16)

    p = proj(C_GQK, C_GQK + G_KW)
    oa_ref[:, C_GQK:C_GQK + G_KW] = (p * (G_DK ** -0.5)).astype(BF16)
    p = proj(C_GQK + G_KW, C_SMALL)
    oa_ref[:, C_GQK + G_KW:C_SMALL] = p.astype(BF16)


def _inproj(x2, mod3, w_in, w_conv, b_conv, wa_pad, b_gla, bg, *, S, tm, lm, layer):
    N, D = x2.shape
    tpb = S // tm
    kern = functools.partial(_inproj_kernel, tm=tm, tpb=tpb, lm=lm)
    return pl.pallas_call(
        kern,
        grid=(N // tm,),
        in_specs=[pl.BlockSpec((tm, D), lambda i: (i, 0)),
                  pl.BlockSpec((1, 6, D), lambda i: (i // tpb, 0, 0)),
                  pl.BlockSpec((1, IN_TOT, D), lambda i: (layer, 0, 0), pipeline_mode=pl.Buffered(1)),
                  pl.BlockSpec((CONV_W, 2 * M_W), lambda i: (0, 0)),
                  pl.BlockSpec((1, 2 * M_W), lambda i: (0, 0)),
                  pl.BlockSpec((LANES, G_KW), lambda i: (0, 0)),
                  pl.BlockSpec((1, G_KW), lambda i: (0, 0)),
                  pl.BlockSpec((2 * SUBLANES, 1), lambda i: (0, 0))],
        out_specs=[pl.BlockSpec((tm, C_SMALL), lambda i: (i, 0)),
                   pl.BlockSpec((tm, G_KW), lambda i: (i, 0)),
                   pl.BlockSpec((tm // lm, 2 * SUBLANES, lm), lambda i: (i, 0, 0))],
        out_shape=[jax.ShapeDtypeStruct((N, C_SMALL), BF16),
                   jax.ShapeDtypeStruct((N, G_KW), F32),
                   jax.ShapeDtypeStruct((N // lm, 2 * SUBLANES, lm), F32)],
        scratch_shapes=[pltpu.VMEM((SUBLANES + tm, 2 * M_W), F32), pltpu.VMEM((D, C_TOT), BF16)],
        compiler_params=_cparams(),
        name="inproj",
    )(x2, mod3, w_in, w_conv, b_conv, wa_pad, b_gla, bg)


def _mlstm_sel():
    sel = np.zeros((2 * LANES, 2 * M_HEADS * M_HD), np.float32)
    for j in range(2 * M_HEADS):
        src = (SUBLANES if j < M_HEADS else 3 * SUBLANES) + j % M_HEADS
        sel[src, M_HD * j:M_HD * (j + 1)] = 1.0
        sel[LANES + src, M_HD * j:M_HD * (j + 1)] = 1.0
    return sel


def _mlstm_kernel(qk_ref, vo_ref, g_ref, u_ref, gain_ref, sel_ref, out_ref, c_ref, zt_ref, a_ref, dec_ref, m_ref,
                  *, L, NC, nb):
    @pl.when(pl.program_id(1) == 0)
    def _():
        c_ref[...] = jnp.zeros_like(c_ref)
        m_ref[...] = jnp.zeros_like(m_ref)

    tril = (lax.broadcasted_iota(jnp.int32, (L, L), 0) >= lax.broadcasted_iota(jnp.int32, (L, L), 1))
    ones_v = jnp.ones((L, M_HD), BF16)
    zpad = jnp.zeros((LANES - 4 * SUBLANES, L), F32)
    zgroup = lax.broadcasted_iota(jnp.int32, (L, LANES), 1) // SUBLANES
    factor_cols = (zgroup == 1) | (zgroup == 3)

    order = [(bi, c) for bi in range(nb) for c in range(NC)]
    f_all = jnp.concatenate([g_ref[bi, c, SUBLANES:2 * SUBLANES, :] for bi, c in order], axis=0)
    i_all = jnp.concatenate([g_ref[bi, c, 0:SUBLANES, :] for bi, c in order], axis=0)
    b_all = jnp.dot(f_all, u_ref[...], preferred_element_type=F32, precision=HIGHEST)
    a_all = i_all - b_all
    lane_all = lax.broadcasted_iota(jnp.int32, a_all.shape, 1)
    g_all = a_all
    s = 1
    while s < L:
        g_all = jnp.maximum(g_all, jnp.where(lane_all >= s, pltpu.roll(g_all, s, 1), -jnp.inf))
        s *= 2
    for bi in range(nb):
        m_prev = m_ref[bi][:, 0:1]
        for c in range(NC):
            ci = bi * NC + c
            r8 = slice(SUBLANES * ci, SUBLANES * (ci + 1))
            a, b = a_all[r8], b_all[r8]
            a_ref[ci] = a
            M = jnp.maximum(g_all[r8], m_prev)
            ML = M[:, L - 1:L]
            Z = jnp.concatenate([M, jnp.exp(m_prev - M), jnp.exp(-(b + M)), jnp.exp(a - ML), zpad],
                                axis=0)
            zt_ref[ci] = Z.T
            dec_ref[ci] = jnp.broadcast_to(jnp.exp(m_prev - ML), (SUBLANES, 2 * M_HD))
            m_prev = b[:, L - 1:L] + ML
        m_ref[bi] = jnp.broadcast_to(m_prev, (SUBLANES, LANES))

    chains = [(bi, h) for bi in range(nb) for h in range(M_HEADS)]
    nt = (((1,), (1,)), ((), ()))
    tn = (((0,), (0,)), ((), ()))

    def chunk(c, carry):
        rows = pl.ds(pl.multiple_of(c * L, L), L)
        Zt = [zt_ref[bi * NC + c] for bi in range(nb)]
        a = [a_ref[bi * NC + c] for bi in range(nb)]
        dec = [dec_ref[bi * NC + c] for bi in range(nb)]
        hs = [slice(h * M_HD, (h + 1) * M_HD) for h in range(M_HEADS)]
        hs2 = [slice(M_W + h * M_HD, M_W + (h + 1) * M_HD) for h in range(M_HEADS)]
        q = [qk_ref[bi, rows, hs[h]] for bi, h in chains]
        k = [qk_ref[bi, rows, hs2[h]] for bi, h in chains]
        vext = [jnp.concatenate([vo_ref[bi, rows, hs[h]], ones_v], axis=1) for bi, h in chains]
        cst = [c_ref[bi * M_HEADS + h] for bi, h in chains]
        n = range(len(chains))
        sc = [lax.dot_general(q[i], k[i], nt, preferred_element_type=F32) for i in n]
        qc = [jnp.dot(q[i], cst[i].astype(BF16), preferred_element_type=F32) for i in n]
        pm = [(sc[i] * jnp.exp(jnp.where(tril, a[bi][h:h + 1, :] - Zt[bi][:, h:h + 1], -jnp.inf))).astype(BF16)
              for i, (bi, h) in enumerate(chains)]
        pv = [jnp.dot(pm[i], vext[i], preferred_element_type=F32) for i in n]
        rep = []
        for bi in range(nb):
            zf = jnp.where(factor_cols, Zt[bi], 0.0)
            zh = zf.astype(BF16)
            zl = (zf - zh.astype(F32)).astype(BF16)
            rep.append(jnp.dot(jnp.concatenate([zh, zl], axis=1), sel_ref[...], preferred_element_type=F32))
        e_inter = [rep[bi][:, M_HD * h:M_HD * (h + 1)] for bi, h in chains]
        w_state = [rep[bi][:, M_HD * (M_HEADS + h):M_HD * (M_HEADS + h + 1)] for bi, h in chains]
        kw = [(w_state[i] * k[i].astype(F32)).astype(BF16) for i in n]
        upd = [lax.dot_general(kw[i], vext[i], tn, preferred_element_type=F32) for i in n]
        for i, (bi, h) in enumerate(chains):
            c_ref[bi * M_HEADS + h] = dec[bi][h:h + 1, :] * cst[i] + upd[i]
            nd = pv[i] + jnp.concatenate([e_inter[i], e_inter[i]], axis=1) * qc[i]
            hh = nd[:, :M_HD] / jnp.maximum(jnp.abs(nd[:, M_HD:]),
                                            Zt[bi][:, 2 * SUBLANES + h:2 * SUBLANES + h + 1])
            hh = _sigmoid(vo_ref[bi, rows, hs2[h]].astype(F32)) * hh
            hn = hh * lax.rsqrt(jnp.mean(hh * hh, axis=-1, keepdims=True) + LN_EPS)
            out_ref[bi, rows, hs[h]] = (hn * gain_ref[:, hs[h]]).astype(BF16)
        return carry

    lax.fori_loop(0, NC, chunk, 0, unroll=True)


def _mlstm(oa, g3, u_tri, gain, *, B, S, L, nb, ts):
    N = oa.shape[0]
    NC = ts // L
    oa3 = oa.reshape(B, S, oa.shape[1])
    g4 = g3.reshape(B, S // L, 2 * SUBLANES, L)
    sel = jnp.asarray(_mlstm_sel(), BF16)
    kern = functools.partial(_mlstm_kernel, L=L, NC=NC, nb=nb)
    out = pl.pallas_call(
        kern,
        grid=(B // nb, S // ts),
        in_specs=[pl.BlockSpec((nb, ts, 2 * M_W), lambda b, t: (b, t, C_QK // (2 * M_W))),
                  pl.BlockSpec((nb, ts, 2 * M_W), lambda b, t: (b, t, C_VO // (2 * M_W))),
                  pl.BlockSpec((nb, NC, 2 * SUBLANES, L), lambda b, t: (b, t, 0, 0)),
                  pl.BlockSpec((L, L), lambda b, t: (0, 0)),
                  pl.BlockSpec((1, M_W), lambda b, t: (0, 0)),
                  pl.BlockSpec(sel.shape, lambda b, t: (0, 0))],
        out_specs=pl.BlockSpec((nb, ts, M_W), lambda b, t: (b, t, 0)),
        out_shape=jax.ShapeDtypeStruct((B, S, M_W), BF16),
        scratch_shapes=[pltpu.VMEM((nb * M_HEADS, M_HD, 2 * M_HD), F32),
                        pltpu.VMEM((nb * NC, L, LANES), F32),
                        pltpu.VMEM((nb * NC, SUBLANES, L), F32),
                        pltpu.VMEM((nb * NC, SUBLANES, 2 * M_HD), F32),
                        pltpu.VMEM((nb, SUBLANES, LANES), F32)],
        compiler_params=_cparams(2),
        name="mlstm",
    )(oa3, oa3, g4, u_tri, gain, sel)
    return out.reshape(N, M_W)


_G_LEVELS = 6
_G_XROW = 2 * G_CHUNK + SUBLANES


def _gla_consts():
    L = G_CHUNK
    t = np.arange(L)
    blocks = [(t[None, :] <= t[:, None]).astype(np.float32),
              (t[None, :] > t[:, None]).astype(np.float32),
              np.ones((SUBLANES, L), np.float32)]
    masks = [np.eye(L, dtype=np.float32)]
    m = 1
    while m < L:
        wl = np.zeros((L, L), np.float32)
        for r in range(L):
            r0 = (r // (2 * m)) * 2 * m + m
            if r % (2 * m) >= m:
                wl[r, r0:r + 1] = 1.0
            else:
                wl[r, r + 1:r0] = 1.0
        blocks.append(wl)
        tt, ss = t[:, None], t[None, :]
        masks.append(((tt // (2 * m) == ss // (2 * m)) & (tt % (2 * m) >= m)
                      & (ss % (2 * m) < m)).astype(np.float32))
        m *= 2
    w = np.concatenate(blocks, axis=0)
    w3 = np.concatenate([w, w, w], axis=1)
    mk = np.stack([np.concatenate([x] * G_HEADS, axis=0) for x in masks])
    return w3, mk


def _gla_kernel(qk_ref, v_ref, gg_ref, la_ref, w3_ref, mk_ref, gain_ref, out_ref, st_ref, *, NC, nb):
    L = G_CHUNK

    @pl.when(pl.program_id(1) == 0)
    def _():
        st_ref[...] = jnp.zeros_like(st_ref)

    lane_head = lax.broadcasted_iota(jnp.int32, (L, G_KW), 1) // G_DK
    br = lax.broadcasted_iota(jnp.int32, (2 * G_DV, LANES), 0) < G_DV
    bl = lax.broadcasted_iota(jnp.int32, (2 * G_DV, LANES), 1) < G_DK
    bmask = br == bl
    nt = (((1,), (1,)), ((), ()))
    tn = (((0,), (0,)), ((), ()))

    def chunk(c, carry):
        rows = pl.ds(pl.multiple_of(c * L, L), L)
        X, q, k = [], [], []
        for bi in range(nb):
            la = la_ref[bi, rows, :]
            hi = la.astype(BF16)
            r1 = la - hi.astype(F32)
            mid = r1.astype(BF16)
            lo = (r1 - mid.astype(F32)).astype(BF16)
            stk = jnp.concatenate([hi, mid, lo], axis=0)
            X.append(jnp.exp(jnp.dot(w3_ref[...], stk, preferred_element_type=F32)))
            q.append(qk_ref[bi, rows, 0:G_KW].astype(F32))
            k.append(qk_ref[bi, rows, G_KW:2 * G_KW].astype(F32))

        sc = [[None] * (_G_LEVELS + 1) for _ in range(nb)]
        for lev in range(_G_LEVELS + 1):
            for bi in range(nb):
                if lev == 0:
                    qt, kt = q[bi], k[bi]
                else:
                    xl = X[bi][_G_XROW + L * (lev - 1):_G_XROW + L * lev, :]
                    qt, kt = q[bi] * xl, k[bi] * xl
                q4 = jnp.concatenate([jnp.where(lane_head == h, qt, 0.0) for h in range(G_HEADS)],
                                     axis=0).astype(BF16)
                sc[bi][lev] = lax.dot_general(q4, kt.astype(BF16), nt, preferred_element_type=F32)
        Ab = []
        for bi in range(nb):
            A = sc[bi][0] * mk_ref[0]
            for lev in range(1, _G_LEVELS + 1):
                A = A + sc[bi][lev] * mk_ref[lev]
            Ab.append(A.astype(BF16))

        for bi in range(nb):
            gg = gg_ref[bi, rows, :].astype(F32)
            gate = gg * _sigmoid(gg)
            for p in range(2):
                ls = slice(LANES * p, LANES * (p + 1))
                vp = v_ref[bi, rows, 2 * G_DV * p:2 * G_DV * (p + 1)]
                oi = [jnp.dot(Ab[bi][L * (2 * p + hh):L * (2 * p + hh + 1)],
                              vp[:, G_DV * hh:G_DV * (hh + 1)], preferred_element_type=F32)
                      for hh in range(2)]
                st = st_ref[bi, p]
                qc = (q[bi][:, ls] * X[bi][0:L, ls]).astype(BF16)
                o_inter = lax.dot_general(qc, st.astype(BF16), nt, preferred_element_type=F32)
                kc = (k[bi][:, ls] * X[bi][L:2 * L, ls]).astype(BF16)
                upd = lax.dot_general(vp, kc, tn, preferred_element_type=F32)
                dec = X[bi][2 * L:2 * L + 1, ls]
                st_ref[bi, p] = jnp.where(bmask, dec * st + upd, 0.0)
                for hh in range(2):
                    o = o_inter[:, G_DV * hh:G_DV * (hh + 1)] + oi[hh]
                    hn = o * lax.rsqrt(jnp.mean(o * o, axis=-1, keepdims=True) + LN_EPS)
                    hs = slice(G_DV * (2 * p + hh), G_DV * (2 * p + hh + 1))
                    out_ref[bi, rows, hs] = (hn * gain_ref[:, hs] * gate[:, hs]).astype(BF16)
        return carry

    lax.fori_loop(0, NC, chunk, 0, unroll=2 if NC % 2 == 0 else 1)


def _gla(oa, la, w3, mk, gain, *, B, S, nb, ts):
    N = oa.shape[0]
    oa3 = oa.reshape(B, S, oa.shape[1])
    la3 = la.reshape(B, S, G_KW)
    kern = functools.partial(_gla_kernel, NC=ts // G_CHUNK, nb=nb)
    out = pl.pallas_call(
        kern,
        grid=(B // nb, S // ts),
        in_specs=[pl.BlockSpec((nb, ts, 2 * G_KW), lambda b, t: (b, t, C_GQK // (2 * G_KW))),
                  pl.BlockSpec((nb, ts, G_W), lambda b, t: (b, t, C_GV // G_W)),
                  pl.BlockSpec((nb, ts, G_W), lambda b, t: (b, t, C_GG // G_W)),
                  pl.BlockSpec((nb, ts, G_KW), lambda b, t: (b, t, 0)),
                  pl.BlockSpec(w3.shape, lambda b, t: (0, 0)),
                  pl.BlockSpec(mk.shape, lambda b, t: (0, 0, 0)),
                  pl.BlockSpec((1, G_W), lambda b, t: (0, 0))],
        out_specs=pl.BlockSpec((nb, ts, G_W), lambda b, t: (b, t, 0)),
        out_shape=jax.ShapeDtypeStruct((B, S, G_W), BF16),
        scratch_shapes=[pltpu.VMEM((nb, 2, 2 * G_DV, LANES), F32)],
        compiler_params=_cparams(2),
        name="gla",
    )(oa3, oa3, oa3, la3, w3, mk, gain)
    return out.reshape(N, G_W)


def _layer_norm(z, g, b):
    mu = jnp.mean(z, axis=-1, keepdims=True)
    zc = z - mu
    var = jnp.mean(zc * zc, axis=-1, keepdims=True)
    return zc * lax.rsqrt(var + LN_EPS) * g + b


def _outproj_kernel(hm_ref, hg_ref, wf_ref, x_ref, mod_ref, g_ref, b_ref, wrg_ref, wre_ref, br_ref,
                    x1_ref, u2_ref, rrow_ref, w_ref, wr_ref, *, tb, nh):
    @pl.when(pl.program_id(0) == 0)
    def _():
        w_ref[...] = wf_ref[...].astype(BF16)
        z = lambda n: jnp.zeros((n, wrg_ref.shape[2]), F32)
        wt = jnp.concatenate([wrg_ref[0], z(SUBLANES - N_GROUPS), wre_ref[0],
                              z(LANES - SUBLANES - N_EXP)], axis=0).T
        hi = wt.astype(BF16)
        wr_ref[:, 0:LANES] = hi
        wr_ref[:, LANES:2 * LANES] = (wt - hi.astype(F32)).astype(BF16)

    mod = mod_ref[0]
    blocks = [slice(tb * j, tb * (j + 1)) for j in range(nh)]
    y = [jnp.dot(hm_ref[r, :], w_ref[0:M_W, :], preferred_element_type=F32)
         + jnp.dot(hg_ref[r, :], w_ref[M_W:M_W + G_W, :], preferred_element_type=F32) for r in blocks]
    u2 = []
    for j, r in enumerate(blocks):
        z = ALPHA * x_ref[r, :] + (1.0 + mod[2:3, :]) * y[j]
        x1 = _layer_norm(z, g_ref[...], b_ref[...])
        x1_ref[r, :] = x1
        u2.append(x1 * (1.0 + mod[4:5, :]) + mod[3:4, :])
        u2_ref[r, :] = u2[j].astype(BF16)

    u2h = [u.astype(BF16) for u in u2]
    u2l = [(u2[j] - u2h[j].astype(F32)).astype(BF16) for j in range(nh)]
    lh = [jnp.dot(u, wr_ref[...], preferred_element_type=F32) for u in u2h]
    ll = [jnp.dot(u, wr_ref[:, 0:LANES], preferred_element_type=F32) for u in u2l]
    for j in range(nh):
        logits = lh[j][:, 0:LANES] + lh[j][:, LANES:2 * LANES] + ll[j] + br_ref[...]
        rrow_ref[j] = _route_select(logits.T, tb)


def _route_select(lt, tm):
    row = lax.broadcasted_iota(jnp.int32, (SUBLANES, tm), 0)
    gl = jnp.where(row < N_GROUPS, lt[0:SUBLANES, :], -jnp.inf)
    gmax = jnp.max(gl, axis=0, keepdims=True)
    gsel = jnp.min(jnp.where(gl == gmax, row, SUBLANES), axis=0, keepdims=True)
    pg = 1.0 / jnp.sum(jnp.exp(gl - gmax), axis=0, keepdims=True)
    ein = jnp.zeros((SUBLANES, tm), F32)
    for g in range(N_GROUPS):
        ein = jnp.where(gsel == g, lt[SUBLANES * (g + 1):SUBLANES * (g + 2), :], ein)
    v1 = jnp.max(ein, axis=0, keepdims=True)
    i1 = jnp.min(jnp.where(ein == v1, row, SUBLANES), axis=0, keepdims=True)
    rest = jnp.where(row == i1, -jnp.inf, ein)
    v2 = jnp.max(rest, axis=0, keepdims=True)
    i2 = jnp.min(jnp.where(rest == v2, row, SUBLANES), axis=0, keepdims=True)
    t2 = jnp.exp(v2 - v1)
    p1 = 1.0 / (1.0 + t2)
    e0 = (gsel * E_PER_G + i1).astype(F32)
    e1 = (gsel * E_PER_G + i2).astype(F32)
    return jnp.concatenate([e0, e1, pg * p1, pg * (t2 * p1), jnp.zeros((SUBLANES - 4, tm), F32)], axis=0)


def _outproj(hm, hg, w_out, x2, mod3, g, b, wrg_t, wre_t, br, *, S, tb, nh, layer):
    N, D = x2.shape
    tm = tb * nh
    tpb = S // tm
    kern = functools.partial(_outproj_kernel, tb=tb, nh=nh)
    return pl.pallas_call(
        kern,
        grid=(N // tm,),
        in_specs=[pl.BlockSpec((tm, M_W), lambda i: (i, 0)),
                  pl.BlockSpec((tm, G_W), lambda i: (i, 0)),
                  pl.BlockSpec((M_W + G_W, D), lambda i: (0, 0), pipeline_mode=pl.Buffered(1)),
                  pl.BlockSpec((tm, D), lambda i: (i, 0)),
                  pl.BlockSpec((1, 6, D), lambda i: (i // tpb, 0, 0)),
                  pl.BlockSpec((1, D), lambda i: (0, 0)),
                  pl.BlockSpec((1, D), lambda i: (0, 0)),
                  pl.BlockSpec((1, N_GROUPS, D), lambda i: (layer, 0, 0)),
                  pl.BlockSpec((1, N_EXP, D), lambda i: (layer, 0, 0)),
                  pl.BlockSpec((1, LANES), lambda i: (0, 0))],
        out_specs=[pl.BlockSpec((tm, D), lambda i: (i, 0)),
                   pl.BlockSpec((tm, D), lambda i: (i, 0)),
                   pl.BlockSpec((nh, SUBLANES, tb), lambda i: (i, 0, 0))],
        out_shape=[jax.ShapeDtypeStruct((N, D), F32),
                   jax.ShapeDtypeStruct((N, D), BF16),
                   jax.ShapeDtypeStruct((N // tb, SUBLANES, tb), F32)],
        scratch_shapes=[pltpu.VMEM((M_W + G_W, D), BF16), pltpu.VMEM((D, 2 * LANES), BF16)],
        compiler_params=_cparams(),
        name="outproj",
    )(hm, hg, w_out, x2, mod3, g, b, wrg_t, wre_t, br)


def _slots_per_tile(tb):
    worst = 2 * tb + N_EXP * (GRAN - 1)
    return -(-worst // LANES) * LANES


def _ffn_tiles(n_tok, tb):
    worst_rows = 2 * n_tok + (n_tok // tb) * N_EXP * (GRAN - 1)
    return -(-worst_rows // FFN_TM) + N_EXP


def _route_kernel(rr_ref, u_ref, lt_ref, srow_ref, col_ref, gd_ref, meta_ref, mg_ref, part_ref,
                  *, NT, tb, TM):
    iota_e = lax.broadcasted_iota(jnp.int32, (N_EXP, tb), 0).astype(F32)
    glane = lax.broadcasted_iota(jnp.int32, (N_EXP, LANES), 1).astype(F32)
    ltri = lt_ref[...]

    def prefix_e(col):
        return jnp.dot(ltri, jnp.broadcast_to(col, (N_EXP, LANES)),
                       preferred_element_type=F32, precision=HIGHEST)[:, 0:1]

    def p1(j, run8):
        r = rr_ref[j]
        oh0 = jnp.where(iota_e == r[0:1, :], 1.0, 0.0)
        oh1 = jnp.where(iota_e == r[1:2, :], 1.0, 0.0)
        cum0 = jnp.dot(oh0.astype(BF16), u_ref[...], preferred_element_type=F32)
        cum1 = jnp.dot(oh1.astype(BF16), u_ref[...], preferred_element_type=F32)
        c0 = jnp.sum(oh0, axis=1, keepdims=True)
        n8 = jnp.floor((c0 + jnp.sum(oh1, axis=1, keepdims=True) + (GRAN - 1.0)) * (1.0 / GRAN))
        lo8 = prefix_e(n8)
        s0 = jnp.sum(oh0 * (GRAN * lo8 + cum0 - 1.0), axis=0, keepdims=True)
        s1 = jnp.sum(oh1 * (GRAN * lo8 + c0 + cum1 - 1.0), axis=0, keepdims=True)
        info = jnp.concatenate([s0, s1, r[2:4, :], jnp.zeros((SUBLANES - 4, tb), F32)], axis=0)
        srow_ref[j] = info
        col_ref[pl.ds(pl.multiple_of(j * tb, tb), tb), :] = jnp.concatenate(
            [info, jnp.zeros((LANES - SUBLANES, tb), F32)], axis=0).T
        mg = jnp.where((lo8 <= glane) & (glane < lo8 + n8), 1.0, 0.0)
        mg_ref[j] = mg
        part = jnp.sum(mg * (run8 + glane - lo8), axis=0, keepdims=True)
        gcnt = jnp.broadcast_to(jnp.sum(n8, axis=0, keepdims=True), (1, LANES))
        part_ref[j] = jnp.concatenate([part, gcnt, jnp.zeros((SUBLANES - 2, LANES), F32)], axis=0)
        return run8 + n8

    tot8 = lax.fori_loop(0, NT, p1, jnp.zeros((N_EXP, 1), F32), unroll=8 if NT % 8 == 0 else 1)
    seg_t = jnp.floor((tot8 * GRAN + (TM - 1.0)) * (1.0 / TM))
    base_t = prefix_e(seg_t)
    base8 = base_t * (TM // GRAN)
    lane1 = lax.broadcasted_iota(jnp.int32, (1, LANES), 1)

    def p2(j, carry):
        pr = part_ref[j]
        dst = (pr[0:1, :] + jnp.sum(mg_ref[j] * base8, axis=0, keepdims=True)) * GRAN
        gd_ref[j] = jnp.where(lane1 == G_LAST, pr[1:2, :], dst).astype(jnp.int32)
        return carry

    lax.fori_loop(0, NT, p2, 0, unroll=8 if NT % 8 == 0 else 1)
    eye = jnp.where(glane == lax.broadcasted_iota(jnp.int32, (N_EXP, LANES), 0).astype(F32), 1.0, 0.0)
    tail_row = jnp.sum(eye * ((base8 + tot8) * GRAN), axis=0, keepdims=True)
    tail_n8 = jnp.sum(eye * (seg_t * (TM // GRAN) - tot8), axis=0, keepdims=True)
    nv_l = jnp.broadcast_to(jnp.sum(seg_t, axis=0, keepdims=True), (1, LANES))
    gd_ref[NT] = jnp.where(lane1 == G_LAST, nv_l, tail_row).astype(jnp.int32)
    gd_ref[NT + 1] = tail_n8.astype(jnp.int32)
    ti = lax.broadcasted_iota(jnp.int32, (N_EXP, tb), 1).astype(F32)
    te = jnp.sum(jnp.where(base_t <= ti, 1.0, 0.0), axis=0, keepdims=True) - 1.0
    nv = jnp.broadcast_to(jnp.sum(seg_t, axis=0, keepdims=True), (1, tb))
    own = jnp.where((base_t <= ti) & (ti < base_t + seg_t), 1.0, 0.0)
    vr = jnp.sum(own * jnp.clip(tot8 * GRAN - (ti - base_t) * TM, 0.0, TM), axis=0, keepdims=True)
    meta_ref[...] = jnp.concatenate([te, nv, vr, jnp.zeros((SUBLANES - 3, tb), F32)],
                                    axis=0).astype(jnp.int32)


def _route(rrow, u_cnt, ltri, *, TM):
    NT, _, tb = rrow.shape
    kern = functools.partial(_route_kernel, NT=NT, tb=tb, TM=TM)
    full3 = lambda i: (0, 0, 0)
    return pl.pallas_call(
        kern,
        grid=(1,),
        in_specs=[pl.BlockSpec((NT, SUBLANES, tb), full3),
                  pl.BlockSpec((tb, tb), lambda i: (0, 0)),
                  pl.BlockSpec((N_EXP, N_EXP), lambda i: (0, 0))],
        out_specs=[pl.BlockSpec((NT, SUBLANES, tb), full3),
                   pl.BlockSpec((NT * tb, LANES), lambda i: (0, 0)),
                   pl.BlockSpec((NT + 2, 1, LANES), full3),
                   pl.BlockSpec((SUBLANES, tb), lambda i: (0, 0))],
        out_shape=[jax.ShapeDtypeStruct((NT, SUBLANES, tb), F32),
                   jax.ShapeDtypeStruct((NT * tb, LANES), F32),
                   jax.ShapeDtypeStruct((NT + 2, 1, LANES), jnp.int32),
                   jax.ShapeDtypeStruct((SUBLANES, tb), jnp.int32)],
        scratch_shapes=[pltpu.VMEM((NT, N_EXP, LANES), F32), pltpu.VMEM((NT, SUBLANES, LANES), F32)],
        compiler_params=_cparams(),
        name="route",
    )(rrow, u_cnt, ltri)


_HI_MASK = 0xFFFF0000


def _pack_halves(x):
    c = x.shape[1] // 2
    lo = lax.bitcast_convert_type(x[:, :c], U32)
    hi = lax.bitcast_convert_type(x[:, c:], U32)
    return (lo >> 16) | (hi & U32(_HI_MASK))


def _unpack_halves(w):
    lo = lax.bitcast_convert_type(w << 16, F32)
    hi = lax.bitcast_convert_type(w & U32(_HI_MASK), F32)
    return jnp.concatenate([lo, hi], axis=1).astype(BF16)


def _granule_copy(src_ref, src_row, dst_ref, dst_row, sem, n=1):
    cols = pl.ds(0, min(src_ref.shape[-1], dst_ref.shape[-1]))
    return pltpu.make_async_copy(src_ref.at[pl.ds(src_row, n * GRAN), cols],
                                 dst_ref.at[pl.ds(dst_row, n * GRAN), cols], sem)


def _for_granules(n, body, unroll=8):
    def blk(i, carry):
        for t in range(unroll):
            body(i * unroll + t)
        return carry

    def one(g, carry):
        body(g)
        return carry

    nblk = n // unroll
    lax.fori_loop(0, nblk, blk, 0)
    lax.fori_loop(nblk * unroll, n, one, 0)


def _wait_granules(n, src_ref, dst_ref, sem, n_max):
    b = 1
    while b <= n_max:
        @pl.when((n & b) != 0)
        def _(b=b):
            _granule_copy(src_ref, 0, dst_ref, 0, sem, n=b).wait()
        b *= 2


def _dispatch_kernel(gd_ref, srow_ref, u_ref, xs_ref, buf, zbuf, sems, *, NT, SL, TM, n_tiles):
    j = pl.program_id(0)
    slot = j % 2
    zsem = sems.at[2]

    def drain(tile, sl):
        _wait_granules(gd_ref[tile, G_LAST], buf.at[sl], xs_ref, sems.at[sl], SL // GRAN)

    def tile_fill(t):
        return pltpu.make_async_copy(zbuf, xs_ref.at[pl.ds(pl.multiple_of(t * TM, TM), TM), :], zsem)

    def zero_fill(wait):
        for e in range(N_EXP):
            n, row0 = gd_ref[NT + 1, e], gd_ref[NT, e]
            b = TM // GRAN // 2
            while b >= 1:
                @pl.when((n & b) != 0)
                def _(b=b, n=n, row0=row0):
                    start = pl.multiple_of(row0 + ((n >> b.bit_length()) << b.bit_length()) * GRAN, GRAN)
                    cp = pltpu.make_async_copy(zbuf.at[pl.ds(0, b * GRAN), :],
                                               xs_ref.at[pl.ds(start, b * GRAN), :], zsem)
                    cp.wait() if wait else cp.start()
                b //= 2

        def zt(t, carry):
            tile_fill(t).wait() if wait else tile_fill(t).start()
            return carry
        lax.fori_loop(gd_ref[NT, G_LAST], n_tiles, zt, 0)

    @pl.when(j == 0)
    def _():
        zbuf[...] = jnp.zeros_like(zbuf)
        zero_fill(False)

    @pl.when(j >= 2)
    def _():
        drain(j - 2, slot)

    s = srow_ref[0]
    rows = lax.broadcasted_iota(jnp.int32, (SL, s.shape[1]), 0).astype(F32)
    m0 = rows == s[0:1, :]
    m1 = rows == s[1:2, :]
    oh = jnp.where(m0 | m1, 1.0, 0.0).astype(BF16)
    dw = u_ref.shape[1] // 2
    buf[slot, :, 0:dw] = _pack_halves(jnp.dot(oh, u_ref[...], preferred_element_type=F32))
    wrow = jnp.sum(jnp.where(m0, s[2:3, :], 0.0) + jnp.where(m1, s[3:4, :], 0.0), axis=1, keepdims=True)
    buf[slot, :, dw:dw + LANES] = lax.bitcast_convert_type(jnp.broadcast_to(wrow, (SL, LANES)), U32)

    def issue(g):
        _granule_copy(buf.at[slot], pl.multiple_of(g * GRAN, GRAN), xs_ref,
                      pl.multiple_of(gd_ref[j, g], GRAN), sems.at[slot]).start()

    _for_granules(gd_ref[j, G_LAST], issue)

    @pl.when(j == NT - 1)
    def _():
        drain(j, slot)
        if NT > 1:
            drain(j - 1, 1 - slot)
        zero_fill(True)


def _dispatch(gd, srow, u2, *, n_tiles, TM):
    N, D = u2.shape
    NT, _, tb = srow.shape
    SL = _slots_per_tile(tb)
    n_rows = n_tiles * TM
    kern = functools.partial(_dispatch_kernel, NT=NT, SL=SL, TM=TM, n_tiles=n_tiles)
    grid_spec = pltpu.PrefetchScalarGridSpec(
        num_scalar_prefetch=1,
        grid=(NT,),
        in_specs=[pl.BlockSpec((1, SUBLANES, tb), lambda j, gd: (j, 0, 0)),
                  pl.BlockSpec((tb, D), lambda j, gd: (j, 0))],
        out_specs=pl.BlockSpec(memory_space=pl.ANY),
        scratch_shapes=[pltpu.VMEM((2, SL, D // 2 + LANES), U32), pltpu.VMEM((TM, D // 2 + LANES), U32),
                        pltpu.SemaphoreType.DMA((3,))],
    )
    return pl.pallas_call(
        kern,
        grid_spec=grid_spec,
        out_shape=jax.ShapeDtypeStruct((n_rows, D // 2 + LANES), U32),
        compiler_params=_cparams(),
        name="dispatch",
    )(gd, srow, u2)


def _ffn_kernel(meta_ref, xs_ref, wg_ref, wu_ref, wd_ref, o_ref, wgb, wub, wdb, sg, su, sd, slot_ref, sems):
    i = pl.program_id(0)
    nv = meta_ref[TILES_USED, 0]
    tile_expert = lambda t: meta_ref[TILE_EXPERT, t]
    e = tile_expert(i)

    def weight_copies(ex, sl):
        return (pltpu.make_async_copy(wg_ref.at[ex], sg.at[sl], sems.at[sl]),
                pltpu.make_async_copy(wu_ref.at[ex], su.at[sl], sems.at[sl]),
                pltpu.make_async_copy(wd_ref.at[ex], sd.at[sl], sems.at[sl]))

    @pl.when(i == 0)
    def _():
        slot_ref[0] = 0
        for cp in weight_copies(e, 0):
            cp.start()

    new_expert = (i < nv) & ((i == 0) | (e != tile_expert(jnp.maximum(i - 1, 0))))

    @pl.when(new_expert)
    def _():
        sl = slot_ref[0]
        for cp in weight_copies(e, sl):
            cp.wait()
        nxt = lax.while_loop(lambda t: (t < nv) & (tile_expert(jnp.minimum(t, nv - 1)) == e), lambda t: t + 1, i + 1)

        @pl.when(nxt < nv)
        def _():
            for cp in weight_copies(tile_expert(nxt), 1 - sl):
                cp.start()

    hm = xs_ref.shape[0] // FFN_SUB
    dw = xs_ref.shape[1] - LANES

    def swiglu_rows(nsub, cast):
        if cast:
            sl = slot_ref[0]
            wgb[...] = sg[sl].astype(BF16)
            wub[...] = su[sl].astype(BF16)
            wdb[...] = sd[sl].astype(BF16)
            slot_ref[0] = 1 - sl
        halves = tuple(slice(hm * j, hm * (j + 1)) for j in range(nsub))
        x = [_unpack_halves(xs_ref[r, 0:dw]) for r in halves]
        g = [jnp.dot(x[j], wgb[...], preferred_element_type=F32) for j in range(nsub)]
        u = [jnp.dot(x[j], wub[...], preferred_element_type=F32) for j in range(nsub)]
        h = [(g[j] * _sigmoid(g[j]) * u[j]).astype(BF16) for j in range(nsub)]
        y = [jnp.dot(h[j], wdb[...], preferred_element_type=F32) for j in range(nsub)]
        for j in range(nsub):
            wt = lax.bitcast_convert_type(xs_ref[halves[j], dw:dw + LANES], F32)
            yw = y[j] * jnp.concatenate([wt] * (2 * dw // LANES), axis=1)
            o_ref[halves[j], 0:dw] = _pack_halves(yw.astype(BF16).astype(F32))
            o_ref[halves[j], dw:dw + LANES] = xs_ref[halves[j], dw:dw + LANES]
        if nsub < FFN_SUB:
            o_ref[hm * nsub:, :] = jnp.zeros((hm * (FFN_SUB - nsub), dw + LANES), U32)

    used = meta_ref[TILE_ROWS, i]
    for nsub in range(1, FFN_SUB + 1):
        lo, hi = hm * (nsub - 1), hm * nsub
        rows_here = (i < nv) & (used > lo) & ((used <= hi) if nsub < FFN_SUB else True)
        pl.when(rows_here & new_expert)(functools.partial(swiglu_rows, nsub, True))
        pl.when(rows_here & jnp.logical_not(new_expert))(functools.partial(swiglu_rows, nsub, False))


def _ffn(meta, xs, wg, wu, wd, *, TM):
    P, XW = xs.shape
    DW = XW - LANES
    D = 2 * DW
    n_tiles = P // TM
    assert n_tiles <= meta.shape[1]
    used_tile = lambda i, meta: (jnp.maximum(jnp.minimum(i, meta[TILES_USED, 0] - 1), 0), 0)
    grid_spec = pltpu.PrefetchScalarGridSpec(
        num_scalar_prefetch=1,
        grid=(n_tiles,),
        in_specs=[pl.BlockSpec((TM, XW), used_tile),
                  pl.BlockSpec(memory_space=pl.ANY),
                  pl.BlockSpec(memory_space=pl.ANY),
                  pl.BlockSpec(memory_space=pl.ANY)],
        out_specs=pl.BlockSpec((TM, XW), used_tile),
        scratch_shapes=[pltpu.VMEM((D, D_EXP), BF16), pltpu.VMEM((D, D_EXP), BF16),
                        pltpu.VMEM((D_EXP, D), BF16),
                        pltpu.VMEM((2, D, D_EXP), F32), pltpu.VMEM((2, D, D_EXP), F32),
                        pltpu.VMEM((2, D_EXP, D), F32), pltpu.SMEM((1,), jnp.int32),
                        pltpu.SemaphoreType.DMA((2,))],
    )
    return pl.pallas_call(
        _ffn_kernel,
        grid_spec=grid_spec,
        out_shape=jax.ShapeDtypeStruct((P, XW), U32),
        input_output_aliases={1: 0},
        compiler_params=_cparams(),
        name="ffn",
    )(meta, xs, wg, wu, wd)


def _combine_kernel(gd_ref, ys_ref, col_ref, x1_ref, mod_ref, g_ref, b_ref, o_ref, buf, sems, *, NT, SL):
    j = pl.program_id(0)
    slot = j % 2

    def fetch(tile, sl):
        def f(g):
            _granule_copy(ys_ref, pl.multiple_of(gd_ref[tile, g], GRAN), buf.at[sl],
                          pl.multiple_of(g * GRAN, GRAN), sems.at[sl]).start()
        _for_granules(gd_ref[tile, G_LAST], f)

    @pl.when(j == 0)
    def _():
        fetch(0, 0)

    @pl.when(j + 1 < NT)
    def _():
        fetch(j + 1, 1 - slot)

    ng = gd_ref[j, G_LAST]

    _wait_granules(ng, ys_ref, buf.at[slot], sems.at[slot], SL // GRAN)

    rows = lax.broadcasted_iota(jnp.int32, (SL, 1), 0)
    yb = _unpack_halves(jnp.where(rows < ng * GRAN, buf[slot], U32(0)))
    col = col_ref[...]
    tb = col.shape[0]
    lanes = lax.broadcasted_iota(jnp.int32, (tb, SL), 1).astype(F32)
    sel = jnp.where((lanes == col[:, 0:1]) | (lanes == col[:, 1:2]), 1.0, 0.0).astype(BF16)
    y = jnp.dot(sel, yb, preferred_element_type=F32)
    mod = mod_ref[0]
    z = ALPHA * x1_ref[...] + (1.0 + mod[5:6, :]) * y
    o_ref[...] = _layer_norm(z, g_ref[...], b_ref[...])


def _combine(gd, ys, col, x1, mod3, g, b, *, S, tb):
    N, D = x1.shape
    NT = N // tb
    tpb = S // tb
    SL = _slots_per_tile(tb)
    kern = functools.partial(_combine_kernel, NT=NT, SL=SL)
    grid_spec = pltpu.PrefetchScalarGridSpec(
        num_scalar_prefetch=1,
        grid=(NT,),
        in_specs=[pl.BlockSpec(memory_space=pl.ANY),
                  pl.BlockSpec((tb, LANES), lambda j, gd: (j, 0)),
                  pl.BlockSpec((tb, D), lambda j, gd: (j, 0)),
                  pl.BlockSpec((1, 6, D), lambda j, gd: (j // tpb, 0, 0)),
                  pl.BlockSpec((1, D), lambda j, gd: (0, 0)),
                  pl.BlockSpec((1, D), lambda j, gd: (0, 0))],
        out_specs=pl.BlockSpec((tb, D), lambda j, gd: (j, 0)),
        scratch_shapes=[pltpu.VMEM((2, SL, D // 2), U32), pltpu.SemaphoreType.DMA((2,))],
    )
    return pl.pallas_call(
        kern,
        grid_spec=grid_spec,
        out_shape=jax.ShapeDtypeStruct((N, D), F32),
        compiler_params=_cparams(),
        name="combine",
    )(gd, ys, col, x1, mod3, g, b)


def _layer(x, c, l, w_ada, b_ada, w_in, w_conv, b_conv, b_igate, b_fgate, mlstm_norm_g, w_gla_a, b_gla_a,
           gla_norm_g, w_out, ln1_g, ln1_b, w_route_group, b_route_group, w_route_expert, b_route_expert,
           w_gate, w_up, w_down, ln2_g, ln2_b):
    B, S, D = x.shape
    N = B * S
    x2 = x.reshape(N, D)
    tm_in = min(512, S)
    tm = min(256, S)
    lm = min(256, S)
    assert S % tm_in == 0 and S % tm == 0 and tm_in % lm == 0 and S % G_CHUNK == 0
    assert w_in.shape[1:] == (D, IN_TOT) and w_gate.shape[1:] == (N_EXP, D, D_EXP)

    mod3 = _ada(c, w_ada[l], b_ada[l]).reshape(B, 6, D)

    wa_pad = jnp.pad(w_gla_a[l], ((SM_A, LANES - SM_A - G_RANK), (0, 0))).astype(BF16)
    bg = jnp.concatenate([jnp.pad(b_igate[l], (0, SUBLANES - M_HEADS)),
                          jnp.pad(b_fgate[l], (0, SUBLANES - M_HEADS))]).reshape(2 * SUBLANES, 1)
    oa, la, g3 = _inproj(x2, mod3, jnp.swapaxes(w_in, 1, 2), w_conv[l], b_conv[l].reshape(1, -1), wa_pad,
                         b_gla_a[l].reshape(1, -1), bg, S=S, tm=tm_in, lm=lm, layer=l)

    u_tri = jnp.asarray(np.triu(np.ones((lm, lm), np.float32)))
    nb = 4 if B % 4 == 0 else (2 if B % 2 == 0 else 1)
    ts = min(512, S)
    hm = _mlstm(oa, g3, u_tri, mlstm_norm_g[l].reshape(1, -1), B=B, S=S, L=lm, nb=nb, ts=ts)
    w3_np, mk_np = _gla_consts()
    hg = _gla(oa, la, jnp.asarray(w3_np, BF16), jnp.asarray(mk_np), gla_norm_g[l].reshape(1, -1), B=B, S=S,
              nb=nb, ts=ts)

    br = jnp.concatenate([jnp.pad(b_route_group[l], (0, SUBLANES - N_GROUPS)),
                          jnp.pad(b_route_expert[l], (0, LANES - SUBLANES - N_EXP))]).reshape(1, LANES)
    x1, u2, rrow = _outproj(hm, hg, w_out[l], x2, mod3, ln1_g[l].reshape(1, -1), ln1_b[l].reshape(1, -1),
                            jnp.swapaxes(w_route_group, 1, 2), jnp.swapaxes(w_route_expert, 1, 2), br,
                            S=S, tb=tm, nh=4 if S % (4 * tm) == 0 else 1, layer=l)

    u_cnt = jnp.asarray(np.triu(np.ones((tm, tm), np.float32)), BF16)
    ltri = jnp.asarray(np.tril(np.ones((N_EXP, N_EXP), np.float32), -1))
    srow, col, gd3, meta = _route(rrow, u_cnt, ltri, TM=FFN_TM)
    gd = gd3.reshape(N // tm + 2, LANES)
    n_tiles = _ffn_tiles(N, tm)

    xs = _dispatch(gd, srow, u2, n_tiles=n_tiles, TM=FFN_TM)
    ys = _ffn(meta, xs, w_gate[l], w_up[l], w_down[l], TM=FFN_TM)
    out = _combine(gd, ys, col, x1, mod3, ln2_g[l].reshape(1, -1), ln2_b[l].reshape(1, -1), S=S, tb=tm)
    return out.reshape(B, S, D)


def kernel(x, c, w_ada, b_ada, w_in, w_conv, b_conv, b_igate, b_fgate, mlstm_norm_g, w_gla_a, b_gla_a,
           gla_norm_g, w_out, ln1_g, ln1_b, w_route_group, b_route_group, w_route_expert, b_route_expert,
           w_gate, w_up, w_down, ln2_g, ln2_b):
    for l in range(DEPTH):
        x = _layer(x, c, l, w_ada, b_ada, w_in, w_conv, b_conv, b_igate, b_fgate, mlstm_norm_g, w_gla_a,
                   b_gla_a, gla_norm_g, w_out, ln1_g, ln1_b, w_route_group, b_route_group, w_route_expert,
                   b_route_expert, w_gate, w_up, w_down, ln2_g, ln2_b)
    return x
```

```python
import functools

import numpy as np
import jax
import jax.numpy as jnp
from jax import lax
from jax.experimental import pallas as pl
from jax.experimental.pallas import tpu as pltpu

F32 = jnp.float32
BF16 = jnp.bfloat16
U32 = jnp.uint32
HIGHEST = lax.Precision.HIGHEST

DEPTH = 1
M_HEADS = 4
M_HD = 128
M_W = M_HEADS * M_HD
CONV_W = 4
G_HEADS = 4
G_DK = 64
G_DV = 128
G_W = G_HEADS * G_DV
G_KW = G_HEADS * G_DK
G_RANK = 16
G_TAU = 16.0
G_CHUNK = 64
N_GROUPS = 4
E_PER_G = 8
N_EXP = N_GROUPS * E_PER_G
D_EXP = 512
ALPHA = (2 * DEPTH) ** 0.25
LN_EPS = 1e-5

LANES = 128
SUBLANES = 8
VMEM_LIMIT = 48 * 1024 * 1024

C_QK = 0
C_VO = 1024
C_GQK = 2048
C_GV = 2560
C_GG = 3072
C_SMALL = 3584
C_TOT = 3712
SM_I, SM_F, SM_A = 0, 8, 16
IN_GATES = 4 * M_W
IN_G = IN_GATES + 2 * M_HEADS
IN_GA = IN_G + 2 * G_KW + 2 * G_W
IN_TOT = IN_GA + G_RANK

FFN_TM = 512
FFN_SUB = 2
GRAN = SUBLANES
G_LAST = LANES - 1
ZERO_COPIES = 8
TILE_EXPERT, TILES_USED, TILE_ROWS = 0, 1, 2


def _cparams(n_axes=1):
    return pltpu.CompilerParams(dimension_semantics=("arbitrary",) * n_axes,
                                vmem_limit_bytes=VMEM_LIMIT)


def _sigmoid(x):
    return 1.0 / (1.0 + jnp.exp(-x))


def _log_sigmoid(x):
    return jnp.minimum(x, 0.0) - jnp.log(1.0 + jnp.exp(-jnp.abs(x)))


def _ada_kernel(c_ref, w_ref, b_ref, o_ref):
    c = c_ref[...]
    ca = (c * _sigmoid(c)).astype(BF16)
    o_ref[...] = jnp.dot(ca, w_ref[...].astype(BF16), preferred_element_type=F32) + b_ref[...]


def _ada(c, w, b):
    B, D = c.shape
    n_out = w.shape[1]
    tn = 1024
    return pl.pallas_call(
        _ada_kernel,
        grid=(n_out // tn,),
        in_specs=[pl.BlockSpec((B, D), lambda j: (0, 0)),
                  pl.BlockSpec((D, tn), lambda j: (0, j)),
                  pl.BlockSpec((1, tn), lambda j: (0, j))],
        out_specs=pl.BlockSpec((B, tn), lambda j: (0, j)),
        out_shape=jax.ShapeDtypeStruct((B, n_out), F32),
        compiler_params=_cparams(),
        name="ada",
    )(c, w, b.reshape(1, n_out))


def _inproj_kernel(x_ref, mod_ref, win_ref, wc_ref, bc_ref, wa_ref, ba_ref, bg_ref,
                   oa_ref, la_ref, g_ref, halo_ref, w_ref, *, tm, tpb, lm):
    i = pl.program_id(0)

    @pl.when(i == 0)
    def _():
        rc = 2 * LANES
        for r in range(0, IN_GATES, rc):
            w_ref[:, r:r + rc] = win_ref[0, r:r + rc, :].T.astype(BF16)
        for r in range(0, C_SMALL - C_GQK, rc):
            w_ref[:, C_GQK + r:C_GQK + r + rc] = win_ref[0, IN_G + r:IN_G + r + rc, :].T.astype(BF16)
        gates = win_ref[0, IN_GATES:IN_G, :]
        z = lambda n: jnp.zeros((n, gates.shape[1]), F32)
        small = jnp.concatenate([gates[0:M_HEADS], z(SM_F - M_HEADS), gates[M_HEADS:2 * M_HEADS],
                                 z(SM_A - SM_F - M_HEADS), win_ref[0, IN_GA:IN_TOT, :],
                                 z(LANES - SM_A - G_RANK)], axis=0)
        w_ref[:, C_SMALL:C_TOT] = small.T.astype(BF16)

    @pl.when(i % tpb == 0)
    def _():
        halo_ref[0:SUBLANES, :] = jnp.zeros((SUBLANES, halo_ref.shape[1]), F32)

    mod = mod_ref[0]
    u = (x_ref[...] * (1.0 + mod[1:2, :]) + mod[0:1, :]).astype(BF16)

    def proj(c0, c1):
        return jnp.dot(u, w_ref[:, c0:c1], preferred_element_type=F32)

    p = proj(C_QK, C_QK + 2 * M_W)
    halo_ref[SUBLANES:SUBLANES + tm, :] = p
    acc = bc_ref[...] + wc_ref[CONV_W - 1:CONV_W, :] * p
    for j in range(CONV_W - 1):
        acc = acc + wc_ref[j:j + 1, :] * halo_ref[pl.ds(SUBLANES - (CONV_W - 1) + j, tm), :]
    halo_ref[0:SUBLANES, :] = p[tm - SUBLANES:, :]
    qk = acc * _sigmoid(acc)
    oa_ref[:, C_QK:C_QK + M_W] = qk[:, :M_W].astype(BF16)
    oa_ref[:, C_QK + M_W:C_QK + 2 * M_W] = (qk[:, M_W:] * (M_HD ** -0.5)).astype(BF16)

    ps = proj(C_SMALL, C_TOT)
    la = jnp.dot(ps.astype(BF16), wa_ref[...], preferred_element_type=F32) + ba_ref[...]
    la_ref[...] = _log_sigmoid(la) * (1.0 / G_TAU)
    pt = ps.T
    gi = pt[SM_I:SM_I + SUBLANES, :] + bg_ref[0:SUBLANES, :]
    gf = _log_sigmoid(pt[SM_F:SM_F + SUBLANES, :] + bg_ref[SUBLANES:2 * SUBLANES, :])
    for j in range(tm // lm):
        g_ref[j, 0:SUBLANES, :] = gi[:, j * lm:(j + 1) * lm]
        g_ref[j, SUBLANES:2 * SUBLANES, :] = gf[:, j * lm:(j + 1) * lm]

    p = proj(C_VO, C_VO + 2 * M_W)
    oa_ref[:, C_VO:C_VO + 2 * M_W] = p.astype(BF16)

    p = proj(C_GQK, C_GQK + G_KW)
    oa_ref[:, C_GQK:C_GQK + G_KW] = (p * (G_DK ** -0.5)).astype(BF16)
    p = proj(C_GQK + G_KW, C_SMALL)
    oa_ref[:, C_GQK + G_KW:C_SMALL] = p.astype(BF16)


def _inproj(x2, mod3, w_in, w_conv, b_conv, wa_pad, b_gla, bg, *, S, tm, lm, layer):
    N, D = x2.shape
    tpb = S // tm
    kern = functools.partial(_inproj_kernel, tm=tm, tpb=tpb, lm=lm)
    return pl.pallas_call(
        kern,
        grid=(N // tm,),
        in_specs=[pl.BlockSpec((tm, D), lambda i: (i, 0)),
                  pl.BlockSpec((1, 6, D), lambda i: (i // tpb, 0, 0)),
                  pl.BlockSpec((1, IN_TOT, D), lambda i: (layer, 0, 0), pipeline_mode=pl.Buffered(1)),
                  pl.BlockSpec((CONV_W, 2 * M_W), lambda i: (0, 0)),
                  pl.BlockSpec((1, 2 * M_W), lambda i: (0, 0)),
                  pl.BlockSpec((LANES, G_KW), lambda i: (0, 0)),
                  pl.BlockSpec((1, G_KW), lambda i: (0, 0)),
                  pl.BlockSpec((2 * SUBLANES, 1), lambda i: (0, 0))],
        out_specs=[pl.BlockSpec((tm, C_SMALL), lambda i: (i, 0)),
                   pl.BlockSpec((tm, G_KW), lambda i: (i, 0)),
                   pl.BlockSpec((tm // lm, 2 * SUBLANES, lm), lambda i: (i, 0, 0))],
        out_shape=[jax.ShapeDtypeStruct((N, C_SMALL), BF16),
                   jax.ShapeDtypeStruct((N, G_KW), F32),
                   jax.ShapeDtypeStruct((N // lm, 2 * SUBLANES, lm), F32)],
        scratch_shapes=[pltpu.VMEM((SUBLANES + tm, 2 * M_W), F32), pltpu.VMEM((D, C_TOT), BF16)],
        compiler_params=_cparams(),
        name="inproj",
    )(x2, mod3, w_in, w_conv, b_conv, wa_pad, b_gla, bg)


def _mlstm_sel():
    sel = np.zeros((2 * LANES, 2 * M_HEADS * M_HD), np.float32)
    for j in range(2 * M_HEADS):
        src = (SUBLANES if j < M_HEADS else 3 * SUBLANES) + j % M_HEADS
        sel[src, M_HD * j:M_HD * (j + 1)] = 1.0
        sel[LANES + src, M_HD * j:M_HD * (j + 1)] = 1.0
    return sel


def _mlstm_kernel(qk_ref, vo_ref, g_ref, u_ref, gain_ref, sel_ref, out_ref, c_ref, zt_ref, a_ref, dec_ref, m_ref,
                  *, L, NC, nb):
    @pl.when(pl.program_id(1) == 0)
    def _():
        c_ref[...] = jnp.zeros_like(c_ref)
        m_ref[...] = jnp.zeros_like(m_ref)

    tril = (lax.broadcasted_iota(jnp.int32, (L, L), 0) >= lax.broadcasted_iota(jnp.int32, (L, L), 1))
    ones_v = jnp.ones((L, M_HD), BF16)
    zpad = jnp.zeros((LANES - 4 * SUBLANES, L), F32)
    zgroup = lax.broadcasted_iota(jnp.int32, (L, LANES), 1) // SUBLANES
    factor_cols = (zgroup == 1) | (zgroup == 3)

    order = [(bi, c) for bi in range(nb) for c in range(NC)]
    f_all = jnp.concatenate([g_ref[bi, c, SUBLANES:2 * SUBLANES, :] for bi, c in order], axis=0)
    i_all = jnp.concatenate([g_ref[bi, c, 0:SUBLANES, :] for bi, c in order], axis=0)
    b_all = jnp.dot(f_all, u_ref[...], preferred_element_type=F32, precision=HIGHEST)
    a_all = i_all - b_all
    lane_all = lax.broadcasted_iota(jnp.int32, a_all.shape, 1)
    g_all = a_all
    s = 1
    while s < L:
        g_all = jnp.maximum(g_all, jnp.where(lane_all >= s, pltpu.roll(g_all, s, 1), -jnp.inf))
        s *= 2
    for bi in range(nb):
        m_prev = m_ref[bi][:, 0:1]
        for c in range(NC):
            ci = bi * NC + c
            r8 = slice(SUBLANES * ci, SUBLANES * (ci + 1))
            a, b = a_all[r8], b_all[r8]
            a_ref[ci] = a
            M = jnp.maximum(g_all[r8], m_prev)
            ML = M[:, L - 1:L]
            Z = jnp.concatenate([M, jnp.exp(m_prev - M), jnp.exp(-(b + M)), jnp.exp(a - ML), zpad],
                                axis=0)
            zt_ref[ci] = Z.T
            dec_ref[ci] = jnp.broadcast_to(jnp.exp(m_prev - ML), (SUBLANES, 2 * M_HD))
            m_prev = b[:, L - 1:L] + ML
        m_ref[bi] = jnp.broadcast_to(m_prev, (SUBLANES, LANES))

    chains = [(bi, h) for bi in range(nb) for h in range(M_HEADS)]
    nt = (((1,), (1,)), ((), ()))
    tn = (((0,), (0,)), ((), ()))

    def chunk(c, carry):
        rows = pl.ds(pl.multiple_of(c * L, L), L)
        Zt = [zt_ref[bi * NC + c] for bi in range(nb)]
        a = [a_ref[bi * NC + c] for bi in range(nb)]
        dec = [dec_ref[bi * NC + c] for bi in range(nb)]
        hs = [slice(h * M_HD, (h + 1) * M_HD) for h in range(M_HEADS)]
        hs2 = [slice(M_W + h * M_HD, M_W + (h + 1) * M_HD) for h in range(M_HEADS)]
        q = [qk_ref[bi, rows, hs[h]] for bi, h in chains]
        k = [qk_ref[bi, rows, hs2[h]] for bi, h in chains]
        vext = [jnp.concatenate([vo_ref[bi, rows, hs[h]], ones_v], axis=1) for bi, h in chains]
        cst = [c_ref[bi * M_HEADS + h] for bi, h in chains]
        n = range(len(chains))
        sc = [lax.dot_general(q[i], k[i], nt, preferred_element_type=F32) for i in n]
        qc = [jnp.dot(q[i], cst[i].astype(BF16), preferred_element_type=F32) for i in n]
        pm = [(sc[i] * jnp.exp(jnp.where(tril, a[bi][h:h + 1, :] - Zt[bi][:, h:h + 1], -jnp.inf))).astype(BF16)
              for i, (bi, h) in enumerate(chains)]
        pv = [jnp.dot(pm[i], vext[i], preferred_element_type=F32) for i in n]
        rep = []
        for bi in range(nb):
            zf = jnp.where(factor_cols, Zt[bi], 0.0)
            zh = zf.astype(BF16)
            zl = (zf - zh.astype(F32)).astype(BF16)
            rep.append(jnp.dot(jnp.concatenate([zh, zl], axis=1), sel_ref[...], preferred_element_type=F32))
        e_inter = [rep[bi][:, M_HD * h:M_HD * (h + 1)] for bi, h in chains]
        w_state = [rep[bi][:, M_HD * (M_HEADS + h):M_HD * (M_HEADS + h + 1)] for bi, h in chains]
        kw = [(w_state[i] * k[i].astype(F32)).astype(BF16) for i in n]
        upd = [lax.dot_general(kw[i], vext[i], tn, preferred_element_type=F32) for i in n]
        for i, (bi, h) in enumerate(chains):
            c_ref[bi * M_HEADS + h] = dec[bi][h:h + 1, :] * cst[i] + upd[i]
            nd = pv[i] + jnp.concatenate([e_inter[i], e_inter[i]], axis=1) * qc[i]
            hh = nd[:, :M_HD] / jnp.maximum(jnp.abs(nd[:, M_HD:]),
                                            Zt[bi][:, 2 * SUBLANES + h:2 * SUBLANES + h + 1])
            hh = _sigmoid(vo_ref[bi, rows, hs2[h]].astype(F32)) * hh
            hn = hh * lax.rsqrt(jnp.mean(hh * hh, axis=-1, keepdims=True) + LN_EPS)
            out_ref[bi, rows, hs[h]] = (hn * gain_ref[:, hs[h]]).astype(BF16)
        return carry

    lax.fori_loop(0, NC, chunk, 0, unroll=True)


def _mlstm(oa, g3, u_tri, gain, *, B, S, L, nb, ts):
    N = oa.shape[0]
    NC = ts // L
    oa3 = oa.reshape(B, S, oa.shape[1])
    g4 = g3.reshape(B, S // L, 2 * SUBLANES, L)
    sel = jnp.asarray(_mlstm_sel(), BF16)
    kern = functools.partial(_mlstm_kernel, L=L, NC=NC, nb=nb)
    out = pl.pallas_call(
        kern,
        grid=(B // nb, S // ts),
        in_specs=[pl.BlockSpec((nb, ts, 2 * M_W), lambda b, t: (b, t, C_QK // (2 * M_W))),
                  pl.BlockSpec((nb, ts, 2 * M_W), lambda b, t: (b, t, C_VO // (2 * M_W))),
                  pl.BlockSpec((nb, NC, 2 * SUBLANES, L), lambda b, t: (b, t, 0, 0)),
                  pl.BlockSpec((L, L), lambda b, t: (0, 0)),
                  pl.BlockSpec((1, M_W), lambda b, t: (0, 0)),
                  pl.BlockSpec(sel.shape, lambda b, t: (0, 0))],
        out_specs=pl.BlockSpec((nb, ts, M_W), lambda b, t: (b, t, 0)),
        out_shape=jax.ShapeDtypeStruct((B, S, M_W), BF16),
        scratch_shapes=[pltpu.VMEM((nb * M_HEADS, M_HD, 2 * M_HD), F32),
                        pltpu.VMEM((nb * NC, L, LANES), F32),
                        pltpu.VMEM((nb * NC, SUBLANES, L), F32),
                        pltpu.VMEM((nb * NC, SUBLANES, 2 * M_HD), F32),
                        pltpu.VMEM((nb, SUBLANES, LANES), F32)],
        compiler_params=_cparams(2),
        name="mlstm",
    )(oa3, oa3, g4, u_tri, gain, sel)
    return out.reshape(N, M_W)


_G_LEVELS = 6
_G_XROW = 2 * G_CHUNK + SUBLANES


def _gla_consts():
    L = G_CHUNK
    t = np.arange(L)
    blocks = [(t[None, :] <= t[:, None]).astype(np.float32),
              (t[None, :] > t[:, None]).astype(np.float32),
              np.ones((SUBLANES, L), np.float32)]
    masks = [np.eye(L, dtype=np.float32)]
    m = 1
    while m < L:
        wl = np.zeros((L, L), np.float32)
        for r in range(L):
            r0 = (r // (2 * m)) * 2 * m + m
            if r % (2 * m) >= m:
                wl[r, r0:r + 1] = 1.0
            else:
                wl[r, r + 1:r0] = 1.0
        blocks.append(wl)
        tt, ss = t[:, None], t[None, :]
        masks.append(((tt // (2 * m) == ss // (2 * m)) & (tt % (2 * m) >= m)
                      & (ss % (2 * m) < m)).astype(np.float32))
        m *= 2
    w = np.concatenate(blocks, axis=0)
    w3 = np.concatenate([w, w, w], axis=1)
    mk = np.stack([np.concatenate([x] * G_HEADS, axis=0) for x in masks])
    return w3, mk


def _gla_kernel(qk_ref, v_ref, gg_ref, la_ref, w3_ref, mk_ref, gain_ref, out_ref, zrows_ref, st_ref, zb_ref, zsem,
                *, NC, nb):
    L = G_CHUNK
    step = pl.program_id(0) * pl.num_programs(1) + pl.program_id(1)
    zn = zb_ref.shape[0]

    def zero_copy(i):
        row = pl.multiple_of((step * ZERO_COPIES + i) * zn, GRAN)
        return pltpu.make_async_copy(zb_ref, zrows_ref.at[pl.ds(row, zn), :], zsem)

    @pl.when(step == 0)
    def _():
        zb_ref[...] = jnp.zeros_like(zb_ref)

    for i in range(ZERO_COPIES):
        zero_copy(i).start()

    @pl.when(pl.program_id(1) == 0)
    def _():
        st_ref[...] = jnp.zeros_like(st_ref)

    lane_head = lax.broadcasted_iota(jnp.int32, (L, G_KW), 1) // G_DK
    br = lax.broadcasted_iota(jnp.int32, (2 * G_DV, LANES), 0) < G_DV
    bl = lax.broadcasted_iota(jnp.int32, (2 * G_DV, LANES), 1) < G_DK
    bmask = br == bl
    nt = (((1,), (1,)), ((), ()))
    tn = (((0,), (0,)), ((), ()))

    def chunk(c, carry):
        rows = pl.ds(pl.multiple_of(c * L, L), L)
        X, q, k = [], [], []
        for bi in range(nb):
            la = la_ref[bi, rows, :]
            hi = la.astype(BF16)
            r1 = la - hi.astype(F32)
            mid = r1.astype(BF16)
            lo = (r1 - mid.astype(F32)).astype(BF16)
            stk = jnp.concatenate([hi, mid, lo], axis=0)
            X.append(jnp.exp(jnp.dot(w3_ref[...], stk, preferred_element_type=F32)))
            q.append(qk_ref[bi, rows, 0:G_KW].astype(F32))
            k.append(qk_ref[bi, rows, G_KW:2 * G_KW].astype(F32))

        sc = [[None] * (_G_LEVELS + 1) for _ in range(nb)]
        for lev in range(_G_LEVELS + 1):
            for bi in range(nb):
                if lev == 0:
                    qt, kt = q[bi], k[bi]
                else:
                    xl = X[bi][_G_XROW + L * (lev - 1):_G_XROW + L * lev, :]
                    qt, kt = q[bi] * xl, k[bi] * xl
                q4 = jnp.concatenate([jnp.where(lane_head == h, qt, 0.0) for h in range(G_HEADS)],
                                     axis=0).astype(BF16)
                sc[bi][lev] = lax.dot_general(q4, kt.astype(BF16), nt, preferred_element_type=F32)
        Ab = []
        for bi in range(nb):
            A = sc[bi][0] * mk_ref[0]
            for lev in range(1, _G_LEVELS + 1):
                A = A + sc[bi][lev] * mk_ref[lev]
            Ab.append(A.astype(BF16))

        for bi in range(nb):
            gg = gg_ref[bi, rows, :].astype(F32)
            gate = gg * _sigmoid(gg)
            for p in range(2):
                ls = slice(LANES * p, LANES * (p + 1))
                vp = v_ref[bi, rows, 2 * G_DV * p:2 * G_DV * (p + 1)]
                oi = [jnp.dot(Ab[bi][L * (2 * p + hh):L * (2 * p + hh + 1)],
                              vp[:, G_DV * hh:G_DV * (hh + 1)], preferred_element_type=F32)
                      for hh in range(2)]
                st = st_ref[bi, p]
                qc = (q[bi][:, ls] * X[bi][0:L, ls]).astype(BF16)
                o_inter = lax.dot_general(qc, st.astype(BF16), nt, preferred_element_type=F32)
                kc = (k[bi][:, ls] * X[bi][L:2 * L, ls]).astype(BF16)
                upd = lax.dot_general(vp, kc, tn, preferred_element_type=F32)
                dec = X[bi][2 * L:2 * L + 1, ls]
                st_ref[bi, p] = jnp.where(bmask, dec * st + upd, 0.0)
                for hh in range(2):
                    o = o_inter[:, G_DV * hh:G_DV * (hh + 1)] + oi[hh]
                    hn = o * lax.rsqrt(jnp.mean(o * o, axis=-1, keepdims=True) + LN_EPS)
                    hs = slice(G_DV * (2 * p + hh), G_DV * (2 * p + hh + 1))
                    out_ref[bi, rows, hs] = (hn * gain_ref[:, hs] * gate[:, hs]).astype(BF16)
        return carry

    lax.fori_loop(0, NC, chunk, 0, unroll=2 if NC % 2 == 0 else 1)
    for i in range(ZERO_COPIES):
        zero_copy(i).wait()


def _gla(oa, la, w3, mk, gain, *, B, S, nb, ts, zero_shape):
    N = oa.shape[0]
    oa3 = oa.reshape(B, S, oa.shape[1])
    la3 = la.reshape(B, S, G_KW)
    kern = functools.partial(_gla_kernel, NC=ts // G_CHUNK, nb=nb)
    n_copies = (B // nb) * (S // ts) * ZERO_COPIES
    assert zero_shape[0] % (n_copies * GRAN) == 0
    out, zrows = pl.pallas_call(
        kern,
        grid=(B // nb, S // ts),
        in_specs=[pl.BlockSpec((nb, ts, 2 * G_KW), lambda b, t: (b, t, C_GQK // (2 * G_KW))),
                  pl.BlockSpec((nb, ts, G_W), lambda b, t: (b, t, C_GV // G_W)),
                  pl.BlockSpec((nb, ts, G_W), lambda b, t: (b, t, C_GG // G_W)),
                  pl.BlockSpec((nb, ts, G_KW), lambda b, t: (b, t, 0)),
                  pl.BlockSpec(w3.shape, lambda b, t: (0, 0)),
                  pl.BlockSpec(mk.shape, lambda b, t: (0, 0, 0)),
                  pl.BlockSpec((1, G_W), lambda b, t: (0, 0))],
        out_specs=[pl.BlockSpec((nb, ts, G_W), lambda b, t: (b, t, 0)), pl.BlockSpec(memory_space=pl.ANY)],
        out_shape=[jax.ShapeDtypeStruct((B, S, G_W), BF16), jax.ShapeDtypeStruct(zero_shape, U32)],
        scratch_shapes=[pltpu.VMEM((nb, 2, 2 * G_DV, LANES), F32),
                        pltpu.VMEM((zero_shape[0] // n_copies, zero_shape[1]), U32),
                        pltpu.SemaphoreType.DMA(())],
        compiler_params=_cparams(2),
        name="gla",
    )(oa3, oa3, oa3, la3, w3, mk, gain)
    return out.reshape(N, G_W), zrows


def _layer_norm(z, g, b):
    mu = jnp.mean(z, axis=-1, keepdims=True)
    zc = z - mu
    var = jnp.mean(zc * zc, axis=-1, keepdims=True)
    return zc * lax.rsqrt(var + LN_EPS) * g + b


def _outproj_kernel(hm_ref, hg_ref, wf_ref, x_ref, mod_ref, g_ref, b_ref, wrg_ref, wre_ref, br_ref,
                    x1_ref, u2_ref, rrow_ref, w_ref, wr_ref, *, tb, nh):
    @pl.when(pl.program_id(0) == 0)
    def _():
        w_ref[...] = wf_ref[...].astype(BF16)
        z = lambda n: jnp.zeros((n, wrg_ref.shape[2]), F32)
        wt = jnp.concatenate([wrg_ref[0], z(SUBLANES - N_GROUPS), wre_ref[0],
                              z(LANES - SUBLANES - N_EXP)], axis=0).T
        hi = wt.astype(BF16)
        wr_ref[:, 0:LANES] = hi
        wr_ref[:, LANES:2 * LANES] = (wt - hi.astype(F32)).astype(BF16)

    mod = mod_ref[0]
    blocks = [slice(tb * j, tb * (j + 1)) for j in range(nh)]
    y = [jnp.dot(hm_ref[r, :], w_ref[0:M_W, :], preferred_element_type=F32)
         + jnp.dot(hg_ref[r, :], w_ref[M_W:M_W + G_W, :], preferred_element_type=F32) for r in blocks]
    u2 = []
    for j, r in enumerate(blocks):
        z = ALPHA * x_ref[r, :] + (1.0 + mod[2:3, :]) * y[j]
        x1 = _layer_norm(z, g_ref[...], b_ref[...])
        x1_ref[r, :] = x1
        u2.append(x1 * (1.0 + mod[4:5, :]) + mod[3:4, :])
        u2_ref[r, :] = u2[j].astype(BF16)

    u2h = [u.astype(BF16) for u in u2]
    u2l = [(u2[j] - u2h[j].astype(F32)).astype(BF16) for j in range(nh)]
    lh = [jnp.dot(u, wr_ref[...], preferred_element_type=F32) for u in u2h]
    ll = [jnp.dot(u, wr_ref[:, 0:LANES], preferred_element_type=F32) for u in u2l]
    for j in range(nh):
        logits = lh[j][:, 0:LANES] + lh[j][:, LANES:2 * LANES] + ll[j] + br_ref[...]
        rrow_ref[j] = _route_select(logits.T, tb)


def _route_select(lt, tm):
    row = lax.broadcasted_iota(jnp.int32, (SUBLANES, tm), 0)
    gl = jnp.where(row < N_GROUPS, lt[0:SUBLANES, :], -jnp.inf)
    gmax = jnp.max(gl, axis=0, keepdims=True)
    gsel = jnp.min(jnp.where(gl == gmax, row, SUBLANES), axis=0, keepdims=True)
    pg = 1.0 / jnp.sum(jnp.exp(gl - gmax), axis=0, keepdims=True)
    ein = jnp.zeros((SUBLANES, tm), F32)
    for g in range(N_GROUPS):
        ein = jnp.where(gsel == g, lt[SUBLANES * (g + 1):SUBLANES * (g + 2), :], ein)
    v1 = jnp.max(ein, axis=0, keepdims=True)
    i1 = jnp.min(jnp.where(ein == v1, row, SUBLANES), axis=0, keepdims=True)
    rest = jnp.where(row == i1, -jnp.inf, ein)
    v2 = jnp.max(rest, axis=0, keepdims=True)
    i2 = jnp.min(jnp.where(rest == v2, row, SUBLANES), axis=0, keepdims=True)
    t2 = jnp.exp(v2 - v1)
    p1 = 1.0 / (1.0 + t2)
    e0 = (gsel * E_PER_G + i1).astype(F32)
    e1 = (gsel * E_PER_G + i2).astype(F32)
    return jnp.concatenate([e0, e1, pg * p1, pg * (t2 * p1), jnp.zeros((SUBLANES - 4, tm), F32)], axis=0)


def _outproj(hm, hg, w_out, x2, mod3, g, b, wrg_t, wre_t, br, *, S, tb, nh, layer):
    N, D = x2.shape
    tm = tb * nh
    tpb = S // tm
    kern = functools.partial(_outproj_kernel, tb=tb, nh=nh)
    return pl.pallas_call(
        kern,
        grid=(N // tm,),
        in_specs=[pl.BlockSpec((tm, M_W), lambda i: (i, 0)),
                  pl.BlockSpec((tm, G_W), lambda i: (i, 0)),
                  pl.BlockSpec((M_W + G_W, D), lambda i: (0, 0), pipeline_mode=pl.Buffered(1)),
                  pl.BlockSpec((tm, D), lambda i: (i, 0)),
                  pl.BlockSpec((1, 6, D), lambda i: (i // tpb, 0, 0)),
                  pl.BlockSpec((1, D), lambda i: (0, 0)),
                  pl.BlockSpec((1, D), lambda i: (0, 0)),
                  pl.BlockSpec((1, N_GROUPS, D), lambda i: (layer, 0, 0)),
                  pl.BlockSpec((1, N_EXP, D), lambda i: (layer, 0, 0)),
                  pl.BlockSpec((1, LANES), lambda i: (0, 0))],
        out_specs=[pl.BlockSpec((tm, D), lambda i: (i, 0)),
                   pl.BlockSpec((tm, D), lambda i: (i, 0)),
                   pl.BlockSpec((nh, SUBLANES, tb), lambda i: (i, 0, 0))],
        out_shape=[jax.ShapeDtypeStruct((N, D), F32),
                   jax.ShapeDtypeStruct((N, D), BF16),
                   jax.ShapeDtypeStruct((N // tb, SUBLANES, tb), F32)],
        scratch_shapes=[pltpu.VMEM((M_W + G_W, D), BF16), pltpu.VMEM((D, 2 * LANES), BF16)],
        compiler_params=_cparams(),
        name="outproj",
    )(hm, hg, w_out, x2, mod3, g, b, wrg_t, wre_t, br)


def _slots_per_tile(tb):
    worst = 2 * tb + N_EXP * (GRAN - 1)
    return -(-worst // LANES) * LANES


def _ffn_tiles(n_tok, tb):
    worst_rows = 2 * n_tok + (n_tok // tb) * N_EXP * (GRAN - 1)
    return -(-worst_rows // FFN_TM) + N_EXP


def _route_kernel(rr_ref, u_ref, lt_ref, srow_ref, col_ref, gd_ref, meta_ref, mg_ref, part_ref,
                  *, NT, tb, TM):
    iota_e = lax.broadcasted_iota(jnp.int32, (N_EXP, tb), 0).astype(F32)
    glane = lax.broadcasted_iota(jnp.int32, (N_EXP, LANES), 1).astype(F32)
    ltri = lt_ref[...]

    def prefix_e(col):
        return jnp.dot(ltri, jnp.broadcast_to(col, (N_EXP, LANES)),
                       preferred_element_type=F32, precision=HIGHEST)[:, 0:1]

    def p1(j, run8):
        r = rr_ref[j]
        oh0 = jnp.where(iota_e == r[0:1, :], 1.0, 0.0)
        oh1 = jnp.where(iota_e == r[1:2, :], 1.0, 0.0)
        cum0 = jnp.dot(oh0.astype(BF16), u_ref[...], preferred_element_type=F32)
        cum1 = jnp.dot(oh1.astype(BF16), u_ref[...], preferred_element_type=F32)
        c0 = jnp.sum(oh0, axis=1, keepdims=True)
        n8 = jnp.floor((c0 + jnp.sum(oh1, axis=1, keepdims=True) + (GRAN - 1.0)) * (1.0 / GRAN))
        lo8 = prefix_e(n8)
        s0 = jnp.sum(oh0 * (GRAN * lo8 + cum0 - 1.0), axis=0, keepdims=True)
        s1 = jnp.sum(oh1 * (GRAN * lo8 + c0 + cum1 - 1.0), axis=0, keepdims=True)
        info = jnp.concatenate([s0, s1, r[2:4, :], jnp.zeros((SUBLANES - 4, tb), F32)], axis=0)
        srow_ref[j] = info
        col_ref[pl.ds(pl.multiple_of(j * tb, tb), tb), :] = jnp.concatenate(
            [info, jnp.zeros((LANES - SUBLANES, tb), F32)], axis=0).T
        mg = jnp.where((lo8 <= glane) & (glane < lo8 + n8), 1.0, 0.0)
        mg_ref[j] = mg
        part = jnp.sum(mg * (run8 + glane - lo8), axis=0, keepdims=True)
        gcnt = jnp.broadcast_to(jnp.sum(n8, axis=0, keepdims=True), (1, LANES))
        part_ref[j] = jnp.concatenate([part, gcnt, jnp.zeros((SUBLANES - 2, LANES), F32)], axis=0)
        return run8 + n8

    tot8 = lax.fori_loop(0, NT, p1, jnp.zeros((N_EXP, 1), F32), unroll=8 if NT % 8 == 0 else 1)
    seg_t = jnp.floor((tot8 * GRAN + (TM - 1.0)) * (1.0 / TM))
    base_t = prefix_e(seg_t)
    base8 = base_t * (TM // GRAN)
    lane1 = lax.broadcasted_iota(jnp.int32, (1, LANES), 1)

    def p2(j, carry):
        pr = part_ref[j]
        dst = (pr[0:1, :] + jnp.sum(mg_ref[j] * base8, axis=0, keepdims=True)) * GRAN
        gd_ref[j] = jnp.where(lane1 == G_LAST, pr[1:2, :], dst).astype(jnp.int32)
        return carry

    lax.fori_loop(0, NT, p2, 0, unroll=8 if NT % 8 == 0 else 1)
    eye = jnp.where(glane == lax.broadcasted_iota(jnp.int32, (N_EXP, LANES), 0).astype(F32), 1.0, 0.0)
    tail_row = jnp.sum(eye * ((base8 + tot8) * GRAN), axis=0, keepdims=True)
    tail_n8 = jnp.sum(eye * (seg_t * (TM // GRAN) - tot8), axis=0, keepdims=True)
    nv_l = jnp.broadcast_to(jnp.sum(seg_t, axis=0, keepdims=True), (1, LANES))
    gd_ref[NT] = jnp.where(lane1 == G_LAST, nv_l, tail_row).astype(jnp.int32)
    gd_ref[NT + 1] = tail_n8.astype(jnp.int32)
    ti = lax.broadcasted_iota(jnp.int32, (N_EXP, tb), 1).astype(F32)
    te = jnp.sum(jnp.where(base_t <= ti, 1.0, 0.0), axis=0, keepdims=True) - 1.0
    nv = jnp.broadcast_to(jnp.sum(seg_t, axis=0, keepdims=True), (1, tb))
    own = jnp.where((base_t <= ti) & (ti < base_t + seg_t), 1.0, 0.0)
    vr = jnp.sum(own * jnp.clip(tot8 * GRAN - (ti - base_t) * TM, 0.0, TM), axis=0, keepdims=True)
    meta_ref[...] = jnp.concatenate([te, nv, vr, jnp.zeros((SUBLANES - 3, tb), F32)],
                                    axis=0).astype(jnp.int32)


def _route(rrow, u_cnt, ltri, *, TM):
    NT, _, tb = rrow.shape
    kern = functools.partial(_route_kernel, NT=NT, tb=tb, TM=TM)
    full3 = lambda i: (0, 0, 0)
    return pl.pallas_call(
        kern,
        grid=(1,),
        in_specs=[pl.BlockSpec((NT, SUBLANES, tb), full3),
                  pl.BlockSpec((tb, tb), lambda i: (0, 0)),
                  pl.BlockSpec((N_EXP, N_EXP), lambda i: (0, 0))],
        out_specs=[pl.BlockSpec((NT, SUBLANES, tb), full3),
                   pl.BlockSpec((NT * tb, LANES), lambda i: (0, 0)),
                   pl.BlockSpec((NT + 2, 1, LANES), full3),
                   pl.BlockSpec((SUBLANES, tb), lambda i: (0, 0))],
        out_shape=[jax.ShapeDtypeStruct((NT, SUBLANES, tb), F32),
                   jax.ShapeDtypeStruct((NT * tb, LANES), F32),
                   jax.ShapeDtypeStruct((NT + 2, 1, LANES), jnp.int32),
                   jax.ShapeDtypeStruct((SUBLANES, tb), jnp.int32)],
        scratch_shapes=[pltpu.VMEM((NT, N_EXP, LANES), F32), pltpu.VMEM((NT, SUBLANES, LANES), F32)],
        compiler_params=_cparams(),
        name="route",
    )(rrow, u_cnt, ltri)


_HI_MASK = 0xFFFF0000


def _pack_halves(x):
    c = x.shape[1] // 2
    lo = lax.bitcast_convert_type(x[:, :c], U32)
    hi = lax.bitcast_convert_type(x[:, c:], U32)
    return (lo >> 16) | (hi & U32(_HI_MASK))


def _unpack_halves(w):
    lo = lax.bitcast_convert_type(w << 16, F32)
    hi = lax.bitcast_convert_type(w & U32(_HI_MASK), F32)
    return jnp.concatenate([lo, hi], axis=1).astype(BF16)


def _granule_copy(src_ref, src_row, dst_ref, dst_row, sem, n=1):
    cols = pl.ds(0, min(src_ref.shape[-1], dst_ref.shape[-1]))
    return pltpu.make_async_copy(src_ref.at[pl.ds(src_row, n * GRAN), cols],
                                 dst_ref.at[pl.ds(dst_row, n * GRAN), cols], sem)


def _for_granules(n, body, unroll=8):
    def blk(i, carry):
        for t in range(unroll):
            body(i * unroll + t)
        return carry

    def one(g, carry):
        body(g)
        return carry

    nblk = n // unroll
    lax.fori_loop(0, nblk, blk, 0)
    lax.fori_loop(nblk * unroll, n, one, 0)


def _wait_granules(n, src_ref, dst_ref, sem, n_max):
    b = 1
    while b <= n_max:
        @pl.when((n & b) != 0)
        def _(b=b):
            _granule_copy(src_ref, 0, dst_ref, 0, sem, n=b).wait()
        b *= 2


def _dispatch_kernel(gd_ref, srow_ref, u_ref, xs0_ref, xs_ref, buf, sems, *, NT, SL):
    del xs0_ref
    j = pl.program_id(0)
    slot = j % 2

    def drain(tile, sl):
        _wait_granules(gd_ref[tile, G_LAST], buf.at[sl], xs_ref, sems.at[sl], SL // GRAN)

    @pl.when(j >= 2)
    def _():
        drain(j - 2, slot)

    s = srow_ref[0]
    rows = lax.broadcasted_iota(jnp.int32, (SL, s.shape[1]), 0).astype(F32)
    m0 = rows == s[0:1, :]
    m1 = rows == s[1:2, :]
    oh = jnp.where(m0 | m1, 1.0, 0.0).astype(BF16)
    dw = u_ref.shape[1] // 2
    buf[slot, :, 0:dw] = _pack_halves(jnp.dot(oh, u_ref[...], preferred_element_type=F32))
    wrow = jnp.sum(jnp.where(m0, s[2:3, :], 0.0) + jnp.where(m1, s[3:4, :], 0.0), axis=1, keepdims=True)
    buf[slot, :, dw:dw + LANES] = lax.bitcast_convert_type(jnp.broadcast_to(wrow, (SL, LANES)), U32)

    def issue(g):
        _granule_copy(buf.at[slot], pl.multiple_of(g * GRAN, GRAN), xs_ref,
                      pl.multiple_of(gd_ref[j, g], GRAN), sems.at[slot]).start()

    _for_granules(gd_ref[j, G_LAST], issue)

    @pl.when(j == NT - 1)
    def _():
        drain(j, slot)
        if NT > 1:
            drain(j - 1, 1 - slot)


def _dispatch(gd, srow, u2, xs0):
    N, D = u2.shape
    NT, _, tb = srow.shape
    SL = _slots_per_tile(tb)
    assert xs0.shape[1] == D // 2 + LANES and xs0.dtype == U32
    kern = functools.partial(_dispatch_kernel, NT=NT, SL=SL)
    grid_spec = pltpu.PrefetchScalarGridSpec(
        num_scalar_prefetch=1,
        grid=(NT,),
        in_specs=[pl.BlockSpec((1, SUBLANES, tb), lambda j, gd: (j, 0, 0)),
                  pl.BlockSpec((tb, D), lambda j, gd: (j, 0)),
                  pl.BlockSpec(memory_space=pl.ANY)],
        out_specs=pl.BlockSpec(memory_space=pl.ANY),
        scratch_shapes=[pltpu.VMEM((2, SL, D // 2 + LANES), U32), pltpu.SemaphoreType.DMA((2,))],
    )
    return pl.pallas_call(
        kern,
        grid_spec=grid_spec,
        out_shape=jax.ShapeDtypeStruct(xs0.shape, U32),
        input_output_aliases={3: 0},
        compiler_params=_cparams(),
        name="dispatch",
    )(gd, srow, u2, xs0)


def _ffn_kernel(meta_ref, xs_ref, wg_ref, wu_ref, wd_ref, o_ref, wgb, wub, wdb, sg, su, sd, slot_ref, sems):
    i = pl.program_id(0)
    nv = meta_ref[TILES_USED, 0]
    tile_expert = lambda t: meta_ref[TILE_EXPERT, t]
    e = tile_expert(i)

    def weight_copies(ex, sl):
        return (pltpu.make_async_copy(wg_ref.at[ex], sg.at[sl], sems.at[sl]),
                pltpu.make_async_copy(wu_ref.at[ex], su.at[sl], sems.at[sl]),
                pltpu.make_async_copy(wd_ref.at[ex], sd.at[sl], sems.at[sl]))

    @pl.when(i == 0)
    def _():
        slot_ref[0] = 0
        for cp in weight_copies(e, 0):
            cp.start()

    new_expert = (i < nv) & ((i == 0) | (e != tile_expert(jnp.maximum(i - 1, 0))))

    @pl.when(new_expert)
    def _():
        sl = slot_ref[0]
        for cp in weight_copies(e, sl):
            cp.wait()
        nxt = lax.while_loop(lambda t: (t < nv) & (tile_expert(jnp.minimum(t, nv - 1)) == e), lambda t: t + 1, i + 1)

        @pl.when(nxt < nv)
        def _():
            for cp in weight_copies(tile_expert(nxt), 1 - sl):
                cp.start()

    hm = xs_ref.shape[0] // FFN_SUB
    dw = xs_ref.shape[1] - LANES

    def swiglu_rows(nsub, cast):
        if cast:
            sl = slot_ref[0]
            wgb[...] = sg[sl].astype(BF16)
            wub[...] = su[sl].astype(BF16)
            wdb[...] = sd[sl].astype(BF16)
            slot_ref[0] = 1 - sl
        halves = tuple(slice(hm * j, hm * (j + 1)) for j in range(nsub))
        x = [_unpack_halves(xs_ref[r, 0:dw]) for r in halves]
        g = [jnp.dot(x[j], wgb[...], preferred_element_type=F32) for j in range(nsub)]
        u = [jnp.dot(x[j], wub[...], preferred_element_type=F32) for j in range(nsub)]
        h = [(g[j] * _sigmoid(g[j]) * u[j]).astype(BF16) for j in range(nsub)]
        y = [jnp.dot(h[j], wdb[...], preferred_element_type=F32) for j in range(nsub)]
        for j in range(nsub):
            wt = lax.bitcast_convert_type(xs_ref[halves[j], dw:dw + LANES], F32)
            yw = y[j] * jnp.concatenate([wt] * (2 * dw // LANES), axis=1)
            o_ref[halves[j], 0:dw] = _pack_halves(yw.astype(BF16).astype(F32))
            o_ref[halves[j], dw:dw + LANES] = xs_ref[halves[j], dw:dw + LANES]
        if nsub < FFN_SUB:
            o_ref[hm * nsub:, :] = jnp.zeros((hm * (FFN_SUB - nsub), dw + LANES), U32)

    used = meta_ref[TILE_ROWS, i]
    for nsub in range(1, FFN_SUB + 1):
        lo, hi = hm * (nsub - 1), hm * nsub
        rows_here = (i < nv) & (used > lo) & ((used <= hi) if nsub < FFN_SUB else True)
        pl.when(rows_here & new_expert)(functools.partial(swiglu_rows, nsub, True))
        pl.when(rows_here & jnp.logical_not(new_expert))(functools.partial(swiglu_rows, nsub, False))


def _ffn(meta, xs, wg, wu, wd, *, TM):
    P, XW = xs.shape
    DW = XW - LANES
    D = 2 * DW
    n_tiles = P // TM
    assert n_tiles <= meta.shape[1]
    used_tile = lambda i, meta: (jnp.maximum(jnp.minimum(i, meta[TILES_USED, 0] - 1), 0), 0)
    grid_spec = pltpu.PrefetchScalarGridSpec(
        num_scalar_prefetch=1,
        grid=(n_tiles,),
        in_specs=[pl.BlockSpec((TM, XW), used_tile),
                  pl.BlockSpec(memory_space=pl.ANY),
                  pl.BlockSpec(memory_space=pl.ANY),
                  pl.BlockSpec(memory_space=pl.ANY)],
        out_specs=pl.BlockSpec((TM, XW), used_tile),
        scratch_shapes=[pltpu.VMEM((D, D_EXP), BF16), pltpu.VMEM((D, D_EXP), BF16),
                        pltpu.VMEM((D_EXP, D), BF16),
                        pltpu.VMEM((2, D, D_EXP), F32), pltpu.VMEM((2, D, D_EXP), F32),
                        pltpu.VMEM((2, D_EXP, D), F32), pltpu.SMEM((1,), jnp.int32),
                        pltpu.SemaphoreType.DMA((2,))],
    )
    return pl.pallas_call(
        _ffn_kernel,
        grid_spec=grid_spec,
        out_shape=jax.ShapeDtypeStruct((P, XW), U32),
        input_output_aliases={1: 0},
        compiler_params=_cparams(),
        name="ffn",
    )(meta, xs, wg, wu, wd)


def _combine_kernel(gd_ref, ys_ref, col_ref, x1_ref, mod_ref, g_ref, b_ref, o_ref, buf, sems, *, NT, SL):
    j = pl.program_id(0)
    slot = j % 2

    def fetch(tile, sl):
        def f(g):
            _granule_copy(ys_ref, pl.multiple_of(gd_ref[tile, g], GRAN), buf.at[sl],
                          pl.multiple_of(g * GRAN, GRAN), sems.at[sl]).start()
        _for_granules(gd_ref[tile, G_LAST], f)

    @pl.when(j == 0)
    def _():
        fetch(0, 0)

    @pl.when(j + 1 < NT)
    def _():
        fetch(j + 1, 1 - slot)

    ng = gd_ref[j, G_LAST]

    _wait_granules(ng, ys_ref, buf.at[slot], sems.at[slot], SL // GRAN)

    rows = lax.broadcasted_iota(jnp.int32, (SL, 1), 0)
    yb = _unpack_halves(jnp.where(rows < ng * GRAN, buf[slot], U32(0)))
    col = col_ref[...]
    tb = col.shape[0]
    lanes = lax.broadcasted_iota(jnp.int32, (tb, SL), 1).astype(F32)
    sel = jnp.where((lanes == col[:, 0:1]) | (lanes == col[:, 1:2]), 1.0, 0.0).astype(BF16)
    y = jnp.dot(sel, yb, preferred_element_type=F32)
    mod = mod_ref[0]
    z = ALPHA * x1_ref[...] + (1.0 + mod[5:6, :]) * y
    o_ref[...] = _layer_norm(z, g_ref[...], b_ref[...])


def _combine(gd, ys, col, x1, mod3, g, b, *, S, tb):
    N, D = x1.shape
    NT = N // tb
    tpb = S // tb
    SL = _slots_per_tile(tb)
    kern = functools.partial(_combine_kernel, NT=NT, SL=SL)
    grid_spec = pltpu.PrefetchScalarGridSpec(
        num_scalar_prefetch=1,
        grid=(NT,),
        in_specs=[pl.BlockSpec(memory_space=pl.ANY),
                  pl.BlockSpec((tb, LANES), lambda j, gd: (j, 0)),
                  pl.BlockSpec((tb, D), lambda j, gd: (j, 0)),
                  pl.BlockSpec((1, 6, D), lambda j, gd: (j // tpb, 0, 0)),
                  pl.BlockSpec((1, D), lambda j, gd: (0, 0)),
                  pl.BlockSpec((1, D), lambda j, gd: (0, 0))],
        out_specs=pl.BlockSpec((tb, D), lambda j, gd: (j, 0)),
        scratch_shapes=[pltpu.VMEM((2, SL, D // 2), U32), pltpu.SemaphoreType.DMA((2,))],
    )
    return pl.pallas_call(
        kern,
        grid_spec=grid_spec,
        out_shape=jax.ShapeDtypeStruct((N, D), F32),
        compiler_params=_cparams(),
        name="combine",
    )(gd, ys, col, x1, mod3, g, b)


def _layer(x, c, l, w_ada, b_ada, w_in, w_conv, b_conv, b_igate, b_fgate, mlstm_norm_g, w_gla_a, b_gla_a,
           gla_norm_g, w_out, ln1_g, ln1_b, w_route_group, b_route_group, w_route_expert, b_route_expert,
           w_gate, w_up, w_down, ln2_g, ln2_b):
    B, S, D = x.shape
    N = B * S
    x2 = x.reshape(N, D)
    tm_in = min(512, S)
    tm = min(256, S)
    lm = min(256, S)
    assert S % tm_in == 0 and S % tm == 0 and tm_in % lm == 0 and S % G_CHUNK == 0
    assert w_in.shape[1:] == (D, IN_TOT) and w_gate.shape[1:] == (N_EXP, D, D_EXP)

    mod3 = _ada(c, w_ada[l], b_ada[l]).reshape(B, 6, D)

    wa_pad = jnp.pad(w_gla_a[l], ((SM_A, LANES - SM_A - G_RANK), (0, 0))).astype(BF16)
    bg = jnp.concatenate([jnp.pad(b_igate[l], (0, SUBLANES - M_HEADS)),
                          jnp.pad(b_fgate[l], (0, SUBLANES - M_HEADS))]).reshape(2 * SUBLANES, 1)
    oa, la, g3 = _inproj(x2, mod3, jnp.swapaxes(w_in, 1, 2), w_conv[l], b_conv[l].reshape(1, -1), wa_pad,
                         b_gla_a[l].reshape(1, -1), bg, S=S, tm=tm_in, lm=lm, layer=l)

    u_tri = jnp.asarray(np.triu(np.ones((lm, lm), np.float32)))
    nb = 4 if B % 4 == 0 else (2 if B % 2 == 0 else 1)
    ts = min(512, S)
    hm = _mlstm(oa, g3, u_tri, mlstm_norm_g[l].reshape(1, -1), B=B, S=S, L=lm, nb=nb, ts=ts)
    w3_np, mk_np = _gla_consts()
    n_tiles = _ffn_tiles(N, tm)
    hg, xs0 = _gla(oa, la, jnp.asarray(w3_np, BF16), jnp.asarray(mk_np), gla_norm_g[l].reshape(1, -1), B=B, S=S,
                   nb=nb, ts=ts, zero_shape=(n_tiles * FFN_TM, D // 2 + LANES))

    br = jnp.concatenate([jnp.pad(b_route_group[l], (0, SUBLANES - N_GROUPS)),
                          jnp.pad(b_route_expert[l], (0, LANES - SUBLANES - N_EXP))]).reshape(1, LANES)
    x1, u2, rrow = _outproj(hm, hg, w_out[l], x2, mod3, ln1_g[l].reshape(1, -1), ln1_b[l].reshape(1, -1),
                            jnp.swapaxes(w_route_group, 1, 2), jnp.swapaxes(w_route_expert, 1, 2), br,
                            S=S, tb=tm, nh=4 if S % (4 * tm) == 0 else 1, layer=l)

    u_cnt = jnp.asarray(np.triu(np.ones((tm, tm), np.float32)), BF16)
    ltri = jnp.asarray(np.tril(np.ones((N_EXP, N_EXP), np.float32), -1))
    srow, col, gd3, meta = _route(rrow, u_cnt, ltri, TM=FFN_TM)
    gd = gd3.reshape(N // tm + 2, LANES)
    xs = _dispatch(gd, srow, u2, xs0)
    ys = _ffn(meta, xs, w_gate[l], w_up[l], w_down[l], TM=FFN_TM)
    out = _combine(gd, ys, col, x1, mod3, ln2_g[l].reshape(1, -1), ln2_b[l].reshape(1, -1), S=S, tb=tm)
    return out.reshape(B, S, D)


def kernel(x, c, w_ada, b_ada, w_in, w_conv, b_conv, b_igate, b_fgate, mlstm_norm_g, w_gla_a, b_gla_a,
           gla_norm_g, w_out, ln1_g, ln1_b, w_route_group, b_route_group, w_route_expert, b_route_expert,
           w_gate, w_up, w_down, ln2_g, ln2_b):
    for l in range(DEPTH):
        x = _layer(x, c, l, w_ada, b_ada, w_in, w_conv, b_conv, b_igate, b_fgate, mlstm_norm_g, w_gla_a,
                   b_gla_a, gla_norm_g, w_out, ln1_g, ln1_b, w_route_group, b_route_group, w_route_expert,
                   b_route_expert, w_gate, w_up, w_down, ln2_g, ln2_b)
    return x
```
